```python
import functools
import jax
import jax.numpy as jnp
from jax import lax
import numpy as np

D_MODEL = 2048
BATCH = 4
SEQ = 2048
DEPTH = 2

CTX_LEN = 256
GRID_W = 64
CHUNK = 128
NORM_EPS = 1e-6
ROPE_BASE = 10000.0

RET_HEADS = 4
RET_DK = 128
RET_DV = 256
HGRN_HEADS = 4
HGRN_DK = 128
HGRN_DV = 128
MLSTM_HEADS = 4
MLSTM_DK = 128
MLSTM_DV = 128
MLSTM_CONV = 3

RET_QK = RET_HEADS * RET_DK
RET_WIDTH = RET_HEADS * RET_DV
HGRN_QK = HGRN_HEADS * HGRN_DK
HGRN_WIDTH = HGRN_HEADS * HGRN_DV
MLSTM_QK = MLSTM_HEADS * MLSTM_DK
MLSTM_WIDTH = MLSTM_HEADS * MLSTM_DV
MIX_WIDTH = RET_WIDTH + HGRN_WIDTH + MLSTM_WIDTH
MLSTM_GATES = 4 * MLSTM_HEADS

PROJ_SPLITS = (
    RET_QK, RET_QK, RET_WIDTH, RET_WIDTH,
    HGRN_QK, HGRN_QK, HGRN_QK, HGRN_WIDTH, HGRN_WIDTH,
    MLSTM_QK, MLSTM_QK, MLSTM_WIDTH, MLSTM_WIDTH, MLSTM_GATES,
)
PROJ_COLS = sum(PROJ_SPLITS)

D_FF = 5632
N_EXPERTS = 8
TOP_K = 2
D_FF_EXPERT = 5632
MOE_BLOCK = 128
N_DENSE = (DEPTH + 1) // 2
N_MOE = DEPTH // 2

kernel_name = "hybrid_ret_hgrn2_mlstm_moe_dit_block"


def rmsnorm(x, g):
    xf = x.astype(jnp.float32)
    y = xf * lax.rsqrt(jnp.mean(xf * xf, axis=-1, keepdims=True) + NORM_EPS)
    return (y * g.astype(jnp.float32)).astype(x.dtype)


def head_norm(y, g, n_heads, center):
    B, L, W = y.shape
    yh = y.astype(jnp.float32).reshape(B, L, n_heads, W // n_heads)
    if center:
        yh = yh - jnp.mean(yh, axis=-1, keepdims=True)
    yh = yh * lax.rsqrt(jnp.mean(yh * yh, axis=-1, keepdims=True) + NORM_EPS)
    return yh.reshape(B, L, W) * g.astype(jnp.float32)


def split_heads(t, n_heads):
    B, L, W = t.shape
    return t.reshape(B, L, n_heads, W // n_heads).transpose(0, 2, 1, 3).astype(jnp.float32)


def merge_heads(t):
    B, H, L, d = t.shape
    return t.transpose(0, 2, 1, 3).reshape(B, L, H * d)


def to_chunks(t):
    B, H, L = t.shape[:3]
    t = t.reshape((B, H, L // CHUNK, CHUNK) + t.shape[3:])
    return jnp.moveaxis(t, 2, 0)


def from_chunks(t):
    t = jnp.moveaxis(t, 0, 2)
    return t.reshape((t.shape[0], t.shape[1], -1) + t.shape[4:])


def flip_seq(t):
    return jnp.flip(t, axis=2)


def axial_rotary_tables(n_tokens):
    rows = n_tokens // GRID_W
    row = jnp.repeat(jnp.arange(rows, dtype=jnp.float32), GRID_W)
    col = jnp.tile(jnp.arange(GRID_W, dtype=jnp.float32), rows)
    n_freq = RET_DK // 4
    inv = ROPE_BASE ** (-jnp.arange(n_freq, dtype=jnp.float32) / n_freq)
    ang = jnp.concatenate([row[:, None] * inv, col[:, None] * inv], axis=-1)
    return jnp.cos(ang), jnp.sin(ang)


def apply_rotary(t, cos, sin):
    t1, t2 = jnp.split(t, 2, axis=-1)
    return jnp.concatenate([t1 * cos - t2 * sin, t2 * cos + t1 * sin], axis=-1)


def dwconv_centred(x, w):
    return lax.conv_general_dilated(
        x, w.astype(x.dtype)[:, None, :], window_strides=(1,), padding="SAME",
        dimension_numbers=("NWC", "WIO", "NWC"), feature_group_count=x.shape[-1])


def retention_scan(args, state, log_g):
    q, k, v = args
    pos = jnp.arange(CHUNK, dtype=jnp.float32)
    rel = pos[:, None] - pos[None, :]
    lg = log_g[:, None, None]
    dmat = jnp.where(rel >= 0, jnp.exp(jnp.maximum(rel, 0.0)[None] * lg), 0.0)
    dec_q = jnp.exp((pos + 1.0)[None, :] * log_g[:, None])[..., None]
    dec_k = jnp.exp((CHUNK - 1.0 - pos)[None, :] * log_g[:, None])[..., None]
    dec_s = jnp.exp(CHUNK * log_g)[:, None, None]

    def step(S, xs):
        qc, kc, vc = xs
        a = jnp.einsum("bhtd,bhsd->bhts", qc, kc) * dmat
        o = jnp.einsum("bhts,bhse->bhte", a, vc) + jnp.einsum("bhtd,bhde->bhte", qc * dec_q, S)
        S = dec_s * S + jnp.einsum("bhsd,bhse->bhde", kc * dec_k, vc)
        return S, o

    S, o = lax.scan(step, state, [to_chunks(q), to_chunks(k), to_chunks(v)])
    return from_chunks(o), S


def gla_scan(args, state):
    q, k, v, lf = args
    tril = jnp.tril(jnp.ones((CHUNK, CHUNK), dtype=bool))

    def step(S, xs):
        qc, kc, vc, lfc = xs
        b = jnp.cumsum(lfc, axis=-2)
        diff = jnp.where(tril[:, :, None], b[..., :, None, :] - b[..., None, :, :], -jnp.inf)
        a = jnp.sum(qc[..., :, None, :] * kc[..., None, :, :] * jnp.exp(diff), axis=-1)
        o = jnp.einsum("bhts,bhse->bhte", a, vc) + jnp.einsum("bhtd,bhde->bhte", qc * jnp.exp(b), S)
        b_last = b[..., -1:, :]
        S = jnp.exp(b_last[..., 0, :])[..., None] * S + jnp.einsum("bhsd,bhse->bhde", kc * jnp.exp(b_last - b), vc)
        return S, o

    S, o = lax.scan(step, state, [to_chunks(q), to_chunks(k), to_chunks(v), to_chunks(lf)])
    return from_chunks(o), S


def mlstm_scan(args, state):
    q, k, v, ig, lf = args
    tril = jnp.tril(jnp.ones((CHUNK, CHUNK), dtype=bool))

    def step(carry, xs):
        C, n, m = carry
        qc, kc, vc, igc, lfc = xs
        b = jnp.cumsum(lfc, axis=-1)
        dmat = jnp.where(tril, b[..., :, None] - b[..., None, :] + igc[..., None, :], -jnp.inf)
        m_inter = b + m[..., None]
        m_t = jnp.maximum(jnp.max(dmat, axis=-1), m_inter)
        w_intra = jnp.exp(dmat - m_t[..., None])
        w_inter = jnp.exp(m_inter - m_t)
        a = jnp.einsum("bhtd,bhsd->bhts", qc, kc) * w_intra
        num = jnp.einsum("bhts,bhse->bhte", a, vc) + w_inter[..., None] * jnp.einsum("bhtd,bhde->bhte", qc, C)
        den = jnp.sum(a, axis=-1) + w_inter * jnp.einsum("bhtd,bhd->bht", qc, n)
        h = num / jnp.maximum(jnp.abs(den), jnp.exp(-m_t))[..., None]
        b_last = b[..., -1]
        m_new = jnp.maximum(b_last + m, jnp.max(b_last[..., None] - b + igc, axis=-1))
        w_s = jnp.exp(b_last[..., None] - b + igc - m_new[..., None])
        decay = jnp.exp(b_last + m - m_new)
        kw = kc * w_s[..., None]
        C = decay[..., None, None] * C + jnp.einsum("bhsd,bhse->bhde", kw, vc)
        n = decay[..., None] * n + jnp.sum(kw, axis=-2)
        return (C, n, m_new), h

    xs = [to_chunks(q), to_chunks(k), to_chunks(v), to_chunks(ig), to_chunks(lf)]
    state, h = lax.scan(step, state, xs)
    return from_chunks(h), state


def run_bidirectional(scan_f, scan_b, ctx_f, lat_f, ctx_b, lat_b, init):
    oc_f, s_f = scan_f(ctx_f, init)
    ol_f, _ = scan_f(lat_f, s_f)
    oc_b, s_b = scan_b([flip_seq(t) for t in ctx_b], init)
    ol_b, _ = scan_b([flip_seq(t) for t in lat_b], s_b)
    return oc_f + flip_seq(oc_b), ol_f + flip_seq(ol_b)


def token_mixers(pc, pl, cos, sin, layer, ret_decay, hgrn_lb, ret_gn, hgrn_gn,
                 mlstm_conv, mlstm_gate_b, mlstm_gn):
    f32 = jnp.float32
    B = pl.shape[0]
    offsets = [int(o) for o in np.cumsum(PROJ_SPLITS)[:-1]]
    pcs = jnp.split(pc, offsets, axis=-1)
    pls = jnp.split(pl, offsets, axis=-1)

    def ret_args(ps, rotate):
        q = split_heads(ps[0], RET_HEADS)
        k = split_heads(ps[1], RET_HEADS) * RET_DK ** -0.5
        if rotate:
            q = apply_rotary(q, cos, sin)
            k = apply_rotary(k, cos, sin)
        return [q, k, split_heads(ps[2], RET_HEADS)]

    log_g = jax.nn.log_sigmoid(ret_decay.astype(f32))
    rc, rl = ret_args(pcs, False), ret_args(pls, True)
    ret_c, ret_l = run_bidirectional(
        functools.partial(retention_scan, log_g=log_g[0]),
        functools.partial(retention_scan, log_g=log_g[1]),
        rc, rl, rc, rl, jnp.zeros((B, RET_HEADS, RET_DK, RET_DV), f32))

    def ret_out(o, ps):
        return head_norm(merge_heads(o), ret_gn, RET_HEADS, False) * jax.nn.silu(ps[3].astype(f32))

    sm = jax.nn.softmax(hgrn_lb.astype(f32), axis=0)
    lb = jnp.clip(jnp.cumsum(sm, axis=0)[layer] - sm[0], 0.0, 1.0)
    log_lb, log_ub = jnp.log(lb), jnp.log1p(-lb)

    def hgrn_args(ps):
        q = jax.nn.silu(split_heads(ps[4], HGRN_HEADS))
        i = split_heads(ps[7], HGRN_HEADS)
        dirs = []
        for f_raw in (ps[5], ps[6]):
            lf = jnp.logaddexp(log_lb, log_ub + jax.nn.log_sigmoid(f_raw.astype(f32)))
            lf = split_heads(lf, HGRN_HEADS)
            dirs.append([q, -jnp.expm1(lf), i, lf])
        return dirs

    hcf, hcb = hgrn_args(pcs)
    hlf, hlb = hgrn_args(pls)
    hg_c, hg_l = run_bidirectional(gla_scan, gla_scan, hcf, hlf, hcb, hlb,
                                   jnp.zeros((B, HGRN_HEADS, HGRN_DK, HGRN_DV), f32))

    def hgrn_out(o, ps):
        return head_norm(merge_heads(o), hgrn_gn, HGRN_HEADS, False) * jax.nn.silu(ps[8].astype(f32))

    def mlstm_args(ps):
        qk = jax.nn.silu(dwconv_centred(jnp.concatenate([ps[9], ps[10]], axis=-1), mlstm_conv))
        q = split_heads(qk[..., :MLSTM_QK], MLSTM_HEADS)
        k = split_heads(qk[..., MLSTM_QK:], MLSTM_HEADS) * MLSTM_DK ** -0.5
        v = split_heads(ps[11], MLSTM_HEADS)
        gates = jnp.swapaxes(ps[13].astype(f32) + mlstm_gate_b.astype(f32), 1, 2)
        ig_f, ig_b, fg_f, fg_b = jnp.split(gates, 4, axis=1)
        return ([q, k, v, ig_f, jax.nn.log_sigmoid(fg_f)],
                [q, k, v, ig_b, jax.nn.log_sigmoid(fg_b)])

    mcf, mcb = mlstm_args(pcs)
    mlf, mlb = mlstm_args(pls)
    m_init = (jnp.zeros((B, MLSTM_HEADS, MLSTM_DK, MLSTM_DV), f32),
              jnp.zeros((B, MLSTM_HEADS, MLSTM_DK), f32),
              jnp.zeros((B, MLSTM_HEADS), f32))
    ml_c, ml_l = run_bidirectional(mlstm_scan, mlstm_scan, mcf, mlf, mcb, mlb, m_init)

    def mlstm_out(o, ps):
        return head_norm(merge_heads(o), mlstm_gn, MLSTM_HEADS, True) * jax.nn.sigmoid(ps[12].astype(f32))

    mix_c = jnp.concatenate([ret_out(ret_c, pcs), hgrn_out(hg_c, pcs), mlstm_out(ml_c, pcs)], axis=-1)
    mix_l = jnp.concatenate([ret_out(ret_l, pls), hgrn_out(hg_l, pls), mlstm_out(ml_l, pls)], axis=-1)
    return mix_c.astype(pc.dtype), mix_l.astype(pl.dtype)


def swiglu(h, w1, w3, w2):
    return (jax.nn.silu(h @ w1) * (h @ w3)) @ w2


def moe_swiglu(xt, w_router, w1, w3, w2):
    T, D = xt.shape
    logits = (xt @ w_router).astype(jnp.float32)
    top_val, top_idx = lax.top_k(logits, TOP_K)
    gates = jax.nn.softmax(top_val, axis=-1)
    n_assign = T * TOP_K
    flat_e = top_idx.reshape(-1)
    flat_tok = jnp.arange(n_assign, dtype=jnp.int32) // TOP_K
    order = jnp.argsort(flat_e)
    sorted_e = flat_e[order]
    counts = jnp.bincount(flat_e, length=N_EXPERTS)
    padded = (counts + MOE_BLOCK - 1) // MOE_BLOCK * MOE_BLOCK
    pad_end = jnp.cumsum(padded)
    pad_start = pad_end - padded
    start = jnp.cumsum(counts) - counts
    dest = pad_start[sorted_e] + jnp.arange(n_assign, dtype=jnp.int32) - start[sorted_e]
    n_blocks = -(-n_assign // MOE_BLOCK) + N_EXPERTS
    n_slots = n_blocks * MOE_BLOCK
    slot_tok = jnp.full((n_slots,), T, dtype=jnp.int32).at[dest].set(flat_tok[order])
    slot_gate = jnp.zeros((n_slots,), jnp.float32).at[dest].set(gates.reshape(-1)[order])
    block_e = jnp.minimum(
        jnp.searchsorted(pad_end, jnp.arange(n_blocks, dtype=jnp.int32) * MOE_BLOCK, side="right"),
        N_EXPERTS - 1)
    x_pad = jnp.concatenate([xt, jnp.zeros((1, D), xt.dtype)], axis=0)
    xb = x_pad[slot_tok].reshape(n_blocks, MOE_BLOCK, D)

    def expert_block(args):
        xblk, e = args
        return swiglu(xblk, w1[e], w3[e], w2[e])

    yb = lax.map(expert_block, (xb, block_e))
    y = jnp.zeros((T + 1, D), jnp.float32).at[slot_tok].add(
        yb.reshape(n_slots, D).astype(jnp.float32) * slot_gate[:, None])
    return y[:T].astype(xt.dtype)


def setup_inputs(seed: int = 0) -> dict:
    key = jax.random.key(seed)
    ks = jax.random.split(key, 24)
    f32 = jnp.float32

    def nrm(k, shape, scale):
        return scale * jax.random.normal(k, shape, f32)

    H = MLSTM_HEADS
    ret_base = jnp.log(2.0 ** (5.0 + jnp.arange(RET_HEADS, dtype=f32)) - 1.0)
    f_base = jnp.tile(jnp.linspace(3.0, 6.0, H, dtype=f32), 2)[None]
    return {
        "x": nrm(ks[0], (BATCH, SEQ, D_MODEL), 1.0),
        "c": nrm(ks[1], (BATCH, D_MODEL), 1.0),
        "ctx": nrm(ks[2], (BATCH, CTX_LEN, D_MODEL), 1.0),
        "c_ctx": nrm(ks[3], (D_MODEL,), 1.0),
        "w_ada": nrm(ks[4], (DEPTH, D_MODEL, 6 * D_MODEL), 0.5 * D_MODEL ** -0.5),
        "b_ada": nrm(ks[5], (DEPTH, 6 * D_MODEL), 0.02),
        "norm_g": 1.0 + nrm(ks[6], (DEPTH, 4, D_MODEL), 0.05),
        "w_in": nrm(ks[7], (DEPTH, D_MODEL, PROJ_COLS), D_MODEL ** -0.5),
        "w_out": nrm(ks[8], (DEPTH, MIX_WIDTH, D_MODEL), MIX_WIDTH ** -0.5),
        "ret_decay": ret_base + nrm(ks[9], (DEPTH, 2, RET_HEADS), 0.1),
        "ret_gn": 1.0 + nrm(ks[10], (DEPTH, RET_WIDTH), 0.05),
        "hgrn_lb": nrm(ks[11], (DEPTH, HGRN_QK), 1.0),
        "hgrn_gn": 1.0 + nrm(ks[12], (DEPTH, HGRN_WIDTH), 0.05),
        "mlstm_conv": nrm(ks[13], (DEPTH, MLSTM_CONV, 2 * MLSTM_QK), MLSTM_CONV ** -0.5),
        "mlstm_gate_b": jnp.concatenate(
            [nrm(ks[14], (DEPTH, 2 * H), 0.1), f_base + nrm(ks[15], (DEPTH, 2 * H), 0.1)], axis=-1),
        "mlstm_gn": 1.0 + nrm(ks[16], (DEPTH, MLSTM_WIDTH), 0.05),
        "w_ffn1": nrm(ks[17], (N_DENSE, D_MODEL, D_FF), D_MODEL ** -0.5),
        "w_ffn3": nrm(ks[18], (N_DENSE, D_MODEL, D_FF), D_MODEL ** -0.5),
        "w_ffn2": nrm(ks[19], (N_DENSE, D_FF, D_MODEL), D_FF ** -0.5),
        "w_router": nrm(ks[20], (N_MOE, D_MODEL, N_EXPERTS), D_MODEL ** -0.5),
        "w_exp1": nrm(ks[21], (N_MOE, N_EXPERTS, D_MODEL, D_FF_EXPERT), D_MODEL ** -0.5),
        "w_exp3": nrm(ks[22], (N_MOE, N_EXPERTS, D_MODEL, D_FF_EXPERT), D_MODEL ** -0.5),
        "w_exp2": nrm(ks[23], (N_MOE, N_EXPERTS, D_FF_EXPERT, D_MODEL), D_FF_EXPERT ** -0.5),
    }


def reference(x, c, ctx, c_ctx, w_ada, b_ada, norm_g, w_in, w_out, ret_decay, ret_gn,
              hgrn_lb, hgrn_gn, mlstm_conv, mlstm_gate_b, mlstm_gn,
              w_ffn1, w_ffn3, w_ffn2, w_router, w_exp1, w_exp3, w_exp2):
    B, L, D = x.shape
    Lc = ctx.shape[1]
    cos, sin = axial_rotary_tables(L)
    xc = ctx
    s_lat = jax.nn.silu(c)
    s_ctx = jax.nn.silu(c_ctx)
    for layer in range(DEPTH):
        last = layer == DEPTH - 1
        mod = (s_lat @ w_ada[layer] + b_ada[layer]).reshape(B, 6, 1, D)
        modc = (s_ctx @ w_ada[layer] + b_ada[layer]).reshape(6, D)
        g = norm_g[layer]

        hl = rmsnorm(x, g[0]) * (1.0 + mod[:, 1]) + mod[:, 0]
        hc = rmsnorm(xc, g[0]) * (1.0 + modc[1]) + modc[0]
        p = jnp.concatenate([hc, hl], axis=1) @ w_in[layer]
        mix_c, mix_l = token_mixers(p[:, :Lc], p[:, Lc:], cos, sin, layer, ret_decay[layer], hgrn_lb,
                                    ret_gn[layer], hgrn_gn[layer], mlstm_conv[layer],
                                    mlstm_gate_b[layer], mlstm_gn[layer])
        if last:
            x = x + mod[:, 2] * rmsnorm(mix_l @ w_out[layer], g[1])
        else:
            y = rmsnorm(jnp.concatenate([mix_c, mix_l], axis=1) @ w_out[layer], g[1])
            xc = xc + modc[2] * y[:, :Lc]
            x = x + mod[:, 2] * y[:, Lc:]

        hl = rmsnorm(x, g[2]) * (1.0 + mod[:, 4]) + mod[:, 3]
        if last:
            h = hl
        else:
            hc = rmsnorm(xc, g[2]) * (1.0 + modc[4]) + modc[3]
            h = jnp.concatenate([hc, hl], axis=1)
        j = layer // 2
        if layer % 2 == 0:
            f = swiglu(h, w_ffn1[j], w_ffn3[j], w_ffn2[j])
        else:
            f = moe_swiglu(h.reshape(-1, D), w_router[j], w_exp1[j], w_exp3[j], w_exp2[j]).reshape(h.shape)
        f = rmsnorm(f, g[3])
        if last:
            x = x + mod[:, 5] * f
        else:
            xc = xc + modc[5] * f[:, :Lc]
            x = x + mod[:, 5] * f[:, Lc:]
    return x
```

```python
import functools

import numpy as np
import jax
import jax.numpy as jnp
from jax import lax
from jax.experimental import pallas as pl
from jax.experimental.pallas import tpu as pltpu

F32 = jnp.float32
BF16 = jnp.bfloat16

CHUNK = 128
NORM_EPS = 1e-6
ROPE_BASE = 10000.0
GRID_W = 64

RET_HEADS, RET_DK, RET_DV = 4, 128, 256
HGRN_HEADS, HGRN_DK, HGRN_DV = 4, 128, 128
MLSTM_HEADS, MLSTM_DK, MLSTM_DV = 4, 128, 128
N_EXPERTS = 8
TOP_K = 2

RET_QK = RET_HEADS * RET_DK
RET_WIDTH = RET_HEADS * RET_DV
HGRN_QK = HGRN_HEADS * HGRN_DK
HGRN_WIDTH = HGRN_HEADS * HGRN_DV
MLSTM_QK = MLSTM_HEADS * MLSTM_DK
MLSTM_WIDTH = MLSTM_HEADS * MLSTM_DV
MLSTM_GATES = 4 * MLSTM_HEADS

_C_RET_Q = 0
_C_RET_K = _C_RET_Q + RET_QK // 128
_C_RET_V = _C_RET_K + RET_QK // 128
_C_RET_G = _C_RET_V + RET_WIDTH // 128
_C_HG_Q = _C_RET_G + RET_WIDTH // 128
_C_HG_FF = _C_HG_Q + HGRN_QK // 128
_C_HG_FB = _C_HG_FF + HGRN_QK // 128
_C_HG_I = _C_HG_FB + HGRN_QK // 128
_C_HG_G = _C_HG_I + HGRN_WIDTH // 128
_C_ML_Q = _C_HG_G + HGRN_WIDTH // 128
_C_ML_K = _C_ML_Q + MLSTM_QK // 128
_C_ML_V = _C_ML_K + MLSTM_QK // 128
_C_ML_O = _C_ML_V + MLSTM_WIDTH // 128
PROJ_MAIN = (_C_ML_O + MLSTM_WIDTH // 128) * 128

_VMEM_CAP_BYTES = 56 * 1024 * 1024

ROW_TILE = 512
MM_TM, MM_TN = 1024, 512
FFN_TF = 512
MOE_TM = 512


def _cparams(sem, vmem_mb):
    return pltpu.CompilerParams(dimension_semantics=sem,
                                vmem_limit_bytes=min(int(vmem_mb * 1024 * 1024), _VMEM_CAP_BYTES))


def _bf(x):
    return x.astype(BF16)


def _dot(a, b):
    return jnp.dot(a, b, preferred_element_type=F32)


def _dot_nt(a, b):
    return lax.dot_general(a, b, (((1,), (1,)), ((), ())), preferred_element_type=F32)


def _dot_tn(a, b):
    return lax.dot_general(a, b, (((0,), (0,)), ((), ())), preferred_element_type=F32)


def _sigmoid(x):
    return 1.0 / (1.0 + jnp.exp(-x))


def _silu(x):
    return x * _sigmoid(x)


def _log_sigmoid(x):
    return jnp.minimum(x, 0.0) - jnp.log1p(jnp.exp(-jnp.abs(x)))


def _rms(x, g):
    return x * lax.rsqrt(jnp.mean(x * x, axis=-1, keepdims=True) + NORM_EPS) * g


def _mod_index(tile_rows, n_lat_rows_per_batch, n_batch):
    return lambda i: jnp.minimum((i * tile_rows) // n_lat_rows_per_batch, n_batch)


def _ada_kernel(s_ref, w_ref, b_ref, o_ref):
    s = _bf(_silu(s_ref[...]))
    o_ref[0] = _dot(s, _bf(w_ref[0])) + b_ref[0]


def _ada(s_in, w_ada, b_ada):
    depth, d, n = w_ada.shape
    tn = 1024
    return pl.pallas_call(
        _ada_kernel,
        grid=(depth, n // tn),
        in_specs=[pl.BlockSpec((8, d), lambda l, j: (0, 0)),
                  pl.BlockSpec((1, d, tn), lambda l, j: (l, 0, j)),
                  pl.BlockSpec((1, 1, tn), lambda l, j: (l, 0, j))],
        out_specs=pl.BlockSpec((1, 8, tn), lambda l, j: (l, 0, j)),
        out_shape=jax.ShapeDtypeStruct((depth, 8, n), F32),
        compiler_params=_cparams(("arbitrary", "arbitrary"), 40),
        name="ada",
    )(s_in, w_ada, b_ada.reshape(depth, 1, n))


def _prenorm_kernel(x_ref, g_ref, mod_ref, o_ref, *, sc, sh):
    m = mod_ref[0]
    y = _rms(x_ref[...], g_ref[...])
    o_ref[...] = (y * (1.0 + m[sc:sc + 1]) + m[sh:sh + 1]).astype(o_ref.dtype)


def _prenorm(xs, n_rows, g, modtab, sc, sh, lat_rows, out_dtype):
    d = xs.shape[1]
    nb = modtab.shape[0] - 1
    mi = _mod_index(ROW_TILE, lat_rows, nb)
    return pl.pallas_call(
        functools.partial(_prenorm_kernel, sc=sc, sh=sh),
        grid=(n_rows // ROW_TILE,),
        in_specs=[pl.BlockSpec((ROW_TILE, d), lambda i: (i, 0)),
                  pl.BlockSpec((1, d), lambda i: (0, 0)),
                  pl.BlockSpec((1, 6, d), lambda i: (mi(i), 0, 0))],
        out_specs=pl.BlockSpec((ROW_TILE, d), lambda i: (i, 0)),
        out_shape=jax.ShapeDtypeStruct((n_rows, d), out_dtype),
        compiler_params=_cparams(("arbitrary",), 32),
        name="prenorm",
    )(xs, g.reshape(1, d), modtab)


def _mm_kernel(x_ref, w_ref, o_ref):
    o_ref[...] = _dot(x_ref[...], w_ref[...]).astype(o_ref.dtype)


def _mm(x, w, tm, tn, out_dtype=F32):
    m, k = x.shape
    n = w.shape[1]
    return pl.pallas_call(
        _mm_kernel,
        grid=(m // tm, n // tn),
        in_specs=[pl.BlockSpec((tm, k), lambda i, j: (i, 0)),
                  pl.BlockSpec((k, tn), lambda i, j: (0, j))],
        out_specs=pl.BlockSpec((tm, tn), lambda i, j: (i, j)),
        out_shape=jax.ShapeDtypeStruct((m, n), out_dtype),
        compiler_params=_cparams(("arbitrary", "arbitrary"), 40),
        name="proj",
    )(x, w)


def _bwd_chunk(i, n_ctx, n_lat):
    return jnp.where(i < n_ctx, n_ctx - 1 - i, 2 * n_ctx + n_lat - 1 - i)


def _rows(c):
    return pl.ds(pl.multiple_of(c * CHUNK, CHUNK), CHUNK)


def _ret_kernel(ql, qc, kl, kc, vl, vc, gl, gc, cos_ref, sin_ref, lg_ref, gn_ref,
                ol_ref, oc_ref,
                q_s, k_s, v_s, o_s, st_s, dm_s, dq_s, dk_s, *, n_ctx, n_lat):
    lc = n_ctx * CHUNK
    nch = n_ctx + n_lat
    t_i = lax.broadcasted_iota(jnp.int32, (CHUNK, CHUNK), 0).astype(F32)
    s_i = lax.broadcasted_iota(jnp.int32, (CHUNK, CHUNK), 1).astype(F32)
    for d in range(2):
        lg = lg_ref[d, 0][0:1, 0:CHUNK]
        rel = (t_i - s_i) if d == 0 else (s_i - t_i)
        dm_s[d] = jnp.where(rel >= 0, jnp.exp(jnp.maximum(rel, 0.0) * lg), 0.0)
        p = t_i if d == 0 else (CHUNK - 1.0) - t_i
        dq_s[d] = jnp.exp((p + 1.0) * lg)
        dk_s[d] = jnp.exp((CHUNK - 1.0 - p) * lg)
        st_s[d] = jnp.zeros(st_s.shape[1:], F32)

    v_s[0:lc, :] = vc[...]
    v_s[lc:, :] = vl[...]
    scale = RET_DK ** -0.5

    def prep(c, carry):
        rows = _rows(c)
        cs = cos_ref[rows, :]
        sn = sin_ref[rows, :]
        q = q_s[rows, :]
        k = k_s[rows, :]
        q_s[rows, :] = q * cs + pltpu.roll(q, RET_DK // 2, 1) * sn
        k_s[rows, :] = (k * cs + pltpu.roll(k, RET_DK // 2, 1) * sn) * scale
        return carry

    q_s[0:lc, :] = qc[...]
    q_s[lc:, :] = ql[...]
    k_s[0:lc, :] = kc[...]
    k_s[lc:, :] = kl[...]
    lax.fori_loop(0, nch, prep, 0)

    def chunk(c, d):
        rows = _rows(c)
        q = q_s[rows, :]
        k = k_s[rows, :]
        vb = _bf(v_s[rows, :])
        a = _dot_nt(_bf(q), _bf(k)) * dm_s[d]
        st = st_s[d]
        o = _dot(_bf(a), vb) + _dot(_bf(q * dq_s[d]), _bf(st))
        dec_s = jnp.exp(float(CHUNK) * lg_ref[d, 0][0:1, :])
        st_s[d] = dec_s * st + _dot_tn(_bf(k * dk_s[d]), vb)
        return rows, o

    def fwd(c, carry):
        rows, o = chunk(c, 0)
        o_s[rows, :] = o
        return carry

    lax.fori_loop(0, nch, fwd, 0)

    def bwd(i, carry):
        rows, o = chunk(_bwd_chunk(i, n_ctx, n_lat), 1)
        o_s[rows, :] = o_s[rows, :] + o
        return carry

    lax.fori_loop(0, nch, bwd, 0)

    gn = gn_ref[...]

    def fin(o, g):
        return (_rms(o, gn) * _silu(g)).astype(BF16)

    if oc_ref is not None:
        oc_ref[...] = fin(o_s[0:lc, :], gc[...])
    ol_ref[...] = fin(o_s[lc:, :], gl[...])


def _retention(p, nb, lat_len, ctx_len, cosf, sinf, lg_b, gn, emit_ctx):
    n_ctx, n_lat = ctx_len // CHUNK, lat_len // CHUNK
    t_len = lat_len + ctx_len
    n_lat_blk = nb * lat_len // ctx_len

    def spec128(col0):
        return (pl.BlockSpec((lat_len, 128), lambda b, h: (b, col0 + h)),
                pl.BlockSpec((ctx_len, 128), lambda b, h: (n_lat_blk + b, col0 + h)))

    def spec256(col0):
        return (pl.BlockSpec((lat_len, 256), lambda b, h: (b, col0 // 2 + h)),
                pl.BlockSpec((ctx_len, 256), lambda b, h: (n_lat_blk + b, col0 // 2 + h)))

    in_specs = [*spec128(_C_RET_Q), *spec128(_C_RET_K), *spec256(_C_RET_V), *spec256(_C_RET_G),
                pl.BlockSpec((t_len, 128), lambda b, h: (0, 0)),
                pl.BlockSpec((t_len, 128), lambda b, h: (0, 0)),
                pl.BlockSpec((2, 1, 8, RET_DV), lambda b, h: (0, h, 0, 0)),
                pl.BlockSpec((1, RET_DV), lambda b, h: (0, h))]
    out_specs = [pl.BlockSpec((lat_len, RET_DV), lambda b, h: (b, h))]
    out_shape = [jax.ShapeDtypeStruct((nb * lat_len, RET_WIDTH), BF16)]
    if emit_ctx:
        out_specs.append(pl.BlockSpec((ctx_len, RET_DV), lambda b, h: (b, h)))
        out_shape.append(jax.ShapeDtypeStruct((nb * ctx_len, RET_WIDTH), BF16))

    def body(*refs):
        n_in = len(in_specs)
        ins, rest = refs[:n_in], refs[n_in:]
        if emit_ctx:
            ol, oc, scr = rest[0], rest[1], rest[2:]
        else:
            ol, oc, scr = rest[0], None, rest[1:]
        _ret_kernel(*ins, ol, oc, *scr, n_ctx=n_ctx, n_lat=n_lat)

    outs = pl.pallas_call(
        body,
        grid=(nb, RET_HEADS),
        in_specs=in_specs,
        out_specs=out_specs,
        out_shape=out_shape,
        scratch_shapes=[pltpu.VMEM((t_len, RET_DK), F32), pltpu.VMEM((t_len, RET_DK), F32),
                        pltpu.VMEM((t_len, RET_DV), F32), pltpu.VMEM((t_len, RET_DV), F32),
                        pltpu.VMEM((2, RET_DK, RET_DV), F32),
                        pltpu.VMEM((2, CHUNK, CHUNK), F32), pltpu.VMEM((2, CHUNK, CHUNK), F32),
                        pltpu.VMEM((2, CHUNK, CHUNK), F32)],
        compiler_params=_cparams(("arbitrary", "arbitrary"), 48),
        name="retention",
    )(p, p, p, p, p, p, p, p, cosf, sinf, lg_b, gn.reshape(1, RET_WIDTH))
    return outs


_HG_LEVELS = (64, 32, 16, 8, 4, 2, 1)
_HG_BLOCKS = 2 + len(_HG_LEVELS)


def _hgrn_constants():
    c = CHUNK
    t = np.arange(c)[:, None]
    u = np.arange(c)[None, :]
    mats = [(u <= t), (u > t)]
    masks = []
    for m in _HG_LEVELS:
        base = (t // (2 * m)) * (2 * m)
        ref = base + m - 1
        lower = t >= base + m
        mats.append(np.where(lower, (u > ref) & (u <= t), (u > t) & (u <= ref)))
        tb = t // (2 * m)
        sb = u // (2 * m)
        masks.append((tb == sb) & lower & (u < (sb * 2 * m + m)))
    masks.append(t == u)
    mf = np.concatenate([x.astype(np.float32) for x in mats], axis=0)
    kf = np.stack([x.astype(np.float32) for x in masks], axis=0)
    mb = np.concatenate([x.astype(np.float32)[::-1, ::-1] for x in mats], axis=0)
    kb = np.stack([x.astype(np.float32)[::-1, ::-1] for x in masks], axis=0)
    return np.stack([mf, mb], 0), np.stack([kf, kb], 0)


def _hgrn_kernel(ql, qc, ffl, ffc, fbl, fbc, il, ic, gl, gc, llb_ref, lub_ref, gn_ref, m_ref, k_ref,
                 ol_ref, oc_ref, q_s, lff_s, lfb_s, v_s, o_s, st_s, *, n_ctx, n_lat):
    lc = n_ctx * CHUNK
    nch = n_ctx + n_lat
    llb = llb_ref[...]
    lub = lub_ref[...]

    def log_f(x):
        return jnp.logaddexp(llb, lub + _log_sigmoid(x))

    q_s[0:lc, :] = qc[...]
    q_s[lc:, :] = ql[...]
    lff_s[0:lc, :] = ffc[...]
    lff_s[lc:, :] = ffl[...]
    lfb_s[0:lc, :] = fbc[...]
    lfb_s[lc:, :] = fbl[...]
    v_s[0:lc, :] = ic[...]
    v_s[lc:, :] = il[...]

    def prep(c, carry):
        rows = _rows(c)
        q_s[rows, :] = _silu(q_s[rows, :])
        lff_s[rows, :] = log_f(lff_s[rows, :])
        lfb_s[rows, :] = log_f(lfb_s[rows, :])
        return carry

    lax.fori_loop(0, nch, prep, 0)
    st_s[...] = jnp.zeros(st_s.shape, F32)
    nlev = len(_HG_LEVELS)

    def chunk(c, d):
        rows = _rows(c)
        q = q_s[rows, :]
        lf = (lff_s if d == 0 else lfb_s)[rows, :]
        k = 1.0 - jnp.exp(lf)
        vb = _bf(v_s[rows, :])
        lf_hi = _bf(lf)
        lf_lo = _bf(lf - lf_hi.astype(F32))
        mall = m_ref[d]
        e = _dot(mall, lf_hi) + _dot(mall, lf_lo)
        b = e[0:CHUNK]
        a = k_ref[d, nlev] * _dot_nt(_bf(q), _bf(k))
        for l in range(nlev):
            x = jnp.exp(e[(2 + l) * CHUNK:(3 + l) * CHUNK])
            a = a + k_ref[d, l] * _dot_nt(_bf(q * x), _bf(k * x))
        st = st_s[d]
        o = _dot(_bf(a), vb) + _dot_nt(_bf(q * jnp.exp(b)), _bf(st))
        b_last = b[CHUNK - 1:CHUNK] if d == 0 else b[0:1]
        kd = k * jnp.exp(e[CHUNK:2 * CHUNK])
        st_s[d] = jnp.exp(b_last) * st + _dot_tn(vb, _bf(kd))
        return rows, o

    def fwd(c, carry):
        rows, o = chunk(c, 0)
        o_s[rows, :] = o
        return carry

    lax.fori_loop(0, nch, fwd, 0)

    def bwd(i, carry):
        rows, o = chunk(_bwd_chunk(i, n_ctx, n_lat), 1)
        o_s[rows, :] = o_s[rows, :] + o
        return carry

    lax.fori_loop(0, nch, bwd, 0)
    gn = gn_ref[...]

    def fin(o, g):
        return (_rms(o, gn) * _silu(g)).astype(BF16)

    if oc_ref is not None:
        oc_ref[...] = fin(o_s[0:lc, :], gc[...])
    ol_ref[...] = fin(o_s[lc:, :], gl[...])


def _hgrn(p, nb, lat_len, ctx_len, log_lb, log_ub, gn, mats, masks, emit_ctx):
    n_ctx, n_lat = ctx_len // CHUNK, lat_len // CHUNK
    t_len = lat_len + ctx_len
    n_lat_blk = nb * lat_len // ctx_len

    def spec(col0):
        return (pl.BlockSpec((lat_len, 128), lambda b, h: (b, col0 + h)),
                pl.BlockSpec((ctx_len, 128), lambda b, h: (n_lat_blk + b, col0 + h)))

    vec = pl.BlockSpec((1, 128), lambda b, h: (0, h))
    in_specs = [*spec(_C_HG_Q), *spec(_C_HG_FF), *spec(_C_HG_FB), *spec(_C_HG_I), *spec(_C_HG_G),
                vec, vec, vec,
                pl.BlockSpec(mats.shape, lambda b, h: (0, 0, 0)),
                pl.BlockSpec(masks.shape, lambda b, h: (0, 0, 0, 0))]
    out_specs = [pl.BlockSpec((lat_len, HGRN_DV), lambda b, h: (b, h))]
    out_shape = [jax.ShapeDtypeStruct((nb * lat_len, HGRN_WIDTH), BF16)]
    if emit_ctx:
        out_specs.append(pl.BlockSpec((ctx_len, HGRN_DV), lambda b, h: (b, h)))
        out_shape.append(jax.ShapeDtypeStruct((nb * ctx_len, HGRN_WIDTH), BF16))

    def body(*refs):
        n_in = len(in_specs)
        ins, rest = refs[:n_in], refs[n_in:]
        if emit_ctx:
            ol, oc, scr = rest[0], rest[1], rest[2:]
        else:
            ol, oc, scr = rest[0], None, rest[1:]
        _hgrn_kernel(*ins, ol, oc, *scr, n_ctx=n_ctx, n_lat=n_lat)

    return pl.pallas_call(
        body,
        grid=(nb, HGRN_HEADS),
        in_specs=in_specs,
        out_specs=out_specs,
        out_shape=out_shape,
        scratch_shapes=[pltpu.VMEM((t_len, HGRN_DK), F32), pltpu.VMEM((t_len, HGRN_DK), F32),
                        pltpu.VMEM((t_len, HGRN_DK), F32), pltpu.VMEM((t_len, HGRN_DV), F32),
                        pltpu.VMEM((t_len, HGRN_DV), F32), pltpu.VMEM((2, HGRN_DV, HGRN_DK), F32)],
        compiler_params=_cparams(("arbitrary", "arbitrary"), 48),
        name="hgrn2",
    )(p, p, p, p, p, p, p, p, p, p, log_lb.reshape(1, HGRN_QK), log_ub.reshape(1, HGRN_QK),
      gn.reshape(1, HGRN_WIDTH), mats, masks)


def _mlstm_kernel(ql, qc, kl, kc, vl, vc, ogl, ogc, gcl, gcc, grl, grc, bc_ref, br_ref,
                  wq_ref, wk_ref, gn_ref, ol_ref, oc_ref,
                  qr_s, kr_s, q_s, k_s, v_s, gc_s, gr_s, o_s, st_s, m_s, *, n_ctx, n_lat):
    lc = n_ctx * CHUNK
    nch = n_ctx + n_lat
    t_len = nch * CHUNK
    h = pl.program_id(1)
    nh = MLSTM_HEADS
    ext = 2 * MLSTM_DV

    qr_s[0:lc, :] = qc[...]
    qr_s[lc:, :] = ql[...]
    kr_s[0:lc, :] = kc[...]
    kr_s[lc:, :] = kl[...]
    v_s[0:lc, 0:MLSTM_DV] = vc[...]
    v_s[lc:, 0:MLSTM_DV] = vl[...]
    lane = lax.broadcasted_iota(jnp.int32, (t_len, MLSTM_DV), 1)
    v_s[:, MLSTM_DV:] = jnp.where(lane == 0, 1.0, 0.0)
    gc_s[0:lc, :] = gcc[...] + bc_ref[...]
    gc_s[lc:, :] = gcl[...] + bc_ref[...]
    for cc in range(nch):
        src, c0 = (grc, cc) if cc < n_ctx else (grl, cc - n_ctx)
        gr_s[cc] = src[0, :, c0 * CHUNK:(c0 + 1) * CHUNK] + br_ref[...]

    row = lax.broadcasted_iota(jnp.int32, (CHUNK, MLSTM_DK), 0)
    scale = MLSTM_DK ** -0.5

    def conv(src, w_ref, c):
        r0 = pl.multiple_of(c * CHUNK, CHUNK)
        x = src[pl.ds(r0, CHUNK), :]
        has_prev = jnp.logical_and(c != 0, c != n_ctx)
        has_next = jnp.logical_and(c != n_ctx - 1, c != nch - 1)
        pr = src[pl.ds(jnp.maximum(r0 - 1, 0), 1), :]
        nx = src[pl.ds(jnp.minimum(r0 + CHUNK, t_len - 1), 1), :]
        pr = jnp.where(has_prev, pr, 0.0)
        nx = jnp.where(has_next, nx, 0.0)
        xp = jnp.where(row == 0, pr, pltpu.roll(x, 1, 0))
        xn = jnp.where(row == CHUNK - 1, nx, pltpu.roll(x, CHUNK - 1, 0))
        w = w_ref[...]
        return _silu(w[0:1] * xp + w[1:2] * x + w[2:3] * xn)

    def prep(c, carry):
        rows = _rows(c)
        q_s[rows, :] = conv(qr_s, wq_ref, c)
        k_s[rows, :] = conv(kr_s, wk_ref, c) * scale
        return carry

    lax.fori_loop(0, nch, prep, 0)
    st_s[...] = jnp.zeros(st_s.shape, F32)
    m_s[...] = jnp.zeros(m_s.shape, F32)

    t_i = lax.broadcasted_iota(jnp.int32, (CHUNK, CHUNK), 0)
    s_i = lax.broadcasted_iota(jnp.int32, (CHUNK, CHUNK), 1)
    lane16 = lax.broadcasted_iota(jnp.int32, (CHUNK, MLSTM_GATES), 1)
    sub16 = lax.broadcasted_iota(jnp.int32, (MLSTM_GATES, CHUNK), 0)

    def pick_col(g, j):
        return jnp.sum(jnp.where(lane16 == j, g, 0.0), axis=1, keepdims=True)

    def pick_row(g, j):
        return jnp.sum(jnp.where(sub16 == j, g, 0.0), axis=0, keepdims=True)

    def chunk(c, d):
        rows = _rows(c)
        q = q_s[rows, :]
        k = k_s[rows, :]
        vb = _bf(v_s[rows, :])
        g_c = gc_s[rows, :]
        g_r = gr_s[c]
        ig_c = pick_col(g_c, d * nh + h)
        lf_c = _log_sigmoid(pick_col(g_c, 2 * nh + d * nh + h))
        ig_r = pick_row(g_r, d * nh + h)
        lf_r = _log_sigmoid(pick_row(g_r, 2 * nh + d * nh + h))
        tri = (s_i <= t_i) if d == 0 else (s_i >= t_i)
        tri_t = (t_i <= s_i) if d == 0 else (t_i >= s_i)
        b_c = jnp.sum(jnp.where(tri, lf_r, 0.0), axis=1, keepdims=True)
        b_r = jnp.sum(jnp.where(tri_t, lf_c, 0.0), axis=0, keepdims=True)
        m_prev = m_s[d]
        dm = jnp.where(tri, b_c - b_r + ig_r, -jnp.inf)
        m_inter = b_c + m_prev
        m_t = jnp.maximum(jnp.max(dm, axis=1, keepdims=True), m_inter)
        w_intra = jnp.exp(dm - m_t)
        w_inter = jnp.exp(m_inter - m_t)
        qb = _bf(q)
        a = _dot_nt(qb, _bf(k)) * w_intra
        st = st_s[d]
        nd = _dot(_bf(a), vb) + w_inter * _dot(qb, _bf(st))
        num = nd[:, 0:MLSTM_DV]
        den = nd[:, MLSTM_DV:MLSTM_DV + 1]
        hh = num / jnp.maximum(jnp.abs(den), jnp.exp(-m_t))
        b_last = b_c[CHUNK - 1:CHUNK] if d == 0 else b_c[0:1]
        m_new = jnp.maximum(b_last + m_prev, jnp.max(b_last - b_r + ig_r, axis=1, keepdims=True))
        w_s = jnp.exp(b_last - b_c + ig_c - m_new)
        decay = jnp.exp(b_last + m_prev - m_new)
        st_s[d] = decay * st + _dot_tn(_bf(k * w_s), vb)
        m_s[d] = m_new
        return rows, hh

    def fwd(c, carry):
        rows, o = chunk(c, 0)
        o_s[rows, :] = o
        return carry

    lax.fori_loop(0, nch, fwd, 0)

    def bwd(i, carry):
        rows, o = chunk(_bwd_chunk(i, n_ctx, n_lat), 1)
        o_s[rows, :] = o_s[rows, :] + o
        return carry

    lax.fori_loop(0, nch, bwd, 0)
    gn = gn_ref[...]

    def fin(o, og):
        y = o - jnp.mean(o, axis=-1, keepdims=True)
        y = y * lax.rsqrt(jnp.mean(y * y, axis=-1, keepdims=True) + NORM_EPS)
        return (y * gn * _sigmoid(og)).astype(BF16)

    if oc_ref is not None:
        oc_ref[...] = fin(o_s[0:lc, :], ogc[...])
    ol_ref[...] = fin(o_s[lc:, :], ogl[...])


def _mlstm(p, g_col, g_row_lat, g_row_ctx, nb, lat_len, ctx_len, gate_b, conv_w, gn, emit_ctx):
    n_ctx, n_lat = ctx_len // CHUNK, lat_len // CHUNK
    t_len = lat_len + ctx_len
    n_lat_blk = nb * lat_len // ctx_len
    ng = MLSTM_GATES

    def spec(col0):
        return (pl.BlockSpec((lat_len, 128), lambda b, h: (b, col0 + h)),
                pl.BlockSpec((ctx_len, 128), lambda b, h: (n_lat_blk + b, col0 + h)))

    in_specs = [*spec(_C_ML_Q), *spec(_C_ML_K), *spec(_C_ML_V), *spec(_C_ML_O),
                pl.BlockSpec((lat_len, ng), lambda b, h: (b, 0)),
                pl.BlockSpec((ctx_len, ng), lambda b, h: (n_lat_blk + b, 0)),
                pl.BlockSpec((1, ng, lat_len), lambda b, h: (b, 0, 0)),
                pl.BlockSpec((1, ng, ctx_len), lambda b, h: (b, 0, 0)),
                pl.BlockSpec((1, ng), lambda b, h: (0, 0)),
                pl.BlockSpec((ng, 1), lambda b, h: (0, 0)),
                pl.BlockSpec((3, 128), lambda b, h: (0, h)),
                pl.BlockSpec((3, 128), lambda b, h: (0, MLSTM_HEADS + h)),
                pl.BlockSpec((1, 128), lambda b, h: (0, h))]
    out_specs = [pl.BlockSpec((lat_len, MLSTM_DV), lambda b, h: (b, h))]
    out_shape = [jax.ShapeDtypeStruct((nb * lat_len, MLSTM_WIDTH), BF16)]
    if emit_ctx:
        out_specs.append(pl.BlockSpec((ctx_len, MLSTM_DV), lambda b, h: (b, h)))
        out_shape.append(jax.ShapeDtypeStruct((nb * ctx_len, MLSTM_WIDTH), BF16))

    def body(*refs):
        n_in = len(in_specs)
        ins, rest = refs[:n_in], refs[n_in:]
        if emit_ctx:
            ol, oc, scr = rest[0], rest[1], rest[2:]
        else:
            ol, oc, scr = rest[0], None, rest[1:]
        _mlstm_kernel(*ins, ol, oc, *scr, n_ctx=n_ctx, n_lat=n_lat)

    dk, dv = MLSTM_DK, MLSTM_DV
    return pl.pallas_call(
        body,
        grid=(nb, MLSTM_HEADS),
        in_specs=in_specs,
        out_specs=out_specs,
        out_shape=out_shape,
        scratch_shapes=[pltpu.VMEM((t_len, dk), F32), pltpu.VMEM((t_len, dk), F32),
                        pltpu.VMEM((t_len, dk), F32), pltpu.VMEM((t_len, dk), F32),
                        pltpu.VMEM((t_len, 2 * dv), F32),
                        pltpu.VMEM((t_len, ng), F32), pltpu.VMEM((n_ctx + n_lat, ng, CHUNK), F32),
                        pltpu.VMEM((t_len, dv), F32),
                        pltpu.VMEM((2, dk, 2 * dv), F32), pltpu.VMEM((2, 1, 1), F32)],
        compiler_params=_cparams(("arbitrary", "arbitrary"), 48),
        name="mlstm",
    )(p, p, p, p, p, p, p, p, g_col, g_col, g_row_lat, g_row_ctx,
      gate_b.reshape(1, ng), gate_b.reshape(ng, 1), conv_w, conv_w, gn.reshape(1, MLSTM_WIDTH))


def _wout_kernel(mr_ref, mh_ref, mm_ref, wr_ref, wh_ref, wm_ref, x_ref, g_ref, mod_ref, o_ref):
    y = _dot(mr_ref[...], wr_ref[...]) + _dot(mh_ref[...], wh_ref[...]) + _dot(mm_ref[...], wm_ref[...])
    m = mod_ref[0]
    o_ref[...] = x_ref[...] + m[2:3] * _rms(y, g_ref[...])


def _wout(mix_r, mix_h, mix_m, w_out, xs, n_rows, g1, modtab, lat_rows):
    d = xs.shape[1]
    nb = modtab.shape[0] - 1
    mi = _mod_index(ROW_TILE, lat_rows, nb)
    tm = ROW_TILE
    w_r = w_out[0:RET_WIDTH]
    w_h = w_out[RET_WIDTH:RET_WIDTH + HGRN_WIDTH]
    w_m = w_out[RET_WIDTH + HGRN_WIDTH:]
    full = lambda a: pl.BlockSpec(a.shape, lambda i: (0, 0))
    return pl.pallas_call(
        _wout_kernel,
        grid=(n_rows // tm,),
        in_specs=[pl.BlockSpec((tm, RET_WIDTH), lambda i: (i, 0)),
                  pl.BlockSpec((tm, HGRN_WIDTH), lambda i: (i, 0)),
                  pl.BlockSpec((tm, MLSTM_WIDTH), lambda i: (i, 0)),
                  full(w_r), full(w_h), full(w_m),
                  pl.BlockSpec((tm, d), lambda i: (i, 0)),
                  pl.BlockSpec((1, d), lambda i: (0, 0)),
                  pl.BlockSpec((1, 6, d), lambda i: (mi(i), 0, 0))],
        out_specs=pl.BlockSpec((tm, d), lambda i: (i, 0)),
        out_shape=jax.ShapeDtypeStruct((n_rows, d), F32),
        compiler_params=_cparams(("arbitrary",), 48),
        name="wout",
    )(mix_r, mix_h, mix_m, w_r, w_h, w_m, xs, g1.reshape(1, d), modtab)


def _ffn_kernel(h_ref, w1_ref, w3_ref, w2_ref, x_ref, g_ref, mod_ref, o_ref, acc_ref):
    f = pl.program_id(1)

    @pl.when(f == 0)
    def _():
        acc_ref[...] = jnp.zeros(acc_ref.shape, F32)

    h = h_ref[...]
    u = _silu(_dot(h, w1_ref[...])) * _dot(h, w3_ref[...])
    acc_ref[...] += _dot(_bf(u), w2_ref[...])

    @pl.when(f == pl.num_programs(1) - 1)
    def _():
        m = mod_ref[0]
        o_ref[...] = x_ref[...] + m[5:6] * _rms(acc_ref[...], g_ref[...])


def _ffn(hb, w1, w3, w2, xs, g3, modtab, lat_rows):
    n_rows, d = hb.shape
    dff = w1.shape[1]
    nb = modtab.shape[0] - 1
    tm, tf = ROW_TILE, FFN_TF
    mi = _mod_index(tm, lat_rows, nb)
    return pl.pallas_call(
        _ffn_kernel,
        grid=(n_rows // tm, dff // tf),
        in_specs=[pl.BlockSpec((tm, d), lambda i, f: (i, 0)),
                  pl.BlockSpec((d, tf), lambda i, f: (0, f)),
                  pl.BlockSpec((d, tf), lambda i, f: (0, f)),
                  pl.BlockSpec((tf, d), lambda i, f: (f, 0)),
                  pl.BlockSpec((tm, d), lambda i, f: (i, 0)),
                  pl.BlockSpec((1, d), lambda i, f: (0, 0)),
                  pl.BlockSpec((1, 6, d), lambda i, f: (mi(i), 0, 0))],
        out_specs=pl.BlockSpec((tm, d), lambda i, f: (i, 0)),
        out_shape=jax.ShapeDtypeStruct((n_rows, d), F32),
        scratch_shapes=[pltpu.VMEM((tm, d), F32)],
        compiler_params=_cparams(("arbitrary", "arbitrary"), 48),
        name="ffn",
    )(hb, w1, w3, w2, xs, g3.reshape(1, d), modtab)


_META_E0, _META_E1, _META_R0, _META_R1, _META_G0, _META_G1 = range(6)


def _router_kernel(x_ref, g_ref, mod_ref, wr_ref, tri_ref, meta_ref, cnt_ref, carry_ref):
    i = pl.program_id(0)

    @pl.when(i == 0)
    def _():
        carry_ref[...] = jnp.zeros(carry_ref.shape, F32)

    m = mod_ref[0]
    hmod = _rms(x_ref[...], g_ref[...]) * (1.0 + m[4:5]) + m[3:4]
    logits = jnp.dot(hmod, wr_ref[...], precision=lax.Precision.HIGHEST, preferred_element_type=F32)
    tm = logits.shape[0]
    lane = lax.broadcasted_iota(jnp.int32, logits.shape, 1)
    lanef = lane.astype(F32)
    logits = jnp.where(lane < N_EXPERTS, logits, -jnp.inf)
    v0 = jnp.max(logits, axis=1, keepdims=True)
    e0 = jnp.min(jnp.where(logits == v0, lanef, 1e9), axis=1, keepdims=True)
    rest = jnp.where(lanef == e0, -jnp.inf, logits)
    v1 = jnp.max(rest, axis=1, keepdims=True)
    e1 = jnp.min(jnp.where(rest == v1, lanef, 1e9), axis=1, keepdims=True)
    ex = jnp.exp(v1 - v0)
    g0 = 1.0 / (1.0 + ex)
    g1 = ex / (1.0 + ex)
    oh0 = lanef == e0
    oh1 = lanef == e1
    oh = jnp.where(jnp.logical_or(oh0, oh1), 1.0, 0.0)
    before = _dot(tri_ref[...], _bf(oh)) + carry_ref[0:1, :]
    r0 = jnp.sum(jnp.where(oh0, before, 0.0), axis=1, keepdims=True)
    r1 = jnp.sum(jnp.where(oh1, before, 0.0), axis=1, keepdims=True)
    carry_ref[0:1, :] = carry_ref[0:1, :] + jnp.sum(oh, axis=0, keepdims=True)
    meta = jnp.zeros(logits.shape, F32)
    for j, val in ((_META_E0, e0), (_META_E1, e1), (_META_R0, r0), (_META_R1, r1),
                   (_META_G0, g0), (_META_G1, g1)):
        meta = jnp.where(lane == j, val, meta)
    meta_ref[...] = meta
    cnt_ref[...] = carry_ref[...]


def _router(xs, n_rows, g2, modtab, w_router, lat_rows):
    d = xs.shape[1]
    nb = modtab.shape[0] - 1
    tm = ROW_TILE
    mi = _mod_index(tm, lat_rows, nb)
    wr = jnp.zeros((d, 128), F32).at[:, :N_EXPERTS].set(w_router)
    tri = jnp.asarray(np.tril(np.ones((tm, tm), np.float32), -1), BF16)
    return pl.pallas_call(
        _router_kernel,
        grid=(n_rows // tm,),
        in_specs=[pl.BlockSpec((tm, d), lambda i: (i, 0)),
                  pl.BlockSpec((1, d), lambda i: (0, 0)),
                  pl.BlockSpec((1, 6, d), lambda i: (mi(i), 0, 0)),
                  pl.BlockSpec((d, 128), lambda i: (0, 0)),
                  pl.BlockSpec((tm, tm), lambda i: (0, 0))],
        out_specs=[pl.BlockSpec((tm, 128), lambda i: (i, 0)),
                   pl.BlockSpec((8, 128), lambda i: (0, 0))],
        out_shape=[jax.ShapeDtypeStruct((n_rows, 128), F32),
                   jax.ShapeDtypeStruct((8, 128), F32)],
        scratch_shapes=[pltpu.VMEM((8, 128), F32)],
        compiler_params=_cparams(("arbitrary",), 32),
        name="router",
    )(xs, g2.reshape(1, d), modtab, wr, tri)


def _dispatch_kernel(dest_ref, x_ref, g_ref, mod_ref, init_ref, o_ref, h_s, sem):
    del init_ref
    i = pl.program_id(0)
    tm = h_s.shape[0]
    m = mod_ref[0]
    h_s[...] = _rms(x_ref[...], g_ref[...]) * (1.0 + m[4:5]) + m[3:4]

    def row_copy(r, k):
        dst = dest_ref[TOP_K * (i * tm + r) + k]
        return pltpu.make_async_copy(h_s.at[pl.ds(r, 1)], o_ref.at[pl.ds(dst, 1)], sem)

    def start(r, carry):
        for k in range(TOP_K):
            row_copy(r, k).start()
        return carry

    lax.fori_loop(0, tm, start, 0)

    def wait(r, carry):
        for k in range(TOP_K):
            row_copy(r, k).wait()
        return carry

    lax.fori_loop(0, tm, wait, 0)


def _dispatch(dest, xs, n_rows, g2, modtab, n_slots, lat_rows):
    d = xs.shape[1]
    nb = modtab.shape[0] - 1
    tm = ROW_TILE
    mi = _mod_index(tm, lat_rows, nb)
    init = jnp.zeros((n_slots, d), F32)
    grid_spec = pltpu.PrefetchScalarGridSpec(
        num_scalar_prefetch=1,
        grid=(n_rows // tm,),
        in_specs=[pl.BlockSpec((tm, d), lambda i, dst: (i, 0)),
                  pl.BlockSpec((1, d), lambda i, dst: (0, 0)),
                  pl.BlockSpec((1, 6, d), lambda i, dst: (mi(i), 0, 0)),
                  pl.BlockSpec(memory_space=pl.ANY)],
        out_specs=pl.BlockSpec(memory_space=pl.ANY),
        scratch_shapes=[pltpu.VMEM((tm, d), F32), pltpu.SemaphoreType.DMA(())],
    )
    return pl.pallas_call(
        _dispatch_kernel,
        grid_spec=grid_spec,
        out_shape=jax.ShapeDtypeStruct((n_slots, d), F32),
        input_output_aliases={4: 0},
        compiler_params=_cparams(("arbitrary",), 32),
        name="dispatch",
    )(dest, xs, g2.reshape(1, d), modtab, init)


def _expert_kernel(te_ref, nu_ref, x_ref, w1_ref, w3_ref, w2_ref, o_ref, xb_s, acc_s):
    j = pl.program_id(0)
    f = pl.program_id(1)

    @pl.when(j < nu_ref[0])
    def _():
        @pl.when(f == 0)
        def _():
            xb_s[...] = _bf(x_ref[...])
            acc_s[...] = jnp.zeros(acc_s.shape, F32)

        h = xb_s[...]
        u = _silu(_dot(h, w1_ref[0])) * _dot(h, w3_ref[0])
        acc_s[...] += _dot(_bf(u), w2_ref[0])

        @pl.when(f == pl.num_programs(1) - 1)
        def _():
            o_ref[...] = acc_s[...]

    @pl.when(jnp.logical_and(j >= nu_ref[0], f == pl.num_programs(1) - 1))
    def _():
        o_ref[...] = jnp.zeros(o_ref.shape, F32)


def _experts(tile_e, n_used, xsort, w1, w3, w2):
    n_slots, d = xsort.shape
    dff = w1.shape[2]
    tm, tf = MOE_TM, FFN_TF
    nf = dff // tf

    def jj(j, nu):
        return jnp.minimum(j, nu[0] - 1)

    def ff(j, f, nu):
        return jnp.where(j < nu[0], f, nf - 1)

    grid_spec = pltpu.PrefetchScalarGridSpec(
        num_scalar_prefetch=2,
        grid=(n_slots // tm, nf),
        in_specs=[pl.BlockSpec((tm, d), lambda j, f, te, nu: (jj(j, nu), 0)),
                  pl.BlockSpec((1, d, tf), lambda j, f, te, nu: (te[jj(j, nu)], 0, ff(j, f, nu))),
                  pl.BlockSpec((1, d, tf), lambda j, f, te, nu: (te[jj(j, nu)], 0, ff(j, f, nu))),
                  pl.BlockSpec((1, tf, d), lambda j, f, te, nu: (te[jj(j, nu)], ff(j, f, nu), 0))],
        out_specs=pl.BlockSpec((tm, d), lambda j, f, te, nu: (j, 0)),
        scratch_shapes=[pltpu.VMEM((tm, d), BF16), pltpu.VMEM((tm, d), F32)],
    )
    return pl.pallas_call(
        _expert_kernel,
        grid_spec=grid_spec,
        out_shape=jax.ShapeDtypeStruct((n_slots, d), F32),
        compiler_params=_cparams(("arbitrary", "arbitrary"), 48),
        name="experts",
    )(tile_e, n_used, xsort, w1, w3, w2)


def _combine_kernel(dest_ref, y_ref, meta_ref, x_ref, g_ref, mod_ref, o_ref, buf_s, sem):
    i = pl.program_id(0)
    tm = x_ref.shape[0]

    def row_copy(r, k):
        src = dest_ref[TOP_K * (i * tm + r) + k]
        return pltpu.make_async_copy(y_ref.at[pl.ds(src, 1)], buf_s.at[k, pl.ds(r, 1)], sem)

    def start(r, carry):
        for k in range(TOP_K):
            row_copy(r, k).start()
        return carry

    lax.fori_loop(0, tm, start, 0)

    def wait(r, carry):
        for k in range(TOP_K):
            row_copy(r, k).wait()
        return carry

    lax.fori_loop(0, tm, wait, 0)

    meta = meta_ref[...]
    lane = lax.broadcasted_iota(jnp.int32, meta.shape, 1)
    g0 = jnp.sum(jnp.where(lane == _META_G0, meta, 0.0), axis=1, keepdims=True)
    g1 = jnp.sum(jnp.where(lane == _META_G1, meta, 0.0), axis=1, keepdims=True)
    y = buf_s[0] * g0 + buf_s[1] * g1
    m = mod_ref[0]
    o_ref[...] = x_ref[...] + m[5:6] * _rms(y, g_ref[...])


def _combine(dest, yb, meta, xs, n_rows, g3, modtab, lat_rows):
    d = xs.shape[1]
    nb = modtab.shape[0] - 1
    tm = ROW_TILE
    mi = _mod_index(tm, lat_rows, nb)
    grid_spec = pltpu.PrefetchScalarGridSpec(
        num_scalar_prefetch=1,
        grid=(n_rows // tm,),
        in_specs=[pl.BlockSpec(memory_space=pl.ANY),
                  pl.BlockSpec((tm, 128), lambda i, dst: (i, 0)),
                  pl.BlockSpec((tm, d), lambda i, dst: (i, 0)),
                  pl.BlockSpec((1, d), lambda i, dst: (0, 0)),
                  pl.BlockSpec((1, 6, d), lambda i, dst: (mi(i), 0, 0))],
        out_specs=pl.BlockSpec((tm, d), lambda i, dst: (i, 0)),
        scratch_shapes=[pltpu.VMEM((TOP_K, tm, d), F32), pltpu.SemaphoreType.DMA(())],
    )
    return pl.pallas_call(
        _combine_kernel,
        grid_spec=grid_spec,
        out_shape=jax.ShapeDtypeStruct((n_rows, d), F32),
        compiler_params=_cparams(("arbitrary",), 40),
        name="combine",
    )(dest, yb, meta, xs, g3.reshape(1, d), modtab)


def _moe(xs, n_rows, g2, g3, modtab, w_router, w1, w3, w2, lat_rows):
    meta, cnt = _router(xs, n_rows, g2, modtab, w_router, lat_rows)
    counts = cnt[0, :N_EXPERTS].astype(jnp.int32)
    padded = (counts + MOE_TM - 1) // MOE_TM * MOE_TM
    pad_end = jnp.cumsum(padded)
    pad_start = pad_end - padded
    e = meta[:, _META_E0:_META_E1 + 1].astype(jnp.int32)
    r = meta[:, _META_R0:_META_R1 + 1].astype(jnp.int32)
    dest = (pad_start[e] + r).reshape(-1)
    n_tiles = (n_rows * TOP_K) // MOE_TM + N_EXPERTS
    n_slots = n_tiles * MOE_TM
    tile_e = jnp.minimum(
        jnp.searchsorted(pad_end, jnp.arange(n_tiles, dtype=jnp.int32) * MOE_TM, side="right"),
        N_EXPERTS - 1).astype(jnp.int32)
    n_used = (pad_end[-1:] // MOE_TM).astype(jnp.int32)
    xsort = _dispatch(dest, xs, n_rows, g2, modtab, n_slots, lat_rows)
    yb = _experts(tile_e, n_used, xsort, w1, w3, w2)
    return _combine(dest, yb, meta, xs, n_rows, g3, modtab, lat_rows)


def _rotary_tables(lat_len, ctx_len):
    rows = lat_len // GRID_W
    row = jnp.repeat(jnp.arange(rows, dtype=F32), GRID_W)
    col = jnp.tile(jnp.arange(GRID_W, dtype=F32), rows)
    n_freq = RET_DK // 4
    inv = ROPE_BASE ** (-jnp.arange(n_freq, dtype=F32) / n_freq)
    ang = jnp.concatenate([row[:, None] * inv, col[:, None] * inv], axis=-1)
    cos, sin = jnp.cos(ang), jnp.sin(ang)
    cosf = jnp.concatenate([jnp.ones((ctx_len, RET_DK), F32), jnp.concatenate([cos, cos], -1)], 0)
    sinf = jnp.concatenate([jnp.zeros((ctx_len, RET_DK), F32), jnp.concatenate([-sin, sin], -1)], 0)
    return cosf, sinf


def kernel(x, c, ctx, c_ctx, w_ada, b_ada, norm_g, w_in, w_out, ret_decay, ret_gn, hgrn_lb, hgrn_gn,
           mlstm_conv, mlstm_gate_b, mlstm_gn, w_ffn1, w_ffn3, w_ffn2, w_router, w_exp1, w_exp3, w_exp2):
    nb, lat_len, d = x.shape
    ctx_len = ctx.shape[1]
    depth = w_ada.shape[0]
    n_lat_rows = nb * lat_len
    n_rows = n_lat_rows + nb * ctx_len

    xs = jnp.concatenate([x.reshape(n_lat_rows, d), ctx.reshape(nb * ctx_len, d)], axis=0)
    s_in = jnp.zeros((8, d), F32).at[:nb].set(c).at[nb].set(c_ctx)
    mod_all = _ada(s_in, w_ada, b_ada)[:, :nb + 1].reshape(depth, nb + 1, 6, d)

    cosf, sinf = _rotary_tables(lat_len, ctx_len)
    hg_mats, hg_masks = _hgrn_constants()
    hg_mats = jnp.asarray(hg_mats, BF16)
    hg_masks = jnp.asarray(hg_masks, F32)
    sm = jax.nn.softmax(hgrn_lb.astype(F32), axis=0)
    lb_all = jnp.clip(jnp.cumsum(sm, axis=0) - sm[0], 0.0, 1.0)

    for layer in range(depth):
        last = layer == depth - 1
        modtab = mod_all[layer]
        g = norm_g[layer]

        hb = _prenorm(xs, n_rows, g[0], modtab, 1, 0, lat_len, BF16)
        w_in_l = w_in[layer]
        p = _mm(hb, _bf(w_in_l[:, :PROJ_MAIN]), MM_TM, MM_TN)
        w_g = jnp.zeros((d, 128), BF16).at[:, :MLSTM_GATES].set(_bf(w_in_l[:, PROJ_MAIN:]))
        gates = _mm(hb, w_g, MM_TM, 128)[:, :MLSTM_GATES]
        g_row_lat = jnp.swapaxes(gates[:n_lat_rows].reshape(nb, lat_len, MLSTM_GATES), 1, 2)
        g_row_ctx = jnp.swapaxes(gates[n_lat_rows:].reshape(nb, ctx_len, MLSTM_GATES), 1, 2)

        log_g = jax.nn.log_sigmoid(ret_decay[layer].astype(F32))
        lg_b = jnp.broadcast_to(log_g[:, :, None, None], (2, RET_HEADS, 8, RET_DV))
        lb = lb_all[layer]
        emit_ctx = not last
        o_ret = _retention(p, nb, lat_len, ctx_len, cosf, sinf, lg_b, ret_gn[layer], emit_ctx)
        o_hg = _hgrn(p, nb, lat_len, ctx_len, jnp.log(lb), jnp.log1p(-lb), hgrn_gn[layer],
                     hg_mats, hg_masks, emit_ctx)
        o_ml = _mlstm(p, gates, g_row_lat, g_row_ctx, nb, lat_len, ctx_len, mlstm_gate_b[layer],
                      mlstm_conv[layer], mlstm_gn[layer], emit_ctx)
        if last:
            rows_now = n_lat_rows
            mix = [o_ret[0], o_hg[0], o_ml[0]]
        else:
            rows_now = n_rows
            mix = [jnp.concatenate(o, axis=0) for o in (o_ret, o_hg, o_ml)]
        xs = _wout(mix[0], mix[1], mix[2], _bf(w_out[layer]), xs, rows_now, g[1], modtab, lat_len)

        j = layer // 2
        if layer % 2 == 0:
            hb2 = _prenorm(xs, rows_now, g[2], modtab, 4, 3, lat_len, BF16)
            xs = _ffn(hb2, _bf(w_ffn1[j]), _bf(w_ffn3[j]), _bf(w_ffn2[j]), xs, g[3], modtab, lat_len)
        else:
            xs = _moe(xs, rows_now, g[2], g[3], modtab, w_router[j], _bf(w_exp1[j]), _bf(w_exp3[j]),
                      _bf(w_exp2[j]), lat_len)
    return xs[:n_lat_rows].reshape(nb, lat_len, d)
```

```python
import functools

import numpy as np
import jax
import jax.numpy as jnp
from jax import lax
from jax.experimental import pallas as pl
from jax.experimental.pallas import tpu as pltpu

F32 = jnp.float32
BF16 = jnp.bfloat16

CHUNK = 128
NORM_EPS = 1e-6
ROPE_BASE = 10000.0
GRID_W = 64

RET_HEADS, RET_DK, RET_DV = 4, 128, 256
HGRN_HEADS, HGRN_DK, HGRN_DV = 4, 128, 128
MLSTM_HEADS, MLSTM_DK, MLSTM_DV = 4, 128, 128
N_EXPERTS = 8
TOP_K = 2

RET_QK = RET_HEADS * RET_DK
RET_WIDTH = RET_HEADS * RET_DV
HGRN_QK = HGRN_HEADS * HGRN_DK
HGRN_WIDTH = HGRN_HEADS * HGRN_DV
MLSTM_QK = MLSTM_HEADS * MLSTM_DK
MLSTM_WIDTH = MLSTM_HEADS * MLSTM_DV
MLSTM_GATES = 4 * MLSTM_HEADS

_C_RET_Q = 0
_C_RET_K = _C_RET_Q + RET_QK // 128
_C_RET_V = _C_RET_K + RET_QK // 128
_C_RET_G = _C_RET_V + RET_WIDTH // 128
_C_HG_Q = _C_RET_G + RET_WIDTH // 128
_C_HG_FF = _C_HG_Q + HGRN_QK // 128
_C_HG_FB = _C_HG_FF + HGRN_QK // 128
_C_HG_I = _C_HG_FB + HGRN_QK // 128
_C_HG_G = _C_HG_I + HGRN_WIDTH // 128
_C_ML_Q = _C_HG_G + HGRN_WIDTH // 128
_C_ML_K = _C_ML_Q + MLSTM_QK // 128
_C_ML_V = _C_ML_K + MLSTM_QK // 128
_C_ML_O = _C_ML_V + MLSTM_WIDTH // 128
PROJ_MAIN = (_C_ML_O + MLSTM_WIDTH // 128) * 128

_VMEM_CAP_BYTES = 56 * 1024 * 1024

ROW_TILE = 512
MM_TM, MM_TN = 1024, 512
FFN_TF = 512
MOE_TM = 512
SCAN_UNROLL = 2


def _cparams(sem, vmem_mb):
    return pltpu.CompilerParams(dimension_semantics=sem,
                                vmem_limit_bytes=min(int(vmem_mb * 1024 * 1024), _VMEM_CAP_BYTES))


def _bf(x):
    return x.astype(BF16)


def _dot(a, b):
    return jnp.dot(a, b, preferred_element_type=F32)


def _dot_nt(a, b):
    return lax.dot_general(a, b, (((1,), (1,)), ((), ())), preferred_element_type=F32)


def _dot_tn(a, b):
    return lax.dot_general(a, b, (((0,), (0,)), ((), ())), preferred_element_type=F32)


def _sigmoid(x):
    return 1.0 / (1.0 + jnp.exp(-x))


def _silu(x):
    return x * _sigmoid(x)


def _log_sigmoid(x):
    return jnp.minimum(x, 0.0) - jnp.log1p(jnp.exp(-jnp.abs(x)))


def _rms(x, g):
    return x * lax.rsqrt(jnp.mean(x * x, axis=-1, keepdims=True) + NORM_EPS) * g


def _mod_index(tile_rows, n_lat_rows_per_batch, n_batch):
    return lambda i: jnp.minimum((i * tile_rows) // n_lat_rows_per_batch, n_batch)


def _ada_kernel(s_ref, w_ref, b_ref, o_ref):
    s = _bf(_silu(s_ref[...]))
    o_ref[0] = _dot(s, _bf(w_ref[0])) + b_ref[0]


def _ada(s_in, w_ada, b_ada):
    depth, d, n = w_ada.shape
    tn = 1024
    return pl.pallas_call(
        _ada_kernel,
        grid=(depth, n // tn),
        in_specs=[pl.BlockSpec((8, d), lambda l, j: (0, 0)),
                  pl.BlockSpec((1, d, tn), lambda l, j: (l, 0, j)),
                  pl.BlockSpec((1, 1, tn), lambda l, j: (l, 0, j))],
        out_specs=pl.BlockSpec((1, 8, tn), lambda l, j: (l, 0, j)),
        out_shape=jax.ShapeDtypeStruct((depth, 8, n), F32),
        compiler_params=_cparams(("arbitrary", "arbitrary"), 40),
        name="ada",
    )(s_in, w_ada, b_ada.reshape(depth, 1, n))


def _prenorm_kernel(x_ref, g_ref, mod_ref, o_ref, *, sc, sh):
    m = mod_ref[0]
    y = _rms(x_ref[...], g_ref[...])
    o_ref[...] = (y * (1.0 + m[sc:sc + 1]) + m[sh:sh + 1]).astype(o_ref.dtype)


def _prenorm(xs, n_rows, g, modtab, sc, sh, lat_rows, out_dtype):
    d = xs.shape[1]
    nb = modtab.shape[0] - 1
    mi = _mod_index(ROW_TILE, lat_rows, nb)
    return pl.pallas_call(
        functools.partial(_prenorm_kernel, sc=sc, sh=sh),
        grid=(n_rows // ROW_TILE,),
        in_specs=[pl.BlockSpec((ROW_TILE, d), lambda i: (i, 0)),
                  pl.BlockSpec((1, d), lambda i: (0, 0)),
                  pl.BlockSpec((1, 6, d), lambda i: (mi(i), 0, 0))],
        out_specs=pl.BlockSpec((ROW_TILE, d), lambda i: (i, 0)),
        out_shape=jax.ShapeDtypeStruct((n_rows, d), out_dtype),
        compiler_params=_cparams(("arbitrary",), 32),
        name="prenorm",
    )(xs, g.reshape(1, d), modtab)


def _mm_kernel(x_ref, w_ref, o_ref):
    o_ref[...] = _dot(x_ref[...], w_ref[...]).astype(o_ref.dtype)


def _mm(x, w, tm, tn, out_dtype=F32):
    m, k = x.shape
    n = w.shape[1]
    return pl.pallas_call(
        _mm_kernel,
        grid=(m // tm, n // tn),
        in_specs=[pl.BlockSpec((tm, k), lambda i, j: (i, 0)),
                  pl.BlockSpec((k, tn), lambda i, j: (0, j))],
        out_specs=pl.BlockSpec((tm, tn), lambda i, j: (i, j)),
        out_shape=jax.ShapeDtypeStruct((m, n), out_dtype),
        compiler_params=_cparams(("arbitrary", "arbitrary"), 40),
        name="proj",
    )(x, w)


def _bwd_chunk(i, n_ctx, n_lat):
    return jnp.where(i < n_ctx, n_ctx - 1 - i, 2 * n_ctx + n_lat - 1 - i)


def _rows(c):
    return pl.ds(pl.multiple_of(c * CHUNK, CHUNK), CHUNK)


def _for_chunks(n_ctx, n_lat, fn, with_ctx=True):
    def run(part, n, off):
        def body(c, carry):
            fn(part, c, c + off)
            return carry
        lax.fori_loop(0, n, body, 0, unroll=SCAN_UNROLL if n % SCAN_UNROLL == 0 else 1)
    if with_ctx:
        run(0, n_ctx, 0)
    run(1, n_lat, n_ctx)


def _lanes(d, w):
    return slice(d * w, (d + 1) * w)


def _mixer_call(kernel_fn, name, in_specs, args, nb, heads, lat_len, ctx_len, dv, scratch, emit_ctx, vmem_mb):
    width = heads * dv
    out_specs = [pl.BlockSpec((lat_len, dv), lambda b, h: (b, h))]
    out_shape = [jax.ShapeDtypeStruct((nb * lat_len, width), BF16)]
    if emit_ctx:
        out_specs.append(pl.BlockSpec((ctx_len, dv), lambda b, h: (b, h)))
        out_shape.append(jax.ShapeDtypeStruct((nb * ctx_len, width), BF16))
    n_in = len(in_specs)

    def body(*refs):
        ins, rest = refs[:n_in], refs[n_in:]
        if emit_ctx:
            ol, oc, scr = rest[0], rest[1], rest[2:]
        else:
            ol, oc, scr = rest[0], None, rest[1:]
        kernel_fn(*ins, ol, oc, *scr, n_ctx=ctx_len // CHUNK, n_lat=lat_len // CHUNK)

    return pl.pallas_call(
        body,
        grid=(nb, heads),
        in_specs=in_specs,
        out_specs=out_specs,
        out_shape=out_shape,
        scratch_shapes=scratch,
        compiler_params=_cparams(("arbitrary", "arbitrary"), vmem_mb),
        name=name,
    )(*args)


def _seq_specs(col0, width, nb, lat_len, ctx_len):
    n_lat_blk = nb * lat_len // ctx_len
    c0 = col0 * 128 // width
    return (pl.BlockSpec((lat_len, width), lambda b, h: (b, c0 + h)),
            pl.BlockSpec((ctx_len, width), lambda b, h: (n_lat_blk + b, c0 + h)))


def _ret_kernel(ql, qc, kl, kc, vl, vc, gl, gc, cos_ref, sin_ref, lg_ref, gn_ref, ol_ref, oc_ref,
                qs_s, kv_s, sall_s, o_s, st_s, dm_s, dq_s, dk_s, *, n_ctx, n_lat):
    nch = n_ctx + n_lat
    qr, kr, vr, gr, outr = (qc, ql), (kc, kl), (vc, vl), (gc, gl), (oc_ref, ol_ref)
    t_i = lax.broadcasted_iota(jnp.int32, (CHUNK, CHUNK), 0).astype(F32)
    s_i = lax.broadcasted_iota(jnp.int32, (CHUNK, CHUNK), 1).astype(F32)
    dm = None
    for d in range(2):
        lg = lg_ref[d, 0][0:1, 0:CHUNK]
        rel = (t_i - s_i) if d == 0 else (s_i - t_i)
        dmd = jnp.where(rel >= 0, jnp.exp(jnp.maximum(rel, 0.0) * lg), 0.0)
        dm = dmd if dm is None else dm + dmd
        p = t_i if d == 0 else (CHUNK - 1.0) - t_i
        dq_s[d] = jnp.exp((p + 1.0) * lg)
        dk_s[d] = jnp.exp((CHUNK - 1.0 - p) * lg)
    dm_s[...] = dm
    scale = RET_DK ** -0.5

    def phase_a(part, cl, cg):
        rl, rg = _rows(cl), _rows(cg)
        cs = cos_ref[rg, :]
        sn = sin_ref[rg, :]
        q = qr[part][rl, :]
        k = kr[part][rl, :]
        q = q * cs + pltpu.roll(q, RET_DK // 2, 1) * sn
        k = (k * cs + pltpu.roll(k, RET_DK // 2, 1) * sn) * scale
        vb = _bf(vr[part][rl, :])
        s = _dot_nt(_bf(q), _bf(k))
        o_s[rg, :] = _dot(_bf(s * dm_s[...]), vb)
        qs_s[rg, :] = jnp.concatenate([_bf(q * dq_s[0]), _bf(q * dq_s[1])], axis=1)
        kd = jnp.concatenate([_bf(k * dk_s[0]), _bf(k * dk_s[1])], axis=1)
        kv_s[cg] = _dot_tn(kd, vb)

    _for_chunks(n_ctx, n_lat, phase_a)

    st_s[...] = jnp.zeros(st_s.shape, F32)
    dec = [jnp.exp(float(CHUNK) * lg_ref[d, 0][0:1, :]) for d in range(2)]

    def phase_b(i, carry):
        for d, c in ((0, i), (1, _bwd_chunk(i, n_ctx, n_lat))):
            st = st_s[d]
            sall_s[c, _lanes(d, RET_DK), :] = _bf(st)
            st_s[d] = dec[d] * st + kv_s[c, _lanes(d, RET_DK), :]
        return carry

    lax.fori_loop(0, nch, phase_b, 0)
    gn = gn_ref[...]

    def phase_c(part, cl, cg):
        rl, rg = _rows(cl), _rows(cg)
        o = o_s[rg, :] + _dot(qs_s[rg, :], sall_s[cg])
        outr[part][rl, :] = (_rms(o, gn) * _silu(gr[part][rl, :])).astype(BF16)

    _for_chunks(n_ctx, n_lat, phase_c, with_ctx=oc_ref is not None)


def _retention(p, nb, lat_len, ctx_len, cosf, sinf, lg_b, gn, emit_ctx):
    t_len = lat_len + ctx_len
    nch = t_len // CHUNK
    s128 = functools.partial(_seq_specs, width=128, nb=nb, lat_len=lat_len, ctx_len=ctx_len)
    s256 = functools.partial(_seq_specs, width=256, nb=nb, lat_len=lat_len, ctx_len=ctx_len)
    in_specs = [*s128(_C_RET_Q), *s128(_C_RET_K), *s256(_C_RET_V), *s256(_C_RET_G),
                pl.BlockSpec((t_len, 128), lambda b, h: (0, 0)),
                pl.BlockSpec((t_len, 128), lambda b, h: (0, 0)),
                pl.BlockSpec((2, 1, 8, RET_DV), lambda b, h: (0, h, 0, 0)),
                pl.BlockSpec((1, RET_DV), lambda b, h: (0, h))]
    scratch = [pltpu.VMEM((t_len, 2 * RET_DK), BF16),
               pltpu.VMEM((nch, 2 * RET_DK, RET_DV), F32),
               pltpu.VMEM((nch, 2 * RET_DK, RET_DV), BF16),
               pltpu.VMEM((t_len, RET_DV), F32),
               pltpu.VMEM((2, RET_DK, RET_DV), F32),
               pltpu.VMEM((CHUNK, CHUNK), F32),
               pltpu.VMEM((2, CHUNK, CHUNK), F32), pltpu.VMEM((2, CHUNK, CHUNK), F32)]
    args = (p, p, p, p, p, p, p, p, cosf, sinf, lg_b, gn.reshape(1, RET_WIDTH))
    return _mixer_call(_ret_kernel, "retention", in_specs, args, nb, RET_HEADS, lat_len, ctx_len, RET_DV,
                       scratch, emit_ctx, 48)


_HG_LEVELS = (64, 32, 16, 8, 4, 2, 1)


def _hgrn_constants():
    c = CHUNK
    t = np.arange(c)[:, None]
    u = np.arange(c)[None, :]
    cum = (u <= t).astype(np.float32)
    masks, signs = [], []
    for m in _HG_LEVELS:
        base = (t // (2 * m)) * (2 * m)
        lower = t >= base + m
        tb = t // (2 * m)
        sb = u // (2 * m)
        masks.append((tb == sb) & lower & (u < (sb * 2 * m + m)))
        signs.append(np.broadcast_to(np.where(lower, 1.0, -1.0), (c, c)))
    masks.append(t == u)
    kf = np.stack([x.astype(np.float32) for x in masks], axis=0)
    kb = np.stack([x.astype(np.float32)[::-1, ::-1] for x in masks], axis=0)
    sf = np.stack([x.astype(np.float32) for x in signs], axis=0)
    sb_ = np.stack([x.astype(np.float32)[::-1, ::-1] for x in signs], axis=0)
    return (np.stack([cum, cum[::-1, ::-1]], 0), np.stack([kf, kb], 0),
            np.stack([sf[:_HG_WIDE], sb_[:_HG_WIDE]], 0))


_HG_WIDE = 5


def _hgrn_level_exponents(b, lf, d, sgn_ref, row):
    out = []
    for l, m in enumerate(_HG_LEVELS[:_HG_WIDE]):
        pieces = []
        for j in range(CHUNK // (2 * m)):
            r = 2 * m * j + (m - 1 if d == 0 else m)
            pieces.append(jnp.broadcast_to(b[r:r + 1, :], (2 * m, CHUNK)))
        bref = pieces[0] if len(pieces) == 1 else jnp.concatenate(pieces, axis=0)
        out.append((b - bref) * sgn_ref[d, l])
    up = pltpu.roll(lf, CHUNK - 1, 0)
    dn = pltpu.roll(lf, 1, 0)
    r4 = row % 4
    if d == 0:
        e2 = jnp.where(r4 == 0, up, jnp.where(r4 == 1, 0.0, jnp.where(r4 == 2, lf, lf + dn)))
        e1 = jnp.where(row % 2 == 1, lf, 0.0)
    else:
        e2 = jnp.where(r4 == 0, lf + up, jnp.where(r4 == 1, lf, jnp.where(r4 == 2, 0.0, dn)))
        e1 = jnp.where(row % 2 == 0, lf, 0.0)
    return out + [e2, e1]


def _hgrn_kernel(ql, qc, ffl, ffc, fbl, fbc, il, ic, gl, gc, llb_ref, lub_ref, gn_ref, c_ref, k_ref, sgn_ref,
                 ol_ref, oc_ref, qs_s, kv_s, sall_s, dec_s, o_s, st_s, *, n_ctx, n_lat):
    nch = n_ctx + n_lat
    qr, fr, ir, gr, outr = (qc, ql), ((ffc, ffl), (fbc, fbl)), (ic, il), (gc, gl), (oc_ref, ol_ref)
    llb = llb_ref[...]
    lub = lub_ref[...]
    nlev = len(_HG_LEVELS)
    row = lax.broadcasted_iota(jnp.int32, (CHUNK, HGRN_DK), 0)

    def phase_a(part, cl, cg):
        rl, rg = _rows(cl), _rows(cg)
        q = _silu(qr[part][rl, :])
        qb = _bf(q)
        vb = _bf(ir[part][rl, :])
        a_sum, qs, kds = None, [], []
        for d in range(2):
            lsg = lub + _log_sigmoid(fr[d][part][rl, :])
            lf = jnp.maximum(llb, lsg) + jnp.log1p(jnp.exp(-jnp.abs(llb - lsg)))
            k = 1.0 - jnp.exp(lf)
            lf_hi = _bf(lf)
            lf_lo = _bf(lf - lf_hi.astype(F32))
            b2 = _dot(c_ref[d], jnp.concatenate([lf_hi, lf_lo], axis=1))
            b = b2[:, 0:HGRN_DK] + b2[:, HGRN_DK:]
            es = _hgrn_level_exponents(b, lf, d, sgn_ref, row)
            a = k_ref[d, nlev] * _dot_nt(qb, _bf(k))
            for l in range(nlev):
                x = jnp.exp(es[l])
                a = a + k_ref[d, l] * _dot_nt(_bf(q * x), _bf(k * x))
            a_sum = a if a_sum is None else a_sum + a
            b_last = b[CHUNK - 1:CHUNK] if d == 0 else b[0:1]
            qs.append(_bf(q * jnp.exp(b)))
            kds.append(_bf(k * jnp.exp(b_last - b)))
            dec_s[cg, :, _lanes(d, HGRN_DK)] = jnp.exp(b_last)
        o_s[rg, :] = _dot(_bf(a_sum), vb)
        qs_s[rg, :] = jnp.concatenate(qs, axis=1)
        kv_s[cg] = _dot_tn(vb, jnp.concatenate(kds, axis=1))

    _for_chunks(n_ctx, n_lat, phase_a)
    st_s[...] = jnp.zeros(st_s.shape, F32)

    def phase_b(i, carry):
        for d, c in ((0, i), (1, _bwd_chunk(i, n_ctx, n_lat))):
            ln = _lanes(d, HGRN_DK)
            st = st_s[d]
            sall_s[c, :, ln] = _bf(st)
            st_s[d] = dec_s[c, :, ln] * st + kv_s[c, :, ln]
        return carry

    lax.fori_loop(0, nch, phase_b, 0)
    gn = gn_ref[...]

    def phase_c(part, cl, cg):
        rl, rg = _rows(cl), _rows(cg)
        o = o_s[rg, :] + _dot_nt(qs_s[rg, :], sall_s[cg])
        outr[part][rl, :] = (_rms(o, gn) * _silu(gr[part][rl, :])).astype(BF16)

    _for_chunks(n_ctx, n_lat, phase_c, with_ctx=oc_ref is not None)


def _hgrn(p, nb, lat_len, ctx_len, log_lb, log_ub, gn, consts, emit_ctx):
    mats, masks, signs = consts
    t_len = lat_len + ctx_len
    nch = t_len // CHUNK
    spec = functools.partial(_seq_specs, width=128, nb=nb, lat_len=lat_len, ctx_len=ctx_len)
    vec = pl.BlockSpec((1, 128), lambda b, h: (0, h))
    in_specs = [*spec(_C_HG_Q), *spec(_C_HG_FF), *spec(_C_HG_FB), *spec(_C_HG_I), *spec(_C_HG_G),
                vec, vec, vec,
                pl.BlockSpec(mats.shape, lambda b, h: (0, 0, 0)),
                pl.BlockSpec(masks.shape, lambda b, h: (0, 0, 0, 0)),
                pl.BlockSpec(signs.shape, lambda b, h: (0, 0, 0, 0))]
    scratch = [pltpu.VMEM((t_len, 2 * HGRN_DK), BF16),
               pltpu.VMEM((nch, HGRN_DV, 2 * HGRN_DK), F32),
               pltpu.VMEM((nch, HGRN_DV, 2 * HGRN_DK), BF16),
               pltpu.VMEM((nch, 1, 2 * HGRN_DK), F32),
               pltpu.VMEM((t_len, HGRN_DV), F32),
               pltpu.VMEM((2, HGRN_DV, HGRN_DK), F32)]
    args = (p, p, p, p, p, p, p, p, p, p, log_lb.reshape(1, HGRN_QK), log_ub.reshape(1, HGRN_QK),
            gn.reshape(1, HGRN_WIDTH), mats, masks, signs)
    return _mixer_call(_hgrn_kernel, "hgrn2", in_specs, args, nb, HGRN_HEADS, lat_len, ctx_len, HGRN_DV,
                       scratch, emit_ctx, 48)


def _mlstm_kernel(ql, qc, kl, kc, vl, vc, ogl, ogc, gcl, gcc, grl, grc, bc_ref, br_ref,
                  wq_ref, wk_ref, gn_ref, ol_ref, oc_ref,
                  q_s, gr_s, intra_s, kv_s, call_s, rho_s, bcol_s, bl_s, mu_s, mprev_s, st_s, m_s,
                  *, n_ctx, n_lat):
    nch = n_ctx + n_lat
    h = pl.program_id(1)
    nh = MLSTM_HEADS
    dk, dv = MLSTM_DK, MLSTM_DV
    ext = 2 * dv
    qr, kr, vr, ogr, gcr, outr = (qc, ql), (kc, kl), (vc, vl), (ogc, ogl), (gcc, gcl), (oc_ref, ol_ref)
    n_loc = (n_ctx, n_lat)

    for cc in range(nch):
        src, c0 = (grc, cc) if cc < n_ctx else (grl, cc - n_ctx)
        gr_s[cc] = src[0, :, c0 * CHUNK:(c0 + 1) * CHUNK] + br_ref[...]

    row = lax.broadcasted_iota(jnp.int32, (CHUNK, dk), 0)
    lane = lax.broadcasted_iota(jnp.int32, (CHUNK, dv), 1)
    ones_col = jnp.where(lane == 0, 1.0, 0.0).astype(BF16)
    scale = dk ** -0.5
    t_i = lax.broadcasted_iota(jnp.int32, (CHUNK, CHUNK), 0)
    s_i = lax.broadcasted_iota(jnp.int32, (CHUNK, CHUNK), 1)
    lane16 = lax.broadcasted_iota(jnp.int32, (CHUNK, MLSTM_GATES), 1)
    sub16 = lax.broadcasted_iota(jnp.int32, (MLSTM_GATES, CHUNK), 0)

    def conv(src, w_ref, part, cl):
        r0 = pl.multiple_of(cl * CHUNK, CHUNK)
        n_rows = n_loc[part] * CHUNK
        x = src[pl.ds(r0, CHUNK), :]
        pr = src[pl.ds(jnp.maximum(r0 - 1, 0), 1), :]
        nx = src[pl.ds(jnp.minimum(r0 + CHUNK, n_rows - 1), 1), :]
        pr = jnp.where(cl != 0, pr, 0.0)
        nx = jnp.where(cl != n_loc[part] - 1, nx, 0.0)
        xp = jnp.where(row == 0, pr, pltpu.roll(x, 1, 0))
        xn = jnp.where(row == CHUNK - 1, nx, pltpu.roll(x, CHUNK - 1, 0))
        w = w_ref[...]
        return _silu(w[0:1] * xp + w[1:2] * x + w[2:3] * xn)

    def pick_col(g, j):
        return jnp.sum(jnp.where(lane16 == j, g, 0.0), axis=1, keepdims=True)

    def pick_row(g, j):
        return jnp.sum(jnp.where(sub16 == j, g, 0.0), axis=0, keepdims=True)

    def phase_a(part, cl, cg):
        rl, rg = _rows(cl), _rows(cg)
        q = conv(qr[part], wq_ref, part, cl)
        k = conv(kr[part], wk_ref, part, cl) * scale
        qb = _bf(q)
        q_s[rg, :] = qb
        vb = jnp.concatenate([_bf(vr[part][rl, :]), ones_col], axis=1)
        s = _dot_nt(qb, _bf(k))
        g_c = gcr[part][rl, :] + bc_ref[...]
        g_r = gr_s[cg]
        kws = []
        for d in range(2):
            ig_c = pick_col(g_c, d * nh + h)
            lf_c = _log_sigmoid(pick_col(g_c, 2 * nh + d * nh + h))
            ig_r = pick_row(g_r, d * nh + h)
            lf_r = _log_sigmoid(pick_row(g_r, 2 * nh + d * nh + h))
            tri = (s_i <= t_i) if d == 0 else (s_i >= t_i)
            tri_t = (t_i <= s_i) if d == 0 else (t_i >= s_i)
            b_c = jnp.sum(jnp.where(tri, lf_r, 0.0), axis=1, keepdims=True)
            b_r = jnp.sum(jnp.where(tri_t, lf_c, 0.0), axis=0, keepdims=True)
            dmat = jnp.where(tri, b_c - b_r + ig_r, -jnp.inf)
            rho = jnp.max(dmat, axis=1, keepdims=True)
            intra_s[d, rg, :] = _dot(_bf(s * jnp.exp(dmat - rho)), vb)
            b_last = b_c[CHUNK - 1:CHUNK] if d == 0 else b_c[0:1]
            mu = jnp.max(b_last - b_r + ig_r, axis=1, keepdims=True)
            kws.append(_bf(k * jnp.exp(b_last - b_c + ig_c - mu)))
            rho_s[d, rg, :] = rho
            bcol_s[d, rg, :] = b_c
            bl_s[cg, :, _lanes(d, 128)] = jnp.broadcast_to(b_last, (1, 128))
            mu_s[cg, :, _lanes(d, 128)] = jnp.broadcast_to(mu, (1, 128))
        kv_s[cg] = _dot_tn(jnp.concatenate(kws, axis=1), vb)

    _for_chunks(n_ctx, n_lat, phase_a)
    st_s[...] = jnp.zeros(st_s.shape, F32)
    m_s[...] = jnp.zeros(m_s.shape, F32)

    def wide(v):
        return jnp.concatenate([v, v], axis=1)

    def phase_b(i, carry):
        for d, c in ((0, i), (1, _bwd_chunk(i, n_ctx, n_lat))):
            ln = _lanes(d, 128)
            m_prev = m_s[d]
            st = st_s[d]
            mprev_s[c, :, ln] = m_prev
            call_s[c, :, _lanes(d, ext)] = _bf(st)
            bl = bl_s[c, :, ln]
            mu = mu_s[c, :, ln]
            m_new = jnp.maximum(bl + m_prev, mu)
            st_s[d] = (wide(jnp.exp(bl + m_prev - m_new)) * st
                       + wide(jnp.exp(mu - m_new)) * kv_s[c, _lanes(d, dk), :])
            m_s[d] = m_new
        return carry

    lax.fori_loop(0, nch, phase_b, 0)
    gn = gn_ref[...]

    def phase_c(part, cl, cg):
        rl, rg = _rows(cl), _rows(cg)
        qc_all = _dot(q_s[rg, :], call_s[cg])
        o = None
        for d in range(2):
            rho = rho_s[d, rg, :]
            b_c = bcol_s[d, rg, :]
            m_prev = mprev_s[cg, :, _lanes(d, 128)][:, 0:1]
            m_t = jnp.maximum(rho, b_c + m_prev)
            nd = (jnp.exp(rho - m_t) * intra_s[d, rg, :]
                  + jnp.exp(b_c + m_prev - m_t) * qc_all[:, _lanes(d, ext)])
            hh = nd[:, 0:dv] / jnp.maximum(jnp.abs(nd[:, dv:dv + 1]), jnp.exp(-m_t))
            o = hh if o is None else o + hh
        y = o - jnp.mean(o, axis=-1, keepdims=True)
        y = y * lax.rsqrt(jnp.mean(y * y, axis=-1, keepdims=True) + NORM_EPS)
        outr[part][rl, :] = (y * gn * _sigmoid(ogr[part][rl, :])).astype(BF16)

    _for_chunks(n_ctx, n_lat, phase_c, with_ctx=oc_ref is not None)


def _mlstm(p, g_col, g_row_lat, g_row_ctx, nb, lat_len, ctx_len, gate_b, conv_w, gn, emit_ctx):
    t_len = lat_len + ctx_len
    nch = t_len // CHUNK
    n_lat_blk = nb * lat_len // ctx_len
    ng = MLSTM_GATES
    dk, dv = MLSTM_DK, MLSTM_DV
    spec = functools.partial(_seq_specs, width=128, nb=nb, lat_len=lat_len, ctx_len=ctx_len)
    in_specs = [*spec(_C_ML_Q), *spec(_C_ML_K), *spec(_C_ML_V), *spec(_C_ML_O),
                pl.BlockSpec((lat_len, ng), lambda b, h: (b, 0)),
                pl.BlockSpec((ctx_len, ng), lambda b, h: (n_lat_blk + b, 0)),
                pl.BlockSpec((1, ng, lat_len), lambda b, h: (b, 0, 0)),
                pl.BlockSpec((1, ng, ctx_len), lambda b, h: (b, 0, 0)),
                pl.BlockSpec((1, ng), lambda b, h: (0, 0)),
                pl.BlockSpec((ng, 1), lambda b, h: (0, 0)),
                pl.BlockSpec((3, 128), lambda b, h: (0, h)),
                pl.BlockSpec((3, 128), lambda b, h: (0, MLSTM_HEADS + h)),
                pl.BlockSpec((1, 128), lambda b, h: (0, h))]
    scratch = [pltpu.VMEM((t_len, dk), BF16),
               pltpu.VMEM((nch, ng, CHUNK), F32),
               pltpu.VMEM((2, t_len, 2 * dv), F32),
               pltpu.VMEM((nch, 2 * dk, 2 * dv), F32),
               pltpu.VMEM((nch, dk, 4 * dv), BF16),
               pltpu.VMEM((2, t_len, 1), F32), pltpu.VMEM((2, t_len, 1), F32),
               pltpu.VMEM((nch, 1, 256), F32), pltpu.VMEM((nch, 1, 256), F32), pltpu.VMEM((nch, 1, 256), F32),
               pltpu.VMEM((2, dk, 2 * dv), F32), pltpu.VMEM((2, 1, 128), F32)]
    args = (p, p, p, p, p, p, p, p, g_col, g_col, g_row_lat, g_row_ctx,
            gate_b.reshape(1, ng), gate_b.reshape(ng, 1), conv_w, conv_w, gn.reshape(1, MLSTM_WIDTH))
    return _mixer_call(_mlstm_kernel, "mlstm", in_specs, args, nb, MLSTM_HEADS, lat_len, ctx_len, dv,
                       scratch, emit_ctx, 48)


def _wout_kernel(mr_ref, mh_ref, mm_ref, wr_ref, wh_ref, wm_ref, x_ref, g_ref, mod_ref, o_ref):
    y = _dot(mr_ref[...], wr_ref[...]) + _dot(mh_ref[...], wh_ref[...]) + _dot(mm_ref[...], wm_ref[...])
    m = mod_ref[0]
    o_ref[...] = x_ref[...] + m[2:3] * _rms(y, g_ref[...])


def _wout(mix_r, mix_h, mix_m, w_out, xs, n_rows, g1, modtab, lat_rows):
    d = xs.shape[1]
    nb = modtab.shape[0] - 1
    mi = _mod_index(ROW_TILE, lat_rows, nb)
    tm = ROW_TILE
    w_r = w_out[0:RET_WIDTH]
    w_h = w_out[RET_WIDTH:RET_WIDTH + HGRN_WIDTH]
    w_m = w_out[RET_WIDTH + HGRN_WIDTH:]
    full = lambda a: pl.BlockSpec(a.shape, lambda i: (0, 0))
    return pl.pallas_call(
        _wout_kernel,
        grid=(n_rows // tm,),
        in_specs=[pl.BlockSpec((tm, RET_WIDTH), lambda i: (i, 0)),
                  pl.BlockSpec((tm, HGRN_WIDTH), lambda i: (i, 0)),
                  pl.BlockSpec((tm, MLSTM_WIDTH), lambda i: (i, 0)),
                  full(w_r), full(w_h), full(w_m),
                  pl.BlockSpec((tm, d), lambda i: (i, 0)),
                  pl.BlockSpec((1, d), lambda i: (0, 0)),
                  pl.BlockSpec((1, 6, d), lambda i: (mi(i), 0, 0))],
        out_specs=pl.BlockSpec((tm, d), lambda i: (i, 0)),
        out_shape=jax.ShapeDtypeStruct((n_rows, d), F32),
        compiler_params=_cparams(("arbitrary",), 48),
        name="wout",
    )(mix_r, mix_h, mix_m, w_r, w_h, w_m, xs, g1.reshape(1, d), modtab)


def _ffn_kernel(h_ref, w1_ref, w3_ref, w2_ref, x_ref, g_ref, mod_ref, o_ref, acc_ref):
    f = pl.program_id(1)

    @pl.when(f == 0)
    def _():
        acc_ref[...] = jnp.zeros(acc_ref.shape, F32)

    h = h_ref[...]
    u = _silu(_dot(h, w1_ref[...])) * _dot(h, w3_ref[...])
    acc_ref[...] += _dot(_bf(u), w2_ref[...])

    @pl.when(f == pl.num_programs(1) - 1)
    def _():
        m = mod_ref[0]
        o_ref[...] = x_ref[...] + m[5:6] * _rms(acc_ref[...], g_ref[...])


def _ffn(hb, w1, w3, w2, xs, g3, modtab, lat_rows):
    n_rows, d = hb.shape
    dff = w1.shape[1]
    nb = modtab.shape[0] - 1
    tm, tf = ROW_TILE, FFN_TF
    mi = _mod_index(tm, lat_rows, nb)
    return pl.pallas_call(
        _ffn_kernel,
        grid=(n_rows // tm, dff // tf),
        in_specs=[pl.BlockSpec((tm, d), lambda i, f: (i, 0)),
                  pl.BlockSpec((d, tf), lambda i, f: (0, f)),
                  pl.BlockSpec((d, tf), lambda i, f: (0, f)),
                  pl.BlockSpec((tf, d), lambda i, f: (f, 0)),
                  pl.BlockSpec((tm, d), lambda i, f: (i, 0)),
                  pl.BlockSpec((1, d), lambda i, f: (0, 0)),
                  pl.BlockSpec((1, 6, d), lambda i, f: (mi(i), 0, 0))],
        out_specs=pl.BlockSpec((tm, d), lambda i, f: (i, 0)),
        out_shape=jax.ShapeDtypeStruct((n_rows, d), F32),
        scratch_shapes=[pltpu.VMEM((tm, d), F32)],
        compiler_params=_cparams(("arbitrary", "arbitrary"), 48),
        name="ffn",
    )(hb, w1, w3, w2, xs, g3.reshape(1, d), modtab)


_META_E0, _META_E1, _META_R0, _META_R1, _META_G0, _META_G1 = range(6)


def _router_kernel(x_ref, g_ref, mod_ref, wr_ref, tri_ref, meta_ref, cnt_ref, carry_ref):
    i = pl.program_id(0)

    @pl.when(i == 0)
    def _():
        carry_ref[...] = jnp.zeros(carry_ref.shape, F32)

    m = mod_ref[0]
    hmod = _rms(x_ref[...], g_ref[...]) * (1.0 + m[4:5]) + m[3:4]
    logits = jnp.dot(hmod, wr_ref[...], precision=lax.Precision.HIGHEST, preferred_element_type=F32)
    lane = lax.broadcasted_iota(jnp.int32, logits.shape, 1)
    lanef = lane.astype(F32)
    logits = jnp.where(lane < N_EXPERTS, logits, -jnp.inf)
    v0 = jnp.max(logits, axis=1, keepdims=True)
    e0 = jnp.min(jnp.where(logits == v0, lanef, 1e9), axis=1, keepdims=True)
    rest = jnp.where(lanef == e0, -jnp.inf, logits)
    v1 = jnp.max(rest, axis=1, keepdims=True)
    e1 = jnp.min(jnp.where(rest == v1, lanef, 1e9), axis=1, keepdims=True)
    ex = jnp.exp(v1 - v0)
    g0 = 1.0 / (1.0 + ex)
    g1 = ex / (1.0 + ex)
    oh0 = lanef == e0
    oh1 = lanef == e1
    oh = jnp.where(jnp.logical_or(oh0, oh1), 1.0, 0.0)
    before = _dot(tri_ref[...], _bf(oh)) + carry_ref[0:1, :]
    r0 = jnp.sum(jnp.where(oh0, before, 0.0), axis=1, keepdims=True)
    r1 = jnp.sum(jnp.where(oh1, before, 0.0), axis=1, keepdims=True)
    carry_ref[0:1, :] = carry_ref[0:1, :] + jnp.sum(oh, axis=0, keepdims=True)
    meta = jnp.zeros(logits.shape, F32)
    for j, val in ((_META_E0, e0), (_META_E1, e1), (_META_R0, r0), (_META_R1, r1),
                   (_META_G0, g0), (_META_G1, g1)):
        meta = jnp.where(lane == j, val, meta)
    meta_ref[...] = meta
    cnt_ref[...] = carry_ref[...]


def _router(xs, n_rows, g2, modtab, w_router, lat_rows):
    d = xs.shape[1]
    nb = modtab.shape[0] - 1
    tm = ROW_TILE
    mi = _mod_index(tm, lat_rows, nb)
    wr = jnp.zeros((d, 128), F32).at[:, :N_EXPERTS].set(w_router)
    tri = jnp.asarray(np.tril(np.ones((tm, tm), np.float32), -1), BF16)
    return pl.pallas_call(
        _router_kernel,
        grid=(n_rows // tm,),
        in_specs=[pl.BlockSpec((tm, d), lambda i: (i, 0)),
                  pl.BlockSpec((1, d), lambda i: (0, 0)),
                  pl.BlockSpec((1, 6, d), lambda i: (mi(i), 0, 0)),
                  pl.BlockSpec((d, 128), lambda i: (0, 0)),
                  pl.BlockSpec((tm, tm), lambda i: (0, 0))],
        out_specs=[pl.BlockSpec((tm, 128), lambda i: (i, 0)),
                   pl.BlockSpec((8, 128), lambda i: (0, 0))],
        out_shape=[jax.ShapeDtypeStruct((n_rows, 128), F32),
                   jax.ShapeDtypeStruct((8, 128), F32)],
        scratch_shapes=[pltpu.VMEM((8, 128), F32)],
        compiler_params=_cparams(("arbitrary",), 32),
        name="router",
    )(xs, g2.reshape(1, d), modtab, wr, tri)


def _dispatch_kernel(dest_ref, x_ref, g_ref, mod_ref, init_ref, o_ref, h_s, sem):
    del init_ref
    i = pl.program_id(0)
    tm = h_s.shape[0]
    m = mod_ref[0]
    h_s[...] = _rms(x_ref[...], g_ref[...]) * (1.0 + m[4:5]) + m[3:4]

    def row_copy(r, k):
        dst = dest_ref[TOP_K * (i * tm + r) + k]
        return pltpu.make_async_copy(h_s.at[pl.ds(r, 1)], o_ref.at[pl.ds(dst, 1)], sem)

    def start(r, carry):
        for k in range(TOP_K):
            row_copy(r, k).start()
        return carry

    lax.fori_loop(0, tm, start, 0)

    def wait(r, carry):
        for k in range(TOP_K):
            row_copy(r, k).wait()
        return carry

    lax.fori_loop(0, tm, wait, 0)


def _dispatch(dest, xs, n_rows, g2, modtab, n_slots, lat_rows):
    d = xs.shape[1]
    nb = modtab.shape[0] - 1
    tm = ROW_TILE
    mi = _mod_index(tm, lat_rows, nb)
    init = jnp.zeros((n_slots, d), F32)
    grid_spec = pltpu.PrefetchScalarGridSpec(
        num_scalar_prefetch=1,
        grid=(n_rows // tm,),
        in_specs=[pl.BlockSpec((tm, d), lambda i, dst: (i, 0)),
                  pl.BlockSpec((1, d), lambda i, dst: (0, 0)),
                  pl.BlockSpec((1, 6, d), lambda i, dst: (mi(i), 0, 0)),
                  pl.BlockSpec(memory_space=pl.ANY)],
        out_specs=pl.BlockSpec(memory_space=pl.ANY),
        scratch_shapes=[pltpu.VMEM((tm, d), F32), pltpu.SemaphoreType.DMA(())],
    )
    return pl.pallas_call(
        _dispatch_kernel,
        grid_spec=grid_spec,
        out_shape=jax.ShapeDtypeStruct((n_slots, d), F32),
        input_output_aliases={4: 0},
        compiler_params=_cparams(("arbitrary",), 32),
        name="dispatch",
    )(dest, xs, g2.reshape(1, d), modtab, init)


def _expert_kernel(te_ref, nu_ref, x_ref, w1_ref, w3_ref, w2_ref, o_ref, xb_s, acc_s):
    j = pl.program_id(0)
    f = pl.program_id(1)

    @pl.when(j < nu_ref[0])
    def _():
        @pl.when(f == 0)
        def _():
            xb_s[...] = _bf(x_ref[...])
            acc_s[...] = jnp.zeros(acc_s.shape, F32)

        h = xb_s[...]
        u = _silu(_dot(h, w1_ref[0])) * _dot(h, w3_ref[0])
        acc_s[...] += _dot(_bf(u), w2_ref[0])

        @pl.when(f == pl.num_programs(1) - 1)
        def _():
            o_ref[...] = acc_s[...]

    @pl.when(jnp.logical_and(j >= nu_ref[0], f == pl.num_programs(1) - 1))
    def _():
        o_ref[...] = jnp.zeros(o_ref.shape, F32)


def _experts(tile_e, n_used, xsort, w1, w3, w2):
    n_slots, d = xsort.shape
    dff = w1.shape[2]
    tm, tf = MOE_TM, FFN_TF
    nf = dff // tf

    def jj(j, nu):
        return jnp.minimum(j, nu[0] - 1)

    def ff(j, f, nu):
        return jnp.where(j < nu[0], f, nf - 1)

    grid_spec = pltpu.PrefetchScalarGridSpec(
        num_scalar_prefetch=2,
        grid=(n_slots // tm, nf),
        in_specs=[pl.BlockSpec((tm, d), lambda j, f, te, nu: (jj(j, nu), 0)),
                  pl.BlockSpec((1, d, tf), lambda j, f, te, nu: (te[jj(j, nu)], 0, ff(j, f, nu))),
                  pl.BlockSpec((1, d, tf), lambda j, f, te, nu: (te[jj(j, nu)], 0, ff(j, f, nu))),
                  pl.BlockSpec((1, tf, d), lambda j, f, te, nu: (te[jj(j, nu)], ff(j, f, nu), 0))],
        out_specs=pl.BlockSpec((tm, d), lambda j, f, te, nu: (j, 0)),
        scratch_shapes=[pltpu.VMEM((tm, d), BF16), pltpu.VMEM((tm, d), F32)],
    )
    return pl.pallas_call(
        _expert_kernel,
        grid_spec=grid_spec,
        out_shape=jax.ShapeDtypeStruct((n_slots, d), F32),
        compiler_params=_cparams(("arbitrary", "arbitrary"), 48),
        name="experts",
    )(tile_e, n_used, xsort, w1, w3, w2)


def _combine_kernel(dest_ref, y_ref, meta_ref, x_ref, g_ref, mod_ref, o_ref, buf_s, sem):
    i = pl.program_id(0)
    tm = x_ref.shape[0]

    def row_copy(r, k):
        src = dest_ref[TOP_K * (i * tm + r) + k]
        return pltpu.make_async_copy(y_ref.at[pl.ds(src, 1)], buf_s.at[k, pl.ds(r, 1)], sem)

    def start(r, carry):
        for k in range(TOP_K):
            row_copy(r, k).start()
        return carry

    lax.fori_loop(0, tm, start, 0)

    def wait(r, carry):
        for k in range(TOP_K):
            row_copy(r, k).wait()
        return carry

    lax.fori_loop(0, tm, wait, 0)

    meta = meta_ref[...]
    lane = lax.broadcasted_iota(jnp.int32, meta.shape, 1)
    g0 = jnp.sum(jnp.where(lane == _META_G0, meta, 0.0), axis=1, keepdims=True)
    g1 = jnp.sum(jnp.where(lane == _META_G1, meta, 0.0), axis=1, keepdims=True)
    y = buf_s[0] * g0 + buf_s[1] * g1
    m = mod_ref[0]
    o_ref[...] = x_ref[...] + m[5:6] * _rms(y, g_ref[...])


def _combine(dest, yb, meta, xs, n_rows, g3, modtab, lat_rows):
    d = xs.shape[1]
    nb = modtab.shape[0] - 1
    tm = ROW_TILE
    mi = _mod_index(tm, lat_rows, nb)
    grid_spec = pltpu.PrefetchScalarGridSpec(
        num_scalar_prefetch=1,
        grid=(n_rows // tm,),
        in_specs=[pl.BlockSpec(memory_space=pl.ANY),
                  pl.BlockSpec((tm, 128), lambda i, dst: (i, 0)),
                  pl.BlockSpec((tm, d), lambda i, dst: (i, 0)),
                  pl.BlockSpec((1, d), lambda i, dst: (0, 0)),
                  pl.BlockSpec((1, 6, d), lambda i, dst: (mi(i), 0, 0))],
        out_specs=pl.BlockSpec((tm, d), lambda i, dst: (i, 0)),
        scratch_shapes=[pltpu.VMEM((TOP_K, tm, d), F32), pltpu.SemaphoreType.DMA(())],
    )
    return pl.pallas_call(
        _combine_kernel,
        grid_spec=grid_spec,
        out_shape=jax.ShapeDtypeStruct((n_rows, d), F32),
        compiler_params=_cparams(("arbitrary",), 40),
        name="combine",
    )(dest, yb, meta, xs, g3.reshape(1, d), modtab)


def _moe(xs, n_rows, g2, g3, modtab, w_router, w1, w3, w2, lat_rows):
    meta, cnt = _router(xs, n_rows, g2, modtab, w_router, lat_rows)
    counts = cnt[0, :N_EXPERTS].astype(jnp.int32)
    padded = (counts + MOE_TM - 1) // MOE_TM * MOE_TM
    pad_end = jnp.cumsum(padded)
    pad_start = pad_end - padded
    e = meta[:, _META_E0:_META_E1 + 1].astype(jnp.int32)
    r = meta[:, _META_R0:_META_R1 + 1].astype(jnp.int32)
    dest = (pad_start[e] + r).reshape(-1)
    n_tiles = (n_rows * TOP_K) // MOE_TM + N_EXPERTS
    n_slots = n_tiles * MOE_TM
    tile_start = jnp.arange(n_tiles, dtype=jnp.int32) * MOE_TM
    tile_e = jnp.minimum(jnp.sum((tile_start[:, None] >= pad_end[None, :]).astype(jnp.int32), axis=1),
                         N_EXPERTS - 1)
    n_used = (pad_end[-1:] // MOE_TM).astype(jnp.int32)
    xsort = _dispatch(dest, xs, n_rows, g2, modtab, n_slots, lat_rows)
    yb = _experts(tile_e, n_used, xsort, w1, w3, w2)
    return _combine(dest, yb, meta, xs, n_rows, g3, modtab, lat_rows)


def _rotary_tables(lat_len, ctx_len):
    rows = lat_len // GRID_W
    row = jnp.repeat(jnp.arange(rows, dtype=F32), GRID_W)
    col = jnp.tile(jnp.arange(GRID_W, dtype=F32), rows)
    n_freq = RET_DK // 4
    inv = ROPE_BASE ** (-jnp.arange(n_freq, dtype=F32) / n_freq)
    ang = jnp.concatenate([row[:, None] * inv, col[:, None] * inv], axis=-1)
    cos, sin = jnp.cos(ang), jnp.sin(ang)
    cosf = jnp.concatenate([jnp.ones((ctx_len, RET_DK), F32), jnp.concatenate([cos, cos], -1)], 0)
    sinf = jnp.concatenate([jnp.zeros((ctx_len, RET_DK), F32), jnp.concatenate([-sin, sin], -1)], 0)
    return cosf, sinf


def kernel(x, c, ctx, c_ctx, w_ada, b_ada, norm_g, w_in, w_out, ret_decay, ret_gn, hgrn_lb, hgrn_gn,
           mlstm_conv, mlstm_gate_b, mlstm_gn, w_ffn1, w_ffn3, w_ffn2, w_router, w_exp1, w_exp3, w_exp2):
    nb, lat_len, d = x.shape
    ctx_len = ctx.shape[1]
    depth = w_ada.shape[0]
    n_lat_rows = nb * lat_len
    n_rows = n_lat_rows + nb * ctx_len

    xs = jnp.concatenate([x.reshape(n_lat_rows, d), ctx.reshape(nb * ctx_len, d)], axis=0)
    s_in = jnp.zeros((8, d), F32).at[:nb].set(c).at[nb].set(c_ctx)
    mod_all = _ada(s_in, w_ada, b_ada)[:, :nb + 1].reshape(depth, nb + 1, 6, d)

    cosf, sinf = _rotary_tables(lat_len, ctx_len)
    hg_cum, hg_masks, hg_signs = _hgrn_constants()
    hg_consts = (jnp.asarray(hg_cum, BF16), jnp.asarray(hg_masks, F32), jnp.asarray(hg_signs, F32))
    sm = jax.nn.softmax(hgrn_lb.astype(F32), axis=0)
    lb_all = jnp.clip(jnp.cumsum(sm, axis=0) - sm[0], 0.0, 1.0)

    for layer in range(depth):
        last = layer == depth - 1
        modtab = mod_all[layer]
        g = norm_g[layer]

        hb = _prenorm(xs, n_rows, g[0], modtab, 1, 0, lat_len, BF16)
        w_in_l = w_in[layer]
        p = _mm(hb, _bf(w_in_l[:, :PROJ_MAIN]), MM_TM, MM_TN)
        w_g = jnp.zeros((d, 128), BF16).at[:, :MLSTM_GATES].set(_bf(w_in_l[:, PROJ_MAIN:]))
        gates = _mm(hb, w_g, MM_TM, 128)[:, :MLSTM_GATES]
        g_row_lat = jnp.swapaxes(gates[:n_lat_rows].reshape(nb, lat_len, MLSTM_GATES), 1, 2)
        g_row_ctx = jnp.swapaxes(gates[n_lat_rows:].reshape(nb, ctx_len, MLSTM_GATES), 1, 2)

        log_g = jax.nn.log_sigmoid(ret_decay[layer].astype(F32))
        lg_b = jnp.broadcast_to(log_g[:, :, None, None], (2, RET_HEADS, 8, RET_DV))
        lb = lb_all[layer]
        emit_ctx = not last
        o_ret = _retention(p, nb, lat_len, ctx_len, cosf, sinf, lg_b, ret_gn[layer], emit_ctx)
        o_hg = _hgrn(p, nb, lat_len, ctx_len, jnp.log(lb), jnp.log1p(-lb), hgrn_gn[layer],
                     hg_consts, emit_ctx)
        o_ml = _mlstm(p, gates, g_row_lat, g_row_ctx, nb, lat_len, ctx_len, mlstm_gate_b[layer],
                      mlstm_conv[layer], mlstm_gn[layer], emit_ctx)
        if last:
            rows_now = n_lat_rows
            mix = [o_ret[0], o_hg[0], o_ml[0]]
        else:
            rows_now = n_rows
            mix = [jnp.concatenate(o, axis=0) for o in (o_ret, o_hg, o_ml)]
        xs = _wout(mix[0], mix[1], mix[2], _bf(w_out[layer]), xs, rows_now, g[1], modtab, lat_len)

        j = layer // 2
        if layer % 2 == 0:
            hb2 = _prenorm(xs, rows_now, g[2], modtab, 4, 3, lat_len, BF16)
            xs = _ffn(hb2, _bf(w_ffn1[j]), _bf(w_ffn3[j]), _bf(w_ffn2[j]), xs, g[3], modtab, lat_len)
        else:
            xs = _moe(xs, rows_now, g[2], g[3], modtab, w_router[j], _bf(w_exp1[j]), _bf(w_exp3[j]),
                      _bf(w_exp2[j]), lat_len)
    return xs[:n_lat_rows].reshape(nb, lat_len, d)
```

```python
import functools

import numpy as np
import jax
import jax.numpy as jnp
from jax import lax
from jax.experimental import pallas as pl
from jax.experimental.pallas import tpu as pltpu

F32 = jnp.float32
BF16 = jnp.bfloat16

CHUNK = 128
NORM_EPS = 1e-6
ROPE_BASE = 10000.0
GRID_W = 64

RET_HEADS, RET_DK, RET_DV = 4, 128, 256
HGRN_HEADS, HGRN_DK, HGRN_DV = 4, 128, 128
MLSTM_HEADS, MLSTM_DK, MLSTM_DV = 4, 128, 128
N_EXPERTS = 8
TOP_K = 2

RET_QK = RET_HEADS * RET_DK
RET_WIDTH = RET_HEADS * RET_DV
HGRN_QK = HGRN_HEADS * HGRN_DK
HGRN_WIDTH = HGRN_HEADS * HGRN_DV
MLSTM_QK = MLSTM_HEADS * MLSTM_DK
MLSTM_WIDTH = MLSTM_HEADS * MLSTM_DV
MLSTM_GATES = 4 * MLSTM_HEADS

_C_RET_Q = 0
_C_RET_K = _C_RET_Q + RET_QK // 128
_C_RET_V = _C_RET_K + RET_QK // 128
_C_RET_G = _C_RET_V + RET_WIDTH // 128
_C_HG_Q = _C_RET_G + RET_WIDTH // 128
_C_HG_FF = _C_HG_Q + HGRN_QK // 128
_C_HG_FB = _C_HG_FF + HGRN_QK // 128
_C_HG_I = _C_HG_FB + HGRN_QK // 128
_C_HG_G = _C_HG_I + HGRN_WIDTH // 128
_C_ML_Q = _C_HG_G + HGRN_WIDTH // 128
_C_ML_K = _C_ML_Q + MLSTM_QK // 128
_C_ML_V = _C_ML_K + MLSTM_QK // 128
_C_ML_O = _C_ML_V + MLSTM_WIDTH // 128
PROJ_MAIN = (_C_ML_O + MLSTM_WIDTH // 128) * 128

_VMEM_CAP_BYTES = 56 * 1024 * 1024

ROW_TILE = 512
MM_TM, MM_TN = 1024, 512
FFN_TF = 512
MOE_TM = 512
SCAN_UNROLL = 2


def _cparams(sem, vmem_mb):
    return pltpu.CompilerParams(dimension_semantics=sem,
                                vmem_limit_bytes=min(int(vmem_mb * 1024 * 1024), _VMEM_CAP_BYTES))


def _bf(x):
    return x.astype(BF16)


def _dot(a, b):
    return jnp.dot(a, b, preferred_element_type=F32)


def _dot_nt(a, b):
    return lax.dot_general(a, b, (((1,), (1,)), ((), ())), preferred_element_type=F32)


def _dot_tn(a, b):
    return lax.dot_general(a, b, (((0,), (0,)), ((), ())), preferred_element_type=F32)


def _sigmoid(x):
    return 1.0 / (1.0 + jnp.exp(-x))


def _silu(x):
    return x * _sigmoid(x)


def _log_sigmoid(x):
    return jnp.minimum(x, 0.0) - jnp.log1p(jnp.exp(-jnp.abs(x)))


def _rms(x, g):
    return x * lax.rsqrt(jnp.mean(x * x, axis=-1, keepdims=True) + NORM_EPS) * g


def _mod_index(tile_rows, n_lat_rows_per_batch, n_batch):
    return lambda i: jnp.minimum((i * tile_rows) // n_lat_rows_per_batch, n_batch)


def _ada_kernel(s_ref, w_ref, b_ref, o_ref):
    s = _bf(_silu(s_ref[...]))
    o_ref[0] = _dot(s, _bf(w_ref[0])) + b_ref[0]


def _ada(s_in, w_ada, b_ada):
    depth, d, n = w_ada.shape
    tn = 1024
    return pl.pallas_call(
        _ada_kernel,
        grid=(depth, n // tn),
        in_specs=[pl.BlockSpec((8, d), lambda l, j: (0, 0)),
                  pl.BlockSpec((1, d, tn), lambda l, j: (l, 0, j)),
                  pl.BlockSpec((1, 1, tn), lambda l, j: (l, 0, j))],
        out_specs=pl.BlockSpec((1, 8, tn), lambda l, j: (l, 0, j)),
        out_shape=jax.ShapeDtypeStruct((depth, 8, n), F32),
        compiler_params=_cparams(("arbitrary", "arbitrary"), 40),
        name="ada",
    )(s_in, w_ada, b_ada.reshape(depth, 1, n))


def _prenorm_kernel(x_ref, g_ref, mod_ref, o_ref, *, sc, sh):
    m = mod_ref[0]
    y = _rms(x_ref[...], g_ref[...])
    o_ref[...] = (y * (1.0 + m[sc:sc + 1]) + m[sh:sh + 1]).astype(o_ref.dtype)


def _prenorm(xs, n_rows, g, modtab, sc, sh, lat_rows, out_dtype):
    d = xs.shape[1]
    nb = modtab.shape[0] - 1
    mi = _mod_index(ROW_TILE, lat_rows, nb)
    return pl.pallas_call(
        functools.partial(_prenorm_kernel, sc=sc, sh=sh),
        grid=(n_rows // ROW_TILE,),
        in_specs=[pl.BlockSpec((ROW_TILE, d), lambda i: (i, 0)),
                  pl.BlockSpec((1, d), lambda i: (0, 0)),
                  pl.BlockSpec((1, 6, d), lambda i: (mi(i), 0, 0))],
        out_specs=pl.BlockSpec((ROW_TILE, d), lambda i: (i, 0)),
        out_shape=jax.ShapeDtypeStruct((n_rows, d), out_dtype),
        compiler_params=_cparams(("arbitrary",), 32),
        name="prenorm",
    )(xs, g.reshape(1, d), modtab)


def _mm_kernel(x_ref, w_ref, o_ref):
    o_ref[...] = _dot(x_ref[...], w_ref[...]).astype(o_ref.dtype)


def _mm(x, w, tm, tn, out_dtype=F32):
    m, k = x.shape
    n = w.shape[1]
    return pl.pallas_call(
        _mm_kernel,
        grid=(m // tm, n // tn),
        in_specs=[pl.BlockSpec((tm, k), lambda i, j: (i, 0)),
                  pl.BlockSpec((k, tn), lambda i, j: (0, j))],
        out_specs=pl.BlockSpec((tm, tn), lambda i, j: (i, j)),
        out_shape=jax.ShapeDtypeStruct((m, n), out_dtype),
        compiler_params=_cparams(("arbitrary", "arbitrary"), 40),
        name="proj",
    )(x, w)


def _bwd_chunk(i, n_ctx, n_lat):
    return jnp.where(i < n_ctx, n_ctx - 1 - i, 2 * n_ctx + n_lat - 1 - i)


def _rows(c):
    return pl.ds(pl.multiple_of(c * CHUNK, CHUNK), CHUNK)


def _for_chunks(n_ctx, n_lat, fn, with_ctx=True):
    def run(part, n, off):
        def body(c, carry):
            fn(part, c, c + off)
            return carry
        lax.fori_loop(0, n, body, 0, unroll=SCAN_UNROLL if n % SCAN_UNROLL == 0 else 1)
    if with_ctx:
        run(0, n_ctx, 0)
    run(1, n_lat, n_ctx)


def _lanes(d, w):
    return slice(d * w, (d + 1) * w)


def _mixer_call(kernel_fn, name, in_specs, args, nb, heads, lat_len, ctx_len, dv, scratch, emit_ctx, vmem_mb):
    width = heads * dv
    out_specs = [pl.BlockSpec((lat_len, dv), lambda b, h: (b, h))]
    out_shape = [jax.ShapeDtypeStruct((nb * lat_len, width), BF16)]
    if emit_ctx:
        out_specs.append(pl.BlockSpec((ctx_len, dv), lambda b, h: (b, h)))
        out_shape.append(jax.ShapeDtypeStruct((nb * ctx_len, width), BF16))
    n_in = len(in_specs)

    def body(*refs):
        ins, rest = refs[:n_in], refs[n_in:]
        if emit_ctx:
            ol, oc, scr = rest[0], rest[1], rest[2:]
        else:
            ol, oc, scr = rest[0], None, rest[1:]
        kernel_fn(*ins, ol, oc, *scr, n_ctx=ctx_len // CHUNK, n_lat=lat_len // CHUNK)

    return pl.pallas_call(
        body,
        grid=(nb, heads),
        in_specs=in_specs,
        out_specs=out_specs,
        out_shape=out_shape,
        scratch_shapes=scratch,
        compiler_params=_cparams(("arbitrary", "arbitrary"), vmem_mb),
        name=name,
    )(*args)


def _seq_specs(col0, width, nb, lat_len, ctx_len):
    n_lat_blk = nb * lat_len // ctx_len
    c0 = col0 * 128 // width
    return (pl.BlockSpec((lat_len, width), lambda b, h: (b, c0 + h)),
            pl.BlockSpec((ctx_len, width), lambda b, h: (n_lat_blk + b, c0 + h)))


def _ret_kernel(ql, qc, kl, kc, vl, vc, gl, gc, cos_ref, sin_ref, lg_ref, gn_ref, ol_ref, oc_ref,
                qs_s, kv_s, sall_s, o_s, st_s, dm_s, dq_s, dk_s, *, n_ctx, n_lat):
    nch = n_ctx + n_lat
    qr, kr, vr, gr, outr = (qc, ql), (kc, kl), (vc, vl), (gc, gl), (oc_ref, ol_ref)
    t_i = lax.broadcasted_iota(jnp.int32, (CHUNK, CHUNK), 0).astype(F32)
    s_i = lax.broadcasted_iota(jnp.int32, (CHUNK, CHUNK), 1).astype(F32)
    dm = None
    for d in range(2):
        lg = lg_ref[d, 0][0:1, 0:CHUNK]
        rel = (t_i - s_i) if d == 0 else (s_i - t_i)
        dmd = jnp.where(rel >= 0, jnp.exp(jnp.maximum(rel, 0.0) * lg), 0.0)
        dm = dmd if dm is None else dm + dmd
        p = t_i if d == 0 else (CHUNK - 1.0) - t_i
        dq_s[d] = jnp.exp((p + 1.0) * lg)
        dk_s[d] = jnp.exp((CHUNK - 1.0 - p) * lg)
    dm_s[...] = dm
    scale = RET_DK ** -0.5

    def phase_a(part, cl, cg):
        rl, rg = _rows(cl), _rows(cg)
        cs = cos_ref[rg, :]
        sn = sin_ref[rg, :]
        q = qr[part][rl, :]
        k = kr[part][rl, :]
        q = q * cs + pltpu.roll(q, RET_DK // 2, 1) * sn
        k = (k * cs + pltpu.roll(k, RET_DK // 2, 1) * sn) * scale
        vb = _bf(vr[part][rl, :])
        s = _dot_nt(_bf(q), _bf(k))
        o_s[rg, :] = _dot(_bf(s * dm_s[...]), vb)
        qs_s[rg, :] = jnp.concatenate([_bf(q * dq_s[0]), _bf(q * dq_s[1])], axis=1)
        kd = jnp.concatenate([_bf(k * dk_s[0]), _bf(k * dk_s[1])], axis=1)
        kv_s[cg] = _dot_tn(kd, vb)

    _for_chunks(n_ctx, n_lat, phase_a)

    st_s[...] = jnp.zeros(st_s.shape, F32)
    dec = [jnp.exp(float(CHUNK) * lg_ref[d, 0][0:1, :]) for d in range(2)]

    def phase_b(i, carry):
        for d, c in ((0, i), (1, _bwd_chunk(i, n_ctx, n_lat))):
            st = st_s[d]
            sall_s[c, _lanes(d, RET_DK), :] = _bf(st)
            st_s[d] = dec[d] * st + kv_s[c, _lanes(d, RET_DK), :]
        return carry

    lax.fori_loop(0, nch, phase_b, 0)
    gn = gn_ref[...]

    def phase_c(part, cl, cg):
        rl, rg = _rows(cl), _rows(cg)
        o = o_s[rg, :] + _dot(qs_s[rg, :], sall_s[cg])
        outr[part][rl, :] = (_rms(o, gn) * _silu(gr[part][rl, :])).astype(BF16)

    _for_chunks(n_ctx, n_lat, phase_c, with_ctx=oc_ref is not None)


def _retention(p, nb, lat_len, ctx_len, cosf, sinf, lg_b, gn, emit_ctx):
    t_len = lat_len + ctx_len
    nch = t_len // CHUNK
    s128 = functools.partial(_seq_specs, width=128, nb=nb, lat_len=lat_len, ctx_len=ctx_len)
    s256 = functools.partial(_seq_specs, width=256, nb=nb, lat_len=lat_len, ctx_len=ctx_len)
    in_specs = [*s128(_C_RET_Q), *s128(_C_RET_K), *s256(_C_RET_V), *s256(_C_RET_G),
                pl.BlockSpec((t_len, 128), lambda b, h: (0, 0)),
                pl.BlockSpec((t_len, 128), lambda b, h: (0, 0)),
                pl.BlockSpec((2, 1, 8, RET_DV), lambda b, h: (0, h, 0, 0)),
                pl.BlockSpec((1, RET_DV), lambda b, h: (0, h))]
    scratch = [pltpu.VMEM((t_len, 2 * RET_DK), BF16),
               pltpu.VMEM((nch, 2 * RET_DK, RET_DV), F32),
               pltpu.VMEM((nch, 2 * RET_DK, RET_DV), BF16),
               pltpu.VMEM((t_len, RET_DV), F32),
               pltpu.VMEM((2, RET_DK, RET_DV), F32),
               pltpu.VMEM((CHUNK, CHUNK), F32),
               pltpu.VMEM((2, CHUNK, CHUNK), F32), pltpu.VMEM((2, CHUNK, CHUNK), F32)]
    args = (p, p, p, p, p, p, p, p, cosf, sinf, lg_b, gn.reshape(1, RET_WIDTH))
    return _mixer_call(_ret_kernel, "retention", in_specs, args, nb, RET_HEADS, lat_len, ctx_len, RET_DV,
                       scratch, emit_ctx, 48)


_HG_LEVELS = (64, 32, 16, 8, 4, 2, 1)


def _hgrn_constants():
    c = CHUNK
    t = np.arange(c)[:, None]
    u = np.arange(c)[None, :]
    cum = (u <= t).astype(np.float32)
    masks, signs = [], []
    for m in _HG_LEVELS:
        base = (t // (2 * m)) * (2 * m)
        lower = t >= base + m
        tb = t // (2 * m)
        sb = u // (2 * m)
        masks.append((tb == sb) & lower & (u < (sb * 2 * m + m)))
        signs.append(np.broadcast_to(np.where(lower, 1.0, -1.0), (c, c)))
    masks.append(t == u)
    kf = np.stack([x.astype(np.float32) for x in masks], axis=0)
    kb = np.stack([x.astype(np.float32)[::-1, ::-1] for x in masks], axis=0)
    sf = np.stack([x.astype(np.float32) for x in signs], axis=0)
    sb_ = np.stack([x.astype(np.float32)[::-1, ::-1] for x in signs], axis=0)
    return (np.stack([cum, cum[::-1, ::-1]], 0), np.stack([kf, kb], 0),
            np.stack([sf[:_HG_WIDE], sb_[:_HG_WIDE]], 0))


_HG_WIDE = 5


def _hgrn_level_exponents(b, lf, d, sgn_ref, row):
    out = []
    for l, m in enumerate(_HG_LEVELS[:_HG_WIDE]):
        pieces = []
        for j in range(CHUNK // (2 * m)):
            r = 2 * m * j + (m - 1 if d == 0 else m)
            pieces.append(jnp.broadcast_to(b[r:r + 1, :], (2 * m, CHUNK)))
        bref = pieces[0] if len(pieces) == 1 else jnp.concatenate(pieces, axis=0)
        out.append((b - bref) * sgn_ref[d, l])
    up = pltpu.roll(lf, CHUNK - 1, 0)
    dn = pltpu.roll(lf, 1, 0)
    r4 = row % 4
    if d == 0:
        e2 = jnp.where(r4 == 0, up, jnp.where(r4 == 1, 0.0, jnp.where(r4 == 2, lf, lf + dn)))
        e1 = jnp.where(row % 2 == 1, lf, 0.0)
    else:
        e2 = jnp.where(r4 == 0, lf + up, jnp.where(r4 == 1, lf, jnp.where(r4 == 2, 0.0, dn)))
        e1 = jnp.where(row % 2 == 0, lf, 0.0)
    return out + [e2, e1]


def _hgrn_kernel(ql, qc, ffl, ffc, fbl, fbc, il, ic, gl, gc, llb_ref, lub_ref, gn_ref, c_ref, k_ref, sgn_ref,
                 ol_ref, oc_ref, qs_s, kv_s, sall_s, dec_s, o_s, st_s, *, n_ctx, n_lat):
    nch = n_ctx + n_lat
    qr, fr, ir, gr, outr = (qc, ql), ((ffc, ffl), (fbc, fbl)), (ic, il), (gc, gl), (oc_ref, ol_ref)
    llb = llb_ref[...]
    lub = lub_ref[...]
    nlev = len(_HG_LEVELS)
    row = lax.broadcasted_iota(jnp.int32, (CHUNK, HGRN_DK), 0)

    def phase_a(part, cl, cg):
        rl, rg = _rows(cl), _rows(cg)
        q = _silu(qr[part][rl, :])
        qb = _bf(q)
        vb = _bf(ir[part][rl, :])
        a_sum, qs, kds = None, [], []
        for d in range(2):
            lsg = lub + _log_sigmoid(fr[d][part][rl, :])
            lf = jnp.maximum(llb, lsg) + jnp.log1p(jnp.exp(-jnp.abs(llb - lsg)))
            k = 1.0 - jnp.exp(lf)
            lf_hi = _bf(lf)
            lf_lo = _bf(lf - lf_hi.astype(F32))
            b2 = _dot(c_ref[d], jnp.concatenate([lf_hi, lf_lo], axis=1))
            b = b2[:, 0:HGRN_DK] + b2[:, HGRN_DK:]
            es = _hgrn_level_exponents(b, lf, d, sgn_ref, row)
            a = k_ref[d, nlev] * _dot_nt(qb, _bf(k))
            for l in range(nlev):
                x = jnp.exp(es[l])
                a = a + k_ref[d, l] * _dot_nt(_bf(q * x), _bf(k * x))
            a_sum = a if a_sum is None else a_sum + a
            b_last = b[CHUNK - 1:CHUNK] if d == 0 else b[0:1]
            qs.append(_bf(q * jnp.exp(b)))
            kds.append(_bf(k * jnp.exp(b_last - b)))
            dec_s[cg, :, _lanes(d, HGRN_DK)] = jnp.exp(b_last)
        o_s[rg, :] = _dot(_bf(a_sum), vb)
        qs_s[rg, :] = jnp.concatenate(qs, axis=1)
        kv_s[cg] = _dot_tn(vb, jnp.concatenate(kds, axis=1))

    _for_chunks(n_ctx, n_lat, phase_a)
    st_s[...] = jnp.zeros(st_s.shape, F32)

    def phase_b(i, carry):
        for d, c in ((0, i), (1, _bwd_chunk(i, n_ctx, n_lat))):
            ln = _lanes(d, HGRN_DK)
            st = st_s[d]
            sall_s[c, :, ln] = _bf(st)
            st_s[d] = dec_s[c, :, ln] * st + kv_s[c, :, ln]
        return carry

    lax.fori_loop(0, nch, phase_b, 0)
    gn = gn_ref[...]

    def phase_c(part, cl, cg):
        rl, rg = _rows(cl), _rows(cg)
        o = o_s[rg, :] + _dot_nt(qs_s[rg, :], sall_s[cg])
        outr[part][rl, :] = (_rms(o, gn) * _silu(gr[part][rl, :])).astype(BF16)

    _for_chunks(n_ctx, n_lat, phase_c, with_ctx=oc_ref is not None)


def _hgrn(p, nb, lat_len, ctx_len, log_lb, log_ub, gn, consts, emit_ctx):
    mats, masks, signs = consts
    t_len = lat_len + ctx_len
    nch = t_len // CHUNK
    spec = functools.partial(_seq_specs, width=128, nb=nb, lat_len=lat_len, ctx_len=ctx_len)
    vec = pl.BlockSpec((1, 128), lambda b, h: (0, h))
    in_specs = [*spec(_C_HG_Q), *spec(_C_HG_FF), *spec(_C_HG_FB), *spec(_C_HG_I), *spec(_C_HG_G),
                vec, vec, vec,
                pl.BlockSpec(mats.shape, lambda b, h: (0, 0, 0)),
                pl.BlockSpec(masks.shape, lambda b, h: (0, 0, 0, 0)),
                pl.BlockSpec(signs.shape, lambda b, h: (0, 0, 0, 0))]
    scratch = [pltpu.VMEM((t_len, 2 * HGRN_DK), BF16),
               pltpu.VMEM((nch, HGRN_DV, 2 * HGRN_DK), F32),
               pltpu.VMEM((nch, HGRN_DV, 2 * HGRN_DK), BF16),
               pltpu.VMEM((nch, 1, 2 * HGRN_DK), F32),
               pltpu.VMEM((t_len, HGRN_DV), F32),
               pltpu.VMEM((2, HGRN_DV, HGRN_DK), F32)]
    args = (p, p, p, p, p, p, p, p, p, p, log_lb.reshape(1, HGRN_QK), log_ub.reshape(1, HGRN_QK),
            gn.reshape(1, HGRN_WIDTH), mats, masks, signs)
    return _mixer_call(_hgrn_kernel, "hgrn2", in_specs, args, nb, HGRN_HEADS, lat_len, ctx_len, HGRN_DV,
                       scratch, emit_ctx, 48)


def _mlstm_kernel(ql, qc, kl, kc, vl, vc, ogl, ogc, gcl, gcc, grl, grc, bc_ref, br_ref,
                  wq_ref, wk_ref, gn_ref, ol_ref, oc_ref,
                  q_s, gr_s, intra_s, kv_s, call_s, rho_s, bcol_s, bl_s, mu_s, mprev_s, st_s, m_s,
                  *, n_ctx, n_lat):
    nch = n_ctx + n_lat
    h = pl.program_id(1)
    nh = MLSTM_HEADS
    dk, dv = MLSTM_DK, MLSTM_DV
    ext = 2 * dv
    qr, kr, vr, ogr, gcr, outr = (qc, ql), (kc, kl), (vc, vl), (ogc, ogl), (gcc, gcl), (oc_ref, ol_ref)
    n_loc = (n_ctx, n_lat)

    for cc in range(nch):
        src, c0 = (grc, cc) if cc < n_ctx else (grl, cc - n_ctx)
        gr_s[cc] = src[0, :, c0 * CHUNK:(c0 + 1) * CHUNK] + br_ref[...]

    row = lax.broadcasted_iota(jnp.int32, (CHUNK, dk), 0)
    lane = lax.broadcasted_iota(jnp.int32, (CHUNK, dv), 1)
    ones_col = jnp.where(lane == 0, 1.0, 0.0).astype(BF16)
    scale = dk ** -0.5
    t_i = lax.broadcasted_iota(jnp.int32, (CHUNK, CHUNK), 0)
    s_i = lax.broadcasted_iota(jnp.int32, (CHUNK, CHUNK), 1)
    lane16 = lax.broadcasted_iota(jnp.int32, (CHUNK, MLSTM_GATES), 1)
    sub16 = lax.broadcasted_iota(jnp.int32, (MLSTM_GATES, CHUNK), 0)

    def conv(src, w_ref, part, cl):
        r0 = pl.multiple_of(cl * CHUNK, CHUNK)
        n_rows = n_loc[part] * CHUNK
        x = src[pl.ds(r0, CHUNK), :]
        pr = src[pl.ds(jnp.maximum(r0 - 1, 0), 1), :]
        nx = src[pl.ds(jnp.minimum(r0 + CHUNK, n_rows - 1), 1), :]
        pr = jnp.where(cl != 0, pr, 0.0)
        nx = jnp.where(cl != n_loc[part] - 1, nx, 0.0)
        xp = jnp.where(row == 0, pr, pltpu.roll(x, 1, 0))
        xn = jnp.where(row == CHUNK - 1, nx, pltpu.roll(x, CHUNK - 1, 0))
        w = w_ref[...]
        return _silu(w[0:1] * xp + w[1:2] * x + w[2:3] * xn)

    def pick_col(g, j):
        return jnp.sum(jnp.where(lane16 == j, g, 0.0), axis=1, keepdims=True)

    def pick_row(g, j):
        return jnp.sum(jnp.where(sub16 == j, g, 0.0), axis=0, keepdims=True)

    def phase_a(part, cl, cg):
        rl, rg = _rows(cl), _rows(cg)
        q = conv(qr[part], wq_ref, part, cl)
        k = conv(kr[part], wk_ref, part, cl) * scale
        qb = _bf(q)
        q_s[rg, :] = qb
        vb = jnp.concatenate([_bf(vr[part][rl, :]), ones_col], axis=1)
        s = _dot_nt(qb, _bf(k))
        g_c = gcr[part][rl, :] + bc_ref[...]
        g_r = gr_s[cg]
        kws = []
        for d in range(2):
            ig_c = pick_col(g_c, d * nh + h)
            lf_c = _log_sigmoid(pick_col(g_c, 2 * nh + d * nh + h))
            ig_r = pick_row(g_r, d * nh + h)
            lf_r = _log_sigmoid(pick_row(g_r, 2 * nh + d * nh + h))
            tri = (s_i <= t_i) if d == 0 else (s_i >= t_i)
            tri_t = (t_i <= s_i) if d == 0 else (t_i >= s_i)
            b_c = jnp.sum(jnp.where(tri, lf_r, 0.0), axis=1, keepdims=True)
            b_r = jnp.sum(jnp.where(tri_t, lf_c, 0.0), axis=0, keepdims=True)
            dmat = jnp.where(tri, b_c - b_r + ig_r, -jnp.inf)
            rho = jnp.max(dmat, axis=1, keepdims=True)
            intra_s[d, rg, :] = _dot(_bf(s * jnp.exp(dmat - rho)), vb)
            b_last = b_c[CHUNK - 1:CHUNK] if d == 0 else b_c[0:1]
            mu = jnp.max(b_last - b_r + ig_r, axis=1, keepdims=True)
            kws.append(_bf(k * jnp.exp(b_last - b_c + ig_c - mu)))
            rho_s[d, rg, :] = rho
            bcol_s[d, rg, :] = b_c
            bl_s[cg, :, _lanes(d, 128)] = jnp.broadcast_to(b_last, (1, 128))
            mu_s[cg, :, _lanes(d, 128)] = jnp.broadcast_to(mu, (1, 128))
        kv_s[cg] = _dot_tn(jnp.concatenate(kws, axis=1), vb)

    _for_chunks(n_ctx, n_lat, phase_a)
    st_s[...] = jnp.zeros(st_s.shape, F32)
    m_s[...] = jnp.zeros(m_s.shape, F32)

    def wide(v):
        return jnp.concatenate([v, v], axis=1)

    def phase_b(i, carry):
        for d, c in ((0, i), (1, _bwd_chunk(i, n_ctx, n_lat))):
            ln = _lanes(d, 128)
            m_prev = m_s[d]
            st = st_s[d]
            mprev_s[c, :, ln] = m_prev
            call_s[c, :, _lanes(d, ext)] = _bf(st)
            bl = bl_s[c, :, ln]
            mu = mu_s[c, :, ln]
            m_new = jnp.maximum(bl + m_prev, mu)
            st_s[d] = (wide(jnp.exp(bl + m_prev - m_new)) * st
                       + wide(jnp.exp(mu - m_new)) * kv_s[c, _lanes(d, dk), :])
            m_s[d] = m_new
        return carry

    lax.fori_loop(0, nch, phase_b, 0)
    gn = gn_ref[...]

    def phase_c(part, cl, cg):
        rl, rg = _rows(cl), _rows(cg)
        qc_all = _dot(q_s[rg, :], call_s[cg])
        o = None
        for d in range(2):
            rho = rho_s[d, rg, :]
            b_c = bcol_s[d, rg, :]
            m_prev = mprev_s[cg, :, _lanes(d, 128)][:, 0:1]
            m_t = jnp.maximum(rho, b_c + m_prev)
            nd = (jnp.exp(rho - m_t) * intra_s[d, rg, :]
                  + jnp.exp(b_c + m_prev - m_t) * qc_all[:, _lanes(d, ext)])
            hh = nd[:, 0:dv] / jnp.maximum(jnp.abs(nd[:, dv:dv + 1]), jnp.exp(-m_t))
            o = hh if o is None else o + hh
        y = o - jnp.mean(o, axis=-1, keepdims=True)
        y = y * lax.rsqrt(jnp.mean(y * y, axis=-1, keepdims=True) + NORM_EPS)
        outr[part][rl, :] = (y * gn * _sigmoid(ogr[part][rl, :])).astype(BF16)

    _for_chunks(n_ctx, n_lat, phase_c, with_ctx=oc_ref is not None)


def _mlstm(p, g_col, g_row_lat, g_row_ctx, nb, lat_len, ctx_len, gate_b, conv_w, gn, emit_ctx):
    t_len = lat_len + ctx_len
    nch = t_len // CHUNK
    n_lat_blk = nb * lat_len // ctx_len
    ng = MLSTM_GATES
    dk, dv = MLSTM_DK, MLSTM_DV
    spec = functools.partial(_seq_specs, width=128, nb=nb, lat_len=lat_len, ctx_len=ctx_len)
    in_specs = [*spec(_C_ML_Q), *spec(_C_ML_K), *spec(_C_ML_V), *spec(_C_ML_O),
                pl.BlockSpec((lat_len, ng), lambda b, h: (b, 0)),
                pl.BlockSpec((ctx_len, ng), lambda b, h: (n_lat_blk + b, 0)),
                pl.BlockSpec((1, ng, lat_len), lambda b, h: (b, 0, 0)),
                pl.BlockSpec((1, ng, ctx_len), lambda b, h: (b, 0, 0)),
                pl.BlockSpec((1, ng), lambda b, h: (0, 0)),
                pl.BlockSpec((ng, 1), lambda b, h: (0, 0)),
                pl.BlockSpec((3, 128), lambda b, h: (0, h)),
                pl.BlockSpec((3, 128), lambda b, h: (0, MLSTM_HEADS + h)),
                pl.BlockSpec((1, 128), lambda b, h: (0, h))]
    scratch = [pltpu.VMEM((t_len, dk), BF16),
               pltpu.VMEM((nch, ng, CHUNK), F32),
               pltpu.VMEM((2, t_len, 2 * dv), F32),
               pltpu.VMEM((nch, 2 * dk, 2 * dv), F32),
               pltpu.VMEM((nch, dk, 4 * dv), BF16),
               pltpu.VMEM((2, t_len, 1), F32), pltpu.VMEM((2, t_len, 1), F32),
               pltpu.VMEM((nch, 1, 256), F32), pltpu.VMEM((nch, 1, 256), F32), pltpu.VMEM((nch, 1, 256), F32),
               pltpu.VMEM((2, dk, 2 * dv), F32), pltpu.VMEM((2, 1, 128), F32)]
    args = (p, p, p, p, p, p, p, p, g_col, g_col, g_row_lat, g_row_ctx,
            gate_b.reshape(1, ng), gate_b.reshape(ng, 1), conv_w, conv_w, gn.reshape(1, MLSTM_WIDTH))
    return _mixer_call(_mlstm_kernel, "mlstm", in_specs, args, nb, MLSTM_HEADS, lat_len, ctx_len, dv,
                       scratch, emit_ctx, 48)


def _wout_kernel(*refs, n_lat_tiles, has_ctx, emit_next):
    it = iter(refs)
    lat = [next(it) for _ in range(3)]
    ctx = [next(it) for _ in range(3)] if has_ctx else None
    wr_ref, wh_ref, wm_ref, x_ref, g_ref, mod_ref = (next(it) for _ in range(6))
    g2_ref = next(it) if emit_next else None
    o_ref = next(it)
    h2_ref = next(it) if emit_next else None

    def run(mix):
        y = _dot(mix[0][...], wr_ref[...]) + _dot(mix[1][...], wh_ref[...]) + _dot(mix[2][...], wm_ref[...])
        m = mod_ref[0]
        xn = x_ref[...] + m[2:3] * _rms(y, g_ref[...])
        o_ref[...] = xn
        if emit_next:
            h2_ref[...] = (_rms(xn, g2_ref[...]) * (1.0 + m[4:5]) + m[3:4]).astype(h2_ref.dtype)

    if has_ctx:
        i = pl.program_id(0)
        pl.when(i < n_lat_tiles)(lambda: run(lat))
        pl.when(i >= n_lat_tiles)(lambda: run(ctx))
    else:
        run(lat)


def _wout(mix_lat, mix_ctx, w_out, xs, n_rows, g1, modtab, lat_rows, g2_next):
    d = xs.shape[1]
    nb = modtab.shape[0] - 1
    mi = _mod_index(ROW_TILE, lat_rows, nb)
    tm = ROW_TILE
    n_lat_tiles = mix_lat[0].shape[0] // tm
    has_ctx = mix_ctx is not None
    emit_next = g2_next is not None
    widths = (RET_WIDTH, HGRN_WIDTH, MLSTM_WIDTH)
    w_r = w_out[0:RET_WIDTH]
    w_h = w_out[RET_WIDTH:RET_WIDTH + HGRN_WIDTH]
    w_m = w_out[RET_WIDTH + HGRN_WIDTH:]
    full = lambda a: pl.BlockSpec(a.shape, lambda i: (0, 0))
    in_specs = [pl.BlockSpec((tm, w), lambda i: (jnp.minimum(i, n_lat_tiles - 1), 0)) for w in widths]
    args = list(mix_lat)
    if has_ctx:
        in_specs += [pl.BlockSpec((tm, w), lambda i: (jnp.maximum(i - n_lat_tiles, 0), 0)) for w in widths]
        args += list(mix_ctx)
    in_specs += [full(w_r), full(w_h), full(w_m),
                 pl.BlockSpec((tm, d), lambda i: (i, 0)),
                 pl.BlockSpec((1, d), lambda i: (0, 0)),
                 pl.BlockSpec((1, 6, d), lambda i: (mi(i), 0, 0))]
    args += [w_r, w_h, w_m, xs, g1.reshape(1, d), modtab]
    out_specs = [pl.BlockSpec((tm, d), lambda i: (i, 0))]
    out_shape = [jax.ShapeDtypeStruct((n_rows, d), F32)]
    if emit_next:
        in_specs.append(pl.BlockSpec((1, d), lambda i: (0, 0)))
        args.append(g2_next.reshape(1, d))
        out_specs.append(pl.BlockSpec((tm, d), lambda i: (i, 0)))
        out_shape.append(jax.ShapeDtypeStruct((n_rows, d), BF16))
    return pl.pallas_call(
        functools.partial(_wout_kernel, n_lat_tiles=n_lat_tiles, has_ctx=has_ctx, emit_next=emit_next),
        grid=(n_rows // tm,),
        in_specs=in_specs,
        out_specs=out_specs,
        out_shape=out_shape,
        compiler_params=_cparams(("arbitrary",), 52),
        name="wout",
    )(*args)


def _ffn_kernel(*refs, emit_next):
    if emit_next:
        h_ref, w1_ref, w3_ref, w2_ref, x_ref, g_ref, mod_ref, gn_ref, modn_ref, o_ref, hn_ref, acc_ref = refs
    else:
        h_ref, w1_ref, w3_ref, w2_ref, x_ref, g_ref, mod_ref, o_ref, acc_ref = refs
    f = pl.program_id(1)

    @pl.when(f == 0)
    def _():
        acc_ref[...] = jnp.zeros(acc_ref.shape, F32)

    h = h_ref[...]
    u = _silu(_dot(h, w1_ref[...])) * _dot(h, w3_ref[...])
    acc_ref[...] += _dot(_bf(u), w2_ref[...])

    @pl.when(f == pl.num_programs(1) - 1)
    def _():
        m = mod_ref[0]
        xn = x_ref[...] + m[5:6] * _rms(acc_ref[...], g_ref[...])
        o_ref[...] = xn
        if emit_next:
            mn = modn_ref[0]
            hn_ref[...] = (_rms(xn, gn_ref[...]) * (1.0 + mn[1:2]) + mn[0:1]).astype(hn_ref.dtype)


def _ffn(hb, w1, w3, w2, xs, g3, modtab, lat_rows, g_next, modtab_next):
    n_rows, d = hb.shape
    dff = w1.shape[1]
    nb = modtab.shape[0] - 1
    tm, tf = ROW_TILE, FFN_TF
    mi = _mod_index(tm, lat_rows, nb)
    emit_next = g_next is not None
    in_specs = [pl.BlockSpec((tm, d), lambda i, f: (i, 0)),
                pl.BlockSpec((d, tf), lambda i, f: (0, f)),
                pl.BlockSpec((d, tf), lambda i, f: (0, f)),
                pl.BlockSpec((tf, d), lambda i, f: (f, 0)),
                pl.BlockSpec((tm, d), lambda i, f: (i, 0)),
                pl.BlockSpec((1, d), lambda i, f: (0, 0)),
                pl.BlockSpec((1, 6, d), lambda i, f: (mi(i), 0, 0))]
    args = [hb, w1, w3, w2, xs, g3.reshape(1, d), modtab]
    out_specs = [pl.BlockSpec((tm, d), lambda i, f: (i, 0))]
    out_shape = [jax.ShapeDtypeStruct((n_rows, d), F32)]
    if emit_next:
        in_specs += [pl.BlockSpec((1, d), lambda i, f: (0, 0)),
                     pl.BlockSpec((1, 6, d), lambda i, f: (mi(i), 0, 0))]
        args += [g_next.reshape(1, d), modtab_next]
        out_specs.append(pl.BlockSpec((tm, d), lambda i, f: (i, 0)))
        out_shape.append(jax.ShapeDtypeStruct((n_rows, d), BF16))
    return pl.pallas_call(
        functools.partial(_ffn_kernel, emit_next=emit_next),
        grid=(n_rows // tm, dff // tf),
        in_specs=in_specs,
        out_specs=out_specs,
        out_shape=out_shape,
        scratch_shapes=[pltpu.VMEM((tm, d), F32)],
        compiler_params=_cparams(("arbitrary", "arbitrary"), 48),
        name="ffn",
    )(*args)


_META_E0, _META_E1, _META_R0, _META_R1, _META_G0, _META_G1 = range(6)


def _router_kernel(x_ref, g_ref, mod_ref, wr_ref, tri_ref, meta_ref, cnt_ref, carry_ref):
    i = pl.program_id(0)

    @pl.when(i == 0)
    def _():
        carry_ref[...] = jnp.zeros(carry_ref.shape, F32)

    m = mod_ref[0]
    hmod = _rms(x_ref[...], g_ref[...]) * (1.0 + m[4:5]) + m[3:4]
    logits = jnp.dot(hmod, wr_ref[...], precision=lax.Precision.HIGHEST, preferred_element_type=F32)
    lane = lax.broadcasted_iota(jnp.int32, logits.shape, 1)
    lanef = lane.astype(F32)
    logits = jnp.where(lane < N_EXPERTS, logits, -jnp.inf)
    v0 = jnp.max(logits, axis=1, keepdims=True)
    e0 = jnp.min(jnp.where(logits == v0, lanef, 1e9), axis=1, keepdims=True)
    rest = jnp.where(lanef == e0, -jnp.inf, logits)
    v1 = jnp.max(rest, axis=1, keepdims=True)
    e1 = jnp.min(jnp.where(rest == v1, lanef, 1e9), axis=1, keepdims=True)
    ex = jnp.exp(v1 - v0)
    g0 = 1.0 / (1.0 + ex)
    g1 = ex / (1.0 + ex)
    oh0 = lanef == e0
    oh1 = lanef == e1
    oh = jnp.where(jnp.logical_or(oh0, oh1), 1.0, 0.0)
    before = _dot(tri_ref[...], _bf(oh)) + carry_ref[0:1, :]
    r0 = jnp.sum(jnp.where(oh0, before, 0.0), axis=1, keepdims=True)
    r1 = jnp.sum(jnp.where(oh1, before, 0.0), axis=1, keepdims=True)
    carry_ref[0:1, :] = carry_ref[0:1, :] + jnp.sum(oh, axis=0, keepdims=True)
    meta = jnp.zeros(logits.shape, F32)
    for j, val in ((_META_E0, e0), (_META_E1, e1), (_META_R0, r0), (_META_R1, r1),
                   (_META_G0, g0), (_META_G1, g1)):
        meta = jnp.where(lane == j, val, meta)
    meta_ref[...] = meta
    cnt_ref[...] = carry_ref[...]


def _router(xs, n_rows, g2, modtab, w_router, lat_rows):
    d = xs.shape[1]
    nb = modtab.shape[0] - 1
    tm = ROW_TILE
    mi = _mod_index(tm, lat_rows, nb)
    wr = jnp.zeros((d, 128), F32).at[:, :N_EXPERTS].set(w_router)
    tri = jnp.asarray(np.tril(np.ones((tm, tm), np.float32), -1), BF16)
    return pl.pallas_call(
        _router_kernel,
        grid=(n_rows // tm,),
        in_specs=[pl.BlockSpec((tm, d), lambda i: (i, 0)),
                  pl.BlockSpec((1, d), lambda i: (0, 0)),
                  pl.BlockSpec((1, 6, d), lambda i: (mi(i), 0, 0)),
                  pl.BlockSpec((d, 128), lambda i: (0, 0)),
                  pl.BlockSpec((tm, tm), lambda i: (0, 0))],
        out_specs=[pl.BlockSpec((tm, 128), lambda i: (i, 0)),
                   pl.BlockSpec((8, 128), lambda i: (0, 0))],
        out_shape=[jax.ShapeDtypeStruct((n_rows, 128), F32),
                   jax.ShapeDtypeStruct((8, 128), F32)],
        scratch_shapes=[pltpu.VMEM((8, 128), F32)],
        compiler_params=_cparams(("arbitrary",), 32),
        name="router",
    )(xs, g2.reshape(1, d), modtab, wr, tri)


def _dispatch_kernel(dest_ref, x_ref, g_ref, mod_ref, init_ref, o_ref, h_s, sem):
    del init_ref
    i = pl.program_id(0)
    tm = h_s.shape[0]
    m = mod_ref[0]
    h_s[...] = _rms(x_ref[...], g_ref[...]) * (1.0 + m[4:5]) + m[3:4]

    def row_copy(r, k):
        dst = dest_ref[TOP_K * (i * tm + r) + k]
        return pltpu.make_async_copy(h_s.at[pl.ds(r, 1)], o_ref.at[pl.ds(dst, 1)], sem)

    def start(r, carry):
        for k in range(TOP_K):
            row_copy(r, k).start(priority=k)
        return carry

    lax.fori_loop(0, tm, start, 0)

    def wait(r, carry):
        for k in range(TOP_K):
            row_copy(r, k).wait()
        return carry

    lax.fori_loop(0, tm, wait, 0)


def _dispatch(dest, xs, n_rows, g2, modtab, n_slots, lat_rows):
    d = xs.shape[1]
    nb = modtab.shape[0] - 1
    tm = ROW_TILE
    mi = _mod_index(tm, lat_rows, nb)
    init = jnp.zeros((n_slots, d), F32)
    grid_spec = pltpu.PrefetchScalarGridSpec(
        num_scalar_prefetch=1,
        grid=(n_rows // tm,),
        in_specs=[pl.BlockSpec((tm, d), lambda i, dst: (i, 0)),
                  pl.BlockSpec((1, d), lambda i, dst: (0, 0)),
                  pl.BlockSpec((1, 6, d), lambda i, dst: (mi(i), 0, 0)),
                  pl.BlockSpec(memory_space=pl.ANY)],
        out_specs=pl.BlockSpec(memory_space=pl.ANY),
        scratch_shapes=[pltpu.VMEM((tm, d), F32), pltpu.SemaphoreType.DMA(())],
    )
    return pl.pallas_call(
        _dispatch_kernel,
        grid_spec=grid_spec,
        out_shape=jax.ShapeDtypeStruct((n_slots, d), F32),
        input_output_aliases={4: 0},
        compiler_params=_cparams(("arbitrary",), 32),
        name="dispatch",
    )(dest, xs, g2.reshape(1, d), modtab, init)


def _expert_kernel(te_ref, nu_ref, x_ref, w1_ref, w3_ref, w2_ref, o_ref, xb_s, acc_s):
    j = pl.program_id(0)
    f = pl.program_id(1)

    @pl.when(j < nu_ref[0])
    def _():
        @pl.when(f == 0)
        def _():
            xb_s[...] = _bf(x_ref[...])
            acc_s[...] = jnp.zeros(acc_s.shape, F32)

        h = xb_s[...]
        u = _silu(_dot(h, _bf(w1_ref[0]))) * _dot(h, _bf(w3_ref[0]))
        acc_s[...] += _dot(_bf(u), _bf(w2_ref[0]))

        @pl.when(f == pl.num_programs(1) - 1)
        def _():
            o_ref[...] = acc_s[...]

    @pl.when(jnp.logical_and(j >= nu_ref[0], f == pl.num_programs(1) - 1))
    def _():
        o_ref[...] = jnp.zeros(o_ref.shape, F32)


def _experts(tile_e, n_used, xsort, w1, w3, w2):
    n_slots, d = xsort.shape
    dff = w1.shape[2]
    tm, tf = MOE_TM, FFN_TF
    nf = dff // tf

    def jj(j, nu):
        return jnp.maximum(jnp.minimum(j, nu[0] - 1), 0)

    def ff(j, f, nu):
        return jnp.where(j < nu[0], f, nf - 1)

    grid_spec = pltpu.PrefetchScalarGridSpec(
        num_scalar_prefetch=2,
        grid=(n_slots // tm, nf),
        in_specs=[pl.BlockSpec((tm, d), lambda j, f, te, nu: (jj(j, nu), 0)),
                  pl.BlockSpec((1, d, tf), lambda j, f, te, nu: (te[jj(j, nu)], 0, ff(j, f, nu))),
                  pl.BlockSpec((1, d, tf), lambda j, f, te, nu: (te[jj(j, nu)], 0, ff(j, f, nu))),
                  pl.BlockSpec((1, tf, d), lambda j, f, te, nu: (te[jj(j, nu)], ff(j, f, nu), 0))],
        out_specs=pl.BlockSpec((tm, d), lambda j, f, te, nu: (j, 0)),
        scratch_shapes=[pltpu.VMEM((tm, d), BF16), pltpu.VMEM((tm, d), F32)],
    )
    return pl.pallas_call(
        _expert_kernel,
        grid_spec=grid_spec,
        out_shape=jax.ShapeDtypeStruct((n_slots, d), F32),
        compiler_params=_cparams(("arbitrary", "arbitrary"), 56),
        name="experts",
    )(tile_e, n_used, xsort, w1, w3, w2)


def _combine_kernel(dest_ref, y_ref, meta_ref, x_ref, g_ref, mod_ref, o_ref, buf_s, sem):
    i = pl.program_id(0)
    tm = x_ref.shape[0]

    def row_copy(r, k):
        src = dest_ref[TOP_K * (i * tm + r) + k]
        return pltpu.make_async_copy(y_ref.at[pl.ds(src, 1)], buf_s.at[k, pl.ds(r, 1)], sem)

    def start(r, carry):
        for k in range(TOP_K):
            row_copy(r, k).start(priority=k)
        return carry

    lax.fori_loop(0, tm, start, 0)

    def wait(r, carry):
        for k in range(TOP_K):
            row_copy(r, k).wait()
        return carry

    lax.fori_loop(0, tm, wait, 0)

    meta = meta_ref[...]
    lane = lax.broadcasted_iota(jnp.int32, meta.shape, 1)
    g0 = jnp.sum(jnp.where(lane == _META_G0, meta, 0.0), axis=1, keepdims=True)
    g1 = jnp.sum(jnp.where(lane == _META_G1, meta, 0.0), axis=1, keepdims=True)
    y = buf_s[0] * g0 + buf_s[1] * g1
    m = mod_ref[0]
    o_ref[...] = x_ref[...] + m[5:6] * _rms(y, g_ref[...])


def _combine(dest, yb, meta, xs, n_rows, g3, modtab, lat_rows):
    d = xs.shape[1]
    nb = modtab.shape[0] - 1
    tm = ROW_TILE
    mi = _mod_index(tm, lat_rows, nb)
    grid_spec = pltpu.PrefetchScalarGridSpec(
        num_scalar_prefetch=1,
        grid=(n_rows // tm,),
        in_specs=[pl.BlockSpec(memory_space=pl.ANY),
                  pl.BlockSpec((tm, 128), lambda i, dst: (i, 0)),
                  pl.BlockSpec((tm, d), lambda i, dst: (i, 0)),
                  pl.BlockSpec((1, d), lambda i, dst: (0, 0)),
                  pl.BlockSpec((1, 6, d), lambda i, dst: (mi(i), 0, 0))],
        out_specs=pl.BlockSpec((tm, d), lambda i, dst: (i, 0)),
        scratch_shapes=[pltpu.VMEM((TOP_K, tm, d), F32), pltpu.SemaphoreType.DMA(())],
    )
    return pl.pallas_call(
        _combine_kernel,
        grid_spec=grid_spec,
        out_shape=jax.ShapeDtypeStruct((n_rows, d), F32),
        compiler_params=_cparams(("arbitrary",), 40),
        name="combine",
    )(dest, yb, meta, xs, g3.reshape(1, d), modtab)


def _moe(xs, n_rows, g2, g3, modtab, w_router, w1, w3, w2, lat_rows):
    meta, cnt = _router(xs, n_rows, g2, modtab, w_router, lat_rows)
    counts = cnt[0, :N_EXPERTS].astype(jnp.int32)
    padded = (counts + MOE_TM - 1) // MOE_TM * MOE_TM
    pad_end = jnp.cumsum(padded)
    pad_start = pad_end - padded
    e = meta[:, _META_E0:_META_E1 + 1].astype(jnp.int32)
    r = meta[:, _META_R0:_META_R1 + 1].astype(jnp.int32)
    dest = (pad_start[e] + r).reshape(-1)
    n_tiles = (n_rows * TOP_K) // MOE_TM + N_EXPERTS
    n_slots = n_tiles * MOE_TM
    tile_start = jnp.arange(n_tiles, dtype=jnp.int32) * MOE_TM
    tile_e = jnp.minimum(jnp.sum((tile_start[:, None] >= pad_end[None, :]).astype(jnp.int32), axis=1),
                         N_EXPERTS - 1)
    n_used = (pad_end[-1:] // MOE_TM).astype(jnp.int32)
    xsort = _dispatch(dest, xs, n_rows, g2, modtab, n_slots, lat_rows)
    yb = _experts(tile_e, n_used, xsort, w1, w3, w2)
    return _combine(dest, yb, meta, xs, n_rows, g3, modtab, lat_rows)


def _rotary_tables(lat_len, ctx_len):
    rows = lat_len // GRID_W
    row = jnp.repeat(jnp.arange(rows, dtype=F32), GRID_W)
    col = jnp.tile(jnp.arange(GRID_W, dtype=F32), rows)
    n_freq = RET_DK // 4
    inv = ROPE_BASE ** (-jnp.arange(n_freq, dtype=F32) / n_freq)
    ang = jnp.concatenate([row[:, None] * inv, col[:, None] * inv], axis=-1)
    cos, sin = jnp.cos(ang), jnp.sin(ang)
    cosf = jnp.concatenate([jnp.ones((ctx_len, RET_DK), F32), jnp.concatenate([cos, cos], -1)], 0)
    sinf = jnp.concatenate([jnp.zeros((ctx_len, RET_DK), F32), jnp.concatenate([-sin, sin], -1)], 0)
    return cosf, sinf


def kernel(x, c, ctx, c_ctx, w_ada, b_ada, norm_g, w_in, w_out, ret_decay, ret_gn, hgrn_lb, hgrn_gn,
           mlstm_conv, mlstm_gate_b, mlstm_gn, w_ffn1, w_ffn3, w_ffn2, w_router, w_exp1, w_exp3, w_exp2):
    nb, lat_len, d = x.shape
    ctx_len = ctx.shape[1]
    depth = w_ada.shape[0]
    n_lat_rows = nb * lat_len
    n_rows = n_lat_rows + nb * ctx_len

    xs = jnp.concatenate([x.reshape(n_lat_rows, d), ctx.reshape(nb * ctx_len, d)], axis=0)
    s_in = jnp.zeros((8, d), F32).at[:nb].set(c).at[nb].set(c_ctx)
    mod_all = _ada(s_in, w_ada, b_ada)[:, :nb + 1].reshape(depth, nb + 1, 6, d)

    cosf, sinf = _rotary_tables(lat_len, ctx_len)
    hg_cum, hg_masks, hg_signs = _hgrn_constants()
    hg_consts = (jnp.asarray(hg_cum, BF16), jnp.asarray(hg_masks, F32), jnp.asarray(hg_signs, F32))
    sm = jax.nn.softmax(hgrn_lb.astype(F32), axis=0)
    lb_all = jnp.clip(jnp.cumsum(sm, axis=0) - sm[0], 0.0, 1.0)

    hb = None
    for layer in range(depth):
        last = layer == depth - 1
        modtab = mod_all[layer]
        g = norm_g[layer]

        if hb is None:
            hb = _prenorm(xs, n_rows, g[0], modtab, 1, 0, lat_len, BF16)
        w_in_l = w_in[layer]
        p = _mm(hb, _bf(w_in_l[:, :PROJ_MAIN]), MM_TM, MM_TN)
        w_g = jnp.zeros((d, 128), BF16).at[:, :MLSTM_GATES].set(_bf(w_in_l[:, PROJ_MAIN:]))
        gates = _mm(hb, w_g, MM_TM, 128)[:, :MLSTM_GATES]
        hb = None
        g_row_lat = jnp.swapaxes(gates[:n_lat_rows].reshape(nb, lat_len, MLSTM_GATES), 1, 2)
        g_row_ctx = jnp.swapaxes(gates[n_lat_rows:].reshape(nb, ctx_len, MLSTM_GATES), 1, 2)

        log_g = jax.nn.log_sigmoid(ret_decay[layer].astype(F32))
        lg_b = jnp.broadcast_to(log_g[:, :, None, None], (2, RET_HEADS, 8, RET_DV))
        lb = lb_all[layer]
        emit_ctx = not last
        o_ret = _retention(p, nb, lat_len, ctx_len, cosf, sinf, lg_b, ret_gn[layer], emit_ctx)
        o_hg = _hgrn(p, nb, lat_len, ctx_len, jnp.log(lb), jnp.log1p(-lb), hgrn_gn[layer],
                     hg_consts, emit_ctx)
        o_ml = _mlstm(p, gates, g_row_lat, g_row_ctx, nb, lat_len, ctx_len, mlstm_gate_b[layer],
                      mlstm_conv[layer], mlstm_gn[layer], emit_ctx)
        mix_lat = [o_ret[0], o_hg[0], o_ml[0]]
        mix_ctx = [o_ret[1], o_hg[1], o_ml[1]] if emit_ctx else None
        rows_now = n_lat_rows if last else n_rows
        dense = layer % 2 == 0
        res = _wout(mix_lat, mix_ctx, _bf(w_out[layer]), xs, rows_now, g[1], modtab, lat_len,
                    g[2] if dense else None)
        xs = res[0]

        j = layer // 2
        if dense:
            nxt = (None, None) if last else (norm_g[layer + 1][0], mod_all[layer + 1])
            res = _ffn(res[1], _bf(w_ffn1[j]), _bf(w_ffn3[j]), _bf(w_ffn2[j]), xs, g[3], modtab, lat_len, *nxt)
            xs = res[0]
            hb = None if last else res[1]
        else:
            xs = _moe(xs, rows_now, g[2], g[3], modtab, w_router[j], w_exp1[j], w_exp3[j], w_exp2[j], lat_len)
    return xs[:n_lat_rows].reshape(nb, lat_len, d)
```

```python
import functools

import numpy as np
import jax
import jax.numpy as jnp
from jax import lax
from jax.experimental import pallas as pl
from jax.experimental.pallas import tpu as pltpu

F32 = jnp.float32
BF16 = jnp.bfloat16

CHUNK = 128
NORM_EPS = 1e-6
ROPE_BASE = 10000.0
GRID_W = 64

RET_HEADS, RET_DK, RET_DV = 4, 128, 256
HGRN_HEADS, HGRN_DK, HGRN_DV = 4, 128, 128
MLSTM_HEADS, MLSTM_DK, MLSTM_DV = 4, 128, 128
N_EXPERTS = 8
TOP_K = 2

RET_QK = RET_HEADS * RET_DK
RET_WIDTH = RET_HEADS * RET_DV
HGRN_QK = HGRN_HEADS * HGRN_DK
HGRN_WIDTH = HGRN_HEADS * HGRN_DV
MLSTM_QK = MLSTM_HEADS * MLSTM_DK
MLSTM_WIDTH = MLSTM_HEADS * MLSTM_DV
MLSTM_GATES = 4 * MLSTM_HEADS

_C_RET_Q = 0
_C_RET_K = _C_RET_Q + RET_QK // 128
_C_RET_V = _C_RET_K + RET_QK // 128
_C_RET_G = _C_RET_V + RET_WIDTH // 128
_C_HG_Q = _C_RET_G + RET_WIDTH // 128
_C_HG_FF = _C_HG_Q + HGRN_QK // 128
_C_HG_FB = _C_HG_FF + HGRN_QK // 128
_C_HG_I = _C_HG_FB + HGRN_QK // 128
_C_HG_G = _C_HG_I + HGRN_WIDTH // 128
_C_ML_Q = _C_HG_G + HGRN_WIDTH // 128
_C_ML_K = _C_ML_Q + MLSTM_QK // 128
_C_ML_V = _C_ML_K + MLSTM_QK // 128
_C_ML_O = _C_ML_V + MLSTM_WIDTH // 128
PROJ_MAIN = (_C_ML_O + MLSTM_WIDTH // 128) * 128

_VMEM_CAP_BYTES = 56 * 1024 * 1024

ROW_TILE = 512
MM_TM, MM_TN = 1024, 512
FFN_TF = 512
MOE_TM = 512
RET_UNROLL = (4, 4)
HGRN_UNROLL = (2, 4)
MLSTM_UNROLL = (2, 2)
DMA_UNROLL = 8


def _cparams(sem, vmem_mb):
    return pltpu.CompilerParams(dimension_semantics=sem,
                                vmem_limit_bytes=min(int(vmem_mb * 1024 * 1024), _VMEM_CAP_BYTES))


def _bf(x):
    return x.astype(BF16)


def _dot(a, b):
    return jnp.dot(a, b, preferred_element_type=F32)


def _dot_nt(a, b):
    return lax.dot_general(a, b, (((1,), (1,)), ((), ())), preferred_element_type=F32)


def _dot_tn(a, b):
    return lax.dot_general(a, b, (((0,), (0,)), ((), ())), preferred_element_type=F32)


def _sigmoid(x):
    return 1.0 / (1.0 + jnp.exp(-x))


def _silu(x):
    return x * _sigmoid(x)


def _log_sigmoid(x):
    return jnp.minimum(x, 0.0) - jnp.log(1.0 + jnp.exp(-jnp.abs(x)))


def _rms(x, g):
    return x * lax.rsqrt(jnp.mean(x * x, axis=-1, keepdims=True) + NORM_EPS) * g


def _mod_index(tile_rows, n_lat_rows_per_batch, n_batch):
    return lambda i: jnp.minimum((i * tile_rows) // n_lat_rows_per_batch, n_batch)


def _ada_kernel(s_ref, w_ref, b_ref, o_ref):
    s = _bf(_silu(s_ref[...]))
    o_ref[0] = _dot(s, _bf(w_ref[0])) + b_ref[0]


def _ada(s_in, w_ada, b_ada):
    depth, d, n = w_ada.shape
    tn = 1024
    return pl.pallas_call(
        _ada_kernel,
        grid=(depth, n // tn),
        in_specs=[pl.BlockSpec((8, d), lambda l, j: (0, 0)),
                  pl.BlockSpec((1, d, tn), lambda l, j: (l, 0, j)),
                  pl.BlockSpec((1, 1, tn), lambda l, j: (l, 0, j))],
        out_specs=pl.BlockSpec((1, 8, tn), lambda l, j: (l, 0, j)),
        out_shape=jax.ShapeDtypeStruct((depth, 8, n), F32),
        compiler_params=_cparams(("arbitrary", "arbitrary"), 40),
        name="ada",
    )(s_in, w_ada, b_ada.reshape(depth, 1, n))


def _part_specs(parts, tm, width):
    if len(parts) == 1:
        return [pl.BlockSpec((tm, width), lambda i: (i, 0))]
    n0 = parts[0].shape[0] // tm
    return [pl.BlockSpec((tm, width), lambda i: (jnp.minimum(i, n0 - 1), 0)),
            pl.BlockSpec((tm, width), lambda i: (jnp.maximum(i - n0, 0), 0))]


def _on_part(n_first, n_parts, fn):
    if n_parts == 1:
        fn(0)
    else:
        i = pl.program_id(0)
        pl.when(i < n_first)(lambda: fn(0))
        pl.when(i >= n_first)(lambda: fn(1))


def _prenorm_kernel(*refs, sc, sh, n_first):
    x_refs, (g_ref, mod_ref, o_ref) = refs[:-3], refs[-3:]

    def run(part):
        m = mod_ref[0]
        y = _rms(x_refs[part][...], g_ref[...])
        o_ref[...] = (y * (1.0 + m[sc:sc + 1]) + m[sh:sh + 1]).astype(o_ref.dtype)

    _on_part(n_first, len(x_refs), run)


def _prenorm(x_parts, n_rows, g, modtab, sc, sh, lat_rows, out_dtype):
    d = x_parts[0].shape[1]
    nb = modtab.shape[0] - 1
    mi = _mod_index(ROW_TILE, lat_rows, nb)
    return pl.pallas_call(
        functools.partial(_prenorm_kernel, sc=sc, sh=sh, n_first=x_parts[0].shape[0] // ROW_TILE),
        grid=(n_rows // ROW_TILE,),
        in_specs=[*_part_specs(x_parts, ROW_TILE, d),
                  pl.BlockSpec((1, d), lambda i: (0, 0)),
                  pl.BlockSpec((1, 6, d), lambda i: (mi(i), 0, 0))],
        out_specs=pl.BlockSpec((ROW_TILE, d), lambda i: (i, 0)),
        out_shape=jax.ShapeDtypeStruct((n_rows, d), out_dtype),
        compiler_params=_cparams(("arbitrary",), 32),
        name="prenorm",
    )(*x_parts, g.reshape(1, d), modtab)


def _mm_kernel(x_ref, w_ref, o_ref):
    o_ref[...] = _dot(x_ref[...], w_ref[...]).astype(o_ref.dtype)


def _mm(x, w, tm, tn, n_cols=None, out_dtype=F32):
    m, k = x.shape
    n = w.shape[1] if n_cols is None else n_cols
    return pl.pallas_call(
        _mm_kernel,
        grid=(m // tm, n // tn),
        in_specs=[pl.BlockSpec((tm, k), lambda i, j: (i, 0)),
                  pl.BlockSpec((k, tn), lambda i, j: (0, j))],
        out_specs=pl.BlockSpec((tm, tn), lambda i, j: (i, j)),
        out_shape=jax.ShapeDtypeStruct((m, n), out_dtype),
        compiler_params=_cparams(("arbitrary", "arbitrary"), 40),
        name="proj",
    )(x, w)


def _bwd_chunk(i, n_ctx, n_lat):
    return jnp.where(i < n_ctx, n_ctx - 1 - i, 2 * n_ctx + n_lat - 1 - i)


def _rows(c):
    return pl.ds(pl.multiple_of(c * CHUNK, CHUNK), CHUNK)


def _for_chunks(n_ctx, n_lat, fn, unroll, with_ctx=True):
    def run(part, n, off):
        def body(c, carry):
            fn(part, c, c + off)
            return carry
        lax.fori_loop(0, n, body, 0, unroll=max(u for u in (1, 2, unroll) if n % u == 0 and u <= unroll))
    if with_ctx:
        run(0, n_ctx, 0)
    run(1, n_lat, n_ctx)


def _lanes(d, w):
    return slice(d * w, (d + 1) * w)


def _mixer_call(kernel_fn, name, in_specs, args, nb, heads, lat_len, ctx_len, dv, scratch, emit_ctx, vmem_mb):
    width = heads * dv
    out_specs = [pl.BlockSpec((lat_len, dv), lambda b, h: (b, h))]
    out_shape = [jax.ShapeDtypeStruct((nb * lat_len, width), BF16)]
    if emit_ctx:
        out_specs.append(pl.BlockSpec((ctx_len, dv), lambda b, h: (b, h)))
        out_shape.append(jax.ShapeDtypeStruct((nb * ctx_len, width), BF16))
    n_in = len(in_specs)

    def body(*refs):
        ins, rest = refs[:n_in], refs[n_in:]
        if emit_ctx:
            ol, oc, scr = rest[0], rest[1], rest[2:]
        else:
            ol, oc, scr = rest[0], None, rest[1:]
        kernel_fn(*ins, ol, oc, *scr, n_ctx=ctx_len // CHUNK, n_lat=lat_len // CHUNK)

    return pl.pallas_call(
        body,
        grid=(nb, heads),
        in_specs=in_specs,
        out_specs=out_specs,
        out_shape=out_shape,
        scratch_shapes=scratch,
        compiler_params=_cparams(("arbitrary", "arbitrary"), vmem_mb),
        name=name,
    )(*args)


def _seq_specs(col0, width, nb, lat_len, ctx_len):
    n_lat_blk = nb * lat_len // ctx_len
    c0 = col0 * 128 // width
    return (pl.BlockSpec((lat_len, width), lambda b, h: (b, c0 + h)),
            pl.BlockSpec((ctx_len, width), lambda b, h: (n_lat_blk + b, c0 + h)))


def _ret_kernel(ql, qc, kl, kc, vl, vc, gl, gc, cos_ref, sin_ref, lg_ref, gn_ref, ol_ref, oc_ref,
                qs_s, kv_s, sall_s, o_s, st_s, dm_s, dq_s, dk_s, *, n_ctx, n_lat):
    nch = n_ctx + n_lat
    unroll_a, unroll_c = RET_UNROLL
    qr, kr, vr, gr, outr = (qc, ql), (kc, kl), (vc, vl), (gc, gl), (oc_ref, ol_ref)
    t_i = lax.broadcasted_iota(jnp.int32, (CHUNK, CHUNK), 0).astype(F32)
    s_i = lax.broadcasted_iota(jnp.int32, (CHUNK, CHUNK), 1).astype(F32)
    dm = None
    for d in range(2):
        lg = lg_ref[d, 0][0:1, 0:CHUNK]
        rel = (t_i - s_i) if d == 0 else (s_i - t_i)
        dmd = jnp.where(rel >= 0, jnp.exp(jnp.maximum(rel, 0.0) * lg), 0.0)
        dm = dmd if dm is None else dm + dmd
        p = t_i if d == 0 else (CHUNK - 1.0) - t_i
        dq_s[d] = jnp.exp((p + 1.0) * lg)
        dk_s[d] = jnp.exp((CHUNK - 1.0 - p) * lg)
    dm_s[...] = dm
    scale = RET_DK ** -0.5

    def phase_a(part, cl, cg):
        rl, rg = _rows(cl), _rows(cg)
        cs = cos_ref[rg, :]
        sn = sin_ref[rg, :]
        q = qr[part][rl, :]
        k = kr[part][rl, :]
        q = q * cs + pltpu.roll(q, RET_DK // 2, 1) * sn
        k = (k * cs + pltpu.roll(k, RET_DK // 2, 1) * sn) * scale
        vb = _bf(vr[part][rl, :])
        s = _dot_nt(_bf(q), _bf(k))
        o_s[rg, :] = _dot(_bf(s * dm_s[...]), vb)
        qs_s[rg, :] = jnp.concatenate([_bf(q * dq_s[0]), _bf(q * dq_s[1])], axis=1)
        kd = jnp.concatenate([_bf(k * dk_s[0]), _bf(k * dk_s[1])], axis=1)
        kv_s[cg] = _dot_tn(kd, vb)

    _for_chunks(n_ctx, n_lat, phase_a, unroll_a)

    st_s[...] = jnp.zeros(st_s.shape, F32)
    dec = [jnp.exp(float(CHUNK) * lg_ref[d, 0][0:1, :]) for d in range(2)]

    def phase_b(i, carry):
        for d, c in ((0, i), (1, _bwd_chunk(i, n_ctx, n_lat))):
            st = st_s[d]
            sall_s[c, _lanes(d, RET_DK), :] = _bf(st)
            st_s[d] = dec[d] * st + kv_s[c, _lanes(d, RET_DK), :]
        return carry

    lax.fori_loop(0, nch, phase_b, 0)
    gn = gn_ref[...]

    def phase_c(part, cl, cg):
        rl, rg = _rows(cl), _rows(cg)
        o = o_s[rg, :] + _dot(qs_s[rg, :], sall_s[cg])
        outr[part][rl, :] = (_rms(o, gn) * _silu(gr[part][rl, :])).astype(BF16)

    _for_chunks(n_ctx, n_lat, phase_c, unroll_c, with_ctx=oc_ref is not None)


def _retention(p, nb, lat_len, ctx_len, cosf, sinf, lg_b, gn, emit_ctx):
    t_len = lat_len + ctx_len
    nch = t_len // CHUNK
    s128 = functools.partial(_seq_specs, width=128, nb=nb, lat_len=lat_len, ctx_len=ctx_len)
    s256 = functools.partial(_seq_specs, width=256, nb=nb, lat_len=lat_len, ctx_len=ctx_len)
    in_specs = [*s128(_C_RET_Q), *s128(_C_RET_K), *s256(_C_RET_V), *s256(_C_RET_G),
                pl.BlockSpec((t_len, 128), lambda b, h: (0, 0)),
                pl.BlockSpec((t_len, 128), lambda b, h: (0, 0)),
                pl.BlockSpec((2, 1, 8, RET_DV), lambda b, h: (0, h, 0, 0)),
                pl.BlockSpec((1, RET_DV), lambda b, h: (0, h))]
    scratch = [pltpu.VMEM((t_len, 2 * RET_DK), BF16),
               pltpu.VMEM((nch, 2 * RET_DK, RET_DV), F32),
               pltpu.VMEM((nch, 2 * RET_DK, RET_DV), BF16),
               pltpu.VMEM((t_len, RET_DV), F32),
               pltpu.VMEM((2, RET_DK, RET_DV), F32),
               pltpu.VMEM((CHUNK, CHUNK), F32),
               pltpu.VMEM((2, CHUNK, CHUNK), F32), pltpu.VMEM((2, CHUNK, CHUNK), F32)]
    args = (p, p, p, p, p, p, p, p, cosf, sinf, lg_b, gn.reshape(1, RET_WIDTH))
    return _mixer_call(_ret_kernel, "retention", in_specs, args, nb, RET_HEADS, lat_len, ctx_len, RET_DV,
                       scratch, emit_ctx, 48)


_HG_LEVELS = (64, 32, 16, 8, 4, 2, 1)


def _hgrn_constants():
    c = CHUNK
    t = np.arange(c)[:, None]
    u = np.arange(c)[None, :]
    cum = (u <= t).astype(np.float32)
    masks, signs = [], []
    for m in _HG_LEVELS:
        base = (t // (2 * m)) * (2 * m)
        lower = t >= base + m
        tb = t // (2 * m)
        sb = u // (2 * m)
        masks.append((tb == sb) & lower & (u < (sb * 2 * m + m)))
        signs.append(np.broadcast_to(np.where(lower, 1.0, -1.0), (c, c)))
    masks.append(t == u)
    kf = np.stack([x.astype(np.float32) for x in masks], axis=0)
    kb = np.stack([x.astype(np.float32)[::-1, ::-1] for x in masks], axis=0)
    sf = np.stack([x.astype(np.float32) for x in signs], axis=0)
    sb_ = np.stack([x.astype(np.float32)[::-1, ::-1] for x in signs], axis=0)
    return (np.stack([cum, cum[::-1, ::-1]], 0), np.stack([kf, kb], 0),
            np.stack([sf[:_HG_WIDE], sb_[:_HG_WIDE]], 0))


_HG_WIDE = 5


def _hgrn_level_exponents(b, lf, d, sgn_ref, row):
    out = []
    for l, m in enumerate(_HG_LEVELS[:_HG_WIDE]):
        pieces = []
        for j in range(CHUNK // (2 * m)):
            r = 2 * m * j + (m - 1 if d == 0 else m)
            pieces.append(jnp.broadcast_to(b[r:r + 1, :], (2 * m, CHUNK)))
        bref = pieces[0] if len(pieces) == 1 else jnp.concatenate(pieces, axis=0)
        out.append((b - bref) * sgn_ref[d, l])
    up = pltpu.roll(lf, CHUNK - 1, 0)
    dn = pltpu.roll(lf, 1, 0)
    r4 = row % 4
    if d == 0:
        e2 = jnp.where(r4 == 0, up, jnp.where(r4 == 1, 0.0, jnp.where(r4 == 2, lf, lf + dn)))
        e1 = jnp.where(row % 2 == 1, lf, 0.0)
    else:
        e2 = jnp.where(r4 == 0, lf + up, jnp.where(r4 == 1, lf, jnp.where(r4 == 2, 0.0, dn)))
        e1 = jnp.where(row % 2 == 0, lf, 0.0)
    return out + [e2, e1]


def _hgrn_kernel(ql, qc, ffl, ffc, fbl, fbc, il, ic, gl, gc, llb_ref, lub_ref, gn_ref, c_ref, k_ref, sgn_ref,
                 ol_ref, oc_ref, qs_s, kv_s, sall_s, dec_s, o_s, st_s, *, n_ctx, n_lat):
    nch = n_ctx + n_lat
    unroll_a, unroll_c = HGRN_UNROLL
    qr, fr, ir, gr, outr =(qc, ql), ((ffc, ffl), (fbc, fbl)), (ic, il), (gc, gl), (oc_ref, ol_ref)
    llb = llb_ref[...]
    lub = lub_ref[...]
    nlev = len(_HG_LEVELS)
    row = lax.broadcasted_iota(jnp.int32, (CHUNK, HGRN_DK), 0)

    def phase_a(part, cl, cg):
        rl, rg = _rows(cl), _rows(cg)
        q = _silu(qr[part][rl, :])
        qb = _bf(q)
        vb = _bf(ir[part][rl, :])
        a_sum, qs, kds = None, [], []
        for d in range(2):
            lsg = lub + _log_sigmoid(fr[d][part][rl, :])
            lf = jnp.maximum(llb, lsg) + jnp.log(1.0 + jnp.exp(-jnp.abs(llb - lsg)))
            k = 1.0 - jnp.exp(lf)
            lf_hi = _bf(lf)
            lf_lo = _bf(lf - lf_hi.astype(F32))
            b2 = _dot(c_ref[d], jnp.concatenate([lf_hi, lf_lo], axis=1))
            b = b2[:, 0:HGRN_DK] + b2[:, HGRN_DK:]
            es = _hgrn_level_exponents(b, lf, d, sgn_ref, row)
            kb = _bf(k)
            a = k_ref[d, nlev] * _dot_nt(qb, kb)
            for l in range(nlev):
                xb = _bf(jnp.exp(es[l]))
                a = a + k_ref[d, l] * _dot_nt(qb * xb, kb * xb)
            a_sum = a if a_sum is None else a_sum + a
            b_last = b[CHUNK - 1:CHUNK] if d == 0 else b[0:1]
            qs.append(_bf(q * jnp.exp(b)))
            kds.append(_bf(k * jnp.exp(b_last - b)))
            dec_s[cg, :, _lanes(d, HGRN_DK)] = jnp.exp(b_last)
        o_s[rg, :] = _dot(_bf(a_sum), vb)
        qs_s[rg, :] = jnp.concatenate(qs, axis=1)
        kv_s[cg] = _dot_tn(vb, jnp.concatenate(kds, axis=1))

    _for_chunks(n_ctx, n_lat, phase_a, unroll_a)
    st_s[...] = jnp.zeros(st_s.shape, F32)

    def phase_b(i, carry):
        for d, c in ((0, i), (1, _bwd_chunk(i, n_ctx, n_lat))):
            ln = _lanes(d, HGRN_DK)
            st = st_s[d]
            sall_s[c, :, ln] = _bf(st)
            st_s[d] = dec_s[c, :, ln] * st + kv_s[c, :, ln]
        return carry

    lax.fori_loop(0, nch, phase_b, 0)
    gn = gn_ref[...]

    def phase_c(part, cl, cg):
        rl, rg = _rows(cl), _rows(cg)
        o = o_s[rg, :] + _dot_nt(qs_s[rg, :], sall_s[cg])
        outr[part][rl, :] = (_rms(o, gn) * _silu(gr[part][rl, :])).astype(BF16)

    _for_chunks(n_ctx, n_lat, phase_c, unroll_c, with_ctx=oc_ref is not None)


def _hgrn(p, nb, lat_len, ctx_len, log_lb, log_ub, gn, consts, emit_ctx):
    mats, masks, signs = consts
    t_len = lat_len + ctx_len
    nch = t_len // CHUNK
    spec = functools.partial(_seq_specs, width=128, nb=nb, lat_len=lat_len, ctx_len=ctx_len)
    vec = pl.BlockSpec((1, 128), lambda b, h: (0, h))
    in_specs = [*spec(_C_HG_Q), *spec(_C_HG_FF), *spec(_C_HG_FB), *spec(_C_HG_I), *spec(_C_HG_G),
                vec, vec, vec,
                pl.BlockSpec(mats.shape, lambda b, h: (0, 0, 0)),
                pl.BlockSpec(masks.shape, lambda b, h: (0, 0, 0, 0)),
                pl.BlockSpec(signs.shape, lambda b, h: (0, 0, 0, 0))]
    scratch = [pltpu.VMEM((t_len, 2 * HGRN_DK), BF16),
               pltpu.VMEM((nch, HGRN_DV, 2 * HGRN_DK), F32),
               pltpu.VMEM((nch, HGRN_DV, 2 * HGRN_DK), BF16),
               pltpu.VMEM((nch, 1, 2 * HGRN_DK), F32),
               pltpu.VMEM((t_len, HGRN_DV), F32),
               pltpu.VMEM((2, HGRN_DV, HGRN_DK), F32)]
    args = (p, p, p, p, p, p, p, p, p, p, log_lb.reshape(1, HGRN_QK), log_ub.reshape(1, HGRN_QK),
            gn.reshape(1, HGRN_WIDTH), mats, masks, signs)
    return _mixer_call(_hgrn_kernel, "hgrn2", in_specs, args, nb, HGRN_HEADS, lat_len, ctx_len, HGRN_DV,
                       scratch, emit_ctx, 48)


def _mlstm_kernel(ql, qc, kl, kc, vl, vc, ogl, ogc, gcl, gcc, grl, grc, bc_ref, br_ref,
                  wq_ref, wk_ref, gn_ref, ol_ref, oc_ref,
                  q_s, gr_s, intra_s, kv_s, call_s, rho_s, bcol_s, bl_s, mu_s, mprev_s, st_s, m_s,
                  *, n_ctx, n_lat):
    nch = n_ctx + n_lat
    unroll_a, unroll_c = MLSTM_UNROLL
    h = pl.program_id(1)
    nh = MLSTM_HEADS
    dk, dv = MLSTM_DK, MLSTM_DV
    ext = 2 * dv
    qr, kr, vr, ogr, gcr, outr = (qc, ql), (kc, kl), (vc, vl), (ogc, ogl), (gcc, gcl), (oc_ref, ol_ref)
    n_loc = (n_ctx, n_lat)

    for cc in range(nch):
        src, c0 = (grc, cc) if cc < n_ctx else (grl, cc - n_ctx)
        gr_s[cc] = src[0, :, c0 * CHUNK:(c0 + 1) * CHUNK] + br_ref[...]

    row = lax.broadcasted_iota(jnp.int32, (CHUNK, dk), 0)
    lane = lax.broadcasted_iota(jnp.int32, (CHUNK, dv), 1)
    ones_col = jnp.where(lane == 0, 1.0, 0.0).astype(BF16)
    scale = dk ** -0.5
    t_i = lax.broadcasted_iota(jnp.int32, (CHUNK, CHUNK), 0)
    s_i = lax.broadcasted_iota(jnp.int32, (CHUNK, CHUNK), 1)
    lane16 = lax.broadcasted_iota(jnp.int32, (CHUNK, MLSTM_GATES), 1)
    sub16 = lax.broadcasted_iota(jnp.int32, (MLSTM_GATES, CHUNK), 0)

    def conv(src, w_ref, part, cl):
        r0 = pl.multiple_of(cl * CHUNK, CHUNK)
        n_rows = n_loc[part] * CHUNK
        x = src[pl.ds(r0, CHUNK), :]
        pr = src[pl.ds(jnp.maximum(r0 - 1, 0), 1), :]
        nx = src[pl.ds(jnp.minimum(r0 + CHUNK, n_rows - 1), 1), :]
        pr = jnp.where(cl != 0, pr, 0.0)
        nx = jnp.where(cl != n_loc[part] - 1, nx, 0.0)
        xp = jnp.where(row == 0, pr, pltpu.roll(x, 1, 0))
        xn = jnp.where(row == CHUNK - 1, nx, pltpu.roll(x, CHUNK - 1, 0))
        w = w_ref[...]
        return _silu(w[0:1] * xp + w[1:2] * x + w[2:3] * xn)

    def pick_col(g, j):
        return jnp.sum(jnp.where(lane16 == j, g, 0.0), axis=1, keepdims=True)

    def pick_row(g, j):
        return jnp.sum(jnp.where(sub16 == j, g, 0.0), axis=0, keepdims=True)

    def phase_a(part, cl, cg):
        rl, rg = _rows(cl), _rows(cg)
        q = conv(qr[part], wq_ref, part, cl)
        k = conv(kr[part], wk_ref, part, cl) * scale
        qb = _bf(q)
        q_s[rg, :] = qb
        vb = jnp.concatenate([_bf(vr[part][rl, :]), ones_col], axis=1)
        s = _dot_nt(qb, _bf(k))
        g_c = gcr[part][rl, :] + bc_ref[...]
        g_r = gr_s[cg]
        kws = []
        for d in range(2):
            ig_c = pick_col(g_c, d * nh + h)
            lf_c = _log_sigmoid(pick_col(g_c, 2 * nh + d * nh + h))
            ig_r = pick_row(g_r, d * nh + h)
            lf_r = _log_sigmoid(pick_row(g_r, 2 * nh + d * nh + h))
            tri = (s_i <= t_i) if d == 0 else (s_i >= t_i)
            tri_t = (t_i <= s_i) if d == 0 else (t_i >= s_i)
            b_c = jnp.sum(jnp.where(tri, lf_r, 0.0), axis=1, keepdims=True)
            b_r = jnp.sum(jnp.where(tri_t, lf_c, 0.0), axis=0, keepdims=True)
            dmat = jnp.where(tri, b_c - b_r + ig_r, -jnp.inf)
            rho = jnp.max(dmat, axis=1, keepdims=True)
            intra_s[d, rg, :] = _dot(_bf(s * jnp.exp(dmat - rho)), vb)
            b_last = b_c[CHUNK - 1:CHUNK] if d == 0 else b_c[0:1]
            mu = jnp.max(b_last - b_r + ig_r, axis=1, keepdims=True)
            kws.append(_bf(k * jnp.exp(b_last - b_c + ig_c - mu)))
            rho_s[d, rg, :] = rho
            bcol_s[d, rg, :] = b_c
            bl_s[cg, :, _lanes(d, 128)] = jnp.broadcast_to(b_last, (1, 128))
            mu_s[cg, :, _lanes(d, 128)] = jnp.broadcast_to(mu, (1, 128))
        kv_s[cg] = _dot_tn(jnp.concatenate(kws, axis=1), vb)

    _for_chunks(n_ctx, n_lat, phase_a, unroll_a)
    st_s[...] = jnp.zeros(st_s.shape, F32)
    m_s[...] = jnp.zeros(m_s.shape, F32)

    def wide(v):
        return jnp.concatenate([v, v], axis=1)

    def phase_b(i, carry):
        for d, c in ((0, i), (1, _bwd_chunk(i, n_ctx, n_lat))):
            ln = _lanes(d, 128)
            m_prev = m_s[d]
            st = st_s[d]
            mprev_s[c, :, ln] = m_prev
            call_s[c, :, _lanes(d, ext)] = _bf(st)
            bl = bl_s[c, :, ln]
            mu = mu_s[c, :, ln]
            m_new = jnp.maximum(bl + m_prev, mu)
            st_s[d] = (wide(jnp.exp(bl + m_prev - m_new)) * st
                       + wide(jnp.exp(mu - m_new)) * kv_s[c, _lanes(d, dk), :])
            m_s[d] = m_new
        return carry

    lax.fori_loop(0, nch, phase_b, 0)
    gn = gn_ref[...]

    def phase_c(part, cl, cg):
        rl, rg = _rows(cl), _rows(cg)
        qc_all = _dot(q_s[rg, :], call_s[cg])
        o = None
        for d in range(2):
            rho = rho_s[d, rg, :]
            b_c = bcol_s[d, rg, :]
            m_prev = mprev_s[cg, :, _lanes(d, 128)][:, 0:1]
            m_t = jnp.maximum(rho, b_c + m_prev)
            nd = (jnp.exp(rho - m_t) * intra_s[d, rg, :]
                  + jnp.exp(b_c + m_prev - m_t) * qc_all[:, _lanes(d, ext)])
            hh = nd[:, 0:dv] / jnp.maximum(jnp.abs(nd[:, dv:dv + 1]), jnp.exp(-m_t))
            o = hh if o is None else o + hh
        y = o - jnp.mean(o, axis=-1, keepdims=True)
        y = y * lax.rsqrt(jnp.mean(y * y, axis=-1, keepdims=True) + NORM_EPS)
        outr[part][rl, :] = (y * gn * _sigmoid(ogr[part][rl, :])).astype(BF16)

    _for_chunks(n_ctx, n_lat, phase_c, unroll_c, with_ctx=oc_ref is not None)


def _mlstm(p, g_col, g_row_lat, g_row_ctx, nb, lat_len, ctx_len, gate_b, conv_w, gn, emit_ctx):
    t_len = lat_len + ctx_len
    nch = t_len // CHUNK
    n_lat_blk = nb * lat_len // ctx_len
    ng = MLSTM_GATES
    dk, dv = MLSTM_DK, MLSTM_DV
    spec = functools.partial(_seq_specs, width=128, nb=nb, lat_len=lat_len, ctx_len=ctx_len)
    in_specs = [*spec(_C_ML_Q), *spec(_C_ML_K), *spec(_C_ML_V), *spec(_C_ML_O),
                pl.BlockSpec((lat_len, ng), lambda b, h: (b, 0)),
                pl.BlockSpec((ctx_len, ng), lambda b, h: (n_lat_blk + b, 0)),
                pl.BlockSpec((1, ng, lat_len), lambda b, h: (b, 0, 0)),
                pl.BlockSpec((1, ng, ctx_len), lambda b, h: (b, 0, 0)),
                pl.BlockSpec((1, ng), lambda b, h: (0, 0)),
                pl.BlockSpec((ng, 1), lambda b, h: (0, 0)),
                pl.BlockSpec((3, 128), lambda b, h: (0, h)),
                pl.BlockSpec((3, 128), lambda b, h: (0, MLSTM_HEADS + h)),
                pl.BlockSpec((1, 128), lambda b, h: (0, h))]
    scratch = [pltpu.VMEM((t_len, dk), BF16),
               pltpu.VMEM((nch, ng, CHUNK), F32),
               pltpu.VMEM((2, t_len, 2 * dv), F32),
               pltpu.VMEM((nch, 2 * dk, 2 * dv), F32),
               pltpu.VMEM((nch, dk, 4 * dv), BF16),
               pltpu.VMEM((2, t_len, 1), F32), pltpu.VMEM((2, t_len, 1), F32),
               pltpu.VMEM((nch, 1, 256), F32), pltpu.VMEM((nch, 1, 256), F32), pltpu.VMEM((nch, 1, 256), F32),
               pltpu.VMEM((2, dk, 2 * dv), F32), pltpu.VMEM((2, 1, 128), F32)]
    args = (p, p, p, p, p, p, p, p, g_col, g_col, g_row_lat, g_row_ctx,
            gate_b.reshape(1, ng), gate_b.reshape(ng, 1), conv_w, conv_w, gn.reshape(1, MLSTM_WIDTH))
    return _mixer_call(_mlstm_kernel, "mlstm", in_specs, args, nb, MLSTM_HEADS, lat_len, ctx_len, dv,
                       scratch, emit_ctx, 48)


def _wout_kernel(*refs, n_lat_tiles, n_mix_parts, n_x_parts, emit_next):
    it = iter(refs)
    mix = [[next(it) for _ in range(n_mix_parts)] for _ in range(3)]
    x_parts = [next(it) for _ in range(n_x_parts)]
    wr_ref, wh_ref, wm_ref, g_ref, mod_ref = (next(it) for _ in range(5))
    g2_ref = next(it) if emit_next else None
    o_ref = next(it)
    h2_ref = next(it) if emit_next else None

    def run(part):
        pm = min(part, n_mix_parts - 1)
        y = (_dot(mix[0][pm][...], wr_ref[...]) + _dot(mix[1][pm][...], wh_ref[...])
             + _dot(mix[2][pm][...], wm_ref[...]))
        m = mod_ref[0]
        xn = x_parts[min(part, n_x_parts - 1)][...] + m[2:3] * _rms(y, g_ref[...])
        o_ref[...] = xn
        if emit_next:
            h2_ref[...] = (_rms(xn, g2_ref[...]) * (1.0 + m[4:5]) + m[3:4]).astype(h2_ref.dtype)

    _on_part(n_lat_tiles, max(n_mix_parts, n_x_parts), run)


def _wout(mix_parts, w_out, x_parts, n_rows, g1, modtab, lat_rows, g2_next):
    d = x_parts[0].shape[1]
    nb = modtab.shape[0] - 1
    mi = _mod_index(ROW_TILE, lat_rows, nb)
    tm = ROW_TILE
    n_lat_tiles = mix_parts[0][0].shape[0] // tm
    emit_next = g2_next is not None
    widths = (RET_WIDTH, HGRN_WIDTH, MLSTM_WIDTH)
    w_r = w_out[0:RET_WIDTH]
    w_h = w_out[RET_WIDTH:RET_WIDTH + HGRN_WIDTH]
    w_m = w_out[RET_WIDTH + HGRN_WIDTH:]
    full = lambda a: pl.BlockSpec(a.shape, lambda i: (0, 0))
    in_specs, args = [], []
    for parts, w in zip(mix_parts, widths):
        in_specs += _part_specs(parts, tm, w)
        args += list(parts)
    in_specs += _part_specs(x_parts, tm, d)
    args += list(x_parts)
    in_specs += [full(w_r), full(w_h), full(w_m),
                 pl.BlockSpec((1, d), lambda i: (0, 0)),
                 pl.BlockSpec((1, 6, d), lambda i: (mi(i), 0, 0))]
    args += [w_r, w_h, w_m, g1.reshape(1, d), modtab]
    out_specs = [pl.BlockSpec((tm, d), lambda i: (i, 0))]
    out_shape = [jax.ShapeDtypeStruct((n_rows, d), F32)]
    if emit_next:
        in_specs.append(pl.BlockSpec((1, d), lambda i: (0, 0)))
        args.append(g2_next.reshape(1, d))
        out_specs.append(pl.BlockSpec((tm, d), lambda i: (i, 0)))
        out_shape.append(jax.ShapeDtypeStruct((n_rows, d), BF16))
    return pl.pallas_call(
        functools.partial(_wout_kernel, n_lat_tiles=n_lat_tiles, n_mix_parts=len(mix_parts[0]),
                          n_x_parts=len(x_parts), emit_next=emit_next),
        grid=(n_rows // tm,),
        in_specs=in_specs,
        out_specs=out_specs,
        out_shape=out_shape,
        compiler_params=_cparams(("arbitrary",), 52),
        name="wout",
    )(*args)


def _ffn_kernel(*refs, emit_next):
    if emit_next:
        h_ref, w1_ref, w3_ref, w2_ref, x_ref, g_ref, mod_ref, gn_ref, modn_ref, o_ref, hn_ref, acc_ref = refs
    else:
        h_ref, w1_ref, w3_ref, w2_ref, x_ref, g_ref, mod_ref, o_ref, acc_ref = refs
    f = pl.program_id(1)

    @pl.when(f == 0)
    def _():
        acc_ref[...] = jnp.zeros(acc_ref.shape, F32)

    h = h_ref[...]
    u = _silu(_dot(h, w1_ref[...])) * _dot(h, w3_ref[...])
    acc_ref[...] += _dot(_bf(u), w2_ref[...])

    @pl.when(f == pl.num_programs(1) - 1)
    def _():
        m = mod_ref[0]
        xn = x_ref[...] + m[5:6] * _rms(acc_ref[...], g_ref[...])
        o_ref[...] = xn
        if emit_next:
            mn = modn_ref[0]
            hn_ref[...] = (_rms(xn, gn_ref[...]) * (1.0 + mn[1:2]) + mn[0:1]).astype(hn_ref.dtype)


def _ffn(hb, w1, w3, w2, xs, g3, modtab, lat_rows, g_next, modtab_next):
    n_rows, d = hb.shape
    dff = w1.shape[1]
    nb = modtab.shape[0] - 1
    tm, tf = ROW_TILE, FFN_TF
    mi = _mod_index(tm, lat_rows, nb)
    emit_next = g_next is not None
    in_specs = [pl.BlockSpec((tm, d), lambda i, f: (i, 0)),
                pl.BlockSpec((d, tf), lambda i, f: (0, f)),
                pl.BlockSpec((d, tf), lambda i, f: (0, f)),
                pl.BlockSpec((tf, d), lambda i, f: (f, 0)),
                pl.BlockSpec((tm, d), lambda i, f: (i, 0)),
                pl.BlockSpec((1, d), lambda i, f: (0, 0)),
                pl.BlockSpec((1, 6, d), lambda i, f: (mi(i), 0, 0))]
    args = [hb, w1, w3, w2, xs, g3.reshape(1, d), modtab]
    out_specs = [pl.BlockSpec((tm, d), lambda i, f: (i, 0))]
    out_shape = [jax.ShapeDtypeStruct((n_rows, d), F32)]
    if emit_next:
        in_specs += [pl.BlockSpec((1, d), lambda i, f: (0, 0)),
                     pl.BlockSpec((1, 6, d), lambda i, f: (mi(i), 0, 0))]
        args += [g_next.reshape(1, d), modtab_next]
        out_specs.append(pl.BlockSpec((tm, d), lambda i, f: (i, 0)))
        out_shape.append(jax.ShapeDtypeStruct((n_rows, d), BF16))
    return pl.pallas_call(
        functools.partial(_ffn_kernel, emit_next=emit_next),
        grid=(n_rows // tm, dff // tf),
        in_specs=in_specs,
        out_specs=out_specs,
        out_shape=out_shape,
        scratch_shapes=[pltpu.VMEM((tm, d), F32)],
        compiler_params=_cparams(("arbitrary", "arbitrary"), 48),
        name="ffn",
    )(*args)


_META_E0, _META_E1, _META_R0, _META_R1, _META_G0, _META_G1 = range(6)


def _router_kernel(x_ref, g_ref, mod_ref, wr_ref, tri_ref, meta_ref, cnt_ref, carry_ref):
    i = pl.program_id(0)

    @pl.when(i == 0)
    def _():
        carry_ref[...] = jnp.zeros(carry_ref.shape, F32)

    m = mod_ref[0]
    hmod = _rms(x_ref[...], g_ref[...]) * (1.0 + m[4:5]) + m[3:4]
    logits = jnp.dot(hmod, wr_ref[...], precision=lax.Precision.HIGHEST, preferred_element_type=F32)
    lane = lax.broadcasted_iota(jnp.int32, logits.shape, 1)
    lanef = lane.astype(F32)
    logits = jnp.where(lane < N_EXPERTS, logits, -jnp.inf)
    v0 = jnp.max(logits, axis=1, keepdims=True)
    e0 = jnp.min(jnp.where(logits == v0, lanef, 1e9), axis=1, keepdims=True)
    rest = jnp.where(lanef == e0, -jnp.inf, logits)
    v1 = jnp.max(rest, axis=1, keepdims=True)
    e1 = jnp.min(jnp.where(rest == v1, lanef, 1e9), axis=1, keepdims=True)
    ex = jnp.exp(v1 - v0)
    g0 = 1.0 / (1.0 + ex)
    g1 = ex / (1.0 + ex)
    oh0 = lanef == e0
    oh1 = lanef == e1
    oh = jnp.where(jnp.logical_or(oh0, oh1), 1.0, 0.0)
    before = _dot(tri_ref[...], _bf(oh)) + carry_ref[0:1, :]
    r0 = jnp.sum(jnp.where(oh0, before, 0.0), axis=1, keepdims=True)
    r1 = jnp.sum(jnp.where(oh1, before, 0.0), axis=1, keepdims=True)
    carry_ref[0:1, :] = carry_ref[0:1, :] + jnp.sum(oh, axis=0, keepdims=True)
    meta = jnp.zeros(logits.shape, F32)
    for j, val in ((_META_E0, e0), (_META_E1, e1), (_META_R0, r0), (_META_R1, r1),
                   (_META_G0, g0), (_META_G1, g1)):
        meta = jnp.where(lane == j, val, meta)
    meta_ref[...] = meta
    cnt_ref[...] = carry_ref[...]


def _router(xs, n_rows, g2, modtab, w_router, lat_rows):
    d = xs.shape[1]
    nb = modtab.shape[0] - 1
    tm = ROW_TILE
    mi = _mod_index(tm, lat_rows, nb)
    wr = jnp.zeros((d, 128), F32).at[:, :N_EXPERTS].set(w_router)
    tri = jnp.asarray(np.tril(np.ones((tm, tm), np.float32), -1), BF16)
    return pl.pallas_call(
        _router_kernel,
        grid=(n_rows // tm,),
        in_specs=[pl.BlockSpec((tm, d), lambda i: (i, 0)),
                  pl.BlockSpec((1, d), lambda i: (0, 0)),
                  pl.BlockSpec((1, 6, d), lambda i: (mi(i), 0, 0)),
                  pl.BlockSpec((d, 128), lambda i: (0, 0)),
                  pl.BlockSpec((tm, tm), lambda i: (0, 0))],
        out_specs=[pl.BlockSpec((tm, 128), lambda i: (i, 0)),
                   pl.BlockSpec((8, 128), lambda i: (0, 0))],
        out_shape=[jax.ShapeDtypeStruct((n_rows, 128), F32),
                   jax.ShapeDtypeStruct((8, 128), F32)],
        scratch_shapes=[pltpu.VMEM((8, 128), F32)],
        compiler_params=_cparams(("arbitrary",), 32),
        name="router",
    )(xs, g2.reshape(1, d), modtab, wr, tri)


def _dispatch_kernel(dest_ref, x_ref, g_ref, mod_ref, init_ref, o_ref, h_s, sem):
    del init_ref
    i = pl.program_id(0)
    tm = h_s.shape[0]
    m = mod_ref[0]
    h_s[...] = _rms(x_ref[...], g_ref[...]) * (1.0 + m[4:5]) + m[3:4]

    def row_copy(r, k):
        dst = dest_ref[TOP_K * (i * tm + r) + k]
        return pltpu.make_async_copy(h_s.at[pl.ds(r, 1)], o_ref.at[pl.ds(dst, 1)], sem)

    def start(r, carry):
        for k in range(TOP_K):
            row_copy(r, k).start(priority=k)
        return carry

    lax.fori_loop(0, tm, start, 0, unroll=DMA_UNROLL)

    def wait(r, carry):
        for k in range(TOP_K):
            row_copy(r, k).wait()
        return carry

    lax.fori_loop(0, tm, wait, 0, unroll=DMA_UNROLL)


def _dispatch(dest, xs, n_rows, g2, modtab, n_slots, lat_rows):
    d = xs.shape[1]
    nb = modtab.shape[0] - 1
    tm = ROW_TILE
    mi = _mod_index(tm, lat_rows, nb)
    init = jnp.zeros((n_slots, d), F32)
    grid_spec = pltpu.PrefetchScalarGridSpec(
        num_scalar_prefetch=1,
        grid=(n_rows // tm,),
        in_specs=[pl.BlockSpec((tm, d), lambda i, dst: (i, 0)),
                  pl.BlockSpec((1, d), lambda i, dst: (0, 0)),
                  pl.BlockSpec((1, 6, d), lambda i, dst: (mi(i), 0, 0)),
                  pl.BlockSpec(memory_space=pl.ANY)],
        out_specs=pl.BlockSpec(memory_space=pl.ANY),
        scratch_shapes=[pltpu.VMEM((tm, d), F32), pltpu.SemaphoreType.DMA(())],
    )
    return pl.pallas_call(
        _dispatch_kernel,
        grid_spec=grid_spec,
        out_shape=jax.ShapeDtypeStruct((n_slots, d), F32),
        input_output_aliases={4: 0},
        compiler_params=_cparams(("arbitrary",), 32),
        name="dispatch",
    )(dest, xs, g2.reshape(1, d), modtab, init)


def _expert_kernel(te_ref, nu_ref, x_ref, w1_ref, w3_ref, w2_ref, o_ref, xb_s, acc_s):
    j = pl.program_id(0)
    f = pl.program_id(1)

    @pl.when(j < nu_ref[0])
    def _():
        @pl.when(f == 0)
        def _():
            xb_s[...] = _bf(x_ref[...])
            acc_s[...] = jnp.zeros(acc_s.shape, F32)

        h = xb_s[...]
        u = _silu(_dot(h, _bf(w1_ref[0]))) * _dot(h, _bf(w3_ref[0]))
        acc_s[...] += _dot(_bf(u), _bf(w2_ref[0]))

        @pl.when(f == pl.num_programs(1) - 1)
        def _():
            o_ref[...] = acc_s[...]

    @pl.when(jnp.logical_and(j >= nu_ref[0], f == pl.num_programs(1) - 1))
    def _():
        o_ref[...] = jnp.zeros(o_ref.shape, F32)


def _experts(tile_e, n_used, xsort, w1, w3, w2):
    n_slots, d = xsort.shape
    dff = w1.shape[2]
    tm, tf = MOE_TM, FFN_TF
    nf = dff // tf

    def jj(j, nu):
        return jnp.maximum(jnp.minimum(j, nu[0] - 1), 0)

    def ff(j, f, nu):
        return jnp.where(j < nu[0], f, nf - 1)

    grid_spec = pltpu.PrefetchScalarGridSpec(
        num_scalar_prefetch=2,
        grid=(n_slots // tm, nf),
        in_specs=[pl.BlockSpec((tm, d), lambda j, f, te, nu: (jj(j, nu), 0)),
                  pl.BlockSpec((1, d, tf), lambda j, f, te, nu: (te[jj(j, nu)], 0, ff(j, f, nu))),
                  pl.BlockSpec((1, d, tf), lambda j, f, te, nu: (te[jj(j, nu)], 0, ff(j, f, nu))),
                  pl.BlockSpec((1, tf, d), lambda j, f, te, nu: (te[jj(j, nu)], ff(j, f, nu), 0))],
        out_specs=pl.BlockSpec((tm, d), lambda j, f, te, nu: (j, 0)),
        scratch_shapes=[pltpu.VMEM((tm, d), BF16), pltpu.VMEM((tm, d), F32)],
    )
    return pl.pallas_call(
        _expert_kernel,
        grid_spec=grid_spec,
        out_shape=jax.ShapeDtypeStruct((n_slots, d), F32),
        compiler_params=_cparams(("arbitrary", "arbitrary"), 56),
        name="experts",
    )(tile_e, n_used, xsort, w1, w3, w2)


def _combine_kernel(dest_ref, y_ref, meta_ref, x_ref, g_ref, mod_ref, o_ref, buf_s, sem):
    i = pl.program_id(0)
    tm = x_ref.shape[0]

    def row_copy(r, k):
        src = dest_ref[TOP_K * (i * tm + r) + k]
        return pltpu.make_async_copy(y_ref.at[pl.ds(src, 1)], buf_s.at[k, pl.ds(r, 1)], sem)

    def start(r, carry):
        for k in range(TOP_K):
            row_copy(r, k).start(priority=k)
        return carry

    lax.fori_loop(0, tm, start, 0, unroll=DMA_UNROLL)

    def wait(r, carry):
        for k in range(TOP_K):
            row_copy(r, k).wait()
        return carry

    lax.fori_loop(0, tm, wait, 0, unroll=DMA_UNROLL)

    meta = meta_ref[...]
    lane = lax.broadcasted_iota(jnp.int32, meta.shape, 1)
    g0 = jnp.sum(jnp.where(lane == _META_G0, meta, 0.0), axis=1, keepdims=True)
    g1 = jnp.sum(jnp.where(lane == _META_G1, meta, 0.0), axis=1, keepdims=True)
    y = buf_s[0] * g0 + buf_s[1] * g1
    m = mod_ref[0]
    o_ref[...] = x_ref[...] + m[5:6] * _rms(y, g_ref[...])


def _combine(dest, yb, meta, xs, n_rows, g3, modtab, lat_rows):
    d = xs.shape[1]
    nb = modtab.shape[0] - 1
    tm = ROW_TILE
    mi = _mod_index(tm, lat_rows, nb)
    grid_spec = pltpu.PrefetchScalarGridSpec(
        num_scalar_prefetch=1,
        grid=(n_rows // tm,),
        in_specs=[pl.BlockSpec(memory_space=pl.ANY),
                  pl.BlockSpec((tm, 128), lambda i, dst: (i, 0)),
                  pl.BlockSpec((tm, d), lambda i, dst: (i, 0)),
                  pl.BlockSpec((1, d), lambda i, dst: (0, 0)),
                  pl.BlockSpec((1, 6, d), lambda i, dst: (mi(i), 0, 0))],
        out_specs=pl.BlockSpec((tm, d), lambda i, dst: (i, 0)),
        scratch_shapes=[pltpu.VMEM((TOP_K, tm, d), F32), pltpu.SemaphoreType.DMA(())],
    )
    return pl.pallas_call(
        _combine_kernel,
        grid_spec=grid_spec,
        out_shape=jax.ShapeDtypeStruct((n_rows, d), F32),
        compiler_params=_cparams(("arbitrary",), 40),
        name="combine",
    )(dest, yb, meta, xs, g3.reshape(1, d), modtab)


def _moe(xs, n_rows, g2, g3, modtab, w_router, w1, w3, w2, lat_rows):
    meta, cnt = _router(xs, n_rows, g2, modtab, w_router, lat_rows)
    counts = cnt[0, :N_EXPERTS].astype(jnp.int32)
    padded = (counts + MOE_TM - 1) // MOE_TM * MOE_TM
    pad_end = jnp.cumsum(padded)
    pad_start = pad_end - padded
    e = meta[:, _META_E0:_META_E1 + 1].astype(jnp.int32)
    r = meta[:, _META_R0:_META_R1 + 1].astype(jnp.int32)
    dest = (pad_start[e] + r).reshape(-1)
    n_tiles = (n_rows * TOP_K) // MOE_TM + N_EXPERTS
    n_slots = n_tiles * MOE_TM
    tile_start = jnp.arange(n_tiles, dtype=jnp.int32) * MOE_TM
    tile_e = jnp.minimum(jnp.sum((tile_start[:, None] >= pad_end[None, :]).astype(jnp.int32), axis=1),
                         N_EXPERTS - 1)
    n_used = (pad_end[-1:] // MOE_TM).astype(jnp.int32)
    xsort = _dispatch(dest, xs, n_rows, g2, modtab, n_slots, lat_rows)
    yb = _experts(tile_e, n_used, xsort, w1, w3, w2)
    return _combine(dest, yb, meta, xs, n_rows, g3, modtab, lat_rows)


def _rotary_tables(lat_len, ctx_len):
    rows = lat_len // GRID_W
    row = jnp.repeat(jnp.arange(rows, dtype=F32), GRID_W)
    col = jnp.tile(jnp.arange(GRID_W, dtype=F32), rows)
    n_freq = RET_DK // 4
    inv = ROPE_BASE ** (-jnp.arange(n_freq, dtype=F32) / n_freq)
    ang = jnp.concatenate([row[:, None] * inv, col[:, None] * inv], axis=-1)
    cos, sin = jnp.cos(ang), jnp.sin(ang)
    cosf = jnp.concatenate([jnp.ones((ctx_len, RET_DK), F32), jnp.concatenate([cos, cos], -1)], 0)
    sinf = jnp.concatenate([jnp.zeros((ctx_len, RET_DK), F32), jnp.concatenate([-sin, sin], -1)], 0)
    return cosf, sinf


def kernel(x, c, ctx, c_ctx, w_ada, b_ada, norm_g, w_in, w_out, ret_decay, ret_gn, hgrn_lb, hgrn_gn,
           mlstm_conv, mlstm_gate_b, mlstm_gn, w_ffn1, w_ffn3, w_ffn2, w_router, w_exp1, w_exp3, w_exp2):
    nb, lat_len, d = x.shape
    ctx_len = ctx.shape[1]
    depth = w_ada.shape[0]
    n_lat_rows = nb * lat_len
    n_rows = n_lat_rows + nb * ctx_len

    x_parts = [x.reshape(n_lat_rows, d), ctx.reshape(nb * ctx_len, d)]
    w_in_b = _bf(w_in)
    s_in = jnp.zeros((8, d), F32).at[:nb].set(c).at[nb].set(c_ctx)
    mod_all = _ada(s_in, w_ada, b_ada)[:, :nb + 1].reshape(depth, nb + 1, 6, d)

    cosf, sinf = _rotary_tables(lat_len, ctx_len)
    hg_cum, hg_masks, hg_signs = _hgrn_constants()
    hg_consts = (jnp.asarray(hg_cum, BF16), jnp.asarray(hg_masks, F32), jnp.asarray(hg_signs, F32))
    sm = jax.nn.softmax(hgrn_lb.astype(F32), axis=0)
    lb_all = jnp.clip(jnp.cumsum(sm, axis=0) - sm[0], 0.0, 1.0)

    hb = None
    for layer in range(depth):
        last = layer == depth - 1
        modtab = mod_all[layer]
        g = norm_g[layer]

        if hb is None:
            hb = _prenorm(x_parts, n_rows, g[0], modtab, 1, 0, lat_len, BF16)
        p = _mm(hb, w_in_b[layer], MM_TM, MM_TN, n_cols=PROJ_MAIN)
        w_g = jnp.zeros((d, 128), BF16).at[:, :MLSTM_GATES].set(w_in_b[layer][:, PROJ_MAIN:])
        gates = _mm(hb, w_g, MM_TM, 128)[:, :MLSTM_GATES]
        hb = None
        g_row_lat = jnp.swapaxes(gates[:n_lat_rows].reshape(nb, lat_len, MLSTM_GATES), 1, 2)
        g_row_ctx = jnp.swapaxes(gates[n_lat_rows:].reshape(nb, ctx_len, MLSTM_GATES), 1, 2)

        log_g = jax.nn.log_sigmoid(ret_decay[layer].astype(F32))
        lg_b = jnp.broadcast_to(log_g[:, :, None, None], (2, RET_HEADS, 8, RET_DV))
        lb = lb_all[layer]
        emit_ctx = not last
        o_ret = _retention(p, nb, lat_len, ctx_len, cosf, sinf, lg_b, ret_gn[layer], emit_ctx)
        o_hg = _hgrn(p, nb, lat_len, ctx_len, jnp.log(lb), jnp.log1p(-lb), hgrn_gn[layer],
                     hg_consts, emit_ctx)
        o_ml = _mlstm(p, gates, g_row_lat, g_row_ctx, nb, lat_len, ctx_len, mlstm_gate_b[layer],
                      mlstm_conv[layer], mlstm_gn[layer], emit_ctx)
        mix_parts = [list(o) for o in (o_ret, o_hg, o_ml)]
        rows_now = n_lat_rows if last else n_rows
        dense = layer % 2 == 0
        res = _wout(mix_parts, _bf(w_out[layer]), x_parts, rows_now, g[1], modtab, lat_len,
                    g[2] if dense else None)
        xs = res[0]
        x_parts = [xs]

        j = layer // 2
        if dense:
            nxt = (None, None) if last else (norm_g[layer + 1][0], mod_all[layer + 1])
            res = _ffn(res[1], _bf(w_ffn1[j]), _bf(w_ffn3[j]), _bf(w_ffn2[j]), xs, g[3], modtab, lat_len, *nxt)
            xs = res[0]
            hb = None if last else res[1]
        else:
            xs = _moe(xs, rows_now, g[2], g[3], modtab, w_router[j], w_exp1[j], w_exp3[j], w_exp2[j], lat_len)
        x_parts = [xs]
    return xs[:n_lat_rows].reshape(nb, lat_len, d)
```

```python
import functools

import numpy as np
import jax
import jax.numpy as jnp
from jax import lax
from jax.experimental import pallas as pl
from jax.experimental.pallas import tpu as pltpu

F32 = jnp.float32
BF16 = jnp.bfloat16

CHUNK = 128
NORM_EPS = 1e-6
ROPE_BASE = 10000.0
GRID_W = 64

RET_HEADS, RET_DK, RET_DV = 4, 128, 256
HGRN_HEADS, HGRN_DK, HGRN_DV = 4, 128, 128
MLSTM_HEADS, MLSTM_DK, MLSTM_DV = 4, 128, 128
N_EXPERTS = 8
TOP_K = 2

RET_QK = RET_HEADS * RET_DK
RET_WIDTH = RET_HEADS * RET_DV
HGRN_QK = HGRN_HEADS * HGRN_DK
HGRN_WIDTH = HGRN_HEADS * HGRN_DV
MLSTM_QK = MLSTM_HEADS * MLSTM_DK
MLSTM_WIDTH = MLSTM_HEADS * MLSTM_DV
MLSTM_GATES = 4 * MLSTM_HEADS

_C_RET_Q = 0
_C_RET_K = _C_RET_Q + RET_QK // 128
_C_RET_V = _C_RET_K + RET_QK // 128
_C_RET_G = _C_RET_V + RET_WIDTH // 128
_C_HG_Q = _C_RET_G + RET_WIDTH // 128
_C_HG_FF = _C_HG_Q + HGRN_QK // 128
_C_HG_FB = _C_HG_FF + HGRN_QK // 128
_C_HG_I = _C_HG_FB + HGRN_QK // 128
_C_HG_G = _C_HG_I + HGRN_WIDTH // 128
_C_ML_Q = _C_HG_G + HGRN_WIDTH // 128
_C_ML_K = _C_ML_Q + MLSTM_QK // 128
_C_ML_V = _C_ML_K + MLSTM_QK // 128
_C_ML_O = _C_ML_V + MLSTM_WIDTH // 128
PROJ_MAIN = (_C_ML_O + MLSTM_WIDTH // 128) * 128

_VMEM_CAP_BYTES = 56 * 1024 * 1024

ROW_TILE = 512
MM_TM, MM_TN = 1024, 1536
FFN_TF = 512
MOE_TM = 512
RET_UNROLL = (4, 4)
HGRN_UNROLL = (2, 4)
MLSTM_UNROLL = (2, 2)
DMA_UNROLL = 8


def _cparams(sem, vmem_mb):
    return pltpu.CompilerParams(dimension_semantics=sem,
                                vmem_limit_bytes=min(int(vmem_mb * 1024 * 1024), _VMEM_CAP_BYTES))


def _bf(x):
    return x.astype(BF16)


def _dot(a, b):
    return jnp.dot(a, b, preferred_element_type=F32)


def _dot_nt(a, b):
    return lax.dot_general(a, b, (((1,), (1,)), ((), ())), preferred_element_type=F32)


def _dot_tn(a, b):
    return lax.dot_general(a, b, (((0,), (0,)), ((), ())), preferred_element_type=F32)


def _sigmoid(x):
    return 1.0 / (1.0 + jnp.exp(-x))


def _silu(x):
    return x * _sigmoid(x)


def _log_sigmoid(x):
    return jnp.minimum(x, 0.0) - jnp.log(1.0 + jnp.exp(-jnp.abs(x)))


def _rms(x, g):
    return x * lax.rsqrt(jnp.mean(x * x, axis=-1, keepdims=True) + NORM_EPS) * g


def _mod_index(tile_rows, n_lat_rows_per_batch, n_batch):
    return lambda i: jnp.minimum((i * tile_rows) // n_lat_rows_per_batch, n_batch)


def _ada_kernel(s_ref, w_ref, b_ref, o_ref):
    s = _bf(_silu(s_ref[...]))
    o_ref[0] = _dot(s, _bf(w_ref[0])) + b_ref[0]


def _ada(s_in, w_ada, b_ada):
    depth, d, n = w_ada.shape
    tn = 1024
    return pl.pallas_call(
        _ada_kernel,
        grid=(depth, n // tn),
        in_specs=[pl.BlockSpec((8, d), lambda l, j: (0, 0)),
                  pl.BlockSpec((1, d, tn), lambda l, j: (l, 0, j)),
                  pl.BlockSpec((1, 1, tn), lambda l, j: (l, 0, j))],
        out_specs=pl.BlockSpec((1, 8, tn), lambda l, j: (l, 0, j)),
        out_shape=jax.ShapeDtypeStruct((depth, 8, n), F32),
        compiler_params=_cparams(("arbitrary", "arbitrary"), 40),
        name="ada",
    )(s_in, w_ada, b_ada.reshape(depth, 1, n))


def _part_specs(parts, tm, width):
    if len(parts) == 1:
        return [pl.BlockSpec((tm, width), lambda i: (i, 0))]
    n0 = parts[0].shape[0] // tm
    return [pl.BlockSpec((tm, width), lambda i: (jnp.minimum(i, n0 - 1), 0)),
            pl.BlockSpec((tm, width), lambda i: (jnp.maximum(i - n0, 0), 0))]


def _on_part(n_first, n_parts, fn):
    if n_parts == 1:
        fn(0)
    else:
        i = pl.program_id(0)
        pl.when(i < n_first)(lambda: fn(0))
        pl.when(i >= n_first)(lambda: fn(1))


def _prenorm_kernel(*refs, sc, sh, n_first):
    x_refs, (g_ref, mod_ref, o_ref) = refs[:-3], refs[-3:]

    def run(part):
        m = mod_ref[0]
        y = _rms(x_refs[part][...], g_ref[...])
        o_ref[...] = (y * (1.0 + m[sc:sc + 1]) + m[sh:sh + 1]).astype(o_ref.dtype)

    _on_part(n_first, len(x_refs), run)


def _prenorm(x_parts, n_rows, g, modtab, sc, sh, lat_rows, out_dtype):
    d = x_parts[0].shape[1]
    nb = modtab.shape[0] - 1
    mi = _mod_index(ROW_TILE, lat_rows, nb)
    return pl.pallas_call(
        functools.partial(_prenorm_kernel, sc=sc, sh=sh, n_first=x_parts[0].shape[0] // ROW_TILE),
        grid=(n_rows // ROW_TILE,),
        in_specs=[*_part_specs(x_parts, ROW_TILE, d),
                  pl.BlockSpec((1, d), lambda i: (0, 0)),
                  pl.BlockSpec((1, 6, d), lambda i: (mi(i), 0, 0))],
        out_specs=pl.BlockSpec((ROW_TILE, d), lambda i: (i, 0)),
        out_shape=jax.ShapeDtypeStruct((n_rows, d), out_dtype),
        compiler_params=_cparams(("arbitrary",), 32),
        name="prenorm",
    )(*x_parts, g.reshape(1, d), modtab)


def _mm_kernel(x_ref, w_ref, o_ref):
    o_ref[...] = _dot(x_ref[...], w_ref[...]).astype(o_ref.dtype)


def _mm(x, w, tm, tn, n_cols=None, out_dtype=F32):
    m, k = x.shape
    n = w.shape[1] if n_cols is None else n_cols
    return pl.pallas_call(
        _mm_kernel,
        grid=(m // tm, n // tn),
        in_specs=[pl.BlockSpec((tm, k), lambda i, j: (i, 0)),
                  pl.BlockSpec((k, tn), lambda i, j: (0, j))],
        out_specs=pl.BlockSpec((tm, tn), lambda i, j: (i, j)),
        out_shape=jax.ShapeDtypeStruct((m, n), out_dtype),
        compiler_params=_cparams(("arbitrary", "arbitrary"), 40),
        name="proj",
    )(x, w)


def _bwd_chunk(i, n_ctx, n_lat):
    return jnp.where(i < n_ctx, n_ctx - 1 - i, 2 * n_ctx + n_lat - 1 - i)


def _rows(c):
    return pl.ds(pl.multiple_of(c * CHUNK, CHUNK), CHUNK)


def _for_chunks(n_ctx, n_lat, fn, unroll, with_ctx=True):
    def run(part, n, off):
        def body(c, carry):
            fn(part, c, c + off)
            return carry
        lax.fori_loop(0, n, body, 0, unroll=max(u for u in (1, 2, unroll) if n % u == 0 and u <= unroll))
    if with_ctx:
        run(0, n_ctx, 0)
    run(1, n_lat, n_ctx)


def _lanes(d, w):
    return slice(d * w, (d + 1) * w)


def _mixer_call(kernel_fn, name, in_specs, args, nb, heads, lat_len, ctx_len, dv, scratch, emit_ctx, vmem_mb):
    width = heads * dv
    out_specs = [pl.BlockSpec((lat_len, dv), lambda b, h: (b, h))]
    out_shape = [jax.ShapeDtypeStruct((nb * lat_len, width), BF16)]
    if emit_ctx:
        out_specs.append(pl.BlockSpec((ctx_len, dv), lambda b, h: (b, h)))
        out_shape.append(jax.ShapeDtypeStruct((nb * ctx_len, width), BF16))
    n_in = len(in_specs)

    def body(*refs):
        ins, rest = refs[:n_in], refs[n_in:]
        if emit_ctx:
            ol, oc, scr = rest[0], rest[1], rest[2:]
        else:
            ol, oc, scr = rest[0], None, rest[1:]
        kernel_fn(*ins, ol, oc, *scr, n_ctx=ctx_len // CHUNK, n_lat=lat_len // CHUNK)

    return pl.pallas_call(
        body,
        grid=(nb, heads),
        in_specs=in_specs,
        out_specs=out_specs,
        out_shape=out_shape,
        scratch_shapes=scratch,
        compiler_params=_cparams(("arbitrary", "arbitrary"), vmem_mb),
        name=name,
    )(*args)


def _seq_specs(col0, width, nb, lat_len, ctx_len):
    n_lat_blk = nb * lat_len // ctx_len
    c0 = col0 * 128 // width
    return (pl.BlockSpec((lat_len, width), lambda b, h: (b, c0 + h)),
            pl.BlockSpec((ctx_len, width), lambda b, h: (n_lat_blk + b, c0 + h)))


def _ret_kernel(ql, qc, kl, kc, vl, vc, gl, gc, cos_ref, sin_ref, lg_ref, gn_ref, ol_ref, oc_ref,
                qs_s, kv_s, sall_s, o_s, st_s, dm_s, dq_s, dk_s, *, n_ctx, n_lat):
    nch = n_ctx + n_lat
    unroll_a, unroll_c = RET_UNROLL
    qr, kr, vr, gr, outr = (qc, ql), (kc, kl), (vc, vl), (gc, gl), (oc_ref, ol_ref)
    t_i = lax.broadcasted_iota(jnp.int32, (CHUNK, CHUNK), 0).astype(F32)
    s_i = lax.broadcasted_iota(jnp.int32, (CHUNK, CHUNK), 1).astype(F32)
    dm = None
    for d in range(2):
        lg = lg_ref[d, 0][0:1, 0:CHUNK]
        rel = (t_i - s_i) if d == 0 else (s_i - t_i)
        dmd = jnp.where(rel >= 0, jnp.exp(jnp.maximum(rel, 0.0) * lg), 0.0)
        dm = dmd if dm is None else dm + dmd
        p = t_i if d == 0 else (CHUNK - 1.0) - t_i
        dq_s[d] = jnp.exp((p + 1.0) * lg)
        dk_s[d] = jnp.exp((CHUNK - 1.0 - p) * lg)
    dm_s[...] = dm
    scale = RET_DK ** -0.5

    def phase_a(part, cl, cg):
        rl, rg = _rows(cl), _rows(cg)
        cs = cos_ref[rg, :]
        sn = sin_ref[rg, :]
        q = qr[part][rl, :]
        k = kr[part][rl, :]
        q = q * cs + pltpu.roll(q, RET_DK // 2, 1) * sn
        k = (k * cs + pltpu.roll(k, RET_DK // 2, 1) * sn) * scale
        vb = _bf(vr[part][rl, :])
        s = _dot_nt(_bf(q), _bf(k))
        o_s[rg, :] = _dot(_bf(s * dm_s[...]), vb)
        qs_s[rg, :] = jnp.concatenate([_bf(q * dq_s[0]), _bf(q * dq_s[1])], axis=1)
        kd = jnp.concatenate([_bf(k * dk_s[0]), _bf(k * dk_s[1])], axis=1)
        kv_s[cg] = _dot_tn(kd, vb)

    _for_chunks(n_ctx, n_lat, phase_a, unroll_a)

    st_s[...] = jnp.zeros(st_s.shape, F32)
    dec = [jnp.exp(float(CHUNK) * lg_ref[d, 0][0:1, :]) for d in range(2)]

    def phase_b(i, carry):
        for d, c in ((0, i), (1, _bwd_chunk(i, n_ctx, n_lat))):
            st = st_s[d]
            sall_s[c, _lanes(d, RET_DK), :] = _bf(st)
            st_s[d] = dec[d] * st + kv_s[c, _lanes(d, RET_DK), :]
        return carry

    lax.fori_loop(0, nch, phase_b, 0)
    gn = gn_ref[...]

    def phase_c(part, cl, cg):
        rl, rg = _rows(cl), _rows(cg)
        o = o_s[rg, :] + _dot(qs_s[rg, :], sall_s[cg])
        outr[part][rl, :] = (_rms(o, gn) * _silu(gr[part][rl, :])).astype(BF16)

    _for_chunks(n_ctx, n_lat, phase_c, unroll_c, with_ctx=oc_ref is not None)


def _retention(p, nb, lat_len, ctx_len, cosf, sinf, lg_b, gn, emit_ctx):
    t_len = lat_len + ctx_len
    nch = t_len // CHUNK
    s128 = functools.partial(_seq_specs, width=128, nb=nb, lat_len=lat_len, ctx_len=ctx_len)
    s256 = functools.partial(_seq_specs, width=256, nb=nb, lat_len=lat_len, ctx_len=ctx_len)
    in_specs = [*s128(_C_RET_Q), *s128(_C_RET_K), *s256(_C_RET_V), *s256(_C_RET_G),
                pl.BlockSpec((t_len, 128), lambda b, h: (0, 0)),
                pl.BlockSpec((t_len, 128), lambda b, h: (0, 0)),
                pl.BlockSpec((2, 1, 8, RET_DV), lambda b, h: (0, h, 0, 0)),
                pl.BlockSpec((1, RET_DV), lambda b, h: (0, h))]
    scratch = [pltpu.VMEM((t_len, 2 * RET_DK), BF16),
               pltpu.VMEM((nch, 2 * RET_DK, RET_DV), F32),
               pltpu.VMEM((nch, 2 * RET_DK, RET_DV), BF16),
               pltpu.VMEM((t_len, RET_DV), F32),
               pltpu.VMEM((2, RET_DK, RET_DV), F32),
               pltpu.VMEM((CHUNK, CHUNK), F32),
               pltpu.VMEM((2, CHUNK, CHUNK), F32), pltpu.VMEM((2, CHUNK, CHUNK), F32)]
    args = (p, p, p, p, p, p, p, p, cosf, sinf, lg_b, gn.reshape(1, RET_WIDTH))
    return _mixer_call(_ret_kernel, "retention", in_specs, args, nb, RET_HEADS, lat_len, ctx_len, RET_DV,
                       scratch, emit_ctx, 48)


_HG_LEVELS = (64, 32, 16, 8, 4, 2, 1)


def _hgrn_constants():
    c = CHUNK
    t = np.arange(c)[:, None]
    u = np.arange(c)[None, :]
    cum = (u <= t).astype(np.float32)
    masks, signs = [], []
    for m in _HG_LEVELS:
        base = (t // (2 * m)) * (2 * m)
        lower = t >= base + m
        tb = t // (2 * m)
        sb = u // (2 * m)
        masks.append((tb == sb) & lower & (u < (sb * 2 * m + m)))
        signs.append(np.broadcast_to(np.where(lower, 1.0, -1.0), (c, c)))
    masks.append(t == u)
    kf = np.stack([x.astype(np.float32) for x in masks], axis=0)
    kb = np.stack([x.astype(np.float32)[::-1, ::-1] for x in masks], axis=0)
    sf = np.stack([x.astype(np.float32) for x in signs], axis=0)
    sb_ = np.stack([x.astype(np.float32)[::-1, ::-1] for x in signs], axis=0)
    return (np.stack([cum, cum[::-1, ::-1]], 0), np.stack([kf, kb], 0),
            np.stack([sf[:_HG_WIDE], sb_[:_HG_WIDE]], 0))


_HG_WIDE = 5


def _hgrn_level_exponents(b, lf, d, sgn_ref, row):
    out = []
    for l, m in enumerate(_HG_LEVELS[:_HG_WIDE]):
        pieces = []
        for j in range(CHUNK // (2 * m)):
            r = 2 * m * j + (m - 1 if d == 0 else m)
            pieces.append(jnp.broadcast_to(b[r:r + 1, :], (2 * m, CHUNK)))
        bref = pieces[0] if len(pieces) == 1 else jnp.concatenate(pieces, axis=0)
        out.append((b - bref) * sgn_ref[d, l])
    up = pltpu.roll(lf, CHUNK - 1, 0)
    dn = pltpu.roll(lf, 1, 0)
    r4 = row % 4
    if d == 0:
        e2 = jnp.where(r4 == 0, up, jnp.where(r4 == 1, 0.0, jnp.where(r4 == 2, lf, lf + dn)))
        e1 = jnp.where(row % 2 == 1, lf, 0.0)
    else:
        e2 = jnp.where(r4 == 0, lf + up, jnp.where(r4 == 1, lf, jnp.where(r4 == 2, 0.0, dn)))
        e1 = jnp.where(row % 2 == 0, lf, 0.0)
    return out + [e2, e1]


def _hgrn_kernel(ql, qc, ffl, ffc, fbl, fbc, il, ic, gl, gc, llb_ref, lub_ref, gn_ref, c_ref, k_ref, sgn_ref,
                 ol_ref, oc_ref, qs_s, kv_s, sall_s, dec_s, o_s, st_s, *, n_ctx, n_lat):
    nch = n_ctx + n_lat
    unroll_a, unroll_c = HGRN_UNROLL
    qr, fr, ir, gr, outr =(qc, ql), ((ffc, ffl), (fbc, fbl)), (ic, il), (gc, gl), (oc_ref, ol_ref)
    llb = llb_ref[...]
    lub = lub_ref[...]
    nlev = len(_HG_LEVELS)
    row = lax.broadcasted_iota(jnp.int32, (CHUNK, HGRN_DK), 0)

    def phase_a(part, cl, cg):
        rl, rg = _rows(cl), _rows(cg)
        q = _silu(qr[part][rl, :])
        qb = _bf(q)
        vb = _bf(ir[part][rl, :])
        a_sum, qs, kds = None, [], []
        for d in range(2):
            lsg = lub + _log_sigmoid(fr[d][part][rl, :])
            lf = jnp.maximum(llb, lsg) + jnp.log(1.0 + jnp.exp(-jnp.abs(llb - lsg)))
            k = 1.0 - jnp.exp(lf)
            lf_hi = _bf(lf)
            lf_lo = _bf(lf - lf_hi.astype(F32))
            b2 = _dot(c_ref[d], jnp.concatenate([lf_hi, lf_lo], axis=1))
            b = b2[:, 0:HGRN_DK] + b2[:, HGRN_DK:]
            es = _hgrn_level_exponents(b, lf, d, sgn_ref, row)
            kb = _bf(k)
            a = k_ref[d, nlev] * _dot_nt(qb, kb)
            for l in range(nlev):
                xb = _bf(jnp.exp(es[l]))
                a = a + k_ref[d, l] * _dot_nt(qb * xb, kb * xb)
            a_sum = a if a_sum is None else a_sum + a
            b_last = b[CHUNK - 1:CHUNK] if d == 0 else b[0:1]
            qs.append(_bf(q * jnp.exp(b)))
            kds.append(_bf(k * jnp.exp(b_last - b)))
            dec_s[cg, :, _lanes(d, HGRN_DK)] = jnp.exp(b_last)
        o_s[rg, :] = _dot(_bf(a_sum), vb)
        qs_s[rg, :] = jnp.concatenate(qs, axis=1)
        kv_s[cg] = _dot_tn(vb, jnp.concatenate(kds, axis=1))

    _for_chunks(n_ctx, n_lat, phase_a, unroll_a)
    st_s[...] = jnp.zeros(st_s.shape, F32)

    def phase_b(i, carry):
        for d, c in ((0, i), (1, _bwd_chunk(i, n_ctx, n_lat))):
            ln = _lanes(d, HGRN_DK)
            st = st_s[d]
            sall_s[c, :, ln] = _bf(st)
            st_s[d] = dec_s[c, :, ln] * st + kv_s[c, :, ln]
        return carry

    lax.fori_loop(0, nch, phase_b, 0)
    gn = gn_ref[...]

    def phase_c(part, cl, cg):
        rl, rg = _rows(cl), _rows(cg)
        o = o_s[rg, :] + _dot_nt(qs_s[rg, :], sall_s[cg])
        outr[part][rl, :] = (_rms(o, gn) * _silu(gr[part][rl, :])).astype(BF16)

    _for_chunks(n_ctx, n_lat, phase_c, unroll_c, with_ctx=oc_ref is not None)


def _hgrn(p, nb, lat_len, ctx_len, log_lb, log_ub, gn, consts, emit_ctx):
    mats, masks, signs = consts
    t_len = lat_len + ctx_len
    nch = t_len // CHUNK
    spec = functools.partial(_seq_specs, width=128, nb=nb, lat_len=lat_len, ctx_len=ctx_len)
    vec = pl.BlockSpec((1, 128), lambda b, h: (0, h))
    in_specs = [*spec(_C_HG_Q), *spec(_C_HG_FF), *spec(_C_HG_FB), *spec(_C_HG_I), *spec(_C_HG_G),
                vec, vec, vec,
                pl.BlockSpec(mats.shape, lambda b, h: (0, 0, 0)),
                pl.BlockSpec(masks.shape, lambda b, h: (0, 0, 0, 0)),
                pl.BlockSpec(signs.shape, lambda b, h: (0, 0, 0, 0))]
    scratch = [pltpu.VMEM((t_len, 2 * HGRN_DK), BF16),
               pltpu.VMEM((nch, HGRN_DV, 2 * HGRN_DK), F32),
               pltpu.VMEM((nch, HGRN_DV, 2 * HGRN_DK), BF16),
               pltpu.VMEM((nch, 1, 2 * HGRN_DK), F32),
               pltpu.VMEM((t_len, HGRN_DV), F32),
               pltpu.VMEM((2, HGRN_DV, HGRN_DK), F32)]
    args = (p, p, p, p, p, p, p, p, p, p, log_lb.reshape(1, HGRN_QK), log_ub.reshape(1, HGRN_QK),
            gn.reshape(1, HGRN_WIDTH), mats, masks, signs)
    return _mixer_call(_hgrn_kernel, "hgrn2", in_specs, args, nb, HGRN_HEADS, lat_len, ctx_len, HGRN_DV,
                       scratch, emit_ctx, 48)


def _mlstm_kernel(ql, qc, kl, kc, vl, vc, ogl, ogc, gcl, gcc, grl, grc, bc_ref, br_ref,
                  wq_ref, wk_ref, gn_ref, ol_ref, oc_ref,
                  q_s, gr_s, intra_s, kv_s, call_s, rho_s, bcol_s, bl_s, mu_s, mprev_s, st_s, m_s,
                  *, n_ctx, n_lat):
    nch = n_ctx + n_lat
    unroll_a, unroll_c = MLSTM_UNROLL
    h = pl.program_id(1)
    nh = MLSTM_HEADS
    dk, dv = MLSTM_DK, MLSTM_DV
    ext = 2 * dv
    qr, kr, vr, ogr, gcr, outr = (qc, ql), (kc, kl), (vc, vl), (ogc, ogl), (gcc, gcl), (oc_ref, ol_ref)
    n_loc = (n_ctx, n_lat)

    for cc in range(nch):
        src, c0 = (grc, cc) if cc < n_ctx else (grl, cc - n_ctx)
        gr_s[cc] = src[0, :, c0 * CHUNK:(c0 + 1) * CHUNK] + br_ref[...]

    row = lax.broadcasted_iota(jnp.int32, (CHUNK, dk), 0)
    lane = lax.broadcasted_iota(jnp.int32, (CHUNK, dv), 1)
    del lane
    ones_col = jnp.ones((CHUNK, dv), BF16)
    scale = dk ** -0.5
    t_i = lax.broadcasted_iota(jnp.int32, (CHUNK, CHUNK), 0)
    s_i = lax.broadcasted_iota(jnp.int32, (CHUNK, CHUNK), 1)
    lane16 = lax.broadcasted_iota(jnp.int32, (CHUNK, MLSTM_GATES), 1)
    sub16 = lax.broadcasted_iota(jnp.int32, (MLSTM_GATES, CHUNK), 0)

    def conv(src, w_ref, part, cl):
        r0 = pl.multiple_of(cl * CHUNK, CHUNK)
        n_rows = n_loc[part] * CHUNK
        x = src[pl.ds(r0, CHUNK), :]
        pr = src[pl.ds(jnp.maximum(r0 - 1, 0), 1), :]
        nx = src[pl.ds(jnp.minimum(r0 + CHUNK, n_rows - 1), 1), :]
        pr = jnp.where(cl != 0, pr, 0.0)
        nx = jnp.where(cl != n_loc[part] - 1, nx, 0.0)
        xp = jnp.where(row == 0, pr, pltpu.roll(x, 1, 0))
        xn = jnp.where(row == CHUNK - 1, nx, pltpu.roll(x, CHUNK - 1, 0))
        w = w_ref[...]
        return _silu(w[0:1] * xp + w[1:2] * x + w[2:3] * xn)

    def pick_col(g, j):
        return jnp.sum(jnp.where(lane16 == j, g, 0.0), axis=1, keepdims=True)

    def pick_row(g, j):
        return jnp.sum(jnp.where(sub16 == j, g, 0.0), axis=0, keepdims=True)

    def phase_a(part, cl, cg):
        rl, rg = _rows(cl), _rows(cg)
        q = conv(qr[part], wq_ref, part, cl)
        k = conv(kr[part], wk_ref, part, cl) * scale
        qb = _bf(q)
        q_s[rg, :] = qb
        vb = jnp.concatenate([_bf(vr[part][rl, :]), ones_col], axis=1)
        s = _dot_nt(qb, _bf(k))
        g_c = gcr[part][rl, :] + bc_ref[...]
        g_r = gr_s[cg]
        kws = []
        for d in range(2):
            ig_c = pick_col(g_c, d * nh + h)
            lf_c = _log_sigmoid(pick_col(g_c, 2 * nh + d * nh + h))
            ig_r = pick_row(g_r, d * nh + h)
            lf_r = _log_sigmoid(pick_row(g_r, 2 * nh + d * nh + h))
            tri = (s_i <= t_i) if d == 0 else (s_i >= t_i)
            tri_t = (t_i <= s_i) if d == 0 else (t_i >= s_i)
            b_c = jnp.sum(jnp.where(tri, lf_r, 0.0), axis=1, keepdims=True)
            b_r = jnp.sum(jnp.where(tri_t, lf_c, 0.0), axis=0, keepdims=True)
            dmat = jnp.where(tri, b_c - b_r + ig_r, -jnp.inf)
            rho = jnp.max(dmat, axis=1, keepdims=True)
            intra_s[d, rg, :] = _dot(_bf(s * jnp.exp(dmat - rho)), vb)
            b_last = b_c[CHUNK - 1:CHUNK] if d == 0 else b_c[0:1]
            mu = jnp.max(b_last - b_r + ig_r, axis=1, keepdims=True)
            kws.append(_bf(k * jnp.exp(b_last - b_c + ig_c - mu)))
            rho_s[d, rg, :] = jnp.broadcast_to(rho, (CHUNK, 128))
            bcol_s[d, rg, :] = jnp.broadcast_to(b_c, (CHUNK, 128))
            bl_s[cg, :, _lanes(d, 128)] = jnp.broadcast_to(b_last, (1, 128))
            mu_s[cg, :, _lanes(d, 128)] = jnp.broadcast_to(mu, (1, 128))
        kv_s[cg] = _dot_tn(jnp.concatenate(kws, axis=1), vb)

    _for_chunks(n_ctx, n_lat, phase_a, unroll_a)
    st_s[...] = jnp.zeros(st_s.shape, F32)
    m_s[...] = jnp.zeros(m_s.shape, F32)

    def wide(v):
        return jnp.concatenate([v, v], axis=1)

    def phase_b(i, carry):
        for d, c in ((0, i), (1, _bwd_chunk(i, n_ctx, n_lat))):
            ln = _lanes(d, 128)
            m_prev = m_s[d]
            st = st_s[d]
            mprev_s[c, :, ln] = m_prev
            call_s[c, :, _lanes(d, ext)] = _bf(st)
            bl = bl_s[c, :, ln]
            mu = mu_s[c, :, ln]
            m_new = jnp.maximum(bl + m_prev, mu)
            st_s[d] = (wide(jnp.exp(bl + m_prev - m_new)) * st
                       + wide(jnp.exp(mu - m_new)) * kv_s[c, _lanes(d, dk), :])
            m_s[d] = m_new
        return carry

    lax.fori_loop(0, nch, phase_b, 0)
    gn = gn_ref[...]

    def phase_c(part, cl, cg):
        rl, rg = _rows(cl), _rows(cg)
        qc_all = _dot(q_s[rg, :], call_s[cg])
        o = None
        for d in range(2):
            rho = rho_s[d, rg, :]
            b_c = bcol_s[d, rg, :]
            m_prev = mprev_s[cg, :, _lanes(d, 128)]
            m_t = jnp.maximum(rho, b_c + m_prev)
            nd = (wide(jnp.exp(rho - m_t)) * intra_s[d, rg, :]
                  + wide(jnp.exp(b_c + m_prev - m_t)) * qc_all[:, _lanes(d, ext)])
            hh = nd[:, 0:dv] / jnp.maximum(jnp.abs(nd[:, dv:]), jnp.exp(-m_t))
            o = hh if o is None else o + hh
        y = o - jnp.mean(o, axis=-1, keepdims=True)
        y = y * lax.rsqrt(jnp.mean(y * y, axis=-1, keepdims=True) + NORM_EPS)
        outr[part][rl, :] = (y * gn * _sigmoid(ogr[part][rl, :])).astype(BF16)

    _for_chunks(n_ctx, n_lat, phase_c, unroll_c, with_ctx=oc_ref is not None)


def _mlstm(p, g_col, g_row_lat, g_row_ctx, nb, lat_len, ctx_len, gate_b, conv_w, gn, emit_ctx):
    t_len = lat_len + ctx_len
    nch = t_len // CHUNK
    n_lat_blk = nb * lat_len // ctx_len
    ng = MLSTM_GATES
    dk, dv = MLSTM_DK, MLSTM_DV
    spec = functools.partial(_seq_specs, width=128, nb=nb, lat_len=lat_len, ctx_len=ctx_len)
    in_specs = [*spec(_C_ML_Q), *spec(_C_ML_K), *spec(_C_ML_V), *spec(_C_ML_O),
                pl.BlockSpec((lat_len, ng), lambda b, h: (b, 0)),
                pl.BlockSpec((ctx_len, ng), lambda b, h: (n_lat_blk + b, 0)),
                pl.BlockSpec((1, ng, lat_len), lambda b, h: (b, 0, 0)),
                pl.BlockSpec((1, ng, ctx_len), lambda b, h: (b, 0, 0)),
                pl.BlockSpec((1, ng), lambda b, h: (0, 0)),
                pl.BlockSpec((ng, 1), lambda b, h: (0, 0)),
                pl.BlockSpec((3, 128), lambda b, h: (0, h)),
                pl.BlockSpec((3, 128), lambda b, h: (0, MLSTM_HEADS + h)),
                pl.BlockSpec((1, 128), lambda b, h: (0, h))]
    scratch = [pltpu.VMEM((t_len, dk), BF16),
               pltpu.VMEM((nch, ng, CHUNK), F32),
               pltpu.VMEM((2, t_len, 2 * dv), F32),
               pltpu.VMEM((nch, 2 * dk, 2 * dv), F32),
               pltpu.VMEM((nch, dk, 4 * dv), BF16),
               pltpu.VMEM((2, t_len, 128), F32), pltpu.VMEM((2, t_len, 128), F32),
               pltpu.VMEM((nch, 1, 256), F32), pltpu.VMEM((nch, 1, 256), F32), pltpu.VMEM((nch, 1, 256), F32),
               pltpu.VMEM((2, dk, 2 * dv), F32), pltpu.VMEM((2, 1, 128), F32)]
    args = (p, p, p, p, p, p, p, p, g_col, g_col, g_row_lat, g_row_ctx,
            gate_b.reshape(1, ng), gate_b.reshape(ng, 1), conv_w, conv_w, gn.reshape(1, MLSTM_WIDTH))
    return _mixer_call(_mlstm_kernel, "mlstm", in_specs, args, nb, MLSTM_HEADS, lat_len, ctx_len, dv,
                       scratch, emit_ctx, 48)


def _wout_kernel(*refs, n_lat_tiles, n_mix_parts, n_x_parts, emit_next):
    it = iter(refs)
    mix = [[next(it) for _ in range(n_mix_parts)] for _ in range(3)]
    x_parts = [next(it) for _ in range(n_x_parts)]
    wr_ref, wh_ref, wm_ref, g_ref, mod_ref = (next(it) for _ in range(5))
    g2_ref = next(it) if emit_next else None
    o_ref = next(it)
    h2_ref = next(it) if emit_next else None

    def run(part):
        pm = min(part, n_mix_parts - 1)
        y = (_dot(mix[0][pm][...], wr_ref[...]) + _dot(mix[1][pm][...], wh_ref[...])
             + _dot(mix[2][pm][...], wm_ref[...]))
        m = mod_ref[0]
        xn = x_parts[min(part, n_x_parts - 1)][...] + m[2:3] * _rms(y, g_ref[...])
        o_ref[...] = xn
        if emit_next:
            h2_ref[...] = (_rms(xn, g2_ref[...]) * (1.0 + m[4:5]) + m[3:4]).astype(h2_ref.dtype)

    _on_part(n_lat_tiles, max(n_mix_parts, n_x_parts), run)


def _wout(mix_parts, w_out, x_parts, n_rows, g1, modtab, lat_rows, g2_next):
    d = x_parts[0].shape[1]
    nb = modtab.shape[0] - 1
    mi = _mod_index(ROW_TILE, lat_rows, nb)
    tm = ROW_TILE
    n_lat_tiles = mix_parts[0][0].shape[0] // tm
    emit_next = g2_next is not None
    widths = (RET_WIDTH, HGRN_WIDTH, MLSTM_WIDTH)
    w_r = w_out[0:RET_WIDTH]
    w_h = w_out[RET_WIDTH:RET_WIDTH + HGRN_WIDTH]
    w_m = w_out[RET_WIDTH + HGRN_WIDTH:]
    full = lambda a: pl.BlockSpec(a.shape, lambda i: (0, 0))
    in_specs, args = [], []
    for parts, w in zip(mix_parts, widths):
        in_specs += _part_specs(parts, tm, w)
        args += list(parts)
    in_specs += _part_specs(x_parts, tm, d)
    args += list(x_parts)
    in_specs += [full(w_r), full(w_h), full(w_m),
                 pl.BlockSpec((1, d), lambda i: (0, 0)),
                 pl.BlockSpec((1, 6, d), lambda i: (mi(i), 0, 0))]
    args += [w_r, w_h, w_m, g1.reshape(1, d), modtab]
    out_specs = [pl.BlockSpec((tm, d), lambda i: (i, 0))]
    out_shape = [jax.ShapeDtypeStruct((n_rows, d), F32)]
    if emit_next:
        in_specs.append(pl.BlockSpec((1, d), lambda i: (0, 0)))
        args.append(g2_next.reshape(1, d))
        out_specs.append(pl.BlockSpec((tm, d), lambda i: (i, 0)))
        out_shape.append(jax.ShapeDtypeStruct((n_rows, d), BF16))
    return pl.pallas_call(
        functools.partial(_wout_kernel, n_lat_tiles=n_lat_tiles, n_mix_parts=len(mix_parts[0]),
                          n_x_parts=len(x_parts), emit_next=emit_next),
        grid=(n_rows // tm,),
        in_specs=in_specs,
        out_specs=out_specs,
        out_shape=out_shape,
        compiler_params=_cparams(("arbitrary",), 52),
        name="wout",
    )(*args)


def _ffn_kernel(*refs, emit_next):
    if emit_next:
        h_ref, w1_ref, w3_ref, w2_ref, x_ref, g_ref, mod_ref, gn_ref, modn_ref, o_ref, hn_ref, acc_ref = refs
    else:
        h_ref, w1_ref, w3_ref, w2_ref, x_ref, g_ref, mod_ref, o_ref, acc_ref = refs
    f = pl.program_id(1)

    @pl.when(f == 0)
    def _():
        acc_ref[...] = jnp.zeros(acc_ref.shape, F32)

    h = h_ref[...]
    u = _silu(_dot(h, w1_ref[...])) * _dot(h, w3_ref[...])
    acc_ref[...] += _dot(_bf(u), w2_ref[...])

    @pl.when(f == pl.num_programs(1) - 1)
    def _():
        m = mod_ref[0]
        xn = x_ref[...] + m[5:6] * _rms(acc_ref[...], g_ref[...])
        o_ref[...] = xn
        if emit_next:
            mn = modn_ref[0]
            hn_ref[...] = (_rms(xn, gn_ref[...]) * (1.0 + mn[1:2]) + mn[0:1]).astype(hn_ref.dtype)


def _ffn(hb, w1, w3, w2, xs, g3, modtab, lat_rows, g_next, modtab_next):
    n_rows, d = hb.shape
    dff = w1.shape[1]
    nb = modtab.shape[0] - 1
    tm, tf = ROW_TILE, FFN_TF
    mi = _mod_index(tm, lat_rows, nb)
    emit_next = g_next is not None
    nf = dff // tf

    def fs(i, f):
        return jnp.where(i % 2 == 1, nf - 1 - f, f)

    in_specs = [pl.BlockSpec((tm, d), lambda i, f: (i, 0)),
                pl.BlockSpec((d, tf), lambda i, f: (0, fs(i, f))),
                pl.BlockSpec((d, tf), lambda i, f: (0, fs(i, f))),
                pl.BlockSpec((tf, d), lambda i, f: (fs(i, f), 0)),
                pl.BlockSpec((tm, d), lambda i, f: (i, 0)),
                pl.BlockSpec((1, d), lambda i, f: (0, 0)),
                pl.BlockSpec((1, 6, d), lambda i, f: (mi(i), 0, 0))]
    args = [hb, w1, w3, w2, xs, g3.reshape(1, d), modtab]
    out_specs = [pl.BlockSpec((tm, d), lambda i, f: (i, 0))]
    out_shape = [jax.ShapeDtypeStruct((n_rows, d), F32)]
    if emit_next:
        in_specs += [pl.BlockSpec((1, d), lambda i, f: (0, 0)),
                     pl.BlockSpec((1, 6, d), lambda i, f: (mi(i), 0, 0))]
        args += [g_next.reshape(1, d), modtab_next]
        out_specs.append(pl.BlockSpec((tm, d), lambda i, f: (i, 0)))
        out_shape.append(jax.ShapeDtypeStruct((n_rows, d), BF16))
    return pl.pallas_call(
        functools.partial(_ffn_kernel, emit_next=emit_next),
        grid=(n_rows // tm, dff // tf),
        in_specs=in_specs,
        out_specs=out_specs,
        out_shape=out_shape,
        scratch_shapes=[pltpu.VMEM((tm, d), F32)],
        compiler_params=_cparams(("arbitrary", "arbitrary"), 48),
        name="ffn",
    )(*args)


_META_E0, _META_E1, _META_R0, _META_R1, _META_G0, _META_G1 = range(6)


def _router_kernel(x_ref, g_ref, mod_ref, wr_ref, tri_ref, meta_ref, cnt_ref, carry_ref):
    i = pl.program_id(0)

    @pl.when(i == 0)
    def _():
        carry_ref[...] = jnp.zeros(carry_ref.shape, F32)

    m = mod_ref[0]
    hmod = _rms(x_ref[...], g_ref[...]) * (1.0 + m[4:5]) + m[3:4]
    logits = jnp.dot(hmod, wr_ref[...], precision=lax.Precision.HIGHEST, preferred_element_type=F32)
    lane = lax.broadcasted_iota(jnp.int32, logits.shape, 1)
    lanef = lane.astype(F32)
    logits = jnp.where(lane < N_EXPERTS, logits, -jnp.inf)
    v0 = jnp.max(logits, axis=1, keepdims=True)
    e0 = jnp.min(jnp.where(logits == v0, lanef, 1e9), axis=1, keepdims=True)
    rest = jnp.where(lanef == e0, -jnp.inf, logits)
    v1 = jnp.max(rest, axis=1, keepdims=True)
    e1 = jnp.min(jnp.where(rest == v1, lanef, 1e9), axis=1, keepdims=True)
    ex = jnp.exp(v1 - v0)
    g0 = 1.0 / (1.0 + ex)
    g1 = ex / (1.0 + ex)
    oh0 = lanef == e0
    oh1 = lanef == e1
    oh = jnp.where(jnp.logical_or(oh0, oh1), 1.0, 0.0)
    before = _dot(tri_ref[...], _bf(oh)) + carry_ref[0:1, :]
    r0 = jnp.sum(jnp.where(oh0, before, 0.0), axis=1, keepdims=True)
    r1 = jnp.sum(jnp.where(oh1, before, 0.0), axis=1, keepdims=True)
    carry_ref[0:1, :] = carry_ref[0:1, :] + jnp.sum(oh, axis=0, keepdims=True)
    meta = jnp.zeros(logits.shape, F32)
    for j, val in ((_META_E0, e0), (_META_E1, e1), (_META_R0, r0), (_META_R1, r1),
                   (_META_G0, g0), (_META_G1, g1)):
        meta = jnp.where(lane == j, val, meta)
    meta_ref[...] = meta
    cnt_ref[...] = carry_ref[...]


def _router(xs, n_rows, g2, modtab, w_router, lat_rows):
    d = xs.shape[1]
    nb = modtab.shape[0] - 1
    tm = ROW_TILE
    mi = _mod_index(tm, lat_rows, nb)
    wr = jnp.zeros((d, 128), F32).at[:, :N_EXPERTS].set(w_router)
    tri = jnp.asarray(np.tril(np.ones((tm, tm), np.float32), -1), BF16)
    return pl.pallas_call(
        _router_kernel,
        grid=(n_rows // tm,),
        in_specs=[pl.BlockSpec((tm, d), lambda i: (i, 0)),
                  pl.BlockSpec((1, d), lambda i: (0, 0)),
                  pl.BlockSpec((1, 6, d), lambda i: (mi(i), 0, 0)),
                  pl.BlockSpec((d, 128), lambda i: (0, 0)),
                  pl.BlockSpec((tm, tm), lambda i: (0, 0))],
        out_specs=[pl.BlockSpec((tm, 128), lambda i: (i, 0)),
                   pl.BlockSpec((8, 128), lambda i: (0, 0))],
        out_shape=[jax.ShapeDtypeStruct((n_rows, 128), F32),
                   jax.ShapeDtypeStruct((8, 128), F32)],
        scratch_shapes=[pltpu.VMEM((8, 128), F32)],
        compiler_params=_cparams(("arbitrary",), 32),
        name="router",
    )(xs, g2.reshape(1, d), modtab, wr, tri)


def _dispatch_kernel(dest_ref, x_ref, g_ref, mod_ref, init_ref, o_ref, h_s, sem):
    del init_ref
    i = pl.program_id(0)
    tm = h_s.shape[0]
    m = mod_ref[0]
    h_s[...] = _rms(x_ref[...], g_ref[...]) * (1.0 + m[4:5]) + m[3:4]

    def row_copy(r, k):
        dst = dest_ref[TOP_K * (i * tm + r) + k]
        return pltpu.make_async_copy(h_s.at[pl.ds(r, 1)], o_ref.at[pl.ds(dst, 1)], sem)

    def start(r, carry):
        for k in range(TOP_K):
            row_copy(r, k).start(priority=k)
        return carry

    lax.fori_loop(0, tm, start, 0, unroll=DMA_UNROLL)

    def wait(r, carry):
        for k in range(TOP_K):
            row_copy(r, k).wait()
        return carry

    lax.fori_loop(0, tm, wait, 0, unroll=DMA_UNROLL)


def _dispatch(dest, xs, n_rows, g2, modtab, n_slots, lat_rows):
    d = xs.shape[1]
    nb = modtab.shape[0] - 1
    tm = ROW_TILE
    mi = _mod_index(tm, lat_rows, nb)
    init = jnp.zeros((n_slots, d), F32)
    grid_spec = pltpu.PrefetchScalarGridSpec(
        num_scalar_prefetch=1,
        grid=(n_rows // tm,),
        in_specs=[pl.BlockSpec((tm, d), lambda i, dst: (i, 0)),
                  pl.BlockSpec((1, d), lambda i, dst: (0, 0)),
                  pl.BlockSpec((1, 6, d), lambda i, dst: (mi(i), 0, 0)),
                  pl.BlockSpec(memory_space=pl.ANY)],
        out_specs=pl.BlockSpec(memory_space=pl.ANY),
        scratch_shapes=[pltpu.VMEM((tm, d), F32), pltpu.SemaphoreType.DMA(())],
    )
    return pl.pallas_call(
        _dispatch_kernel,
        grid_spec=grid_spec,
        out_shape=jax.ShapeDtypeStruct((n_slots, d), F32),
        input_output_aliases={4: 0},
        compiler_params=_cparams(("arbitrary",), 32),
        name="dispatch",
    )(dest, xs, g2.reshape(1, d), modtab, init)


def _expert_kernel(te_ref, nu_ref, x_ref, w1_ref, w3_ref, w2_ref, o_ref, xb_s, acc_s):
    j = pl.program_id(0)
    f = pl.program_id(1)

    @pl.when(j < nu_ref[0])
    def _():
        @pl.when(f == 0)
        def _():
            xb_s[...] = _bf(x_ref[...])
            acc_s[...] = jnp.zeros(acc_s.shape, F32)

        h = xb_s[...]
        u = _silu(_dot(h, _bf(w1_ref[0]))) * _dot(h, _bf(w3_ref[0]))
        acc_s[...] += _dot(_bf(u), _bf(w2_ref[0]))

        @pl.when(f == pl.num_programs(1) - 1)
        def _():
            o_ref[...] = acc_s[...]

    @pl.when(jnp.logical_and(j >= nu_ref[0], f == pl.num_programs(1) - 1))
    def _():
        o_ref[...] = jnp.zeros(o_ref.shape, F32)


def _experts(tile_e, n_used, xsort, w1, w3, w2):
    n_slots, d = xsort.shape
    dff = w1.shape[2]
    tm, tf = MOE_TM, FFN_TF
    nf = dff // tf

    def jj(j, nu):
        return jnp.maximum(jnp.minimum(j, nu[0] - 1), 0)

    def ff(j, f, nu):
        snake = jnp.where(j % 2 == 1, nf - 1 - f, f)
        last = jnp.where((nu[0] - 1) % 2 == 1, 0, nf - 1)
        return jnp.where(j < nu[0], snake, last)

    grid_spec = pltpu.PrefetchScalarGridSpec(
        num_scalar_prefetch=2,
        grid=(n_slots // tm, nf),
        in_specs=[pl.BlockSpec((tm, d), lambda j, f, te, nu: (jj(j, nu), 0)),
                  pl.BlockSpec((1, d, tf), lambda j, f, te, nu: (te[jj(j, nu)], 0, ff(j, f, nu))),
                  pl.BlockSpec((1, d, tf), lambda j, f, te, nu: (te[jj(j, nu)], 0, ff(j, f, nu))),
                  pl.BlockSpec((1, tf, d), lambda j, f, te, nu: (te[jj(j, nu)], ff(j, f, nu), 0))],
        out_specs=pl.BlockSpec((tm, d), lambda j, f, te, nu: (j, 0)),
        scratch_shapes=[pltpu.VMEM((tm, d), BF16), pltpu.VMEM((tm, d), F32)],
    )
    return pl.pallas_call(
        _expert_kernel,
        grid_spec=grid_spec,
        out_shape=jax.ShapeDtypeStruct((n_slots, d), F32),
        compiler_params=_cparams(("arbitrary", "arbitrary"), 56),
        name="experts",
    )(tile_e, n_used, xsort, w1, w3, w2)


def _combine_kernel(dest_ref, y_ref, meta_ref, x_ref, g_ref, mod_ref, o_ref, buf_s, sem):
    i = pl.program_id(0)
    tm = x_ref.shape[0]

    def row_copy(r, k):
        src = dest_ref[TOP_K * (i * tm + r) + k]
        return pltpu.make_async_copy(y_ref.at[pl.ds(src, 1)], buf_s.at[k, pl.ds(r, 1)], sem)

    def start(r, carry):
        for k in range(TOP_K):
            row_copy(r, k).start(priority=k)
        return carry

    lax.fori_loop(0, tm, start, 0, unroll=DMA_UNROLL)

    def wait(r, carry):
        for k in range(TOP_K):
            row_copy(r, k).wait()
        return carry

    lax.fori_loop(0, tm, wait, 0, unroll=DMA_UNROLL)

    meta = meta_ref[...]
    lane = lax.broadcasted_iota(jnp.int32, meta.shape, 1)
    g0 = jnp.sum(jnp.where(lane == _META_G0, meta, 0.0), axis=1, keepdims=True)
    g1 = jnp.sum(jnp.where(lane == _META_G1, meta, 0.0), axis=1, keepdims=True)
    y = buf_s[0] * g0 + buf_s[1] * g1
    m = mod_ref[0]
    o_ref[...] = x_ref[...] + m[5:6] * _rms(y, g_ref[...])


def _combine(dest, yb, meta, xs, n_rows, g3, modtab, lat_rows):
    d = xs.shape[1]
    nb = modtab.shape[0] - 1
    tm = ROW_TILE
    mi = _mod_index(tm, lat_rows, nb)
    grid_spec = pltpu.PrefetchScalarGridSpec(
        num_scalar_prefetch=1,
        grid=(n_rows // tm,),
        in_specs=[pl.BlockSpec(memory_space=pl.ANY),
                  pl.BlockSpec((tm, 128), lambda i, dst: (i, 0)),
                  pl.BlockSpec((tm, d), lambda i, dst: (i, 0)),
                  pl.BlockSpec((1, d), lambda i, dst: (0, 0)),
                  pl.BlockSpec((1, 6, d), lambda i, dst: (mi(i), 0, 0))],
        out_specs=pl.BlockSpec((tm, d), lambda i, dst: (i, 0)),
        scratch_shapes=[pltpu.VMEM((TOP_K, tm, d), F32), pltpu.SemaphoreType.DMA(())],
    )
    return pl.pallas_call(
        _combine_kernel,
        grid_spec=grid_spec,
        out_shape=jax.ShapeDtypeStruct((n_rows, d), F32),
        compiler_params=_cparams(("arbitrary",), 40),
        name="combine",
    )(dest, yb, meta, xs, g3.reshape(1, d), modtab)


def _moe(xs, n_rows, g2, g3, modtab, w_router, w1, w3, w2, lat_rows):
    meta, cnt = _router(xs, n_rows, g2, modtab, w_router, lat_rows)
    counts = cnt[0, :N_EXPERTS].astype(jnp.int32)
    padded = (counts + MOE_TM - 1) // MOE_TM * MOE_TM
    pad_end = jnp.cumsum(padded)
    pad_start = pad_end - padded
    e = meta[:, _META_E0:_META_E1 + 1].astype(jnp.int32)
    r = meta[:, _META_R0:_META_R1 + 1].astype(jnp.int32)
    dest = (pad_start[e] + r).reshape(-1)
    n_tiles = (n_rows * TOP_K) // MOE_TM + N_EXPERTS
    n_slots = n_tiles * MOE_TM
    tile_start = jnp.arange(n_tiles, dtype=jnp.int32) * MOE_TM
    tile_e = jnp.minimum(jnp.sum((tile_start[:, None] >= pad_end[None, :]).astype(jnp.int32), axis=1),
                         N_EXPERTS - 1)
    n_used = (pad_end[-1:] // MOE_TM).astype(jnp.int32)
    xsort = _dispatch(dest, xs, n_rows, g2, modtab, n_slots, lat_rows)
    yb = _experts(tile_e, n_used, xsort, w1, w3, w2)
    return _combine(dest, yb, meta, xs, n_rows, g3, modtab, lat_rows)


def _rotary_tables(lat_len, ctx_len):
    rows = lat_len // GRID_W
    row = jnp.repeat(jnp.arange(rows, dtype=F32), GRID_W)
    col = jnp.tile(jnp.arange(GRID_W, dtype=F32), rows)
    n_freq = RET_DK // 4
    inv = ROPE_BASE ** (-jnp.arange(n_freq, dtype=F32) / n_freq)
    ang = jnp.concatenate([row[:, None] * inv, col[:, None] * inv], axis=-1)
    cos, sin = jnp.cos(ang), jnp.sin(ang)
    cosf = jnp.concatenate([jnp.ones((ctx_len, RET_DK), F32), jnp.concatenate([cos, cos], -1)], 0)
    sinf = jnp.concatenate([jnp.zeros((ctx_len, RET_DK), F32), jnp.concatenate([-sin, sin], -1)], 0)
    return cosf, sinf


def kernel(x, c, ctx, c_ctx, w_ada, b_ada, norm_g, w_in, w_out, ret_decay, ret_gn, hgrn_lb, hgrn_gn,
           mlstm_conv, mlstm_gate_b, mlstm_gn, w_ffn1, w_ffn3, w_ffn2, w_router, w_exp1, w_exp3, w_exp2):
    nb, lat_len, d = x.shape
    ctx_len = ctx.shape[1]
    depth = w_ada.shape[0]
    n_lat_rows = nb * lat_len
    n_rows = n_lat_rows + nb * ctx_len

    x_parts = [x.reshape(n_lat_rows, d), ctx.reshape(nb * ctx_len, d)]
    s_in = jnp.zeros((8, d), F32).at[:nb].set(c).at[nb].set(c_ctx)
    mod_all = _ada(s_in, w_ada, b_ada)[:, :nb + 1].reshape(depth, nb + 1, 6, d)

    cosf, sinf = _rotary_tables(lat_len, ctx_len)
    hg_cum, hg_masks, hg_signs = _hgrn_constants()
    hg_consts = (jnp.asarray(hg_cum, BF16), jnp.asarray(hg_masks, F32), jnp.asarray(hg_signs, F32))
    sm = jax.nn.softmax(hgrn_lb.astype(F32), axis=0)
    lb_all = jnp.clip(jnp.cumsum(sm, axis=0) - sm[0], 0.0, 1.0)

    hb = None
    for layer in range(depth):
        last = layer == depth - 1
        modtab = mod_all[layer]
        g = norm_g[layer]

        if hb is None:
            hb = _prenorm(x_parts, n_rows, g[0], modtab, 1, 0, lat_len, BF16)
        w_in_l = _bf(w_in[layer])
        p = _mm(hb, w_in_l, MM_TM, MM_TN, n_cols=PROJ_MAIN)
        w_g = jnp.zeros((d, 128), BF16).at[:, :MLSTM_GATES].set(w_in_l[:, PROJ_MAIN:])
        gates = _mm(hb, w_g, MM_TM, 128)[:, :MLSTM_GATES]
        hb = None
        g_row_lat = jnp.swapaxes(gates[:n_lat_rows].reshape(nb, lat_len, MLSTM_GATES), 1, 2)
        g_row_ctx = jnp.swapaxes(gates[n_lat_rows:].reshape(nb, ctx_len, MLSTM_GATES), 1, 2)

        log_g = jax.nn.log_sigmoid(ret_decay[layer].astype(F32))
        lg_b = jnp.broadcast_to(log_g[:, :, None, None], (2, RET_HEADS, 8, RET_DV))
        lb = lb_all[layer]
        emit_ctx = not last
        o_ret = _retention(p, nb, lat_len, ctx_len, cosf, sinf, lg_b, ret_gn[layer], emit_ctx)
        o_hg = _hgrn(p, nb, lat_len, ctx_len, jnp.log(lb), jnp.log1p(-lb), hgrn_gn[layer],
                     hg_consts, emit_ctx)
        o_ml = _mlstm(p, gates, g_row_lat, g_row_ctx, nb, lat_len, ctx_len, mlstm_gate_b[layer],
                      mlstm_conv[layer], mlstm_gn[layer], emit_ctx)
        mix_parts = [list(o) for o in (o_ret, o_hg, o_ml)]
        rows_now = n_lat_rows if last else n_rows
        dense = layer % 2 == 0
        res = _wout(mix_parts, _bf(w_out[layer]), x_parts, rows_now, g[1], modtab, lat_len,
                    g[2] if dense else None)
        xs = res[0]
        x_parts = [xs]

        j = layer // 2
        if dense:
            nxt = (None, None) if last else (norm_g[layer + 1][0], mod_all[layer + 1])
            res = _ffn(res[1], _bf(w_ffn1[j]), _bf(w_ffn3[j]), _bf(w_ffn2[j]), xs, g[3], modtab, lat_len, *nxt)
            xs = res[0]
            hb = None if last else res[1]
        else:
            xs = _moe(xs, rows_now, g[2], g[3], modtab, w_router[j], w_exp1[j], w_exp3[j], w_exp2[j], lat_len)
        x_parts = [xs]
    return xs[:n_lat_rows].reshape(nb, lat_len, d)
```

```python
import functools

import numpy as np
import jax
import jax.numpy as jnp
from jax import lax
from jax.experimental import pallas as pl
from jax.experimental.pallas import tpu as pltpu

F32 = jnp.float32
BF16 = jnp.bfloat16

CHUNK = 128
NORM_EPS = 1e-6
ROPE_BASE = 10000.0
GRID_W = 64

RET_HEADS, RET_DK, RET_DV = 4, 128, 256
HGRN_HEADS, HGRN_DK, HGRN_DV = 4, 128, 128
MLSTM_HEADS, MLSTM_DK, MLSTM_DV = 4, 128, 128
N_EXPERTS = 8
TOP_K = 2

RET_QK = RET_HEADS * RET_DK
RET_WIDTH = RET_HEADS * RET_DV
HGRN_QK = HGRN_HEADS * HGRN_DK
HGRN_WIDTH = HGRN_HEADS * HGRN_DV
MLSTM_QK = MLSTM_HEADS * MLSTM_DK
MLSTM_WIDTH = MLSTM_HEADS * MLSTM_DV
MLSTM_GATES = 4 * MLSTM_HEADS

_C_RET_Q = 0
_C_RET_K = _C_RET_Q + RET_QK // 128
_C_RET_V = _C_RET_K + RET_QK // 128
_C_RET_G = _C_RET_V + RET_WIDTH // 128
_C_HG_Q = _C_RET_G + RET_WIDTH // 128
_C_HG_FF = _C_HG_Q + HGRN_QK // 128
_C_HG_FB = _C_HG_FF + HGRN_QK // 128
_C_HG_I = _C_HG_FB + HGRN_QK // 128
_C_HG_G = _C_HG_I + HGRN_WIDTH // 128
_C_ML_Q = _C_HG_G + HGRN_WIDTH // 128
_C_ML_K = _C_ML_Q + MLSTM_QK // 128
_C_ML_V = _C_ML_K + MLSTM_QK // 128
_C_ML_O = _C_ML_V + MLSTM_WIDTH // 128
PROJ_MAIN = (_C_ML_O + MLSTM_WIDTH // 128) * 128

_VMEM_CAP_BYTES = 56 * 1024 * 1024

ROW_TILE = 512
MM_TM, MM_TN = 1024, 768
FFN_TF = 512
MOE_TM = 512
MOE_GROUP = 2
MOE_TF = 256
RET_UNROLL = (4, 4)
HGRN_UNROLL = (2, 4)
MLSTM_UNROLL = (2, 2)
DMA_UNROLL = 8


def _cparams(sem, vmem_mb):
    return pltpu.CompilerParams(dimension_semantics=sem,
                                vmem_limit_bytes=min(int(vmem_mb * 1024 * 1024), _VMEM_CAP_BYTES))


def _bf(x):
    return x.astype(BF16)


def _dot(a, b):
    return jnp.dot(a, b, preferred_element_type=F32)


def _dot_nt(a, b):
    return lax.dot_general(a, b, (((1,), (1,)), ((), ())), preferred_element_type=F32)


def _dot_tn(a, b):
    return lax.dot_general(a, b, (((0,), (0,)), ((), ())), preferred_element_type=F32)


def _sigmoid(x):
    return 1.0 / (1.0 + jnp.exp(-x))


def _silu(x):
    return x * _sigmoid(x)


def _log_sigmoid(x):
    return jnp.minimum(x, 0.0) - jnp.log(1.0 + jnp.exp(-jnp.abs(x)))


def _rms(x, g):
    return x * lax.rsqrt(jnp.mean(x * x, axis=-1, keepdims=True) + NORM_EPS) * g


def _mod_index(tile_rows, n_lat_rows_per_batch, n_batch):
    return lambda i: jnp.minimum((i * tile_rows) // n_lat_rows_per_batch, n_batch)


def _ada_kernel(s_ref, w_ref, b_ref, o_ref):
    s = _bf(_silu(s_ref[...]))
    o_ref[0] = _dot(s, _bf(w_ref[0])) + b_ref[0]


def _ada(s_in, w_ada, b_ada):
    depth, d, n = w_ada.shape
    tn = 1024
    return pl.pallas_call(
        _ada_kernel,
        grid=(depth, n // tn),
        in_specs=[pl.BlockSpec((8, d), lambda l, j: (0, 0)),
                  pl.BlockSpec((1, d, tn), lambda l, j: (l, 0, j)),
                  pl.BlockSpec((1, 1, tn), lambda l, j: (l, 0, j))],
        out_specs=pl.BlockSpec((1, 8, tn), lambda l, j: (l, 0, j)),
        out_shape=jax.ShapeDtypeStruct((depth, 8, n), F32),
        compiler_params=_cparams(("arbitrary", "arbitrary"), 40),
        name="ada",
    )(s_in, w_ada, b_ada.reshape(depth, 1, n))


def _part_specs(parts, tm, width):
    if len(parts) == 1:
        return [pl.BlockSpec((tm, width), lambda i: (i, 0))]
    n0 = parts[0].shape[0] // tm
    return [pl.BlockSpec((tm, width), lambda i: (jnp.minimum(i, n0 - 1), 0)),
            pl.BlockSpec((tm, width), lambda i: (jnp.maximum(i - n0, 0), 0))]


def _on_part(n_first, n_parts, fn):
    if n_parts == 1:
        fn(0)
    else:
        i = pl.program_id(0)
        pl.when(i < n_first)(lambda: fn(0))
        pl.when(i >= n_first)(lambda: fn(1))


def _prenorm_kernel(*refs, sc, sh, n_first):
    x_refs, (g_ref, mod_ref, o_ref) = refs[:-3], refs[-3:]

    def run(part):
        m = mod_ref[0]
        y = _rms(x_refs[part][...], g_ref[...])
        o_ref[...] = (y * (1.0 + m[sc:sc + 1]) + m[sh:sh + 1]).astype(o_ref.dtype)

    _on_part(n_first, len(x_refs), run)


def _prenorm(x_parts, n_rows, g, modtab, sc, sh, lat_rows, out_dtype):
    d = x_parts[0].shape[1]
    nb = modtab.shape[0] - 1
    mi = _mod_index(ROW_TILE, lat_rows, nb)
    return pl.pallas_call(
        functools.partial(_prenorm_kernel, sc=sc, sh=sh, n_first=x_parts[0].shape[0] // ROW_TILE),
        grid=(n_rows // ROW_TILE,),
        in_specs=[*_part_specs(x_parts, ROW_TILE, d),
                  pl.BlockSpec((1, d), lambda i: (0, 0)),
                  pl.BlockSpec((1, 6, d), lambda i: (mi(i), 0, 0))],
        out_specs=pl.BlockSpec((ROW_TILE, d), lambda i: (i, 0)),
        out_shape=jax.ShapeDtypeStruct((n_rows, d), out_dtype),
        compiler_params=_cparams(("arbitrary",), 32),
        name="prenorm",
    )(*x_parts, g.reshape(1, d), modtab)


def _mm_kernel(x_ref, w_ref, o_ref, wb_s):
    @pl.when(pl.program_id(1) == 0)
    def _():
        wb_s[...] = _bf(w_ref[0])

    o_ref[...] = _dot(x_ref[...], wb_s[...]).astype(o_ref.dtype)


def _mm(x, w, layer, tm, tn, n_cols=None, out_dtype=F32):
    m, k = x.shape
    n = w.shape[2] if n_cols is None else n_cols
    return pl.pallas_call(
        _mm_kernel,
        grid=(n // tn, m // tm),
        in_specs=[pl.BlockSpec((tm, k), lambda j, i: (i, 0)),
                  pl.BlockSpec((1, k, tn), lambda j, i: (layer, 0, j))],
        out_specs=pl.BlockSpec((tm, tn), lambda j, i: (i, j)),
        out_shape=jax.ShapeDtypeStruct((m, n), out_dtype),
        scratch_shapes=[pltpu.VMEM((k, tn), BF16)],
        compiler_params=_cparams(("arbitrary", "arbitrary"), 48),
        name="proj",
    )(x, w)


def _bwd_chunk(i, n_ctx, n_lat):
    return jnp.where(i < n_ctx, n_ctx - 1 - i, 2 * n_ctx + n_lat - 1 - i)


def _rows(c):
    return pl.ds(pl.multiple_of(c * CHUNK, CHUNK), CHUNK)


def _for_chunks(n_ctx, n_lat, fn, unroll, with_ctx=True):
    def run(part, n, off):
        def body(c, carry):
            fn(part, c, c + off)
            return carry
        lax.fori_loop(0, n, body, 0, unroll=max(u for u in (1, 2, unroll) if n % u == 0 and u <= unroll))
    if with_ctx:
        run(0, n_ctx, 0)
    run(1, n_lat, n_ctx)


def _lanes(d, w):
    return slice(d * w, (d + 1) * w)


def _mixer_call(kernel_fn, name, in_specs, args, nb, heads, lat_len, ctx_len, dv, scratch, emit_ctx, vmem_mb):
    width = heads * dv
    out_specs = [pl.BlockSpec((lat_len, dv), lambda b, h: (b, h))]
    out_shape = [jax.ShapeDtypeStruct((nb * lat_len, width), BF16)]
    if emit_ctx:
        out_specs.append(pl.BlockSpec((ctx_len, dv), lambda b, h: (b, h)))
        out_shape.append(jax.ShapeDtypeStruct((nb * ctx_len, width), BF16))
    n_in = len(in_specs)

    def body(*refs):
        ins, rest = refs[:n_in], refs[n_in:]
        if emit_ctx:
            ol, oc, scr = rest[0], rest[1], rest[2:]
        else:
            ol, oc, scr = rest[0], None, rest[1:]
        kernel_fn(*ins, ol, oc, *scr, n_ctx=ctx_len // CHUNK, n_lat=lat_len // CHUNK)

    return pl.pallas_call(
        body,
        grid=(nb, heads),
        in_specs=in_specs,
        out_specs=out_specs,
        out_shape=out_shape,
        scratch_shapes=scratch,
        compiler_params=_cparams(("arbitrary", "arbitrary"), vmem_mb),
        name=name,
    )(*args)


def _seq_specs(col0, width, nb, lat_len, ctx_len):
    n_lat_blk = nb * lat_len // ctx_len
    c0 = col0 * 128 // width
    return (pl.BlockSpec((lat_len, width), lambda b, h: (b, c0 + h)),
            pl.BlockSpec((ctx_len, width), lambda b, h: (n_lat_blk + b, c0 + h)))


def _ret_kernel(ql, qc, kl, kc, vl, vc, gl, gc, cos_ref, sin_ref, lg_ref, gn_ref, ol_ref, oc_ref,
                qs_s, kv_s, sall_s, o_s, st_s, dm_s, dq_s, dk_s, *, n_ctx, n_lat):
    nch = n_ctx + n_lat
    unroll_a, unroll_c = RET_UNROLL
    qr, kr, vr, gr, outr = (qc, ql), (kc, kl), (vc, vl), (gc, gl), (oc_ref, ol_ref)
    t_i = lax.broadcasted_iota(jnp.int32, (CHUNK, CHUNK), 0).astype(F32)
    s_i = lax.broadcasted_iota(jnp.int32, (CHUNK, CHUNK), 1).astype(F32)
    dm = None
    for d in range(2):
        lg = lg_ref[d, 0][0:1, 0:CHUNK]
        rel = (t_i - s_i) if d == 0 else (s_i - t_i)
        dmd = jnp.where(rel >= 0, jnp.exp(jnp.maximum(rel, 0.0) * lg), 0.0)
        dm = dmd if dm is None else dm + dmd
        p = t_i if d == 0 else (CHUNK - 1.0) - t_i
        dq_s[d] = jnp.exp((p + 1.0) * lg)
        dk_s[d] = jnp.exp((CHUNK - 1.0 - p) * lg)
    dm_s[...] = dm
    scale = RET_DK ** -0.5

    def phase_a(part, cl, cg):
        rl, rg = _rows(cl), _rows(cg)
        cs = cos_ref[rg, :]
        sn = sin_ref[rg, :]
        q = qr[part][rl, :]
        k = kr[part][rl, :]
        q = q * cs + pltpu.roll(q, RET_DK // 2, 1) * sn
        k = (k * cs + pltpu.roll(k, RET_DK // 2, 1) * sn) * scale
        vb = _bf(vr[part][rl, :])
        s = _dot_nt(_bf(q), _bf(k))
        o_s[rg, :] = _dot(_bf(s * dm_s[...]), vb)
        qs_s[rg, :] = jnp.concatenate([_bf(q * dq_s[0]), _bf(q * dq_s[1])], axis=1)
        kd = jnp.concatenate([_bf(k * dk_s[0]), _bf(k * dk_s[1])], axis=1)
        kv_s[cg] = _dot_tn(kd, vb)

    _for_chunks(n_ctx, n_lat, phase_a, unroll_a)

    st_s[...] = jnp.zeros(st_s.shape, F32)
    dec = [jnp.exp(float(CHUNK) * lg_ref[d, 0][0:1, :]) for d in range(2)]

    def phase_b(i, carry):
        for d, c in ((0, i), (1, _bwd_chunk(i, n_ctx, n_lat))):
            st = st_s[d]
            sall_s[c, _lanes(d, RET_DK), :] = _bf(st)
            st_s[d] = dec[d] * st + kv_s[c, _lanes(d, RET_DK), :]
        return carry

    lax.fori_loop(0, nch, phase_b, 0)
    gn = gn_ref[...]

    def phase_c(part, cl, cg):
        rl, rg = _rows(cl), _rows(cg)
        o = o_s[rg, :] + _dot(qs_s[rg, :], sall_s[cg])
        outr[part][rl, :] = (_rms(o, gn) * _silu(gr[part][rl, :])).astype(BF16)

    _for_chunks(n_ctx, n_lat, phase_c, unroll_c, with_ctx=oc_ref is not None)


def _retention(p, nb, lat_len, ctx_len, cosf, sinf, lg_b, gn, emit_ctx):
    t_len = lat_len + ctx_len
    nch = t_len // CHUNK
    s128 = functools.partial(_seq_specs, width=128, nb=nb, lat_len=lat_len, ctx_len=ctx_len)
    s256 = functools.partial(_seq_specs, width=256, nb=nb, lat_len=lat_len, ctx_len=ctx_len)
    in_specs = [*s128(_C_RET_Q), *s128(_C_RET_K), *s256(_C_RET_V), *s256(_C_RET_G),
                pl.BlockSpec((t_len, 128), lambda b, h: (0, 0)),
                pl.BlockSpec((t_len, 128), lambda b, h: (0, 0)),
                pl.BlockSpec((2, 1, 8, RET_DV), lambda b, h: (0, h, 0, 0)),
                pl.BlockSpec((1, RET_DV), lambda b, h: (0, h))]
    scratch = [pltpu.VMEM((t_len, 2 * RET_DK), BF16),
               pltpu.VMEM((nch, 2 * RET_DK, RET_DV), F32),
               pltpu.VMEM((nch, 2 * RET_DK, RET_DV), BF16),
               pltpu.VMEM((t_len, RET_DV), F32),
               pltpu.VMEM((2, RET_DK, RET_DV), F32),
               pltpu.VMEM((CHUNK, CHUNK), F32),
               pltpu.VMEM((2, CHUNK, CHUNK), F32), pltpu.VMEM((2, CHUNK, CHUNK), F32)]
    args = (p, p, p, p, p, p, p, p, cosf, sinf, lg_b, gn.reshape(1, RET_WIDTH))
    return _mixer_call(_ret_kernel, "retention", in_specs, args, nb, RET_HEADS, lat_len, ctx_len, RET_DV,
                       scratch, emit_ctx, 48)


_HG_LEVELS = (64, 32, 16, 8, 4, 2, 1)


def _hgrn_constants():
    c = CHUNK
    t = np.arange(c)[:, None]
    u = np.arange(c)[None, :]
    cum = (u <= t).astype(np.float32)
    masks, signs = [], []
    for m in _HG_LEVELS:
        base = (t // (2 * m)) * (2 * m)
        lower = t >= base + m
        tb = t // (2 * m)
        sb = u // (2 * m)
        masks.append((tb == sb) & lower & (u < (sb * 2 * m + m)))
        signs.append(np.broadcast_to(np.where(lower, 1.0, -1.0), (c, c)))
    masks.append(t == u)
    kf = np.stack([x.astype(np.float32) for x in masks], axis=0)
    kb = np.stack([x.astype(np.float32)[::-1, ::-1] for x in masks], axis=0)
    sf = np.stack([x.astype(np.float32) for x in signs], axis=0)
    sb_ = np.stack([x.astype(np.float32)[::-1, ::-1] for x in signs], axis=0)
    return (np.stack([cum, cum[::-1, ::-1]], 0), np.stack([kf, kb], 0),
            np.stack([sf[:_HG_WIDE], sb_[:_HG_WIDE]], 0))


_HG_WIDE = 5


def _hgrn_level_exponents(b, lf, d, sgn_ref, row):
    out = []
    for l, m in enumerate(_HG_LEVELS[:_HG_WIDE]):
        pieces = []
        for j in range(CHUNK // (2 * m)):
            r = 2 * m * j + (m - 1 if d == 0 else m)
            pieces.append(jnp.broadcast_to(b[r:r + 1, :], (2 * m, CHUNK)))
        bref = pieces[0] if len(pieces) == 1 else jnp.concatenate(pieces, axis=0)
        out.append((b - bref) * sgn_ref[d, l])
    up = pltpu.roll(lf, CHUNK - 1, 0)
    dn = pltpu.roll(lf, 1, 0)
    r4 = row % 4
    if d == 0:
        e2 = jnp.where(r4 == 0, up, jnp.where(r4 == 1, 0.0, jnp.where(r4 == 2, lf, lf + dn)))
        e1 = jnp.where(row % 2 == 1, lf, 0.0)
    else:
        e2 = jnp.where(r4 == 0, lf + up, jnp.where(r4 == 1, lf, jnp.where(r4 == 2, 0.0, dn)))
        e1 = jnp.where(row % 2 == 0, lf, 0.0)
    return out + [e2, e1]


def _hgrn_kernel(ql, qc, ffl, ffc, fbl, fbc, il, ic, gl, gc, llb_ref, lub_ref, gn_ref, c_ref, k_ref, sgn_ref,
                 ol_ref, oc_ref, qs_s, kv_s, sall_s, dec_s, o_s, st_s, *, n_ctx, n_lat):
    nch = n_ctx + n_lat
    unroll_a, unroll_c = HGRN_UNROLL
    qr, fr, ir, gr, outr =(qc, ql), ((ffc, ffl), (fbc, fbl)), (ic, il), (gc, gl), (oc_ref, ol_ref)
    llb = llb_ref[...]
    lub = lub_ref[...]
    nlev = len(_HG_LEVELS)
    row = lax.broadcasted_iota(jnp.int32, (CHUNK, HGRN_DK), 0)

    def phase_a(part, cl, cg):
        rl, rg = _rows(cl), _rows(cg)
        q = _silu(qr[part][rl, :])
        qb = _bf(q)
        vb = _bf(ir[part][rl, :])
        a_sum, qs, kds = None, [], []
        for d in range(2):
            lsg = lub + _log_sigmoid(fr[d][part][rl, :])
            lf = jnp.maximum(llb, lsg) + jnp.log(1.0 + jnp.exp(-jnp.abs(llb - lsg)))
            k = 1.0 - jnp.exp(lf)
            lf_hi = _bf(lf)
            lf_lo = _bf(lf - lf_hi.astype(F32))
            b2 = _dot(c_ref[d], jnp.concatenate([lf_hi, lf_lo], axis=1))
            b = b2[:, 0:HGRN_DK] + b2[:, HGRN_DK:]
            es = _hgrn_level_exponents(b, lf, d, sgn_ref, row)
            kb = _bf(k)
            a = k_ref[d, nlev] * _dot_nt(qb, kb)
            for l in range(nlev):
                xb = _bf(jnp.exp(es[l]))
                a = a + k_ref[d, l] * _dot_nt(qb * xb, kb * xb)
            a_sum = a if a_sum is None else a_sum + a
            b_last = b[CHUNK - 1:CHUNK] if d == 0 else b[0:1]
            qs.append(_bf(q * jnp.exp(b)))
            kds.append(_bf(k * jnp.exp(b_last - b)))
            dec_s[cg, :, _lanes(d, HGRN_DK)] = jnp.exp(b_last)
        o_s[rg, :] = _dot(_bf(a_sum), vb)
        qs_s[rg, :] = jnp.concatenate(qs, axis=1)
        kv_s[cg] = _dot_tn(vb, jnp.concatenate(kds, axis=1))

    _for_chunks(n_ctx, n_lat, phase_a, unroll_a)
    st_s[...] = jnp.zeros(st_s.shape, F32)

    def phase_b(i, carry):
        for d, c in ((0, i), (1, _bwd_chunk(i, n_ctx, n_lat))):
            ln = _lanes(d, HGRN_DK)
            st = st_s[d]
            sall_s[c, :, ln] = _bf(st)
            st_s[d] = dec_s[c, :, ln] * st + kv_s[c, :, ln]
        return carry

    lax.fori_loop(0, nch, phase_b, 0)
    gn = gn_ref[...]

    def phase_c(part, cl, cg):
        rl, rg = _rows(cl), _rows(cg)
        o = o_s[rg, :] + _dot_nt(qs_s[rg, :], sall_s[cg])
        outr[part][rl, :] = (_rms(o, gn) * _silu(gr[part][rl, :])).astype(BF16)

    _for_chunks(n_ctx, n_lat, phase_c, unroll_c, with_ctx=oc_ref is not None)


def _hgrn(p, nb, lat_len, ctx_len, log_lb, log_ub, gn, consts, emit_ctx):
    mats, masks, signs = consts
    t_len = lat_len + ctx_len
    nch = t_len // CHUNK
    spec = functools.partial(_seq_specs, width=128, nb=nb, lat_len=lat_len, ctx_len=ctx_len)
    vec = pl.BlockSpec((1, 128), lambda b, h: (0, h))
    in_specs = [*spec(_C_HG_Q), *spec(_C_HG_FF), *spec(_C_HG_FB), *spec(_C_HG_I), *spec(_C_HG_G),
                vec, vec, vec,
                pl.BlockSpec(mats.shape, lambda b, h: (0, 0, 0)),
                pl.BlockSpec(masks.shape, lambda b, h: (0, 0, 0, 0)),
                pl.BlockSpec(signs.shape, lambda b, h: (0, 0, 0, 0))]
    scratch = [pltpu.VMEM((t_len, 2 * HGRN_DK), BF16),
               pltpu.VMEM((nch, HGRN_DV, 2 * HGRN_DK), F32),
               pltpu.VMEM((nch, HGRN_DV, 2 * HGRN_DK), BF16),
               pltpu.VMEM((nch, 1, 2 * HGRN_DK), F32),
               pltpu.VMEM((t_len, HGRN_DV), F32),
               pltpu.VMEM((2, HGRN_DV, HGRN_DK), F32)]
    args = (p, p, p, p, p, p, p, p, p, p, log_lb.reshape(1, HGRN_QK), log_ub.reshape(1, HGRN_QK),
            gn.reshape(1, HGRN_WIDTH), mats, masks, signs)
    return _mixer_call(_hgrn_kernel, "hgrn2", in_specs, args, nb, HGRN_HEADS, lat_len, ctx_len, HGRN_DV,
                       scratch, emit_ctx, 48)


def _mlstm_kernel(ql, qc, kl, kc, vl, vc, ogl, ogc, gcl, gcc, grl, grc, bc_ref, br_ref,
                  wq_ref, wk_ref, gn_ref, ol_ref, oc_ref,
                  q_s, gr_s, intra_s, kv_s, call_s, rho_s, bcol_s, bl_s, mu_s, mprev_s, st_s, m_s,
                  *, n_ctx, n_lat):
    nch = n_ctx + n_lat
    unroll_a, unroll_c = MLSTM_UNROLL
    h = pl.program_id(1)
    nh = MLSTM_HEADS
    dk, dv = MLSTM_DK, MLSTM_DV
    ext = 2 * dv
    qr, kr, vr, ogr, gcr, outr = (qc, ql), (kc, kl), (vc, vl), (ogc, ogl), (gcc, gcl), (oc_ref, ol_ref)
    n_loc = (n_ctx, n_lat)

    for cc in range(nch):
        src, c0 = (grc, cc) if cc < n_ctx else (grl, cc - n_ctx)
        gr_s[cc] = src[0, :, c0 * CHUNK:(c0 + 1) * CHUNK] + br_ref[...]

    row = lax.broadcasted_iota(jnp.int32, (CHUNK, dk), 0)
    lane = lax.broadcasted_iota(jnp.int32, (CHUNK, dv), 1)
    del lane
    ones_col = jnp.ones((CHUNK, dv), BF16)
    scale = dk ** -0.5
    t_i = lax.broadcasted_iota(jnp.int32, (CHUNK, CHUNK), 0)
    s_i = lax.broadcasted_iota(jnp.int32, (CHUNK, CHUNK), 1)
    lane16 = lax.broadcasted_iota(jnp.int32, (CHUNK, MLSTM_GATES), 1)
    sub16 = lax.broadcasted_iota(jnp.int32, (MLSTM_GATES, CHUNK), 0)

    def conv(src, w_ref, part, cl):
        r0 = pl.multiple_of(cl * CHUNK, CHUNK)
        n_rows = n_loc[part] * CHUNK
        x = src[pl.ds(r0, CHUNK), :]
        pr = src[pl.ds(jnp.maximum(r0 - 1, 0), 1), :]
        nx = src[pl.ds(jnp.minimum(r0 + CHUNK, n_rows - 1), 1), :]
        pr = jnp.where(cl != 0, pr, 0.0)
        nx = jnp.where(cl != n_loc[part] - 1, nx, 0.0)
        xp = jnp.where(row == 0, pr, pltpu.roll(x, 1, 0))
        xn = jnp.where(row == CHUNK - 1, nx, pltpu.roll(x, CHUNK - 1, 0))
        w = w_ref[...]
        return _silu(w[0:1] * xp + w[1:2] * x + w[2:3] * xn)

    def pick_col(g, j):
        return jnp.sum(jnp.where(lane16 == j, g, 0.0), axis=1, keepdims=True)

    def pick_row(g, j):
        return jnp.sum(jnp.where(sub16 == j, g, 0.0), axis=0, keepdims=True)

    def phase_a(part, cl, cg):
        rl, rg = _rows(cl), _rows(cg)
        q = conv(qr[part], wq_ref, part, cl)
        k = conv(kr[part], wk_ref, part, cl) * scale
        qb = _bf(q)
        q_s[rg, :] = qb
        vb = jnp.concatenate([_bf(vr[part][rl, :]), ones_col], axis=1)
        s = _dot_nt(qb, _bf(k))
        g_c = gcr[part][rl, :] + bc_ref[...]
        g_r = gr_s[cg]
        kws = []
        for d in range(2):
            ig_c = pick_col(g_c, d * nh + h)
            lf_c = _log_sigmoid(pick_col(g_c, 2 * nh + d * nh + h))
            ig_r = pick_row(g_r, d * nh + h)
            lf_r = _log_sigmoid(pick_row(g_r, 2 * nh + d * nh + h))
            tri = (s_i <= t_i) if d == 0 else (s_i >= t_i)
            tri_t = (t_i <= s_i) if d == 0 else (t_i >= s_i)
            b_c = jnp.sum(jnp.where(tri, lf_r, 0.0), axis=1, keepdims=True)
            b_r = jnp.sum(jnp.where(tri_t, lf_c, 0.0), axis=0, keepdims=True)
            dmat = jnp.where(tri, b_c - b_r + ig_r, -jnp.inf)
            rho = jnp.max(dmat, axis=1, keepdims=True)
            intra_s[d, rg, :] = _dot(_bf(s * jnp.exp(dmat - rho)), vb)
            b_last = b_c[CHUNK - 1:CHUNK] if d == 0 else b_c[0:1]
            mu = jnp.max(b_last - b_r + ig_r, axis=1, keepdims=True)
            kws.append(_bf(k * jnp.exp(b_last - b_c + ig_c - mu)))
            rho_s[d, rg, :] = jnp.broadcast_to(rho, (CHUNK, 128))
            bcol_s[d, rg, :] = jnp.broadcast_to(b_c, (CHUNK, 128))
            bl_s[cg, :, _lanes(d, 128)] = jnp.broadcast_to(b_last, (1, 128))
            mu_s[cg, :, _lanes(d, 128)] = jnp.broadcast_to(mu, (1, 128))
        kv_s[cg] = _dot_tn(jnp.concatenate(kws, axis=1), vb)

    _for_chunks(n_ctx, n_lat, phase_a, unroll_a)
    st_s[...] = jnp.zeros(st_s.shape, F32)
    m_s[...] = jnp.zeros(m_s.shape, F32)

    def wide(v):
        return jnp.concatenate([v, v], axis=1)

    def phase_b(i, carry):
        for d, c in ((0, i), (1, _bwd_chunk(i, n_ctx, n_lat))):
            ln = _lanes(d, 128)
            m_prev = m_s[d]
            st = st_s[d]
            mprev_s[c, :, ln] = m_prev
            call_s[c, :, _lanes(d, ext)] = _bf(st)
            bl = bl_s[c, :, ln]
            mu = mu_s[c, :, ln]
            m_new = jnp.maximum(bl + m_prev, mu)
            st_s[d] = (wide(jnp.exp(bl + m_prev - m_new)) * st
                       + wide(jnp.exp(mu - m_new)) * kv_s[c, _lanes(d, dk), :])
            m_s[d] = m_new
        return carry

    lax.fori_loop(0, nch, phase_b, 0)
    gn = gn_ref[...]

    def phase_c(part, cl, cg):
        rl, rg = _rows(cl), _rows(cg)
        qc_all = _dot(q_s[rg, :], call_s[cg])
        o = None
        for d in range(2):
            rho = rho_s[d, rg, :]
            b_c = bcol_s[d, rg, :]
            m_prev = mprev_s[cg, :, _lanes(d, 128)]
            m_t = jnp.maximum(rho, b_c + m_prev)
            nd = (wide(jnp.exp(rho - m_t)) * intra_s[d, rg, :]
                  + wide(jnp.exp(b_c + m_prev - m_t)) * qc_all[:, _lanes(d, ext)])
            hh = nd[:, 0:dv] / jnp.maximum(jnp.abs(nd[:, dv:]), jnp.exp(-m_t))
            o = hh if o is None else o + hh
        y = o - jnp.mean(o, axis=-1, keepdims=True)
        y = y * lax.rsqrt(jnp.mean(y * y, axis=-1, keepdims=True) + NORM_EPS)
        outr[part][rl, :] = (y * gn * _sigmoid(ogr[part][rl, :])).astype(BF16)

    _for_chunks(n_ctx, n_lat, phase_c, unroll_c, with_ctx=oc_ref is not None)


def _mlstm(p, g_col, g_row_lat, g_row_ctx, nb, lat_len, ctx_len, gate_b, conv_w, gn, emit_ctx):
    t_len = lat_len + ctx_len
    nch = t_len // CHUNK
    n_lat_blk = nb * lat_len // ctx_len
    ng = MLSTM_GATES
    dk, dv = MLSTM_DK, MLSTM_DV
    spec = functools.partial(_seq_specs, width=128, nb=nb, lat_len=lat_len, ctx_len=ctx_len)
    in_specs = [*spec(_C_ML_Q), *spec(_C_ML_K), *spec(_C_ML_V), *spec(_C_ML_O),
                pl.BlockSpec((lat_len, ng), lambda b, h: (b, 0)),
                pl.BlockSpec((ctx_len, ng), lambda b, h: (n_lat_blk + b, 0)),
                pl.BlockSpec((1, ng, lat_len), lambda b, h: (b, 0, 0)),
                pl.BlockSpec((1, ng, ctx_len), lambda b, h: (b, 0, 0)),
                pl.BlockSpec((1, ng), lambda b, h: (0, 0)),
                pl.BlockSpec((ng, 1), lambda b, h: (0, 0)),
                pl.BlockSpec((3, 128), lambda b, h: (0, h)),
                pl.BlockSpec((3, 128), lambda b, h: (0, MLSTM_HEADS + h)),
                pl.BlockSpec((1, 128), lambda b, h: (0, h))]
    scratch = [pltpu.VMEM((t_len, dk), BF16),
               pltpu.VMEM((nch, ng, CHUNK), F32),
               pltpu.VMEM((2, t_len, 2 * dv), F32),
               pltpu.VMEM((nch, 2 * dk, 2 * dv), F32),
               pltpu.VMEM((nch, dk, 4 * dv), BF16),
               pltpu.VMEM((2, t_len, 128), F32), pltpu.VMEM((2, t_len, 128), F32),
               pltpu.VMEM((nch, 1, 256), F32), pltpu.VMEM((nch, 1, 256), F32), pltpu.VMEM((nch, 1, 256), F32),
               pltpu.VMEM((2, dk, 2 * dv), F32), pltpu.VMEM((2, 1, 128), F32)]
    args = (p, p, p, p, p, p, p, p, g_col, g_col, g_row_lat, g_row_ctx,
            gate_b.reshape(1, ng), gate_b.reshape(ng, 1), conv_w, conv_w, gn.reshape(1, MLSTM_WIDTH))
    return _mixer_call(_mlstm_kernel, "mlstm", in_specs, args, nb, MLSTM_HEADS, lat_len, ctx_len, dv,
                       scratch, emit_ctx, 48)


def _wout_kernel(*refs, n_lat_tiles, n_mix_parts, n_x_parts, emit_next):
    it = iter(refs)
    mix = [[next(it) for _ in range(n_mix_parts)] for _ in range(3)]
    x_parts = [next(it) for _ in range(n_x_parts)]
    wr_ref, wh_ref, wm_ref, g_ref, mod_ref = (next(it) for _ in range(5))
    g2_ref = next(it) if emit_next else None
    o_ref = next(it)
    h2_ref = next(it) if emit_next else None

    def run(part):
        pm = min(part, n_mix_parts - 1)
        y = (_dot(mix[0][pm][...], wr_ref[...]) + _dot(mix[1][pm][...], wh_ref[...])
             + _dot(mix[2][pm][...], wm_ref[...]))
        m = mod_ref[0]
        xn = x_parts[min(part, n_x_parts - 1)][...] + m[2:3] * _rms(y, g_ref[...])
        o_ref[...] = xn
        if emit_next:
            h2_ref[...] = (_rms(xn, g2_ref[...]) * (1.0 + m[4:5]) + m[3:4]).astype(h2_ref.dtype)

    _on_part(n_lat_tiles, max(n_mix_parts, n_x_parts), run)


def _wout(mix_parts, w_out, x_parts, n_rows, g1, modtab, lat_rows, g2_next):
    d = x_parts[0].shape[1]
    nb = modtab.shape[0] - 1
    mi = _mod_index(ROW_TILE, lat_rows, nb)
    tm = ROW_TILE
    n_lat_tiles = mix_parts[0][0].shape[0] // tm
    emit_next = g2_next is not None
    widths = (RET_WIDTH, HGRN_WIDTH, MLSTM_WIDTH)
    w_r = w_out[0:RET_WIDTH]
    w_h = w_out[RET_WIDTH:RET_WIDTH + HGRN_WIDTH]
    w_m = w_out[RET_WIDTH + HGRN_WIDTH:]
    full = lambda a: pl.BlockSpec(a.shape, lambda i: (0, 0))
    in_specs, args = [], []
    for parts, w in zip(mix_parts, widths):
        in_specs += _part_specs(parts, tm, w)
        args += list(parts)
    in_specs += _part_specs(x_parts, tm, d)
    args += list(x_parts)
    in_specs += [full(w_r), full(w_h), full(w_m),
                 pl.BlockSpec((1, d), lambda i: (0, 0)),
                 pl.BlockSpec((1, 6, d), lambda i: (mi(i), 0, 0))]
    args += [w_r, w_h, w_m, g1.reshape(1, d), modtab]
    out_specs = [pl.BlockSpec((tm, d), lambda i: (i, 0))]
    out_shape = [jax.ShapeDtypeStruct((n_rows, d), F32)]
    if emit_next:
        in_specs.append(pl.BlockSpec((1, d), lambda i: (0, 0)))
        args.append(g2_next.reshape(1, d))
        out_specs.append(pl.BlockSpec((tm, d), lambda i: (i, 0)))
        out_shape.append(jax.ShapeDtypeStruct((n_rows, d), BF16))
    return pl.pallas_call(
        functools.partial(_wout_kernel, n_lat_tiles=n_lat_tiles, n_mix_parts=len(mix_parts[0]),
                          n_x_parts=len(x_parts), emit_next=emit_next),
        grid=(n_rows // tm,),
        in_specs=in_specs,
        out_specs=out_specs,
        out_shape=out_shape,
        compiler_params=_cparams(("arbitrary",), 52),
        name="wout",
    )(*args)


def _ffn_kernel(*refs, emit_next):
    if emit_next:
        h_ref, w1_ref, w3_ref, w2_ref, x_ref, g_ref, mod_ref, gn_ref, modn_ref, o_ref, hn_ref, acc_ref = refs
    else:
        h_ref, w1_ref, w3_ref, w2_ref, x_ref, g_ref, mod_ref, o_ref, acc_ref = refs
    f = pl.program_id(1)

    @pl.when(f == 0)
    def _():
        acc_ref[...] = jnp.zeros(acc_ref.shape, F32)

    h = h_ref[...]
    u = _silu(_dot(h, w1_ref[...])) * _dot(h, w3_ref[...])
    acc_ref[...] += _dot(_bf(u), w2_ref[...])

    @pl.when(f == pl.num_programs(1) - 1)
    def _():
        m = mod_ref[0]
        xn = x_ref[...] + m[5:6] * _rms(acc_ref[...], g_ref[...])
        o_ref[...] = xn
        if emit_next:
            mn = modn_ref[0]
            hn_ref[...] = (_rms(xn, gn_ref[...]) * (1.0 + mn[1:2]) + mn[0:1]).astype(hn_ref.dtype)


def _ffn(hb, w1, w3, w2, xs, g3, modtab, lat_rows, g_next, modtab_next):
    n_rows, d = hb.shape
    dff = w1.shape[1]
    nb = modtab.shape[0] - 1
    tm, tf = ROW_TILE, FFN_TF
    mi = _mod_index(tm, lat_rows, nb)
    emit_next = g_next is not None
    nf = dff // tf

    def fs(i, f):
        return jnp.where(i % 2 == 1, nf - 1 - f, f)

    in_specs = [pl.BlockSpec((tm, d), lambda i, f: (i, 0)),
                pl.BlockSpec((d, tf), lambda i, f: (0, fs(i, f))),
                pl.BlockSpec((d, tf), lambda i, f: (0, fs(i, f))),
                pl.BlockSpec((tf, d), lambda i, f: (fs(i, f), 0)),
                pl.BlockSpec((tm, d), lambda i, f: (i, 0)),
                pl.BlockSpec((1, d), lambda i, f: (0, 0)),
                pl.BlockSpec((1, 6, d), lambda i, f: (mi(i), 0, 0))]
    args = [hb, w1, w3, w2, xs, g3.reshape(1, d), modtab]
    out_specs = [pl.BlockSpec((tm, d), lambda i, f: (i, 0))]
    out_shape = [jax.ShapeDtypeStruct((n_rows, d), F32)]
    if emit_next:
        in_specs += [pl.BlockSpec((1, d), lambda i, f: (0, 0)),
                     pl.BlockSpec((1, 6, d), lambda i, f: (mi(i), 0, 0))]
        args += [g_next.reshape(1, d), modtab_next]
        out_specs.append(pl.BlockSpec((tm, d), lambda i, f: (i, 0)))
        out_shape.append(jax.ShapeDtypeStruct((n_rows, d), BF16))
    return pl.pallas_call(
        functools.partial(_ffn_kernel, emit_next=emit_next),
        grid=(n_rows // tm, dff // tf),
        in_specs=in_specs,
        out_specs=out_specs,
        out_shape=out_shape,
        scratch_shapes=[pltpu.VMEM((tm, d), F32)],
        compiler_params=_cparams(("arbitrary", "arbitrary"), 48),
        name="ffn",
    )(*args)


_META_E0, _META_E1, _META_R0, _META_R1, _META_G0, _META_G1 = range(6)


def _router_kernel(x_ref, g_ref, mod_ref, wr_ref, tri_ref, meta_ref, cnt_ref, carry_ref):
    i = pl.program_id(0)

    @pl.when(i == 0)
    def _():
        carry_ref[...] = jnp.zeros(carry_ref.shape, F32)

    m = mod_ref[0]
    hmod = _rms(x_ref[...], g_ref[...]) * (1.0 + m[4:5]) + m[3:4]
    logits = jnp.dot(hmod, wr_ref[...], precision=lax.Precision.HIGHEST, preferred_element_type=F32)
    lane = lax.broadcasted_iota(jnp.int32, logits.shape, 1)
    lanef = lane.astype(F32)
    logits = jnp.where(lane < N_EXPERTS, logits, -jnp.inf)
    v0 = jnp.max(logits, axis=1, keepdims=True)
    e0 = jnp.min(jnp.where(logits == v0, lanef, 1e9), axis=1, keepdims=True)
    rest = jnp.where(lanef == e0, -jnp.inf, logits)
    v1 = jnp.max(rest, axis=1, keepdims=True)
    e1 = jnp.min(jnp.where(rest == v1, lanef, 1e9), axis=1, keepdims=True)
    ex = jnp.exp(v1 - v0)
    g0 = 1.0 / (1.0 + ex)
    g1 = ex / (1.0 + ex)
    oh0 = lanef == e0
    oh1 = lanef == e1
    oh = jnp.where(jnp.logical_or(oh0, oh1), 1.0, 0.0)
    before = _dot(tri_ref[...], _bf(oh)) + carry_ref[0:1, :]
    r0 = jnp.sum(jnp.where(oh0, before, 0.0), axis=1, keepdims=True)
    r1 = jnp.sum(jnp.where(oh1, before, 0.0), axis=1, keepdims=True)
    carry_ref[0:1, :] = carry_ref[0:1, :] + jnp.sum(oh, axis=0, keepdims=True)
    meta = jnp.zeros(logits.shape, F32)
    for j, val in ((_META_E0, e0), (_META_E1, e1), (_META_R0, r0), (_META_R1, r1),
                   (_META_G0, g0), (_META_G1, g1)):
        meta = jnp.where(lane == j, val, meta)
    meta_ref[...] = meta
    cnt_ref[...] = carry_ref[...]


def _router(xs, n_rows, g2, modtab, w_router, lat_rows):
    d = xs.shape[1]
    nb = modtab.shape[0] - 1
    tm = ROW_TILE
    mi = _mod_index(tm, lat_rows, nb)
    wr = jnp.zeros((d, 128), F32).at[:, :N_EXPERTS].set(w_router)
    tri = jnp.asarray(np.tril(np.ones((tm, tm), np.float32), -1), BF16)
    return pl.pallas_call(
        _router_kernel,
        grid=(n_rows // tm,),
        in_specs=[pl.BlockSpec((tm, d), lambda i: (i, 0)),
                  pl.BlockSpec((1, d), lambda i: (0, 0)),
                  pl.BlockSpec((1, 6, d), lambda i: (mi(i), 0, 0)),
                  pl.BlockSpec((d, 128), lambda i: (0, 0)),
                  pl.BlockSpec((tm, tm), lambda i: (0, 0))],
        out_specs=[pl.BlockSpec((tm, 128), lambda i: (i, 0)),
                   pl.BlockSpec((8, 128), lambda i: (0, 0))],
        out_shape=[jax.ShapeDtypeStruct((n_rows, 128), F32),
                   jax.ShapeDtypeStruct((8, 128), F32)],
        scratch_shapes=[pltpu.VMEM((8, 128), F32)],
        compiler_params=_cparams(("arbitrary",), 32),
        name="router",
    )(xs, g2.reshape(1, d), modtab, wr, tri)


def _dispatch_kernel(dest_ref, x_ref, g_ref, mod_ref, init_ref, o_ref, h_s, sem):
    del init_ref
    i = pl.program_id(0)
    tm = h_s.shape[0]
    m = mod_ref[0]
    h_s[...] = _rms(x_ref[...], g_ref[...]) * (1.0 + m[4:5]) + m[3:4]

    def row_copy(r, k):
        dst = dest_ref[TOP_K * (i * tm + r) + k]
        return pltpu.make_async_copy(h_s.at[pl.ds(r, 1)], o_ref.at[pl.ds(dst, 1)], sem)

    def start(r, carry):
        for k in range(TOP_K):
            row_copy(r, k).start(priority=k)
        return carry

    lax.fori_loop(0, tm, start, 0, unroll=DMA_UNROLL)

    def wait(r, carry):
        for k in range(TOP_K):
            row_copy(r, k).wait()
        return carry

    lax.fori_loop(0, tm, wait, 0, unroll=DMA_UNROLL)


def _dispatch(dest, xs, n_rows, g2, modtab, n_slots, lat_rows):
    d = xs.shape[1]
    nb = modtab.shape[0] - 1
    tm = ROW_TILE
    mi = _mod_index(tm, lat_rows, nb)
    init = jnp.zeros((n_slots, d), F32)
    grid_spec = pltpu.PrefetchScalarGridSpec(
        num_scalar_prefetch=1,
        grid=(n_rows // tm,),
        in_specs=[pl.BlockSpec((tm, d), lambda i, dst: (i, 0)),
                  pl.BlockSpec((1, d), lambda i, dst: (0, 0)),
                  pl.BlockSpec((1, 6, d), lambda i, dst: (mi(i), 0, 0)),
                  pl.BlockSpec(memory_space=pl.ANY)],
        out_specs=pl.BlockSpec(memory_space=pl.ANY),
        scratch_shapes=[pltpu.VMEM((tm, d), F32), pltpu.SemaphoreType.DMA(())],
    )
    return pl.pallas_call(
        _dispatch_kernel,
        grid_spec=grid_spec,
        out_shape=jax.ShapeDtypeStruct((n_slots, d), F32),
        input_output_aliases={4: 0},
        compiler_params=_cparams(("arbitrary",), 32),
        name="dispatch",
    )(dest, xs, g2.reshape(1, d), modtab, init)


def _expert_kernel(ge_ref, nt_ref, ng_ref, x_ref, w1_ref, w3_ref, w2_ref, o_ref, xb_s):
    s = pl.program_id(0)
    f = pl.program_id(1)
    tm = MOE_TM

    @pl.when(s < ng_ref[0])
    def _():
        @pl.when(f == 0)
        def _():
            xb_s[...] = _bf(x_ref[...])
            o_ref[...] = jnp.zeros(o_ref.shape, F32)

        w1b, w3b, w2b = _bf(w1_ref[0]), _bf(w3_ref[0]), _bf(w2_ref[0])

        def tile(t):
            rows = pl.ds(t * tm, tm)
            h = xb_s[rows, :]
            u = _silu(_dot(h, w1b)) * _dot(h, w3b)
            o_ref[rows, :] += _dot(_bf(u), w2b)

        tile(0)
        for t in range(1, MOE_GROUP):
            pl.when(nt_ref[s] > t)(functools.partial(tile, t))

    @pl.when(jnp.logical_and(s >= ng_ref[0], f == pl.num_programs(1) - 1))
    def _():
        o_ref[...] = jnp.zeros(o_ref.shape, F32)


def _experts(group_e, group_tiles, n_groups, xsort, w1, w3, w2):
    n_slots, d = xsort.shape
    dff = w1.shape[2]
    tg, tf = MOE_TM * MOE_GROUP, MOE_TF
    nf = dff // tf

    def ss(s, ng):
        return jnp.maximum(jnp.minimum(s, ng[0] - 1), 0)

    def ff(s, f, ng):
        snake = jnp.where(s % 2 == 1, nf - 1 - f, f)
        last = jnp.where((ng[0] - 1) % 2 == 1, 0, nf - 1)
        return jnp.where(s < ng[0], snake, last)

    grid_spec = pltpu.PrefetchScalarGridSpec(
        num_scalar_prefetch=3,
        grid=(n_slots // tg, nf),
        in_specs=[pl.BlockSpec((tg, d), lambda s, f, ge, nt, ng: (ss(s, ng), 0), pipeline_mode=pl.Buffered(1)),
                  pl.BlockSpec((1, d, tf), lambda s, f, ge, nt, ng: (ge[ss(s, ng)], 0, ff(s, f, ng))),
                  pl.BlockSpec((1, d, tf), lambda s, f, ge, nt, ng: (ge[ss(s, ng)], 0, ff(s, f, ng))),
                  pl.BlockSpec((1, tf, d), lambda s, f, ge, nt, ng: (ge[ss(s, ng)], ff(s, f, ng), 0))],
        out_specs=pl.BlockSpec((tg, d), lambda s, f, ge, nt, ng: (s, 0)),
        scratch_shapes=[pltpu.VMEM((tg, d), BF16)],
    )
    return pl.pallas_call(
        _expert_kernel,
        grid_spec=grid_spec,
        out_shape=jax.ShapeDtypeStruct((n_slots, d), F32),
        compiler_params=_cparams(("arbitrary", "arbitrary"), 52),
        name="experts",
    )(group_e, group_tiles, n_groups, xsort, w1, w3, w2)


def _combine_kernel(dest_ref, y_ref, meta_ref, x_ref, g_ref, mod_ref, o_ref, buf_s, sem):
    i = pl.program_id(0)
    tm = x_ref.shape[0]

    def row_copy(r, k):
        src = dest_ref[TOP_K * (i * tm + r) + k]
        return pltpu.make_async_copy(y_ref.at[pl.ds(src, 1)], buf_s.at[k, pl.ds(r, 1)], sem)

    def start(r, carry):
        for k in range(TOP_K):
            row_copy(r, k).start(priority=k)
        return carry

    lax.fori_loop(0, tm, start, 0, unroll=DMA_UNROLL)

    def wait(r, carry):
        for k in range(TOP_K):
            row_copy(r, k).wait()
        return carry

    lax.fori_loop(0, tm, wait, 0, unroll=DMA_UNROLL)

    meta = meta_ref[...]
    lane = lax.broadcasted_iota(jnp.int32, meta.shape, 1)
    g0 = jnp.sum(jnp.where(lane == _META_G0, meta, 0.0), axis=1, keepdims=True)
    g1 = jnp.sum(jnp.where(lane == _META_G1, meta, 0.0), axis=1, keepdims=True)
    y = buf_s[0] * g0 + buf_s[1] * g1
    m = mod_ref[0]
    o_ref[...] = x_ref[...] + m[5:6] * _rms(y, g_ref[...])


def _combine(dest, yb, meta, xs, n_rows, g3, modtab, lat_rows):
    d = xs.shape[1]
    nb = modtab.shape[0] - 1
    tm = ROW_TILE
    mi = _mod_index(tm, lat_rows, nb)
    grid_spec = pltpu.PrefetchScalarGridSpec(
        num_scalar_prefetch=1,
        grid=(n_rows // tm,),
        in_specs=[pl.BlockSpec(memory_space=pl.ANY),
                  pl.BlockSpec((tm, 128), lambda i, dst: (i, 0)),
                  pl.BlockSpec((tm, d), lambda i, dst: (i, 0)),
                  pl.BlockSpec((1, d), lambda i, dst: (0, 0)),
                  pl.BlockSpec((1, 6, d), lambda i, dst: (mi(i), 0, 0))],
        out_specs=pl.BlockSpec((tm, d), lambda i, dst: (i, 0)),
        scratch_shapes=[pltpu.VMEM((TOP_K, tm, d), F32), pltpu.SemaphoreType.DMA(())],
    )
    return pl.pallas_call(
        _combine_kernel,
        grid_spec=grid_spec,
        out_shape=jax.ShapeDtypeStruct((n_rows, d), F32),
        compiler_params=_cparams(("arbitrary",), 40),
        name="combine",
    )(dest, yb, meta, xs, g3.reshape(1, d), modtab)


def _moe(xs, n_rows, g2, g3, modtab, w_router, w1, w3, w2, lat_rows):
    meta, cnt = _router(xs, n_rows, g2, modtab, w_router, lat_rows)
    counts = cnt[0, :N_EXPERTS].astype(jnp.int32)
    tg = MOE_TM * MOE_GROUP
    padded = (counts + tg - 1) // tg * tg
    pad_end = jnp.cumsum(padded)
    pad_start = pad_end - padded
    e = meta[:, _META_E0:_META_E1 + 1].astype(jnp.int32)
    r = meta[:, _META_R0:_META_R1 + 1].astype(jnp.int32)
    dest = (pad_start[e] + r).reshape(-1)
    n_groups_max = (n_rows * TOP_K) // tg + N_EXPERTS
    n_slots = n_groups_max * tg
    group_start = jnp.arange(n_groups_max, dtype=jnp.int32) * tg
    group_e = jnp.minimum(jnp.sum((group_start[:, None] >= pad_end[None, :]).astype(jnp.int32), axis=1),
                          N_EXPERTS - 1)
    filled = counts[group_e] - (group_start - pad_start[group_e])
    group_tiles = jnp.clip((filled + MOE_TM - 1) // MOE_TM, 1, MOE_GROUP).astype(jnp.int32)
    n_groups = (pad_end[-1:] // tg).astype(jnp.int32)
    xsort = _dispatch(dest, xs, n_rows, g2, modtab, n_slots, lat_rows)
    yb = _experts(group_e, group_tiles, n_groups, xsort, w1, w3, w2)
    return _combine(dest, yb, meta, xs, n_rows, g3, modtab, lat_rows)


def _rotary_tables(lat_len, ctx_len):
    rows = lat_len // GRID_W
    row = jnp.repeat(jnp.arange(rows, dtype=F32), GRID_W)
    col = jnp.tile(jnp.arange(GRID_W, dtype=F32), rows)
    n_freq = RET_DK // 4
    inv = ROPE_BASE ** (-jnp.arange(n_freq, dtype=F32) / n_freq)
    ang = jnp.concatenate([row[:, None] * inv, col[:, None] * inv], axis=-1)
    cos, sin = jnp.cos(ang), jnp.sin(ang)
    cosf = jnp.concatenate([jnp.ones((ctx_len, RET_DK), F32), jnp.concatenate([cos, cos], -1)], 0)
    sinf = jnp.concatenate([jnp.zeros((ctx_len, RET_DK), F32), jnp.concatenate([-sin, sin], -1)], 0)
    return cosf, sinf


def kernel(x, c, ctx, c_ctx, w_ada, b_ada, norm_g, w_in, w_out, ret_decay, ret_gn, hgrn_lb, hgrn_gn,
           mlstm_conv, mlstm_gate_b, mlstm_gn, w_ffn1, w_ffn3, w_ffn2, w_router, w_exp1, w_exp3, w_exp2):
    nb, lat_len, d = x.shape
    ctx_len = ctx.shape[1]
    depth = w_ada.shape[0]
    n_lat_rows = nb * lat_len
    n_rows = n_lat_rows + nb * ctx_len

    x_parts = [x.reshape(n_lat_rows, d), ctx.reshape(nb * ctx_len, d)]
    s_in = jnp.zeros((8, d), F32).at[:nb].set(c).at[nb].set(c_ctx)
    mod_all = _ada(s_in, w_ada, b_ada)[:, :nb + 1].reshape(depth, nb + 1, 6, d)

    cosf, sinf = _rotary_tables(lat_len, ctx_len)
    hg_cum, hg_masks, hg_signs = _hgrn_constants()
    hg_consts = (jnp.asarray(hg_cum, BF16), jnp.asarray(hg_masks, F32), jnp.asarray(hg_signs, F32))
    sm = jax.nn.softmax(hgrn_lb.astype(F32), axis=0)
    lb_all = jnp.clip(jnp.cumsum(sm, axis=0) - sm[0], 0.0, 1.0)

    hb = None
    for layer in range(depth):
        last = layer == depth - 1
        modtab = mod_all[layer]
        g = norm_g[layer]

        if hb is None:
            hb = _prenorm(x_parts, n_rows, g[0], modtab, 1, 0, lat_len, BF16)
        p = _mm(hb, w_in, layer, MM_TM, MM_TN, n_cols=PROJ_MAIN)
        w_g = jnp.zeros((1, d, 128), F32).at[0, :, :MLSTM_GATES].set(w_in[layer][:, PROJ_MAIN:])
        gates = _mm(hb, w_g, 0, MM_TM, 128)[:, :MLSTM_GATES]
        hb = None
        g_row_lat = jnp.swapaxes(gates[:n_lat_rows].reshape(nb, lat_len, MLSTM_GATES), 1, 2)
        g_row_ctx = jnp.swapaxes(gates[n_lat_rows:].reshape(nb, ctx_len, MLSTM_GATES), 1, 2)

        log_g = jax.nn.log_sigmoid(ret_decay[layer].astype(F32))
        lg_b = jnp.broadcast_to(log_g[:, :, None, None], (2, RET_HEADS, 8, RET_DV))
        lb = lb_all[layer]
        emit_ctx = not last
        o_ret = _retention(p, nb, lat_len, ctx_len, cosf, sinf, lg_b, ret_gn[layer], emit_ctx)
        o_hg = _hgrn(p, nb, lat_len, ctx_len, jnp.log(lb), jnp.log1p(-lb), hgrn_gn[layer],
                     hg_consts, emit_ctx)
        o_ml = _mlstm(p, gates, g_row_lat, g_row_ctx, nb, lat_len, ctx_len, mlstm_gate_b[layer],
                      mlstm_conv[layer], mlstm_gn[layer], emit_ctx)
        mix_parts = [list(o) for o in (o_ret, o_hg, o_ml)]
        rows_now = n_lat_rows if last else n_rows
        dense = layer % 2 == 0
        res = _wout(mix_parts, _bf(w_out[layer]), x_parts, rows_now, g[1], modtab, lat_len,
                    g[2] if dense else None)
        xs = res[0]
        x_parts = [xs]

        j = layer // 2
        if dense:
            nxt = (None, None) if last else (norm_g[layer + 1][0], mod_all[layer + 1])
            res = _ffn(res[1], _bf(w_ffn1[j]), _bf(w_ffn3[j]), _bf(w_ffn2[j]), xs, g[3], modtab, lat_len, *nxt)
            xs = res[0]
            hb = None if last else res[1]
        else:
            xs = _moe(xs, rows_now, g[2], g[3], modtab, w_router[j], w_exp1[j], w_exp3[j], w_exp2[j], lat_len)
        x_parts = [xs]
    return xs[:n_lat_rows].reshape(nb, lat_len, d)
```

```python
import functools

import numpy as np
import jax
import jax.numpy as jnp
from jax import lax
from jax.experimental import pallas as pl
from jax.experimental.pallas import tpu as pltpu

F32 = jnp.float32
BF16 = jnp.bfloat16

CHUNK = 128
NORM_EPS = 1e-6
ROPE_BASE = 10000.0
GRID_W = 64

RET_HEADS, RET_DK, RET_DV = 4, 128, 256
HGRN_HEADS, HGRN_DK, HGRN_DV = 4, 128, 128
MLSTM_HEADS, MLSTM_DK, MLSTM_DV = 4, 128, 128
N_EXPERTS = 8
TOP_K = 2

RET_QK = RET_HEADS * RET_DK
RET_WIDTH = RET_HEADS * RET_DV
HGRN_QK = HGRN_HEADS * HGRN_DK
HGRN_WIDTH = HGRN_HEADS * HGRN_DV
MLSTM_QK = MLSTM_HEADS * MLSTM_DK
MLSTM_WIDTH = MLSTM_HEADS * MLSTM_DV
MLSTM_GATES = 4 * MLSTM_HEADS

_C_RET_Q = 0
_C_RET_K = _C_RET_Q + RET_QK // 128
_C_RET_V = _C_RET_K + RET_QK // 128
_C_RET_G = _C_RET_V + RET_WIDTH // 128
_C_HG_Q = _C_RET_G + RET_WIDTH // 128
_C_HG_FF = _C_HG_Q + HGRN_QK // 128
_C_HG_FB = _C_HG_FF + HGRN_QK // 128
_C_HG_I = _C_HG_FB + HGRN_QK // 128
_C_HG_G = _C_HG_I + HGRN_WIDTH // 128
_C_ML_Q = _C_HG_G + HGRN_WIDTH // 128
_C_ML_K = _C_ML_Q + MLSTM_QK // 128
_C_ML_V = _C_ML_K + MLSTM_QK // 128
_C_ML_O = _C_ML_V + MLSTM_WIDTH // 128
PROJ_MAIN = (_C_ML_O + MLSTM_WIDTH // 128) * 128

_VMEM_CAP_BYTES = 56 * 1024 * 1024

ROW_TILE = 512
MM_TM, MM_TN = 1024, 768
FFN_TF = 512
MOE_TM = 512
MOE_GROUP = 1
MOE_TF = 512
RET_UNROLL = (4, 4)
HGRN_UNROLL = (2, 4)
MLSTM_UNROLL = (2, 2)
DMA_UNROLL = 8


def _cparams(sem, vmem_mb):
    return pltpu.CompilerParams(dimension_semantics=sem,
                                vmem_limit_bytes=min(int(vmem_mb * 1024 * 1024), _VMEM_CAP_BYTES))


def _bf(x):
    return x.astype(BF16)


def _dot(a, b):
    return jnp.dot(a, b, preferred_element_type=F32)


def _dot_nt(a, b):
    return lax.dot_general(a, b, (((1,), (1,)), ((), ())), preferred_element_type=F32)


def _dot_tn(a, b):
    return lax.dot_general(a, b, (((0,), (0,)), ((), ())), preferred_element_type=F32)


def _sigmoid(x):
    return 1.0 / (1.0 + jnp.exp(-x))


def _silu(x):
    return x * _sigmoid(x)


def _log_sigmoid(x):
    return jnp.minimum(x, 0.0) - jnp.log(1.0 + jnp.exp(-jnp.abs(x)))


def _rms(x, g):
    return x * lax.rsqrt(jnp.mean(x * x, axis=-1, keepdims=True) + NORM_EPS) * g


def _mod_index(tile_rows, n_lat_rows_per_batch, n_batch):
    return lambda i: jnp.minimum((i * tile_rows) // n_lat_rows_per_batch, n_batch)


def _ada_kernel(s_ref, w_ref, b_ref, o_ref):
    s = _bf(_silu(s_ref[...]))
    o_ref[0] = _dot(s, _bf(w_ref[0])) + b_ref[0]


def _ada(s_in, w_ada, b_ada):
    depth, d, n = w_ada.shape
    tn = 1024
    return pl.pallas_call(
        _ada_kernel,
        grid=(depth, n // tn),
        in_specs=[pl.BlockSpec((8, d), lambda l, j: (0, 0)),
                  pl.BlockSpec((1, d, tn), lambda l, j: (l, 0, j)),
                  pl.BlockSpec((1, 1, tn), lambda l, j: (l, 0, j))],
        out_specs=pl.BlockSpec((1, 8, tn), lambda l, j: (l, 0, j)),
        out_shape=jax.ShapeDtypeStruct((depth, 8, n), F32),
        compiler_params=_cparams(("arbitrary", "arbitrary"), 40),
        name="ada",
    )(s_in, w_ada, b_ada.reshape(depth, 1, n))


def _part_specs(parts, tm, width):
    if len(parts) == 1:
        return [pl.BlockSpec((tm, width), lambda i: (i, 0))]
    n0 = parts[0].shape[0] // tm
    return [pl.BlockSpec((tm, width), lambda i: (jnp.minimum(i, n0 - 1), 0)),
            pl.BlockSpec((tm, width), lambda i: (jnp.maximum(i - n0, 0), 0))]


def _on_part(n_first, n_parts, fn):
    if n_parts == 1:
        fn(0)
    else:
        i = pl.program_id(0)
        pl.when(i < n_first)(lambda: fn(0))
        pl.when(i >= n_first)(lambda: fn(1))


def _prenorm_kernel(*refs, sc, sh, n_first):
    x_refs, (g_ref, mod_ref, o_ref) = refs[:-3], refs[-3:]

    def run(part):
        m = mod_ref[0]
        y = _rms(x_refs[part][...], g_ref[...])
        o_ref[...] = (y * (1.0 + m[sc:sc + 1]) + m[sh:sh + 1]).astype(o_ref.dtype)

    _on_part(n_first, len(x_refs), run)


def _prenorm(x_parts, n_rows, g, modtab, sc, sh, lat_rows, out_dtype):
    d = x_parts[0].shape[1]
    nb = modtab.shape[0] - 1
    mi = _mod_index(ROW_TILE, lat_rows, nb)
    return pl.pallas_call(
        functools.partial(_prenorm_kernel, sc=sc, sh=sh, n_first=x_parts[0].shape[0] // ROW_TILE),
        grid=(n_rows // ROW_TILE,),
        in_specs=[*_part_specs(x_parts, ROW_TILE, d),
                  pl.BlockSpec((1, d), lambda i: (0, 0)),
                  pl.BlockSpec((1, 6, d), lambda i: (mi(i), 0, 0))],
        out_specs=pl.BlockSpec((ROW_TILE, d), lambda i: (i, 0)),
        out_shape=jax.ShapeDtypeStruct((n_rows, d), out_dtype),
        compiler_params=_cparams(("arbitrary",), 32),
        name="prenorm",
    )(*x_parts, g.reshape(1, d), modtab)


def _mm_kernel(x_ref, wt_ref, o_ref, wb_s):
    @pl.when(pl.program_id(1) == 0)
    def _():
        wb_s[...] = _bf(wt_ref[0])

    o_ref[...] = _dot_nt(x_ref[...], wb_s[...]).astype(o_ref.dtype)


def _mm(x, wt, layer, tm, tn, n_cols=None, out_dtype=F32):
    m, k = x.shape
    n = wt.shape[1] if n_cols is None else n_cols
    return pl.pallas_call(
        _mm_kernel,
        grid=(n // tn, m // tm),
        in_specs=[pl.BlockSpec((tm, k), lambda j, i: (i, 0)),
                  pl.BlockSpec((1, tn, k), lambda j, i: (layer, j, 0))],
        out_specs=pl.BlockSpec((tm, tn), lambda j, i: (i, j)),
        out_shape=jax.ShapeDtypeStruct((m, n), out_dtype),
        scratch_shapes=[pltpu.VMEM((tn, k), BF16)],
        compiler_params=_cparams(("arbitrary", "arbitrary"), 48),
        name="proj",
    )(x, wt)


def _bwd_chunk(i, n_ctx, n_lat):
    return jnp.where(i < n_ctx, n_ctx - 1 - i, 2 * n_ctx + n_lat - 1 - i)


def _rows(c):
    return pl.ds(pl.multiple_of(c * CHUNK, CHUNK), CHUNK)


def _for_chunks(n_ctx, n_lat, fn, unroll, with_ctx=True):
    def run(part, n, off):
        def body(c, carry):
            fn(part, c, c + off)
            return carry
        lax.fori_loop(0, n, body, 0, unroll=max(u for u in (1, 2, unroll) if n % u == 0 and u <= unroll))
    if with_ctx:
        run(0, n_ctx, 0)
    run(1, n_lat, n_ctx)


def _lanes(d, w):
    return slice(d * w, (d + 1) * w)


def _mixer_call(kernel_fn, name, in_specs, args, nb, heads, lat_len, ctx_len, dv, scratch, emit_ctx, vmem_mb):
    width = heads * dv
    out_specs = [pl.BlockSpec((lat_len, dv), lambda b, h: (b, h))]
    out_shape = [jax.ShapeDtypeStruct((nb * lat_len, width), BF16)]
    if emit_ctx:
        out_specs.append(pl.BlockSpec((ctx_len, dv), lambda b, h: (b, h)))
        out_shape.append(jax.ShapeDtypeStruct((nb * ctx_len, width), BF16))
    n_in = len(in_specs)

    def body(*refs):
        ins, rest = refs[:n_in], refs[n_in:]
        if emit_ctx:
            ol, oc, scr = rest[0], rest[1], rest[2:]
        else:
            ol, oc, scr = rest[0], None, rest[1:]
        kernel_fn(*ins, ol, oc, *scr, n_ctx=ctx_len // CHUNK, n_lat=lat_len // CHUNK)

    return pl.pallas_call(
        body,
        grid=(nb, heads),
        in_specs=in_specs,
        out_specs=out_specs,
        out_shape=out_shape,
        scratch_shapes=scratch,
        compiler_params=_cparams(("arbitrary", "arbitrary"), vmem_mb),
        name=name,
    )(*args)


def _seq_specs(col0, width, nb, lat_len, ctx_len):
    n_lat_blk = nb * lat_len // ctx_len
    c0 = col0 * 128 // width
    return (pl.BlockSpec((lat_len, width), lambda b, h: (b, c0 + h)),
            pl.BlockSpec((ctx_len, width), lambda b, h: (n_lat_blk + b, c0 + h)))


def _ret_kernel(ql, qc, kl, kc, vl, vc, gl, gc, cos_ref, sin_ref, lg_ref, gn_ref, ol_ref, oc_ref,
                qs_s, kv_s, sall_s, o_s, st_s, dm_s, dq_s, dk_s, *, n_ctx, n_lat):
    nch = n_ctx + n_lat
    unroll_a, unroll_c = RET_UNROLL
    qr, kr, vr, gr, outr = (qc, ql), (kc, kl), (vc, vl), (gc, gl), (oc_ref, ol_ref)
    t_i = lax.broadcasted_iota(jnp.int32, (CHUNK, CHUNK), 0).astype(F32)
    s_i = lax.broadcasted_iota(jnp.int32, (CHUNK, CHUNK), 1).astype(F32)
    dm = None
    for d in range(2):
        lg = lg_ref[d, 0][0:1, 0:CHUNK]
        rel = (t_i - s_i) if d == 0 else (s_i - t_i)
        dmd = jnp.where(rel >= 0, jnp.exp(jnp.maximum(rel, 0.0) * lg), 0.0)
        dm = dmd if dm is None else dm + dmd
        p = t_i if d == 0 else (CHUNK - 1.0) - t_i
        dq_s[d] = jnp.exp((p + 1.0) * lg)
        dk_s[d] = jnp.exp((CHUNK - 1.0 - p) * lg)
    dm_s[...] = dm
    scale = RET_DK ** -0.5

    def phase_a(part, cl, cg):
        rl, rg = _rows(cl), _rows(cg)
        cs = cos_ref[rg, :]
        sn = sin_ref[rg, :]
        q = qr[part][rl, :]
        k = kr[part][rl, :]
        q = q * cs + pltpu.roll(q, RET_DK // 2, 1) * sn
        k = (k * cs + pltpu.roll(k, RET_DK // 2, 1) * sn) * scale
        vb = _bf(vr[part][rl, :])
        s = _dot_nt(_bf(q), _bf(k))
        o_s[rg, :] = _dot(_bf(s * dm_s[...]), vb)
        qs_s[rg, :] = jnp.concatenate([_bf(q * dq_s[0]), _bf(q * dq_s[1])], axis=1)
        kd = jnp.concatenate([_bf(k * dk_s[0]), _bf(k * dk_s[1])], axis=1)
        kv_s[cg] = _dot_tn(kd, vb)

    _for_chunks(n_ctx, n_lat, phase_a, unroll_a)

    st_s[...] = jnp.zeros(st_s.shape, F32)
    dec = [jnp.exp(float(CHUNK) * lg_ref[d, 0][0:1, :]) for d in range(2)]

    def phase_b(i, carry):
        for d, c in ((0, i), (1, _bwd_chunk(i, n_ctx, n_lat))):
            st = st_s[d]
            sall_s[c, _lanes(d, RET_DK), :] = _bf(st)
            st_s[d] = dec[d] * st + kv_s[c, _lanes(d, RET_DK), :]
        return carry

    lax.fori_loop(0, nch, phase_b, 0)
    gn = gn_ref[...]

    def phase_c(part, cl, cg):
        rl, rg = _rows(cl), _rows(cg)
        o = o_s[rg, :] + _dot(qs_s[rg, :], sall_s[cg])
        outr[part][rl, :] = (_rms(o, gn) * _silu(gr[part][rl, :])).astype(BF16)

    _for_chunks(n_ctx, n_lat, phase_c, unroll_c, with_ctx=oc_ref is not None)


def _retention(p, nb, lat_len, ctx_len, cosf, sinf, lg_b, gn, emit_ctx):
    t_len = lat_len + ctx_len
    nch = t_len // CHUNK
    s128 = functools.partial(_seq_specs, width=128, nb=nb, lat_len=lat_len, ctx_len=ctx_len)
    s256 = functools.partial(_seq_specs, width=256, nb=nb, lat_len=lat_len, ctx_len=ctx_len)
    in_specs = [*s128(_C_RET_Q), *s128(_C_RET_K), *s256(_C_RET_V), *s256(_C_RET_G),
                pl.BlockSpec((t_len, 128), lambda b, h: (0, 0)),
                pl.BlockSpec((t_len, 128), lambda b, h: (0, 0)),
                pl.BlockSpec((2, 1, 8, RET_DV), lambda b, h: (0, h, 0, 0)),
                pl.BlockSpec((1, RET_DV), lambda b, h: (0, h))]
    scratch = [pltpu.VMEM((t_len, 2 * RET_DK), BF16),
               pltpu.VMEM((nch, 2 * RET_DK, RET_DV), F32),
               pltpu.VMEM((nch, 2 * RET_DK, RET_DV), BF16),
               pltpu.VMEM((t_len, RET_DV), F32),
               pltpu.VMEM((2, RET_DK, RET_DV), F32),
               pltpu.VMEM((CHUNK, CHUNK), F32),
               pltpu.VMEM((2, CHUNK, CHUNK), F32), pltpu.VMEM((2, CHUNK, CHUNK), F32)]
    args = (p, p, p, p, p, p, p, p, cosf, sinf, lg_b, gn.reshape(1, RET_WIDTH))
    return _mixer_call(_ret_kernel, "retention", in_specs, args, nb, RET_HEADS, lat_len, ctx_len, RET_DV,
                       scratch, emit_ctx, 48)


_HG_LEVELS = (64, 32, 16, 8, 4, 2, 1)


def _hgrn_constants():
    c = CHUNK
    t = np.arange(c)[:, None]
    u = np.arange(c)[None, :]
    cum = (u <= t).astype(np.float32)
    masks, signs = [], []
    for m in _HG_LEVELS:
        base = (t // (2 * m)) * (2 * m)
        lower = t >= base + m
        tb = t // (2 * m)
        sb = u // (2 * m)
        masks.append((tb == sb) & lower & (u < (sb * 2 * m + m)))
        signs.append(np.broadcast_to(np.where(lower, 1.0, -1.0), (c, c)))
    masks.append(t == u)
    kf = np.stack([x.astype(np.float32) for x in masks], axis=0)
    kb = np.stack([x.astype(np.float32)[::-1, ::-1] for x in masks], axis=0)
    sf = np.stack([x.astype(np.float32) for x in signs], axis=0)
    sb_ = np.stack([x.astype(np.float32)[::-1, ::-1] for x in signs], axis=0)
    return (np.stack([cum, cum[::-1, ::-1]], 0), np.stack([kf, kb], 0),
            np.stack([sf[:_HG_WIDE], sb_[:_HG_WIDE]], 0))


_HG_WIDE = 5


def _hgrn_level_exponents(b, lf, d, sgn_ref, row):
    out = []
    for l, m in enumerate(_HG_LEVELS[:_HG_WIDE]):
        pieces = []
        for j in range(CHUNK // (2 * m)):
            r = 2 * m * j + (m - 1 if d == 0 else m)
            pieces.append(jnp.broadcast_to(b[r:r + 1, :], (2 * m, CHUNK)))
        bref = pieces[0] if len(pieces) == 1 else jnp.concatenate(pieces, axis=0)
        out.append((b - bref) * sgn_ref[d, l])
    up = pltpu.roll(lf, CHUNK - 1, 0)
    dn = pltpu.roll(lf, 1, 0)
    r4 = row % 4
    if d == 0:
        e2 = jnp.where(r4 == 0, up, jnp.where(r4 == 1, 0.0, jnp.where(r4 == 2, lf, lf + dn)))
        e1 = jnp.where(row % 2 == 1, lf, 0.0)
    else:
        e2 = jnp.where(r4 == 0, lf + up, jnp.where(r4 == 1, lf, jnp.where(r4 == 2, 0.0, dn)))
        e1 = jnp.where(row % 2 == 0, lf, 0.0)
    return out + [e2, e1]


def _hgrn_kernel(ql, qc, ffl, ffc, fbl, fbc, il, ic, gl, gc, llb_ref, lub_ref, gn_ref, c_ref, k_ref, sgn_ref,
                 ol_ref, oc_ref, qs_s, kv_s, sall_s, dec_s, o_s, st_s, *, n_ctx, n_lat):
    nch = n_ctx + n_lat
    unroll_a, unroll_c = HGRN_UNROLL
    qr, fr, ir, gr, outr =(qc, ql), ((ffc, ffl), (fbc, fbl)), (ic, il), (gc, gl), (oc_ref, ol_ref)
    llb = llb_ref[...]
    lub = lub_ref[...]
    nlev = len(_HG_LEVELS)
    row = lax.broadcasted_iota(jnp.int32, (CHUNK, HGRN_DK), 0)

    def phase_a(part, cl, cg):
        rl, rg = _rows(cl), _rows(cg)
        q = _silu(qr[part][rl, :])
        qb = _bf(q)
        vb = _bf(ir[part][rl, :])
        a_sum, qs, kds = None, [], []
        for d in range(2):
            lsg = lub + _log_sigmoid(fr[d][part][rl, :])
            lf = jnp.maximum(llb, lsg) + jnp.log(1.0 + jnp.exp(-jnp.abs(llb - lsg)))
            k = 1.0 - jnp.exp(lf)
            lf_hi = _bf(lf)
            lf_lo = _bf(lf - lf_hi.astype(F32))
            b2 = _dot(c_ref[d], jnp.concatenate([lf_hi, lf_lo], axis=1))
            b = b2[:, 0:HGRN_DK] + b2[:, HGRN_DK:]
            es = _hgrn_level_exponents(b, lf, d, sgn_ref, row)
            kb = _bf(k)
            a = k_ref[d, nlev] * _dot_nt(qb, kb)
            for l in range(nlev):
                xb = _bf(jnp.exp(es[l]))
                a = a + k_ref[d, l] * _dot_nt(qb * xb, kb * xb)
            a_sum = a if a_sum is None else a_sum + a
            b_last = b[CHUNK - 1:CHUNK] if d == 0 else b[0:1]
            qs.append(_bf(q * jnp.exp(b)))
            kds.append(_bf(k * jnp.exp(b_last - b)))
            dec_s[cg, :, _lanes(d, HGRN_DK)] = jnp.exp(b_last)
        o_s[rg, :] = _dot(_bf(a_sum), vb)
        qs_s[rg, :] = jnp.concatenate(qs, axis=1)
        kv_s[cg] = _dot_tn(vb, jnp.concatenate(kds, axis=1))

    _for_chunks(n_ctx, n_lat, phase_a, unroll_a)
    st_s[...] = jnp.zeros(st_s.shape, F32)

    def phase_b(i, carry):
        for d, c in ((0, i), (1, _bwd_chunk(i, n_ctx, n_lat))):
            ln = _lanes(d, HGRN_DK)
            st = st_s[d]
            sall_s[c, :, ln] = _bf(st)
            st_s[d] = dec_s[c, :, ln] * st + kv_s[c, :, ln]
        return carry

    lax.fori_loop(0, nch, phase_b, 0)
    gn = gn_ref[...]

    def phase_c(part, cl, cg):
        rl, rg = _rows(cl), _rows(cg)
        o = o_s[rg, :] + _dot_nt(qs_s[rg, :], sall_s[cg])
        outr[part][rl, :] = (_rms(o, gn) * _silu(gr[part][rl, :])).astype(BF16)

    _for_chunks(n_ctx, n_lat, phase_c, unroll_c, with_ctx=oc_ref is not None)


def _hgrn(p, nb, lat_len, ctx_len, log_lb, log_ub, gn, consts, emit_ctx):
    mats, masks, signs = consts
    t_len = lat_len + ctx_len
    nch = t_len // CHUNK
    spec = functools.partial(_seq_specs, width=128, nb=nb, lat_len=lat_len, ctx_len=ctx_len)
    vec = pl.BlockSpec((1, 128), lambda b, h: (0, h))
    in_specs = [*spec(_C_HG_Q), *spec(_C_HG_FF), *spec(_C_HG_FB), *spec(_C_HG_I), *spec(_C_HG_G),
                vec, vec, vec,
                pl.BlockSpec(mats.shape, lambda b, h: (0, 0, 0)),
                pl.BlockSpec(masks.shape, lambda b, h: (0, 0, 0, 0)),
                pl.BlockSpec(signs.shape, lambda b, h: (0, 0, 0, 0))]
    scratch = [pltpu.VMEM((t_len, 2 * HGRN_DK), BF16),
               pltpu.VMEM((nch, HGRN_DV, 2 * HGRN_DK), F32),
               pltpu.VMEM((nch, HGRN_DV, 2 * HGRN_DK), BF16),
               pltpu.VMEM((nch, 1, 2 * HGRN_DK), F32),
               pltpu.VMEM((t_len, HGRN_DV), F32),
               pltpu.VMEM((2, HGRN_DV, HGRN_DK), F32)]
    args = (p, p, p, p, p, p, p, p, p, p, log_lb.reshape(1, HGRN_QK), log_ub.reshape(1, HGRN_QK),
            gn.reshape(1, HGRN_WIDTH), mats, masks, signs)
    return _mixer_call(_hgrn_kernel, "hgrn2", in_specs, args, nb, HGRN_HEADS, lat_len, ctx_len, HGRN_DV,
                       scratch, emit_ctx, 48)


def _mlstm_kernel(ql, qc, kl, kc, vl, vc, ogl, ogc, gcl, gcc, grl, grc, bc_ref, br_ref,
                  wq_ref, wk_ref, gn_ref, ol_ref, oc_ref,
                  q_s, gr_s, intra_s, kv_s, call_s, rho_s, bcol_s, bl_s, mu_s, mprev_s, st_s, m_s,
                  *, n_ctx, n_lat):
    nch = n_ctx + n_lat
    unroll_a, unroll_c = MLSTM_UNROLL
    h = pl.program_id(1)
    nh = MLSTM_HEADS
    dk, dv = MLSTM_DK, MLSTM_DV
    ext = 2 * dv
    qr, kr, vr, ogr, gcr, outr = (qc, ql), (kc, kl), (vc, vl), (ogc, ogl), (gcc, gcl), (oc_ref, ol_ref)
    n_loc = (n_ctx, n_lat)

    for cc in range(nch):
        src, c0 = (grc, cc) if cc < n_ctx else (grl, cc - n_ctx)
        gr_s[cc] = src[0, :, c0 * CHUNK:(c0 + 1) * CHUNK] + br_ref[...]

    row = lax.broadcasted_iota(jnp.int32, (CHUNK, dk), 0)
    lane = lax.broadcasted_iota(jnp.int32, (CHUNK, dv), 1)
    del lane
    ones_col = jnp.ones((CHUNK, dv), BF16)
    scale = dk ** -0.5
    t_i = lax.broadcasted_iota(jnp.int32, (CHUNK, CHUNK), 0)
    s_i = lax.broadcasted_iota(jnp.int32, (CHUNK, CHUNK), 1)
    lane16 = lax.broadcasted_iota(jnp.int32, (CHUNK, MLSTM_GATES), 1)
    sub16 = lax.broadcasted_iota(jnp.int32, (MLSTM_GATES, CHUNK), 0)

    def conv(src, w_ref, part, cl):
        r0 = pl.multiple_of(cl * CHUNK, CHUNK)
        n_rows = n_loc[part] * CHUNK
        x = src[pl.ds(r0, CHUNK), :]
        pr = src[pl.ds(jnp.maximum(r0 - 1, 0), 1), :]
        nx = src[pl.ds(jnp.minimum(r0 + CHUNK, n_rows - 1), 1), :]
        pr = jnp.where(cl != 0, pr, 0.0)
        nx = jnp.where(cl != n_loc[part] - 1, nx, 0.0)
        xp = jnp.where(row == 0, pr, pltpu.roll(x, 1, 0))
        xn = jnp.where(row == CHUNK - 1, nx, pltpu.roll(x, CHUNK - 1, 0))
        w = w_ref[...]
        return _silu(w[0:1] * xp + w[1:2] * x + w[2:3] * xn)

    def pick_col(g, j):
        return jnp.sum(jnp.where(lane16 == j, g, 0.0), axis=1, keepdims=True)

    def pick_row(g, j):
        return jnp.sum(jnp.where(sub16 == j, g, 0.0), axis=0, keepdims=True)

    def phase_a(part, cl, cg):
        rl, rg = _rows(cl), _rows(cg)
        q = conv(qr[part], wq_ref, part, cl)
        k = conv(kr[part], wk_ref, part, cl) * scale
        qb = _bf(q)
        q_s[rg, :] = qb
        vb = jnp.concatenate([_bf(vr[part][rl, :]), ones_col], axis=1)
        s = _dot_nt(qb, _bf(k))
        g_c = gcr[part][rl, :] + bc_ref[...]
        g_r = gr_s[cg]
        kws = []
        for d in range(2):
            ig_c = pick_col(g_c, d * nh + h)
            lf_c = _log_sigmoid(pick_col(g_c, 2 * nh + d * nh + h))
            ig_r = pick_row(g_r, d * nh + h)
            lf_r = _log_sigmoid(pick_row(g_r, 2 * nh + d * nh + h))
            tri = (s_i <= t_i) if d == 0 else (s_i >= t_i)
            tri_t = (t_i <= s_i) if d == 0 else (t_i >= s_i)
            b_c = jnp.sum(jnp.where(tri, lf_r, 0.0), axis=1, keepdims=True)
            b_r = jnp.sum(jnp.where(tri_t, lf_c, 0.0), axis=0, keepdims=True)
            dmat = jnp.where(tri, b_c - b_r + ig_r, -jnp.inf)
            rho = jnp.max(dmat, axis=1, keepdims=True)
            intra_s[d, rg, :] = _dot(_bf(s * jnp.exp(dmat - rho)), vb)
            b_last = b_c[CHUNK - 1:CHUNK] if d == 0 else b_c[0:1]
            mu = jnp.max(b_last - b_r + ig_r, axis=1, keepdims=True)
            kws.append(_bf(k * jnp.exp(b_last - b_c + ig_c - mu)))
            rho_s[d, rg, :] = jnp.broadcast_to(rho, (CHUNK, 128))
            bcol_s[d, rg, :] = jnp.broadcast_to(b_c, (CHUNK, 128))
            bl_s[cg, :, _lanes(d, 128)] = jnp.broadcast_to(b_last, (1, 128))
            mu_s[cg, :, _lanes(d, 128)] = jnp.broadcast_to(mu, (1, 128))
        kv_s[cg] = _dot_tn(jnp.concatenate(kws, axis=1), vb)

    _for_chunks(n_ctx, n_lat, phase_a, unroll_a)
    st_s[...] = jnp.zeros(st_s.shape, F32)
    m_s[...] = jnp.zeros(m_s.shape, F32)

    def wide(v):
        return jnp.concatenate([v, v], axis=1)

    def phase_b(i, carry):
        for d, c in ((0, i), (1, _bwd_chunk(i, n_ctx, n_lat))):
            ln = _lanes(d, 128)
            m_prev = m_s[d]
            st = st_s[d]
            mprev_s[c, :, ln] = m_prev
            call_s[c, :, _lanes(d, ext)] = _bf(st)
            bl = bl_s[c, :, ln]
            mu = mu_s[c, :, ln]
            m_new = jnp.maximum(bl + m_prev, mu)
            st_s[d] = (wide(jnp.exp(bl + m_prev - m_new)) * st
                       + wide(jnp.exp(mu - m_new)) * kv_s[c, _lanes(d, dk), :])
            m_s[d] = m_new
        return carry

    lax.fori_loop(0, nch, phase_b, 0)
    gn = gn_ref[...]

    def phase_c(part, cl, cg):
        rl, rg = _rows(cl), _rows(cg)
        qc_all = _dot(q_s[rg, :], call_s[cg])
        o = None
        for d in range(2):
            rho = rho_s[d, rg, :]
            b_c = bcol_s[d, rg, :]
            m_prev = mprev_s[cg, :, _lanes(d, 128)]
            m_t = jnp.maximum(rho, b_c + m_prev)
            nd = (wide(jnp.exp(rho - m_t)) * intra_s[d, rg, :]
                  + wide(jnp.exp(b_c + m_prev - m_t)) * qc_all[:, _lanes(d, ext)])
            hh = nd[:, 0:dv] / jnp.maximum(jnp.abs(nd[:, dv:]), jnp.exp(-m_t))
            o = hh if o is None else o + hh
        y = o - jnp.mean(o, axis=-1, keepdims=True)
        y = y * lax.rsqrt(jnp.mean(y * y, axis=-1, keepdims=True) + NORM_EPS)
        outr[part][rl, :] = (y * gn * _sigmoid(ogr[part][rl, :])).astype(BF16)

    _for_chunks(n_ctx, n_lat, phase_c, unroll_c, with_ctx=oc_ref is not None)


def _mlstm(p, g_col, g_row_lat, g_row_ctx, nb, lat_len, ctx_len, gate_b, conv_w, gn, emit_ctx):
    t_len = lat_len + ctx_len
    nch = t_len // CHUNK
    n_lat_blk = nb * lat_len // ctx_len
    ng = MLSTM_GATES
    dk, dv = MLSTM_DK, MLSTM_DV
    spec = functools.partial(_seq_specs, width=128, nb=nb, lat_len=lat_len, ctx_len=ctx_len)
    in_specs = [*spec(_C_ML_Q), *spec(_C_ML_K), *spec(_C_ML_V), *spec(_C_ML_O),
                pl.BlockSpec((lat_len, ng), lambda b, h: (b, 0)),
                pl.BlockSpec((ctx_len, ng), lambda b, h: (n_lat_blk + b, 0)),
                pl.BlockSpec((1, ng, lat_len), lambda b, h: (b, 0, 0)),
                pl.BlockSpec((1, ng, ctx_len), lambda b, h: (b, 0, 0)),
                pl.BlockSpec((1, ng), lambda b, h: (0, 0)),
                pl.BlockSpec((ng, 1), lambda b, h: (0, 0)),
                pl.BlockSpec((3, 128), lambda b, h: (0, h)),
                pl.BlockSpec((3, 128), lambda b, h: (0, MLSTM_HEADS + h)),
                pl.BlockSpec((1, 128), lambda b, h: (0, h))]
    scratch = [pltpu.VMEM((t_len, dk), BF16),
               pltpu.VMEM((nch, ng, CHUNK), F32),
               pltpu.VMEM((2, t_len, 2 * dv), F32),
               pltpu.VMEM((nch, 2 * dk, 2 * dv), F32),
               pltpu.VMEM((nch, dk, 4 * dv), BF16),
               pltpu.VMEM((2, t_len, 128), F32), pltpu.VMEM((2, t_len, 128), F32),
               pltpu.VMEM((nch, 1, 256), F32), pltpu.VMEM((nch, 1, 256), F32), pltpu.VMEM((nch, 1, 256), F32),
               pltpu.VMEM((2, dk, 2 * dv), F32), pltpu.VMEM((2, 1, 128), F32)]
    args = (p, p, p, p, p, p, p, p, g_col, g_col, g_row_lat, g_row_ctx,
            gate_b.reshape(1, ng), gate_b.reshape(ng, 1), conv_w, conv_w, gn.reshape(1, MLSTM_WIDTH))
    return _mixer_call(_mlstm_kernel, "mlstm", in_specs, args, nb, MLSTM_HEADS, lat_len, ctx_len, dv,
                       scratch, emit_ctx, 48)


def _wout_kernel(*refs, n_lat_tiles, n_mix_parts, n_x_parts, emit_next):
    it = iter(refs)
    mix = [[next(it) for _ in range(n_mix_parts)] for _ in range(3)]
    x_parts = [next(it) for _ in range(n_x_parts)]
    wr_ref, wh_ref, wm_ref, g_ref, mod_ref = (next(it) for _ in range(5))
    g2_ref = next(it) if emit_next else None
    o_ref = next(it)
    h2_ref = next(it) if emit_next else None

    def run(part):
        pm = min(part, n_mix_parts - 1)
        y = (_dot(mix[0][pm][...], wr_ref[...]) + _dot(mix[1][pm][...], wh_ref[...])
             + _dot(mix[2][pm][...], wm_ref[...]))
        m = mod_ref[0]
        xn = x_parts[min(part, n_x_parts - 1)][...] + m[2:3] * _rms(y, g_ref[...])
        o_ref[...] = xn
        if emit_next:
            h2_ref[...] = (_rms(xn, g2_ref[...]) * (1.0 + m[4:5]) + m[3:4]).astype(h2_ref.dtype)

    _on_part(n_lat_tiles, max(n_mix_parts, n_x_parts), run)


def _wout(mix_parts, w_out, x_parts, n_rows, g1, modtab, lat_rows, g2_next):
    d = x_parts[0].shape[1]
    nb = modtab.shape[0] - 1
    mi = _mod_index(ROW_TILE, lat_rows, nb)
    tm = ROW_TILE
    n_lat_tiles = mix_parts[0][0].shape[0] // tm
    emit_next = g2_next is not None
    widths = (RET_WIDTH, HGRN_WIDTH, MLSTM_WIDTH)
    w_r = w_out[0:RET_WIDTH]
    w_h = w_out[RET_WIDTH:RET_WIDTH + HGRN_WIDTH]
    w_m = w_out[RET_WIDTH + HGRN_WIDTH:]
    full = lambda a: pl.BlockSpec(a.shape, lambda i: (0, 0))
    in_specs, args = [], []
    for parts, w in zip(mix_parts, widths):
        in_specs += _part_specs(parts, tm, w)
        args += list(parts)
    in_specs += _part_specs(x_parts, tm, d)
    args += list(x_parts)
    in_specs += [full(w_r), full(w_h), full(w_m),
                 pl.BlockSpec((1, d), lambda i: (0, 0)),
                 pl.BlockSpec((1, 6, d), lambda i: (mi(i), 0, 0))]
    args += [w_r, w_h, w_m, g1.reshape(1, d), modtab]
    out_specs = [pl.BlockSpec((tm, d), lambda i: (i, 0))]
    out_shape = [jax.ShapeDtypeStruct((n_rows, d), F32)]
    if emit_next:
        in_specs.append(pl.BlockSpec((1, d), lambda i: (0, 0)))
        args.append(g2_next.reshape(1, d))
        out_specs.append(pl.BlockSpec((tm, d), lambda i: (i, 0)))
        out_shape.append(jax.ShapeDtypeStruct((n_rows, d), BF16))
    return pl.pallas_call(
        functools.partial(_wout_kernel, n_lat_tiles=n_lat_tiles, n_mix_parts=len(mix_parts[0]),
                          n_x_parts=len(x_parts), emit_next=emit_next),
        grid=(n_rows // tm,),
        in_specs=in_specs,
        out_specs=out_specs,
        out_shape=out_shape,
        compiler_params=_cparams(("arbitrary",), 52),
        name="wout",
    )(*args)


def _ffn_kernel(*refs, emit_next):
    if emit_next:
        h_ref, w1_ref, w3_ref, w2_ref, x_ref, g_ref, mod_ref, gn_ref, modn_ref, o_ref, hn_ref, acc_ref = refs
    else:
        h_ref, w1_ref, w3_ref, w2_ref, x_ref, g_ref, mod_ref, o_ref, acc_ref = refs
    f = pl.program_id(1)

    @pl.when(f == 0)
    def _():
        acc_ref[...] = jnp.zeros(acc_ref.shape, F32)

    h = h_ref[...]
    u = _silu(_dot(h, w1_ref[...])) * _dot(h, w3_ref[...])
    acc_ref[...] += _dot(_bf(u), w2_ref[...])

    @pl.when(f == pl.num_programs(1) - 1)
    def _():
        m = mod_ref[0]
        xn = x_ref[...] + m[5:6] * _rms(acc_ref[...], g_ref[...])
        o_ref[...] = xn
        if emit_next:
            mn = modn_ref[0]
            hn_ref[...] = (_rms(xn, gn_ref[...]) * (1.0 + mn[1:2]) + mn[0:1]).astype(hn_ref.dtype)


def _ffn(hb, w1, w3, w2, xs, g3, modtab, lat_rows, g_next, modtab_next):
    n_rows, d = hb.shape
    dff = w1.shape[1]
    nb = modtab.shape[0] - 1
    tm, tf = ROW_TILE, FFN_TF
    mi = _mod_index(tm, lat_rows, nb)
    emit_next = g_next is not None
    nf = dff // tf

    def fs(i, f):
        return jnp.where(i % 2 == 1, nf - 1 - f, f)

    in_specs = [pl.BlockSpec((tm, d), lambda i, f: (i, 0)),
                pl.BlockSpec((d, tf), lambda i, f: (0, fs(i, f))),
                pl.BlockSpec((d, tf), lambda i, f: (0, fs(i, f))),
                pl.BlockSpec((tf, d), lambda i, f: (fs(i, f), 0)),
                pl.BlockSpec((tm, d), lambda i, f: (i, 0)),
                pl.BlockSpec((1, d), lambda i, f: (0, 0)),
                pl.BlockSpec((1, 6, d), lambda i, f: (mi(i), 0, 0))]
    args = [hb, w1, w3, w2, xs, g3.reshape(1, d), modtab]
    out_specs = [pl.BlockSpec((tm, d), lambda i, f: (i, 0))]
    out_shape = [jax.ShapeDtypeStruct((n_rows, d), F32)]
    if emit_next:
        in_specs += [pl.BlockSpec((1, d), lambda i, f: (0, 0)),
                     pl.BlockSpec((1, 6, d), lambda i, f: (mi(i), 0, 0))]
        args += [g_next.reshape(1, d), modtab_next]
        out_specs.append(pl.BlockSpec((tm, d), lambda i, f: (i, 0)))
        out_shape.append(jax.ShapeDtypeStruct((n_rows, d), BF16))
    return pl.pallas_call(
        functools.partial(_ffn_kernel, emit_next=emit_next),
        grid=(n_rows // tm, dff // tf),
        in_specs=in_specs,
        out_specs=out_specs,
        out_shape=out_shape,
        scratch_shapes=[pltpu.VMEM((tm, d), F32)],
        compiler_params=_cparams(("arbitrary", "arbitrary"), 48),
        name="ffn",
    )(*args)


_META_E0, _META_E1, _META_R0, _META_R1, _META_G0, _META_G1 = range(6)


def _router_kernel(x_ref, g_ref, mod_ref, wr_ref, tri_ref, meta_ref, cnt_ref, carry_ref):
    i = pl.program_id(0)

    @pl.when(i == 0)
    def _():
        carry_ref[...] = jnp.zeros(carry_ref.shape, F32)

    m = mod_ref[0]
    hmod = _rms(x_ref[...], g_ref[...]) * (1.0 + m[4:5]) + m[3:4]
    logits = jnp.dot(hmod, wr_ref[...], precision=lax.Precision.HIGHEST, preferred_element_type=F32)
    lane = lax.broadcasted_iota(jnp.int32, logits.shape, 1)
    lanef = lane.astype(F32)
    logits = jnp.where(lane < N_EXPERTS, logits, -jnp.inf)
    v0 = jnp.max(logits, axis=1, keepdims=True)
    e0 = jnp.min(jnp.where(logits == v0, lanef, 1e9), axis=1, keepdims=True)
    rest = jnp.where(lanef == e0, -jnp.inf, logits)
    v1 = jnp.max(rest, axis=1, keepdims=True)
    e1 = jnp.min(jnp.where(rest == v1, lanef, 1e9), axis=1, keepdims=True)
    ex = jnp.exp(v1 - v0)
    g0 = 1.0 / (1.0 + ex)
    g1 = ex / (1.0 + ex)
    oh0 = lanef == e0
    oh1 = lanef == e1
    oh = jnp.where(jnp.logical_or(oh0, oh1), 1.0, 0.0)
    before = _dot(tri_ref[...], _bf(oh)) + carry_ref[0:1, :]
    r0 = jnp.sum(jnp.where(oh0, before, 0.0), axis=1, keepdims=True)
    r1 = jnp.sum(jnp.where(oh1, before, 0.0), axis=1, keepdims=True)
    carry_ref[0:1, :] = carry_ref[0:1, :] + jnp.sum(oh, axis=0, keepdims=True)
    meta = jnp.zeros(logits.shape, F32)
    for j, val in ((_META_E0, e0), (_META_E1, e1), (_META_R0, r0), (_META_R1, r1),
                   (_META_G0, g0), (_META_G1, g1)):
        meta = jnp.where(lane == j, val, meta)
    meta_ref[...] = meta
    cnt_ref[...] = carry_ref[...]


def _router(xs, n_rows, g2, modtab, w_router, lat_rows):
    d = xs.shape[1]
    nb = modtab.shape[0] - 1
    tm = ROW_TILE
    mi = _mod_index(tm, lat_rows, nb)
    wr = jnp.zeros((d, 128), F32).at[:, :N_EXPERTS].set(w_router)
    tri = jnp.asarray(np.tril(np.ones((tm, tm), np.float32), -1), BF16)
    return pl.pallas_call(
        _router_kernel,
        grid=(n_rows // tm,),
        in_specs=[pl.BlockSpec((tm, d), lambda i: (i, 0)),
                  pl.BlockSpec((1, d), lambda i: (0, 0)),
                  pl.BlockSpec((1, 6, d), lambda i: (mi(i), 0, 0)),
                  pl.BlockSpec((d, 128), lambda i: (0, 0)),
                  pl.BlockSpec((tm, tm), lambda i: (0, 0))],
        out_specs=[pl.BlockSpec((tm, 128), lambda i: (i, 0)),
                   pl.BlockSpec((8, 128), lambda i: (0, 0))],
        out_shape=[jax.ShapeDtypeStruct((n_rows, 128), F32),
                   jax.ShapeDtypeStruct((8, 128), F32)],
        scratch_shapes=[pltpu.VMEM((8, 128), F32)],
        compiler_params=_cparams(("arbitrary",), 32),
        name="router",
    )(xs, g2.reshape(1, d), modtab, wr, tri)


def _dispatch_kernel(dest_ref, x_ref, g_ref, mod_ref, init_ref, o_ref, h_s, sem):
    del init_ref
    i = pl.program_id(0)
    tm = h_s.shape[0]
    m = mod_ref[0]
    h_s[...] = _rms(x_ref[...], g_ref[...]) * (1.0 + m[4:5]) + m[3:4]

    def row_copy(r, k):
        dst = dest_ref[TOP_K * (i * tm + r) + k]
        return pltpu.make_async_copy(h_s.at[pl.ds(r, 1)], o_ref.at[pl.ds(dst, 1)], sem)

    def start(r, carry):
        for k in range(TOP_K):
            row_copy(r, k).start(priority=k)
        return carry

    lax.fori_loop(0, tm, start, 0, unroll=DMA_UNROLL)

    def wait(r, carry):
        for k in range(TOP_K):
            row_copy(r, k).wait()
        return carry

    lax.fori_loop(0, tm, wait, 0, unroll=DMA_UNROLL)


def _dispatch(dest, xs, n_rows, g2, modtab, n_slots, lat_rows):
    d = xs.shape[1]
    nb = modtab.shape[0] - 1
    tm = ROW_TILE
    mi = _mod_index(tm, lat_rows, nb)
    init = jnp.zeros((n_slots, d), F32)
    grid_spec = pltpu.PrefetchScalarGridSpec(
        num_scalar_prefetch=1,
        grid=(n_rows // tm,),
        in_specs=[pl.BlockSpec((tm, d), lambda i, dst: (i, 0)),
                  pl.BlockSpec((1, d), lambda i, dst: (0, 0)),
                  pl.BlockSpec((1, 6, d), lambda i, dst: (mi(i), 0, 0)),
                  pl.BlockSpec(memory_space=pl.ANY)],
        out_specs=pl.BlockSpec(memory_space=pl.ANY),
        scratch_shapes=[pltpu.VMEM((tm, d), F32), pltpu.SemaphoreType.DMA(())],
    )
    return pl.pallas_call(
        _dispatch_kernel,
        grid_spec=grid_spec,
        out_shape=jax.ShapeDtypeStruct((n_slots, d), F32),
        input_output_aliases={4: 0},
        compiler_params=_cparams(("arbitrary",), 32),
        name="dispatch",
    )(dest, xs, g2.reshape(1, d), modtab, init)


def _expert_kernel(ge_ref, nt_ref, ng_ref, x_ref, w1_ref, w3_ref, w2_ref, o_ref, xb_s):
    s = pl.program_id(0)
    f = pl.program_id(1)
    tm = MOE_TM

    @pl.when(s < ng_ref[0])
    def _():
        @pl.when(f == 0)
        def _():
            xb_s[...] = _bf(x_ref[...])
            o_ref[...] = jnp.zeros(o_ref.shape, F32)

        w1b, w3b, w2b = _bf(w1_ref[0]), _bf(w3_ref[0]), _bf(w2_ref[0])

        def tile(t):
            rows = pl.ds(t * tm, tm)
            h = xb_s[rows, :]
            u = _silu(_dot(h, w1b)) * _dot(h, w3b)
            o_ref[rows, :] += _dot(_bf(u), w2b)

        tile(0)
        for t in range(1, MOE_GROUP):
            pl.when(nt_ref[s] > t)(functools.partial(tile, t))

    @pl.when(jnp.logical_and(s >= ng_ref[0], f == pl.num_programs(1) - 1))
    def _():
        o_ref[...] = jnp.zeros(o_ref.shape, F32)


def _experts(group_e, group_tiles, n_groups, xsort, w1, w3, w2):
    n_slots, d = xsort.shape
    dff = w1.shape[2]
    tg, tf = MOE_TM * MOE_GROUP, MOE_TF
    nf = dff // tf

    def ss(s, ng):
        return jnp.maximum(jnp.minimum(s, ng[0] - 1), 0)

    def ff(s, f, ng):
        snake = jnp.where(s % 2 == 1, nf - 1 - f, f)
        last = jnp.where((ng[0] - 1) % 2 == 1, 0, nf - 1)
        return jnp.where(s < ng[0], snake, last)

    grid_spec = pltpu.PrefetchScalarGridSpec(
        num_scalar_prefetch=3,
        grid=(n_slots // tg, nf),
        in_specs=[pl.BlockSpec((tg, d), lambda s, f, ge, nt, ng: (ss(s, ng), 0)),
                  pl.BlockSpec((1, d, tf), lambda s, f, ge, nt, ng: (ge[ss(s, ng)], 0, ff(s, f, ng))),
                  pl.BlockSpec((1, d, tf), lambda s, f, ge, nt, ng: (ge[ss(s, ng)], 0, ff(s, f, ng))),
                  pl.BlockSpec((1, tf, d), lambda s, f, ge, nt, ng: (ge[ss(s, ng)], ff(s, f, ng), 0))],
        out_specs=pl.BlockSpec((tg, d), lambda s, f, ge, nt, ng: (s, 0)),
        scratch_shapes=[pltpu.VMEM((tg, d), BF16)],
    )
    return pl.pallas_call(
        _expert_kernel,
        grid_spec=grid_spec,
        out_shape=jax.ShapeDtypeStruct((n_slots, d), F32),
        compiler_params=_cparams(("arbitrary", "arbitrary"), 52),
        name="experts",
    )(group_e, group_tiles, n_groups, xsort, w1, w3, w2)


def _combine_kernel(dest_ref, y_ref, meta_ref, x_ref, g_ref, mod_ref, o_ref, buf_s, sem):
    i = pl.program_id(0)
    tm = x_ref.shape[0]

    def row_copy(r, k):
        src = dest_ref[TOP_K * (i * tm + r) + k]
        return pltpu.make_async_copy(y_ref.at[pl.ds(src, 1)], buf_s.at[k, pl.ds(r, 1)], sem)

    def start(r, carry):
        for k in range(TOP_K):
            row_copy(r, k).start(priority=k)
        return carry

    lax.fori_loop(0, tm, start, 0, unroll=DMA_UNROLL)

    def wait(r, carry):
        for k in range(TOP_K):
            row_copy(r, k).wait()
        return carry

    lax.fori_loop(0, tm, wait, 0, unroll=DMA_UNROLL)

    meta = meta_ref[...]
    lane = lax.broadcasted_iota(jnp.int32, meta.shape, 1)
    g0 = jnp.sum(jnp.where(lane == _META_G0, meta, 0.0), axis=1, keepdims=True)
    g1 = jnp.sum(jnp.where(lane == _META_G1, meta, 0.0), axis=1, keepdims=True)
    y = buf_s[0] * g0 + buf_s[1] * g1
    m = mod_ref[0]
    o_ref[...] = x_ref[...] + m[5:6] * _rms(y, g_ref[...])


def _combine(dest, yb, meta, xs, n_rows, g3, modtab, lat_rows):
    d = xs.shape[1]
    nb = modtab.shape[0] - 1
    tm = ROW_TILE
    mi = _mod_index(tm, lat_rows, nb)
    grid_spec = pltpu.PrefetchScalarGridSpec(
        num_scalar_prefetch=1,
        grid=(n_rows // tm,),
        in_specs=[pl.BlockSpec(memory_space=pl.ANY),
                  pl.BlockSpec((tm, 128), lambda i, dst: (i, 0)),
                  pl.BlockSpec((tm, d), lambda i, dst: (i, 0)),
                  pl.BlockSpec((1, d), lambda i, dst: (0, 0)),
                  pl.BlockSpec((1, 6, d), lambda i, dst: (mi(i), 0, 0))],
        out_specs=pl.BlockSpec((tm, d), lambda i, dst: (i, 0)),
        scratch_shapes=[pltpu.VMEM((TOP_K, tm, d), F32), pltpu.SemaphoreType.DMA(())],
    )
    return pl.pallas_call(
        _combine_kernel,
        grid_spec=grid_spec,
        out_shape=jax.ShapeDtypeStruct((n_rows, d), F32),
        compiler_params=_cparams(("arbitrary",), 40),
        name="combine",
    )(dest, yb, meta, xs, g3.reshape(1, d), modtab)


def _moe(xs, n_rows, g2, g3, modtab, w_router, w1, w3, w2, lat_rows):
    meta, cnt = _router(xs, n_rows, g2, modtab, w_router, lat_rows)
    counts = cnt[0, :N_EXPERTS].astype(jnp.int32)
    tg = MOE_TM * MOE_GROUP
    padded = (counts + tg - 1) // tg * tg
    pad_end = jnp.cumsum(padded)
    pad_start = pad_end - padded
    e = meta[:, _META_E0:_META_E1 + 1].astype(jnp.int32)
    r = meta[:, _META_R0:_META_R1 + 1].astype(jnp.int32)
    dest = (pad_start[e] + r).reshape(-1)
    n_groups_max = (n_rows * TOP_K) // tg + N_EXPERTS
    n_slots = n_groups_max * tg
    group_start = jnp.arange(n_groups_max, dtype=jnp.int32) * tg
    group_e = jnp.minimum(jnp.sum((group_start[:, None] >= pad_end[None, :]).astype(jnp.int32), axis=1),
                          N_EXPERTS - 1)
    filled = counts[group_e] - (group_start - pad_start[group_e])
    group_tiles = jnp.clip((filled + MOE_TM - 1) // MOE_TM, 1, MOE_GROUP).astype(jnp.int32)
    n_groups = (pad_end[-1:] // tg).astype(jnp.int32)
    xsort = _dispatch(dest, xs, n_rows, g2, modtab, n_slots, lat_rows)
    yb = _experts(group_e, group_tiles, n_groups, xsort, w1, w3, w2)
    return _combine(dest, yb, meta, xs, n_rows, g3, modtab, lat_rows)


def _rotary_tables(lat_len, ctx_len):
    rows = lat_len // GRID_W
    row = jnp.repeat(jnp.arange(rows, dtype=F32), GRID_W)
    col = jnp.tile(jnp.arange(GRID_W, dtype=F32), rows)
    n_freq = RET_DK // 4
    inv = ROPE_BASE ** (-jnp.arange(n_freq, dtype=F32) / n_freq)
    ang = jnp.concatenate([row[:, None] * inv, col[:, None] * inv], axis=-1)
    cos, sin = jnp.cos(ang), jnp.sin(ang)
    cosf = jnp.concatenate([jnp.ones((ctx_len, RET_DK), F32), jnp.concatenate([cos, cos], -1)], 0)
    sinf = jnp.concatenate([jnp.zeros((ctx_len, RET_DK), F32), jnp.concatenate([-sin, sin], -1)], 0)
    return cosf, sinf


def kernel(x, c, ctx, c_ctx, w_ada, b_ada, norm_g, w_in, w_out, ret_decay, ret_gn, hgrn_lb, hgrn_gn,
           mlstm_conv, mlstm_gate_b, mlstm_gn, w_ffn1, w_ffn3, w_ffn2, w_router, w_exp1, w_exp3, w_exp2):
    nb, lat_len, d = x.shape
    ctx_len = ctx.shape[1]
    depth = w_ada.shape[0]
    n_lat_rows = nb * lat_len
    n_rows = n_lat_rows + nb * ctx_len

    x_parts = [x.reshape(n_lat_rows, d), ctx.reshape(nb * ctx_len, d)]
    w_in_t = jnp.swapaxes(w_in, 1, 2)
    s_in = jnp.zeros((8, d), F32).at[:nb].set(c).at[nb].set(c_ctx)
    mod_all = _ada(s_in, w_ada, b_ada)[:, :nb + 1].reshape(depth, nb + 1, 6, d)

    cosf, sinf = _rotary_tables(lat_len, ctx_len)
    hg_cum, hg_masks, hg_signs = _hgrn_constants()
    hg_consts = (jnp.asarray(hg_cum, BF16), jnp.asarray(hg_masks, F32), jnp.asarray(hg_signs, F32))
    sm = jax.nn.softmax(hgrn_lb.astype(F32), axis=0)
    lb_all = jnp.clip(jnp.cumsum(sm, axis=0) - sm[0], 0.0, 1.0)

    hb = None
    for layer in range(depth):
        last = layer == depth - 1
        modtab = mod_all[layer]
        g = norm_g[layer]

        if hb is None:
            hb = _prenorm(x_parts, n_rows, g[0], modtab, 1, 0, lat_len, BF16)
        p = _mm(hb, w_in_t, layer, MM_TM, MM_TN, n_cols=PROJ_MAIN)
        w_g = jnp.zeros((1, 128, d), F32).at[0, :MLSTM_GATES].set(w_in_t[layer, PROJ_MAIN:])
        gates = _mm(hb, w_g, 0, MM_TM, 128)[:, :MLSTM_GATES]
        hb = None
        g_row_lat = jnp.swapaxes(gates[:n_lat_rows].reshape(nb, lat_len, MLSTM_GATES), 1, 2)
        g_row_ctx = jnp.swapaxes(gates[n_lat_rows:].reshape(nb, ctx_len, MLSTM_GATES), 1, 2)

        log_g = jax.nn.log_sigmoid(ret_decay[layer].astype(F32))
        lg_b = jnp.broadcast_to(log_g[:, :, None, None], (2, RET_HEADS, 8, RET_DV))
        lb = lb_all[layer]
        emit_ctx = not last
        o_ret = _retention(p, nb, lat_len, ctx_len, cosf, sinf, lg_b, ret_gn[layer], emit_ctx)
        o_hg = _hgrn(p, nb, lat_len, ctx_len, jnp.log(lb), jnp.log1p(-lb), hgrn_gn[layer],
                     hg_consts, emit_ctx)
        o_ml = _mlstm(p, gates, g_row_lat, g_row_ctx, nb, lat_len, ctx_len, mlstm_gate_b[layer],
                      mlstm_conv[layer], mlstm_gn[layer], emit_ctx)
        mix_parts = [list(o) for o in (o_ret, o_hg, o_ml)]
        rows_now = n_lat_rows if last else n_rows
        dense = layer % 2 == 0
        res = _wout(mix_parts, _bf(w_out[layer]), x_parts, rows_now, g[1], modtab, lat_len,
                    g[2] if dense else None)
        xs = res[0]
        x_parts = [xs]

        j = layer // 2
        if dense:
            nxt = (None, None) if last else (norm_g[layer + 1][0], mod_all[layer + 1])
            res = _ffn(res[1], _bf(w_ffn1[j]), _bf(w_ffn3[j]), _bf(w_ffn2[j]), xs, g[3], modtab, lat_len, *nxt)
            xs = res[0]
            hb = None if last else res[1]
        else:
            xs = _moe(xs, rows_now, g[2], g[3], modtab, w_router[j], w_exp1[j], w_exp3[j], w_exp2[j], lat_len)
        x_parts = [xs]
    return xs[:n_lat_rows].reshape(nb, lat_len, d)
```

```python
import functools

import numpy as np
import jax
import jax.numpy as jnp
from jax import lax
from jax.experimental import pallas as pl
from jax.experimental.pallas import tpu as pltpu

F32 = jnp.float32
BF16 = jnp.bfloat16

CHUNK = 128
NORM_EPS = 1e-6
ROPE_BASE = 10000.0
GRID_W = 64

RET_HEADS, RET_DK, RET_DV = 4, 128, 256
HGRN_HEADS, HGRN_DK, HGRN_DV = 4, 128, 128
MLSTM_HEADS, MLSTM_DK, MLSTM_DV = 4, 128, 128
N_EXPERTS = 8
TOP_K = 2

RET_QK = RET_HEADS * RET_DK
RET_WIDTH = RET_HEADS * RET_DV
HGRN_QK = HGRN_HEADS * HGRN_DK
HGRN_WIDTH = HGRN_HEADS * HGRN_DV
MLSTM_QK = MLSTM_HEADS * MLSTM_DK
MLSTM_WIDTH = MLSTM_HEADS * MLSTM_DV
MLSTM_GATES = 4 * MLSTM_HEADS

_C_RET_Q = 0
_C_RET_K = _C_RET_Q + RET_QK // 128
_C_RET_V = _C_RET_K + RET_QK // 128
_C_RET_G = _C_RET_V + RET_WIDTH // 128
_C_HG_Q = _C_RET_G + RET_WIDTH // 128
_C_HG_FF = _C_HG_Q + HGRN_QK // 128
_C_HG_FB = _C_HG_FF + HGRN_QK // 128
_C_HG_I = _C_HG_FB + HGRN_QK // 128
_C_HG_G = _C_HG_I + HGRN_WIDTH // 128
_C_ML_Q = _C_HG_G + HGRN_WIDTH // 128
_C_ML_K = _C_ML_Q + MLSTM_QK // 128
_C_ML_V = _C_ML_K + MLSTM_QK // 128
_C_ML_O = _C_ML_V + MLSTM_WIDTH // 128
PROJ_MAIN = (_C_ML_O + MLSTM_WIDTH // 128) * 128

_VMEM_CAP_BYTES = 56 * 1024 * 1024

ROW_TILE = 512
MM_TM, MM_TN = 1024, 768
FFN_TF = 512
MOE_TM = 512
MOE_GROUP = 1
MOE_TF = 512
RET_UNROLL = (4, 4)
HGRN_UNROLL = (2, 4)
MLSTM_UNROLL = (2, 2)
DMA_UNROLL = 8


def _cparams(sem, vmem_mb):
    return pltpu.CompilerParams(dimension_semantics=sem,
                                vmem_limit_bytes=min(int(vmem_mb * 1024 * 1024), _VMEM_CAP_BYTES))


def _bf(x):
    return x.astype(BF16)


def _dot(a, b):
    return jnp.dot(a, b, preferred_element_type=F32)


def _dot_nt(a, b):
    return lax.dot_general(a, b, (((1,), (1,)), ((), ())), preferred_element_type=F32)


def _dot_tn(a, b):
    return lax.dot_general(a, b, (((0,), (0,)), ((), ())), preferred_element_type=F32)


def _sigmoid(x):
    return 1.0 / (1.0 + jnp.exp(-x))


def _silu(x):
    return x * _sigmoid(x)


def _log_sigmoid(x):
    return jnp.minimum(x, 0.0) - jnp.log(1.0 + jnp.exp(-jnp.abs(x)))


def _rms(x, g):
    return x * lax.rsqrt(jnp.mean(x * x, axis=-1, keepdims=True) + NORM_EPS) * g


def _mod_index(tile_rows, n_lat_rows_per_batch, n_batch):
    return lambda i: jnp.minimum((i * tile_rows) // n_lat_rows_per_batch, n_batch)


def _ada_kernel(s_ref, w_ref, b_ref, o_ref):
    s = _bf(_silu(s_ref[...]))
    o_ref[0] = _dot(s, _bf(w_ref[0])) + b_ref[0]


def _ada(s_in, w_ada, b_ada):
    depth, d, n = w_ada.shape
    tn = 1024
    return pl.pallas_call(
        _ada_kernel,
        grid=(depth, n // tn),
        in_specs=[pl.BlockSpec((8, d), lambda l, j: (0, 0)),
                  pl.BlockSpec((1, d, tn), lambda l, j: (l, 0, j)),
                  pl.BlockSpec((1, 1, tn), lambda l, j: (l, 0, j))],
        out_specs=pl.BlockSpec((1, 8, tn), lambda l, j: (l, 0, j)),
        out_shape=jax.ShapeDtypeStruct((depth, 8, n), F32),
        compiler_params=_cparams(("arbitrary", "arbitrary"), 40),
        name="ada",
    )(s_in, w_ada, b_ada.reshape(depth, 1, n))


def _part_specs(parts, tm, width):
    if len(parts) == 1:
        return [pl.BlockSpec((tm, width), lambda i: (i, 0))]
    n0 = parts[0].shape[0] // tm
    return [pl.BlockSpec((tm, width), lambda i: (jnp.minimum(i, n0 - 1), 0)),
            pl.BlockSpec((tm, width), lambda i: (jnp.maximum(i - n0, 0), 0))]


def _on_part(n_first, n_parts, fn):
    if n_parts == 1:
        fn(0)
    else:
        i = pl.program_id(0)
        pl.when(i < n_first)(lambda: fn(0))
        pl.when(i >= n_first)(lambda: fn(1))


def _prenorm_kernel(*refs, sc, sh, n_first):
    x_refs, (g_ref, mod_ref, o_ref) = refs[:-3], refs[-3:]

    def run(part):
        m = mod_ref[0]
        y = _rms(x_refs[part][...], g_ref[...])
        o_ref[...] = (y * (1.0 + m[sc:sc + 1]) + m[sh:sh + 1]).astype(o_ref.dtype)

    _on_part(n_first, len(x_refs), run)


def _prenorm(x_parts, n_rows, g, modtab, sc, sh, lat_rows, out_dtype):
    d = x_parts[0].shape[1]
    nb = modtab.shape[0] - 1
    mi = _mod_index(ROW_TILE, lat_rows, nb)
    return pl.pallas_call(
        functools.partial(_prenorm_kernel, sc=sc, sh=sh, n_first=x_parts[0].shape[0] // ROW_TILE),
        grid=(n_rows // ROW_TILE,),
        in_specs=[*_part_specs(x_parts, ROW_TILE, d),
                  pl.BlockSpec((1, d), lambda i: (0, 0)),
                  pl.BlockSpec((1, 6, d), lambda i: (mi(i), 0, 0))],
        out_specs=pl.BlockSpec((ROW_TILE, d), lambda i: (i, 0)),
        out_shape=jax.ShapeDtypeStruct((n_rows, d), out_dtype),
        compiler_params=_cparams(("arbitrary",), 32),
        name="prenorm",
    )(*x_parts, g.reshape(1, d), modtab)


def _mm_kernel(x_ref, wt_ref, o_ref, wb_s):
    @pl.when(pl.program_id(1) == 0)
    def _():
        wb_s[...] = _bf(wt_ref[0])

    o_ref[...] = _dot_nt(x_ref[...], wb_s[...]).astype(o_ref.dtype)


def _mm(x, wt, layer, tm, tn, n_cols=None, out_dtype=F32):
    m, k = x.shape
    n = wt.shape[1] if n_cols is None else n_cols
    return pl.pallas_call(
        _mm_kernel,
        grid=(n // tn, m // tm),
        in_specs=[pl.BlockSpec((tm, k), lambda j, i: (i, 0)),
                  pl.BlockSpec((1, tn, k), lambda j, i: (layer, j, 0))],
        out_specs=pl.BlockSpec((tm, tn), lambda j, i: (i, j)),
        out_shape=jax.ShapeDtypeStruct((m, n), out_dtype),
        scratch_shapes=[pltpu.VMEM((tn, k), BF16)],
        compiler_params=_cparams(("arbitrary", "arbitrary"), 48),
        name="proj",
    )(x, wt)


def _bwd_chunk(i, n_ctx, n_lat):
    return jnp.where(i < n_ctx, n_ctx - 1 - i, 2 * n_ctx + n_lat - 1 - i)


def _rows(c):
    return pl.ds(pl.multiple_of(c * CHUNK, CHUNK), CHUNK)


def _for_chunks(n_ctx, n_lat, fn, unroll, with_ctx=True):
    def run(part, n, off):
        def body(c, carry):
            fn(part, c, c + off)
            return carry
        lax.fori_loop(0, n, body, 0, unroll=max(u for u in (1, 2, unroll) if n % u == 0 and u <= unroll))
    if with_ctx:
        run(0, n_ctx, 0)
    run(1, n_lat, n_ctx)


def _lanes(d, w):
    return slice(d * w, (d + 1) * w)


def _mixer_call(kernel_fn, name, in_specs, args, nb, heads, lat_len, ctx_len, dv, scratch, emit_ctx, vmem_mb):
    width = heads * dv
    out_specs = [pl.BlockSpec((lat_len, dv), lambda b, h: (b, h))]
    out_shape = [jax.ShapeDtypeStruct((nb * lat_len, width), BF16)]
    if emit_ctx:
        out_specs.append(pl.BlockSpec((ctx_len, dv), lambda b, h: (b, h)))
        out_shape.append(jax.ShapeDtypeStruct((nb * ctx_len, width), BF16))
    n_in = len(in_specs)

    def body(*refs):
        ins, rest = refs[:n_in], refs[n_in:]
        if emit_ctx:
            ol, oc, scr = rest[0], rest[1], rest[2:]
        else:
            ol, oc, scr = rest[0], None, rest[1:]
        kernel_fn(*ins, ol, oc, *scr, n_ctx=ctx_len // CHUNK, n_lat=lat_len // CHUNK)

    return pl.pallas_call(
        body,
        grid=(nb, heads),
        in_specs=in_specs,
        out_specs=out_specs,
        out_shape=out_shape,
        scratch_shapes=scratch,
        compiler_params=_cparams(("arbitrary", "arbitrary"), vmem_mb),
        name=name,
    )(*args)


def _seq_specs(col0, width, nb, lat_len, ctx_len):
    n_lat_blk = nb * lat_len // ctx_len
    c0 = col0 * 128 // width
    return (pl.BlockSpec((lat_len, width), lambda b, h: (b, c0 + h)),
            pl.BlockSpec((ctx_len, width), lambda b, h: (n_lat_blk + b, c0 + h)))


def _ret_kernel(ql, qc, kl, kc, vl, vc, gl, gc, cos_ref, sin_ref, lg_ref, gn_ref, ol_ref, oc_ref,
                qs_s, kv_s, sall_s, o_s, st_s, dm_s, dq_s, dk_s, *, n_ctx, n_lat):
    nch = n_ctx + n_lat
    unroll_a, unroll_c = RET_UNROLL
    qr, kr, vr, gr, outr = (qc, ql), (kc, kl), (vc, vl), (gc, gl), (oc_ref, ol_ref)
    t_i = lax.broadcasted_iota(jnp.int32, (CHUNK, CHUNK), 0).astype(F32)
    s_i = lax.broadcasted_iota(jnp.int32, (CHUNK, CHUNK), 1).astype(F32)
    dm = None
    for d in range(2):
        lg = lg_ref[d, 0][0:1, 0:CHUNK]
        rel = (t_i - s_i) if d == 0 else (s_i - t_i)
        dmd = jnp.where(rel >= 0, jnp.exp(jnp.maximum(rel, 0.0) * lg), 0.0)
        dm = dmd if dm is None else dm + dmd
        p = t_i if d == 0 else (CHUNK - 1.0) - t_i
        dq_s[d] = jnp.exp((p + 1.0) * lg)
        dk_s[d] = jnp.exp((CHUNK - 1.0 - p) * lg)
    dm_s[...] = dm
    scale = RET_DK ** -0.5

    def phase_a(part, cl, cg):
        rl, rg = _rows(cl), _rows(cg)
        cs = cos_ref[rg, :]
        sn = sin_ref[rg, :]
        q = qr[part][rl, :]
        k = kr[part][rl, :]
        q = q * cs + pltpu.roll(q, RET_DK // 2, 1) * sn
        k = (k * cs + pltpu.roll(k, RET_DK // 2, 1) * sn) * scale
        vb = _bf(vr[part][rl, :])
        s = _dot_nt(_bf(q), _bf(k))
        o_s[rg, :] = _dot(_bf(s * dm_s[...]), vb)
        qs_s[rg, :] = jnp.concatenate([_bf(q * dq_s[0]), _bf(q * dq_s[1])], axis=1)
        kd = jnp.concatenate([_bf(k * dk_s[0]), _bf(k * dk_s[1])], axis=1)
        kv_s[cg] = _dot_tn(kd, vb)

    _for_chunks(n_ctx, n_lat, phase_a, unroll_a)

    st_s[...] = jnp.zeros(st_s.shape, F32)
    dec = [jnp.exp(float(CHUNK) * lg_ref[d, 0][0:1, :]) for d in range(2)]

    def phase_b(i, carry):
        for d, c in ((0, i), (1, _bwd_chunk(i, n_ctx, n_lat))):
            st = st_s[d]
            sall_s[c, _lanes(d, RET_DK), :] = _bf(st)
            st_s[d] = dec[d] * st + kv_s[c, _lanes(d, RET_DK), :]
        return carry

    lax.fori_loop(0, nch, phase_b, 0)
    gn = gn_ref[...]

    def phase_c(part, cl, cg):
        rl, rg = _rows(cl), _rows(cg)
        o = o_s[rg, :] + _dot(qs_s[rg, :], sall_s[cg])
        outr[part][rl, :] = (_rms(o, gn) * _silu(gr[part][rl, :])).astype(BF16)

    _for_chunks(n_ctx, n_lat, phase_c, unroll_c, with_ctx=oc_ref is not None)


def _retention(p, nb, lat_len, ctx_len, cosf, sinf, lg_b, gn, emit_ctx):
    t_len = lat_len + ctx_len
    nch = t_len // CHUNK
    s128 = functools.partial(_seq_specs, width=128, nb=nb, lat_len=lat_len, ctx_len=ctx_len)
    s256 = functools.partial(_seq_specs, width=256, nb=nb, lat_len=lat_len, ctx_len=ctx_len)
    in_specs = [*s128(_C_RET_Q), *s128(_C_RET_K), *s256(_C_RET_V), *s256(_C_RET_G),
                pl.BlockSpec((t_len, 128), lambda b, h: (0, 0)),
                pl.BlockSpec((t_len, 128), lambda b, h: (0, 0)),
                pl.BlockSpec((2, 1, 8, RET_DV), lambda b, h: (0, h, 0, 0)),
                pl.BlockSpec((1, RET_DV), lambda b, h: (0, h))]
    scratch = [pltpu.VMEM((t_len, 2 * RET_DK), BF16),
               pltpu.VMEM((nch, 2 * RET_DK, RET_DV), F32),
               pltpu.VMEM((nch, 2 * RET_DK, RET_DV), BF16),
               pltpu.VMEM((t_len, RET_DV), F32),
               pltpu.VMEM((2, RET_DK, RET_DV), F32),
               pltpu.VMEM((CHUNK, CHUNK), F32),
               pltpu.VMEM((2, CHUNK, CHUNK), F32), pltpu.VMEM((2, CHUNK, CHUNK), F32)]
    args = (p, p, p, p, p, p, p, p, cosf, sinf, lg_b, gn.reshape(1, RET_WIDTH))
    return _mixer_call(_ret_kernel, "retention", in_specs, args, nb, RET_HEADS, lat_len, ctx_len, RET_DV,
                       scratch, emit_ctx, 48)


_HG_LEVELS = (64, 32, 16, 8, 4, 2, 1)


def _hgrn_constants():
    c = CHUNK
    t = np.arange(c)[:, None]
    u = np.arange(c)[None, :]
    cum = (u <= t).astype(np.float32)
    masks, signs = [], []
    for m in _HG_LEVELS:
        base = (t // (2 * m)) * (2 * m)
        lower = t >= base + m
        tb = t // (2 * m)
        sb = u // (2 * m)
        masks.append((tb == sb) & lower & (u < (sb * 2 * m + m)))
        signs.append(np.broadcast_to(np.where(lower, 1.0, -1.0), (c, c)))
    masks.append(t == u)
    kf = np.stack([x.astype(np.float32) for x in masks], axis=0)
    kb = np.stack([x.astype(np.float32)[::-1, ::-1] for x in masks], axis=0)
    sf = np.stack([x.astype(np.float32) for x in signs], axis=0)
    sb_ = np.stack([x.astype(np.float32)[::-1, ::-1] for x in signs], axis=0)
    return (np.stack([cum, cum[::-1, ::-1]], 0), np.stack([kf, kb], 0),
            np.stack([sf[:_HG_WIDE], sb_[:_HG_WIDE]], 0))


_HG_WIDE = 5


def _hgrn_level_exponents(b, lf, d, sgn_ref, row):
    out = []
    for l, m in enumerate(_HG_LEVELS[:_HG_WIDE]):
        pieces = []
        for j in range(CHUNK // (2 * m)):
            r = 2 * m * j + (m - 1 if d == 0 else m)
            pieces.append(jnp.broadcast_to(b[r:r + 1, :], (2 * m, CHUNK)))
        bref = pieces[0] if len(pieces) == 1 else jnp.concatenate(pieces, axis=0)
        out.append((b - bref) * sgn_ref[d, l])
    up = pltpu.roll(lf, CHUNK - 1, 0)
    dn = pltpu.roll(lf, 1, 0)
    r4 = row % 4
    if d == 0:
        e2 = jnp.where(r4 == 0, up, jnp.where(r4 == 1, 0.0, jnp.where(r4 == 2, lf, lf + dn)))
        e1 = jnp.where(row % 2 == 1, lf, 0.0)
    else:
        e2 = jnp.where(r4 == 0, lf + up, jnp.where(r4 == 1, lf, jnp.where(r4 == 2, 0.0, dn)))
        e1 = jnp.where(row % 2 == 0, lf, 0.0)
    return out + [e2, e1]


def _hgrn_kernel(ql, qc, ffl, ffc, fbl, fbc, il, ic, gl, gc, llb_ref, lub_ref, gn_ref, c_ref, k_ref, sgn_ref,
                 ol_ref, oc_ref, qs_s, kv_s, sall_s, dec_s, o_s, st_s, *, n_ctx, n_lat):
    nch = n_ctx + n_lat
    unroll_a, unroll_c = HGRN_UNROLL
    qr, fr, ir, gr, outr =(qc, ql), ((ffc, ffl), (fbc, fbl)), (ic, il), (gc, gl), (oc_ref, ol_ref)
    llb = llb_ref[...]
    lub = lub_ref[...]
    nlev = len(_HG_LEVELS)
    row = lax.broadcasted_iota(jnp.int32, (CHUNK, HGRN_DK), 0)

    def phase_a(part, cl, cg):
        rl, rg = _rows(cl), _rows(cg)
        q = _silu(qr[part][rl, :])
        qb = _bf(q)
        vb = _bf(ir[part][rl, :])
        a_sum, qs, kds = None, [], []
        for d in range(2):
            lsg = lub + _log_sigmoid(fr[d][part][rl, :])
            lf = jnp.maximum(llb, lsg) + jnp.log(1.0 + jnp.exp(-jnp.abs(llb - lsg)))
            k = 1.0 - jnp.exp(lf)
            lf_hi = _bf(lf)
            lf_lo = _bf(lf - lf_hi.astype(F32))
            b2 = _dot(c_ref[d], jnp.concatenate([lf_hi, lf_lo], axis=1))
            b = b2[:, 0:HGRN_DK] + b2[:, HGRN_DK:]
            es = _hgrn_level_exponents(b, lf, d, sgn_ref, row)
            kb = _bf(k)
            a = k_ref[d, nlev] * _dot_nt(qb, kb)
            for l in range(nlev):
                xb = _bf(jnp.exp(es[l]))
                a = a + k_ref[d, l] * _dot_nt(qb * xb, kb * xb)
            a_sum = a if a_sum is None else a_sum + a
            b_last = b[CHUNK - 1:CHUNK] if d == 0 else b[0:1]
            qs.append(_bf(q * jnp.exp(b)))
            kds.append(_bf(k * jnp.exp(b_last - b)))
            dec_s[cg, :, _lanes(d, HGRN_DK)] = jnp.exp(b_last)
        o_s[rg, :] = _dot(_bf(a_sum), vb)
        qs_s[rg, :] = jnp.concatenate(qs, axis=1)
        kv_s[cg] = _dot_tn(vb, jnp.concatenate(kds, axis=1))

    _for_chunks(n_ctx, n_lat, phase_a, unroll_a)
    st_s[...] = jnp.zeros(st_s.shape, F32)

    def phase_b(i, carry):
        for d, c in ((0, i), (1, _bwd_chunk(i, n_ctx, n_lat))):
            ln = _lanes(d, HGRN_DK)
            st = st_s[d]
            sall_s[c, :, ln] = _bf(st)
            st_s[d] = dec_s[c, :, ln] * st + kv_s[c, :, ln]
        return carry

    lax.fori_loop(0, nch, phase_b, 0)
    gn = gn_ref[...]

    def phase_c(part, cl, cg):
        rl, rg = _rows(cl), _rows(cg)
        o = o_s[rg, :] + _dot_nt(qs_s[rg, :], sall_s[cg])
        outr[part][rl, :] = (_rms(o, gn) * _silu(gr[part][rl, :])).astype(BF16)

    _for_chunks(n_ctx, n_lat, phase_c, unroll_c, with_ctx=oc_ref is not None)


def _hgrn(p, nb, lat_len, ctx_len, log_lb, log_ub, gn, consts, emit_ctx):
    mats, masks, signs = consts
    t_len = lat_len + ctx_len
    nch = t_len // CHUNK
    spec = functools.partial(_seq_specs, width=128, nb=nb, lat_len=lat_len, ctx_len=ctx_len)
    vec = pl.BlockSpec((1, 128), lambda b, h: (0, h))
    in_specs = [*spec(_C_HG_Q), *spec(_C_HG_FF), *spec(_C_HG_FB), *spec(_C_HG_I), *spec(_C_HG_G),
                vec, vec, vec,
                pl.BlockSpec(mats.shape, lambda b, h: (0, 0, 0)),
                pl.BlockSpec(masks.shape, lambda b, h: (0, 0, 0, 0)),
                pl.BlockSpec(signs.shape, lambda b, h: (0, 0, 0, 0))]
    scratch = [pltpu.VMEM((t_len, 2 * HGRN_DK), BF16),
               pltpu.VMEM((nch, HGRN_DV, 2 * HGRN_DK), F32),
               pltpu.VMEM((nch, HGRN_DV, 2 * HGRN_DK), BF16),
               pltpu.VMEM((nch, 1, 2 * HGRN_DK), F32),
               pltpu.VMEM((t_len, HGRN_DV), F32),
               pltpu.VMEM((2, HGRN_DV, HGRN_DK), F32)]
    args = (p, p, p, p, p, p, p, p, p, p, log_lb.reshape(1, HGRN_QK), log_ub.reshape(1, HGRN_QK),
            gn.reshape(1, HGRN_WIDTH), mats, masks, signs)
    return _mixer_call(_hgrn_kernel, "hgrn2", in_specs, args, nb, HGRN_HEADS, lat_len, ctx_len, HGRN_DV,
                       scratch, emit_ctx, 48)


def _mlstm_kernel(ql, qc, kl, kc, vl, vc, ogl, ogc, gcl, gcc, grl, grc, bc_ref, br_ref,
                  wq_ref, wk_ref, gn_ref, ol_ref, oc_ref,
                  q_s, gr_s, intra_s, kv_s, call_s, rho_s, bcol_s, bl_s, mu_s, mprev_s, st_s, m_s,
                  *, n_ctx, n_lat):
    nch = n_ctx + n_lat
    unroll_a, unroll_c = MLSTM_UNROLL
    h = pl.program_id(1)
    nh = MLSTM_HEADS
    dk, dv = MLSTM_DK, MLSTM_DV
    ext = 2 * dv
    qr, kr, vr, ogr, gcr, outr = (qc, ql), (kc, kl), (vc, vl), (ogc, ogl), (gcc, gcl), (oc_ref, ol_ref)
    n_loc = (n_ctx, n_lat)

    for cc in range(nch):
        src, c0 = (grc, cc) if cc < n_ctx else (grl, cc - n_ctx)
        gr_s[cc] = src[0, :, c0 * CHUNK:(c0 + 1) * CHUNK] + br_ref[...]

    row = lax.broadcasted_iota(jnp.int32, (CHUNK, dk), 0)
    lane = lax.broadcasted_iota(jnp.int32, (CHUNK, dv), 1)
    del lane
    ones_col = jnp.ones((CHUNK, dv), BF16)
    scale = dk ** -0.5
    t_i = lax.broadcasted_iota(jnp.int32, (CHUNK, CHUNK), 0)
    s_i = lax.broadcasted_iota(jnp.int32, (CHUNK, CHUNK), 1)
    lane16 = lax.broadcasted_iota(jnp.int32, (CHUNK, MLSTM_GATES), 1)
    sub16 = lax.broadcasted_iota(jnp.int32, (MLSTM_GATES, CHUNK), 0)

    def conv(src, w_ref, part, cl):
        r0 = pl.multiple_of(cl * CHUNK, CHUNK)
        n_rows = n_loc[part] * CHUNK
        x = src[pl.ds(r0, CHUNK), :]
        pr = src[pl.ds(jnp.maximum(r0 - 1, 0), 1), :]
        nx = src[pl.ds(jnp.minimum(r0 + CHUNK, n_rows - 1), 1), :]
        pr = jnp.where(cl != 0, pr, 0.0)
        nx = jnp.where(cl != n_loc[part] - 1, nx, 0.0)
        xp = jnp.where(row == 0, pr, pltpu.roll(x, 1, 0))
        xn = jnp.where(row == CHUNK - 1, nx, pltpu.roll(x, CHUNK - 1, 0))
        w = w_ref[...]
        return _silu(w[0:1] * xp + w[1:2] * x + w[2:3] * xn)

    def pick_col(g, j):
        return jnp.sum(jnp.where(lane16 == j, g, 0.0), axis=1, keepdims=True)

    def pick_row(g, j):
        return jnp.sum(jnp.where(sub16 == j, g, 0.0), axis=0, keepdims=True)

    def phase_a(part, cl, cg):
        rl, rg = _rows(cl), _rows(cg)
        q = conv(qr[part], wq_ref, part, cl)
        k = conv(kr[part], wk_ref, part, cl) * scale
        qb = _bf(q)
        q_s[rg, :] = qb
        vb = jnp.concatenate([_bf(vr[part][rl, :]), ones_col], axis=1)
        s = _dot_nt(qb, _bf(k))
        g_c = gcr[part][rl, :] + bc_ref[...]
        g_r = gr_s[cg]
        kws = []
        for d in range(2):
            ig_c = pick_col(g_c, d * nh + h)
            lf_c = _log_sigmoid(pick_col(g_c, 2 * nh + d * nh + h))
            ig_r = pick_row(g_r, d * nh + h)
            lf_r = _log_sigmoid(pick_row(g_r, 2 * nh + d * nh + h))
            tri = (s_i <= t_i) if d == 0 else (s_i >= t_i)
            tri_t = (t_i <= s_i) if d == 0 else (t_i >= s_i)
            b_c = jnp.sum(jnp.where(tri, lf_r, 0.0), axis=1, keepdims=True)
            b_r = jnp.sum(jnp.where(tri_t, lf_c, 0.0), axis=0, keepdims=True)
            dmat = jnp.where(tri, b_c - b_r + ig_r, -jnp.inf)
            rho = jnp.max(dmat, axis=1, keepdims=True)
            intra_s[d, rg, :] = _dot(_bf(s * jnp.exp(dmat - rho)), vb)
            b_last = b_c[CHUNK - 1:CHUNK] if d == 0 else b_c[0:1]
            mu = jnp.max(b_last - b_r + ig_r, axis=1, keepdims=True)
            kws.append(_bf(k * jnp.exp(b_last - b_c + ig_c - mu)))
            rho_s[d, rg, :] = jnp.broadcast_to(rho, (CHUNK, 128))
            bcol_s[d, rg, :] = jnp.broadcast_to(b_c, (CHUNK, 128))
            bl_s[cg, :, _lanes(d, 128)] = jnp.broadcast_to(b_last, (1, 128))
            mu_s[cg, :, _lanes(d, 128)] = jnp.broadcast_to(mu, (1, 128))
        kv_s[cg] = _dot_tn(jnp.concatenate(kws, axis=1), vb)

    _for_chunks(n_ctx, n_lat, phase_a, unroll_a)
    st_s[...] = jnp.zeros(st_s.shape, F32)
    m_s[...] = jnp.zeros(m_s.shape, F32)

    def wide(v):
        return jnp.concatenate([v, v], axis=1)

    def phase_b(i, carry):
        for d, c in ((0, i), (1, _bwd_chunk(i, n_ctx, n_lat))):
            ln = _lanes(d, 128)
            m_prev = m_s[d]
            st = st_s[d]
            mprev_s[c, :, ln] = m_prev
            call_s[c, :, _lanes(d, ext)] = _bf(st)
            bl = bl_s[c, :, ln]
            mu = mu_s[c, :, ln]
            m_new = jnp.maximum(bl + m_prev, mu)
            st_s[d] = (wide(jnp.exp(bl + m_prev - m_new)) * st
                       + wide(jnp.exp(mu - m_new)) * kv_s[c, _lanes(d, dk), :])
            m_s[d] = m_new
        return carry

    lax.fori_loop(0, nch, phase_b, 0)
    gn = gn_ref[...]

    def phase_c(part, cl, cg):
        rl, rg = _rows(cl), _rows(cg)
        qc_all = _dot(q_s[rg, :], call_s[cg])
        o = None
        for d in range(2):
            rho = rho_s[d, rg, :]
            b_c = bcol_s[d, rg, :]
            m_prev = mprev_s[cg, :, _lanes(d, 128)]
            m_t = jnp.maximum(rho, b_c + m_prev)
            nd = (wide(jnp.exp(rho - m_t)) * intra_s[d, rg, :]
                  + wide(jnp.exp(b_c + m_prev - m_t)) * qc_all[:, _lanes(d, ext)])
            hh = nd[:, 0:dv] / jnp.maximum(jnp.abs(nd[:, dv:]), jnp.exp(-m_t))
            o = hh if o is None else o + hh
        y = o - jnp.mean(o, axis=-1, keepdims=True)
        y = y * lax.rsqrt(jnp.mean(y * y, axis=-1, keepdims=True) + NORM_EPS)
        outr[part][rl, :] = (y * gn * _sigmoid(ogr[part][rl, :])).astype(BF16)

    _for_chunks(n_ctx, n_lat, phase_c, unroll_c, with_ctx=oc_ref is not None)


def _mlstm(p, g_col, g_row_lat, g_row_ctx, nb, lat_len, ctx_len, gate_b, conv_w, gn, emit_ctx):
    t_len = lat_len + ctx_len
    nch = t_len // CHUNK
    n_lat_blk = nb * lat_len // ctx_len
    ng = MLSTM_GATES
    dk, dv = MLSTM_DK, MLSTM_DV
    spec = functools.partial(_seq_specs, width=128, nb=nb, lat_len=lat_len, ctx_len=ctx_len)
    in_specs = [*spec(_C_ML_Q), *spec(_C_ML_K), *spec(_C_ML_V), *spec(_C_ML_O),
                pl.BlockSpec((lat_len, ng), lambda b, h: (b, 0)),
                pl.BlockSpec((ctx_len, ng), lambda b, h: (n_lat_blk + b, 0)),
                pl.BlockSpec((1, ng, lat_len), lambda b, h: (b, 0, 0)),
                pl.BlockSpec((1, ng, ctx_len), lambda b, h: (b, 0, 0)),
                pl.BlockSpec((1, ng), lambda b, h: (0, 0)),
                pl.BlockSpec((ng, 1), lambda b, h: (0, 0)),
                pl.BlockSpec((3, 128), lambda b, h: (0, h)),
                pl.BlockSpec((3, 128), lambda b, h: (0, MLSTM_HEADS + h)),
                pl.BlockSpec((1, 128), lambda b, h: (0, h))]
    scratch = [pltpu.VMEM((t_len, dk), BF16),
               pltpu.VMEM((nch, ng, CHUNK), F32),
               pltpu.VMEM((2, t_len, 2 * dv), F32),
               pltpu.VMEM((nch, 2 * dk, 2 * dv), F32),
               pltpu.VMEM((nch, dk, 4 * dv), BF16),
               pltpu.VMEM((2, t_len, 128), F32), pltpu.VMEM((2, t_len, 128), F32),
               pltpu.VMEM((nch, 1, 256), F32), pltpu.VMEM((nch, 1, 256), F32), pltpu.VMEM((nch, 1, 256), F32),
               pltpu.VMEM((2, dk, 2 * dv), F32), pltpu.VMEM((2, 1, 128), F32)]
    args = (p, p, p, p, p, p, p, p, g_col, g_col, g_row_lat, g_row_ctx,
            gate_b.reshape(1, ng), gate_b.reshape(ng, 1), conv_w, conv_w, gn.reshape(1, MLSTM_WIDTH))
    return _mixer_call(_mlstm_kernel, "mlstm", in_specs, args, nb, MLSTM_HEADS, lat_len, ctx_len, dv,
                       scratch, emit_ctx, 48)


def _wout_kernel(*refs, n_lat_tiles, n_mix_parts, n_x_parts, emit_next):
    it = iter(refs)
    mix = [[next(it) for _ in range(n_mix_parts)] for _ in range(3)]
    x_parts = [next(it) for _ in range(n_x_parts)]
    wr_ref, wh_ref, wm_ref, g_ref, mod_ref = (next(it) for _ in range(5))
    g2_ref = next(it) if emit_next else None
    o_ref = next(it)
    h2_ref = next(it) if emit_next else None

    def run(part):
        pm = min(part, n_mix_parts - 1)
        y = (_dot(mix[0][pm][...], wr_ref[...]) + _dot(mix[1][pm][...], wh_ref[...])
             + _dot(mix[2][pm][...], wm_ref[...]))
        m = mod_ref[0]
        xn = x_parts[min(part, n_x_parts - 1)][...] + m[2:3] * _rms(y, g_ref[...])
        o_ref[...] = xn
        if emit_next:
            h2_ref[...] = (_rms(xn, g2_ref[...]) * (1.0 + m[4:5]) + m[3:4]).astype(h2_ref.dtype)

    _on_part(n_lat_tiles, max(n_mix_parts, n_x_parts), run)


def _wout(mix_parts, w_out, x_parts, n_rows, g1, modtab, lat_rows, g2_next):
    d = x_parts[0].shape[1]
    nb = modtab.shape[0] - 1
    mi = _mod_index(ROW_TILE, lat_rows, nb)
    tm = ROW_TILE
    n_lat_tiles = mix_parts[0][0].shape[0] // tm
    emit_next = g2_next is not None
    widths = (RET_WIDTH, HGRN_WIDTH, MLSTM_WIDTH)
    w_r = w_out[0:RET_WIDTH]
    w_h = w_out[RET_WIDTH:RET_WIDTH + HGRN_WIDTH]
    w_m = w_out[RET_WIDTH + HGRN_WIDTH:]
    full = lambda a: pl.BlockSpec(a.shape, lambda i: (0, 0))
    in_specs, args = [], []
    for parts, w in zip(mix_parts, widths):
        in_specs += _part_specs(parts, tm, w)
        args += list(parts)
    in_specs += _part_specs(x_parts, tm, d)
    args += list(x_parts)
    in_specs += [full(w_r), full(w_h), full(w_m),
                 pl.BlockSpec((1, d), lambda i: (0, 0)),
                 pl.BlockSpec((1, 6, d), lambda i: (mi(i), 0, 0))]
    args += [w_r, w_h, w_m, g1.reshape(1, d), modtab]
    out_specs = [pl.BlockSpec((tm, d), lambda i: (i, 0))]
    out_shape = [jax.ShapeDtypeStruct((n_rows, d), F32)]
    if emit_next:
        in_specs.append(pl.BlockSpec((1, d), lambda i: (0, 0)))
        args.append(g2_next.reshape(1, d))
        out_specs.append(pl.BlockSpec((tm, d), lambda i: (i, 0)))
        out_shape.append(jax.ShapeDtypeStruct((n_rows, d), BF16))
    return pl.pallas_call(
        functools.partial(_wout_kernel, n_lat_tiles=n_lat_tiles, n_mix_parts=len(mix_parts[0]),
                          n_x_parts=len(x_parts), emit_next=emit_next),
        grid=(n_rows // tm,),
        in_specs=in_specs,
        out_specs=out_specs,
        out_shape=out_shape,
        compiler_params=_cparams(("arbitrary",), 52),
        name="wout",
    )(*args)


def _ffn_kernel(*refs, emit_next):
    if emit_next:
        h_ref, w1_ref, w3_ref, w2_ref, x_ref, g_ref, mod_ref, gn_ref, modn_ref, o_ref, hn_ref, acc_ref = refs
    else:
        h_ref, w1_ref, w3_ref, w2_ref, x_ref, g_ref, mod_ref, o_ref, acc_ref = refs
    f = pl.program_id(1)

    @pl.when(f == 0)
    def _():
        acc_ref[...] = jnp.zeros(acc_ref.shape, F32)

    h = h_ref[...]
    u = _silu(_dot(h, w1_ref[0])) * _dot(h, w3_ref[0])
    acc_ref[...] += _dot(_bf(u), w2_ref[...])

    @pl.when(f == pl.num_programs(1) - 1)
    def _():
        m = mod_ref[0]
        xn = x_ref[...] + m[5:6] * _rms(acc_ref[...], g_ref[...])
        o_ref[...] = xn
        if emit_next:
            mn = modn_ref[0]
            hn_ref[...] = (_rms(xn, gn_ref[...]) * (1.0 + mn[1:2]) + mn[0:1]).astype(hn_ref.dtype)


def _col_blocks(w, tf):
    d, n = w.shape
    return jnp.transpose(w.reshape(d, n // tf, tf), (1, 0, 2))


def _ffn(hb, w1, w3, w2, xs, g3, modtab, lat_rows, g_next, modtab_next):
    n_rows, d = hb.shape
    nf, _, tf = w1.shape
    dff = nf * tf
    nb = modtab.shape[0] - 1
    tm = ROW_TILE
    mi = _mod_index(tm, lat_rows, nb)
    emit_next = g_next is not None

    def fs(i, f):
        return jnp.where(i % 2 == 1, nf - 1 - f, f)

    in_specs = [pl.BlockSpec((tm, d), lambda i, f: (i, 0)),
                pl.BlockSpec((1, d, tf), lambda i, f: (fs(i, f), 0, 0)),
                pl.BlockSpec((1, d, tf), lambda i, f: (fs(i, f), 0, 0)),
                pl.BlockSpec((tf, d), lambda i, f: (fs(i, f), 0)),
                pl.BlockSpec((tm, d), lambda i, f: (i, 0)),
                pl.BlockSpec((1, d), lambda i, f: (0, 0)),
                pl.BlockSpec((1, 6, d), lambda i, f: (mi(i), 0, 0))]
    args = [hb, w1, w3, w2, xs, g3.reshape(1, d), modtab]
    out_specs = [pl.BlockSpec((tm, d), lambda i, f: (i, 0))]
    out_shape = [jax.ShapeDtypeStruct((n_rows, d), F32)]
    if emit_next:
        in_specs += [pl.BlockSpec((1, d), lambda i, f: (0, 0)),
                     pl.BlockSpec((1, 6, d), lambda i, f: (mi(i), 0, 0))]
        args += [g_next.reshape(1, d), modtab_next]
        out_specs.append(pl.BlockSpec((tm, d), lambda i, f: (i, 0)))
        out_shape.append(jax.ShapeDtypeStruct((n_rows, d), BF16))
    return pl.pallas_call(
        functools.partial(_ffn_kernel, emit_next=emit_next),
        grid=(n_rows // tm, dff // tf),
        in_specs=in_specs,
        out_specs=out_specs,
        out_shape=out_shape,
        scratch_shapes=[pltpu.VMEM((tm, d), F32)],
        compiler_params=_cparams(("arbitrary", "arbitrary"), 48),
        name="ffn",
    )(*args)


_META_E0, _META_E1, _META_R0, _META_R1, _META_G0, _META_G1 = range(6)


def _router_kernel(x_ref, g_ref, mod_ref, wr_ref, tri_ref, meta_ref, cnt_ref, carry_ref):
    i = pl.program_id(0)

    @pl.when(i == 0)
    def _():
        carry_ref[...] = jnp.zeros(carry_ref.shape, F32)

    m = mod_ref[0]
    hmod = _rms(x_ref[...], g_ref[...]) * (1.0 + m[4:5]) + m[3:4]
    logits = jnp.dot(hmod, wr_ref[...], precision=lax.Precision.HIGHEST, preferred_element_type=F32)
    lane = lax.broadcasted_iota(jnp.int32, logits.shape, 1)
    lanef = lane.astype(F32)
    logits = jnp.where(lane < N_EXPERTS, logits, -jnp.inf)
    v0 = jnp.max(logits, axis=1, keepdims=True)
    e0 = jnp.min(jnp.where(logits == v0, lanef, 1e9), axis=1, keepdims=True)
    rest = jnp.where(lanef == e0, -jnp.inf, logits)
    v1 = jnp.max(rest, axis=1, keepdims=True)
    e1 = jnp.min(jnp.where(rest == v1, lanef, 1e9), axis=1, keepdims=True)
    ex = jnp.exp(v1 - v0)
    g0 = 1.0 / (1.0 + ex)
    g1 = ex / (1.0 + ex)
    oh0 = lanef == e0
    oh1 = lanef == e1
    oh = jnp.where(jnp.logical_or(oh0, oh1), 1.0, 0.0)
    before = _dot(tri_ref[...], _bf(oh)) + carry_ref[0:1, :]
    r0 = jnp.sum(jnp.where(oh0, before, 0.0), axis=1, keepdims=True)
    r1 = jnp.sum(jnp.where(oh1, before, 0.0), axis=1, keepdims=True)
    carry_ref[0:1, :] = carry_ref[0:1, :] + jnp.sum(oh, axis=0, keepdims=True)
    meta = jnp.zeros(logits.shape, F32)
    for j, val in ((_META_E0, e0), (_META_E1, e1), (_META_R0, r0), (_META_R1, r1),
                   (_META_G0, g0), (_META_G1, g1)):
        meta = jnp.where(lane == j, val, meta)
    meta_ref[...] = meta
    cnt_ref[...] = carry_ref[...]


def _router(xs, n_rows, g2, modtab, w_router, lat_rows):
    d = xs.shape[1]
    nb = modtab.shape[0] - 1
    tm = ROW_TILE
    mi = _mod_index(tm, lat_rows, nb)
    wr = jnp.zeros((d, 128), F32).at[:, :N_EXPERTS].set(w_router)
    tri = jnp.asarray(np.tril(np.ones((tm, tm), np.float32), -1), BF16)
    return pl.pallas_call(
        _router_kernel,
        grid=(n_rows // tm,),
        in_specs=[pl.BlockSpec((tm, d), lambda i: (i, 0)),
                  pl.BlockSpec((1, d), lambda i: (0, 0)),
                  pl.BlockSpec((1, 6, d), lambda i: (mi(i), 0, 0)),
                  pl.BlockSpec((d, 128), lambda i: (0, 0)),
                  pl.BlockSpec((tm, tm), lambda i: (0, 0))],
        out_specs=[pl.BlockSpec((tm, 128), lambda i: (i, 0)),
                   pl.BlockSpec((8, 128), lambda i: (0, 0))],
        out_shape=[jax.ShapeDtypeStruct((n_rows, 128), F32),
                   jax.ShapeDtypeStruct((8, 128), F32)],
        scratch_shapes=[pltpu.VMEM((8, 128), F32)],
        compiler_params=_cparams(("arbitrary",), 32),
        name="router",
    )(xs, g2.reshape(1, d), modtab, wr, tri)


def _dispatch_kernel(dest_ref, x_ref, g_ref, mod_ref, init_ref, o_ref, h_s, sem):
    del init_ref
    i = pl.program_id(0)
    tm = h_s.shape[0]
    m = mod_ref[0]
    h_s[...] = _rms(x_ref[...], g_ref[...]) * (1.0 + m[4:5]) + m[3:4]

    def row_copy(r, k):
        dst = dest_ref[TOP_K * (i * tm + r) + k]
        return pltpu.make_async_copy(h_s.at[pl.ds(r, 1)], o_ref.at[pl.ds(dst, 1)], sem)

    def start(r, carry):
        for k in range(TOP_K):
            row_copy(r, k).start(priority=k)
        return carry

    lax.fori_loop(0, tm, start, 0, unroll=DMA_UNROLL)

    def wait(r, carry):
        for k in range(TOP_K):
            row_copy(r, k).wait()
        return carry

    lax.fori_loop(0, tm, wait, 0, unroll=DMA_UNROLL)


def _dispatch(dest, xs, n_rows, g2, modtab, n_slots, lat_rows):
    d = xs.shape[1]
    nb = modtab.shape[0] - 1
    tm = ROW_TILE
    mi = _mod_index(tm, lat_rows, nb)
    init = jnp.zeros((n_slots, d), F32)
    grid_spec = pltpu.PrefetchScalarGridSpec(
        num_scalar_prefetch=1,
        grid=(n_rows // tm,),
        in_specs=[pl.BlockSpec((tm, d), lambda i, dst: (i, 0)),
                  pl.BlockSpec((1, d), lambda i, dst: (0, 0)),
                  pl.BlockSpec((1, 6, d), lambda i, dst: (mi(i), 0, 0)),
                  pl.BlockSpec(memory_space=pl.ANY)],
        out_specs=pl.BlockSpec(memory_space=pl.ANY),
        scratch_shapes=[pltpu.VMEM((tm, d), F32), pltpu.SemaphoreType.DMA(())],
    )
    return pl.pallas_call(
        _dispatch_kernel,
        grid_spec=grid_spec,
        out_shape=jax.ShapeDtypeStruct((n_slots, d), F32),
        input_output_aliases={4: 0},
        compiler_params=_cparams(("arbitrary",), 32),
        name="dispatch",
    )(dest, xs, g2.reshape(1, d), modtab, init)


def _expert_kernel(ge_ref, nt_ref, ng_ref, x_ref, w1_ref, w3_ref, w2_ref, o_ref, xb_s):
    s = pl.program_id(0)
    f = pl.program_id(1)
    tm = MOE_TM

    @pl.when(s < ng_ref[0])
    def _():
        @pl.when(f == 0)
        def _():
            xb_s[...] = _bf(x_ref[...])
            o_ref[...] = jnp.zeros(o_ref.shape, F32)

        w1b, w3b, w2b = _bf(w1_ref[0]), _bf(w3_ref[0]), _bf(w2_ref[0])

        def tile(t):
            rows = pl.ds(t * tm, tm)
            h = xb_s[rows, :]
            u = _silu(_dot(h, w1b)) * _dot(h, w3b)
            o_ref[rows, :] += _dot(_bf(u), w2b)

        tile(0)
        for t in range(1, MOE_GROUP):
            pl.when(nt_ref[s] > t)(functools.partial(tile, t))

    @pl.when(jnp.logical_and(s >= ng_ref[0], f == pl.num_programs(1) - 1))
    def _():
        o_ref[...] = jnp.zeros(o_ref.shape, F32)


def _experts(group_e, group_tiles, n_groups, xsort, w1, w3, w2):
    n_slots, d = xsort.shape
    dff = w1.shape[2]
    tg, tf = MOE_TM * MOE_GROUP, MOE_TF
    nf = dff // tf

    def ss(s, ng):
        return jnp.maximum(jnp.minimum(s, ng[0] - 1), 0)

    def ff(s, f, ng):
        snake = jnp.where(s % 2 == 1, nf - 1 - f, f)
        last = jnp.where((ng[0] - 1) % 2 == 1, 0, nf - 1)
        return jnp.where(s < ng[0], snake, last)

    grid_spec = pltpu.PrefetchScalarGridSpec(
        num_scalar_prefetch=3,
        grid=(n_slots // tg, nf),
        in_specs=[pl.BlockSpec((tg, d), lambda s, f, ge, nt, ng: (ss(s, ng), 0)),
                  pl.BlockSpec((1, d, tf), lambda s, f, ge, nt, ng: (ge[ss(s, ng)], 0, ff(s, f, ng))),
                  pl.BlockSpec((1, d, tf), lambda s, f, ge, nt, ng: (ge[ss(s, ng)], 0, ff(s, f, ng))),
                  pl.BlockSpec((1, tf, d), lambda s, f, ge, nt, ng: (ge[ss(s, ng)], ff(s, f, ng), 0))],
        out_specs=pl.BlockSpec((tg, d), lambda s, f, ge, nt, ng: (s, 0)),
        scratch_shapes=[pltpu.VMEM((tg, d), BF16)],
    )
    return pl.pallas_call(
        _expert_kernel,
        grid_spec=grid_spec,
        out_shape=jax.ShapeDtypeStruct((n_slots, d), F32),
        compiler_params=_cparams(("arbitrary", "arbitrary"), 52),
        name="experts",
    )(group_e, group_tiles, n_groups, xsort, w1, w3, w2)


def _combine_kernel(dest_ref, y_ref, meta_ref, x_ref, g_ref, mod_ref, o_ref, buf_s, sem):
    i = pl.program_id(0)
    tm = x_ref.shape[0]

    def row_copy(r, k):
        src = dest_ref[TOP_K * (i * tm + r) + k]
        return pltpu.make_async_copy(y_ref.at[pl.ds(src, 1)], buf_s.at[k, pl.ds(r, 1)], sem)

    def start(r, carry):
        for k in range(TOP_K):
            row_copy(r, k).start(priority=k)
        return carry

    lax.fori_loop(0, tm, start, 0, unroll=DMA_UNROLL)

    def wait(r, carry):
        for k in range(TOP_K):
            row_copy(r, k).wait()
        return carry

    lax.fori_loop(0, tm, wait, 0, unroll=DMA_UNROLL)

    meta = meta_ref[...]
    lane = lax.broadcasted_iota(jnp.int32, meta.shape, 1)
    g0 = jnp.sum(jnp.where(lane == _META_G0, meta, 0.0), axis=1, keepdims=True)
    g1 = jnp.sum(jnp.where(lane == _META_G1, meta, 0.0), axis=1, keepdims=True)
    y = buf_s[0] * g0 + buf_s[1] * g1
    m = mod_ref[0]
    o_ref[...] = x_ref[...] + m[5:6] * _rms(y, g_ref[...])


def _combine(dest, yb, meta, xs, n_rows, g3, modtab, lat_rows):
    d = xs.shape[1]
    nb = modtab.shape[0] - 1
    tm = ROW_TILE
    mi = _mod_index(tm, lat_rows, nb)
    grid_spec = pltpu.PrefetchScalarGridSpec(
        num_scalar_prefetch=1,
        grid=(n_rows // tm,),
        in_specs=[pl.BlockSpec(memory_space=pl.ANY),
                  pl.BlockSpec((tm, 128), lambda i, dst: (i, 0)),
                  pl.BlockSpec((tm, d), lambda i, dst: (i, 0)),
                  pl.BlockSpec((1, d), lambda i, dst: (0, 0)),
                  pl.BlockSpec((1, 6, d), lambda i, dst: (mi(i), 0, 0))],
        out_specs=pl.BlockSpec((tm, d), lambda i, dst: (i, 0)),
        scratch_shapes=[pltpu.VMEM((TOP_K, tm, d), F32), pltpu.SemaphoreType.DMA(())],
    )
    return pl.pallas_call(
        _combine_kernel,
        grid_spec=grid_spec,
        out_shape=jax.ShapeDtypeStruct((n_rows, d), F32),
        compiler_params=_cparams(("arbitrary",), 40),
        name="combine",
    )(dest, yb, meta, xs, g3.reshape(1, d), modtab)


def _moe(xs, n_rows, g2, g3, modtab, w_router, w1, w3, w2, lat_rows):
    meta, cnt = _router(xs, n_rows, g2, modtab, w_router, lat_rows)
    counts = cnt[0, :N_EXPERTS].astype(jnp.int32)
    tg = MOE_TM * MOE_GROUP
    padded = (counts + tg - 1) // tg * tg
    pad_end = jnp.cumsum(padded)
    pad_start = pad_end - padded
    e = meta[:, _META_E0:_META_E1 + 1].astype(jnp.int32)
    r = meta[:, _META_R0:_META_R1 + 1].astype(jnp.int32)
    dest = (pad_start[e] + r).reshape(-1)
    n_groups_max = (n_rows * TOP_K) // tg + N_EXPERTS
    n_slots = n_groups_max * tg
    group_start = jnp.arange(n_groups_max, dtype=jnp.int32) * tg
    group_e = jnp.minimum(jnp.sum((group_start[:, None] >= pad_end[None, :]).astype(jnp.int32), axis=1),
                          N_EXPERTS - 1)
    filled = counts[group_e] - (group_start - pad_start[group_e])
    group_tiles = jnp.clip((filled + MOE_TM - 1) // MOE_TM, 1, MOE_GROUP).astype(jnp.int32)
    n_groups = (pad_end[-1:] // tg).astype(jnp.int32)
    xsort = _dispatch(dest, xs, n_rows, g2, modtab, n_slots, lat_rows)
    yb = _experts(group_e, group_tiles, n_groups, xsort, w1, w3, w2)
    return _combine(dest, yb, meta, xs, n_rows, g3, modtab, lat_rows)


def _rotary_tables(lat_len, ctx_len):
    rows = lat_len // GRID_W
    row = jnp.repeat(jnp.arange(rows, dtype=F32), GRID_W)
    col = jnp.tile(jnp.arange(GRID_W, dtype=F32), rows)
    n_freq = RET_DK // 4
    inv = ROPE_BASE ** (-jnp.arange(n_freq, dtype=F32) / n_freq)
    ang = jnp.concatenate([row[:, None] * inv, col[:, None] * inv], axis=-1)
    cos, sin = jnp.cos(ang), jnp.sin(ang)
    cosf = jnp.concatenate([jnp.ones((ctx_len, RET_DK), F32), jnp.concatenate([cos, cos], -1)], 0)
    sinf = jnp.concatenate([jnp.zeros((ctx_len, RET_DK), F32), jnp.concatenate([-sin, sin], -1)], 0)
    return cosf, sinf


def kernel(x, c, ctx, c_ctx, w_ada, b_ada, norm_g, w_in, w_out, ret_decay, ret_gn, hgrn_lb, hgrn_gn,
           mlstm_conv, mlstm_gate_b, mlstm_gn, w_ffn1, w_ffn3, w_ffn2, w_router, w_exp1, w_exp3, w_exp2):
    nb, lat_len, d = x.shape
    ctx_len = ctx.shape[1]
    depth = w_ada.shape[0]
    n_lat_rows = nb * lat_len
    n_rows = n_lat_rows + nb * ctx_len

    x_parts = [x.reshape(n_lat_rows, d), ctx.reshape(nb * ctx_len, d)]
    w_in_t = jnp.swapaxes(w_in, 1, 2)
    s_in = jnp.zeros((8, d), F32).at[:nb].set(c).at[nb].set(c_ctx)
    mod_all = _ada(s_in, w_ada, b_ada)[:, :nb + 1].reshape(depth, nb + 1, 6, d)

    cosf, sinf = _rotary_tables(lat_len, ctx_len)
    hg_cum, hg_masks, hg_signs = _hgrn_constants()
    hg_consts = (jnp.asarray(hg_cum, BF16), jnp.asarray(hg_masks, F32), jnp.asarray(hg_signs, F32))
    sm = jax.nn.softmax(hgrn_lb.astype(F32), axis=0)
    lb_all = jnp.clip(jnp.cumsum(sm, axis=0) - sm[0], 0.0, 1.0)

    hb = None
    for layer in range(depth):
        last = layer == depth - 1
        modtab = mod_all[layer]
        g = norm_g[layer]

        if hb is None:
            hb = _prenorm(x_parts, n_rows, g[0], modtab, 1, 0, lat_len, BF16)
        p = _mm(hb, w_in_t, layer, MM_TM, MM_TN, n_cols=PROJ_MAIN)
        w_g = jnp.zeros((1, 128, d), F32).at[0, :MLSTM_GATES].set(w_in_t[layer, PROJ_MAIN:])
        gates = _mm(hb, w_g, 0, MM_TM, 128)[:, :MLSTM_GATES]
        hb = None
        g_row_lat = jnp.swapaxes(gates[:n_lat_rows].reshape(nb, lat_len, MLSTM_GATES), 1, 2)
        g_row_ctx = jnp.swapaxes(gates[n_lat_rows:].reshape(nb, ctx_len, MLSTM_GATES), 1, 2)

        log_g = jax.nn.log_sigmoid(ret_decay[layer].astype(F32))
        lg_b = jnp.broadcast_to(log_g[:, :, None, None], (2, RET_HEADS, 8, RET_DV))
        lb = lb_all[layer]
        emit_ctx = not last
        o_ret = _retention(p, nb, lat_len, ctx_len, cosf, sinf, lg_b, ret_gn[layer], emit_ctx)
        o_hg = _hgrn(p, nb, lat_len, ctx_len, jnp.log(lb), jnp.log1p(-lb), hgrn_gn[layer],
                     hg_consts, emit_ctx)
        o_ml = _mlstm(p, gates, g_row_lat, g_row_ctx, nb, lat_len, ctx_len, mlstm_gate_b[layer],
                      mlstm_conv[layer], mlstm_gn[layer], emit_ctx)
        mix_parts = [list(o) for o in (o_ret, o_hg, o_ml)]
        rows_now = n_lat_rows if last else n_rows
        dense = layer % 2 == 0
        res = _wout(mix_parts, _bf(w_out[layer]), x_parts, rows_now, g[1], modtab, lat_len,
                    g[2] if dense else None)
        xs = res[0]
        x_parts = [xs]

        j = layer // 2
        if dense:
            nxt = (None, None) if last else (norm_g[layer + 1][0], mod_all[layer + 1])
            res = _ffn(res[1], _col_blocks(_bf(w_ffn1[j]), FFN_TF), _col_blocks(_bf(w_ffn3[j]), FFN_TF),
                       _bf(w_ffn2[j]), xs, g[3], modtab, lat_len, *nxt)
            xs = res[0]
            hb = None if last else res[1]
        else:
            xs = _moe(xs, rows_now, g[2], g[3], modtab, w_router[j], w_exp1[j], w_exp3[j], w_exp2[j], lat_len)
        x_parts = [xs]
    return xs[:n_lat_rows].reshape(nb, lat_len, d)
```

```python
import functools

import numpy as np
import jax
import jax.numpy as jnp
from jax import lax
from jax.experimental import pallas as pl
from jax.experimental.pallas import tpu as pltpu

F32 = jnp.float32
BF16 = jnp.bfloat16

CHUNK = 128
NORM_EPS = 1e-6
ROPE_BASE = 10000.0
GRID_W = 64

RET_HEADS, RET_DK, RET_DV = 4, 128, 256
HGRN_HEADS, HGRN_DK, HGRN_DV = 4, 128, 128
MLSTM_HEADS, MLSTM_DK, MLSTM_DV = 4, 128, 128
N_EXPERTS = 8
TOP_K = 2

RET_QK = RET_HEADS * RET_DK
RET_WIDTH = RET_HEADS * RET_DV
HGRN_QK = HGRN_HEADS * HGRN_DK
HGRN_WIDTH = HGRN_HEADS * HGRN_DV
MLSTM_QK = MLSTM_HEADS * MLSTM_DK
MLSTM_WIDTH = MLSTM_HEADS * MLSTM_DV
MLSTM_GATES = 4 * MLSTM_HEADS

_C_RET_Q = 0
_C_RET_K = _C_RET_Q + RET_QK // 128
_C_RET_V = _C_RET_K + RET_QK // 128
_C_RET_G = _C_RET_V + RET_WIDTH // 128
_C_HG_Q = _C_RET_G + RET_WIDTH // 128
_C_HG_FF = _C_HG_Q + HGRN_QK // 128
_C_HG_FB = _C_HG_FF + HGRN_QK // 128
_C_HG_I = _C_HG_FB + HGRN_QK // 128
_C_HG_G = _C_HG_I + HGRN_WIDTH // 128
_C_ML_Q = _C_HG_G + HGRN_WIDTH // 128
_C_ML_K = _C_ML_Q + MLSTM_QK // 128
_C_ML_V = _C_ML_K + MLSTM_QK // 128
_C_ML_O = _C_ML_V + MLSTM_WIDTH // 128
PROJ_MAIN = (_C_ML_O + MLSTM_WIDTH // 128) * 128

_VMEM_CAP_BYTES = 56 * 1024 * 1024

ROW_TILE = 512
MM_TM, MM_TN = 1024, 768
FFN_TF = 512
MOE_TM = 768
MOE_GROUP = 1
MOE_TF = 512
RET_UNROLL = (4, 4)
HGRN_UNROLL = (2, 4)
MLSTM_UNROLL = (2, 2)
DMA_UNROLL = 8


def _cparams(sem, vmem_mb):
    return pltpu.CompilerParams(dimension_semantics=sem,
                                vmem_limit_bytes=min(int(vmem_mb * 1024 * 1024), _VMEM_CAP_BYTES))


def _bf(x):
    return x.astype(BF16)


def _dot(a, b):
    return jnp.dot(a, b, preferred_element_type=F32)


def _dot_nt(a, b):
    return lax.dot_general(a, b, (((1,), (1,)), ((), ())), preferred_element_type=F32)


def _dot_tn(a, b):
    return lax.dot_general(a, b, (((0,), (0,)), ((), ())), preferred_element_type=F32)


def _sigmoid(x):
    return 1.0 / (1.0 + jnp.exp(-x))


def _silu(x):
    return x * _sigmoid(x)


def _log_sigmoid(x):
    return jnp.minimum(x, 0.0) - jnp.log(1.0 + jnp.exp(-jnp.abs(x)))


def _rms(x, g):
    return x * lax.rsqrt(jnp.mean(x * x, axis=-1, keepdims=True) + NORM_EPS) * g


def _mod_index(tile_rows, n_lat_rows_per_batch, n_batch):
    return lambda i: jnp.minimum((i * tile_rows) // n_lat_rows_per_batch, n_batch)


def _ada_kernel(s_ref, w_ref, b_ref, o_ref):
    s = _bf(_silu(s_ref[...]))
    o_ref[0] = _dot(s, _bf(w_ref[0])) + b_ref[0]


def _ada(s_in, w_ada, b_ada):
    depth, d, n = w_ada.shape
    tn = 1024
    return pl.pallas_call(
        _ada_kernel,
        grid=(depth, n // tn),
        in_specs=[pl.BlockSpec((8, d), lambda l, j: (0, 0)),
                  pl.BlockSpec((1, d, tn), lambda l, j: (l, 0, j)),
                  pl.BlockSpec((1, 1, tn), lambda l, j: (l, 0, j))],
        out_specs=pl.BlockSpec((1, 8, tn), lambda l, j: (l, 0, j)),
        out_shape=jax.ShapeDtypeStruct((depth, 8, n), F32),
        compiler_params=_cparams(("arbitrary", "arbitrary"), 40),
        name="ada",
    )(s_in, w_ada, b_ada.reshape(depth, 1, n))


def _part_specs(parts, tm, width):
    if len(parts) == 1:
        return [pl.BlockSpec((tm, width), lambda i: (i, 0))]
    n0 = parts[0].shape[0] // tm
    return [pl.BlockSpec((tm, width), lambda i: (jnp.minimum(i, n0 - 1), 0)),
            pl.BlockSpec((tm, width), lambda i: (jnp.maximum(i - n0, 0), 0))]


def _on_part(n_first, n_parts, fn):
    if n_parts == 1:
        fn(0)
    else:
        i = pl.program_id(0)
        pl.when(i < n_first)(lambda: fn(0))
        pl.when(i >= n_first)(lambda: fn(1))


def _prenorm_kernel(*refs, sc, sh, n_first):
    x_refs, (g_ref, mod_ref, o_ref) = refs[:-3], refs[-3:]

    def run(part):
        m = mod_ref[0]
        y = _rms(x_refs[part][...], g_ref[...])
        o_ref[...] = (y * (1.0 + m[sc:sc + 1]) + m[sh:sh + 1]).astype(o_ref.dtype)

    _on_part(n_first, len(x_refs), run)


def _prenorm(x_parts, n_rows, g, modtab, sc, sh, lat_rows, out_dtype):
    d = x_parts[0].shape[1]
    nb = modtab.shape[0] - 1
    mi = _mod_index(ROW_TILE, lat_rows, nb)
    return pl.pallas_call(
        functools.partial(_prenorm_kernel, sc=sc, sh=sh, n_first=x_parts[0].shape[0] // ROW_TILE),
        grid=(n_rows // ROW_TILE,),
        in_specs=[*_part_specs(x_parts, ROW_TILE, d),
                  pl.BlockSpec((1, d), lambda i: (0, 0)),
                  pl.BlockSpec((1, 6, d), lambda i: (mi(i), 0, 0))],
        out_specs=pl.BlockSpec((ROW_TILE, d), lambda i: (i, 0)),
        out_shape=jax.ShapeDtypeStruct((n_rows, d), out_dtype),
        compiler_params=_cparams(("arbitrary",), 32),
        name="prenorm",
    )(*x_parts, g.reshape(1, d), modtab)


def _mm_kernel(x_ref, wt_ref, o_ref, wb_s):
    @pl.when(pl.program_id(1) == 0)
    def _():
        wb_s[...] = _bf(wt_ref[0])

    o_ref[...] = _dot_nt(x_ref[...], wb_s[...]).astype(o_ref.dtype)


def _mm(x, wt, layer, tm, tn, n_cols=None, out_dtype=F32):
    m, k = x.shape
    n = wt.shape[1] if n_cols is None else n_cols
    return pl.pallas_call(
        _mm_kernel,
        grid=(n // tn, m // tm),
        in_specs=[pl.BlockSpec((tm, k), lambda j, i: (i, 0)),
                  pl.BlockSpec((1, tn, k), lambda j, i: (layer, j, 0))],
        out_specs=pl.BlockSpec((tm, tn), lambda j, i: (i, j)),
        out_shape=jax.ShapeDtypeStruct((m, n), out_dtype),
        scratch_shapes=[pltpu.VMEM((tn, k), BF16)],
        compiler_params=_cparams(("arbitrary", "arbitrary"), 48),
        name="proj",
    )(x, wt)


def _bwd_chunk(i, n_ctx, n_lat):
    return jnp.where(i < n_ctx, n_ctx - 1 - i, 2 * n_ctx + n_lat - 1 - i)


def _rows(c):
    return pl.ds(pl.multiple_of(c * CHUNK, CHUNK), CHUNK)


def _for_chunks(n_ctx, n_lat, fn, unroll, with_ctx=True):
    def run(part, n, off):
        def body(c, carry):
            fn(part, c, c + off)
            return carry
        lax.fori_loop(0, n, body, 0, unroll=max(u for u in (1, 2, unroll) if n % u == 0 and u <= unroll))
    if with_ctx:
        run(0, n_ctx, 0)
    run(1, n_lat, n_ctx)


def _lanes(d, w):
    return slice(d * w, (d + 1) * w)


def _mixer_call(kernel_fn, name, in_specs, args, nb, heads, lat_len, ctx_len, dv, scratch, emit_ctx, vmem_mb):
    width = heads * dv
    out_specs = [pl.BlockSpec((lat_len, dv), lambda b, h: (b, h))]
    out_shape = [jax.ShapeDtypeStruct((nb * lat_len, width), BF16)]
    if emit_ctx:
        out_specs.append(pl.BlockSpec((ctx_len, dv), lambda b, h: (b, h)))
        out_shape.append(jax.ShapeDtypeStruct((nb * ctx_len, width), BF16))
    n_in = len(in_specs)

    def body(*refs):
        ins, rest = refs[:n_in], refs[n_in:]
        if emit_ctx:
            ol, oc, scr = rest[0], rest[1], rest[2:]
        else:
            ol, oc, scr = rest[0], None, rest[1:]
        kernel_fn(*ins, ol, oc, *scr, n_ctx=ctx_len // CHUNK, n_lat=lat_len // CHUNK)

    return pl.pallas_call(
        body,
        grid=(nb, heads),
        in_specs=in_specs,
        out_specs=out_specs,
        out_shape=out_shape,
        scratch_shapes=scratch,
        compiler_params=_cparams(("arbitrary", "arbitrary"), vmem_mb),
        name=name,
    )(*args)


def _seq_specs(col0, width, nb, lat_len, ctx_len):
    n_lat_blk = nb * lat_len // ctx_len
    c0 = col0 * 128 // width
    return (pl.BlockSpec((lat_len, width), lambda b, h: (b, c0 + h)),
            pl.BlockSpec((ctx_len, width), lambda b, h: (n_lat_blk + b, c0 + h)))


def _ret_kernel(ql, qc, kl, kc, vl, vc, gl, gc, cos_ref, sin_ref, lg_ref, gn_ref, ol_ref, oc_ref,
                qs_s, kv_s, sall_s, o_s, st_s, dm_s, dq_s, dk_s, *, n_ctx, n_lat):
    nch = n_ctx + n_lat
    unroll_a, unroll_c = RET_UNROLL
    qr, kr, vr, gr, outr = (qc, ql), (kc, kl), (vc, vl), (gc, gl), (oc_ref, ol_ref)
    t_i = lax.broadcasted_iota(jnp.int32, (CHUNK, CHUNK), 0).astype(F32)
    s_i = lax.broadcasted_iota(jnp.int32, (CHUNK, CHUNK), 1).astype(F32)
    dm = None
    for d in range(2):
        lg = lg_ref[d, 0][0:1, 0:CHUNK]
        rel = (t_i - s_i) if d == 0 else (s_i - t_i)
        dmd = jnp.where(rel >= 0, jnp.exp(jnp.maximum(rel, 0.0) * lg), 0.0)
        dm = dmd if dm is None else dm + dmd
        p = t_i if d == 0 else (CHUNK - 1.0) - t_i
        dq_s[d] = jnp.exp((p + 1.0) * lg)
        dk_s[d] = jnp.exp((CHUNK - 1.0 - p) * lg)
    dm_s[...] = dm
    scale = RET_DK ** -0.5

    def phase_a(part, cl, cg):
        rl, rg = _rows(cl), _rows(cg)
        cs = cos_ref[rg, :]
        sn = sin_ref[rg, :]
        q = qr[part][rl, :]
        k = kr[part][rl, :]
        q = q * cs + pltpu.roll(q, RET_DK // 2, 1) * sn
        k = (k * cs + pltpu.roll(k, RET_DK // 2, 1) * sn) * scale
        vb = _bf(vr[part][rl, :])
        s = _dot_nt(_bf(q), _bf(k))
        o_s[rg, :] = _dot(_bf(s * dm_s[...]), vb)
        qs_s[rg, :] = jnp.concatenate([_bf(q * dq_s[0]), _bf(q * dq_s[1])], axis=1)
        kd = jnp.concatenate([_bf(k * dk_s[0]), _bf(k * dk_s[1])], axis=1)
        kv_s[cg] = _dot_tn(kd, vb)

    _for_chunks(n_ctx, n_lat, phase_a, unroll_a)

    st_s[...] = jnp.zeros(st_s.shape, F32)
    dec = [jnp.exp(float(CHUNK) * lg_ref[d, 0][0:1, :]) for d in range(2)]

    def phase_b(i, carry):
        for d, c in ((0, i), (1, _bwd_chunk(i, n_ctx, n_lat))):
            st = st_s[d]
            sall_s[c, _lanes(d, RET_DK), :] = _bf(st)
            st_s[d] = dec[d] * st + kv_s[c, _lanes(d, RET_DK), :]
        return carry

    lax.fori_loop(0, nch, phase_b, 0)
    gn = gn_ref[...]

    def phase_c(part, cl, cg):
        rl, rg = _rows(cl), _rows(cg)
        o = o_s[rg, :] + _dot(qs_s[rg, :], sall_s[cg])
        outr[part][rl, :] = (_rms(o, gn) * _silu(gr[part][rl, :])).astype(BF16)

    _for_chunks(n_ctx, n_lat, phase_c, unroll_c, with_ctx=oc_ref is not None)


def _retention(p, nb, lat_len, ctx_len, cosf, sinf, lg_b, gn, emit_ctx):
    t_len = lat_len + ctx_len
    nch = t_len // CHUNK
    s128 = functools.partial(_seq_specs, width=128, nb=nb, lat_len=lat_len, ctx_len=ctx_len)
    s256 = functools.partial(_seq_specs, width=256, nb=nb, lat_len=lat_len, ctx_len=ctx_len)
    in_specs = [*s128(_C_RET_Q), *s128(_C_RET_K), *s256(_C_RET_V), *s256(_C_RET_G),
                pl.BlockSpec((t_len, 128), lambda b, h: (0, 0)),
                pl.BlockSpec((t_len, 128), lambda b, h: (0, 0)),
                pl.BlockSpec((2, 1, 8, RET_DV), lambda b, h: (0, h, 0, 0)),
                pl.BlockSpec((1, RET_DV), lambda b, h: (0, h))]
    scratch = [pltpu.VMEM((t_len, 2 * RET_DK), BF16),
               pltpu.VMEM((nch, 2 * RET_DK, RET_DV), F32),
               pltpu.VMEM((nch, 2 * RET_DK, RET_DV), BF16),
               pltpu.VMEM((t_len, RET_DV), F32),
               pltpu.VMEM((2, RET_DK, RET_DV), F32),
               pltpu.VMEM((CHUNK, CHUNK), F32),
               pltpu.VMEM((2, CHUNK, CHUNK), F32), pltpu.VMEM((2, CHUNK, CHUNK), F32)]
    args = (p, p, p, p, p, p, p, p, cosf, sinf, lg_b, gn.reshape(1, RET_WIDTH))
    return _mixer_call(_ret_kernel, "retention", in_specs, args, nb, RET_HEADS, lat_len, ctx_len, RET_DV,
                       scratch, emit_ctx, 48)


_HG_LEVELS = (64, 32, 16, 8, 4, 2, 1)


def _hgrn_constants():
    c = CHUNK
    t = np.arange(c)[:, None]
    u = np.arange(c)[None, :]
    cum = (u <= t).astype(np.float32)
    masks, signs = [], []
    for m in _HG_LEVELS:
        base = (t // (2 * m)) * (2 * m)
        lower = t >= base + m
        tb = t // (2 * m)
        sb = u // (2 * m)
        masks.append((tb == sb) & lower & (u < (sb * 2 * m + m)))
        signs.append(np.broadcast_to(np.where(lower, 1.0, -1.0), (c, c)))
    masks.append(t == u)
    kf = np.stack([x.astype(np.float32) for x in masks], axis=0)
    kb = np.stack([x.astype(np.float32)[::-1, ::-1] for x in masks], axis=0)
    sf = np.stack([x.astype(np.float32) for x in signs], axis=0)
    sb_ = np.stack([x.astype(np.float32)[::-1, ::-1] for x in signs], axis=0)
    return (np.stack([cum, cum[::-1, ::-1]], 0), np.stack([kf, kb], 0),
            np.stack([sf[:_HG_WIDE], sb_[:_HG_WIDE]], 0))


_HG_WIDE = 5


def _hgrn_level_exponents(b, lf, d, sgn_ref, row):
    out = []
    for l, m in enumerate(_HG_LEVELS[:_HG_WIDE]):
        pieces = []
        for j in range(CHUNK // (2 * m)):
            r = 2 * m * j + (m - 1 if d == 0 else m)
            pieces.append(jnp.broadcast_to(b[r:r + 1, :], (2 * m, CHUNK)))
        bref = pieces[0] if len(pieces) == 1 else jnp.concatenate(pieces, axis=0)
        out.append((b - bref) * sgn_ref[d, l])
    up = pltpu.roll(lf, CHUNK - 1, 0)
    dn = pltpu.roll(lf, 1, 0)
    r4 = row % 4
    if d == 0:
        e2 = jnp.where(r4 == 0, up, jnp.where(r4 == 1, 0.0, jnp.where(r4 == 2, lf, lf + dn)))
        e1 = jnp.where(row % 2 == 1, lf, 0.0)
    else:
        e2 = jnp.where(r4 == 0, lf + up, jnp.where(r4 == 1, lf, jnp.where(r4 == 2, 0.0, dn)))
        e1 = jnp.where(row % 2 == 0, lf, 0.0)
    return out + [e2, e1]


def _hgrn_kernel(ql, qc, ffl, ffc, fbl, fbc, il, ic, gl, gc, llb_ref, lub_ref, gn_ref, c_ref, k_ref, sgn_ref,
                 ol_ref, oc_ref, qs_s, kv_s, sall_s, dec_s, o_s, st_s, *, n_ctx, n_lat):
    nch = n_ctx + n_lat
    unroll_a, unroll_c = HGRN_UNROLL
    qr, fr, ir, gr, outr =(qc, ql), ((ffc, ffl), (fbc, fbl)), (ic, il), (gc, gl), (oc_ref, ol_ref)
    llb = llb_ref[...]
    lub = lub_ref[...]
    nlev = len(_HG_LEVELS)
    row = lax.broadcasted_iota(jnp.int32, (CHUNK, HGRN_DK), 0)

    def phase_a(part, cl, cg):
        rl, rg = _rows(cl), _rows(cg)
        q = _silu(qr[part][rl, :])
        qb = _bf(q)
        vb = _bf(ir[part][rl, :])
        a_sum, qs, kds = None, [], []
        for d in range(2):
            lsg = lub + _log_sigmoid(fr[d][part][rl, :])
            lf = jnp.maximum(llb, lsg) + jnp.log(1.0 + jnp.exp(-jnp.abs(llb - lsg)))
            k = 1.0 - jnp.exp(lf)
            lf_hi = _bf(lf)
            lf_lo = _bf(lf - lf_hi.astype(F32))
            b2 = _dot(c_ref[d], jnp.concatenate([lf_hi, lf_lo], axis=1))
            b = b2[:, 0:HGRN_DK] + b2[:, HGRN_DK:]
            es = _hgrn_level_exponents(b, lf, d, sgn_ref, row)
            kb = _bf(k)
            a = k_ref[d, nlev] * _dot_nt(qb, kb)
            for l in range(nlev):
                xb = _bf(jnp.exp(es[l]))
                a = a + k_ref[d, l] * _dot_nt(qb * xb, kb * xb)
            a_sum = a if a_sum is None else a_sum + a
            b_last = b[CHUNK - 1:CHUNK] if d == 0 else b[0:1]
            qs.append(_bf(q * jnp.exp(b)))
            kds.append(_bf(k * jnp.exp(b_last - b)))
            dec_s[cg, :, _lanes(d, HGRN_DK)] = jnp.exp(b_last)
        o_s[rg, :] = _dot(_bf(a_sum), vb)
        qs_s[rg, :] = jnp.concatenate(qs, axis=1)
        kv_s[cg] = _dot_tn(vb, jnp.concatenate(kds, axis=1))

    _for_chunks(n_ctx, n_lat, phase_a, unroll_a)
    st_s[...] = jnp.zeros(st_s.shape, F32)

    def phase_b(i, carry):
        for d, c in ((0, i), (1, _bwd_chunk(i, n_ctx, n_lat))):
            ln = _lanes(d, HGRN_DK)
            st = st_s[d]
            sall_s[c, :, ln] = _bf(st)
            st_s[d] = dec_s[c, :, ln] * st + kv_s[c, :, ln]
        return carry

    lax.fori_loop(0, nch, phase_b, 0)
    gn = gn_ref[...]

    def phase_c(part, cl, cg):
        rl, rg = _rows(cl), _rows(cg)
        o = o_s[rg, :] + _dot_nt(qs_s[rg, :], sall_s[cg])
        outr[part][rl, :] = (_rms(o, gn) * _silu(gr[part][rl, :])).astype(BF16)

    _for_chunks(n_ctx, n_lat, phase_c, unroll_c, with_ctx=oc_ref is not None)


def _hgrn(p, nb, lat_len, ctx_len, log_lb, log_ub, gn, consts, emit_ctx):
    mats, masks, signs = consts
    t_len = lat_len + ctx_len
    nch = t_len // CHUNK
    spec = functools.partial(_seq_specs, width=128, nb=nb, lat_len=lat_len, ctx_len=ctx_len)
    vec = pl.BlockSpec((1, 128), lambda b, h: (0, h))
    in_specs = [*spec(_C_HG_Q), *spec(_C_HG_FF), *spec(_C_HG_FB), *spec(_C_HG_I), *spec(_C_HG_G),
                vec, vec, vec,
                pl.BlockSpec(mats.shape, lambda b, h: (0, 0, 0)),
                pl.BlockSpec(masks.shape, lambda b, h: (0, 0, 0, 0)),
                pl.BlockSpec(signs.shape, lambda b, h: (0, 0, 0, 0))]
    scratch = [pltpu.VMEM((t_len, 2 * HGRN_DK), BF16),
               pltpu.VMEM((nch, HGRN_DV, 2 * HGRN_DK), F32),
               pltpu.VMEM((nch, HGRN_DV, 2 * HGRN_DK), BF16),
               pltpu.VMEM((nch, 1, 2 * HGRN_DK), F32),
               pltpu.VMEM((t_len, HGRN_DV), F32),
               pltpu.VMEM((2, HGRN_DV, HGRN_DK), F32)]
    args = (p, p, p, p, p, p, p, p, p, p, log_lb.reshape(1, HGRN_QK), log_ub.reshape(1, HGRN_QK),
            gn.reshape(1, HGRN_WIDTH), mats, masks, signs)
    return _mixer_call(_hgrn_kernel, "hgrn2", in_specs, args, nb, HGRN_HEADS, lat_len, ctx_len, HGRN_DV,
                       scratch, emit_ctx, 48)


def _mlstm_kernel(ql, qc, kl, kc, vl, vc, ogl, ogc, gcl, gcc, grl, grc, bc_ref, br_ref,
                  wq_ref, wk_ref, gn_ref, ol_ref, oc_ref,
                  q_s, gr_s, intra_s, kv_s, call_s, rho_s, bcol_s, bl_s, mu_s, mprev_s, st_s, m_s,
                  *, n_ctx, n_lat):
    nch = n_ctx + n_lat
    unroll_a, unroll_c = MLSTM_UNROLL
    h = pl.program_id(1)
    nh = MLSTM_HEADS
    dk, dv = MLSTM_DK, MLSTM_DV
    ext = 2 * dv
    qr, kr, vr, ogr, gcr, outr = (qc, ql), (kc, kl), (vc, vl), (ogc, ogl), (gcc, gcl), (oc_ref, ol_ref)
    n_loc = (n_ctx, n_lat)

    for cc in range(nch):
        src, c0 = (grc, cc) if cc < n_ctx else (grl, cc - n_ctx)
        gr_s[cc] = src[0, :, c0 * CHUNK:(c0 + 1) * CHUNK] + br_ref[...]

    row = lax.broadcasted_iota(jnp.int32, (CHUNK, dk), 0)
    lane = lax.broadcasted_iota(jnp.int32, (CHUNK, dv), 1)
    del lane
    ones_col = jnp.ones((CHUNK, dv), BF16)
    scale = dk ** -0.5
    t_i = lax.broadcasted_iota(jnp.int32, (CHUNK, CHUNK), 0)
    s_i = lax.broadcasted_iota(jnp.int32, (CHUNK, CHUNK), 1)
    lane16 = lax.broadcasted_iota(jnp.int32, (CHUNK, MLSTM_GATES), 1)
    sub16 = lax.broadcasted_iota(jnp.int32, (MLSTM_GATES, CHUNK), 0)

    def conv(src, w_ref, part, cl):
        r0 = pl.multiple_of(cl * CHUNK, CHUNK)
        n_rows = n_loc[part] * CHUNK
        x = src[pl.ds(r0, CHUNK), :]
        pr = src[pl.ds(jnp.maximum(r0 - 1, 0), 1), :]
        nx = src[pl.ds(jnp.minimum(r0 + CHUNK, n_rows - 1), 1), :]
        pr = jnp.where(cl != 0, pr, 0.0)
        nx = jnp.where(cl != n_loc[part] - 1, nx, 0.0)
        xp = jnp.where(row == 0, pr, pltpu.roll(x, 1, 0))
        xn = jnp.where(row == CHUNK - 1, nx, pltpu.roll(x, CHUNK - 1, 0))
        w = w_ref[...]
        return _silu(w[0:1] * xp + w[1:2] * x + w[2:3] * xn)

    def pick_col(g, j):
        return jnp.sum(jnp.where(lane16 == j, g, 0.0), axis=1, keepdims=True)

    def pick_row(g, j):
        return jnp.sum(jnp.where(sub16 == j, g, 0.0), axis=0, keepdims=True)

    def phase_a(part, cl, cg):
        rl, rg = _rows(cl), _rows(cg)
        q = conv(qr[part], wq_ref, part, cl)
        k = conv(kr[part], wk_ref, part, cl) * scale
        qb = _bf(q)
        q_s[rg, :] = qb
        vb = jnp.concatenate([_bf(vr[part][rl, :]), ones_col], axis=1)
        s = _dot_nt(qb, _bf(k))
        g_c = gcr[part][rl, :] + bc_ref[...]
        g_r = gr_s[cg]
        kws = []
        for d in range(2):
            ig_c = pick_col(g_c, d * nh + h)
            lf_c = _log_sigmoid(pick_col(g_c, 2 * nh + d * nh + h))
            ig_r = pick_row(g_r, d * nh + h)
            lf_r = _log_sigmoid(pick_row(g_r, 2 * nh + d * nh + h))
            tri = (s_i <= t_i) if d == 0 else (s_i >= t_i)
            tri_t = (t_i <= s_i) if d == 0 else (t_i >= s_i)
            b_c = jnp.sum(jnp.where(tri, lf_r, 0.0), axis=1, keepdims=True)
            b_r = jnp.sum(jnp.where(tri_t, lf_c, 0.0), axis=0, keepdims=True)
            dmat = jnp.where(tri, b_c - b_r + ig_r, -jnp.inf)
            rho = jnp.max(dmat, axis=1, keepdims=True)
            intra_s[d, rg, :] = _dot(_bf(s * jnp.exp(dmat - rho)), vb)
            b_last = b_c[CHUNK - 1:CHUNK] if d == 0 else b_c[0:1]
            mu = jnp.max(b_last - b_r + ig_r, axis=1, keepdims=True)
            kws.append(_bf(k * jnp.exp(b_last - b_c + ig_c - mu)))
            rho_s[d, rg, :] = jnp.broadcast_to(rho, (CHUNK, 128))
            bcol_s[d, rg, :] = jnp.broadcast_to(b_c, (CHUNK, 128))
            bl_s[cg, :, _lanes(d, 128)] = jnp.broadcast_to(b_last, (1, 128))
            mu_s[cg, :, _lanes(d, 128)] = jnp.broadcast_to(mu, (1, 128))
        kv_s[cg] = _dot_tn(jnp.concatenate(kws, axis=1), vb)

    _for_chunks(n_ctx, n_lat, phase_a, unroll_a)
    st_s[...] = jnp.zeros(st_s.shape, F32)
    m_s[...] = jnp.zeros(m_s.shape, F32)

    def wide(v):
        return jnp.concatenate([v, v], axis=1)

    def phase_b(i, carry):
        for d, c in ((0, i), (1, _bwd_chunk(i, n_ctx, n_lat))):
            ln = _lanes(d, 128)
            m_prev = m_s[d]
            st = st_s[d]
            mprev_s[c, :, ln] = m_prev
            call_s[c, :, _lanes(d, ext)] = _bf(st)
            bl = bl_s[c, :, ln]
            mu = mu_s[c, :, ln]
            m_new = jnp.maximum(bl + m_prev, mu)
            st_s[d] = (wide(jnp.exp(bl + m_prev - m_new)) * st
                       + wide(jnp.exp(mu - m_new)) * kv_s[c, _lanes(d, dk), :])
            m_s[d] = m_new
        return carry

    lax.fori_loop(0, nch, phase_b, 0)
    gn = gn_ref[...]

    def phase_c(part, cl, cg):
        rl, rg = _rows(cl), _rows(cg)
        qc_all = _dot(q_s[rg, :], call_s[cg])
        o = None
        for d in range(2):
            rho = rho_s[d, rg, :]
            b_c = bcol_s[d, rg, :]
            m_prev = mprev_s[cg, :, _lanes(d, 128)]
            m_t = jnp.maximum(rho, b_c + m_prev)
            nd = (wide(jnp.exp(rho - m_t)) * intra_s[d, rg, :]
                  + wide(jnp.exp(b_c + m_prev - m_t)) * qc_all[:, _lanes(d, ext)])
            hh = nd[:, 0:dv] / jnp.maximum(jnp.abs(nd[:, dv:]), jnp.exp(-m_t))
            o = hh if o is None else o + hh
        y = o - jnp.mean(o, axis=-1, keepdims=True)
        y = y * lax.rsqrt(jnp.mean(y * y, axis=-1, keepdims=True) + NORM_EPS)
        outr[part][rl, :] = (y * gn * _sigmoid(ogr[part][rl, :])).astype(BF16)

    _for_chunks(n_ctx, n_lat, phase_c, unroll_c, with_ctx=oc_ref is not None)


def _mlstm(p, g_col, g_row_lat, g_row_ctx, nb, lat_len, ctx_len, gate_b, conv_w, gn, emit_ctx):
    t_len = lat_len + ctx_len
    nch = t_len // CHUNK
    n_lat_blk = nb * lat_len // ctx_len
    ng = MLSTM_GATES
    dk, dv = MLSTM_DK, MLSTM_DV
    spec = functools.partial(_seq_specs, width=128, nb=nb, lat_len=lat_len, ctx_len=ctx_len)
    in_specs = [*spec(_C_ML_Q), *spec(_C_ML_K), *spec(_C_ML_V), *spec(_C_ML_O),
                pl.BlockSpec((lat_len, ng), lambda b, h: (b, 0)),
                pl.BlockSpec((ctx_len, ng), lambda b, h: (n_lat_blk + b, 0)),
                pl.BlockSpec((1, ng, lat_len), lambda b, h: (b, 0, 0)),
                pl.BlockSpec((1, ng, ctx_len), lambda b, h: (b, 0, 0)),
                pl.BlockSpec((1, ng), lambda b, h: (0, 0)),
                pl.BlockSpec((ng, 1), lambda b, h: (0, 0)),
                pl.BlockSpec((3, 128), lambda b, h: (0, h)),
                pl.BlockSpec((3, 128), lambda b, h: (0, MLSTM_HEADS + h)),
                pl.BlockSpec((1, 128), lambda b, h: (0, h))]
    scratch = [pltpu.VMEM((t_len, dk), BF16),
               pltpu.VMEM((nch, ng, CHUNK), F32),
               pltpu.VMEM((2, t_len, 2 * dv), F32),
               pltpu.VMEM((nch, 2 * dk, 2 * dv), F32),
               pltpu.VMEM((nch, dk, 4 * dv), BF16),
               pltpu.VMEM((2, t_len, 128), F32), pltpu.VMEM((2, t_len, 128), F32),
               pltpu.VMEM((nch, 1, 256), F32), pltpu.VMEM((nch, 1, 256), F32), pltpu.VMEM((nch, 1, 256), F32),
               pltpu.VMEM((2, dk, 2 * dv), F32), pltpu.VMEM((2, 1, 128), F32)]
    args = (p, p, p, p, p, p, p, p, g_col, g_col, g_row_lat, g_row_ctx,
            gate_b.reshape(1, ng), gate_b.reshape(ng, 1), conv_w, conv_w, gn.reshape(1, MLSTM_WIDTH))
    return _mixer_call(_mlstm_kernel, "mlstm", in_specs, args, nb, MLSTM_HEADS, lat_len, ctx_len, dv,
                       scratch, emit_ctx, 48)


def _wout_kernel(*refs, n_lat_tiles, n_mix_parts, n_x_parts, emit_next):
    it = iter(refs)
    mix = [[next(it) for _ in range(n_mix_parts)] for _ in range(3)]
    x_parts = [next(it) for _ in range(n_x_parts)]
    wr_ref, wh_ref, wm_ref, g_ref, mod_ref = (next(it) for _ in range(5))
    g2_ref = next(it) if emit_next else None
    o_ref = next(it)
    h2_ref = next(it) if emit_next else None

    def run(part):
        pm = min(part, n_mix_parts - 1)
        y = (_dot(mix[0][pm][...], wr_ref[...]) + _dot(mix[1][pm][...], wh_ref[...])
             + _dot(mix[2][pm][...], wm_ref[...]))
        m = mod_ref[0]
        xn = x_parts[min(part, n_x_parts - 1)][...] + m[2:3] * _rms(y, g_ref[...])
        o_ref[...] = xn
        if emit_next:
            h2_ref[...] = (_rms(xn, g2_ref[...]) * (1.0 + m[4:5]) + m[3:4]).astype(h2_ref.dtype)

    _on_part(n_lat_tiles, max(n_mix_parts, n_x_parts), run)


def _wout(mix_parts, w_out, x_parts, n_rows, g1, modtab, lat_rows, g2_next):
    d = x_parts[0].shape[1]
    nb = modtab.shape[0] - 1
    mi = _mod_index(ROW_TILE, lat_rows, nb)
    tm = ROW_TILE
    n_lat_tiles = mix_parts[0][0].shape[0] // tm
    emit_next = g2_next is not None
    widths = (RET_WIDTH, HGRN_WIDTH, MLSTM_WIDTH)
    w_r = w_out[0:RET_WIDTH]
    w_h = w_out[RET_WIDTH:RET_WIDTH + HGRN_WIDTH]
    w_m = w_out[RET_WIDTH + HGRN_WIDTH:]
    full = lambda a: pl.BlockSpec(a.shape, lambda i: (0, 0))
    in_specs, args = [], []
    for parts, w in zip(mix_parts, widths):
        in_specs += _part_specs(parts, tm, w)
        args += list(parts)
    in_specs += _part_specs(x_parts, tm, d)
    args += list(x_parts)
    in_specs += [full(w_r), full(w_h), full(w_m),
                 pl.BlockSpec((1, d), lambda i: (0, 0)),
                 pl.BlockSpec((1, 6, d), lambda i: (mi(i), 0, 0))]
    args += [w_r, w_h, w_m, g1.reshape(1, d), modtab]
    out_specs = [pl.BlockSpec((tm, d), lambda i: (i, 0))]
    out_shape = [jax.ShapeDtypeStruct((n_rows, d), F32)]
    if emit_next:
        in_specs.append(pl.BlockSpec((1, d), lambda i: (0, 0)))
        args.append(g2_next.reshape(1, d))
        out_specs.append(pl.BlockSpec((tm, d), lambda i: (i, 0)))
        out_shape.append(jax.ShapeDtypeStruct((n_rows, d), BF16))
    return pl.pallas_call(
        functools.partial(_wout_kernel, n_lat_tiles=n_lat_tiles, n_mix_parts=len(mix_parts[0]),
                          n_x_parts=len(x_parts), emit_next=emit_next),
        grid=(n_rows // tm,),
        in_specs=in_specs,
        out_specs=out_specs,
        out_shape=out_shape,
        compiler_params=_cparams(("arbitrary",), 52),
        name="wout",
    )(*args)


def _ffn_kernel(*refs, emit_next):
    if emit_next:
        h_ref, w1_ref, w3_ref, w2_ref, x_ref, g_ref, mod_ref, gn_ref, modn_ref, o_ref, hn_ref, acc_ref = refs
    else:
        h_ref, w1_ref, w3_ref, w2_ref, x_ref, g_ref, mod_ref, o_ref, acc_ref = refs
    f = pl.program_id(1)

    @pl.when(f == 0)
    def _():
        acc_ref[...] = jnp.zeros(acc_ref.shape, F32)

    h = h_ref[...]
    u = _silu(_dot(h, w1_ref[...])) * _dot(h, w3_ref[...])
    acc_ref[...] += _dot(_bf(u), w2_ref[...])

    @pl.when(f == pl.num_programs(1) - 1)
    def _():
        m = mod_ref[0]
        xn = x_ref[...] + m[5:6] * _rms(acc_ref[...], g_ref[...])
        o_ref[...] = xn
        if emit_next:
            mn = modn_ref[0]
            hn_ref[...] = (_rms(xn, gn_ref[...]) * (1.0 + mn[1:2]) + mn[0:1]).astype(hn_ref.dtype)


def _ffn(hb, w1, w3, w2, xs, g3, modtab, lat_rows, g_next, modtab_next):
    n_rows, d = hb.shape
    dff = w1.shape[1]
    nb = modtab.shape[0] - 1
    tm, tf = ROW_TILE, FFN_TF
    mi = _mod_index(tm, lat_rows, nb)
    emit_next = g_next is not None
    nf = dff // tf

    def fs(i, f):
        return jnp.where(i % 2 == 1, nf - 1 - f, f)

    in_specs = [pl.BlockSpec((tm, d), lambda i, f: (i, 0)),
                pl.BlockSpec((d, tf), lambda i, f: (0, fs(i, f))),
                pl.BlockSpec((d, tf), lambda i, f: (0, fs(i, f))),
                pl.BlockSpec((tf, d), lambda i, f: (fs(i, f), 0)),
                pl.BlockSpec((tm, d), lambda i, f: (i, 0)),
                pl.BlockSpec((1, d), lambda i, f: (0, 0)),
                pl.BlockSpec((1, 6, d), lambda i, f: (mi(i), 0, 0))]
    args = [hb, w1, w3, w2, xs, g3.reshape(1, d), modtab]
    out_specs = [pl.BlockSpec((tm, d), lambda i, f: (i, 0))]
    out_shape = [jax.ShapeDtypeStruct((n_rows, d), F32)]
    if emit_next:
        in_specs += [pl.BlockSpec((1, d), lambda i, f: (0, 0)),
                     pl.BlockSpec((1, 6, d), lambda i, f: (mi(i), 0, 0))]
        args += [g_next.reshape(1, d), modtab_next]
        out_specs.append(pl.BlockSpec((tm, d), lambda i, f: (i, 0)))
        out_shape.append(jax.ShapeDtypeStruct((n_rows, d), BF16))
    return pl.pallas_call(
        functools.partial(_ffn_kernel, emit_next=emit_next),
        grid=(n_rows // tm, dff // tf),
        in_specs=in_specs,
        out_specs=out_specs,
        out_shape=out_shape,
        scratch_shapes=[pltpu.VMEM((tm, d), F32)],
        compiler_params=_cparams(("arbitrary", "arbitrary"), 48),
        name="ffn",
    )(*args)


_META_E0, _META_E1, _META_R0, _META_R1, _META_G0, _META_G1 = range(6)


def _router_kernel(x_ref, g_ref, mod_ref, wr_ref, tri_ref, meta_ref, cnt_ref, carry_ref):
    i = pl.program_id(0)

    @pl.when(i == 0)
    def _():
        carry_ref[...] = jnp.zeros(carry_ref.shape, F32)

    m = mod_ref[0]
    hmod = _rms(x_ref[...], g_ref[...]) * (1.0 + m[4:5]) + m[3:4]
    logits = jnp.dot(hmod, wr_ref[...], precision=lax.Precision.HIGHEST, preferred_element_type=F32)
    lane = lax.broadcasted_iota(jnp.int32, logits.shape, 1)
    lanef = lane.astype(F32)
    logits = jnp.where(lane < N_EXPERTS, logits, -jnp.inf)
    v0 = jnp.max(logits, axis=1, keepdims=True)
    e0 = jnp.min(jnp.where(logits == v0, lanef, 1e9), axis=1, keepdims=True)
    rest = jnp.where(lanef == e0, -jnp.inf, logits)
    v1 = jnp.max(rest, axis=1, keepdims=True)
    e1 = jnp.min(jnp.where(rest == v1, lanef, 1e9), axis=1, keepdims=True)
    ex = jnp.exp(v1 - v0)
    g0 = 1.0 / (1.0 + ex)
    g1 = ex / (1.0 + ex)
    oh0 = lanef == e0
    oh1 = lanef == e1
    oh = jnp.where(jnp.logical_or(oh0, oh1), 1.0, 0.0)
    before = _dot(tri_ref[...], _bf(oh)) + carry_ref[0:1, :]
    r0 = jnp.sum(jnp.where(oh0, before, 0.0), axis=1, keepdims=True)
    r1 = jnp.sum(jnp.where(oh1, before, 0.0), axis=1, keepdims=True)
    carry_ref[0:1, :] = carry_ref[0:1, :] + jnp.sum(oh, axis=0, keepdims=True)
    meta = jnp.zeros(logits.shape, F32)
    for j, val in ((_META_E0, e0), (_META_E1, e1), (_META_R0, r0), (_META_R1, r1),
                   (_META_G0, g0), (_META_G1, g1)):
        meta = jnp.where(lane == j, val, meta)
    meta_ref[...] = meta
    cnt_ref[...] = carry_ref[...]


def _router(xs, n_rows, g2, modtab, w_router, lat_rows):
    d = xs.shape[1]
    nb = modtab.shape[0] - 1
    tm = ROW_TILE
    mi = _mod_index(tm, lat_rows, nb)
    wr = jnp.zeros((d, 128), F32).at[:, :N_EXPERTS].set(w_router)
    tri = jnp.asarray(np.tril(np.ones((tm, tm), np.float32), -1), BF16)
    return pl.pallas_call(
        _router_kernel,
        grid=(n_rows // tm,),
        in_specs=[pl.BlockSpec((tm, d), lambda i: (i, 0)),
                  pl.BlockSpec((1, d), lambda i: (0, 0)),
                  pl.BlockSpec((1, 6, d), lambda i: (mi(i), 0, 0)),
                  pl.BlockSpec((d, 128), lambda i: (0, 0)),
                  pl.BlockSpec((tm, tm), lambda i: (0, 0))],
        out_specs=[pl.BlockSpec((tm, 128), lambda i: (i, 0)),
                   pl.BlockSpec((8, 128), lambda i: (0, 0))],
        out_shape=[jax.ShapeDtypeStruct((n_rows, 128), F32),
                   jax.ShapeDtypeStruct((8, 128), F32)],
        scratch_shapes=[pltpu.VMEM((8, 128), F32)],
        compiler_params=_cparams(("arbitrary",), 32),
        name="router",
    )(xs, g2.reshape(1, d), modtab, wr, tri)


def _dispatch_kernel(dest_ref, x_ref, g_ref, mod_ref, init_ref, o_ref, h_s, sem):
    del init_ref
    i = pl.program_id(0)
    tm = h_s.shape[0]
    m = mod_ref[0]
    h_s[...] = _rms(x_ref[...], g_ref[...]) * (1.0 + m[4:5]) + m[3:4]

    def row_copy(r, k):
        dst = dest_ref[TOP_K * (i * tm + r) + k]
        return pltpu.make_async_copy(h_s.at[pl.ds(r, 1)], o_ref.at[pl.ds(dst, 1)], sem)

    def start(r, carry):
        for k in range(TOP_K):
            row_copy(r, k).start(priority=k)
        return carry

    lax.fori_loop(0, tm, start, 0, unroll=DMA_UNROLL)

    def wait(r, carry):
        for k in range(TOP_K):
            row_copy(r, k).wait()
        return carry

    lax.fori_loop(0, tm, wait, 0, unroll=DMA_UNROLL)


def _dispatch(dest, xs, n_rows, g2, modtab, n_slots, lat_rows):
    d = xs.shape[1]
    nb = modtab.shape[0] - 1
    tm = ROW_TILE
    mi = _mod_index(tm, lat_rows, nb)
    init = jnp.zeros((n_slots, d), F32)
    grid_spec = pltpu.PrefetchScalarGridSpec(
        num_scalar_prefetch=1,
        grid=(n_rows // tm,),
        in_specs=[pl.BlockSpec((tm, d), lambda i, dst: (i, 0)),
                  pl.BlockSpec((1, d), lambda i, dst: (0, 0)),
                  pl.BlockSpec((1, 6, d), lambda i, dst: (mi(i), 0, 0)),
                  pl.BlockSpec(memory_space=pl.ANY)],
        out_specs=pl.BlockSpec(memory_space=pl.ANY),
        scratch_shapes=[pltpu.VMEM((tm, d), F32), pltpu.SemaphoreType.DMA(())],
    )
    return pl.pallas_call(
        _dispatch_kernel,
        grid_spec=grid_spec,
        out_shape=jax.ShapeDtypeStruct((n_slots, d), F32),
        input_output_aliases={4: 0},
        compiler_params=_cparams(("arbitrary",), 32),
        name="dispatch",
    )(dest, xs, g2.reshape(1, d), modtab, init)


def _expert_kernel(ge_ref, nt_ref, ng_ref, x_ref, w1_ref, w3_ref, w2_ref, o_ref, xb_s):
    s = pl.program_id(0)
    f = pl.program_id(1)
    tm = MOE_TM

    @pl.when(s < ng_ref[0])
    def _():
        @pl.when(f == 0)
        def _():
            xb_s[...] = _bf(x_ref[...])
            o_ref[...] = jnp.zeros(o_ref.shape, F32)

        w1b, w3b, w2b = _bf(w1_ref[0]), _bf(w3_ref[0]), _bf(w2_ref[0])

        def tile(t):
            rows = pl.ds(t * tm, tm)
            h = xb_s[rows, :]
            u = _silu(_dot(h, w1b)) * _dot(h, w3b)
            o_ref[rows, :] += _dot(_bf(u), w2b)

        tile(0)
        for t in range(1, MOE_GROUP):
            pl.when(nt_ref[s] > t)(functools.partial(tile, t))

    @pl.when(jnp.logical_and(s >= ng_ref[0], f == pl.num_programs(1) - 1))
    def _():
        o_ref[...] = jnp.zeros(o_ref.shape, F32)


def _experts(group_e, group_tiles, n_groups, xsort, w1, w3, w2):
    n_slots, d = xsort.shape
    dff = w1.shape[2]
    tg, tf = MOE_TM * MOE_GROUP, MOE_TF
    nf = dff // tf

    def ss(s, ng):
        return jnp.maximum(jnp.minimum(s, ng[0] - 1), 0)

    def ff(s, f, ng):
        snake = jnp.where(s % 2 == 1, nf - 1 - f, f)
        last = jnp.where((ng[0] - 1) % 2 == 1, 0, nf - 1)
        return jnp.where(s < ng[0], snake, last)

    grid_spec = pltpu.PrefetchScalarGridSpec(
        num_scalar_prefetch=3,
        grid=(n_slots // tg, nf),
        in_specs=[pl.BlockSpec((tg, d), lambda s, f, ge, nt, ng: (ss(s, ng), 0)),
                  pl.BlockSpec((1, d, tf), lambda s, f, ge, nt, ng: (ge[ss(s, ng)], 0, ff(s, f, ng))),
                  pl.BlockSpec((1, d, tf), lambda s, f, ge, nt, ng: (ge[ss(s, ng)], 0, ff(s, f, ng))),
                  pl.BlockSpec((1, tf, d), lambda s, f, ge, nt, ng: (ge[ss(s, ng)], ff(s, f, ng), 0))],
        out_specs=pl.BlockSpec((tg, d), lambda s, f, ge, nt, ng: (s, 0)),
        scratch_shapes=[pltpu.VMEM((tg, d), BF16)],
    )
    return pl.pallas_call(
        _expert_kernel,
        grid_spec=grid_spec,
        out_shape=jax.ShapeDtypeStruct((n_slots, d), F32),
        compiler_params=_cparams(("arbitrary", "arbitrary"), 56),
        name="experts",
    )(group_e, group_tiles, n_groups, xsort, w1, w3, w2)


def _combine_kernel(dest_ref, y_ref, meta_ref, x_ref, g_ref, mod_ref, o_ref, buf_s, sem):
    i = pl.program_id(0)
    tm = x_ref.shape[0]

    def row_copy(r, k):
        src = dest_ref[TOP_K * (i * tm + r) + k]
        return pltpu.make_async_copy(y_ref.at[pl.ds(src, 1)], buf_s.at[k, pl.ds(r, 1)], sem)

    def start(r, carry):
        for k in range(TOP_K):
            row_copy(r, k).start(priority=k)
        return carry

    lax.fori_loop(0, tm, start, 0, unroll=DMA_UNROLL)

    def wait(r, carry):
        for k in range(TOP_K):
            row_copy(r, k).wait()
        return carry

    lax.fori_loop(0, tm, wait, 0, unroll=DMA_UNROLL)

    meta = meta_ref[...]
    lane = lax.broadcasted_iota(jnp.int32, meta.shape, 1)
    g0 = jnp.sum(jnp.where(lane == _META_G0, meta, 0.0), axis=1, keepdims=True)
    g1 = jnp.sum(jnp.where(lane == _META_G1, meta, 0.0), axis=1, keepdims=True)
    y = buf_s[0] * g0 + buf_s[1] * g1
    m = mod_ref[0]
    o_ref[...] = x_ref[...] + m[5:6] * _rms(y, g_ref[...])


def _combine(dest, yb, meta, xs, n_rows, g3, modtab, lat_rows):
    d = xs.shape[1]
    nb = modtab.shape[0] - 1
    tm = ROW_TILE
    mi = _mod_index(tm, lat_rows, nb)
    grid_spec = pltpu.PrefetchScalarGridSpec(
        num_scalar_prefetch=1,
        grid=(n_rows // tm,),
        in_specs=[pl.BlockSpec(memory_space=pl.ANY),
                  pl.BlockSpec((tm, 128), lambda i, dst: (i, 0)),
                  pl.BlockSpec((tm, d), lambda i, dst: (i, 0)),
                  pl.BlockSpec((1, d), lambda i, dst: (0, 0)),
                  pl.BlockSpec((1, 6, d), lambda i, dst: (mi(i), 0, 0))],
        out_specs=pl.BlockSpec((tm, d), lambda i, dst: (i, 0)),
        scratch_shapes=[pltpu.VMEM((TOP_K, tm, d), F32), pltpu.SemaphoreType.DMA(())],
    )
    return pl.pallas_call(
        _combine_kernel,
        grid_spec=grid_spec,
        out_shape=jax.ShapeDtypeStruct((n_rows, d), F32),
        compiler_params=_cparams(("arbitrary",), 40),
        name="combine",
    )(dest, yb, meta, xs, g3.reshape(1, d), modtab)


def _moe(xs, n_rows, g2, g3, modtab, w_router, w1, w3, w2, lat_rows):
    meta, cnt = _router(xs, n_rows, g2, modtab, w_router, lat_rows)
    counts = cnt[0, :N_EXPERTS].astype(jnp.int32)
    tg = MOE_TM * MOE_GROUP
    padded = (counts + tg - 1) // tg * tg
    pad_end = jnp.cumsum(padded)
    pad_start = pad_end - padded
    e = meta[:, _META_E0:_META_E1 + 1].astype(jnp.int32)
    r = meta[:, _META_R0:_META_R1 + 1].astype(jnp.int32)
    dest = (pad_start[e] + r).reshape(-1)
    n_groups_max = (n_rows * TOP_K) // tg + N_EXPERTS
    n_slots = n_groups_max * tg
    group_start = jnp.arange(n_groups_max, dtype=jnp.int32) * tg
    group_e = jnp.minimum(jnp.sum((group_start[:, None] >= pad_end[None, :]).astype(jnp.int32), axis=1),
                          N_EXPERTS - 1)
    filled = counts[group_e] - (group_start - pad_start[group_e])
    group_tiles = jnp.clip((filled + MOE_TM - 1) // MOE_TM, 1, MOE_GROUP).astype(jnp.int32)
    n_groups = (pad_end[-1:] // tg).astype(jnp.int32)
    xsort = _dispatch(dest, xs, n_rows, g2, modtab, n_slots, lat_rows)
    yb = _experts(group_e, group_tiles, n_groups, xsort, w1, w3, w2)
    return _combine(dest, yb, meta, xs, n_rows, g3, modtab, lat_rows)


def _rotary_tables(lat_len, ctx_len):
    rows = lat_len // GRID_W
    row = jnp.repeat(jnp.arange(rows, dtype=F32), GRID_W)
    col = jnp.tile(jnp.arange(GRID_W, dtype=F32), rows)
    n_freq = RET_DK // 4
    inv = ROPE_BASE ** (-jnp.arange(n_freq, dtype=F32) / n_freq)
    ang = jnp.concatenate([row[:, None] * inv, col[:, None] * inv], axis=-1)
    cos, sin = jnp.cos(ang), jnp.sin(ang)
    cosf = jnp.concatenate([jnp.ones((ctx_len, RET_DK), F32), jnp.concatenate([cos, cos], -1)], 0)
    sinf = jnp.concatenate([jnp.zeros((ctx_len, RET_DK), F32), jnp.concatenate([-sin, sin], -1)], 0)
    return cosf, sinf


def kernel(x, c, ctx, c_ctx, w_ada, b_ada, norm_g, w_in, w_out, ret_decay, ret_gn, hgrn_lb, hgrn_gn,
           mlstm_conv, mlstm_gate_b, mlstm_gn, w_ffn1, w_ffn3, w_ffn2, w_router, w_exp1, w_exp3, w_exp2):
    nb, lat_len, d = x.shape
    ctx_len = ctx.shape[1]
    depth = w_ada.shape[0]
    n_lat_rows = nb * lat_len
    n_rows = n_lat_rows + nb * ctx_len

    x_parts = [x.reshape(n_lat_rows, d), ctx.reshape(nb * ctx_len, d)]
    w_in_t = jnp.swapaxes(w_in, 1, 2)
    s_in = jnp.zeros((8, d), F32).at[:nb].set(c).at[nb].set(c_ctx)
    mod_all = _ada(s_in, w_ada, b_ada)[:, :nb + 1].reshape(depth, nb + 1, 6, d)

    cosf, sinf = _rotary_tables(lat_len, ctx_len)
    hg_cum, hg_masks, hg_signs = _hgrn_constants()
    hg_consts = (jnp.asarray(hg_cum, BF16), jnp.asarray(hg_masks, F32), jnp.asarray(hg_signs, F32))
    sm = jax.nn.softmax(hgrn_lb.astype(F32), axis=0)
    lb_all = jnp.clip(jnp.cumsum(sm, axis=0) - sm[0], 0.0, 1.0)

    hb = None
    for layer in range(depth):
        last = layer == depth - 1
        modtab = mod_all[layer]
        g = norm_g[layer]

        if hb is None:
            hb = _prenorm(x_parts, n_rows, g[0], modtab, 1, 0, lat_len, BF16)
        p = _mm(hb, w_in_t, layer, MM_TM, MM_TN, n_cols=PROJ_MAIN)
        w_g = jnp.zeros((1, 128, d), F32).at[0, :MLSTM_GATES].set(w_in_t[layer, PROJ_MAIN:])
        gates = _mm(hb, w_g, 0, MM_TM, 128)[:, :MLSTM_GATES]
        hb = None
        g_row_lat = jnp.swapaxes(gates[:n_lat_rows].reshape(nb, lat_len, MLSTM_GATES), 1, 2)
        g_row_ctx = jnp.swapaxes(gates[n_lat_rows:].reshape(nb, ctx_len, MLSTM_GATES), 1, 2)

        log_g = jax.nn.log_sigmoid(ret_decay[layer].astype(F32))
        lg_b = jnp.broadcast_to(log_g[:, :, None, None], (2, RET_HEADS, 8, RET_DV))
        lb = lb_all[layer]
        emit_ctx = not last
        o_ret = _retention(p, nb, lat_len, ctx_len, cosf, sinf, lg_b, ret_gn[layer], emit_ctx)
        o_hg = _hgrn(p, nb, lat_len, ctx_len, jnp.log(lb), jnp.log1p(-lb), hgrn_gn[layer],
                     hg_consts, emit_ctx)
        o_ml = _mlstm(p, gates, g_row_lat, g_row_ctx, nb, lat_len, ctx_len, mlstm_gate_b[layer],
                      mlstm_conv[layer], mlstm_gn[layer], emit_ctx)
        mix_parts = [list(o) for o in (o_ret, o_hg, o_ml)]
        rows_now = n_lat_rows if last else n_rows
        dense = layer % 2 == 0
        res = _wout(mix_parts, _bf(w_out[layer]), x_parts, rows_now, g[1], modtab, lat_len,
                    g[2] if dense else None)
        xs = res[0]
        x_parts = [xs]

        j = layer // 2
        if dense:
            nxt = (None, None) if last else (norm_g[layer + 1][0], mod_all[layer + 1])
            res = _ffn(res[1], _bf(w_ffn1[j]), _bf(w_ffn3[j]), _bf(w_ffn2[j]), xs, g[3], modtab, lat_len, *nxt)
            xs = res[0]
            hb = None if last else res[1]
        else:
            xs = _moe(xs, rows_now, g[2], g[3], modtab, w_router[j], w_exp1[j], w_exp3[j], w_exp2[j], lat_len)
        x_parts = [xs]
    return xs[:n_lat_rows].reshape(nb, lat_len, d)
```

```python
import functools

import numpy as np
import jax
import jax.numpy as jnp
from jax import lax
from jax.experimental import pallas as pl
from jax.experimental.pallas import tpu as pltpu

F32 = jnp.float32
BF16 = jnp.bfloat16

CHUNK = 128
NORM_EPS = 1e-6
ROPE_BASE = 10000.0
GRID_W = 64

RET_HEADS, RET_DK, RET_DV = 4, 128, 256
HGRN_HEADS, HGRN_DK, HGRN_DV = 4, 128, 128
MLSTM_HEADS, MLSTM_DK, MLSTM_DV = 4, 128, 128
N_EXPERTS = 8
TOP_K = 2

RET_QK = RET_HEADS * RET_DK
RET_WIDTH = RET_HEADS * RET_DV
HGRN_QK = HGRN_HEADS * HGRN_DK
HGRN_WIDTH = HGRN_HEADS * HGRN_DV
MLSTM_QK = MLSTM_HEADS * MLSTM_DK
MLSTM_WIDTH = MLSTM_HEADS * MLSTM_DV
MLSTM_GATES = 4 * MLSTM_HEADS

_C_RET_Q = 0
_C_RET_K = _C_RET_Q + RET_QK // 128
_C_RET_V = _C_RET_K + RET_QK // 128
_C_RET_G = _C_RET_V + RET_WIDTH // 128
_C_HG_Q = _C_RET_G + RET_WIDTH // 128
_C_HG_FF = _C_HG_Q + HGRN_QK // 128
_C_HG_FB = _C_HG_FF + HGRN_QK // 128
_C_HG_I = _C_HG_FB + HGRN_QK // 128
_C_HG_G = _C_HG_I + HGRN_WIDTH // 128
_C_ML_Q = _C_HG_G + HGRN_WIDTH // 128
_C_ML_K = _C_ML_Q + MLSTM_QK // 128
_C_ML_V = _C_ML_K + MLSTM_QK // 128
_C_ML_O = _C_ML_V + MLSTM_WIDTH // 128
PROJ_MAIN = (_C_ML_O + MLSTM_WIDTH // 128) * 128

_VMEM_CAP_BYTES = 56 * 1024 * 1024

ROW_TILE = 512
MM_TM, MM_TN = 1024, 1280
FFN_TF = 512
MOE_TM = 720
MOE_GROUP = 1
MOE_TF = 512
RET_UNROLL = (4, 4)
HGRN_UNROLL = (2, 4)
MLSTM_UNROLL = (2, 2)
DMA_UNROLL = 8


def _cparams(sem, vmem_mb):
    return pltpu.CompilerParams(dimension_semantics=sem,
                                vmem_limit_bytes=min(int(vmem_mb * 1024 * 1024), _VMEM_CAP_BYTES))


def _bf(x):
    return x.astype(BF16)


def _dot(a, b):
    return jnp.dot(a, b, preferred_element_type=F32)


def _dot_nt(a, b):
    return lax.dot_general(a, b, (((1,), (1,)), ((), ())), preferred_element_type=F32)


def _dot_tn(a, b):
    return lax.dot_general(a, b, (((0,), (0,)), ((), ())), preferred_element_type=F32)


def _sigmoid(x):
    return 1.0 / (1.0 + jnp.exp(-x))


def _silu(x):
    return x * _sigmoid(x)


def _log_sigmoid(x):
    return jnp.minimum(x, 0.0) - jnp.log(1.0 + jnp.exp(-jnp.abs(x)))


def _rms(x, g):
    return x * lax.rsqrt(jnp.mean(x * x, axis=-1, keepdims=True) + NORM_EPS) * g


def _mod_index(tile_rows, n_lat_rows_per_batch, n_batch):
    return lambda i: jnp.minimum((i * tile_rows) // n_lat_rows_per_batch, n_batch)


def _ada_kernel(s_ref, w_ref, b_ref, o_ref):
    s = _bf(_silu(s_ref[...]))
    o_ref[0] = _dot(s, _bf(w_ref[0])) + b_ref[0]


def _ada(s_in, w_ada, b_ada):
    depth, d, n = w_ada.shape
    tn = 1024
    return pl.pallas_call(
        _ada_kernel,
        grid=(depth, n // tn),
        in_specs=[pl.BlockSpec((8, d), lambda l, j: (0, 0)),
                  pl.BlockSpec((1, d, tn), lambda l, j: (l, 0, j)),
                  pl.BlockSpec((1, 1, tn), lambda l, j: (l, 0, j))],
        out_specs=pl.BlockSpec((1, 8, tn), lambda l, j: (l, 0, j)),
        out_shape=jax.ShapeDtypeStruct((depth, 8, n), F32),
        compiler_params=_cparams(("arbitrary", "arbitrary"), 40),
        name="ada",
    )(s_in, w_ada, b_ada.reshape(depth, 1, n))


def _part_specs(parts, tm, width):
    if len(parts) == 1:
        return [pl.BlockSpec((tm, width), lambda i: (i, 0))]
    n0 = parts[0].shape[0] // tm
    return [pl.BlockSpec((tm, width), lambda i: (jnp.minimum(i, n0 - 1), 0)),
            pl.BlockSpec((tm, width), lambda i: (jnp.maximum(i - n0, 0), 0))]


def _on_part(n_first, n_parts, fn):
    if n_parts == 1:
        fn(0)
    else:
        i = pl.program_id(0)
        pl.when(i < n_first)(lambda: fn(0))
        pl.when(i >= n_first)(lambda: fn(1))


def _prenorm_kernel(*refs, sc, sh, n_first):
    x_refs, (g_ref, mod_ref, o_ref) = refs[:-3], refs[-3:]

    def run(part):
        m = mod_ref[0]
        y = _rms(x_refs[part][...], g_ref[...])
        o_ref[...] = (y * (1.0 + m[sc:sc + 1]) + m[sh:sh + 1]).astype(o_ref.dtype)

    _on_part(n_first, len(x_refs), run)


def _prenorm(x_parts, n_rows, g, modtab, sc, sh, lat_rows, out_dtype):
    d = x_parts[0].shape[1]
    nb = modtab.shape[0] - 1
    mi = _mod_index(ROW_TILE, lat_rows, nb)
    return pl.pallas_call(
        functools.partial(_prenorm_kernel, sc=sc, sh=sh, n_first=x_parts[0].shape[0] // ROW_TILE),
        grid=(n_rows // ROW_TILE,),
        in_specs=[*_part_specs(x_parts, ROW_TILE, d),
                  pl.BlockSpec((1, d), lambda i: (0, 0)),
                  pl.BlockSpec((1, 6, d), lambda i: (mi(i), 0, 0))],
        out_specs=pl.BlockSpec((ROW_TILE, d), lambda i: (i, 0)),
        out_shape=jax.ShapeDtypeStruct((n_rows, d), out_dtype),
        compiler_params=_cparams(("arbitrary",), 32),
        name="prenorm",
    )(*x_parts, g.reshape(1, d), modtab)


def _mm_kernel(x_ref, wt_ref, o_ref, wb_s):
    @pl.when(pl.program_id(1) == 0)
    def _():
        wb_s[...] = _bf(wt_ref[0])

    o_ref[...] = _dot_nt(x_ref[...], wb_s[...]).astype(o_ref.dtype)


def _mm(x, wt, layer, tm, tn, n_cols=None, out_dtype=F32):
    m, k = x.shape
    n = wt.shape[1] if n_cols is None else n_cols
    return pl.pallas_call(
        _mm_kernel,
        grid=(n // tn, m // tm),
        in_specs=[pl.BlockSpec((tm, k), lambda j, i: (i, 0)),
                  pl.BlockSpec((1, tn, k), lambda j, i: (layer, j, 0))],
        out_specs=pl.BlockSpec((tm, tn), lambda j, i: (i, j)),
        out_shape=jax.ShapeDtypeStruct((m, n), out_dtype),
        scratch_shapes=[pltpu.VMEM((tn, k), BF16)],
        compiler_params=_cparams(("arbitrary", "arbitrary"), 48),
        name="proj",
    )(x, wt)


def _bwd_chunk(i, n_ctx, n_lat):
    return jnp.where(i < n_ctx, n_ctx - 1 - i, 2 * n_ctx + n_lat - 1 - i)


def _rows(c):
    return pl.ds(pl.multiple_of(c * CHUNK, CHUNK), CHUNK)


def _for_chunks(n_ctx, n_lat, fn, unroll, with_ctx=True):
    def run(part, n, off):
        def body(c, carry):
            fn(part, c, c + off)
            return carry
        lax.fori_loop(0, n, body, 0, unroll=max(u for u in (1, 2, unroll) if n % u == 0 and u <= unroll))
    if with_ctx:
        run(0, n_ctx, 0)
    run(1, n_lat, n_ctx)


def _lanes(d, w):
    return slice(d * w, (d + 1) * w)


def _mixer_call(kernel_fn, name, in_specs, args, nb, heads, lat_len, ctx_len, dv, scratch, emit_ctx, vmem_mb):
    width = heads * dv
    out_specs = [pl.BlockSpec((lat_len, dv), lambda b, h: (b, h))]
    out_shape = [jax.ShapeDtypeStruct((nb * lat_len, width), BF16)]
    if emit_ctx:
        out_specs.append(pl.BlockSpec((ctx_len, dv), lambda b, h: (b, h)))
        out_shape.append(jax.ShapeDtypeStruct((nb * ctx_len, width), BF16))
    n_in = len(in_specs)

    def body(*refs):
        ins, rest = refs[:n_in], refs[n_in:]
        if emit_ctx:
            ol, oc, scr = rest[0], rest[1], rest[2:]
        else:
            ol, oc, scr = rest[0], None, rest[1:]
        kernel_fn(*ins, ol, oc, *scr, n_ctx=ctx_len // CHUNK, n_lat=lat_len // CHUNK)

    return pl.pallas_call(
        body,
        grid=(nb, heads),
        in_specs=in_specs,
        out_specs=out_specs,
        out_shape=out_shape,
        scratch_shapes=scratch,
        compiler_params=_cparams(("arbitrary", "arbitrary"), vmem_mb),
        name=name,
    )(*args)


def _seq_specs(col0, width, nb, lat_len, ctx_len):
    n_lat_blk = nb * lat_len // ctx_len
    c0 = col0 * 128 // width
    return (pl.BlockSpec((lat_len, width), lambda b, h: (b, c0 + h)),
            pl.BlockSpec((ctx_len, width), lambda b, h: (n_lat_blk + b, c0 + h)))


def _ret_kernel(ql, qc, kl, kc, vl, vc, gl, gc, cos_ref, sin_ref, lg_ref, gn_ref, ol_ref, oc_ref,
                qs_s, kv_s, sall_s, o_s, st_s, dm_s, dq_s, dk_s, *, n_ctx, n_lat):
    nch = n_ctx + n_lat
    unroll_a, unroll_c = RET_UNROLL
    qr, kr, vr, gr, outr = (qc, ql), (kc, kl), (vc, vl), (gc, gl), (oc_ref, ol_ref)
    t_i = lax.broadcasted_iota(jnp.int32, (CHUNK, CHUNK), 0).astype(F32)
    s_i = lax.broadcasted_iota(jnp.int32, (CHUNK, CHUNK), 1).astype(F32)
    dm = None
    for d in range(2):
        lg = lg_ref[d, 0][0:1, 0:CHUNK]
        rel = (t_i - s_i) if d == 0 else (s_i - t_i)
        dmd = jnp.where(rel >= 0, jnp.exp(jnp.maximum(rel, 0.0) * lg), 0.0)
        dm = dmd if dm is None else dm + dmd
        p = t_i if d == 0 else (CHUNK - 1.0) - t_i
        dq_s[d] = jnp.exp((p + 1.0) * lg)
        dk_s[d] = jnp.exp((CHUNK - 1.0 - p) * lg)
    dm_s[...] = dm
    scale = RET_DK ** -0.5

    def phase_a(part, cl, cg):
        rl, rg = _rows(cl), _rows(cg)
        cs = cos_ref[rg, :]
        sn = sin_ref[rg, :]
        q = qr[part][rl, :]
        k = kr[part][rl, :]
        q = q * cs + pltpu.roll(q, RET_DK // 2, 1) * sn
        k = (k * cs + pltpu.roll(k, RET_DK // 2, 1) * sn) * scale
        vb = _bf(vr[part][rl, :])
        s = _dot_nt(_bf(q), _bf(k))
        o_s[rg, :] = _dot(_bf(s * dm_s[...]), vb)
        qs_s[rg, :] = jnp.concatenate([_bf(q * dq_s[0]), _bf(q * dq_s[1])], axis=1)
        kd = jnp.concatenate([_bf(k * dk_s[0]), _bf(k * dk_s[1])], axis=1)
        kv_s[cg] = _dot_tn(kd, vb)

    _for_chunks(n_ctx, n_lat, phase_a, unroll_a)

    st_s[...] = jnp.zeros(st_s.shape, F32)
    dec = [jnp.exp(float(CHUNK) * lg_ref[d, 0][0:1, :]) for d in range(2)]

    def phase_b(i, carry):
        for d, c in ((0, i), (1, _bwd_chunk(i, n_ctx, n_lat))):
            st = st_s[d]
            sall_s[c, _lanes(d, RET_DK), :] = _bf(st)
            st_s[d] = dec[d] * st + kv_s[c, _lanes(d, RET_DK), :]
        return carry

    lax.fori_loop(0, nch, phase_b, 0)
    gn = gn_ref[...]

    def phase_c(part, cl, cg):
        rl, rg = _rows(cl), _rows(cg)
        o = o_s[rg, :] + _dot(qs_s[rg, :], sall_s[cg])
        outr[part][rl, :] = (_rms(o, gn) * _silu(gr[part][rl, :])).astype(BF16)

    _for_chunks(n_ctx, n_lat, phase_c, unroll_c, with_ctx=oc_ref is not None)


def _retention(p, nb, lat_len, ctx_len, cosf, sinf, lg_b, gn, emit_ctx):
    t_len = lat_len + ctx_len
    nch = t_len // CHUNK
    s128 = functools.partial(_seq_specs, width=128, nb=nb, lat_len=lat_len, ctx_len=ctx_len)
    s256 = functools.partial(_seq_specs, width=256, nb=nb, lat_len=lat_len, ctx_len=ctx_len)
    in_specs = [*s128(_C_RET_Q), *s128(_C_RET_K), *s256(_C_RET_V), *s256(_C_RET_G),
                pl.BlockSpec((t_len, 128), lambda b, h: (0, 0)),
                pl.BlockSpec((t_len, 128), lambda b, h: (0, 0)),
                pl.BlockSpec((2, 1, 8, RET_DV), lambda b, h: (0, h, 0, 0)),
                pl.BlockSpec((1, RET_DV), lambda b, h: (0, h))]
    scratch = [pltpu.VMEM((t_len, 2 * RET_DK), BF16),
               pltpu.VMEM((nch, 2 * RET_DK, RET_DV), F32),
               pltpu.VMEM((nch, 2 * RET_DK, RET_DV), BF16),
               pltpu.VMEM((t_len, RET_DV), F32),
               pltpu.VMEM((2, RET_DK, RET_DV), F32),
               pltpu.VMEM((CHUNK, CHUNK), F32),
               pltpu.VMEM((2, CHUNK, CHUNK), F32), pltpu.VMEM((2, CHUNK, CHUNK), F32)]
    args = (p, p, p, p, p, p, p, p, cosf, sinf, lg_b, gn.reshape(1, RET_WIDTH))
    return _mixer_call(_ret_kernel, "retention", in_specs, args, nb, RET_HEADS, lat_len, ctx_len, RET_DV,
                       scratch, emit_ctx, 48)


_HG_LEVELS = (64, 32, 16, 8, 4, 2, 1)


def _hgrn_constants():
    c = CHUNK
    t = np.arange(c)[:, None]
    u = np.arange(c)[None, :]
    cum = (u <= t).astype(np.float32)
    masks, signs = [], []
    for m in _HG_LEVELS:
        base = (t // (2 * m)) * (2 * m)
        lower = t >= base + m
        tb = t // (2 * m)
        sb = u // (2 * m)
        masks.append((tb == sb) & lower & (u < (sb * 2 * m + m)))
        signs.append(np.broadcast_to(np.where(lower, 1.0, -1.0), (c, c)))
    masks.append(t == u)
    kf = np.stack([x.astype(np.float32) for x in masks], axis=0)
    kb = np.stack([x.astype(np.float32)[::-1, ::-1] for x in masks], axis=0)
    sf = np.stack([x.astype(np.float32) for x in signs], axis=0)
    sb_ = np.stack([x.astype(np.float32)[::-1, ::-1] for x in signs], axis=0)
    return (np.stack([cum, cum[::-1, ::-1]], 0), np.stack([kf, kb], 0),
            np.stack([sf[:_HG_WIDE], sb_[:_HG_WIDE]], 0))


_HG_WIDE = 5


def _hgrn_level_exponents(b, lf, d, sgn_ref, row):
    out = []
    for l, m in enumerate(_HG_LEVELS[:_HG_WIDE]):
        pieces = []
        for j in range(CHUNK // (2 * m)):
            r = 2 * m * j + (m - 1 if d == 0 else m)
            pieces.append(jnp.broadcast_to(b[r:r + 1, :], (2 * m, CHUNK)))
        bref = pieces[0] if len(pieces) == 1 else jnp.concatenate(pieces, axis=0)
        out.append((b - bref) * sgn_ref[d, l])
    up = pltpu.roll(lf, CHUNK - 1, 0)
    dn = pltpu.roll(lf, 1, 0)
    r4 = row % 4
    if d == 0:
        e2 = jnp.where(r4 == 0, up, jnp.where(r4 == 1, 0.0, jnp.where(r4 == 2, lf, lf + dn)))
        e1 = jnp.where(row % 2 == 1, lf, 0.0)
    else:
        e2 = jnp.where(r4 == 0, lf + up, jnp.where(r4 == 1, lf, jnp.where(r4 == 2, 0.0, dn)))
        e1 = jnp.where(row % 2 == 0, lf, 0.0)
    return out + [e2, e1]


def _hgrn_kernel(ql, qc, ffl, ffc, fbl, fbc, il, ic, gl, gc, llb_ref, lub_ref, gn_ref, c_ref, k_ref, sgn_ref,
                 ol_ref, oc_ref, qs_s, kv_s, sall_s, dec_s, o_s, st_s, *, n_ctx, n_lat):
    nch = n_ctx + n_lat
    unroll_a, unroll_c = HGRN_UNROLL
    qr, fr, ir, gr, outr =(qc, ql), ((ffc, ffl), (fbc, fbl)), (ic, il), (gc, gl), (oc_ref, ol_ref)
    llb = llb_ref[...]
    lub = lub_ref[...]
    nlev = len(_HG_LEVELS)
    row = lax.broadcasted_iota(jnp.int32, (CHUNK, HGRN_DK), 0)

    def phase_a(part, cl, cg):
        rl, rg = _rows(cl), _rows(cg)
        q = _silu(qr[part][rl, :])
        qb = _bf(q)
        vb = _bf(ir[part][rl, :])
        a_sum, qs, kds = None, [], []
        for d in range(2):
            lsg = lub + _log_sigmoid(fr[d][part][rl, :])
            lf = jnp.maximum(llb, lsg) + jnp.log(1.0 + jnp.exp(-jnp.abs(llb - lsg)))
            k = 1.0 - jnp.exp(lf)
            lf_hi = _bf(lf)
            lf_lo = _bf(lf - lf_hi.astype(F32))
            b2 = _dot(c_ref[d], jnp.concatenate([lf_hi, lf_lo], axis=1))
            b = b2[:, 0:HGRN_DK] + b2[:, HGRN_DK:]
            es = _hgrn_level_exponents(b, lf, d, sgn_ref, row)
            kb = _bf(k)
            a = k_ref[d, nlev] * _dot_nt(qb, kb)
            for l in range(nlev):
                xb = _bf(jnp.exp(es[l]))
                a = a + k_ref[d, l] * _dot_nt(qb * xb, kb * xb)
            a_sum = a if a_sum is None else a_sum + a
            b_last = b[CHUNK - 1:CHUNK] if d == 0 else b[0:1]
            qs.append(_bf(q * jnp.exp(b)))
            kds.append(_bf(k * jnp.exp(b_last - b)))
            dec_s[cg, :, _lanes(d, HGRN_DK)] = jnp.exp(b_last)
        o_s[rg, :] = _dot(_bf(a_sum), vb)
        qs_s[rg, :] = jnp.concatenate(qs, axis=1)
        kv_s[cg] = _dot_tn(vb, jnp.concatenate(kds, axis=1))

    _for_chunks(n_ctx, n_lat, phase_a, unroll_a)
    st_s[...] = jnp.zeros(st_s.shape, F32)

    def phase_b(i, carry):
        for d, c in ((0, i), (1, _bwd_chunk(i, n_ctx, n_lat))):
            ln = _lanes(d, HGRN_DK)
            st = st_s[d]
            sall_s[c, :, ln] = _bf(st)
            st_s[d] = dec_s[c, :, ln] * st + kv_s[c, :, ln]
        return carry

    lax.fori_loop(0, nch, phase_b, 0)
    gn = gn_ref[...]

    def phase_c(part, cl, cg):
        rl, rg = _rows(cl), _rows(cg)
        o = o_s[rg, :] + _dot_nt(qs_s[rg, :], sall_s[cg])
        outr[part][rl, :] = (_rms(o, gn) * _silu(gr[part][rl, :])).astype(BF16)

    _for_chunks(n_ctx, n_lat, phase_c, unroll_c, with_ctx=oc_ref is not None)


def _hgrn(p, nb, lat_len, ctx_len, log_lb, log_ub, gn, consts, emit_ctx):
    mats, masks, signs = consts
    t_len = lat_len + ctx_len
    nch = t_len // CHUNK
    spec = functools.partial(_seq_specs, width=128, nb=nb, lat_len=lat_len, ctx_len=ctx_len)
    vec = pl.BlockSpec((1, 128), lambda b, h: (0, h))
    in_specs = [*spec(_C_HG_Q), *spec(_C_HG_FF), *spec(_C_HG_FB), *spec(_C_HG_I), *spec(_C_HG_G),
                vec, vec, vec,
                pl.BlockSpec(mats.shape, lambda b, h: (0, 0, 0)),
                pl.BlockSpec(masks.shape, lambda b, h: (0, 0, 0, 0)),
                pl.BlockSpec(signs.shape, lambda b, h: (0, 0, 0, 0))]
    scratch = [pltpu.VMEM((t_len, 2 * HGRN_DK), BF16),
               pltpu.VMEM((nch, HGRN_DV, 2 * HGRN_DK), F32),
               pltpu.VMEM((nch, HGRN_DV, 2 * HGRN_DK), BF16),
               pltpu.VMEM((nch, 1, 2 * HGRN_DK), F32),
               pltpu.VMEM((t_len, HGRN_DV), F32),
               pltpu.VMEM((2, HGRN_DV, HGRN_DK), F32)]
    args = (p, p, p, p, p, p, p, p, p, p, log_lb.reshape(1, HGRN_QK), log_ub.reshape(1, HGRN_QK),
            gn.reshape(1, HGRN_WIDTH), mats, masks, signs)
    return _mixer_call(_hgrn_kernel, "hgrn2", in_specs, args, nb, HGRN_HEADS, lat_len, ctx_len, HGRN_DV,
                       scratch, emit_ctx, 48)


def _mlstm_kernel(ql, qc, kl, kc, vl, vc, ogl, ogc, gcl, gcc, grl, grc, bc_ref, br_ref,
                  wq_ref, wk_ref, gn_ref, ol_ref, oc_ref,
                  q_s, gr_s, intra_s, kv_s, call_s, rho_s, bcol_s, bl_s, mu_s, mprev_s, st_s, m_s,
                  *, n_ctx, n_lat):
    nch = n_ctx + n_lat
    unroll_a, unroll_c = MLSTM_UNROLL
    h = pl.program_id(1)
    nh = MLSTM_HEADS
    dk, dv = MLSTM_DK, MLSTM_DV
    ext = 2 * dv
    qr, kr, vr, ogr, gcr, outr = (qc, ql), (kc, kl), (vc, vl), (ogc, ogl), (gcc, gcl), (oc_ref, ol_ref)
    n_loc = (n_ctx, n_lat)

    for cc in range(nch):
        src, c0 = (grc, cc) if cc < n_ctx else (grl, cc - n_ctx)
        gr_s[cc] = src[0, :, c0 * CHUNK:(c0 + 1) * CHUNK] + br_ref[...]

    row = lax.broadcasted_iota(jnp.int32, (CHUNK, dk), 0)
    lane = lax.broadcasted_iota(jnp.int32, (CHUNK, dv), 1)
    del lane
    ones_col = jnp.ones((CHUNK, dv), BF16)
    scale = dk ** -0.5
    t_i = lax.broadcasted_iota(jnp.int32, (CHUNK, CHUNK), 0)
    s_i = lax.broadcasted_iota(jnp.int32, (CHUNK, CHUNK), 1)
    lane16 = lax.broadcasted_iota(jnp.int32, (CHUNK, MLSTM_GATES), 1)
    sub16 = lax.broadcasted_iota(jnp.int32, (MLSTM_GATES, CHUNK), 0)

    def conv(src, w_ref, part, cl):
        r0 = pl.multiple_of(cl * CHUNK, CHUNK)
        n_rows = n_loc[part] * CHUNK
        x = src[pl.ds(r0, CHUNK), :]
        pr = src[pl.ds(jnp.maximum(r0 - 1, 0), 1), :]
        nx = src[pl.ds(jnp.minimum(r0 + CHUNK, n_rows - 1), 1), :]
        pr = jnp.where(cl != 0, pr, 0.0)
        nx = jnp.where(cl != n_loc[part] - 1, nx, 0.0)
        xp = jnp.where(row == 0, pr, pltpu.roll(x, 1, 0))
        xn = jnp.where(row == CHUNK - 1, nx, pltpu.roll(x, CHUNK - 1, 0))
        w = w_ref[...]
        return _silu(w[0:1] * xp + w[1:2] * x + w[2:3] * xn)

    def pick_col(g, j):
        return jnp.sum(jnp.where(lane16 == j, g, 0.0), axis=1, keepdims=True)

    def pick_row(g, j):
        return jnp.sum(jnp.where(sub16 == j, g, 0.0), axis=0, keepdims=True)

    def phase_a(part, cl, cg):
        rl, rg = _rows(cl), _rows(cg)
        q = conv(qr[part], wq_ref, part, cl)
        k = conv(kr[part], wk_ref, part, cl) * scale
        qb = _bf(q)
        q_s[rg, :] = qb
        vb = jnp.concatenate([_bf(vr[part][rl, :]), ones_col], axis=1)
        s = _dot_nt(qb, _bf(k))
        g_c = gcr[part][rl, :] + bc_ref[...]
        g_r = gr_s[cg]
        kws = []
        for d in range(2):
            ig_c = pick_col(g_c, d * nh + h)
            lf_c = _log_sigmoid(pick_col(g_c, 2 * nh + d * nh + h))
            ig_r = pick_row(g_r, d * nh + h)
            lf_r = _log_sigmoid(pick_row(g_r, 2 * nh + d * nh + h))
            tri = (s_i <= t_i) if d == 0 else (s_i >= t_i)
            tri_t = (t_i <= s_i) if d == 0 else (t_i >= s_i)
            b_c = jnp.sum(jnp.where(tri, lf_r, 0.0), axis=1, keepdims=True)
            b_r = jnp.sum(jnp.where(tri_t, lf_c, 0.0), axis=0, keepdims=True)
            dmat = jnp.where(tri, b_c - b_r + ig_r, -jnp.inf)
            rho = jnp.max(dmat, axis=1, keepdims=True)
            intra_s[d, rg, :] = _dot(_bf(s * jnp.exp(dmat - rho)), vb)
            b_last = b_c[CHUNK - 1:CHUNK] if d == 0 else b_c[0:1]
            mu = jnp.max(b_last - b_r + ig_r, axis=1, keepdims=True)
            kws.append(_bf(k * jnp.exp(b_last - b_c + ig_c - mu)))
            rho_s[d, rg, :] = jnp.broadcast_to(rho, (CHUNK, 128))
            bcol_s[d, rg, :] = jnp.broadcast_to(b_c, (CHUNK, 128))
            bl_s[cg, :, _lanes(d, 128)] = jnp.broadcast_to(b_last, (1, 128))
            mu_s[cg, :, _lanes(d, 128)] = jnp.broadcast_to(mu, (1, 128))
        kv_s[cg] = _dot_tn(jnp.concatenate(kws, axis=1), vb)

    _for_chunks(n_ctx, n_lat, phase_a, unroll_a)
    st_s[...] = jnp.zeros(st_s.shape, F32)
    m_s[...] = jnp.zeros(m_s.shape, F32)

    def wide(v):
        return jnp.concatenate([v, v], axis=1)

    def phase_b(i, carry):
        for d, c in ((0, i), (1, _bwd_chunk(i, n_ctx, n_lat))):
            ln = _lanes(d, 128)
            m_prev = m_s[d]
            st = st_s[d]
            mprev_s[c, :, ln] = m_prev
            call_s[c, :, _lanes(d, ext)] = _bf(st)
            bl = bl_s[c, :, ln]
            mu = mu_s[c, :, ln]
            m_new = jnp.maximum(bl + m_prev, mu)
            st_s[d] = (wide(jnp.exp(bl + m_prev - m_new)) * st
                       + wide(jnp.exp(mu - m_new)) * kv_s[c, _lanes(d, dk), :])
            m_s[d] = m_new
        return carry

    lax.fori_loop(0, nch, phase_b, 0)
    gn = gn_ref[...]

    def phase_c(part, cl, cg):
        rl, rg = _rows(cl), _rows(cg)
        qc_all = _dot(q_s[rg, :], call_s[cg])
        o = None
        for d in range(2):
            rho = rho_s[d, rg, :]
            b_c = bcol_s[d, rg, :]
            m_prev = mprev_s[cg, :, _lanes(d, 128)]
            m_t = jnp.maximum(rho, b_c + m_prev)
            nd = (wide(jnp.exp(rho - m_t)) * intra_s[d, rg, :]
                  + wide(jnp.exp(b_c + m_prev - m_t)) * qc_all[:, _lanes(d, ext)])
            hh = nd[:, 0:dv] / jnp.maximum(jnp.abs(nd[:, dv:]), jnp.exp(-m_t))
            o = hh if o is None else o + hh
        y = o - jnp.mean(o, axis=-1, keepdims=True)
        y = y * lax.rsqrt(jnp.mean(y * y, axis=-1, keepdims=True) + NORM_EPS)
        outr[part][rl, :] = (y * gn * _sigmoid(ogr[part][rl, :])).astype(BF16)

    _for_chunks(n_ctx, n_lat, phase_c, unroll_c, with_ctx=oc_ref is not None)


def _mlstm(p, g_col, g_row_lat, g_row_ctx, nb, lat_len, ctx_len, gate_b, conv_w, gn, emit_ctx):
    t_len = lat_len + ctx_len
    nch = t_len // CHUNK
    n_lat_blk = nb * lat_len // ctx_len
    ng = MLSTM_GATES
    dk, dv = MLSTM_DK, MLSTM_DV
    spec = functools.partial(_seq_specs, width=128, nb=nb, lat_len=lat_len, ctx_len=ctx_len)
    in_specs = [*spec(_C_ML_Q), *spec(_C_ML_K), *spec(_C_ML_V), *spec(_C_ML_O),
                pl.BlockSpec((lat_len, ng), lambda b, h: (b, 0)),
                pl.BlockSpec((ctx_len, ng), lambda b, h: (n_lat_blk + b, 0)),
                pl.BlockSpec((1, ng, lat_len), lambda b, h: (b, 0, 0)),
                pl.BlockSpec((1, ng, ctx_len), lambda b, h: (b, 0, 0)),
                pl.BlockSpec((1, ng), lambda b, h: (0, 0)),
                pl.BlockSpec((ng, 1), lambda b, h: (0, 0)),
                pl.BlockSpec((3, 128), lambda b, h: (0, h)),
                pl.BlockSpec((3, 128), lambda b, h: (0, MLSTM_HEADS + h)),
                pl.BlockSpec((1, 128), lambda b, h: (0, h))]
    scratch = [pltpu.VMEM((t_len, dk), BF16),
               pltpu.VMEM((nch, ng, CHUNK), F32),
               pltpu.VMEM((2, t_len, 2 * dv), F32),
               pltpu.VMEM((nch, 2 * dk, 2 * dv), F32),
               pltpu.VMEM((nch, dk, 4 * dv), BF16),
               pltpu.VMEM((2, t_len, 128), F32), pltpu.VMEM((2, t_len, 128), F32),
               pltpu.VMEM((nch, 1, 256), F32), pltpu.VMEM((nch, 1, 256), F32), pltpu.VMEM((nch, 1, 256), F32),
               pltpu.VMEM((2, dk, 2 * dv), F32), pltpu.VMEM((2, 1, 128), F32)]
    args = (p, p, p, p, p, p, p, p, g_col, g_col, g_row_lat, g_row_ctx,
            gate_b.reshape(1, ng), gate_b.reshape(ng, 1), conv_w, conv_w, gn.reshape(1, MLSTM_WIDTH))
    return _mixer_call(_mlstm_kernel, "mlstm", in_specs, args, nb, MLSTM_HEADS, lat_len, ctx_len, dv,
                       scratch, emit_ctx, 48)


def _wout_kernel(*refs, n_lat_tiles, n_mix_parts, n_x_parts, emit_next):
    it = iter(refs)
    mix = [[next(it) for _ in range(n_mix_parts)] for _ in range(3)]
    x_parts = [next(it) for _ in range(n_x_parts)]
    wr_ref, wh_ref, wm_ref, g_ref, mod_ref = (next(it) for _ in range(5))
    g2_ref = next(it) if emit_next else None
    o_ref = next(it)
    h2_ref = next(it) if emit_next else None

    def run(part):
        pm = min(part, n_mix_parts - 1)
        y = (_dot(mix[0][pm][...], wr_ref[...]) + _dot(mix[1][pm][...], wh_ref[...])
             + _dot(mix[2][pm][...], wm_ref[...]))
        m = mod_ref[0]
        xn = x_parts[min(part, n_x_parts - 1)][...] + m[2:3] * _rms(y, g_ref[...])
        o_ref[...] = xn
        if emit_next:
            h2_ref[...] = (_rms(xn, g2_ref[...]) * (1.0 + m[4:5]) + m[3:4]).astype(h2_ref.dtype)

    _on_part(n_lat_tiles, max(n_mix_parts, n_x_parts), run)


def _wout(mix_parts, w_out, x_parts, n_rows, g1, modtab, lat_rows, g2_next):
    d = x_parts[0].shape[1]
    nb = modtab.shape[0] - 1
    mi = _mod_index(ROW_TILE, lat_rows, nb)
    tm = ROW_TILE
    n_lat_tiles = mix_parts[0][0].shape[0] // tm
    emit_next = g2_next is not None
    widths = (RET_WIDTH, HGRN_WIDTH, MLSTM_WIDTH)
    w_r = w_out[0:RET_WIDTH]
    w_h = w_out[RET_WIDTH:RET_WIDTH + HGRN_WIDTH]
    w_m = w_out[RET_WIDTH + HGRN_WIDTH:]
    full = lambda a: pl.BlockSpec(a.shape, lambda i: (0, 0))
    in_specs, args = [], []
    for parts, w in zip(mix_parts, widths):
        in_specs += _part_specs(parts, tm, w)
        args += list(parts)
    in_specs += _part_specs(x_parts, tm, d)
    args += list(x_parts)
    in_specs += [full(w_r), full(w_h), full(w_m),
                 pl.BlockSpec((1, d), lambda i: (0, 0)),
                 pl.BlockSpec((1, 6, d), lambda i: (mi(i), 0, 0))]
    args += [w_r, w_h, w_m, g1.reshape(1, d), modtab]
    out_specs = [pl.BlockSpec((tm, d), lambda i: (i, 0))]
    out_shape = [jax.ShapeDtypeStruct((n_rows, d), F32)]
    if emit_next:
        in_specs.append(pl.BlockSpec((1, d), lambda i: (0, 0)))
        args.append(g2_next.reshape(1, d))
        out_specs.append(pl.BlockSpec((tm, d), lambda i: (i, 0)))
        out_shape.append(jax.ShapeDtypeStruct((n_rows, d), BF16))
    return pl.pallas_call(
        functools.partial(_wout_kernel, n_lat_tiles=n_lat_tiles, n_mix_parts=len(mix_parts[0]),
                          n_x_parts=len(x_parts), emit_next=emit_next),
        grid=(n_rows // tm,),
        in_specs=in_specs,
        out_specs=out_specs,
        out_shape=out_shape,
        compiler_params=_cparams(("arbitrary",), 52),
        name="wout",
    )(*args)


def _ffn_kernel(*refs, emit_next):
    if emit_next:
        h_ref, w1_ref, w3_ref, w2_ref, x_ref, g_ref, mod_ref, gn_ref, modn_ref, o_ref, hn_ref, acc_ref = refs
    else:
        h_ref, w1_ref, w3_ref, w2_ref, x_ref, g_ref, mod_ref, o_ref, acc_ref = refs
    f = pl.program_id(1)

    @pl.when(f == 0)
    def _():
        acc_ref[...] = jnp.zeros(acc_ref.shape, F32)

    h = h_ref[...]
    u = _silu(_dot(h, w1_ref[...])) * _dot(h, w3_ref[...])
    acc_ref[...] += _dot(_bf(u), w2_ref[...])

    @pl.when(f == pl.num_programs(1) - 1)
    def _():
        m = mod_ref[0]
        xn = x_ref[...] + m[5:6] * _rms(acc_ref[...], g_ref[...])
        o_ref[...] = xn
        if emit_next:
            mn = modn_ref[0]
            hn_ref[...] = (_rms(xn, gn_ref[...]) * (1.0 + mn[1:2]) + mn[0:1]).astype(hn_ref.dtype)


def _ffn(hb, w1, w3, w2, xs, g3, modtab, lat_rows, g_next, modtab_next):
    n_rows, d = hb.shape
    dff = w1.shape[1]
    nb = modtab.shape[0] - 1
    tm, tf = ROW_TILE, FFN_TF
    mi = _mod_index(tm, lat_rows, nb)
    emit_next = g_next is not None
    nf = dff // tf

    def fs(i, f):
        return jnp.where(i % 2 == 1, nf - 1 - f, f)

    in_specs = [pl.BlockSpec((tm, d), lambda i, f: (i, 0)),
                pl.BlockSpec((d, tf), lambda i, f: (0, fs(i, f))),
                pl.BlockSpec((d, tf), lambda i, f: (0, fs(i, f))),
                pl.BlockSpec((tf, d), lambda i, f: (fs(i, f), 0)),
                pl.BlockSpec((tm, d), lambda i, f: (i, 0)),
                pl.BlockSpec((1, d), lambda i, f: (0, 0)),
                pl.BlockSpec((1, 6, d), lambda i, f: (mi(i), 0, 0))]
    args = [hb, w1, w3, w2, xs, g3.reshape(1, d), modtab]
    out_specs = [pl.BlockSpec((tm, d), lambda i, f: (i, 0))]
    out_shape = [jax.ShapeDtypeStruct((n_rows, d), F32)]
    if emit_next:
        in_specs += [pl.BlockSpec((1, d), lambda i, f: (0, 0)),
                     pl.BlockSpec((1, 6, d), lambda i, f: (mi(i), 0, 0))]
        args += [g_next.reshape(1, d), modtab_next]
        out_specs.append(pl.BlockSpec((tm, d), lambda i, f: (i, 0)))
        out_shape.append(jax.ShapeDtypeStruct((n_rows, d), BF16))
    return pl.pallas_call(
        functools.partial(_ffn_kernel, emit_next=emit_next),
        grid=(n_rows // tm, dff // tf),
        in_specs=in_specs,
        out_specs=out_specs,
        out_shape=out_shape,
        scratch_shapes=[pltpu.VMEM((tm, d), F32)],
        compiler_params=_cparams(("arbitrary", "arbitrary"), 48),
        name="ffn",
    )(*args)


_META_E0, _META_E1, _META_R0, _META_R1, _META_G0, _META_G1 = range(6)


def _router_kernel(x_ref, g_ref, mod_ref, wr_ref, tri_ref, meta_ref, cnt_ref, carry_ref):
    i = pl.program_id(0)

    @pl.when(i == 0)
    def _():
        carry_ref[...] = jnp.zeros(carry_ref.shape, F32)

    m = mod_ref[0]
    hmod = _rms(x_ref[...], g_ref[...]) * (1.0 + m[4:5]) + m[3:4]
    logits = jnp.dot(hmod, wr_ref[...], precision=lax.Precision.HIGHEST, preferred_element_type=F32)
    lane = lax.broadcasted_iota(jnp.int32, logits.shape, 1)
    lanef = lane.astype(F32)
    logits = jnp.where(lane < N_EXPERTS, logits, -jnp.inf)
    v0 = jnp.max(logits, axis=1, keepdims=True)
    e0 = jnp.min(jnp.where(logits == v0, lanef, 1e9), axis=1, keepdims=True)
    rest = jnp.where(lanef == e0, -jnp.inf, logits)
    v1 = jnp.max(rest, axis=1, keepdims=True)
    e1 = jnp.min(jnp.where(rest == v1, lanef, 1e9), axis=1, keepdims=True)
    ex = jnp.exp(v1 - v0)
    g0 = 1.0 / (1.0 + ex)
    g1 = ex / (1.0 + ex)
    oh0 = lanef == e0
    oh1 = lanef == e1
    oh = jnp.where(jnp.logical_or(oh0, oh1), 1.0, 0.0)
    before = _dot(tri_ref[...], _bf(oh)) + carry_ref[0:1, :]
    r0 = jnp.sum(jnp.where(oh0, before, 0.0), axis=1, keepdims=True)
    r1 = jnp.sum(jnp.where(oh1, before, 0.0), axis=1, keepdims=True)
    carry_ref[0:1, :] = carry_ref[0:1, :] + jnp.sum(oh, axis=0, keepdims=True)
    meta = jnp.zeros(logits.shape, F32)
    for j, val in ((_META_E0, e0), (_META_E1, e1), (_META_R0, r0), (_META_R1, r1),
                   (_META_G0, g0), (_META_G1, g1)):
        meta = jnp.where(lane == j, val, meta)
    meta_ref[...] = meta
    cnt_ref[...] = carry_ref[...]


def _router(xs, n_rows, g2, modtab, w_router, lat_rows):
    d = xs.shape[1]
    nb = modtab.shape[0] - 1
    tm = ROW_TILE
    mi = _mod_index(tm, lat_rows, nb)
    wr = jnp.zeros((d, 128), F32).at[:, :N_EXPERTS].set(w_router)
    tri = jnp.asarray(np.tril(np.ones((tm, tm), np.float32), -1), BF16)
    return pl.pallas_call(
        _router_kernel,
        grid=(n_rows // tm,),
        in_specs=[pl.BlockSpec((tm, d), lambda i: (i, 0)),
                  pl.BlockSpec((1, d), lambda i: (0, 0)),
                  pl.BlockSpec((1, 6, d), lambda i: (mi(i), 0, 0)),
                  pl.BlockSpec((d, 128), lambda i: (0, 0)),
                  pl.BlockSpec((tm, tm), lambda i: (0, 0))],
        out_specs=[pl.BlockSpec((tm, 128), lambda i: (i, 0)),
                   pl.BlockSpec((8, 128), lambda i: (0, 0))],
        out_shape=[jax.ShapeDtypeStruct((n_rows, 128), F32),
                   jax.ShapeDtypeStruct((8, 128), F32)],
        scratch_shapes=[pltpu.VMEM((8, 128), F32)],
        compiler_params=_cparams(("arbitrary",), 32),
        name="router",
    )(xs, g2.reshape(1, d), modtab, wr, tri)


def _dispatch_kernel(dest_ref, x_ref, g_ref, mod_ref, init_ref, o_ref, h_s, sem):
    del init_ref
    i = pl.program_id(0)
    tm = h_s.shape[0]
    m = mod_ref[0]
    h_s[...] = _rms(x_ref[...], g_ref[...]) * (1.0 + m[4:5]) + m[3:4]

    def row_copy(r, k):
        dst = dest_ref[TOP_K * (i * tm + r) + k]
        return pltpu.make_async_copy(h_s.at[pl.ds(r, 1)], o_ref.at[pl.ds(dst, 1)], sem)

    def start(r, carry):
        for k in range(TOP_K):
            row_copy(r, k).start(priority=k)
        return carry

    lax.fori_loop(0, tm, start, 0, unroll=DMA_UNROLL)

    def wait(r, carry):
        for k in range(TOP_K):
            row_copy(r, k).wait()
        return carry

    lax.fori_loop(0, tm, wait, 0, unroll=DMA_UNROLL)


def _dispatch(dest, xs, n_rows, g2, modtab, n_slots, lat_rows):
    d = xs.shape[1]
    nb = modtab.shape[0] - 1
    tm = ROW_TILE
    mi = _mod_index(tm, lat_rows, nb)
    init = jnp.zeros((n_slots, d), F32)
    grid_spec = pltpu.PrefetchScalarGridSpec(
        num_scalar_prefetch=1,
        grid=(n_rows // tm,),
        in_specs=[pl.BlockSpec((tm, d), lambda i, dst: (i, 0)),
                  pl.BlockSpec((1, d), lambda i, dst: (0, 0)),
                  pl.BlockSpec((1, 6, d), lambda i, dst: (mi(i), 0, 0)),
                  pl.BlockSpec(memory_space=pl.ANY)],
        out_specs=pl.BlockSpec(memory_space=pl.ANY),
        scratch_shapes=[pltpu.VMEM((tm, d), F32), pltpu.SemaphoreType.DMA(())],
    )
    return pl.pallas_call(
        _dispatch_kernel,
        grid_spec=grid_spec,
        out_shape=jax.ShapeDtypeStruct((n_slots, d), F32),
        input_output_aliases={4: 0},
        compiler_params=_cparams(("arbitrary",), 32),
        name="dispatch",
    )(dest, xs, g2.reshape(1, d), modtab, init)


def _expert_kernel(ge_ref, nt_ref, ng_ref, x_ref, w1_ref, w3_ref, w2_ref, o_ref, xb_s):
    s = pl.program_id(0)
    f = pl.program_id(1)
    tm = MOE_TM

    @pl.when(s < ng_ref[0])
    def _():
        @pl.when(f == 0)
        def _():
            xb_s[...] = _bf(x_ref[...])
            o_ref[...] = jnp.zeros(o_ref.shape, F32)

        w1b, w3b, w2b = _bf(w1_ref[0]), _bf(w3_ref[0]), _bf(w2_ref[0])

        def tile(t):
            rows = pl.ds(t * tm, tm)
            h = xb_s[rows, :]
            u = _silu(_dot(h, w1b)) * _dot(h, w3b)
            o_ref[rows, :] += _dot(_bf(u), w2b)

        tile(0)
        for t in range(1, MOE_GROUP):
            pl.when(nt_ref[s] > t)(functools.partial(tile, t))

    @pl.when(jnp.logical_and(s >= ng_ref[0], f == pl.num_programs(1) - 1))
    def _():
        o_ref[...] = jnp.zeros(o_ref.shape, F32)


def _experts(group_e, group_tiles, n_groups, xsort, w1, w3, w2):
    n_slots, d = xsort.shape
    dff = w1.shape[2]
    tg, tf = MOE_TM * MOE_GROUP, MOE_TF
    nf = dff // tf

    def ss(s, ng):
        return jnp.maximum(jnp.minimum(s, ng[0] - 1), 0)

    def ff(s, f, ng):
        snake = jnp.where(s % 2 == 1, nf - 1 - f, f)
        last = jnp.where((ng[0] - 1) % 2 == 1, 0, nf - 1)
        return jnp.where(s < ng[0], snake, last)

    grid_spec = pltpu.PrefetchScalarGridSpec(
        num_scalar_prefetch=3,
        grid=(n_slots // tg, nf),
        in_specs=[pl.BlockSpec((tg, d), lambda s, f, ge, nt, ng: (ss(s, ng), 0)),
                  pl.BlockSpec((1, d, tf), lambda s, f, ge, nt, ng: (ge[ss(s, ng)], 0, ff(s, f, ng))),
                  pl.BlockSpec((1, d, tf), lambda s, f, ge, nt, ng: (ge[ss(s, ng)], 0, ff(s, f, ng))),
                  pl.BlockSpec((1, tf, d), lambda s, f, ge, nt, ng: (ge[ss(s, ng)], ff(s, f, ng), 0))],
        out_specs=pl.BlockSpec((tg, d), lambda s, f, ge, nt, ng: (s, 0)),
        scratch_shapes=[pltpu.VMEM((tg, d), BF16)],
    )
    return pl.pallas_call(
        _expert_kernel,
        grid_spec=grid_spec,
        out_shape=jax.ShapeDtypeStruct((n_slots, d), F32),
        compiler_params=_cparams(("arbitrary", "arbitrary"), 56),
        name="experts",
    )(group_e, group_tiles, n_groups, xsort, w1, w3, w2)


def _combine_kernel(dest_ref, y_ref, meta_ref, x_ref, g_ref, mod_ref, o_ref, buf_s, sem):
    i = pl.program_id(0)
    tm = x_ref.shape[0]

    def row_copy(r, k):
        src = dest_ref[TOP_K * (i * tm + r) + k]
        return pltpu.make_async_copy(y_ref.at[pl.ds(src, 1)], buf_s.at[k, pl.ds(r, 1)], sem)

    def start(r, carry):
        for k in range(TOP_K):
            row_copy(r, k).start(priority=k)
        return carry

    lax.fori_loop(0, tm, start, 0, unroll=DMA_UNROLL)

    def wait(r, carry):
        for k in range(TOP_K):
            row_copy(r, k).wait()
        return carry

    lax.fori_loop(0, tm, wait, 0, unroll=DMA_UNROLL)

    meta = meta_ref[...]
    lane = lax.broadcasted_iota(jnp.int32, meta.shape, 1)
    g0 = jnp.sum(jnp.where(lane == _META_G0, meta, 0.0), axis=1, keepdims=True)
    g1 = jnp.sum(jnp.where(lane == _META_G1, meta, 0.0), axis=1, keepdims=True)
    y = buf_s[0] * g0 + buf_s[1] * g1
    m = mod_ref[0]
    o_ref[...] = x_ref[...] + m[5:6] * _rms(y, g_ref[...])


def _combine(dest, yb, meta, xs, n_rows, g3, modtab, lat_rows):
    d = xs.shape[1]
    nb = modtab.shape[0] - 1
    tm = ROW_TILE
    mi = _mod_index(tm, lat_rows, nb)
    grid_spec = pltpu.PrefetchScalarGridSpec(
        num_scalar_prefetch=1,
        grid=(n_rows // tm,),
        in_specs=[pl.BlockSpec(memory_space=pl.ANY),
                  pl.BlockSpec((tm, 128), lambda i, dst: (i, 0)),
                  pl.BlockSpec((tm, d), lambda i, dst: (i, 0)),
                  pl.BlockSpec((1, d), lambda i, dst: (0, 0)),
                  pl.BlockSpec((1, 6, d), lambda i, dst: (mi(i), 0, 0))],
        out_specs=pl.BlockSpec((tm, d), lambda i, dst: (i, 0)),
        scratch_shapes=[pltpu.VMEM((TOP_K, tm, d), F32), pltpu.SemaphoreType.DMA(())],
    )
    return pl.pallas_call(
        _combine_kernel,
        grid_spec=grid_spec,
        out_shape=jax.ShapeDtypeStruct((n_rows, d), F32),
        compiler_params=_cparams(("arbitrary",), 40),
        name="combine",
    )(dest, yb, meta, xs, g3.reshape(1, d), modtab)


def _moe(xs, n_rows, g2, g3, modtab, w_router, w1, w3, w2, lat_rows):
    meta, cnt = _router(xs, n_rows, g2, modtab, w_router, lat_rows)
    counts = cnt[0, :N_EXPERTS].astype(jnp.int32)
    tg = MOE_TM * MOE_GROUP
    padded = (counts + tg - 1) // tg * tg
    pad_end = jnp.cumsum(padded)
    pad_start = pad_end - padded
    e = meta[:, _META_E0:_META_E1 + 1].astype(jnp.int32)
    r = meta[:, _META_R0:_META_R1 + 1].astype(jnp.int32)
    dest = (pad_start[e] + r).reshape(-1)
    n_groups_max = (n_rows * TOP_K) // tg + N_EXPERTS
    n_slots = n_groups_max * tg
    group_start = jnp.arange(n_groups_max, dtype=jnp.int32) * tg
    group_e = jnp.minimum(jnp.sum((group_start[:, None] >= pad_end[None, :]).astype(jnp.int32), axis=1),
                          N_EXPERTS - 1)
    filled = counts[group_e] - (group_start - pad_start[group_e])
    group_tiles = jnp.clip((filled + MOE_TM - 1) // MOE_TM, 1, MOE_GROUP).astype(jnp.int32)
    n_groups = (pad_end[-1:] // tg).astype(jnp.int32)
    xsort = _dispatch(dest, xs, n_rows, g2, modtab, n_slots, lat_rows)
    yb = _experts(group_e, group_tiles, n_groups, xsort, w1, w3, w2)
    return _combine(dest, yb, meta, xs, n_rows, g3, modtab, lat_rows)


def _rotary_tables(lat_len, ctx_len):
    rows = lat_len // GRID_W
    row = jnp.repeat(jnp.arange(rows, dtype=F32), GRID_W)
    col = jnp.tile(jnp.arange(GRID_W, dtype=F32), rows)
    n_freq = RET_DK // 4
    inv = ROPE_BASE ** (-jnp.arange(n_freq, dtype=F32) / n_freq)
    ang = jnp.concatenate([row[:, None] * inv, col[:, None] * inv], axis=-1)
    cos, sin = jnp.cos(ang), jnp.sin(ang)
    cosf = jnp.concatenate([jnp.ones((ctx_len, RET_DK), F32), jnp.concatenate([cos, cos], -1)], 0)
    sinf = jnp.concatenate([jnp.zeros((ctx_len, RET_DK), F32), jnp.concatenate([-sin, sin], -1)], 0)
    return cosf, sinf


def kernel(x, c, ctx, c_ctx, w_ada, b_ada, norm_g, w_in, w_out, ret_decay, ret_gn, hgrn_lb, hgrn_gn,
           mlstm_conv, mlstm_gate_b, mlstm_gn, w_ffn1, w_ffn3, w_ffn2, w_router, w_exp1, w_exp3, w_exp2):
    nb, lat_len, d = x.shape
    ctx_len = ctx.shape[1]
    depth = w_ada.shape[0]
    n_lat_rows = nb * lat_len
    n_rows = n_lat_rows + nb * ctx_len

    x_parts = [x.reshape(n_lat_rows, d), ctx.reshape(nb * ctx_len, d)]
    w_in_t = jnp.swapaxes(w_in, 1, 2)
    s_in = jnp.zeros((8, d), F32).at[:nb].set(c).at[nb].set(c_ctx)
    mod_all = _ada(s_in, w_ada, b_ada)[:, :nb + 1].reshape(depth, nb + 1, 6, d)

    cosf, sinf = _rotary_tables(lat_len, ctx_len)
    hg_cum, hg_masks, hg_signs = _hgrn_constants()
    hg_consts = (jnp.asarray(hg_cum, BF16), jnp.asarray(hg_masks, F32), jnp.asarray(hg_signs, F32))
    sm = jax.nn.softmax(hgrn_lb.astype(F32), axis=0)
    lb_all = jnp.clip(jnp.cumsum(sm, axis=0) - sm[0], 0.0, 1.0)

    hb = None
    for layer in range(depth):
        last = layer == depth - 1
        modtab = mod_all[layer]
        g = norm_g[layer]

        if hb is None:
            hb = _prenorm(x_parts, n_rows, g[0], modtab, 1, 0, lat_len, BF16)
        p = _mm(hb, w_in_t, layer, MM_TM, MM_TN, n_cols=PROJ_MAIN)
        w_g = jnp.zeros((1, 128, d), F32).at[0, :MLSTM_GATES].set(w_in_t[layer, PROJ_MAIN:])
        gates = _mm(hb, w_g, 0, MM_TM, 128)[:, :MLSTM_GATES]
        hb = None
        g_row_lat = jnp.swapaxes(gates[:n_lat_rows].reshape(nb, lat_len, MLSTM_GATES), 1, 2)
        g_row_ctx = jnp.swapaxes(gates[n_lat_rows:].reshape(nb, ctx_len, MLSTM_GATES), 1, 2)

        log_g = jax.nn.log_sigmoid(ret_decay[layer].astype(F32))
        lg_b = jnp.broadcast_to(log_g[:, :, None, None], (2, RET_HEADS, 8, RET_DV))
        lb = lb_all[layer]
        emit_ctx = not last
        o_ret = _retention(p, nb, lat_len, ctx_len, cosf, sinf, lg_b, ret_gn[layer], emit_ctx)
        o_hg = _hgrn(p, nb, lat_len, ctx_len, jnp.log(lb), jnp.log1p(-lb), hgrn_gn[layer],
                     hg_consts, emit_ctx)
        o_ml = _mlstm(p, gates, g_row_lat, g_row_ctx, nb, lat_len, ctx_len, mlstm_gate_b[layer],
                      mlstm_conv[layer], mlstm_gn[layer], emit_ctx)
        mix_parts = [list(o) for o in (o_ret, o_hg, o_ml)]
        rows_now = n_lat_rows if last else n_rows
        dense = layer % 2 == 0
        res = _wout(mix_parts, _bf(w_out[layer]), x_parts, rows_now, g[1], modtab, lat_len,
                    g[2] if dense else None)
        xs = res[0]
        x_parts = [xs]

        j = layer // 2
        if dense:
            nxt = (None, None) if last else (norm_g[layer + 1][0], mod_all[layer + 1])
            res = _ffn(res[1], _bf(w_ffn1[j]), _bf(w_ffn3[j]), _bf(w_ffn2[j]), xs, g[3], modtab, lat_len, *nxt)
            xs = res[0]
            hb = None if last else res[1]
        else:
            xs = _moe(xs, rows_now, g[2], g[3], modtab, w_router[j], w_exp1[j], w_exp3[j], w_exp2[j], lat_len)
        x_parts = [xs]
    return xs[:n_lat_rows].reshape(nb, lat_len, d)
```

```python
import functools

import numpy as np
import jax
import jax.numpy as jnp
from jax import lax
from jax.experimental import pallas as pl
from jax.experimental.pallas import tpu as pltpu

F32 = jnp.float32
BF16 = jnp.bfloat16

CHUNK = 128
NORM_EPS = 1e-6
ROPE_BASE = 10000.0
GRID_W = 64

RET_HEADS, RET_DK, RET_DV = 4, 128, 256
HGRN_HEADS, HGRN_DK, HGRN_DV = 4, 128, 128
MLSTM_HEADS, MLSTM_DK, MLSTM_DV = 4, 128, 128
N_EXPERTS = 8
TOP_K = 2

RET_QK = RET_HEADS * RET_DK
RET_WIDTH = RET_HEADS * RET_DV
HGRN_QK = HGRN_HEADS * HGRN_DK
HGRN_WIDTH = HGRN_HEADS * HGRN_DV
MLSTM_QK = MLSTM_HEADS * MLSTM_DK
MLSTM_WIDTH = MLSTM_HEADS * MLSTM_DV
MLSTM_GATES = 4 * MLSTM_HEADS

_C_RET_Q = 0
_C_RET_K = _C_RET_Q + RET_QK // 128
_C_RET_V = _C_RET_K + RET_QK // 128
_C_RET_G = _C_RET_V + RET_WIDTH // 128
_C_HG_Q = _C_RET_G + RET_WIDTH // 128
_C_HG_FF = _C_HG_Q + HGRN_QK // 128
_C_HG_FB = _C_HG_FF + HGRN_QK // 128
_C_HG_I = _C_HG_FB + HGRN_QK // 128
_C_HG_G = _C_HG_I + HGRN_WIDTH // 128
_C_ML_Q = _C_HG_G + HGRN_WIDTH // 128
_C_ML_K = _C_ML_Q + MLSTM_QK // 128
_C_ML_V = _C_ML_K + MLSTM_QK // 128
_C_ML_O = _C_ML_V + MLSTM_WIDTH // 128
PROJ_MAIN = (_C_ML_O + MLSTM_WIDTH // 128) * 128

_VMEM_CAP_BYTES = 56 * 1024 * 1024

ROW_TILE = 512
MM_TM, MM_TN = 1024, 1280
FFN_TF = 512
MOE_TM = 768
MOE_WSPLIT = 2
MOE_GROUP = 1
MOE_TF = 512
RET_UNROLL = (4, 4)
HGRN_UNROLL = (2, 4)
MLSTM_UNROLL = (2, 2)
DMA_UNROLL = 8


def _cparams(sem, vmem_mb):
    return pltpu.CompilerParams(dimension_semantics=sem,
                                vmem_limit_bytes=min(int(vmem_mb * 1024 * 1024), _VMEM_CAP_BYTES))


def _bf(x):
    return x.astype(BF16)


def _dot(a, b):
    return jnp.dot(a, b, preferred_element_type=F32)


def _dot_nt(a, b):
    return lax.dot_general(a, b, (((1,), (1,)), ((), ())), preferred_element_type=F32)


def _dot_tn(a, b):
    return lax.dot_general(a, b, (((0,), (0,)), ((), ())), preferred_element_type=F32)


def _sigmoid(x):
    return 1.0 / (1.0 + jnp.exp(-x))


def _silu(x):
    return x * _sigmoid(x)


def _log_sigmoid(x):
    return jnp.minimum(x, 0.0) - jnp.log(1.0 + jnp.exp(-jnp.abs(x)))


def _rms(x, g):
    return x * lax.rsqrt(jnp.mean(x * x, axis=-1, keepdims=True) + NORM_EPS) * g


def _mod_index(tile_rows, n_lat_rows_per_batch, n_batch):
    return lambda i: jnp.minimum((i * tile_rows) // n_lat_rows_per_batch, n_batch)


def _ada_kernel(s_ref, w_ref, b_ref, o_ref):
    s = _bf(_silu(s_ref[...]))
    o_ref[0] = _dot(s, _bf(w_ref[0])) + b_ref[0]


def _ada(s_in, w_ada, b_ada):
    depth, d, n = w_ada.shape
    tn = 1024
    return pl.pallas_call(
        _ada_kernel,
        grid=(depth, n // tn),
        in_specs=[pl.BlockSpec((8, d), lambda l, j: (0, 0)),
                  pl.BlockSpec((1, d, tn), lambda l, j: (l, 0, j)),
                  pl.BlockSpec((1, 1, tn), lambda l, j: (l, 0, j))],
        out_specs=pl.BlockSpec((1, 8, tn), lambda l, j: (l, 0, j)),
        out_shape=jax.ShapeDtypeStruct((depth, 8, n), F32),
        compiler_params=_cparams(("arbitrary", "arbitrary"), 40),
        name="ada",
    )(s_in, w_ada, b_ada.reshape(depth, 1, n))


def _part_specs(parts, tm, width):
    if len(parts) == 1:
        return [pl.BlockSpec((tm, width), lambda i: (i, 0))]
    n0 = parts[0].shape[0] // tm
    return [pl.BlockSpec((tm, width), lambda i: (jnp.minimum(i, n0 - 1), 0)),
            pl.BlockSpec((tm, width), lambda i: (jnp.maximum(i - n0, 0), 0))]


def _on_part(n_first, n_parts, fn):
    if n_parts == 1:
        fn(0)
    else:
        i = pl.program_id(0)
        pl.when(i < n_first)(lambda: fn(0))
        pl.when(i >= n_first)(lambda: fn(1))


def _prenorm_kernel(*refs, sc, sh, n_first):
    x_refs, (g_ref, mod_ref, o_ref) = refs[:-3], refs[-3:]

    def run(part):
        m = mod_ref[0]
        y = _rms(x_refs[part][...], g_ref[...])
        o_ref[...] = (y * (1.0 + m[sc:sc + 1]) + m[sh:sh + 1]).astype(o_ref.dtype)

    _on_part(n_first, len(x_refs), run)


def _prenorm(x_parts, n_rows, g, modtab, sc, sh, lat_rows, out_dtype):
    d = x_parts[0].shape[1]
    nb = modtab.shape[0] - 1
    mi = _mod_index(ROW_TILE, lat_rows, nb)
    return pl.pallas_call(
        functools.partial(_prenorm_kernel, sc=sc, sh=sh, n_first=x_parts[0].shape[0] // ROW_TILE),
        grid=(n_rows // ROW_TILE,),
        in_specs=[*_part_specs(x_parts, ROW_TILE, d),
                  pl.BlockSpec((1, d), lambda i: (0, 0)),
                  pl.BlockSpec((1, 6, d), lambda i: (mi(i), 0, 0))],
        out_specs=pl.BlockSpec((ROW_TILE, d), lambda i: (i, 0)),
        out_shape=jax.ShapeDtypeStruct((n_rows, d), out_dtype),
        compiler_params=_cparams(("arbitrary",), 32),
        name="prenorm",
    )(*x_parts, g.reshape(1, d), modtab)


def _mm_kernel(x_ref, wt_ref, o_ref, wb_s):
    @pl.when(pl.program_id(1) == 0)
    def _():
        wb_s[...] = _bf(wt_ref[0])

    o_ref[...] = _dot_nt(x_ref[...], wb_s[...]).astype(o_ref.dtype)


def _mm(x, wt, layer, tm, tn, n_cols=None, out_dtype=F32):
    m, k = x.shape
    n = wt.shape[1] if n_cols is None else n_cols
    return pl.pallas_call(
        _mm_kernel,
        grid=(n // tn, m // tm),
        in_specs=[pl.BlockSpec((tm, k), lambda j, i: (i, 0)),
                  pl.BlockSpec((1, tn, k), lambda j, i: (layer, j, 0))],
        out_specs=pl.BlockSpec((tm, tn), lambda j, i: (i, j)),
        out_shape=jax.ShapeDtypeStruct((m, n), out_dtype),
        scratch_shapes=[pltpu.VMEM((tn, k), BF16)],
        compiler_params=_cparams(("arbitrary", "arbitrary"), 48),
        name="proj",
    )(x, wt)


def _bwd_chunk(i, n_ctx, n_lat):
    return jnp.where(i < n_ctx, n_ctx - 1 - i, 2 * n_ctx + n_lat - 1 - i)


def _rows(c):
    return pl.ds(pl.multiple_of(c * CHUNK, CHUNK), CHUNK)


def _for_chunks(n_ctx, n_lat, fn, unroll, with_ctx=True):
    def run(part, n, off):
        def body(c, carry):
            fn(part, c, c + off)
            return carry
        lax.fori_loop(0, n, body, 0, unroll=max(u for u in (1, 2, unroll) if n % u == 0 and u <= unroll))
    if with_ctx:
        run(0, n_ctx, 0)
    run(1, n_lat, n_ctx)


def _lanes(d, w):
    return slice(d * w, (d + 1) * w)


def _mixer_call(kernel_fn, name, in_specs, args, nb, heads, lat_len, ctx_len, dv, scratch, emit_ctx, vmem_mb):
    width = heads * dv
    out_specs = [pl.BlockSpec((lat_len, dv), lambda b, h: (b, h))]
    out_shape = [jax.ShapeDtypeStruct((nb * lat_len, width), BF16)]
    if emit_ctx:
        out_specs.append(pl.BlockSpec((ctx_len, dv), lambda b, h: (b, h)))
        out_shape.append(jax.ShapeDtypeStruct((nb * ctx_len, width), BF16))
    n_in = len(in_specs)

    def body(*refs):
        ins, rest = refs[:n_in], refs[n_in:]
        if emit_ctx:
            ol, oc, scr = rest[0], rest[1], rest[2:]
        else:
            ol, oc, scr = rest[0], None, rest[1:]
        kernel_fn(*ins, ol, oc, *scr, n_ctx=ctx_len // CHUNK, n_lat=lat_len // CHUNK)

    return pl.pallas_call(
        body,
        grid=(nb, heads),
        in_specs=in_specs,
        out_specs=out_specs,
        out_shape=out_shape,
        scratch_shapes=scratch,
        compiler_params=_cparams(("arbitrary", "arbitrary"), vmem_mb),
        name=name,
    )(*args)


def _seq_specs(col0, width, nb, lat_len, ctx_len):
    n_lat_blk = nb * lat_len // ctx_len
    c0 = col0 * 128 // width
    return (pl.BlockSpec((lat_len, width), lambda b, h: (b, c0 + h)),
            pl.BlockSpec((ctx_len, width), lambda b, h: (n_lat_blk + b, c0 + h)))


def _ret_kernel(ql, qc, kl, kc, vl, vc, gl, gc, cos_ref, sin_ref, lg_ref, gn_ref, ol_ref, oc_ref,
                qs_s, kv_s, sall_s, o_s, st_s, dm_s, dq_s, dk_s, *, n_ctx, n_lat):
    nch = n_ctx + n_lat
    unroll_a, unroll_c = RET_UNROLL
    qr, kr, vr, gr, outr = (qc, ql), (kc, kl), (vc, vl), (gc, gl), (oc_ref, ol_ref)
    t_i = lax.broadcasted_iota(jnp.int32, (CHUNK, CHUNK), 0).astype(F32)
    s_i = lax.broadcasted_iota(jnp.int32, (CHUNK, CHUNK), 1).astype(F32)
    dm = None
    for d in range(2):
        lg = lg_ref[d, 0][0:1, 0:CHUNK]
        rel = (t_i - s_i) if d == 0 else (s_i - t_i)
        dmd = jnp.where(rel >= 0, jnp.exp(jnp.maximum(rel, 0.0) * lg), 0.0)
        dm = dmd if dm is None else dm + dmd
        p = t_i if d == 0 else (CHUNK - 1.0) - t_i
        dq_s[d] = jnp.exp((p + 1.0) * lg)
        dk_s[d] = jnp.exp((CHUNK - 1.0 - p) * lg)
    dm_s[...] = dm
    scale = RET_DK ** -0.5

    def phase_a(part, cl, cg):
        rl, rg = _rows(cl), _rows(cg)
        cs = cos_ref[rg, :]
        sn = sin_ref[rg, :]
        q = qr[part][rl, :]
        k = kr[part][rl, :]
        q = q * cs + pltpu.roll(q, RET_DK // 2, 1) * sn
        k = (k * cs + pltpu.roll(k, RET_DK // 2, 1) * sn) * scale
        vb = _bf(vr[part][rl, :])
        s = _dot_nt(_bf(q), _bf(k))
        o_s[rg, :] = _dot(_bf(s * dm_s[...]), vb)
        qs_s[rg, :] = jnp.concatenate([_bf(q * dq_s[0]), _bf(q * dq_s[1])], axis=1)
        kd = jnp.concatenate([_bf(k * dk_s[0]), _bf(k * dk_s[1])], axis=1)
        kv_s[cg] = _dot_tn(kd, vb)

    _for_chunks(n_ctx, n_lat, phase_a, unroll_a)

    st_s[...] = jnp.zeros(st_s.shape, F32)
    dec = [jnp.exp(float(CHUNK) * lg_ref[d, 0][0:1, :]) for d in range(2)]

    def phase_b(i, carry):
        for d, c in ((0, i), (1, _bwd_chunk(i, n_ctx, n_lat))):
            st = st_s[d]
            sall_s[c, _lanes(d, RET_DK), :] = _bf(st)
            st_s[d] = dec[d] * st + kv_s[c, _lanes(d, RET_DK), :]
        return carry

    lax.fori_loop(0, nch, phase_b, 0)
    gn = gn_ref[...]

    def phase_c(part, cl, cg):
        rl, rg = _rows(cl), _rows(cg)
        o = o_s[rg, :] + _dot(qs_s[rg, :], sall_s[cg])
        outr[part][rl, :] = (_rms(o, gn) * _silu(gr[part][rl, :])).astype(BF16)

    _for_chunks(n_ctx, n_lat, phase_c, unroll_c, with_ctx=oc_ref is not None)


def _retention(p, nb, lat_len, ctx_len, cosf, sinf, lg_b, gn, emit_ctx):
    t_len = lat_len + ctx_len
    nch = t_len // CHUNK
    s128 = functools.partial(_seq_specs, width=128, nb=nb, lat_len=lat_len, ctx_len=ctx_len)
    s256 = functools.partial(_seq_specs, width=256, nb=nb, lat_len=lat_len, ctx_len=ctx_len)
    in_specs = [*s128(_C_RET_Q), *s128(_C_RET_K), *s256(_C_RET_V), *s256(_C_RET_G),
                pl.BlockSpec((t_len, 128), lambda b, h: (0, 0)),
                pl.BlockSpec((t_len, 128), lambda b, h: (0, 0)),
                pl.BlockSpec((2, 1, 8, RET_DV), lambda b, h: (0, h, 0, 0)),
                pl.BlockSpec((1, RET_DV), lambda b, h: (0, h))]
    scratch = [pltpu.VMEM((t_len, 2 * RET_DK), BF16),
               pltpu.VMEM((nch, 2 * RET_DK, RET_DV), F32),
               pltpu.VMEM((nch, 2 * RET_DK, RET_DV), BF16),
               pltpu.VMEM((t_len, RET_DV), F32),
               pltpu.VMEM((2, RET_DK, RET_DV), F32),
               pltpu.VMEM((CHUNK, CHUNK), F32),
               pltpu.VMEM((2, CHUNK, CHUNK), F32), pltpu.VMEM((2, CHUNK, CHUNK), F32)]
    args = (p, p, p, p, p, p, p, p, cosf, sinf, lg_b, gn.reshape(1, RET_WIDTH))
    return _mixer_call(_ret_kernel, "retention", in_specs, args, nb, RET_HEADS, lat_len, ctx_len, RET_DV,
                       scratch, emit_ctx, 48)


_HG_LEVELS = (64, 32, 16, 8, 4, 2, 1)


def _hgrn_constants():
    c = CHUNK
    t = np.arange(c)[:, None]
    u = np.arange(c)[None, :]
    cum = (u <= t).astype(np.float32)
    masks, signs = [], []
    for m in _HG_LEVELS:
        base = (t // (2 * m)) * (2 * m)
        lower = t >= base + m
        tb = t // (2 * m)
        sb = u // (2 * m)
        masks.append((tb == sb) & lower & (u < (sb * 2 * m + m)))
        signs.append(np.broadcast_to(np.where(lower, 1.0, -1.0), (c, c)))
    masks.append(t == u)
    kf = np.stack([x.astype(np.float32) for x in masks], axis=0)
    kb = np.stack([x.astype(np.float32)[::-1, ::-1] for x in masks], axis=0)
    sf = np.stack([x.astype(np.float32) for x in signs], axis=0)
    sb_ = np.stack([x.astype(np.float32)[::-1, ::-1] for x in signs], axis=0)
    return (np.stack([cum, cum[::-1, ::-1]], 0), np.stack([kf, kb], 0),
            np.stack([sf[:_HG_WIDE], sb_[:_HG_WIDE]], 0))


_HG_WIDE = 5


def _hgrn_level_exponents(b, lf, d, sgn_ref, row):
    out = []
    for l, m in enumerate(_HG_LEVELS[:_HG_WIDE]):
        pieces = []
        for j in range(CHUNK // (2 * m)):
            r = 2 * m * j + (m - 1 if d == 0 else m)
            pieces.append(jnp.broadcast_to(b[r:r + 1, :], (2 * m, CHUNK)))
        bref = pieces[0] if len(pieces) == 1 else jnp.concatenate(pieces, axis=0)
        out.append((b - bref) * sgn_ref[d, l])
    up = pltpu.roll(lf, CHUNK - 1, 0)
    dn = pltpu.roll(lf, 1, 0)
    r4 = row % 4
    if d == 0:
        e2 = jnp.where(r4 == 0, up, jnp.where(r4 == 1, 0.0, jnp.where(r4 == 2, lf, lf + dn)))
        e1 = jnp.where(row % 2 == 1, lf, 0.0)
    else:
        e2 = jnp.where(r4 == 0, lf + up, jnp.where(r4 == 1, lf, jnp.where(r4 == 2, 0.0, dn)))
        e1 = jnp.where(row % 2 == 0, lf, 0.0)
    return out + [e2, e1]


def _hgrn_kernel(ql, qc, ffl, ffc, fbl, fbc, il, ic, gl, gc, llb_ref, lub_ref, gn_ref, c_ref, k_ref, sgn_ref,
                 ol_ref, oc_ref, qs_s, kv_s, sall_s, dec_s, o_s, st_s, *, n_ctx, n_lat):
    nch = n_ctx + n_lat
    unroll_a, unroll_c = HGRN_UNROLL
    qr, fr, ir, gr, outr =(qc, ql), ((ffc, ffl), (fbc, fbl)), (ic, il), (gc, gl), (oc_ref, ol_ref)
    llb = llb_ref[...]
    lub = lub_ref[...]
    nlev = len(_HG_LEVELS)
    row = lax.broadcasted_iota(jnp.int32, (CHUNK, HGRN_DK), 0)

    def phase_a(part, cl, cg):
        rl, rg = _rows(cl), _rows(cg)
        q = _silu(qr[part][rl, :])
        qb = _bf(q)
        vb = _bf(ir[part][rl, :])
        a_sum, qs, kds = None, [], []
        for d in range(2):
            lsg = lub + _log_sigmoid(fr[d][part][rl, :])
            lf = jnp.maximum(llb, lsg) + jnp.log(1.0 + jnp.exp(-jnp.abs(llb - lsg)))
            k = 1.0 - jnp.exp(lf)
            lf_hi = _bf(lf)
            lf_lo = _bf(lf - lf_hi.astype(F32))
            b2 = _dot(c_ref[d], jnp.concatenate([lf_hi, lf_lo], axis=1))
            b = b2[:, 0:HGRN_DK] + b2[:, HGRN_DK:]
            es = _hgrn_level_exponents(b, lf, d, sgn_ref, row)
            kb = _bf(k)
            a = k_ref[d, nlev] * _dot_nt(qb, kb)
            for l in range(nlev):
                xb = _bf(jnp.exp(es[l]))
                a = a + k_ref[d, l] * _dot_nt(qb * xb, kb * xb)
            a_sum = a if a_sum is None else a_sum + a
            b_last = b[CHUNK - 1:CHUNK] if d == 0 else b[0:1]
            qs.append(_bf(q * jnp.exp(b)))
            kds.append(_bf(k * jnp.exp(b_last - b)))
            dec_s[cg, :, _lanes(d, HGRN_DK)] = jnp.exp(b_last)
        o_s[rg, :] = _dot(_bf(a_sum), vb)
        qs_s[rg, :] = jnp.concatenate(qs, axis=1)
        kv_s[cg] = _dot_tn(vb, jnp.concatenate(kds, axis=1))

    _for_chunks(n_ctx, n_lat, phase_a, unroll_a)
    st_s[...] = jnp.zeros(st_s.shape, F32)

    def phase_b(i, carry):
        for d, c in ((0, i), (1, _bwd_chunk(i, n_ctx, n_lat))):
            ln = _lanes(d, HGRN_DK)
            st = st_s[d]
            sall_s[c, :, ln] = _bf(st)
            st_s[d] = dec_s[c, :, ln] * st + kv_s[c, :, ln]
        return carry

    lax.fori_loop(0, nch, phase_b, 0)
    gn = gn_ref[...]

    def phase_c(part, cl, cg):
        rl, rg = _rows(cl), _rows(cg)
        o = o_s[rg, :] + _dot_nt(qs_s[rg, :], sall_s[cg])
        outr[part][rl, :] = (_rms(o, gn) * _silu(gr[part][rl, :])).astype(BF16)

    _for_chunks(n_ctx, n_lat, phase_c, unroll_c, with_ctx=oc_ref is not None)


def _hgrn(p, nb, lat_len, ctx_len, log_lb, log_ub, gn, consts, emit_ctx):
    mats, masks, signs = consts
    t_len = lat_len + ctx_len
    nch = t_len // CHUNK
    spec = functools.partial(_seq_specs, width=128, nb=nb, lat_len=lat_len, ctx_len=ctx_len)
    vec = pl.BlockSpec((1, 128), lambda b, h: (0, h))
    in_specs = [*spec(_C_HG_Q), *spec(_C_HG_FF), *spec(_C_HG_FB), *spec(_C_HG_I), *spec(_C_HG_G),
                vec, vec, vec,
                pl.BlockSpec(mats.shape, lambda b, h: (0, 0, 0)),
                pl.BlockSpec(masks.shape, lambda b, h: (0, 0, 0, 0)),
                pl.BlockSpec(signs.shape, lambda b, h: (0, 0, 0, 0))]
    scratch = [pltpu.VMEM((t_len, 2 * HGRN_DK), BF16),
               pltpu.VMEM((nch, HGRN_DV, 2 * HGRN_DK), F32),
               pltpu.VMEM((nch, HGRN_DV, 2 * HGRN_DK), BF16),
               pltpu.VMEM((nch, 1, 2 * HGRN_DK), F32),
               pltpu.VMEM((t_len, HGRN_DV), F32),
               pltpu.VMEM((2, HGRN_DV, HGRN_DK), F32)]
    args = (p, p, p, p, p, p, p, p, p, p, log_lb.reshape(1, HGRN_QK), log_ub.reshape(1, HGRN_QK),
            gn.reshape(1, HGRN_WIDTH), mats, masks, signs)
    return _mixer_call(_hgrn_kernel, "hgrn2", in_specs, args, nb, HGRN_HEADS, lat_len, ctx_len, HGRN_DV,
                       scratch, emit_ctx, 48)


def _mlstm_kernel(ql, qc, kl, kc, vl, vc, ogl, ogc, gcl, gcc, grl, grc, bc_ref, br_ref,
                  wq_ref, wk_ref, gn_ref, ol_ref, oc_ref,
                  q_s, gr_s, intra_s, kv_s, call_s, rho_s, bcol_s, bl_s, mu_s, mprev_s, st_s, m_s,
                  *, n_ctx, n_lat):
    nch = n_ctx + n_lat
    unroll_a, unroll_c = MLSTM_UNROLL
    h = pl.program_id(1)
    nh = MLSTM_HEADS
    dk, dv = MLSTM_DK, MLSTM_DV
    ext = 2 * dv
    qr, kr, vr, ogr, gcr, outr = (qc, ql), (kc, kl), (vc, vl), (ogc, ogl), (gcc, gcl), (oc_ref, ol_ref)
    n_loc = (n_ctx, n_lat)

    for cc in range(nch):
        src, c0 = (grc, cc) if cc < n_ctx else (grl, cc - n_ctx)
        gr_s[cc] = src[0, :, c0 * CHUNK:(c0 + 1) * CHUNK] + br_ref[...]

    row = lax.broadcasted_iota(jnp.int32, (CHUNK, dk), 0)
    lane = lax.broadcasted_iota(jnp.int32, (CHUNK, dv), 1)
    del lane
    ones_col = jnp.ones((CHUNK, dv), BF16)
    scale = dk ** -0.5
    t_i = lax.broadcasted_iota(jnp.int32, (CHUNK, CHUNK), 0)
    s_i = lax.broadcasted_iota(jnp.int32, (CHUNK, CHUNK), 1)
    lane16 = lax.broadcasted_iota(jnp.int32, (CHUNK, MLSTM_GATES), 1)
    sub16 = lax.broadcasted_iota(jnp.int32, (MLSTM_GATES, CHUNK), 0)

    def conv(src, w_ref, part, cl):
        r0 = pl.multiple_of(cl * CHUNK, CHUNK)
        n_rows = n_loc[part] * CHUNK
        x = src[pl.ds(r0, CHUNK), :]
        pr = src[pl.ds(jnp.maximum(r0 - 1, 0), 1), :]
        nx = src[pl.ds(jnp.minimum(r0 + CHUNK, n_rows - 1), 1), :]
        pr = jnp.where(cl != 0, pr, 0.0)
        nx = jnp.where(cl != n_loc[part] - 1, nx, 0.0)
        xp = jnp.where(row == 0, pr, pltpu.roll(x, 1, 0))
        xn = jnp.where(row == CHUNK - 1, nx, pltpu.roll(x, CHUNK - 1, 0))
        w = w_ref[...]
        return _silu(w[0:1] * xp + w[1:2] * x + w[2:3] * xn)

    def pick_col(g, j):
        return jnp.sum(jnp.where(lane16 == j, g, 0.0), axis=1, keepdims=True)

    def pick_row(g, j):
        return jnp.sum(jnp.where(sub16 == j, g, 0.0), axis=0, keepdims=True)

    def phase_a(part, cl, cg):
        rl, rg = _rows(cl), _rows(cg)
        q = conv(qr[part], wq_ref, part, cl)
        k = conv(kr[part], wk_ref, part, cl) * scale
        qb = _bf(q)
        q_s[rg, :] = qb
        vb = jnp.concatenate([_bf(vr[part][rl, :]), ones_col], axis=1)
        s = _dot_nt(qb, _bf(k))
        g_c = gcr[part][rl, :] + bc_ref[...]
        g_r = gr_s[cg]
        kws = []
        for d in range(2):
            ig_c = pick_col(g_c, d * nh + h)
            lf_c = _log_sigmoid(pick_col(g_c, 2 * nh + d * nh + h))
            ig_r = pick_row(g_r, d * nh + h)
            lf_r = _log_sigmoid(pick_row(g_r, 2 * nh + d * nh + h))
            tri = (s_i <= t_i) if d == 0 else (s_i >= t_i)
            tri_t = (t_i <= s_i) if d == 0 else (t_i >= s_i)
            b_c = jnp.sum(jnp.where(tri, lf_r, 0.0), axis=1, keepdims=True)
            b_r = jnp.sum(jnp.where(tri_t, lf_c, 0.0), axis=0, keepdims=True)
            dmat = jnp.where(tri, b_c - b_r + ig_r, -jnp.inf)
            rho = jnp.max(dmat, axis=1, keepdims=True)
            intra_s[d, rg, :] = _dot(_bf(s * jnp.exp(dmat - rho)), vb)
            b_last = b_c[CHUNK - 1:CHUNK] if d == 0 else b_c[0:1]
            mu = jnp.max(b_last - b_r + ig_r, axis=1, keepdims=True)
            kws.append(_bf(k * jnp.exp(b_last - b_c + ig_c - mu)))
            rho_s[d, rg, :] = jnp.broadcast_to(rho, (CHUNK, 128))
            bcol_s[d, rg, :] = jnp.broadcast_to(b_c, (CHUNK, 128))
            bl_s[cg, :, _lanes(d, 128)] = jnp.broadcast_to(b_last, (1, 128))
            mu_s[cg, :, _lanes(d, 128)] = jnp.broadcast_to(mu, (1, 128))
        kv_s[cg] = _dot_tn(jnp.concatenate(kws, axis=1), vb)

    _for_chunks(n_ctx, n_lat, phase_a, unroll_a)
    st_s[...] = jnp.zeros(st_s.shape, F32)
    m_s[...] = jnp.zeros(m_s.shape, F32)

    def wide(v):
        return jnp.concatenate([v, v], axis=1)

    def phase_b(i, carry):
        for d, c in ((0, i), (1, _bwd_chunk(i, n_ctx, n_lat))):
            ln = _lanes(d, 128)
            m_prev = m_s[d]
            st = st_s[d]
            mprev_s[c, :, ln] = m_prev
            call_s[c, :, _lanes(d, ext)] = _bf(st)
            bl = bl_s[c, :, ln]
            mu = mu_s[c, :, ln]
            m_new = jnp.maximum(bl + m_prev, mu)
            st_s[d] = (wide(jnp.exp(bl + m_prev - m_new)) * st
                       + wide(jnp.exp(mu - m_new)) * kv_s[c, _lanes(d, dk), :])
            m_s[d] = m_new
        return carry

    lax.fori_loop(0, nch, phase_b, 0)
    gn = gn_ref[...]

    def phase_c(part, cl, cg):
        rl, rg = _rows(cl), _rows(cg)
        qc_all = _dot(q_s[rg, :], call_s[cg])
        o = None
        for d in range(2):
            rho = rho_s[d, rg, :]
            b_c = bcol_s[d, rg, :]
            m_prev = mprev_s[cg, :, _lanes(d, 128)]
            m_t = jnp.maximum(rho, b_c + m_prev)
            nd = (wide(jnp.exp(rho - m_t)) * intra_s[d, rg, :]
                  + wide(jnp.exp(b_c + m_prev - m_t)) * qc_all[:, _lanes(d, ext)])
            hh = nd[:, 0:dv] / jnp.maximum(jnp.abs(nd[:, dv:]), jnp.exp(-m_t))
            o = hh if o is None else o + hh
        y = o - jnp.mean(o, axis=-1, keepdims=True)
        y = y * lax.rsqrt(jnp.mean(y * y, axis=-1, keepdims=True) + NORM_EPS)
        outr[part][rl, :] = (y * gn * _sigmoid(ogr[part][rl, :])).astype(BF16)

    _for_chunks(n_ctx, n_lat, phase_c, unroll_c, with_ctx=oc_ref is not None)


def _mlstm(p, g_col, g_row_lat, g_row_ctx, nb, lat_len, ctx_len, gate_b, conv_w, gn, emit_ctx):
    t_len = lat_len + ctx_len
    nch = t_len // CHUNK
    n_lat_blk = nb * lat_len // ctx_len
    ng = MLSTM_GATES
    dk, dv = MLSTM_DK, MLSTM_DV
    spec = functools.partial(_seq_specs, width=128, nb=nb, lat_len=lat_len, ctx_len=ctx_len)
    in_specs = [*spec(_C_ML_Q), *spec(_C_ML_K), *spec(_C_ML_V), *spec(_C_ML_O),
                pl.BlockSpec((lat_len, ng), lambda b, h: (b, 0)),
                pl.BlockSpec((ctx_len, ng), lambda b, h: (n_lat_blk + b, 0)),
                pl.BlockSpec((1, ng, lat_len), lambda b, h: (b, 0, 0)),
                pl.BlockSpec((1, ng, ctx_len), lambda b, h: (b, 0, 0)),
                pl.BlockSpec((1, ng), lambda b, h: (0, 0)),
                pl.BlockSpec((ng, 1), lambda b, h: (0, 0)),
                pl.BlockSpec((3, 128), lambda b, h: (0, h)),
                pl.BlockSpec((3, 128), lambda b, h: (0, MLSTM_HEADS + h)),
                pl.BlockSpec((1, 128), lambda b, h: (0, h))]
    scratch = [pltpu.VMEM((t_len, dk), BF16),
               pltpu.VMEM((nch, ng, CHUNK), F32),
               pltpu.VMEM((2, t_len, 2 * dv), F32),
               pltpu.VMEM((nch, 2 * dk, 2 * dv), F32),
               pltpu.VMEM((nch, dk, 4 * dv), BF16),
               pltpu.VMEM((2, t_len, 128), F32), pltpu.VMEM((2, t_len, 128), F32),
               pltpu.VMEM((nch, 1, 256), F32), pltpu.VMEM((nch, 1, 256), F32), pltpu.VMEM((nch, 1, 256), F32),
               pltpu.VMEM((2, dk, 2 * dv), F32), pltpu.VMEM((2, 1, 128), F32)]
    args = (p, p, p, p, p, p, p, p, g_col, g_col, g_row_lat, g_row_ctx,
            gate_b.reshape(1, ng), gate_b.reshape(ng, 1), conv_w, conv_w, gn.reshape(1, MLSTM_WIDTH))
    return _mixer_call(_mlstm_kernel, "mlstm", in_specs, args, nb, MLSTM_HEADS, lat_len, ctx_len, dv,
                       scratch, emit_ctx, 48)


def _wout_kernel(*refs, n_lat_tiles, n_mix_parts, n_x_parts, emit_next):
    it = iter(refs)
    mix = [[next(it) for _ in range(n_mix_parts)] for _ in range(3)]
    x_parts = [next(it) for _ in range(n_x_parts)]
    wr_ref, wh_ref, wm_ref, g_ref, mod_ref = (next(it) for _ in range(5))
    g2_ref = next(it) if emit_next else None
    o_ref = next(it)
    h2_ref = next(it) if emit_next else None

    def run(part):
        pm = min(part, n_mix_parts - 1)
        y = (_dot(mix[0][pm][...], wr_ref[...]) + _dot(mix[1][pm][...], wh_ref[...])
             + _dot(mix[2][pm][...], wm_ref[...]))
        m = mod_ref[0]
        xn = x_parts[min(part, n_x_parts - 1)][...] + m[2:3] * _rms(y, g_ref[...])
        o_ref[...] = xn
        if emit_next:
            h2_ref[...] = (_rms(xn, g2_ref[...]) * (1.0 + m[4:5]) + m[3:4]).astype(h2_ref.dtype)

    _on_part(n_lat_tiles, max(n_mix_parts, n_x_parts), run)


def _wout(mix_parts, w_out, x_parts, n_rows, g1, modtab, lat_rows, g2_next):
    d = x_parts[0].shape[1]
    nb = modtab.shape[0] - 1
    mi = _mod_index(ROW_TILE, lat_rows, nb)
    tm = ROW_TILE
    n_lat_tiles = mix_parts[0][0].shape[0] // tm
    emit_next = g2_next is not None
    widths = (RET_WIDTH, HGRN_WIDTH, MLSTM_WIDTH)
    w_r = w_out[0:RET_WIDTH]
    w_h = w_out[RET_WIDTH:RET_WIDTH + HGRN_WIDTH]
    w_m = w_out[RET_WIDTH + HGRN_WIDTH:]
    full = lambda a: pl.BlockSpec(a.shape, lambda i: (0, 0))
    in_specs, args = [], []
    for parts, w in zip(mix_parts, widths):
        in_specs += _part_specs(parts, tm, w)
        args += list(parts)
    in_specs += _part_specs(x_parts, tm, d)
    args += list(x_parts)
    in_specs += [full(w_r), full(w_h), full(w_m),
                 pl.BlockSpec((1, d), lambda i: (0, 0)),
                 pl.BlockSpec((1, 6, d), lambda i: (mi(i), 0, 0))]
    args += [w_r, w_h, w_m, g1.reshape(1, d), modtab]
    out_specs = [pl.BlockSpec((tm, d), lambda i: (i, 0))]
    out_shape = [jax.ShapeDtypeStruct((n_rows, d), F32)]
    if emit_next:
        in_specs.append(pl.BlockSpec((1, d), lambda i: (0, 0)))
        args.append(g2_next.reshape(1, d))
        out_specs.append(pl.BlockSpec((tm, d), lambda i: (i, 0)))
        out_shape.append(jax.ShapeDtypeStruct((n_rows, d), BF16))
    return pl.pallas_call(
        functools.partial(_wout_kernel, n_lat_tiles=n_lat_tiles, n_mix_parts=len(mix_parts[0]),
                          n_x_parts=len(x_parts), emit_next=emit_next),
        grid=(n_rows // tm,),
        in_specs=in_specs,
        out_specs=out_specs,
        out_shape=out_shape,
        compiler_params=_cparams(("arbitrary",), 52),
        name="wout",
    )(*args)


def _ffn_kernel(*refs, emit_next):
    if emit_next:
        h_ref, w1_ref, w3_ref, w2_ref, x_ref, g_ref, mod_ref, gn_ref, modn_ref, o_ref, hn_ref, acc_ref = refs
    else:
        h_ref, w1_ref, w3_ref, w2_ref, x_ref, g_ref, mod_ref, o_ref, acc_ref = refs
    f = pl.program_id(1)

    @pl.when(f == 0)
    def _():
        acc_ref[...] = jnp.zeros(acc_ref.shape, F32)

    h = h_ref[...]
    u = _silu(_dot(h, w1_ref[...])) * _dot(h, w3_ref[...])
    acc_ref[...] += _dot(_bf(u), w2_ref[...])

    @pl.when(f == pl.num_programs(1) - 1)
    def _():
        m = mod_ref[0]
        xn = x_ref[...] + m[5:6] * _rms(acc_ref[...], g_ref[...])
        o_ref[...] = xn
        if emit_next:
            mn = modn_ref[0]
            hn_ref[...] = (_rms(xn, gn_ref[...]) * (1.0 + mn[1:2]) + mn[0:1]).astype(hn_ref.dtype)


def _ffn(hb, w1, w3, w2, xs, g3, modtab, lat_rows, g_next, modtab_next):
    n_rows, d = hb.shape
    dff = w1.shape[1]
    nb = modtab.shape[0] - 1
    tm, tf = ROW_TILE, FFN_TF
    mi = _mod_index(tm, lat_rows, nb)
    emit_next = g_next is not None
    nf = dff // tf

    def fs(i, f):
        return jnp.where(i % 2 == 1, nf - 1 - f, f)

    in_specs = [pl.BlockSpec((tm, d), lambda i, f: (i, 0)),
                pl.BlockSpec((d, tf), lambda i, f: (0, fs(i, f))),
                pl.BlockSpec((d, tf), lambda i, f: (0, fs(i, f))),
                pl.BlockSpec((tf, d), lambda i, f: (fs(i, f), 0)),
                pl.BlockSpec((tm, d), lambda i, f: (i, 0)),
                pl.BlockSpec((1, d), lambda i, f: (0, 0)),
                pl.BlockSpec((1, 6, d), lambda i, f: (mi(i), 0, 0))]
    args = [hb, w1, w3, w2, xs, g3.reshape(1, d), modtab]
    out_specs = [pl.BlockSpec((tm, d), lambda i, f: (i, 0))]
    out_shape = [jax.ShapeDtypeStruct((n_rows, d), F32)]
    if emit_next:
        in_specs += [pl.BlockSpec((1, d), lambda i, f: (0, 0)),
                     pl.BlockSpec((1, 6, d), lambda i, f: (mi(i), 0, 0))]
        args += [g_next.reshape(1, d), modtab_next]
        out_specs.append(pl.BlockSpec((tm, d), lambda i, f: (i, 0)))
        out_shape.append(jax.ShapeDtypeStruct((n_rows, d), BF16))
    return pl.pallas_call(
        functools.partial(_ffn_kernel, emit_next=emit_next),
        grid=(n_rows // tm, dff // tf),
        in_specs=in_specs,
        out_specs=out_specs,
        out_shape=out_shape,
        scratch_shapes=[pltpu.VMEM((tm, d), F32)],
        compiler_params=_cparams(("arbitrary", "arbitrary"), 48),
        name="ffn",
    )(*args)


_META_E0, _META_E1, _META_R0, _META_R1, _META_G0, _META_G1 = range(6)


def _router_kernel(x_ref, g_ref, mod_ref, wr_ref, tri_ref, meta_ref, cnt_ref, carry_ref):
    i = pl.program_id(0)

    @pl.when(i == 0)
    def _():
        carry_ref[...] = jnp.zeros(carry_ref.shape, F32)

    m = mod_ref[0]
    hmod = _rms(x_ref[...], g_ref[...]) * (1.0 + m[4:5]) + m[3:4]
    logits = jnp.dot(hmod, wr_ref[...], precision=lax.Precision.HIGHEST, preferred_element_type=F32)
    lane = lax.broadcasted_iota(jnp.int32, logits.shape, 1)
    lanef = lane.astype(F32)
    logits = jnp.where(lane < N_EXPERTS, logits, -jnp.inf)
    v0 = jnp.max(logits, axis=1, keepdims=True)
    e0 = jnp.min(jnp.where(logits == v0, lanef, 1e9), axis=1, keepdims=True)
    rest = jnp.where(lanef == e0, -jnp.inf, logits)
    v1 = jnp.max(rest, axis=1, keepdims=True)
    e1 = jnp.min(jnp.where(rest == v1, lanef, 1e9), axis=1, keepdims=True)
    ex = jnp.exp(v1 - v0)
    g0 = 1.0 / (1.0 + ex)
    g1 = ex / (1.0 + ex)
    oh0 = lanef == e0
    oh1 = lanef == e1
    oh = jnp.where(jnp.logical_or(oh0, oh1), 1.0, 0.0)
    before = _dot(tri_ref[...], _bf(oh)) + carry_ref[0:1, :]
    r0 = jnp.sum(jnp.where(oh0, before, 0.0), axis=1, keepdims=True)
    r1 = jnp.sum(jnp.where(oh1, before, 0.0), axis=1, keepdims=True)
    carry_ref[0:1, :] = carry_ref[0:1, :] + jnp.sum(oh, axis=0, keepdims=True)
    meta = jnp.zeros(logits.shape, F32)
    for j, val in ((_META_E0, e0), (_META_E1, e1), (_META_R0, r0), (_META_R1, r1),
                   (_META_G0, g0), (_META_G1, g1)):
        meta = jnp.where(lane == j, val, meta)
    meta_ref[...] = meta
    cnt_ref[...] = carry_ref[...]


def _router(xs, n_rows, g2, modtab, w_router, lat_rows):
    d = xs.shape[1]
    nb = modtab.shape[0] - 1
    tm = ROW_TILE
    mi = _mod_index(tm, lat_rows, nb)
    wr = jnp.zeros((d, 128), F32).at[:, :N_EXPERTS].set(w_router)
    tri = jnp.asarray(np.tril(np.ones((tm, tm), np.float32), -1), BF16)
    return pl.pallas_call(
        _router_kernel,
        grid=(n_rows // tm,),
        in_specs=[pl.BlockSpec((tm, d), lambda i: (i, 0)),
                  pl.BlockSpec((1, d), lambda i: (0, 0)),
                  pl.BlockSpec((1, 6, d), lambda i: (mi(i), 0, 0)),
                  pl.BlockSpec((d, 128), lambda i: (0, 0)),
                  pl.BlockSpec((tm, tm), lambda i: (0, 0))],
        out_specs=[pl.BlockSpec((tm, 128), lambda i: (i, 0)),
                   pl.BlockSpec((8, 128), lambda i: (0, 0))],
        out_shape=[jax.ShapeDtypeStruct((n_rows, 128), F32),
                   jax.ShapeDtypeStruct((8, 128), F32)],
        scratch_shapes=[pltpu.VMEM((8, 128), F32)],
        compiler_params=_cparams(("arbitrary",), 32),
        name="router",
    )(xs, g2.reshape(1, d), modtab, wr, tri)


def _dispatch_kernel(dest_ref, x_ref, g_ref, mod_ref, init_ref, o_ref, h_s, sem):
    del init_ref
    i = pl.program_id(0)
    tm = h_s.shape[0]
    m = mod_ref[0]
    h_s[...] = _rms(x_ref[...], g_ref[...]) * (1.0 + m[4:5]) + m[3:4]

    def row_copy(r, k):
        dst = dest_ref[TOP_K * (i * tm + r) + k]
        return pltpu.make_async_copy(h_s.at[pl.ds(r, 1)], o_ref.at[pl.ds(dst, 1)], sem)

    def start(r, carry):
        for k in range(TOP_K):
            row_copy(r, k).start(priority=k)
        return carry

    lax.fori_loop(0, tm, start, 0, unroll=DMA_UNROLL)

    def wait(r, carry):
        for k in range(TOP_K):
            row_copy(r, k).wait()
        return carry

    lax.fori_loop(0, tm, wait, 0, unroll=DMA_UNROLL)


def _dispatch(dest, xs, n_rows, g2, modtab, n_slots, lat_rows):
    d = xs.shape[1]
    nb = modtab.shape[0] - 1
    tm = ROW_TILE
    mi = _mod_index(tm, lat_rows, nb)
    init = jnp.zeros((n_slots, d), F32)
    grid_spec = pltpu.PrefetchScalarGridSpec(
        num_scalar_prefetch=1,
        grid=(n_rows // tm,),
        in_specs=[pl.BlockSpec((tm, d), lambda i, dst: (i, 0)),
                  pl.BlockSpec((1, d), lambda i, dst: (0, 0)),
                  pl.BlockSpec((1, 6, d), lambda i, dst: (mi(i), 0, 0)),
                  pl.BlockSpec(memory_space=pl.ANY)],
        out_specs=pl.BlockSpec(memory_space=pl.ANY),
        scratch_shapes=[pltpu.VMEM((tm, d), F32), pltpu.SemaphoreType.DMA(())],
    )
    return pl.pallas_call(
        _dispatch_kernel,
        grid_spec=grid_spec,
        out_shape=jax.ShapeDtypeStruct((n_slots, d), F32),
        input_output_aliases={4: 0},
        compiler_params=_cparams(("arbitrary",), 32),
        name="dispatch",
    )(dest, xs, g2.reshape(1, d), modtab, init)


def _expert_kernel(ge_ref, nt_ref, ng_ref, x_ref, *refs):
    w1_refs, w3_refs = refs[:MOE_WSPLIT], refs[MOE_WSPLIT:2 * MOE_WSPLIT]
    w2_ref, o_ref, xb_s = refs[2 * MOE_WSPLIT:]
    s = pl.program_id(0)
    f = pl.program_id(1)
    tm = MOE_TM

    @pl.when(s < ng_ref[0])
    def _():
        @pl.when(f == 0)
        def _():
            xb_s[...] = _bf(x_ref[...])
            o_ref[...] = jnp.zeros(o_ref.shape, F32)

        w1b = jnp.concatenate([_bf(r[0]) for r in w1_refs], axis=0)
        w3b = jnp.concatenate([_bf(r[0]) for r in w3_refs], axis=0)
        w2b = _bf(w2_ref[0])

        def tile(t):
            rows = pl.ds(t * tm, tm)
            h = xb_s[rows, :]
            u = _silu(_dot(h, w1b)) * _dot(h, w3b)
            o_ref[rows, :] += _dot(_bf(u), w2b)

        tile(0)
        for t in range(1, MOE_GROUP):
            pl.when(nt_ref[s] > t)(functools.partial(tile, t))

    @pl.when(jnp.logical_and(s >= ng_ref[0], f == pl.num_programs(1) - 1))
    def _():
        o_ref[...] = jnp.zeros(o_ref.shape, F32)


def _experts(group_e, group_tiles, n_groups, xsort, w1, w3, w2):
    n_slots, d = xsort.shape
    dff = w1.shape[2]
    tg, tf = MOE_TM * MOE_GROUP, MOE_TF
    nf = dff // tf

    def ss(s, ng):
        return jnp.maximum(jnp.minimum(s, ng[0] - 1), 0)

    def ff(s, f, ng):
        snake = jnp.where(s % 2 == 1, nf - 1 - f, f)
        last = jnp.where((ng[0] - 1) % 2 == 1, 0, nf - 1)
        return jnp.where(s < ng[0], snake, last)

    grid_spec = pltpu.PrefetchScalarGridSpec(
        num_scalar_prefetch=3,
        grid=(n_slots // tg, nf),
        in_specs=[pl.BlockSpec((tg, d), lambda s, f, ge, nt, ng: (ss(s, ng), 0)),
                  *[pl.BlockSpec((1, d // MOE_WSPLIT, tf),
                                 lambda s, f, ge, nt, ng, part=part: (ge[ss(s, ng)], part, ff(s, f, ng)))
                    for _ in range(2) for part in range(MOE_WSPLIT)],
                  pl.BlockSpec((1, tf, d), lambda s, f, ge, nt, ng: (ge[ss(s, ng)], ff(s, f, ng), 0))],
        out_specs=pl.BlockSpec((tg, d), lambda s, f, ge, nt, ng: (s, 0)),
        scratch_shapes=[pltpu.VMEM((tg, d), BF16)],
    )
    return pl.pallas_call(
        _expert_kernel,
        grid_spec=grid_spec,
        out_shape=jax.ShapeDtypeStruct((n_slots, d), F32),
        compiler_params=_cparams(("arbitrary", "arbitrary"), 56),
        name="experts",
    )(group_e, group_tiles, n_groups, xsort, *([w1] * MOE_WSPLIT), *([w3] * MOE_WSPLIT), w2)


def _combine_kernel(dest_ref, y_ref, meta_ref, x_ref, g_ref, mod_ref, o_ref, buf_s, sem):
    i = pl.program_id(0)
    tm = x_ref.shape[0]

    def row_copy(r, k):
        src = dest_ref[TOP_K * (i * tm + r) + k]
        return pltpu.make_async_copy(y_ref.at[pl.ds(src, 1)], buf_s.at[k, pl.ds(r, 1)], sem)

    def start(r, carry):
        for k in range(TOP_K):
            row_copy(r, k).start(priority=k)
        return carry

    lax.fori_loop(0, tm, start, 0, unroll=DMA_UNROLL)

    def wait(r, carry):
        for k in range(TOP_K):
            row_copy(r, k).wait()
        return carry

    lax.fori_loop(0, tm, wait, 0, unroll=DMA_UNROLL)

    meta = meta_ref[...]
    lane = lax.broadcasted_iota(jnp.int32, meta.shape, 1)
    g0 = jnp.sum(jnp.where(lane == _META_G0, meta, 0.0), axis=1, keepdims=True)
    g1 = jnp.sum(jnp.where(lane == _META_G1, meta, 0.0), axis=1, keepdims=True)
    y = buf_s[0] * g0 + buf_s[1] * g1
    m = mod_ref[0]
    o_ref[...] = x_ref[...] + m[5:6] * _rms(y, g_ref[...])


def _combine(dest, yb, meta, xs, n_rows, g3, modtab, lat_rows):
    d = xs.shape[1]
    nb = modtab.shape[0] - 1
    tm = ROW_TILE
    mi = _mod_index(tm, lat_rows, nb)
    grid_spec = pltpu.PrefetchScalarGridSpec(
        num_scalar_prefetch=1,
        grid=(n_rows // tm,),
        in_specs=[pl.BlockSpec(memory_space=pl.ANY),
                  pl.BlockSpec((tm, 128), lambda i, dst: (i, 0)),
                  pl.BlockSpec((tm, d), lambda i, dst: (i, 0)),
                  pl.BlockSpec((1, d), lambda i, dst: (0, 0)),
                  pl.BlockSpec((1, 6, d), lambda i, dst: (mi(i), 0, 0))],
        out_specs=pl.BlockSpec((tm, d), lambda i, dst: (i, 0)),
        scratch_shapes=[pltpu.VMEM((TOP_K, tm, d), F32), pltpu.SemaphoreType.DMA(())],
    )
    return pl.pallas_call(
        _combine_kernel,
        grid_spec=grid_spec,
        out_shape=jax.ShapeDtypeStruct((n_rows, d), F32),
        compiler_params=_cparams(("arbitrary",), 40),
        name="combine",
    )(dest, yb, meta, xs, g3.reshape(1, d), modtab)


def _moe(xs, n_rows, g2, g3, modtab, w_router, w1, w3, w2, lat_rows):
    meta, cnt = _router(xs, n_rows, g2, modtab, w_router, lat_rows)
    counts = cnt[0, :N_EXPERTS].astype(jnp.int32)
    tg = MOE_TM * MOE_GROUP
    padded = (counts + tg - 1) // tg * tg
    pad_end = jnp.cumsum(padded)
    pad_start = pad_end - padded
    e = meta[:, _META_E0:_META_E1 + 1].astype(jnp.int32)
    r = meta[:, _META_R0:_META_R1 + 1].astype(jnp.int32)
    dest = (pad_start[e] + r).reshape(-1)
    n_groups_max = (n_rows * TOP_K) // tg + N_EXPERTS
    n_slots = n_groups_max * tg
    group_start = jnp.arange(n_groups_max, dtype=jnp.int32) * tg
    group_e = jnp.minimum(jnp.sum((group_start[:, None] >= pad_end[None, :]).astype(jnp.int32), axis=1),
                          N_EXPERTS - 1)
    filled = counts[group_e] - (group_start - pad_start[group_e])
    group_tiles = jnp.clip((filled + MOE_TM - 1) // MOE_TM, 1, MOE_GROUP).astype(jnp.int32)
    n_groups = (pad_end[-1:] // tg).astype(jnp.int32)
    xsort = _dispatch(dest, xs, n_rows, g2, modtab, n_slots, lat_rows)
    yb = _experts(group_e, group_tiles, n_groups, xsort, w1, w3, w2)
    return _combine(dest, yb, meta, xs, n_rows, g3, modtab, lat_rows)


def _rotary_tables(lat_len, ctx_len):
    rows = lat_len // GRID_W
    row = jnp.repeat(jnp.arange(rows, dtype=F32), GRID_W)
    col = jnp.tile(jnp.arange(GRID_W, dtype=F32), rows)
    n_freq = RET_DK // 4
    inv = ROPE_BASE ** (-jnp.arange(n_freq, dtype=F32) / n_freq)
    ang = jnp.concatenate([row[:, None] * inv, col[:, None] * inv], axis=-1)
    cos, sin = jnp.cos(ang), jnp.sin(ang)
    cosf = jnp.concatenate([jnp.ones((ctx_len, RET_DK), F32), jnp.concatenate([cos, cos], -1)], 0)
    sinf = jnp.concatenate([jnp.zeros((ctx_len, RET_DK), F32), jnp.concatenate([-sin, sin], -1)], 0)
    return cosf, sinf


def kernel(x, c, ctx, c_ctx, w_ada, b_ada, norm_g, w_in, w_out, ret_decay, ret_gn, hgrn_lb, hgrn_gn,
           mlstm_conv, mlstm_gate_b, mlstm_gn, w_ffn1, w_ffn3, w_ffn2, w_router, w_exp1, w_exp3, w_exp2):
    nb, lat_len, d = x.shape
    ctx_len = ctx.shape[1]
    depth = w_ada.shape[0]
    n_lat_rows = nb * lat_len
    n_rows = n_lat_rows + nb * ctx_len

    x_parts = [x.reshape(n_lat_rows, d), ctx.reshape(nb * ctx_len, d)]
    w_in_t = jnp.swapaxes(w_in, 1, 2)
    s_in = jnp.zeros((8, d), F32).at[:nb].set(c).at[nb].set(c_ctx)
    mod_all = _ada(s_in, w_ada, b_ada)[:, :nb + 1].reshape(depth, nb + 1, 6, d)

    cosf, sinf = _rotary_tables(lat_len, ctx_len)
    hg_cum, hg_masks, hg_signs = _hgrn_constants()
    hg_consts = (jnp.asarray(hg_cum, BF16), jnp.asarray(hg_masks, F32), jnp.asarray(hg_signs, F32))
    sm = jax.nn.softmax(hgrn_lb.astype(F32), axis=0)
    lb_all = jnp.clip(jnp.cumsum(sm, axis=0) - sm[0], 0.0, 1.0)

    hb = None
    for layer in range(depth):
        last = layer == depth - 1
        modtab = mod_all[layer]
        g = norm_g[layer]

        if hb is None:
            hb = _prenorm(x_parts, n_rows, g[0], modtab, 1, 0, lat_len, BF16)
        p = _mm(hb, w_in_t, layer, MM_TM, MM_TN, n_cols=PROJ_MAIN)
        w_g = jnp.zeros((1, 128, d), F32).at[0, :MLSTM_GATES].set(w_in_t[layer, PROJ_MAIN:])
        gates = _mm(hb, w_g, 0, MM_TM, 128)[:, :MLSTM_GATES]
        hb = None
        g_row_lat = jnp.swapaxes(gates[:n_lat_rows].reshape(nb, lat_len, MLSTM_GATES), 1, 2)
        g_row_ctx = jnp.swapaxes(gates[n_lat_rows:].reshape(nb, ctx_len, MLSTM_GATES), 1, 2)

        log_g = jax.nn.log_sigmoid(ret_decay[layer].astype(F32))
        lg_b = jnp.broadcast_to(log_g[:, :, None, None], (2, RET_HEADS, 8, RET_DV))
        lb = lb_all[layer]
        emit_ctx = not last
        o_ret = _retention(p, nb, lat_len, ctx_len, cosf, sinf, lg_b, ret_gn[layer], emit_ctx)
        o_hg = _hgrn(p, nb, lat_len, ctx_len, jnp.log(lb), jnp.log1p(-lb), hgrn_gn[layer],
                     hg_consts, emit_ctx)
        o_ml = _mlstm(p, gates, g_row_lat, g_row_ctx, nb, lat_len, ctx_len, mlstm_gate_b[layer],
                      mlstm_conv[layer], mlstm_gn[layer], emit_ctx)
        mix_parts = [list(o) for o in (o_ret, o_hg, o_ml)]
        rows_now = n_lat_rows if last else n_rows
        dense = layer % 2 == 0
        res = _wout(mix_parts, _bf(w_out[layer]), x_parts, rows_now, g[1], modtab, lat_len,
                    g[2] if dense else None)
        xs = res[0]
        x_parts = [xs]

        j = layer // 2
        if dense:
            nxt = (None, None) if last else (norm_g[layer + 1][0], mod_all[layer + 1])
            res = _ffn(res[1], _bf(w_ffn1[j]), _bf(w_ffn3[j]), _bf(w_ffn2[j]), xs, g[3], modtab, lat_len, *nxt)
            xs = res[0]
            hb = None if last else res[1]
        else:
            xs = _moe(xs, rows_now, g[2], g[3], modtab, w_router[j], w_exp1[j], w_exp3[j], w_exp2[j], lat_len)
        x_parts = [xs]
    return xs[:n_lat_rows].reshape(nb, lat_len, d)
```

```python
import functools

import numpy as np
import jax
import jax.numpy as jnp
from jax import lax
from jax.experimental import pallas as pl
from jax.experimental.pallas import tpu as pltpu

F32 = jnp.float32
BF16 = jnp.bfloat16

CHUNK = 128
NORM_EPS = 1e-6
ROPE_BASE = 10000.0
GRID_W = 64

RET_HEADS, RET_DK, RET_DV = 4, 128, 256
HGRN_HEADS, HGRN_DK, HGRN_DV = 4, 128, 128
MLSTM_HEADS, MLSTM_DK, MLSTM_DV = 4, 128, 128
N_EXPERTS = 8
TOP_K = 2

RET_QK = RET_HEADS * RET_DK
RET_WIDTH = RET_HEADS * RET_DV
HGRN_QK = HGRN_HEADS * HGRN_DK
HGRN_WIDTH = HGRN_HEADS * HGRN_DV
MLSTM_QK = MLSTM_HEADS * MLSTM_DK
MLSTM_WIDTH = MLSTM_HEADS * MLSTM_DV
MLSTM_GATES = 4 * MLSTM_HEADS

_C_RET_Q = 0
_C_RET_K = _C_RET_Q + RET_QK // 128
_C_RET_V = _C_RET_K + RET_QK // 128
_C_RET_G = _C_RET_V + RET_WIDTH // 128
_C_HG_Q = _C_RET_G + RET_WIDTH // 128
_C_HG_FF = _C_HG_Q + HGRN_QK // 128
_C_HG_FB = _C_HG_FF + HGRN_QK // 128
_C_HG_I = _C_HG_FB + HGRN_QK // 128
_C_HG_G = _C_HG_I + HGRN_WIDTH // 128
_C_ML_Q = _C_HG_G + HGRN_WIDTH // 128
_C_ML_K = _C_ML_Q + MLSTM_QK // 128
_C_ML_V = _C_ML_K + MLSTM_QK // 128
_C_ML_O = _C_ML_V + MLSTM_WIDTH // 128
PROJ_MAIN = (_C_ML_O + MLSTM_WIDTH // 128) * 128

_VMEM_CAP_BYTES = 56 * 1024 * 1024

ROW_TILE = 512
MM_TM, MM_TN = 1024, 1280
FFN_TF = 512
MOE_TM = 768
MOE_WSPLIT = 1
MOE_GROUP = 1
MOE_TF = 512
RET_UNROLL = (4, 4)
HGRN_UNROLL = (2, 4)
MLSTM_UNROLL = (2, 2)
DMA_UNROLL = 8
EPI_ROWS = None


def _cparams(sem, vmem_mb):
    return pltpu.CompilerParams(dimension_semantics=sem,
                                vmem_limit_bytes=min(int(vmem_mb * 1024 * 1024), _VMEM_CAP_BYTES))


def _bf(x):
    return x.astype(BF16)


def _dot(a, b):
    return jnp.dot(a, b, preferred_element_type=F32)


def _dot_nt(a, b):
    return lax.dot_general(a, b, (((1,), (1,)), ((), ())), preferred_element_type=F32)


def _dot_tn(a, b):
    return lax.dot_general(a, b, (((0,), (0,)), ((), ())), preferred_element_type=F32)


def _sigmoid(x):
    return 1.0 / (1.0 + jnp.exp(-x))


def _silu(x):
    return x * _sigmoid(x)


def _log_sigmoid(x):
    return jnp.minimum(x, 0.0) - jnp.log(1.0 + jnp.exp(-jnp.abs(x)))


def _rms(x, g):
    return x * lax.rsqrt(jnp.mean(x * x, axis=-1, keepdims=True) + NORM_EPS) * g


def _mod_index(tile_rows, n_lat_rows_per_batch, n_batch):
    return lambda i: jnp.minimum((i * tile_rows) // n_lat_rows_per_batch, n_batch)


def _ada_kernel(s_ref, w_ref, b_ref, o_ref):
    s = _bf(_silu(s_ref[...]))
    o_ref[0] = _dot(s, _bf(w_ref[0])) + b_ref[0]


def _ada(s_in, w_ada, b_ada):
    depth, d, n = w_ada.shape
    tn = 1024
    return pl.pallas_call(
        _ada_kernel,
        grid=(depth, n // tn),
        in_specs=[pl.BlockSpec((8, d), lambda l, j: (0, 0)),
                  pl.BlockSpec((1, d, tn), lambda l, j: (l, 0, j)),
                  pl.BlockSpec((1, 1, tn), lambda l, j: (l, 0, j))],
        out_specs=pl.BlockSpec((1, 8, tn), lambda l, j: (l, 0, j)),
        out_shape=jax.ShapeDtypeStruct((depth, 8, n), F32),
        compiler_params=_cparams(("arbitrary", "arbitrary"), 40),
        name="ada",
    )(s_in, w_ada, b_ada.reshape(depth, 1, n))


def _row_chunks(n_rows, fn):
    if EPI_ROWS is None:
        fn(pl.ds(0, n_rows))
        return

    def body(i, carry):
        fn(pl.ds(pl.multiple_of(i * EPI_ROWS, EPI_ROWS), EPI_ROWS))
        return carry
    lax.fori_loop(0, n_rows // EPI_ROWS, body, 0, unroll=2)


def _part_specs(parts, tm, width):
    if len(parts) == 1:
        return [pl.BlockSpec((tm, width), lambda i: (i, 0))]
    n0 = parts[0].shape[0] // tm
    return [pl.BlockSpec((tm, width), lambda i: (jnp.minimum(i, n0 - 1), 0)),
            pl.BlockSpec((tm, width), lambda i: (jnp.maximum(i - n0, 0), 0))]


def _on_part(n_first, n_parts, fn):
    if n_parts == 1:
        fn(0)
    else:
        i = pl.program_id(0)
        pl.when(i < n_first)(lambda: fn(0))
        pl.when(i >= n_first)(lambda: fn(1))


def _prenorm_kernel(*refs, sc, sh, n_first):
    x_refs, (g_ref, mod_ref, o_ref) = refs[:-3], refs[-3:]

    def run(part):
        m = mod_ref[0]
        g = g_ref[...]

        def chunk(rows):
            y = _rms(x_refs[part][rows, :], g)
            o_ref[rows, :] = (y * (1.0 + m[sc:sc + 1]) + m[sh:sh + 1]).astype(o_ref.dtype)

        _row_chunks(o_ref.shape[0], chunk)

    _on_part(n_first, len(x_refs), run)


def _prenorm(x_parts, n_rows, g, modtab, sc, sh, lat_rows, out_dtype):
    d = x_parts[0].shape[1]
    nb = modtab.shape[0] - 1
    mi = _mod_index(ROW_TILE, lat_rows, nb)
    return pl.pallas_call(
        functools.partial(_prenorm_kernel, sc=sc, sh=sh, n_first=x_parts[0].shape[0] // ROW_TILE),
        grid=(n_rows // ROW_TILE,),
        in_specs=[*_part_specs(x_parts, ROW_TILE, d),
                  pl.BlockSpec((1, d), lambda i: (0, 0)),
                  pl.BlockSpec((1, 6, d), lambda i: (mi(i), 0, 0))],
        out_specs=pl.BlockSpec((ROW_TILE, d), lambda i: (i, 0)),
        out_shape=jax.ShapeDtypeStruct((n_rows, d), out_dtype),
        compiler_params=_cparams(("arbitrary",), 32),
        name="prenorm",
    )(*x_parts, g.reshape(1, d), modtab)


def _mm_kernel(x_ref, wt_ref, o_ref, wb_s):
    @pl.when(pl.program_id(1) == 0)
    def _():
        wb_s[...] = _bf(wt_ref[0])

    o_ref[...] = _dot_nt(x_ref[...], wb_s[...]).astype(o_ref.dtype)


def _mm(x, wt, layer, tm, tn, n_cols=None, out_dtype=F32):
    m, k = x.shape
    n = wt.shape[1] if n_cols is None else n_cols
    return pl.pallas_call(
        _mm_kernel,
        grid=(n // tn, m // tm),
        in_specs=[pl.BlockSpec((tm, k), lambda j, i: (i, 0)),
                  pl.BlockSpec((1, tn, k), lambda j, i: (layer, j, 0))],
        out_specs=pl.BlockSpec((tm, tn), lambda j, i: (i, j)),
        out_shape=jax.ShapeDtypeStruct((m, n), out_dtype),
        scratch_shapes=[pltpu.VMEM((tn, k), BF16)],
        compiler_params=_cparams(("arbitrary", "arbitrary"), 48),
        name="proj",
    )(x, wt)


def _bwd_chunk(i, n_ctx, n_lat):
    return jnp.where(i < n_ctx, n_ctx - 1 - i, 2 * n_ctx + n_lat - 1 - i)


def _rows(c):
    return pl.ds(pl.multiple_of(c * CHUNK, CHUNK), CHUNK)


def _for_chunks(n_ctx, n_lat, fn, unroll, with_ctx=True):
    def run(part, n, off):
        def body(c, carry):
            fn(part, c, c + off)
            return carry
        lax.fori_loop(0, n, body, 0, unroll=max(u for u in (1, 2, unroll) if n % u == 0 and u <= unroll))
    if with_ctx:
        run(0, n_ctx, 0)
    run(1, n_lat, n_ctx)


def _lanes(d, w):
    return slice(d * w, (d + 1) * w)


def _mixer_call(kernel_fn, name, in_specs, args, nb, heads, lat_len, ctx_len, dv, scratch, emit_ctx, vmem_mb):
    width = heads * dv
    out_specs = [pl.BlockSpec((lat_len, dv), lambda b, h: (b, h))]
    out_shape = [jax.ShapeDtypeStruct((nb * lat_len, width), BF16)]
    if emit_ctx:
        out_specs.append(pl.BlockSpec((ctx_len, dv), lambda b, h: (b, h)))
        out_shape.append(jax.ShapeDtypeStruct((nb * ctx_len, width), BF16))
    n_in = len(in_specs)

    def body(*refs):
        ins, rest = refs[:n_in], refs[n_in:]
        if emit_ctx:
            ol, oc, scr = rest[0], rest[1], rest[2:]
        else:
            ol, oc, scr = rest[0], None, rest[1:]
        kernel_fn(*ins, ol, oc, *scr, n_ctx=ctx_len // CHUNK, n_lat=lat_len // CHUNK)

    return pl.pallas_call(
        body,
        grid=(nb, heads),
        in_specs=in_specs,
        out_specs=out_specs,
        out_shape=out_shape,
        scratch_shapes=scratch,
        compiler_params=_cparams(("arbitrary", "arbitrary"), vmem_mb),
        name=name,
    )(*args)


def _seq_specs(col0, width, nb, lat_len, ctx_len):
    n_lat_blk = nb * lat_len // ctx_len
    c0 = col0 * 128 // width
    return (pl.BlockSpec((lat_len, width), lambda b, h: (b, c0 + h)),
            pl.BlockSpec((ctx_len, width), lambda b, h: (n_lat_blk + b, c0 + h)))


def _ret_kernel(ql, qc, kl, kc, vl, vc, gl, gc, cos_ref, sin_ref, lg_ref, gn_ref, ol_ref, oc_ref,
                qs_s, kv_s, sall_s, o_s, st_s, dm_s, dq_s, dk_s, *, n_ctx, n_lat):
    nch = n_ctx + n_lat
    unroll_a, unroll_c = RET_UNROLL
    qr, kr, vr, gr, outr = (qc, ql), (kc, kl), (vc, vl), (gc, gl), (oc_ref, ol_ref)
    t_i = lax.broadcasted_iota(jnp.int32, (CHUNK, CHUNK), 0).astype(F32)
    s_i = lax.broadcasted_iota(jnp.int32, (CHUNK, CHUNK), 1).astype(F32)
    dm = None
    for d in range(2):
        lg = lg_ref[d, 0][0:1, 0:CHUNK]
        rel = (t_i - s_i) if d == 0 else (s_i - t_i)
        dmd = jnp.where(rel >= 0, jnp.exp(jnp.maximum(rel, 0.0) * lg), 0.0)
        dm = dmd if dm is None else dm + dmd
        p = t_i if d == 0 else (CHUNK - 1.0) - t_i
        dq_s[d] = jnp.exp((p + 1.0) * lg)
        dk_s[d] = jnp.exp((CHUNK - 1.0 - p) * lg)
    dm_s[...] = dm
    scale = RET_DK ** -0.5

    def phase_a(part, cl, cg):
        rl, rg = _rows(cl), _rows(cg)
        cs = cos_ref[rg, :]
        sn = sin_ref[rg, :]
        q = qr[part][rl, :]
        k = kr[part][rl, :]
        q = q * cs + pltpu.roll(q, RET_DK // 2, 1) * sn
        k = (k * cs + pltpu.roll(k, RET_DK // 2, 1) * sn) * scale
        vb = _bf(vr[part][rl, :])
        s = _dot_nt(_bf(q), _bf(k))
        o_s[rg, :] = _dot(_bf(s * dm_s[...]), vb)
        qs_s[rg, :] = jnp.concatenate([_bf(q * dq_s[0]), _bf(q * dq_s[1])], axis=1)
        kd = jnp.concatenate([_bf(k * dk_s[0]), _bf(k * dk_s[1])], axis=1)
        kv_s[cg] = _dot_tn(kd, vb)

    _for_chunks(n_ctx, n_lat, phase_a, unroll_a)

    st_s[...] = jnp.zeros(st_s.shape, F32)
    dec = [jnp.exp(float(CHUNK) * lg_ref[d, 0][0:1, :]) for d in range(2)]

    def phase_b(i, carry):
        for d, c in ((0, i), (1, _bwd_chunk(i, n_ctx, n_lat))):
            st = st_s[d]
            sall_s[c, _lanes(d, RET_DK), :] = _bf(st)
            st_s[d] = dec[d] * st + kv_s[c, _lanes(d, RET_DK), :]
        return carry

    lax.fori_loop(0, nch, phase_b, 0)
    gn = gn_ref[...]

    def phase_c(part, cl, cg):
        rl, rg = _rows(cl), _rows(cg)
        o = o_s[rg, :] + _dot(qs_s[rg, :], sall_s[cg])
        outr[part][rl, :] = (_rms(o, gn) * _silu(gr[part][rl, :])).astype(BF16)

    _for_chunks(n_ctx, n_lat, phase_c, unroll_c, with_ctx=oc_ref is not None)


def _retention(p, nb, lat_len, ctx_len, cosf, sinf, lg_b, gn, emit_ctx):
    t_len = lat_len + ctx_len
    nch = t_len // CHUNK
    s128 = functools.partial(_seq_specs, width=128, nb=nb, lat_len=lat_len, ctx_len=ctx_len)
    s256 = functools.partial(_seq_specs, width=256, nb=nb, lat_len=lat_len, ctx_len=ctx_len)
    in_specs = [*s128(_C_RET_Q), *s128(_C_RET_K), *s256(_C_RET_V), *s256(_C_RET_G),
                pl.BlockSpec((t_len, 128), lambda b, h: (0, 0)),
                pl.BlockSpec((t_len, 128), lambda b, h: (0, 0)),
                pl.BlockSpec((2, 1, 8, RET_DV), lambda b, h: (0, h, 0, 0)),
                pl.BlockSpec((1, RET_DV), lambda b, h: (0, h))]
    scratch = [pltpu.VMEM((t_len, 2 * RET_DK), BF16),
               pltpu.VMEM((nch, 2 * RET_DK, RET_DV), F32),
               pltpu.VMEM((nch, 2 * RET_DK, RET_DV), BF16),
               pltpu.VMEM((t_len, RET_DV), F32),
               pltpu.VMEM((2, RET_DK, RET_DV), F32),
               pltpu.VMEM((CHUNK, CHUNK), F32),
               pltpu.VMEM((2, CHUNK, CHUNK), F32), pltpu.VMEM((2, CHUNK, CHUNK), F32)]
    args = (p, p, p, p, p, p, p, p, cosf, sinf, lg_b, gn.reshape(1, RET_WIDTH))
    return _mixer_call(_ret_kernel, "retention", in_specs, args, nb, RET_HEADS, lat_len, ctx_len, RET_DV,
                       scratch, emit_ctx, 48)


_HG_LEVELS = (64, 32, 16, 8, 4, 2, 1)


def _hgrn_constants():
    c = CHUNK
    t = np.arange(c)[:, None]
    u = np.arange(c)[None, :]
    cum = (u <= t).astype(np.float32)
    masks, signs = [], []
    for m in _HG_LEVELS:
        base = (t // (2 * m)) * (2 * m)
        lower = t >= base + m
        tb = t // (2 * m)
        sb = u // (2 * m)
        masks.append((tb == sb) & lower & (u < (sb * 2 * m + m)))
        signs.append(np.broadcast_to(np.where(lower, 1.0, -1.0), (c, c)))
    masks.append(t == u)
    kf = np.stack([x.astype(np.float32) for x in masks], axis=0)
    kb = np.stack([x.astype(np.float32)[::-1, ::-1] for x in masks], axis=0)
    sf = np.stack([x.astype(np.float32) for x in signs], axis=0)
    sb_ = np.stack([x.astype(np.float32)[::-1, ::-1] for x in signs], axis=0)
    return (np.stack([cum, cum[::-1, ::-1]], 0), np.stack([kf, kb], 0),
            np.stack([sf[:_HG_WIDE], sb_[:_HG_WIDE]], 0))


_HG_WIDE = 5


def _hgrn_level_exponents(b, lf, d, sgn_ref, row):
    out = []
    for l, m in enumerate(_HG_LEVELS[:_HG_WIDE]):
        pieces = []
        for j in range(CHUNK // (2 * m)):
            r = 2 * m * j + (m - 1 if d == 0 else m)
            pieces.append(jnp.broadcast_to(b[r:r + 1, :], (2 * m, CHUNK)))
        bref = pieces[0] if len(pieces) == 1 else jnp.concatenate(pieces, axis=0)
        out.append((b - bref) * sgn_ref[d, l])
    up = pltpu.roll(lf, CHUNK - 1, 0)
    dn = pltpu.roll(lf, 1, 0)
    r4 = row % 4
    if d == 0:
        e2 = jnp.where(r4 == 0, up, jnp.where(r4 == 1, 0.0, jnp.where(r4 == 2, lf, lf + dn)))
        e1 = jnp.where(row % 2 == 1, lf, 0.0)
    else:
        e2 = jnp.where(r4 == 0, lf + up, jnp.where(r4 == 1, lf, jnp.where(r4 == 2, 0.0, dn)))
        e1 = jnp.where(row % 2 == 0, lf, 0.0)
    return out + [e2, e1]


def _hgrn_kernel(ql, qc, ffl, ffc, fbl, fbc, il, ic, gl, gc, llb_ref, lub_ref, gn_ref, c_ref, k_ref, sgn_ref,
                 ol_ref, oc_ref, qs_s, kv_s, sall_s, dec_s, o_s, st_s, *, n_ctx, n_lat):
    nch = n_ctx + n_lat
    unroll_a, unroll_c = HGRN_UNROLL
    qr, fr, ir, gr, outr =(qc, ql), ((ffc, ffl), (fbc, fbl)), (ic, il), (gc, gl), (oc_ref, ol_ref)
    llb = llb_ref[...]
    lub = lub_ref[...]
    nlev = len(_HG_LEVELS)
    row = lax.broadcasted_iota(jnp.int32, (CHUNK, HGRN_DK), 0)

    def phase_a(part, cl, cg):
        rl, rg = _rows(cl), _rows(cg)
        q = _silu(qr[part][rl, :])
        qb = _bf(q)
        vb = _bf(ir[part][rl, :])
        a_sum, qs, kds = None, [], []
        for d in range(2):
            lsg = lub + _log_sigmoid(fr[d][part][rl, :])
            lf = jnp.maximum(llb, lsg) + jnp.log(1.0 + jnp.exp(-jnp.abs(llb - lsg)))
            k = 1.0 - jnp.exp(lf)
            lf_hi = _bf(lf)
            lf_lo = _bf(lf - lf_hi.astype(F32))
            b2 = _dot(c_ref[d], jnp.concatenate([lf_hi, lf_lo], axis=1))
            b = b2[:, 0:HGRN_DK] + b2[:, HGRN_DK:]
            es = _hgrn_level_exponents(b, lf, d, sgn_ref, row)
            kb = _bf(k)
            a = k_ref[d, nlev] * _dot_nt(qb, kb)
            for l in range(nlev):
                xb = _bf(jnp.exp(es[l]))
                a = a + k_ref[d, l] * _dot_nt(qb * xb, kb * xb)
            a_sum = a if a_sum is None else a_sum + a
            b_last = b[CHUNK - 1:CHUNK] if d == 0 else b[0:1]
            qs.append(_bf(q * jnp.exp(b)))
            kds.append(_bf(k * jnp.exp(b_last - b)))
            dec_s[cg, :, _lanes(d, HGRN_DK)] = jnp.exp(b_last)
        o_s[rg, :] = _dot(_bf(a_sum), vb)
        qs_s[rg, :] = jnp.concatenate(qs, axis=1)
        kv_s[cg] = _dot_tn(vb, jnp.concatenate(kds, axis=1))

    _for_chunks(n_ctx, n_lat, phase_a, unroll_a)
    st_s[...] = jnp.zeros(st_s.shape, F32)

    def phase_b(i, carry):
        for d, c in ((0, i), (1, _bwd_chunk(i, n_ctx, n_lat))):
            ln = _lanes(d, HGRN_DK)
            st = st_s[d]
            sall_s[c, :, ln] = _bf(st)
            st_s[d] = dec_s[c, :, ln] * st + kv_s[c, :, ln]
        return carry

    lax.fori_loop(0, nch, phase_b, 0)
    gn = gn_ref[...]

    def phase_c(part, cl, cg):
        rl, rg = _rows(cl), _rows(cg)
        o = o_s[rg, :] + _dot_nt(qs_s[rg, :], sall_s[cg])
        outr[part][rl, :] = (_rms(o, gn) * _silu(gr[part][rl, :])).astype(BF16)

    _for_chunks(n_ctx, n_lat, phase_c, unroll_c, with_ctx=oc_ref is not None)


def _hgrn(p, nb, lat_len, ctx_len, log_lb, log_ub, gn, consts, emit_ctx):
    mats, masks, signs = consts
    t_len = lat_len + ctx_len
    nch = t_len // CHUNK
    spec = functools.partial(_seq_specs, width=128, nb=nb, lat_len=lat_len, ctx_len=ctx_len)
    vec = pl.BlockSpec((1, 128), lambda b, h: (0, h))
    in_specs = [*spec(_C_HG_Q), *spec(_C_HG_FF), *spec(_C_HG_FB), *spec(_C_HG_I), *spec(_C_HG_G),
                vec, vec, vec,
                pl.BlockSpec(mats.shape, lambda b, h: (0, 0, 0)),
                pl.BlockSpec(masks.shape, lambda b, h: (0, 0, 0, 0)),
                pl.BlockSpec(signs.shape, lambda b, h: (0, 0, 0, 0))]
    scratch = [pltpu.VMEM((t_len, 2 * HGRN_DK), BF16),
               pltpu.VMEM((nch, HGRN_DV, 2 * HGRN_DK), F32),
               pltpu.VMEM((nch, HGRN_DV, 2 * HGRN_DK), BF16),
               pltpu.VMEM((nch, 1, 2 * HGRN_DK), F32),
               pltpu.VMEM((t_len, HGRN_DV), F32),
               pltpu.VMEM((2, HGRN_DV, HGRN_DK), F32)]
    args = (p, p, p, p, p, p, p, p, p, p, log_lb.reshape(1, HGRN_QK), log_ub.reshape(1, HGRN_QK),
            gn.reshape(1, HGRN_WIDTH), mats, masks, signs)
    return _mixer_call(_hgrn_kernel, "hgrn2", in_specs, args, nb, HGRN_HEADS, lat_len, ctx_len, HGRN_DV,
                       scratch, emit_ctx, 48)


def _mlstm_kernel(ql, qc, kl, kc, vl, vc, ogl, ogc, gcl, gcc, grl, grc, bc_ref, br_ref,
                  wq_ref, wk_ref, gn_ref, ol_ref, oc_ref,
                  q_s, gr_s, intra_s, kv_s, call_s, rho_s, bcol_s, bl_s, mu_s, mprev_s, st_s, m_s,
                  *, n_ctx, n_lat):
    nch = n_ctx + n_lat
    unroll_a, unroll_c = MLSTM_UNROLL
    h = pl.program_id(1)
    nh = MLSTM_HEADS
    dk, dv = MLSTM_DK, MLSTM_DV
    ext = 2 * dv
    qr, kr, vr, ogr, gcr, outr = (qc, ql), (kc, kl), (vc, vl), (ogc, ogl), (gcc, gcl), (oc_ref, ol_ref)
    n_loc = (n_ctx, n_lat)

    for cc in range(nch):
        src, c0 = (grc, cc) if cc < n_ctx else (grl, cc - n_ctx)
        gr_s[cc] = src[0, :, c0 * CHUNK:(c0 + 1) * CHUNK] + br_ref[...]

    row = lax.broadcasted_iota(jnp.int32, (CHUNK, dk), 0)
    lane = lax.broadcasted_iota(jnp.int32, (CHUNK, dv), 1)
    del lane
    ones_col = jnp.ones((CHUNK, dv), BF16)
    scale = dk ** -0.5
    t_i = lax.broadcasted_iota(jnp.int32, (CHUNK, CHUNK), 0)
    s_i = lax.broadcasted_iota(jnp.int32, (CHUNK, CHUNK), 1)
    lane16 = lax.broadcasted_iota(jnp.int32, (CHUNK, MLSTM_GATES), 1)
    sub16 = lax.broadcasted_iota(jnp.int32, (MLSTM_GATES, CHUNK), 0)

    def conv(src, w_ref, part, cl):
        r0 = pl.multiple_of(cl * CHUNK, CHUNK)
        n_rows = n_loc[part] * CHUNK
        x = src[pl.ds(r0, CHUNK), :]
        pr = src[pl.ds(jnp.maximum(r0 - 1, 0), 1), :]
        nx = src[pl.ds(jnp.minimum(r0 + CHUNK, n_rows - 1), 1), :]
        pr = jnp.where(cl != 0, pr, 0.0)
        nx = jnp.where(cl != n_loc[part] - 1, nx, 0.0)
        xp = jnp.where(row == 0, pr, pltpu.roll(x, 1, 0))
        xn = jnp.where(row == CHUNK - 1, nx, pltpu.roll(x, CHUNK - 1, 0))
        w = w_ref[...]
        return _silu(w[0:1] * xp + w[1:2] * x + w[2:3] * xn)

    def pick_col(g, j):
        return jnp.sum(jnp.where(lane16 == j, g, 0.0), axis=1, keepdims=True)

    def pick_row(g, j):
        return jnp.sum(jnp.where(sub16 == j, g, 0.0), axis=0, keepdims=True)

    def phase_a(part, cl, cg):
        rl, rg = _rows(cl), _rows(cg)
        q = conv(qr[part], wq_ref, part, cl)
        k = conv(kr[part], wk_ref, part, cl) * scale
        qb = _bf(q)
        q_s[rg, :] = qb
        vb = jnp.concatenate([_bf(vr[part][rl, :]), ones_col], axis=1)
        s = _dot_nt(qb, _bf(k))
        g_c = gcr[part][rl, :] + bc_ref[...]
        g_r = gr_s[cg]
        kws = []
        for d in range(2):
            ig_c = pick_col(g_c, d * nh + h)
            lf_c = _log_sigmoid(pick_col(g_c, 2 * nh + d * nh + h))
            ig_r = pick_row(g_r, d * nh + h)
            lf_r = _log_sigmoid(pick_row(g_r, 2 * nh + d * nh + h))
            tri = (s_i <= t_i) if d == 0 else (s_i >= t_i)
            tri_t = (t_i <= s_i) if d == 0 else (t_i >= s_i)
            b_c = jnp.sum(jnp.where(tri, lf_r, 0.0), axis=1, keepdims=True)
            b_r = jnp.sum(jnp.where(tri_t, lf_c, 0.0), axis=0, keepdims=True)
            dmat = jnp.where(tri, b_c - b_r + ig_r, -jnp.inf)
            rho = jnp.max(dmat, axis=1, keepdims=True)
            intra_s[d, rg, :] = _dot(_bf(s * jnp.exp(dmat - rho)), vb)
            b_last = b_c[CHUNK - 1:CHUNK] if d == 0 else b_c[0:1]
            mu = jnp.max(b_last - b_r + ig_r, axis=1, keepdims=True)
            kws.append(_bf(k * jnp.exp(b_last - b_c + ig_c - mu)))
            rho_s[d, rg, :] = jnp.broadcast_to(rho, (CHUNK, 128))
            bcol_s[d, rg, :] = jnp.broadcast_to(b_c, (CHUNK, 128))
            bl_s[cg, :, _lanes(d, 128)] = jnp.broadcast_to(b_last, (1, 128))
            mu_s[cg, :, _lanes(d, 128)] = jnp.broadcast_to(mu, (1, 128))
        kv_s[cg] = _dot_tn(jnp.concatenate(kws, axis=1), vb)

    _for_chunks(n_ctx, n_lat, phase_a, unroll_a)
    st_s[...] = jnp.zeros(st_s.shape, F32)
    m_s[...] = jnp.zeros(m_s.shape, F32)

    def wide(v):
        return jnp.concatenate([v, v], axis=1)

    def phase_b(i, carry):
        for d, c in ((0, i), (1, _bwd_chunk(i, n_ctx, n_lat))):
            ln = _lanes(d, 128)
            m_prev = m_s[d]
            st = st_s[d]
            mprev_s[c, :, ln] = m_prev
            call_s[c, :, _lanes(d, ext)] = _bf(st)
            bl = bl_s[c, :, ln]
            mu = mu_s[c, :, ln]
            m_new = jnp.maximum(bl + m_prev, mu)
            st_s[d] = (wide(jnp.exp(bl + m_prev - m_new)) * st
                       + wide(jnp.exp(mu - m_new)) * kv_s[c, _lanes(d, dk), :])
            m_s[d] = m_new
        return carry

    lax.fori_loop(0, nch, phase_b, 0)
    gn = gn_ref[...]

    def phase_c(part, cl, cg):
        rl, rg = _rows(cl), _rows(cg)
        qc_all = _dot(q_s[rg, :], call_s[cg])
        o = None
        for d in range(2):
            rho = rho_s[d, rg, :]
            b_c = bcol_s[d, rg, :]
            m_prev = mprev_s[cg, :, _lanes(d, 128)]
            m_t = jnp.maximum(rho, b_c + m_prev)
            nd = (wide(jnp.exp(rho - m_t)) * intra_s[d, rg, :]
                  + wide(jnp.exp(b_c + m_prev - m_t)) * qc_all[:, _lanes(d, ext)])
            hh = nd[:, 0:dv] / jnp.maximum(jnp.abs(nd[:, dv:]), jnp.exp(-m_t))
            o = hh if o is None else o + hh
        y = o - jnp.mean(o, axis=-1, keepdims=True)
        y = y * lax.rsqrt(jnp.mean(y * y, axis=-1, keepdims=True) + NORM_EPS)
        outr[part][rl, :] = (y * gn * _sigmoid(ogr[part][rl, :])).astype(BF16)

    _for_chunks(n_ctx, n_lat, phase_c, unroll_c, with_ctx=oc_ref is not None)


def _mlstm(p, g_col, g_row_lat, g_row_ctx, nb, lat_len, ctx_len, gate_b, conv_w, gn, emit_ctx):
    t_len = lat_len + ctx_len
    nch = t_len // CHUNK
    n_lat_blk = nb * lat_len // ctx_len
    ng = MLSTM_GATES
    dk, dv = MLSTM_DK, MLSTM_DV
    spec = functools.partial(_seq_specs, width=128, nb=nb, lat_len=lat_len, ctx_len=ctx_len)
    in_specs = [*spec(_C_ML_Q), *spec(_C_ML_K), *spec(_C_ML_V), *spec(_C_ML_O),
                pl.BlockSpec((lat_len, ng), lambda b, h: (b, 0)),
                pl.BlockSpec((ctx_len, ng), lambda b, h: (n_lat_blk + b, 0)),
                pl.BlockSpec((1, ng, lat_len), lambda b, h: (b, 0, 0)),
                pl.BlockSpec((1, ng, ctx_len), lambda b, h: (b, 0, 0)),
                pl.BlockSpec((1, ng), lambda b, h: (0, 0)),
                pl.BlockSpec((ng, 1), lambda b, h: (0, 0)),
                pl.BlockSpec((3, 128), lambda b, h: (0, h)),
                pl.BlockSpec((3, 128), lambda b, h: (0, MLSTM_HEADS + h)),
                pl.BlockSpec((1, 128), lambda b, h: (0, h))]
    scratch = [pltpu.VMEM((t_len, dk), BF16),
               pltpu.VMEM((nch, ng, CHUNK), F32),
               pltpu.VMEM((2, t_len, 2 * dv), F32),
               pltpu.VMEM((nch, 2 * dk, 2 * dv), F32),
               pltpu.VMEM((nch, dk, 4 * dv), BF16),
               pltpu.VMEM((2, t_len, 128), F32), pltpu.VMEM((2, t_len, 128), F32),
               pltpu.VMEM((nch, 1, 256), F32), pltpu.VMEM((nch, 1, 256), F32), pltpu.VMEM((nch, 1, 256), F32),
               pltpu.VMEM((2, dk, 2 * dv), F32), pltpu.VMEM((2, 1, 128), F32)]
    args = (p, p, p, p, p, p, p, p, g_col, g_col, g_row_lat, g_row_ctx,
            gate_b.reshape(1, ng), gate_b.reshape(ng, 1), conv_w, conv_w, gn.reshape(1, MLSTM_WIDTH))
    return _mixer_call(_mlstm_kernel, "mlstm", in_specs, args, nb, MLSTM_HEADS, lat_len, ctx_len, dv,
                       scratch, emit_ctx, 48)


def _wout_kernel(*refs, n_lat_tiles, n_mix_parts, n_x_parts, emit_next):
    it = iter(refs)
    mix = [[next(it) for _ in range(n_mix_parts)] for _ in range(3)]
    x_parts = [next(it) for _ in range(n_x_parts)]
    wr_ref, wh_ref, wm_ref, g_ref, mod_ref = (next(it) for _ in range(5))
    g2_ref = next(it) if emit_next else None
    o_ref = next(it)
    h2_ref = next(it) if emit_next else None

    def run(part):
        pm = min(part, n_mix_parts - 1)
        y = (_dot(mix[0][pm][...], wr_ref[...]) + _dot(mix[1][pm][...], wh_ref[...])
             + _dot(mix[2][pm][...], wm_ref[...]))
        o_ref[...] = y
        m = mod_ref[0]
        x_ref = x_parts[min(part, n_x_parts - 1)]
        g = g_ref[...]
        g2 = g2_ref[...] if emit_next else None

        def chunk(rows):
            xn = x_ref[rows, :] + m[2:3] * _rms(o_ref[rows, :], g)
            o_ref[rows, :] = xn
            if emit_next:
                h2_ref[rows, :] = (_rms(xn, g2) * (1.0 + m[4:5]) + m[3:4]).astype(h2_ref.dtype)

        _row_chunks(o_ref.shape[0], chunk)

    _on_part(n_lat_tiles, max(n_mix_parts, n_x_parts), run)


def _wout(mix_parts, w_out, x_parts, n_rows, g1, modtab, lat_rows, g2_next):
    d = x_parts[0].shape[1]
    nb = modtab.shape[0] - 1
    mi = _mod_index(ROW_TILE, lat_rows, nb)
    tm = ROW_TILE
    n_lat_tiles = mix_parts[0][0].shape[0] // tm
    emit_next = g2_next is not None
    widths = (RET_WIDTH, HGRN_WIDTH, MLSTM_WIDTH)
    w_r = w_out[0:RET_WIDTH]
    w_h = w_out[RET_WIDTH:RET_WIDTH + HGRN_WIDTH]
    w_m = w_out[RET_WIDTH + HGRN_WIDTH:]
    full = lambda a: pl.BlockSpec(a.shape, lambda i: (0, 0))
    in_specs, args = [], []
    for parts, w in zip(mix_parts, widths):
        in_specs += _part_specs(parts, tm, w)
        args += list(parts)
    in_specs += _part_specs(x_parts, tm, d)
    args += list(x_parts)
    in_specs += [full(w_r), full(w_h), full(w_m),
                 pl.BlockSpec((1, d), lambda i: (0, 0)),
                 pl.BlockSpec((1, 6, d), lambda i: (mi(i), 0, 0))]
    args += [w_r, w_h, w_m, g1.reshape(1, d), modtab]
    out_specs = [pl.BlockSpec((tm, d), lambda i: (i, 0))]
    out_shape = [jax.ShapeDtypeStruct((n_rows, d), F32)]
    if emit_next:
        in_specs.append(pl.BlockSpec((1, d), lambda i: (0, 0)))
        args.append(g2_next.reshape(1, d))
        out_specs.append(pl.BlockSpec((tm, d), lambda i: (i, 0)))
        out_shape.append(jax.ShapeDtypeStruct((n_rows, d), BF16))
    return pl.pallas_call(
        functools.partial(_wout_kernel, n_lat_tiles=n_lat_tiles, n_mix_parts=len(mix_parts[0]),
                          n_x_parts=len(x_parts), emit_next=emit_next),
        grid=(n_rows // tm,),
        in_specs=in_specs,
        out_specs=out_specs,
        out_shape=out_shape,
        compiler_params=_cparams(("arbitrary",), 52),
        name="wout",
    )(*args)


def _ffn_kernel(*refs, emit_next):
    if emit_next:
        h_ref, w1_ref, w3_ref, w2_ref, x_ref, g_ref, mod_ref, gn_ref, modn_ref, o_ref, hn_ref, acc_ref = refs
    else:
        h_ref, w1_ref, w3_ref, w2_ref, x_ref, g_ref, mod_ref, o_ref, acc_ref = refs
    f = pl.program_id(1)

    @pl.when(f == 0)
    def _():
        acc_ref[...] = jnp.zeros(acc_ref.shape, F32)

    h = h_ref[...]
    u = _silu(_dot(h, w1_ref[...])) * _dot(h, w3_ref[...])
    acc_ref[...] += _dot(_bf(u), w2_ref[...])

    @pl.when(f == pl.num_programs(1) - 1)
    def _():
        m = mod_ref[0]
        g = g_ref[...]
        gn = gn_ref[...] if emit_next else None
        mn = modn_ref[0] if emit_next else None

        def chunk(rows):
            xn = x_ref[rows, :] + m[5:6] * _rms(acc_ref[rows, :], g)
            o_ref[rows, :] = xn
            if emit_next:
                hn_ref[rows, :] = (_rms(xn, gn) * (1.0 + mn[1:2]) + mn[0:1]).astype(hn_ref.dtype)

        _row_chunks(o_ref.shape[0], chunk)


def _ffn(hb, w1, w3, w2, xs, g3, modtab, lat_rows, g_next, modtab_next):
    n_rows, d = hb.shape
    dff = w1.shape[1]
    nb = modtab.shape[0] - 1
    tm, tf = ROW_TILE, FFN_TF
    mi = _mod_index(tm, lat_rows, nb)
    emit_next = g_next is not None
    nf = dff // tf

    def fs(i, f):
        return jnp.where(i % 2 == 1, nf - 1 - f, f)

    in_specs = [pl.BlockSpec((tm, d), lambda i, f: (i, 0)),
                pl.BlockSpec((d, tf), lambda i, f: (0, fs(i, f))),
                pl.BlockSpec((d, tf), lambda i, f: (0, fs(i, f))),
                pl.BlockSpec((tf, d), lambda i, f: (fs(i, f), 0)),
                pl.BlockSpec((tm, d), lambda i, f: (i, 0)),
                pl.BlockSpec((1, d), lambda i, f: (0, 0)),
                pl.BlockSpec((1, 6, d), lambda i, f: (mi(i), 0, 0))]
    args = [hb, w1, w3, w2, xs, g3.reshape(1, d), modtab]
    out_specs = [pl.BlockSpec((tm, d), lambda i, f: (i, 0))]
    out_shape = [jax.ShapeDtypeStruct((n_rows, d), F32)]
    if emit_next:
        in_specs += [pl.BlockSpec((1, d), lambda i, f: (0, 0)),
                     pl.BlockSpec((1, 6, d), lambda i, f: (mi(i), 0, 0))]
        args += [g_next.reshape(1, d), modtab_next]
        out_specs.append(pl.BlockSpec((tm, d), lambda i, f: (i, 0)))
        out_shape.append(jax.ShapeDtypeStruct((n_rows, d), BF16))
    return pl.pallas_call(
        functools.partial(_ffn_kernel, emit_next=emit_next),
        grid=(n_rows // tm, dff // tf),
        in_specs=in_specs,
        out_specs=out_specs,
        out_shape=out_shape,
        scratch_shapes=[pltpu.VMEM((tm, d), F32)],
        compiler_params=_cparams(("arbitrary", "arbitrary"), 48),
        name="ffn",
    )(*args)


_META_E0, _META_E1, _META_R0, _META_R1, _META_G0, _META_G1 = range(6)


def _router_kernel(x_ref, g_ref, mod_ref, wr_ref, tri_ref, meta_ref, cnt_ref, carry_ref):
    i = pl.program_id(0)

    @pl.when(i == 0)
    def _():
        carry_ref[...] = jnp.zeros(carry_ref.shape, F32)

    m = mod_ref[0]
    hmod = _rms(x_ref[...], g_ref[...]) * (1.0 + m[4:5]) + m[3:4]
    h_hi = _bf(hmod)
    h_lo = _bf(hmod - h_hi.astype(F32))
    w = wr_ref[...]
    w_hi = _bf(w)
    w_lo = _bf(w - w_hi.astype(F32))
    logits = _dot(h_hi, w_hi) + _dot(h_lo, w_hi) + _dot(h_hi, w_lo)
    lane = lax.broadcasted_iota(jnp.int32, logits.shape, 1)
    lanef = lane.astype(F32)
    logits = jnp.where(lane < N_EXPERTS, logits, -jnp.inf)
    v0 = jnp.max(logits, axis=1, keepdims=True)
    e0 = jnp.min(jnp.where(logits == v0, lanef, 1e9), axis=1, keepdims=True)
    rest = jnp.where(lanef == e0, -jnp.inf, logits)
    v1 = jnp.max(rest, axis=1, keepdims=True)
    e1 = jnp.min(jnp.where(rest == v1, lanef, 1e9), axis=1, keepdims=True)
    ex = jnp.exp(v1 - v0)
    g0 = 1.0 / (1.0 + ex)
    g1 = ex / (1.0 + ex)
    oh0 = lanef == e0
    oh1 = lanef == e1
    oh = jnp.where(jnp.logical_or(oh0, oh1), 1.0, 0.0)
    before = _dot(tri_ref[...], _bf(oh)) + carry_ref[0:1, :]
    r0 = jnp.sum(jnp.where(oh0, before, 0.0), axis=1, keepdims=True)
    r1 = jnp.sum(jnp.where(oh1, before, 0.0), axis=1, keepdims=True)
    carry_ref[0:1, :] = carry_ref[0:1, :] + jnp.sum(oh, axis=0, keepdims=True)
    meta = jnp.zeros(logits.shape, F32)
    for j, val in ((_META_E0, e0), (_META_E1, e1), (_META_R0, r0), (_META_R1, r1),
                   (_META_G0, g0), (_META_G1, g1)):
        meta = jnp.where(lane == j, val, meta)
    meta_ref[...] = meta
    cnt_ref[...] = carry_ref[...]


def _router(xs, n_rows, g2, modtab, w_router, lat_rows):
    d = xs.shape[1]
    nb = modtab.shape[0] - 1
    tm = ROW_TILE
    mi = _mod_index(tm, lat_rows, nb)
    wr = jnp.zeros((d, 128), F32).at[:, :N_EXPERTS].set(w_router)
    tri = jnp.asarray(np.tril(np.ones((tm, tm), np.float32), -1), BF16)
    return pl.pallas_call(
        _router_kernel,
        grid=(n_rows // tm,),
        in_specs=[pl.BlockSpec((tm, d), lambda i: (i, 0)),
                  pl.BlockSpec((1, d), lambda i: (0, 0)),
                  pl.BlockSpec((1, 6, d), lambda i: (mi(i), 0, 0)),
                  pl.BlockSpec((d, 128), lambda i: (0, 0)),
                  pl.BlockSpec((tm, tm), lambda i: (0, 0))],
        out_specs=[pl.BlockSpec((tm, 128), lambda i: (i, 0)),
                   pl.BlockSpec((8, 128), lambda i: (0, 0))],
        out_shape=[jax.ShapeDtypeStruct((n_rows, 128), F32),
                   jax.ShapeDtypeStruct((8, 128), F32)],
        scratch_shapes=[pltpu.VMEM((8, 128), F32)],
        compiler_params=_cparams(("arbitrary",), 32),
        name="router",
    )(xs, g2.reshape(1, d), modtab, wr, tri)


def _dispatch_kernel(dest_ref, x_ref, g_ref, mod_ref, init_ref, o_ref, h_s, sem):
    del init_ref
    i = pl.program_id(0)
    tm = h_s.shape[0]
    m = mod_ref[0]
    h_s[...] = _rms(x_ref[...], g_ref[...]) * (1.0 + m[4:5]) + m[3:4]

    def row_copy(r, k):
        dst = dest_ref[TOP_K * (i * tm + r) + k]
        return pltpu.make_async_copy(h_s.at[pl.ds(r, 1)], o_ref.at[pl.ds(dst, 1)], sem)

    def start(r, carry):
        for k in range(TOP_K):
            row_copy(r, k).start(priority=k)
        return carry

    lax.fori_loop(0, tm, start, 0, unroll=DMA_UNROLL)

    def wait(r, carry):
        for k in range(TOP_K):
            row_copy(r, k).wait()
        return carry

    lax.fori_loop(0, tm, wait, 0, unroll=DMA_UNROLL)


def _dispatch(dest, xs, n_rows, g2, modtab, n_slots, lat_rows):
    d = xs.shape[1]
    nb = modtab.shape[0] - 1
    tm = ROW_TILE
    mi = _mod_index(tm, lat_rows, nb)
    init = jnp.zeros((n_slots, d), F32)
    grid_spec = pltpu.PrefetchScalarGridSpec(
        num_scalar_prefetch=1,
        grid=(n_rows // tm,),
        in_specs=[pl.BlockSpec((tm, d), lambda i, dst: (i, 0)),
                  pl.BlockSpec((1, d), lambda i, dst: (0, 0)),
                  pl.BlockSpec((1, 6, d), lambda i, dst: (mi(i), 0, 0)),
                  pl.BlockSpec(memory_space=pl.ANY)],
        out_specs=pl.BlockSpec(memory_space=pl.ANY),
        scratch_shapes=[pltpu.VMEM((tm, d), F32), pltpu.SemaphoreType.DMA(())],
    )
    return pl.pallas_call(
        _dispatch_kernel,
        grid_spec=grid_spec,
        out_shape=jax.ShapeDtypeStruct((n_slots, d), F32),
        input_output_aliases={4: 0},
        compiler_params=_cparams(("arbitrary",), 32),
        name="dispatch",
    )(dest, xs, g2.reshape(1, d), modtab, init)


def _expert_kernel(ge_ref, nt_ref, ng_ref, x_ref, *refs):
    w1_refs, w3_refs = refs[:MOE_WSPLIT], refs[MOE_WSPLIT:2 * MOE_WSPLIT]
    w2_ref, o_ref, xb_s = refs[2 * MOE_WSPLIT:]
    s = pl.program_id(0)
    f = pl.program_id(1)
    tm = MOE_TM

    @pl.when(s < ng_ref[0])
    def _():
        @pl.when(f == 0)
        def _():
            xb_s[...] = _bf(x_ref[...])
            o_ref[...] = jnp.zeros(o_ref.shape, F32)

        w1b = jnp.concatenate([_bf(r[0]) for r in w1_refs], axis=0)
        w3b = jnp.concatenate([_bf(r[0]) for r in w3_refs], axis=0)
        w2b = _bf(w2_ref[0])

        def tile(t):
            rows = pl.ds(t * tm, tm)
            h = xb_s[rows, :]
            u = _silu(_dot(h, w1b)) * _dot(h, w3b)
            o_ref[rows, :] += _dot(_bf(u), w2b)

        tile(0)
        for t in range(1, MOE_GROUP):
            pl.when(nt_ref[s] > t)(functools.partial(tile, t))

    @pl.when(jnp.logical_and(s >= ng_ref[0], f == pl.num_programs(1) - 1))
    def _():
        o_ref[...] = jnp.zeros(o_ref.shape, F32)


def _experts(group_e, group_tiles, n_groups, xsort, w1, w3, w2):
    n_slots, d = xsort.shape
    dff = w1.shape[2]
    tg, tf = MOE_TM * MOE_GROUP, MOE_TF
    nf = dff // tf

    def ss(s, ng):
        return jnp.maximum(jnp.minimum(s, ng[0] - 1), 0)

    def ff(s, f, ng):
        snake = jnp.where(s % 2 == 1, nf - 1 - f, f)
        last = jnp.where((ng[0] - 1) % 2 == 1, 0, nf - 1)
        return jnp.where(s < ng[0], snake, last)

    grid_spec = pltpu.PrefetchScalarGridSpec(
        num_scalar_prefetch=3,
        grid=(n_slots // tg, nf),
        in_specs=[pl.BlockSpec((tg, d), lambda s, f, ge, nt, ng: (ss(s, ng), 0)),
                  *[pl.BlockSpec((1, d // MOE_WSPLIT, tf),
                                 lambda s, f, ge, nt, ng, part=part: (ge[ss(s, ng)], part, ff(s, f, ng)))
                    for _ in range(2) for part in range(MOE_WSPLIT)],
                  pl.BlockSpec((1, tf, d), lambda s, f, ge, nt, ng: (ge[ss(s, ng)], ff(s, f, ng), 0))],
        out_specs=pl.BlockSpec((tg, d), lambda s, f, ge, nt, ng: (s, 0)),
        scratch_shapes=[pltpu.VMEM((tg, d), BF16)],
    )
    return pl.pallas_call(
        _expert_kernel,
        grid_spec=grid_spec,
        out_shape=jax.ShapeDtypeStruct((n_slots, d), F32),
        compiler_params=_cparams(("arbitrary", "arbitrary"), 56),
        name="experts",
    )(group_e, group_tiles, n_groups, xsort, *([w1] * MOE_WSPLIT), *([w3] * MOE_WSPLIT), w2)


def _combine_kernel(dest_ref, y_ref, meta_ref, x_ref, g_ref, mod_ref, o_ref, buf_s, sem):
    i = pl.program_id(0)
    tm = x_ref.shape[0]

    def row_copy(r, k):
        src = dest_ref[TOP_K * (i * tm + r) + k]
        return pltpu.make_async_copy(y_ref.at[pl.ds(src, 1)], buf_s.at[k, pl.ds(r, 1)], sem)

    def start(r, carry):
        for k in range(TOP_K):
            row_copy(r, k).start(priority=k)
        return carry

    lax.fori_loop(0, tm, start, 0, unroll=DMA_UNROLL)

    def wait(r, carry):
        for k in range(TOP_K):
            row_copy(r, k).wait()
        return carry

    lax.fori_loop(0, tm, wait, 0, unroll=DMA_UNROLL)

    m = mod_ref[0]
    g = g_ref[...]
    def chunk(rows):
        meta = meta_ref[rows, :]
        lane = lax.broadcasted_iota(jnp.int32, meta.shape, 1)
        g0 = jnp.sum(jnp.where(lane == _META_G0, meta, 0.0), axis=1, keepdims=True)
        g1 = jnp.sum(jnp.where(lane == _META_G1, meta, 0.0), axis=1, keepdims=True)
        y = buf_s[0, rows, :] * g0 + buf_s[1, rows, :] * g1
        o_ref[rows, :] = x_ref[rows, :] + m[5:6] * _rms(y, g)

    _row_chunks(tm, chunk)


def _combine(dest, yb, meta, xs, n_rows, g3, modtab, lat_rows):
    d = xs.shape[1]
    nb = modtab.shape[0] - 1
    tm = ROW_TILE
    mi = _mod_index(tm, lat_rows, nb)
    grid_spec = pltpu.PrefetchScalarGridSpec(
        num_scalar_prefetch=1,
        grid=(n_rows // tm,),
        in_specs=[pl.BlockSpec(memory_space=pl.ANY),
                  pl.BlockSpec((tm, 128), lambda i, dst: (i, 0)),
                  pl.BlockSpec((tm, d), lambda i, dst: (i, 0)),
                  pl.BlockSpec((1, d), lambda i, dst: (0, 0)),
                  pl.BlockSpec((1, 6, d), lambda i, dst: (mi(i), 0, 0))],
        out_specs=pl.BlockSpec((tm, d), lambda i, dst: (i, 0)),
        scratch_shapes=[pltpu.VMEM((TOP_K, tm, d), F32), pltpu.SemaphoreType.DMA(())],
    )
    return pl.pallas_call(
        _combine_kernel,
        grid_spec=grid_spec,
        out_shape=jax.ShapeDtypeStruct((n_rows, d), F32),
        compiler_params=_cparams(("arbitrary",), 40),
        name="combine",
    )(dest, yb, meta, xs, g3.reshape(1, d), modtab)


def _moe(xs, n_rows, g2, g3, modtab, w_router, w1, w3, w2, lat_rows):
    meta, cnt = _router(xs, n_rows, g2, modtab, w_router, lat_rows)
    counts = cnt[0, :N_EXPERTS].astype(jnp.int32)
    tg = MOE_TM * MOE_GROUP
    padded = (counts + tg - 1) // tg * tg
    pad_end = jnp.cumsum(padded)
    pad_start = pad_end - padded
    e = meta[:, _META_E0:_META_E1 + 1].astype(jnp.int32)
    r = meta[:, _META_R0:_META_R1 + 1].astype(jnp.int32)
    dest = (pad_start[e] + r).reshape(-1)
    n_groups_max = (n_rows * TOP_K) // tg + N_EXPERTS
    n_slots = n_groups_max * tg
    group_start = jnp.arange(n_groups_max, dtype=jnp.int32) * tg
    group_e = jnp.minimum(jnp.sum((group_start[:, None] >= pad_end[None, :]).astype(jnp.int32), axis=1),
                          N_EXPERTS - 1)
    filled = counts[group_e] - (group_start - pad_start[group_e])
    group_tiles = jnp.clip((filled + MOE_TM - 1) // MOE_TM, 1, MOE_GROUP).astype(jnp.int32)
    n_groups = (pad_end[-1:] // tg).astype(jnp.int32)
    xsort = _dispatch(dest, xs, n_rows, g2, modtab, n_slots, lat_rows)
    yb = _experts(group_e, group_tiles, n_groups, xsort, w1, w3, w2)
    return _combine(dest, yb, meta, xs, n_rows, g3, modtab, lat_rows)


def _rotary_tables(lat_len, ctx_len):
    rows = lat_len // GRID_W
    row = jnp.repeat(jnp.arange(rows, dtype=F32), GRID_W)
    col = jnp.tile(jnp.arange(GRID_W, dtype=F32), rows)
    n_freq = RET_DK // 4
    inv = ROPE_BASE ** (-jnp.arange(n_freq, dtype=F32) / n_freq)
    ang = jnp.concatenate([row[:, None] * inv, col[:, None] * inv], axis=-1)
    cos, sin = jnp.cos(ang), jnp.sin(ang)
    cosf = jnp.concatenate([jnp.ones((ctx_len, RET_DK), F32), jnp.concatenate([cos, cos], -1)], 0)
    sinf = jnp.concatenate([jnp.zeros((ctx_len, RET_DK), F32), jnp.concatenate([-sin, sin], -1)], 0)
    return cosf, sinf


def kernel(x, c, ctx, c_ctx, w_ada, b_ada, norm_g, w_in, w_out, ret_decay, ret_gn, hgrn_lb, hgrn_gn,
           mlstm_conv, mlstm_gate_b, mlstm_gn, w_ffn1, w_ffn3, w_ffn2, w_router, w_exp1, w_exp3, w_exp2):
    nb, lat_len, d = x.shape
    ctx_len = ctx.shape[1]
    depth = w_ada.shape[0]
    n_lat_rows = nb * lat_len
    n_rows = n_lat_rows + nb * ctx_len

    x_parts = [x.reshape(n_lat_rows, d), ctx.reshape(nb * ctx_len, d)]
    w_in_t = jnp.swapaxes(w_in, 1, 2)
    s_in = jnp.zeros((8, d), F32).at[:nb].set(c).at[nb].set(c_ctx)
    mod_all = _ada(s_in, w_ada, b_ada)[:, :nb + 1].reshape(depth, nb + 1, 6, d)

    cosf, sinf = _rotary_tables(lat_len, ctx_len)
    hg_cum, hg_masks, hg_signs = _hgrn_constants()
    hg_consts = (jnp.asarray(hg_cum, BF16), jnp.asarray(hg_masks, F32), jnp.asarray(hg_signs, F32))
    sm = jax.nn.softmax(hgrn_lb.astype(F32), axis=0)
    lb_all = jnp.clip(jnp.cumsum(sm, axis=0) - sm[0], 0.0, 1.0)

    hb = None
    for layer in range(depth):
        last = layer == depth - 1
        modtab = mod_all[layer]
        g = norm_g[layer]

        if hb is None:
            hb = _prenorm(x_parts, n_rows, g[0], modtab, 1, 0, lat_len, BF16)
        p = _mm(hb, w_in_t, layer, MM_TM, MM_TN, n_cols=PROJ_MAIN)
        w_g = jnp.zeros((1, 128, d), F32).at[0, :MLSTM_GATES].set(w_in_t[layer, PROJ_MAIN:])
        gates = _mm(hb, w_g, 0, MM_TM, 128)[:, :MLSTM_GATES]
        hb = None
        g_row_lat = jnp.swapaxes(gates[:n_lat_rows].reshape(nb, lat_len, MLSTM_GATES), 1, 2)
        g_row_ctx = jnp.swapaxes(gates[n_lat_rows:].reshape(nb, ctx_len, MLSTM_GATES), 1, 2)

        log_g = jax.nn.log_sigmoid(ret_decay[layer].astype(F32))
        lg_b = jnp.broadcast_to(log_g[:, :, None, None], (2, RET_HEADS, 8, RET_DV))
        lb = lb_all[layer]
        emit_ctx = not last
        o_ret = _retention(p, nb, lat_len, ctx_len, cosf, sinf, lg_b, ret_gn[layer], emit_ctx)
        o_hg = _hgrn(p, nb, lat_len, ctx_len, jnp.log(lb), jnp.log1p(-lb), hgrn_gn[layer],
                     hg_consts, emit_ctx)
        o_ml = _mlstm(p, gates, g_row_lat, g_row_ctx, nb, lat_len, ctx_len, mlstm_gate_b[layer],
                      mlstm_conv[layer], mlstm_gn[layer], emit_ctx)
        mix_parts = [list(o) for o in (o_ret, o_hg, o_ml)]
        rows_now = n_lat_rows if last else n_rows
        dense = layer % 2 == 0
        res = _wout(mix_parts, _bf(w_out[layer]), x_parts, rows_now, g[1], modtab, lat_len,
                    g[2] if dense else None)
        xs = res[0]
        x_parts = [xs]

        j = layer // 2
        if dense:
            nxt = (None, None) if last else (norm_g[layer + 1][0], mod_all[layer + 1])
            res = _ffn(res[1], _bf(w_ffn1[j]), _bf(w_ffn3[j]), _bf(w_ffn2[j]), xs, g[3], modtab, lat_len, *nxt)
            xs = res[0]
            hb = None if last else res[1]
        else:
            xs = _moe(xs, rows_now, g[2], g[3], modtab, w_router[j], w_exp1[j], w_exp3[j], w_exp2[j], lat_len)
        x_parts = [xs]
    return xs[:n_lat_rows].reshape(nb, lat_len, d)
```

```python
import functools

import numpy as np
import jax
import jax.numpy as jnp
from jax import lax
from jax.experimental import pallas as pl
from jax.experimental.pallas import tpu as pltpu

F32 = jnp.float32
BF16 = jnp.bfloat16

CHUNK = 128
NORM_EPS = 1e-6
ROPE_BASE = 10000.0
GRID_W = 64

RET_HEADS, RET_DK, RET_DV = 4, 128, 256
HGRN_HEADS, HGRN_DK, HGRN_DV = 4, 128, 128
MLSTM_HEADS, MLSTM_DK, MLSTM_DV = 4, 128, 128
N_EXPERTS = 8
TOP_K = 2

RET_QK = RET_HEADS * RET_DK
RET_WIDTH = RET_HEADS * RET_DV
HGRN_QK = HGRN_HEADS * HGRN_DK
HGRN_WIDTH = HGRN_HEADS * HGRN_DV
MLSTM_QK = MLSTM_HEADS * MLSTM_DK
MLSTM_WIDTH = MLSTM_HEADS * MLSTM_DV
MLSTM_GATES = 4 * MLSTM_HEADS

_C_RET_Q = 0
_C_RET_K = _C_RET_Q + RET_QK // 128
_C_RET_V = _C_RET_K + RET_QK // 128
_C_RET_G = _C_RET_V + RET_WIDTH // 128
_C_HG_Q = _C_RET_G + RET_WIDTH // 128
_C_HG_FF = _C_HG_Q + HGRN_QK // 128
_C_HG_FB = _C_HG_FF + HGRN_QK // 128
_C_HG_I = _C_HG_FB + HGRN_QK // 128
_C_HG_G = _C_HG_I + HGRN_WIDTH // 128
_C_ML_Q = _C_HG_G + HGRN_WIDTH // 128
_C_ML_K = _C_ML_Q + MLSTM_QK // 128
_C_ML_V = _C_ML_K + MLSTM_QK // 128
_C_ML_O = _C_ML_V + MLSTM_WIDTH // 128
PROJ_MAIN = (_C_ML_O + MLSTM_WIDTH // 128) * 128

_VMEM_CAP_BYTES = 56 * 1024 * 1024

ROW_TILE = 512
MM_TM, MM_TN = 1024, 1280
FFN_TF = 512
MOE_TM = 768
MOE_WSPLIT = 1
MOE_GROUP = 1
MOE_TF = 512
RET_UNROLL = (4, 4)
HGRN_UNROLL = (4, 4)
MLSTM_UNROLL = (2, 4)
DMA_UNROLL = 8
EPI_ROWS = None


def _cparams(sem, vmem_mb):
    return pltpu.CompilerParams(dimension_semantics=sem,
                                vmem_limit_bytes=min(int(vmem_mb * 1024 * 1024), _VMEM_CAP_BYTES))


def _bf(x):
    return x.astype(BF16)


def _dot(a, b):
    return jnp.dot(a, b, preferred_element_type=F32)


def _dot_nt(a, b):
    return lax.dot_general(a, b, (((1,), (1,)), ((), ())), preferred_element_type=F32)


def _dot_tn(a, b):
    return lax.dot_general(a, b, (((0,), (0,)), ((), ())), preferred_element_type=F32)


def _sigmoid(x):
    return 1.0 / (1.0 + jnp.exp(-x))


def _silu(x):
    return x * _sigmoid(x)


def _log_sigmoid(x):
    return jnp.minimum(x, 0.0) - jnp.log(1.0 + jnp.exp(-jnp.abs(x)))


def _rms(x, g):
    return x * lax.rsqrt(jnp.mean(x * x, axis=-1, keepdims=True) + NORM_EPS) * g


def _mod_index(tile_rows, n_lat_rows_per_batch, n_batch):
    return lambda i: jnp.minimum((i * tile_rows) // n_lat_rows_per_batch, n_batch)


def _ada_kernel(s_ref, w_ref, b_ref, o_ref):
    s = _bf(_silu(s_ref[...]))
    o_ref[0] = _dot(s, _bf(w_ref[0])) + b_ref[0]


def _ada(s_in, w_ada, b_ada):
    depth, d, n = w_ada.shape
    tn = 1024
    return pl.pallas_call(
        _ada_kernel,
        grid=(depth, n // tn),
        in_specs=[pl.BlockSpec((8, d), lambda l, j: (0, 0)),
                  pl.BlockSpec((1, d, tn), lambda l, j: (l, 0, j)),
                  pl.BlockSpec((1, 1, tn), lambda l, j: (l, 0, j))],
        out_specs=pl.BlockSpec((1, 8, tn), lambda l, j: (l, 0, j)),
        out_shape=jax.ShapeDtypeStruct((depth, 8, n), F32),
        compiler_params=_cparams(("arbitrary", "arbitrary"), 40),
        name="ada",
    )(s_in, w_ada, b_ada.reshape(depth, 1, n))


def _row_chunks(n_rows, fn):
    if EPI_ROWS is None:
        fn(pl.ds(0, n_rows))
        return

    def body(i, carry):
        fn(pl.ds(pl.multiple_of(i * EPI_ROWS, EPI_ROWS), EPI_ROWS))
        return carry
    lax.fori_loop(0, n_rows // EPI_ROWS, body, 0, unroll=2)


def _part_specs(parts, tm, width):
    if len(parts) == 1:
        return [pl.BlockSpec((tm, width), lambda i: (i, 0))]
    n0 = parts[0].shape[0] // tm
    return [pl.BlockSpec((tm, width), lambda i: (jnp.minimum(i, n0 - 1), 0)),
            pl.BlockSpec((tm, width), lambda i: (jnp.maximum(i - n0, 0), 0))]


def _on_part(n_first, n_parts, fn):
    if n_parts == 1:
        fn(0)
    else:
        i = pl.program_id(0)
        pl.when(i < n_first)(lambda: fn(0))
        pl.when(i >= n_first)(lambda: fn(1))


def _prenorm_kernel(*refs, sc, sh, n_first):
    x_refs, (g_ref, mod_ref, o_ref) = refs[:-3], refs[-3:]

    def run(part):
        m = mod_ref[0]
        g = g_ref[...]

        def chunk(rows):
            y = _rms(x_refs[part][rows, :], g)
            o_ref[rows, :] = (y * (1.0 + m[sc:sc + 1]) + m[sh:sh + 1]).astype(o_ref.dtype)

        _row_chunks(o_ref.shape[0], chunk)

    _on_part(n_first, len(x_refs), run)


def _prenorm(x_parts, n_rows, g, modtab, sc, sh, lat_rows, out_dtype):
    d = x_parts[0].shape[1]
    nb = modtab.shape[0] - 1
    mi = _mod_index(ROW_TILE, lat_rows, nb)
    return pl.pallas_call(
        functools.partial(_prenorm_kernel, sc=sc, sh=sh, n_first=x_parts[0].shape[0] // ROW_TILE),
        grid=(n_rows // ROW_TILE,),
        in_specs=[*_part_specs(x_parts, ROW_TILE, d),
                  pl.BlockSpec((1, d), lambda i: (0, 0)),
                  pl.BlockSpec((1, 6, d), lambda i: (mi(i), 0, 0))],
        out_specs=pl.BlockSpec((ROW_TILE, d), lambda i: (i, 0)),
        out_shape=jax.ShapeDtypeStruct((n_rows, d), out_dtype),
        compiler_params=_cparams(("arbitrary",), 32),
        name="prenorm",
    )(*x_parts, g.reshape(1, d), modtab)


def _mm_kernel(x_ref, wt_ref, o_ref, wb_s):
    @pl.when(pl.program_id(1) == 0)
    def _():
        wb_s[...] = _bf(wt_ref[0])

    o_ref[...] = _dot_nt(x_ref[...], wb_s[...]).astype(o_ref.dtype)


def _mm(x, wt, layer, tm, tn, n_cols=None, out_dtype=F32):
    m, k = x.shape
    n = wt.shape[1] if n_cols is None else n_cols
    return pl.pallas_call(
        _mm_kernel,
        grid=(n // tn, m // tm),
        in_specs=[pl.BlockSpec((tm, k), lambda j, i: (i, 0)),
                  pl.BlockSpec((1, tn, k), lambda j, i: (layer, j, 0))],
        out_specs=pl.BlockSpec((tm, tn), lambda j, i: (i, j)),
        out_shape=jax.ShapeDtypeStruct((m, n), out_dtype),
        scratch_shapes=[pltpu.VMEM((tn, k), BF16)],
        compiler_params=_cparams(("arbitrary", "arbitrary"), 48),
        name="proj",
    )(x, wt)


def _bwd_chunk(i, n_ctx, n_lat):
    return jnp.where(i < n_ctx, n_ctx - 1 - i, 2 * n_ctx + n_lat - 1 - i)


def _rows(c):
    return pl.ds(pl.multiple_of(c * CHUNK, CHUNK), CHUNK)


def _for_chunks(n_ctx, n_lat, fn, unroll, with_ctx=True):
    def run(part, n, off):
        def body(c, carry):
            fn(part, c, c + off)
            return carry
        lax.fori_loop(0, n, body, 0, unroll=max(u for u in (1, 2, unroll) if n % u == 0 and u <= unroll))
    if with_ctx:
        run(0, n_ctx, 0)
    run(1, n_lat, n_ctx)


def _lanes(d, w):
    return slice(d * w, (d + 1) * w)


def _mixer_call(kernel_fn, name, in_specs, args, nb, heads, lat_len, ctx_len, dv, scratch, emit_ctx, vmem_mb):
    width = heads * dv
    out_specs = [pl.BlockSpec((lat_len, dv), lambda b, h: (b, h))]
    out_shape = [jax.ShapeDtypeStruct((nb * lat_len, width), BF16)]
    if emit_ctx:
        out_specs.append(pl.BlockSpec((ctx_len, dv), lambda b, h: (b, h)))
        out_shape.append(jax.ShapeDtypeStruct((nb * ctx_len, width), BF16))
    n_in = len(in_specs)

    def body(*refs):
        ins, rest = refs[:n_in], refs[n_in:]
        if emit_ctx:
            ol, oc, scr = rest[0], rest[1], rest[2:]
        else:
            ol, oc, scr = rest[0], None, rest[1:]
        kernel_fn(*ins, ol, oc, *scr, n_ctx=ctx_len // CHUNK, n_lat=lat_len // CHUNK)

    return pl.pallas_call(
        body,
        grid=(nb, heads),
        in_specs=in_specs,
        out_specs=out_specs,
        out_shape=out_shape,
        scratch_shapes=scratch,
        compiler_params=_cparams(("arbitrary", "arbitrary"), vmem_mb),
        name=name,
    )(*args)


def _seq_specs(col0, width, nb, lat_len, ctx_len):
    n_lat_blk = nb * lat_len // ctx_len
    c0 = col0 * 128 // width
    return (pl.BlockSpec((lat_len, width), lambda b, h: (b, c0 + h)),
            pl.BlockSpec((ctx_len, width), lambda b, h: (n_lat_blk + b, c0 + h)))


def _ret_kernel(ql, qc, kl, kc, vl, vc, gl, gc, cos_ref, sin_ref, lg_ref, gn_ref, ol_ref, oc_ref,
                qs_s, kv_s, sall_s, o_s, st_s, dm_s, dq_s, dk_s, *, n_ctx, n_lat):
    nch = n_ctx + n_lat
    unroll_a, unroll_c = RET_UNROLL
    qr, kr, vr, gr, outr = (qc, ql), (kc, kl), (vc, vl), (gc, gl), (oc_ref, ol_ref)
    t_i = lax.broadcasted_iota(jnp.int32, (CHUNK, CHUNK), 0).astype(F32)
    s_i = lax.broadcasted_iota(jnp.int32, (CHUNK, CHUNK), 1).astype(F32)
    dm = None
    for d in range(2):
        lg = lg_ref[d, 0][0:1, 0:CHUNK]
        rel = (t_i - s_i) if d == 0 else (s_i - t_i)
        dmd = jnp.where(rel >= 0, jnp.exp(jnp.maximum(rel, 0.0) * lg), 0.0)
        dm = dmd if dm is None else dm + dmd
        p = t_i if d == 0 else (CHUNK - 1.0) - t_i
        dq_s[d] = jnp.exp((p + 1.0) * lg)
        dk_s[d] = jnp.exp((CHUNK - 1.0 - p) * lg)
    dm_s[...] = dm
    scale = RET_DK ** -0.5

    def phase_a(part, cl, cg):
        rl, rg = _rows(cl), _rows(cg)
        cs = cos_ref[rg, :]
        sn = sin_ref[rg, :]
        q = qr[part][rl, :]
        k = kr[part][rl, :]
        q = q * cs + pltpu.roll(q, RET_DK // 2, 1) * sn
        k = (k * cs + pltpu.roll(k, RET_DK // 2, 1) * sn) * scale
        vb = _bf(vr[part][rl, :])
        s = _dot_nt(_bf(q), _bf(k))
        o_s[rg, :] = _dot(_bf(s * dm_s[...]), vb)
        qs_s[rg, :] = jnp.concatenate([_bf(q * dq_s[0]), _bf(q * dq_s[1])], axis=1)
        kd = jnp.concatenate([_bf(k * dk_s[0]), _bf(k * dk_s[1])], axis=1)
        kv_s[cg] = _dot_tn(kd, vb)

    _for_chunks(n_ctx, n_lat, phase_a, unroll_a)

    st_s[...] = jnp.zeros(st_s.shape, F32)
    dec = [jnp.exp(float(CHUNK) * lg_ref[d, 0][0:1, :]) for d in range(2)]

    def phase_b(i, carry):
        for d, c in ((0, i), (1, _bwd_chunk(i, n_ctx, n_lat))):
            st = st_s[d]
            sall_s[c, _lanes(d, RET_DK), :] = _bf(st)
            st_s[d] = dec[d] * st + kv_s[c, _lanes(d, RET_DK), :]
        return carry

    lax.fori_loop(0, nch, phase_b, 0)
    gn = gn_ref[...]

    def phase_c(part, cl, cg):
        rl, rg = _rows(cl), _rows(cg)
        o = o_s[rg, :] + _dot(qs_s[rg, :], sall_s[cg])
        outr[part][rl, :] = (_rms(o, gn) * _silu(gr[part][rl, :])).astype(BF16)

    _for_chunks(n_ctx, n_lat, phase_c, unroll_c, with_ctx=oc_ref is not None)


def _retention(p, nb, lat_len, ctx_len, cosf, sinf, lg_b, gn, emit_ctx):
    t_len = lat_len + ctx_len
    nch = t_len // CHUNK
    s128 = functools.partial(_seq_specs, width=128, nb=nb, lat_len=lat_len, ctx_len=ctx_len)
    s256 = functools.partial(_seq_specs, width=256, nb=nb, lat_len=lat_len, ctx_len=ctx_len)
    in_specs = [*s128(_C_RET_Q), *s128(_C_RET_K), *s256(_C_RET_V), *s256(_C_RET_G),
                pl.BlockSpec((t_len, 128), lambda b, h: (0, 0)),
                pl.BlockSpec((t_len, 128), lambda b, h: (0, 0)),
                pl.BlockSpec((2, 1, 8, RET_DV), lambda b, h: (0, h, 0, 0)),
                pl.BlockSpec((1, RET_DV), lambda b, h: (0, h))]
    scratch = [pltpu.VMEM((t_len, 2 * RET_DK), BF16),
               pltpu.VMEM((nch, 2 * RET_DK, RET_DV), F32),
               pltpu.VMEM((nch, 2 * RET_DK, RET_DV), BF16),
               pltpu.VMEM((t_len, RET_DV), F32),
               pltpu.VMEM((2, RET_DK, RET_DV), F32),
               pltpu.VMEM((CHUNK, CHUNK), F32),
               pltpu.VMEM((2, CHUNK, CHUNK), F32), pltpu.VMEM((2, CHUNK, CHUNK), F32)]
    args = (p, p, p, p, p, p, p, p, cosf, sinf, lg_b, gn.reshape(1, RET_WIDTH))
    return _mixer_call(_ret_kernel, "retention", in_specs, args, nb, RET_HEADS, lat_len, ctx_len, RET_DV,
                       scratch, emit_ctx, 48)


_HG_LEVELS = (64, 32, 16, 8, 4, 2, 1)


def _hgrn_constants():
    c = CHUNK
    t = np.arange(c)[:, None]
    u = np.arange(c)[None, :]
    cum = (u <= t).astype(np.float32)
    masks, signs = [], []
    for m in _HG_LEVELS:
        base = (t // (2 * m)) * (2 * m)
        lower = t >= base + m
        tb = t // (2 * m)
        sb = u // (2 * m)
        masks.append((tb == sb) & lower & (u < (sb * 2 * m + m)))
        signs.append(np.broadcast_to(np.where(lower, 1.0, -1.0), (c, c)))
    masks.append(t == u)
    kf = np.stack([x.astype(np.float32) for x in masks], axis=0)
    kb = np.stack([x.astype(np.float32)[::-1, ::-1] for x in masks], axis=0)
    sf = np.stack([x.astype(np.float32) for x in signs], axis=0)
    sb_ = np.stack([x.astype(np.float32)[::-1, ::-1] for x in signs], axis=0)
    return (np.stack([cum, cum[::-1, ::-1]], 0), np.stack([kf, kb], 0),
            np.stack([sf[:_HG_WIDE], sb_[:_HG_WIDE]], 0))


_HG_WIDE = 5


def _hgrn_level_exponents(b, lf, d, sgn_ref, row):
    out = []
    for l, m in enumerate(_HG_LEVELS[:_HG_WIDE]):
        pieces = []
        for j in range(CHUNK // (2 * m)):
            r = 2 * m * j + (m - 1 if d == 0 else m)
            pieces.append(jnp.broadcast_to(b[r:r + 1, :], (2 * m, CHUNK)))
        bref = pieces[0] if len(pieces) == 1 else jnp.concatenate(pieces, axis=0)
        out.append((b - bref) * sgn_ref[d, l])
    up = pltpu.roll(lf, CHUNK - 1, 0)
    dn = pltpu.roll(lf, 1, 0)
    r4 = row % 4
    if d == 0:
        e2 = jnp.where(r4 == 0, up, jnp.where(r4 == 1, 0.0, jnp.where(r4 == 2, lf, lf + dn)))
        e1 = jnp.where(row % 2 == 1, lf, 0.0)
    else:
        e2 = jnp.where(r4 == 0, lf + up, jnp.where(r4 == 1, lf, jnp.where(r4 == 2, 0.0, dn)))
        e1 = jnp.where(row % 2 == 0, lf, 0.0)
    return out + [e2, e1]


def _hgrn_kernel(ql, qc, ffl, ffc, fbl, fbc, il, ic, gl, gc, llb_ref, lub_ref, gn_ref, c_ref, k_ref, sgn_ref,
                 ol_ref, oc_ref, qs_s, kv_s, sall_s, dec_s, o_s, st_s, *, n_ctx, n_lat):
    nch = n_ctx + n_lat
    unroll_a, unroll_c = HGRN_UNROLL
    qr, fr, ir, gr, outr =(qc, ql), ((ffc, ffl), (fbc, fbl)), (ic, il), (gc, gl), (oc_ref, ol_ref)
    llb = llb_ref[...]
    lub = lub_ref[...]
    nlev = len(_HG_LEVELS)
    row = lax.broadcasted_iota(jnp.int32, (CHUNK, HGRN_DK), 0)

    def phase_a(part, cl, cg):
        rl, rg = _rows(cl), _rows(cg)
        q = _silu(qr[part][rl, :])
        qb = _bf(q)
        vb = _bf(ir[part][rl, :])
        a_sum, qs, kds = None, [], []
        for d in range(2):
            lsg = lub + _log_sigmoid(fr[d][part][rl, :])
            lf = jnp.maximum(llb, lsg) + jnp.log(1.0 + jnp.exp(-jnp.abs(llb - lsg)))
            k = 1.0 - jnp.exp(lf)
            lf_hi = _bf(lf)
            lf_lo = _bf(lf - lf_hi.astype(F32))
            b2 = _dot(c_ref[d], jnp.concatenate([lf_hi, lf_lo], axis=1))
            b = b2[:, 0:HGRN_DK] + b2[:, HGRN_DK:]
            es = _hgrn_level_exponents(b, lf, d, sgn_ref, row)
            kb = _bf(k)
            a = k_ref[d, nlev] * _dot_nt(qb, kb)
            for l in range(nlev):
                xb = _bf(jnp.exp(es[l]))
                a = a + k_ref[d, l] * _dot_nt(qb * xb, kb * xb)
            a_sum = a if a_sum is None else a_sum + a
            b_last = b[CHUNK - 1:CHUNK] if d == 0 else b[0:1]
            qs.append(_bf(q * jnp.exp(b)))
            kds.append(_bf(k * jnp.exp(b_last - b)))
            dec_s[cg, :, _lanes(d, HGRN_DK)] = jnp.exp(b_last)
        o_s[rg, :] = _dot(_bf(a_sum), vb)
        qs_s[rg, :] = jnp.concatenate(qs, axis=1)
        kv_s[cg] = _dot_tn(vb, jnp.concatenate(kds, axis=1))

    _for_chunks(n_ctx, n_lat, phase_a, unroll_a)
    st_s[...] = jnp.zeros(st_s.shape, F32)

    def phase_b(i, carry):
        for d, c in ((0, i), (1, _bwd_chunk(i, n_ctx, n_lat))):
            ln = _lanes(d, HGRN_DK)
            st = st_s[d]
            sall_s[c, :, ln] = _bf(st)
            st_s[d] = dec_s[c, :, ln] * st + kv_s[c, :, ln]
        return carry

    lax.fori_loop(0, nch, phase_b, 0)
    gn = gn_ref[...]

    def phase_c(part, cl, cg):
        rl, rg = _rows(cl), _rows(cg)
        o = o_s[rg, :] + _dot_nt(qs_s[rg, :], sall_s[cg])
        outr[part][rl, :] = (_rms(o, gn) * _silu(gr[part][rl, :])).astype(BF16)

    _for_chunks(n_ctx, n_lat, phase_c, unroll_c, with_ctx=oc_ref is not None)


def _hgrn(p, nb, lat_len, ctx_len, log_lb, log_ub, gn, consts, emit_ctx):
    mats, masks, signs = consts
    t_len = lat_len + ctx_len
    nch = t_len // CHUNK
    spec = functools.partial(_seq_specs, width=128, nb=nb, lat_len=lat_len, ctx_len=ctx_len)
    vec = pl.BlockSpec((1, 128), lambda b, h: (0, h))
    in_specs = [*spec(_C_HG_Q), *spec(_C_HG_FF), *spec(_C_HG_FB), *spec(_C_HG_I), *spec(_C_HG_G),
                vec, vec, vec,
                pl.BlockSpec(mats.shape, lambda b, h: (0, 0, 0)),
                pl.BlockSpec(masks.shape, lambda b, h: (0, 0, 0, 0)),
                pl.BlockSpec(signs.shape, lambda b, h: (0, 0, 0, 0))]
    scratch = [pltpu.VMEM((t_len, 2 * HGRN_DK), BF16),
               pltpu.VMEM((nch, HGRN_DV, 2 * HGRN_DK), F32),
               pltpu.VMEM((nch, HGRN_DV, 2 * HGRN_DK), BF16),
               pltpu.VMEM((nch, 1, 2 * HGRN_DK), F32),
               pltpu.VMEM((t_len, HGRN_DV), F32),
               pltpu.VMEM((2, HGRN_DV, HGRN_DK), F32)]
    args = (p, p, p, p, p, p, p, p, p, p, log_lb.reshape(1, HGRN_QK), log_ub.reshape(1, HGRN_QK),
            gn.reshape(1, HGRN_WIDTH), mats, masks, signs)
    return _mixer_call(_hgrn_kernel, "hgrn2", in_specs, args, nb, HGRN_HEADS, lat_len, ctx_len, HGRN_DV,
                       scratch, emit_ctx, 48)


def _mlstm_kernel(ql, qc, kl, kc, vl, vc, ogl, ogc, gcl, gcc, grl, grc, bc_ref, br_ref,
                  wq_ref, wk_ref, gn_ref, ol_ref, oc_ref,
                  q_s, gr_s, intra_s, kv_s, call_s, rho_s, bcol_s, bl_s, mu_s, mprev_s, st_s, m_s,
                  *, n_ctx, n_lat):
    nch = n_ctx + n_lat
    unroll_a, unroll_c = MLSTM_UNROLL
    h = pl.program_id(1)
    nh = MLSTM_HEADS
    dk, dv = MLSTM_DK, MLSTM_DV
    ext = 2 * dv
    qr, kr, vr, ogr, gcr, outr = (qc, ql), (kc, kl), (vc, vl), (ogc, ogl), (gcc, gcl), (oc_ref, ol_ref)
    n_loc = (n_ctx, n_lat)

    for cc in range(nch):
        src, c0 = (grc, cc) if cc < n_ctx else (grl, cc - n_ctx)
        gr_s[cc] = src[0, :, c0 * CHUNK:(c0 + 1) * CHUNK] + br_ref[...]

    row = lax.broadcasted_iota(jnp.int32, (CHUNK, dk), 0)
    lane = lax.broadcasted_iota(jnp.int32, (CHUNK, dv), 1)
    del lane
    ones_col = jnp.ones((CHUNK, dv), BF16)
    scale = dk ** -0.5
    t_i = lax.broadcasted_iota(jnp.int32, (CHUNK, CHUNK), 0)
    s_i = lax.broadcasted_iota(jnp.int32, (CHUNK, CHUNK), 1)
    lane16 = lax.broadcasted_iota(jnp.int32, (CHUNK, MLSTM_GATES), 1)
    sub16 = lax.broadcasted_iota(jnp.int32, (MLSTM_GATES, CHUNK), 0)

    def conv(src, w_ref, part, cl):
        r0 = pl.multiple_of(cl * CHUNK, CHUNK)
        n_rows = n_loc[part] * CHUNK
        x = src[pl.ds(r0, CHUNK), :]
        pr = src[pl.ds(jnp.maximum(r0 - 1, 0), 1), :]
        nx = src[pl.ds(jnp.minimum(r0 + CHUNK, n_rows - 1), 1), :]
        pr = jnp.where(cl != 0, pr, 0.0)
        nx = jnp.where(cl != n_loc[part] - 1, nx, 0.0)
        xp = jnp.where(row == 0, pr, pltpu.roll(x, 1, 0))
        xn = jnp.where(row == CHUNK - 1, nx, pltpu.roll(x, CHUNK - 1, 0))
        w = w_ref[...]
        return _silu(w[0:1] * xp + w[1:2] * x + w[2:3] * xn)

    def pick_col(g, j):
        return jnp.sum(jnp.where(lane16 == j, g, 0.0), axis=1, keepdims=True)

    def pick_row(g, j):
        return jnp.sum(jnp.where(sub16 == j, g, 0.0), axis=0, keepdims=True)

    def phase_a(part, cl, cg):
        rl, rg = _rows(cl), _rows(cg)
        q = conv(qr[part], wq_ref, part, cl)
        k = conv(kr[part], wk_ref, part, cl) * scale
        qb = _bf(q)
        q_s[rg, :] = qb
        vb = jnp.concatenate([_bf(vr[part][rl, :]), ones_col], axis=1)
        s = _dot_nt(qb, _bf(k))
        g_c = gcr[part][rl, :] + bc_ref[...]
        g_r = gr_s[cg]
        kws = []
        for d in range(2):
            ig_c = pick_col(g_c, d * nh + h)
            lf_c = _log_sigmoid(pick_col(g_c, 2 * nh + d * nh + h))
            ig_r = pick_row(g_r, d * nh + h)
            lf_r = _log_sigmoid(pick_row(g_r, 2 * nh + d * nh + h))
            tri = (s_i <= t_i) if d == 0 else (s_i >= t_i)
            tri_t = (t_i <= s_i) if d == 0 else (t_i >= s_i)
            b_c = jnp.sum(jnp.where(tri, lf_r, 0.0), axis=1, keepdims=True)
            b_r = jnp.sum(jnp.where(tri_t, lf_c, 0.0), axis=0, keepdims=True)
            dmat = jnp.where(tri, b_c - b_r + ig_r, -jnp.inf)
            rho = jnp.max(dmat, axis=1, keepdims=True)
            intra_s[d, rg, :] = _dot(_bf(s * jnp.exp(dmat - rho)), vb)
            b_last = b_c[CHUNK - 1:CHUNK] if d == 0 else b_c[0:1]
            mu = jnp.max(b_last - b_r + ig_r, axis=1, keepdims=True)
            kws.append(_bf(k * jnp.exp(b_last - b_c + ig_c - mu)))
            rho_s[d, rg, :] = jnp.broadcast_to(rho, (CHUNK, 128))
            bcol_s[d, rg, :] = jnp.broadcast_to(b_c, (CHUNK, 128))
            bl_s[cg, :, _lanes(d, 128)] = jnp.broadcast_to(b_last, (1, 128))
            mu_s[cg, :, _lanes(d, 128)] = jnp.broadcast_to(mu, (1, 128))
        kv_s[cg] = _dot_tn(jnp.concatenate(kws, axis=1), vb)

    _for_chunks(n_ctx, n_lat, phase_a, unroll_a)
    st_s[...] = jnp.zeros(st_s.shape, F32)
    m_s[...] = jnp.zeros(m_s.shape, F32)

    def wide(v):
        return jnp.concatenate([v, v], axis=1)

    def phase_b(i, carry):
        for d, c in ((0, i), (1, _bwd_chunk(i, n_ctx, n_lat))):
            ln = _lanes(d, 128)
            m_prev = m_s[d]
            st = st_s[d]
            mprev_s[c, :, ln] = m_prev
            call_s[c, :, _lanes(d, ext)] = _bf(st)
            bl = bl_s[c, :, ln]
            mu = mu_s[c, :, ln]
            m_new = jnp.maximum(bl + m_prev, mu)
            st_s[d] = (wide(jnp.exp(bl + m_prev - m_new)) * st
                       + wide(jnp.exp(mu - m_new)) * kv_s[c, _lanes(d, dk), :])
            m_s[d] = m_new
        return carry

    lax.fori_loop(0, nch, phase_b, 0)
    gn = gn_ref[...]

    def phase_c(part, cl, cg):
        rl, rg = _rows(cl), _rows(cg)
        qc_all = _dot(q_s[rg, :], call_s[cg])
        o = None
        for d in range(2):
            rho = rho_s[d, rg, :]
            b_c = bcol_s[d, rg, :]
            m_prev = mprev_s[cg, :, _lanes(d, 128)]
            m_t = jnp.maximum(rho, b_c + m_prev)
            nd = (wide(jnp.exp(rho - m_t)) * intra_s[d, rg, :]
                  + wide(jnp.exp(b_c + m_prev - m_t)) * qc_all[:, _lanes(d, ext)])
            hh = nd[:, 0:dv] / jnp.maximum(jnp.abs(nd[:, dv:]), jnp.exp(-m_t))
            o = hh if o is None else o + hh
        y = o - jnp.mean(o, axis=-1, keepdims=True)
        y = y * lax.rsqrt(jnp.mean(y * y, axis=-1, keepdims=True) + NORM_EPS)
        outr[part][rl, :] = (y * gn * _sigmoid(ogr[part][rl, :])).astype(BF16)

    _for_chunks(n_ctx, n_lat, phase_c, unroll_c, with_ctx=oc_ref is not None)


def _mlstm(p, g_col, g_row_lat, g_row_ctx, nb, lat_len, ctx_len, gate_b, conv_w, gn, emit_ctx):
    t_len = lat_len + ctx_len
    nch = t_len // CHUNK
    n_lat_blk = nb * lat_len // ctx_len
    ng = MLSTM_GATES
    dk, dv = MLSTM_DK, MLSTM_DV
    spec = functools.partial(_seq_specs, width=128, nb=nb, lat_len=lat_len, ctx_len=ctx_len)
    in_specs = [*spec(_C_ML_Q), *spec(_C_ML_K), *spec(_C_ML_V), *spec(_C_ML_O),
                pl.BlockSpec((lat_len, ng), lambda b, h: (b, 0)),
                pl.BlockSpec((ctx_len, ng), lambda b, h: (n_lat_blk + b, 0)),
                pl.BlockSpec((1, ng, lat_len), lambda b, h: (b, 0, 0)),
                pl.BlockSpec((1, ng, ctx_len), lambda b, h: (b, 0, 0)),
                pl.BlockSpec((1, ng), lambda b, h: (0, 0)),
                pl.BlockSpec((ng, 1), lambda b, h: (0, 0)),
                pl.BlockSpec((3, 128), lambda b, h: (0, h)),
                pl.BlockSpec((3, 128), lambda b, h: (0, MLSTM_HEADS + h)),
                pl.BlockSpec((1, 128), lambda b, h: (0, h))]
    scratch = [pltpu.VMEM((t_len, dk), BF16),
               pltpu.VMEM((nch, ng, CHUNK), F32),
               pltpu.VMEM((2, t_len, 2 * dv), F32),
               pltpu.VMEM((nch, 2 * dk, 2 * dv), F32),
               pltpu.VMEM((nch, dk, 4 * dv), BF16),
               pltpu.VMEM((2, t_len, 128), F32), pltpu.VMEM((2, t_len, 128), F32),
               pltpu.VMEM((nch, 1, 256), F32), pltpu.VMEM((nch, 1, 256), F32), pltpu.VMEM((nch, 1, 256), F32),
               pltpu.VMEM((2, dk, 2 * dv), F32), pltpu.VMEM((2, 1, 128), F32)]
    args = (p, p, p, p, p, p, p, p, g_col, g_col, g_row_lat, g_row_ctx,
            gate_b.reshape(1, ng), gate_b.reshape(ng, 1), conv_w, conv_w, gn.reshape(1, MLSTM_WIDTH))
    return _mixer_call(_mlstm_kernel, "mlstm", in_specs, args, nb, MLSTM_HEADS, lat_len, ctx_len, dv,
                       scratch, emit_ctx, 48)


def _wout_kernel(*refs, n_lat_tiles, n_mix_parts, n_x_parts, emit_next):
    it = iter(refs)
    mix = [[next(it) for _ in range(n_mix_parts)] for _ in range(3)]
    x_parts = [next(it) for _ in range(n_x_parts)]
    wr_ref, wh_ref, wm_ref, g_ref, mod_ref = (next(it) for _ in range(5))
    g2_ref = next(it) if emit_next else None
    o_ref = next(it)
    h2_ref = next(it) if emit_next else None

    def run(part):
        pm = min(part, n_mix_parts - 1)
        y = (_dot(mix[0][pm][...], wr_ref[...]) + _dot(mix[1][pm][...], wh_ref[...])
             + _dot(mix[2][pm][...], wm_ref[...]))
        o_ref[...] = y
        m = mod_ref[0]
        x_ref = x_parts[min(part, n_x_parts - 1)]
        g = g_ref[...]
        g2 = g2_ref[...] if emit_next else None

        def chunk(rows):
            xn = x_ref[rows, :] + m[2:3] * _rms(o_ref[rows, :], g)
            o_ref[rows, :] = xn
            if emit_next:
                h2_ref[rows, :] = (_rms(xn, g2) * (1.0 + m[4:5]) + m[3:4]).astype(h2_ref.dtype)

        _row_chunks(o_ref.shape[0], chunk)

    _on_part(n_lat_tiles, max(n_mix_parts, n_x_parts), run)


def _wout(mix_parts, w_out, x_parts, n_rows, g1, modtab, lat_rows, g2_next):
    d = x_parts[0].shape[1]
    nb = modtab.shape[0] - 1
    mi = _mod_index(ROW_TILE, lat_rows, nb)
    tm = ROW_TILE
    n_lat_tiles = mix_parts[0][0].shape[0] // tm
    emit_next = g2_next is not None
    widths = (RET_WIDTH, HGRN_WIDTH, MLSTM_WIDTH)
    w_r = w_out[0:RET_WIDTH]
    w_h = w_out[RET_WIDTH:RET_WIDTH + HGRN_WIDTH]
    w_m = w_out[RET_WIDTH + HGRN_WIDTH:]
    full = lambda a: pl.BlockSpec(a.shape, lambda i: (0, 0))
    in_specs, args = [], []
    for parts, w in zip(mix_parts, widths):
        in_specs += _part_specs(parts, tm, w)
        args += list(parts)
    in_specs += _part_specs(x_parts, tm, d)
    args += list(x_parts)
    in_specs += [full(w_r), full(w_h), full(w_m),
                 pl.BlockSpec((1, d), lambda i: (0, 0)),
                 pl.BlockSpec((1, 6, d), lambda i: (mi(i), 0, 0))]
    args += [w_r, w_h, w_m, g1.reshape(1, d), modtab]
    out_specs = [pl.BlockSpec((tm, d), lambda i: (i, 0))]
    out_shape = [jax.ShapeDtypeStruct((n_rows, d), F32)]
    if emit_next:
        in_specs.append(pl.BlockSpec((1, d), lambda i: (0, 0)))
        args.append(g2_next.reshape(1, d))
        out_specs.append(pl.BlockSpec((tm, d), lambda i: (i, 0)))
        out_shape.append(jax.ShapeDtypeStruct((n_rows, d), BF16))
    return pl.pallas_call(
        functools.partial(_wout_kernel, n_lat_tiles=n_lat_tiles, n_mix_parts=len(mix_parts[0]),
                          n_x_parts=len(x_parts), emit_next=emit_next),
        grid=(n_rows // tm,),
        in_specs=in_specs,
        out_specs=out_specs,
        out_shape=out_shape,
        compiler_params=_cparams(("arbitrary",), 52),
        name="wout",
    )(*args)


def _ffn_kernel(*refs, emit_next):
    if emit_next:
        h_ref, w1_ref, w3_ref, w2_ref, x_ref, g_ref, mod_ref, gn_ref, modn_ref, o_ref, hn_ref, acc_ref = refs
    else:
        h_ref, w1_ref, w3_ref, w2_ref, x_ref, g_ref, mod_ref, o_ref, acc_ref = refs
    f = pl.program_id(1)

    @pl.when(f == 0)
    def _():
        acc_ref[...] = jnp.zeros(acc_ref.shape, F32)

    h = h_ref[...]
    u = _silu(_dot(h, w1_ref[...])) * _dot(h, w3_ref[...])
    acc_ref[...] += _dot(_bf(u), w2_ref[...])

    @pl.when(f == pl.num_programs(1) - 1)
    def _():
        m = mod_ref[0]
        g = g_ref[...]
        gn = gn_ref[...] if emit_next else None
        mn = modn_ref[0] if emit_next else None

        def chunk(rows):
            xn = x_ref[rows, :] + m[5:6] * _rms(acc_ref[rows, :], g)
            o_ref[rows, :] = xn
            if emit_next:
                hn_ref[rows, :] = (_rms(xn, gn) * (1.0 + mn[1:2]) + mn[0:1]).astype(hn_ref.dtype)

        _row_chunks(o_ref.shape[0], chunk)


def _ffn(hb, w1, w3, w2, xs, g3, modtab, lat_rows, g_next, modtab_next):
    n_rows, d = hb.shape
    dff = w1.shape[1]
    nb = modtab.shape[0] - 1
    tm, tf = ROW_TILE, FFN_TF
    mi = _mod_index(tm, lat_rows, nb)
    emit_next = g_next is not None
    nf = dff // tf

    def fs(i, f):
        return jnp.where(i % 2 == 1, nf - 1 - f, f)

    in_specs = [pl.BlockSpec((tm, d), lambda i, f: (i, 0)),
                pl.BlockSpec((d, tf), lambda i, f: (0, fs(i, f))),
                pl.BlockSpec((d, tf), lambda i, f: (0, fs(i, f))),
                pl.BlockSpec((tf, d), lambda i, f: (fs(i, f), 0)),
                pl.BlockSpec((tm, d), lambda i, f: (i, 0)),
                pl.BlockSpec((1, d), lambda i, f: (0, 0)),
                pl.BlockSpec((1, 6, d), lambda i, f: (mi(i), 0, 0))]
    args = [hb, w1, w3, w2, xs, g3.reshape(1, d), modtab]
    out_specs = [pl.BlockSpec((tm, d), lambda i, f: (i, 0))]
    out_shape = [jax.ShapeDtypeStruct((n_rows, d), F32)]
    if emit_next:
        in_specs += [pl.BlockSpec((1, d), lambda i, f: (0, 0)),
                     pl.BlockSpec((1, 6, d), lambda i, f: (mi(i), 0, 0))]
        args += [g_next.reshape(1, d), modtab_next]
        out_specs.append(pl.BlockSpec((tm, d), lambda i, f: (i, 0)))
        out_shape.append(jax.ShapeDtypeStruct((n_rows, d), BF16))
    return pl.pallas_call(
        functools.partial(_ffn_kernel, emit_next=emit_next),
        grid=(n_rows // tm, dff // tf),
        in_specs=in_specs,
        out_specs=out_specs,
        out_shape=out_shape,
        scratch_shapes=[pltpu.VMEM((tm, d), F32)],
        compiler_params=_cparams(("arbitrary", "arbitrary"), 48),
        name="ffn",
    )(*args)


_META_E0, _META_E1, _META_R0, _META_R1, _META_G0, _META_G1 = range(6)


def _router_kernel(x_ref, g_ref, mod_ref, wr_ref, tri_ref, meta_ref, cnt_ref, carry_ref):
    i = pl.program_id(0)

    @pl.when(i == 0)
    def _():
        carry_ref[...] = jnp.zeros(carry_ref.shape, F32)

    m = mod_ref[0]
    hmod = _rms(x_ref[...], g_ref[...]) * (1.0 + m[4:5]) + m[3:4]
    h_hi = _bf(hmod)
    h_lo = _bf(hmod - h_hi.astype(F32))
    w = wr_ref[...]
    w_hi = _bf(w)
    w_lo = _bf(w - w_hi.astype(F32))
    logits = _dot(h_hi, w_hi) + _dot(h_lo, w_hi) + _dot(h_hi, w_lo)
    lane = lax.broadcasted_iota(jnp.int32, logits.shape, 1)
    lanef = lane.astype(F32)
    logits = jnp.where(lane < N_EXPERTS, logits, -jnp.inf)
    v0 = jnp.max(logits, axis=1, keepdims=True)
    e0 = jnp.min(jnp.where(logits == v0, lanef, 1e9), axis=1, keepdims=True)
    rest = jnp.where(lanef == e0, -jnp.inf, logits)
    v1 = jnp.max(rest, axis=1, keepdims=True)
    e1 = jnp.min(jnp.where(rest == v1, lanef, 1e9), axis=1, keepdims=True)
    ex = jnp.exp(v1 - v0)
    g0 = 1.0 / (1.0 + ex)
    g1 = ex / (1.0 + ex)
    oh0 = lanef == e0
    oh1 = lanef == e1
    oh = jnp.where(jnp.logical_or(oh0, oh1), 1.0, 0.0)
    before = _dot(tri_ref[...], _bf(oh)) + carry_ref[0:1, :]
    r0 = jnp.sum(jnp.where(oh0, before, 0.0), axis=1, keepdims=True)
    r1 = jnp.sum(jnp.where(oh1, before, 0.0), axis=1, keepdims=True)
    carry_ref[0:1, :] = carry_ref[0:1, :] + jnp.sum(oh, axis=0, keepdims=True)
    meta = jnp.zeros(logits.shape, F32)
    for j, val in ((_META_E0, e0), (_META_E1, e1), (_META_R0, r0), (_META_R1, r1),
                   (_META_G0, g0), (_META_G1, g1)):
        meta = jnp.where(lane == j, val, meta)
    meta_ref[...] = meta
    cnt_ref[...] = carry_ref[...]


def _router(xs, n_rows, g2, modtab, w_router, lat_rows):
    d = xs.shape[1]
    nb = modtab.shape[0] - 1
    tm = ROW_TILE
    mi = _mod_index(tm, lat_rows, nb)
    wr = jnp.zeros((d, 128), F32).at[:, :N_EXPERTS].set(w_router)
    tri = jnp.asarray(np.tril(np.ones((tm, tm), np.float32), -1), BF16)
    return pl.pallas_call(
        _router_kernel,
        grid=(n_rows // tm,),
        in_specs=[pl.BlockSpec((tm, d), lambda i: (i, 0)),
                  pl.BlockSpec((1, d), lambda i: (0, 0)),
                  pl.BlockSpec((1, 6, d), lambda i: (mi(i), 0, 0)),
                  pl.BlockSpec((d, 128), lambda i: (0, 0)),
                  pl.BlockSpec((tm, tm), lambda i: (0, 0))],
        out_specs=[pl.BlockSpec((tm, 128), lambda i: (i, 0)),
                   pl.BlockSpec((8, 128), lambda i: (0, 0))],
        out_shape=[jax.ShapeDtypeStruct((n_rows, 128), F32),
                   jax.ShapeDtypeStruct((8, 128), F32)],
        scratch_shapes=[pltpu.VMEM((8, 128), F32)],
        compiler_params=_cparams(("arbitrary",), 32),
        name="router",
    )(xs, g2.reshape(1, d), modtab, wr, tri)


def _dispatch_kernel(dest_ref, x_ref, g_ref, mod_ref, init_ref, o_ref, h_s, sem):
    del init_ref
    i = pl.program_id(0)
    tm = h_s.shape[0]
    m = mod_ref[0]
    h_s[...] = _rms(x_ref[...], g_ref[...]) * (1.0 + m[4:5]) + m[3:4]

    def row_copy(r, k):
        dst = dest_ref[TOP_K * (i * tm + r) + k]
        return pltpu.make_async_copy(h_s.at[pl.ds(r, 1)], o_ref.at[pl.ds(dst, 1)], sem)

    def start(r, carry):
        for k in range(TOP_K):
            row_copy(r, k).start(priority=k)
        return carry

    lax.fori_loop(0, tm, start, 0, unroll=DMA_UNROLL)

    def wait(r, carry):
        for k in range(TOP_K):
            row_copy(r, k).wait()
        return carry

    lax.fori_loop(0, tm, wait, 0, unroll=DMA_UNROLL)


def _dispatch(dest, xs, n_rows, g2, modtab, n_slots, lat_rows):
    d = xs.shape[1]
    nb = modtab.shape[0] - 1
    tm = ROW_TILE
    mi = _mod_index(tm, lat_rows, nb)
    init = jnp.zeros((n_slots, d), F32)
    grid_spec = pltpu.PrefetchScalarGridSpec(
        num_scalar_prefetch=1,
        grid=(n_rows // tm,),
        in_specs=[pl.BlockSpec((tm, d), lambda i, dst: (i, 0)),
                  pl.BlockSpec((1, d), lambda i, dst: (0, 0)),
                  pl.BlockSpec((1, 6, d), lambda i, dst: (mi(i), 0, 0)),
                  pl.BlockSpec(memory_space=pl.ANY)],
        out_specs=pl.BlockSpec(memory_space=pl.ANY),
        scratch_shapes=[pltpu.VMEM((tm, d), F32), pltpu.SemaphoreType.DMA(())],
    )
    return pl.pallas_call(
        _dispatch_kernel,
        grid_spec=grid_spec,
        out_shape=jax.ShapeDtypeStruct((n_slots, d), F32),
        input_output_aliases={4: 0},
        compiler_params=_cparams(("arbitrary",), 32),
        name="dispatch",
    )(dest, xs, g2.reshape(1, d), modtab, init)


def _expert_kernel(ge_ref, nt_ref, ng_ref, x_ref, *refs):
    w1_refs, w3_refs = refs[:MOE_WSPLIT], refs[MOE_WSPLIT:2 * MOE_WSPLIT]
    w2_ref, o_ref, xb_s = refs[2 * MOE_WSPLIT:]
    s = pl.program_id(0)
    f = pl.program_id(1)
    tm = MOE_TM

    @pl.when(s < ng_ref[0])
    def _():
        @pl.when(f == 0)
        def _():
            xb_s[...] = _bf(x_ref[...])
            o_ref[...] = jnp.zeros(o_ref.shape, F32)

        w1b = jnp.concatenate([_bf(r[0]) for r in w1_refs], axis=0)
        w3b = jnp.concatenate([_bf(r[0]) for r in w3_refs], axis=0)
        w2b = _bf(w2_ref[0])

        def tile(t):
            rows = pl.ds(t * tm, tm)
            h = xb_s[rows, :]
            u = _silu(_dot(h, w1b)) * _dot(h, w3b)
            o_ref[rows, :] += _dot(_bf(u), w2b)

        tile(0)
        for t in range(1, MOE_GROUP):
            pl.when(nt_ref[s] > t)(functools.partial(tile, t))

    @pl.when(jnp.logical_and(s >= ng_ref[0], f == pl.num_programs(1) - 1))
    def _():
        o_ref[...] = jnp.zeros(o_ref.shape, F32)


def _experts(group_e, group_tiles, n_groups, xsort, w1, w3, w2):
    n_slots, d = xsort.shape
    dff = w1.shape[2]
    tg, tf = MOE_TM * MOE_GROUP, MOE_TF
    nf = dff // tf

    def ss(s, ng):
        return jnp.maximum(jnp.minimum(s, ng[0] - 1), 0)

    def ff(s, f, ng):
        snake = jnp.where(s % 2 == 1, nf - 1 - f, f)
        last = jnp.where((ng[0] - 1) % 2 == 1, 0, nf - 1)
        return jnp.where(s < ng[0], snake, last)

    grid_spec = pltpu.PrefetchScalarGridSpec(
        num_scalar_prefetch=3,
        grid=(n_slots // tg, nf),
        in_specs=[pl.BlockSpec((tg, d), lambda s, f, ge, nt, ng: (ss(s, ng), 0)),
                  *[pl.BlockSpec((1, d // MOE_WSPLIT, tf),
                                 lambda s, f, ge, nt, ng, part=part: (ge[ss(s, ng)], part, ff(s, f, ng)))
                    for _ in range(2) for part in range(MOE_WSPLIT)],
                  pl.BlockSpec((1, tf, d), lambda s, f, ge, nt, ng: (ge[ss(s, ng)], ff(s, f, ng), 0))],
        out_specs=pl.BlockSpec((tg, d), lambda s, f, ge, nt, ng: (s, 0)),
        scratch_shapes=[pltpu.VMEM((tg, d), BF16)],
    )
    return pl.pallas_call(
        _expert_kernel,
        grid_spec=grid_spec,
        out_shape=jax.ShapeDtypeStruct((n_slots, d), F32),
        compiler_params=_cparams(("arbitrary", "arbitrary"), 56),
        name="experts",
    )(group_e, group_tiles, n_groups, xsort, *([w1] * MOE_WSPLIT), *([w3] * MOE_WSPLIT), w2)


def _combine_kernel(dest_ref, y_ref, meta_ref, x_ref, g_ref, mod_ref, o_ref, buf_s, sem):
    i = pl.program_id(0)
    tm = x_ref.shape[0]

    def row_copy(r, k):
        src = dest_ref[TOP_K * (i * tm + r) + k]
        return pltpu.make_async_copy(y_ref.at[pl.ds(src, 1)], buf_s.at[k, pl.ds(r, 1)], sem)

    def start(r, carry):
        for k in range(TOP_K):
            row_copy(r, k).start(priority=k)
        return carry

    lax.fori_loop(0, tm, start, 0, unroll=DMA_UNROLL)

    def wait(r, carry):
        for k in range(TOP_K):
            row_copy(r, k).wait()
        return carry

    lax.fori_loop(0, tm, wait, 0, unroll=DMA_UNROLL)

    m = mod_ref[0]
    g = g_ref[...]
    def chunk(rows):
        meta = meta_ref[rows, :]
        lane = lax.broadcasted_iota(jnp.int32, meta.shape, 1)
        g0 = jnp.sum(jnp.where(lane == _META_G0, meta, 0.0), axis=1, keepdims=True)
        g1 = jnp.sum(jnp.where(lane == _META_G1, meta, 0.0), axis=1, keepdims=True)
        y = buf_s[0, rows, :] * g0 + buf_s[1, rows, :] * g1
        o_ref[rows, :] = x_ref[rows, :] + m[5:6] * _rms(y, g)

    _row_chunks(tm, chunk)


def _combine(dest, yb, meta, xs, n_rows, g3, modtab, lat_rows):
    d = xs.shape[1]
    nb = modtab.shape[0] - 1
    tm = ROW_TILE
    mi = _mod_index(tm, lat_rows, nb)
    grid_spec = pltpu.PrefetchScalarGridSpec(
        num_scalar_prefetch=1,
        grid=(n_rows // tm,),
        in_specs=[pl.BlockSpec(memory_space=pl.ANY),
                  pl.BlockSpec((tm, 128), lambda i, dst: (i, 0)),
                  pl.BlockSpec((tm, d), lambda i, dst: (i, 0)),
                  pl.BlockSpec((1, d), lambda i, dst: (0, 0)),
                  pl.BlockSpec((1, 6, d), lambda i, dst: (mi(i), 0, 0))],
        out_specs=pl.BlockSpec((tm, d), lambda i, dst: (i, 0)),
        scratch_shapes=[pltpu.VMEM((TOP_K, tm, d), F32), pltpu.SemaphoreType.DMA(())],
    )
    return pl.pallas_call(
        _combine_kernel,
        grid_spec=grid_spec,
        out_shape=jax.ShapeDtypeStruct((n_rows, d), F32),
        compiler_params=_cparams(("arbitrary",), 40),
        name="combine",
    )(dest, yb, meta, xs, g3.reshape(1, d), modtab)


def _moe(xs, n_rows, g2, g3, modtab, w_router, w1, w3, w2, lat_rows):
    meta, cnt = _router(xs, n_rows, g2, modtab, w_router, lat_rows)
    counts = cnt[0, :N_EXPERTS].astype(jnp.int32)
    tg = MOE_TM * MOE_GROUP
    padded = (counts + tg - 1) // tg * tg
    pad_end = jnp.cumsum(padded)
    pad_start = pad_end - padded
    e = meta[:, _META_E0:_META_E1 + 1].astype(jnp.int32)
    r = meta[:, _META_R0:_META_R1 + 1].astype(jnp.int32)
    dest = (pad_start[e] + r).reshape(-1)
    n_groups_max = (n_rows * TOP_K) // tg + N_EXPERTS
    n_slots = n_groups_max * tg
    group_start = jnp.arange(n_groups_max, dtype=jnp.int32) * tg
    group_e = jnp.minimum(jnp.sum((group_start[:, None] >= pad_end[None, :]).astype(jnp.int32), axis=1),
                          N_EXPERTS - 1)
    filled = counts[group_e] - (group_start - pad_start[group_e])
    group_tiles = jnp.clip((filled + MOE_TM - 1) // MOE_TM, 1, MOE_GROUP).astype(jnp.int32)
    n_groups = (pad_end[-1:] // tg).astype(jnp.int32)
    xsort = _dispatch(dest, xs, n_rows, g2, modtab, n_slots, lat_rows)
    yb = _experts(group_e, group_tiles, n_groups, xsort, w1, w3, w2)
    return _combine(dest, yb, meta, xs, n_rows, g3, modtab, lat_rows)


def _rotary_tables(lat_len, ctx_len):
    rows = lat_len // GRID_W
    row = jnp.repeat(jnp.arange(rows, dtype=F32), GRID_W)
    col = jnp.tile(jnp.arange(GRID_W, dtype=F32), rows)
    n_freq = RET_DK // 4
    inv = ROPE_BASE ** (-jnp.arange(n_freq, dtype=F32) / n_freq)
    ang = jnp.concatenate([row[:, None] * inv, col[:, None] * inv], axis=-1)
    cos, sin = jnp.cos(ang), jnp.sin(ang)
    cosf = jnp.concatenate([jnp.ones((ctx_len, RET_DK), F32), jnp.concatenate([cos, cos], -1)], 0)
    sinf = jnp.concatenate([jnp.zeros((ctx_len, RET_DK), F32), jnp.concatenate([-sin, sin], -1)], 0)
    return cosf, sinf


def kernel(x, c, ctx, c_ctx, w_ada, b_ada, norm_g, w_in, w_out, ret_decay, ret_gn, hgrn_lb, hgrn_gn,
           mlstm_conv, mlstm_gate_b, mlstm_gn, w_ffn1, w_ffn3, w_ffn2, w_router, w_exp1, w_exp3, w_exp2):
    nb, lat_len, d = x.shape
    ctx_len = ctx.shape[1]
    depth = w_ada.shape[0]
    n_lat_rows = nb * lat_len
    n_rows = n_lat_rows + nb * ctx_len

    x_parts = [x.reshape(n_lat_rows, d), ctx.reshape(nb * ctx_len, d)]
    w_in_t = jnp.swapaxes(w_in, 1, 2)
    s_in = jnp.zeros((8, d), F32).at[:nb].set(c).at[nb].set(c_ctx)
    mod_all = _ada(s_in, w_ada, b_ada)[:, :nb + 1].reshape(depth, nb + 1, 6, d)

    cosf, sinf = _rotary_tables(lat_len, ctx_len)
    hg_cum, hg_masks, hg_signs = _hgrn_constants()
    hg_consts = (jnp.asarray(hg_cum, BF16), jnp.asarray(hg_masks, F32), jnp.asarray(hg_signs, F32))
    sm = jax.nn.softmax(hgrn_lb.astype(F32), axis=0)
    lb_all = jnp.clip(jnp.cumsum(sm, axis=0) - sm[0], 0.0, 1.0)

    hb = None
    for layer in range(depth):
        last = layer == depth - 1
        modtab = mod_all[layer]
        g = norm_g[layer]

        if hb is None:
            hb = _prenorm(x_parts, n_rows, g[0], modtab, 1, 0, lat_len, BF16)
        p = _mm(hb, w_in_t, layer, MM_TM, MM_TN, n_cols=PROJ_MAIN)
        w_g = jnp.zeros((1, 128, d), F32).at[0, :MLSTM_GATES].set(w_in_t[layer, PROJ_MAIN:])
        gates = _mm(hb, w_g, 0, MM_TM, 128)[:, :MLSTM_GATES]
        hb = None
        g_row_lat = jnp.swapaxes(gates[:n_lat_rows].reshape(nb, lat_len, MLSTM_GATES), 1, 2)
        g_row_ctx = jnp.swapaxes(gates[n_lat_rows:].reshape(nb, ctx_len, MLSTM_GATES), 1, 2)

        log_g = jax.nn.log_sigmoid(ret_decay[layer].astype(F32))
        lg_b = jnp.broadcast_to(log_g[:, :, None, None], (2, RET_HEADS, 8, RET_DV))
        lb = lb_all[layer]
        emit_ctx = not last
        o_ret = _retention(p, nb, lat_len, ctx_len, cosf, sinf, lg_b, ret_gn[layer], emit_ctx)
        o_hg = _hgrn(p, nb, lat_len, ctx_len, jnp.log(lb), jnp.log1p(-lb), hgrn_gn[layer],
                     hg_consts, emit_ctx)
        o_ml = _mlstm(p, gates, g_row_lat, g_row_ctx, nb, lat_len, ctx_len, mlstm_gate_b[layer],
                      mlstm_conv[layer], mlstm_gn[layer], emit_ctx)
        mix_parts = [list(o) for o in (o_ret, o_hg, o_ml)]
        rows_now = n_lat_rows if last else n_rows
        dense = layer % 2 == 0
        res = _wout(mix_parts, _bf(w_out[layer]), x_parts, rows_now, g[1], modtab, lat_len,
                    g[2] if dense else None)
        xs = res[0]
        x_parts = [xs]

        j = layer // 2
        if dense:
            nxt = (None, None) if last else (norm_g[layer + 1][0], mod_all[layer + 1])
            res = _ffn(res[1], _bf(w_ffn1[j]), _bf(w_ffn3[j]), _bf(w_ffn2[j]), xs, g[3], modtab, lat_len, *nxt)
            xs = res[0]
            hb = None if last else res[1]
        else:
            xs = _moe(xs, rows_now, g[2], g[3], modtab, w_router[j], w_exp1[j], w_exp3[j], w_exp2[j], lat_len)
        x_parts = [xs]
    return xs[:n_lat_rows].reshape(nb, lat_len, d)
```

```python
import functools

import numpy as np
import jax
import jax.numpy as jnp
from jax import lax
from jax.experimental import pallas as pl
from jax.experimental.pallas import tpu as pltpu

F32 = jnp.float32
BF16 = jnp.bfloat16

CHUNK = 128
NORM_EPS = 1e-6
ROPE_BASE = 10000.0
GRID_W = 64

RET_HEADS, RET_DK, RET_DV = 4, 128, 256
HGRN_HEADS, HGRN_DK, HGRN_DV = 4, 128, 128
MLSTM_HEADS, MLSTM_DK, MLSTM_DV = 4, 128, 128
N_EXPERTS = 8
TOP_K = 2

RET_QK = RET_HEADS * RET_DK
RET_WIDTH = RET_HEADS * RET_DV
HGRN_QK = HGRN_HEADS * HGRN_DK
HGRN_WIDTH = HGRN_HEADS * HGRN_DV
MLSTM_QK = MLSTM_HEADS * MLSTM_DK
MLSTM_WIDTH = MLSTM_HEADS * MLSTM_DV
MLSTM_GATES = 4 * MLSTM_HEADS

_C_RET_Q = 0
_C_RET_K = _C_RET_Q + RET_QK // 128
_C_RET_V = _C_RET_K + RET_QK // 128
_C_RET_G = _C_RET_V + RET_WIDTH // 128
_C_HG_Q = _C_RET_G + RET_WIDTH // 128
_C_HG_FF = _C_HG_Q + HGRN_QK // 128
_C_HG_FB = _C_HG_FF + HGRN_QK // 128
_C_HG_I = _C_HG_FB + HGRN_QK // 128
_C_HG_G = _C_HG_I + HGRN_WIDTH // 128
_C_ML_Q = _C_HG_G + HGRN_WIDTH // 128
_C_ML_K = _C_ML_Q + MLSTM_QK // 128
_C_ML_V = _C_ML_K + MLSTM_QK // 128
_C_ML_O = _C_ML_V + MLSTM_WIDTH // 128
PROJ_MAIN = (_C_ML_O + MLSTM_WIDTH // 128) * 128

_VMEM_CAP_BYTES = 56 * 1024 * 1024

ROW_TILE = 512
MM_TM, MM_TN = 1024, 1280
FFN_TF = 512
MOE_TM = 768
MOE_WSPLIT = 1
MOE_GROUP = 1
MOE_TF = 512
RET_UNROLL = (4, 4)
HGRN_UNROLL = (4, 4)
MLSTM_UNROLL = (2, 4)
DMA_UNROLL = 8
EPI_ROWS = None


def _cparams(sem, vmem_mb):
    return pltpu.CompilerParams(dimension_semantics=sem,
                                vmem_limit_bytes=min(int(vmem_mb * 1024 * 1024), _VMEM_CAP_BYTES))


def _bf(x):
    return x.astype(BF16)


def _dot(a, b):
    return jnp.dot(a, b, preferred_element_type=F32)


def _dot_nt(a, b):
    return lax.dot_general(a, b, (((1,), (1,)), ((), ())), preferred_element_type=F32)


def _dot_tn(a, b):
    return lax.dot_general(a, b, (((0,), (0,)), ((), ())), preferred_element_type=F32)


def _sigmoid(x):
    return 1.0 / (1.0 + jnp.exp(-x))


def _silu(x):
    return x * _sigmoid(x)


def _log_sigmoid(x):
    return jnp.minimum(x, 0.0) - jnp.log(1.0 + jnp.exp(-jnp.abs(x)))


def _rms(x, g):
    return x * lax.rsqrt(jnp.mean(x * x, axis=-1, keepdims=True) + NORM_EPS) * g


def _mod_index(tile_rows, n_lat_rows_per_batch, n_batch):
    return lambda i: jnp.minimum((i * tile_rows) // n_lat_rows_per_batch, n_batch)


def _ada_kernel(s_ref, w_ref, b_ref, o_ref):
    s = _bf(_silu(s_ref[...]))
    o_ref[0] = _dot(s, _bf(w_ref[0])) + b_ref[0]


def _ada(s_in, w_ada, b_ada):
    depth, d, n = w_ada.shape
    tn = 1024
    return pl.pallas_call(
        _ada_kernel,
        grid=(depth, n // tn),
        in_specs=[pl.BlockSpec((8, d), lambda l, j: (0, 0)),
                  pl.BlockSpec((1, d, tn), lambda l, j: (l, 0, j)),
                  pl.BlockSpec((1, 1, tn), lambda l, j: (l, 0, j))],
        out_specs=pl.BlockSpec((1, 8, tn), lambda l, j: (l, 0, j)),
        out_shape=jax.ShapeDtypeStruct((depth, 8, n), F32),
        compiler_params=_cparams(("arbitrary", "arbitrary"), 40),
        name="ada",
    )(s_in, w_ada, b_ada.reshape(depth, 1, n))


def _row_chunks(n_rows, fn):
    if EPI_ROWS is None:
        fn(pl.ds(0, n_rows))
        return

    def body(i, carry):
        fn(pl.ds(pl.multiple_of(i * EPI_ROWS, EPI_ROWS), EPI_ROWS))
        return carry
    lax.fori_loop(0, n_rows // EPI_ROWS, body, 0, unroll=2)


def _part_specs(parts, tm, width):
    if len(parts) == 1:
        return [pl.BlockSpec((tm, width), lambda i: (i, 0))]
    n0 = parts[0].shape[0] // tm
    return [pl.BlockSpec((tm, width), lambda i: (jnp.minimum(i, n0 - 1), 0)),
            pl.BlockSpec((tm, width), lambda i: (jnp.maximum(i - n0, 0), 0))]


def _on_part(n_first, n_parts, fn):
    if n_parts == 1:
        fn(0)
    else:
        i = pl.program_id(0)
        pl.when(i < n_first)(lambda: fn(0))
        pl.when(i >= n_first)(lambda: fn(1))


def _prenorm_kernel(*refs, sc, sh, n_first):
    x_refs, (g_ref, mod_ref, o_ref) = refs[:-3], refs[-3:]

    def run(part):
        m = mod_ref[0]
        g = g_ref[...]

        def chunk(rows):
            y = _rms(x_refs[part][rows, :], g)
            o_ref[rows, :] = (y * (1.0 + m[sc:sc + 1]) + m[sh:sh + 1]).astype(o_ref.dtype)

        _row_chunks(o_ref.shape[0], chunk)

    _on_part(n_first, len(x_refs), run)


def _prenorm(x_parts, n_rows, g, modtab, sc, sh, lat_rows, out_dtype):
    d = x_parts[0].shape[1]
    nb = modtab.shape[0] - 1
    mi = _mod_index(ROW_TILE, lat_rows, nb)
    return pl.pallas_call(
        functools.partial(_prenorm_kernel, sc=sc, sh=sh, n_first=x_parts[0].shape[0] // ROW_TILE),
        grid=(n_rows // ROW_TILE,),
        in_specs=[*_part_specs(x_parts, ROW_TILE, d),
                  pl.BlockSpec((1, d), lambda i: (0, 0)),
                  pl.BlockSpec((1, 6, d), lambda i: (mi(i), 0, 0))],
        out_specs=pl.BlockSpec((ROW_TILE, d), lambda i: (i, 0)),
        out_shape=jax.ShapeDtypeStruct((n_rows, d), out_dtype),
        compiler_params=_cparams(("arbitrary",), 32),
        name="prenorm",
    )(*x_parts, g.reshape(1, d), modtab)


def _mm_kernel(x_ref, wt_ref, o_ref, wb_s):
    @pl.when(pl.program_id(1) == 0)
    def _():
        wb_s[...] = _bf(wt_ref[0])

    o_ref[...] = _dot_nt(x_ref[...], wb_s[...]).astype(o_ref.dtype)


def _mm(x, wt, layer, tm, tn, n_cols=None, out_dtype=F32):
    m, k = x.shape
    n = wt.shape[1] if n_cols is None else n_cols
    return pl.pallas_call(
        _mm_kernel,
        grid=(n // tn, m // tm),
        in_specs=[pl.BlockSpec((tm, k), lambda j, i: (i, 0)),
                  pl.BlockSpec((1, tn, k), lambda j, i: (layer, j, 0))],
        out_specs=pl.BlockSpec((tm, tn), lambda j, i: (i, j)),
        out_shape=jax.ShapeDtypeStruct((m, n), out_dtype),
        scratch_shapes=[pltpu.VMEM((tn, k), BF16)],
        compiler_params=_cparams(("arbitrary", "arbitrary"), 48),
        name="proj",
    )(x, wt)


def _bwd_chunk(i, n_ctx, n_lat):
    return jnp.where(i < n_ctx, n_ctx - 1 - i, 2 * n_ctx + n_lat - 1 - i)


def _rows(c):
    return pl.ds(pl.multiple_of(c * CHUNK, CHUNK), CHUNK)


def _for_chunks(n_ctx, n_lat, fn, unroll, with_ctx=True):
    def run(part, n, off):
        def body(c, carry):
            fn(part, c, c + off)
            return carry
        lax.fori_loop(0, n, body, 0, unroll=max(u for u in (1, 2, unroll) if n % u == 0 and u <= unroll))
    if with_ctx:
        run(0, n_ctx, 0)
    run(1, n_lat, n_ctx)


def _lanes(d, w):
    return slice(d * w, (d + 1) * w)


def _mixer_call(kernel_fn, name, in_specs, args, nb, heads, lat_len, ctx_len, dv, scratch, emit_ctx, vmem_mb):
    width = heads * dv
    out_specs = [pl.BlockSpec((lat_len, dv), lambda b, h: (b, h))]
    out_shape = [jax.ShapeDtypeStruct((nb * lat_len, width), BF16)]
    if emit_ctx:
        out_specs.append(pl.BlockSpec((ctx_len, dv), lambda b, h: (b, h)))
        out_shape.append(jax.ShapeDtypeStruct((nb * ctx_len, width), BF16))
    n_in = len(in_specs)

    def body(*refs):
        ins, rest = refs[:n_in], refs[n_in:]
        if emit_ctx:
            ol, oc, scr = rest[0], rest[1], rest[2:]
        else:
            ol, oc, scr = rest[0], None, rest[1:]
        kernel_fn(*ins, ol, oc, *scr, n_ctx=ctx_len // CHUNK, n_lat=lat_len // CHUNK)

    return pl.pallas_call(
        body,
        grid=(nb, heads),
        in_specs=in_specs,
        out_specs=out_specs,
        out_shape=out_shape,
        scratch_shapes=scratch,
        compiler_params=_cparams(("arbitrary", "arbitrary"), vmem_mb),
        name=name,
    )(*args)


def _seq_specs(col0, width, nb, lat_len, ctx_len):
    n_lat_blk = nb * lat_len // ctx_len
    c0 = col0 * 128 // width
    return (pl.BlockSpec((lat_len, width), lambda b, h: (b, c0 + h)),
            pl.BlockSpec((ctx_len, width), lambda b, h: (n_lat_blk + b, c0 + h)))


def _ret_kernel(ql, qc, kl, kc, vl, vc, gl, gc, cos_ref, sin_ref, lg_ref, gn_ref, ol_ref, oc_ref,
                qs_s, kv_s, sall_s, o_s, st_s, dm_s, dq_s, dk_s, *, n_ctx, n_lat):
    nch = n_ctx + n_lat
    unroll_a, unroll_c = RET_UNROLL
    qr, kr, vr, gr, outr = (qc, ql), (kc, kl), (vc, vl), (gc, gl), (oc_ref, ol_ref)
    t_i = lax.broadcasted_iota(jnp.int32, (CHUNK, CHUNK), 0).astype(F32)
    s_i = lax.broadcasted_iota(jnp.int32, (CHUNK, CHUNK), 1).astype(F32)
    dm = None
    for d in range(2):
        lg = lg_ref[d, 0][0:1, 0:CHUNK]
        rel = (t_i - s_i) if d == 0 else (s_i - t_i)
        dmd = jnp.where(rel >= 0, jnp.exp(jnp.maximum(rel, 0.0) * lg), 0.0)
        dm = dmd if dm is None else dm + dmd
        p = t_i if d == 0 else (CHUNK - 1.0) - t_i
        dq_s[d] = jnp.exp((p + 1.0) * lg)
        dk_s[d] = jnp.exp((CHUNK - 1.0 - p) * lg)
    dm_s[...] = dm
    scale = RET_DK ** -0.5

    def phase_a(part, cl, cg):
        rl, rg = _rows(cl), _rows(cg)
        cs = cos_ref[rg, :]
        sn = sin_ref[rg, :]
        q = qr[part][rl, :]
        k = kr[part][rl, :]
        q = q * cs + pltpu.roll(q, RET_DK // 2, 1) * sn
        k = (k * cs + pltpu.roll(k, RET_DK // 2, 1) * sn) * scale
        vb = _bf(vr[part][rl, :])
        s = _dot_nt(_bf(q), _bf(k))
        o_s[rg, :] = _dot(_bf(s * dm_s[...]), vb)
        qs_s[rg, :] = jnp.concatenate([_bf(q * dq_s[0]), _bf(q * dq_s[1])], axis=1)
        kd = jnp.concatenate([_bf(k * dk_s[0]), _bf(k * dk_s[1])], axis=1)
        kv_s[cg] = _dot_tn(kd, vb)

    _for_chunks(n_ctx, n_lat, phase_a, unroll_a)

    st_s[...] = jnp.zeros(st_s.shape, F32)
    dec = [jnp.exp(float(CHUNK) * lg_ref[d, 0][0:1, :]) for d in range(2)]

    def phase_b(i, carry):
        for d, c in ((0, i), (1, _bwd_chunk(i, n_ctx, n_lat))):
            st = st_s[d]
            sall_s[c, _lanes(d, RET_DK), :] = _bf(st)
            st_s[d] = dec[d] * st + kv_s[c, _lanes(d, RET_DK), :]
        return carry

    lax.fori_loop(0, nch, phase_b, 0)
    gn = gn_ref[...]

    def phase_c(part, cl, cg):
        rl, rg = _rows(cl), _rows(cg)
        o = o_s[rg, :] + _dot(qs_s[rg, :], sall_s[cg])
        outr[part][rl, :] = (_rms(o, gn) * _silu(gr[part][rl, :])).astype(BF16)

    _for_chunks(n_ctx, n_lat, phase_c, unroll_c, with_ctx=oc_ref is not None)


def _retention(p, nb, lat_len, ctx_len, cosf, sinf, lg_b, gn, emit_ctx):
    t_len = lat_len + ctx_len
    nch = t_len // CHUNK
    s128 = functools.partial(_seq_specs, width=128, nb=nb, lat_len=lat_len, ctx_len=ctx_len)
    s256 = functools.partial(_seq_specs, width=256, nb=nb, lat_len=lat_len, ctx_len=ctx_len)
    in_specs = [*s128(_C_RET_Q), *s128(_C_RET_K), *s256(_C_RET_V), *s256(_C_RET_G),
                pl.BlockSpec((t_len, 128), lambda b, h: (0, 0)),
                pl.BlockSpec((t_len, 128), lambda b, h: (0, 0)),
                pl.BlockSpec((2, 1, 8, RET_DV), lambda b, h: (0, h, 0, 0)),
                pl.BlockSpec((1, RET_DV), lambda b, h: (0, h))]
    scratch = [pltpu.VMEM((t_len, 2 * RET_DK), BF16),
               pltpu.VMEM((nch, 2 * RET_DK, RET_DV), F32),
               pltpu.VMEM((nch, 2 * RET_DK, RET_DV), BF16),
               pltpu.VMEM((t_len, RET_DV), F32),
               pltpu.VMEM((2, RET_DK, RET_DV), F32),
               pltpu.VMEM((CHUNK, CHUNK), F32),
               pltpu.VMEM((2, CHUNK, CHUNK), F32), pltpu.VMEM((2, CHUNK, CHUNK), F32)]
    args = (p, p, p, p, p, p, p, p, cosf, sinf, lg_b, gn.reshape(1, RET_WIDTH))
    return _mixer_call(_ret_kernel, "retention", in_specs, args, nb, RET_HEADS, lat_len, ctx_len, RET_DV,
                       scratch, emit_ctx, 48)


_HG_LEVELS = (64, 32, 16, 8, 4, 2, 1)


def _hgrn_constants():
    c = CHUNK
    t = np.arange(c)[:, None]
    u = np.arange(c)[None, :]
    cum = (u <= t).astype(np.float32)
    masks, signs = [], []
    for m in _HG_LEVELS:
        base = (t // (2 * m)) * (2 * m)
        lower = t >= base + m
        tb = t // (2 * m)
        sb = u // (2 * m)
        masks.append((tb == sb) & lower & (u < (sb * 2 * m + m)))
        signs.append(np.broadcast_to(np.where(lower, 1.0, -1.0), (c, c)))
    masks.append(t == u)
    kf = np.stack([x.astype(np.float32) for x in masks], axis=0)
    kb = np.stack([x.astype(np.float32)[::-1, ::-1] for x in masks], axis=0)
    sf = np.stack([x.astype(np.float32) for x in signs], axis=0)
    sb_ = np.stack([x.astype(np.float32)[::-1, ::-1] for x in signs], axis=0)
    return (np.stack([cum, cum[::-1, ::-1]], 0), np.stack([kf, kb], 0),
            np.stack([sf[:_HG_WIDE], sb_[:_HG_WIDE]], 0))


_HG_WIDE = 5


def _hgrn_level_exponents(b, lf, d, sgn_ref, row):
    out = []
    for l, m in enumerate(_HG_LEVELS[:_HG_WIDE]):
        pieces = []
        for j in range(CHUNK // (2 * m)):
            r = 2 * m * j + (m - 1 if d == 0 else m)
            pieces.append(jnp.broadcast_to(b[r:r + 1, :], (2 * m, CHUNK)))
        bref = pieces[0] if len(pieces) == 1 else jnp.concatenate(pieces, axis=0)
        out.append((b - bref) * sgn_ref[d, l])
    up = pltpu.roll(lf, CHUNK - 1, 0)
    dn = pltpu.roll(lf, 1, 0)
    r4 = row % 4
    if d == 0:
        e2 = jnp.where(r4 == 0, up, jnp.where(r4 == 1, 0.0, jnp.where(r4 == 2, lf, lf + dn)))
        e1 = jnp.where(row % 2 == 1, lf, 0.0)
    else:
        e2 = jnp.where(r4 == 0, lf + up, jnp.where(r4 == 1, lf, jnp.where(r4 == 2, 0.0, dn)))
        e1 = jnp.where(row % 2 == 0, lf, 0.0)
    return out + [e2, e1]


def _hgrn_kernel(ql, qc, ffl, ffc, fbl, fbc, il, ic, gl, gc, llb_ref, lub_ref, gn_ref, c_ref, k_ref, sgn_ref,
                 ol_ref, oc_ref, qs_s, kv_s, sall_s, dec_s, o_s, st_s, *, n_ctx, n_lat):
    nch = n_ctx + n_lat
    unroll_a, unroll_c = HGRN_UNROLL
    qr, fr, ir, gr, outr =(qc, ql), ((ffc, ffl), (fbc, fbl)), (ic, il), (gc, gl), (oc_ref, ol_ref)
    llb = llb_ref[...]
    lub = lub_ref[...]
    nlev = len(_HG_LEVELS)
    row = lax.broadcasted_iota(jnp.int32, (CHUNK, HGRN_DK), 0)

    def phase_a(part, cl, cg):
        rl, rg = _rows(cl), _rows(cg)
        q = _silu(qr[part][rl, :])
        qb = _bf(q)
        vb = _bf(ir[part][rl, :])
        a_sum, qs, kds = None, [], []
        for d in range(2):
            lsg = lub + _log_sigmoid(fr[d][part][rl, :])
            lf = jnp.maximum(llb, lsg) + jnp.log(1.0 + jnp.exp(-jnp.abs(llb - lsg)))
            k = 1.0 - jnp.exp(lf)
            lf_hi = _bf(lf)
            lf_lo = _bf(lf - lf_hi.astype(F32))
            b2 = _dot(c_ref[d], jnp.concatenate([lf_hi, lf_lo], axis=1))
            b = b2[:, 0:HGRN_DK] + b2[:, HGRN_DK:]
            es = _hgrn_level_exponents(b, lf, d, sgn_ref, row)
            kb = _bf(k)
            a = k_ref[d, nlev] * _dot_nt(qb, kb)
            for l in range(nlev):
                xb = _bf(jnp.exp(es[l]))
                a = a + k_ref[d, l] * _dot_nt(qb * xb, kb * xb)
            a_sum = a if a_sum is None else a_sum + a
            b_last = b[CHUNK - 1:CHUNK] if d == 0 else b[0:1]
            qs.append(_bf(q * jnp.exp(b)))
            kds.append(_bf(k * jnp.exp(b_last - b)))
            dec_s[cg, :, _lanes(d, HGRN_DK)] = jnp.exp(b_last)
        o_s[rg, :] = _dot(_bf(a_sum), vb)
        qs_s[rg, :] = jnp.concatenate(qs, axis=1)
        kv_s[cg] = _dot_tn(vb, jnp.concatenate(kds, axis=1))

    _for_chunks(n_ctx, n_lat, phase_a, unroll_a)
    st_s[...] = jnp.zeros(st_s.shape, F32)

    def phase_b(i, carry):
        for d, c in ((0, i), (1, _bwd_chunk(i, n_ctx, n_lat))):
            ln = _lanes(d, HGRN_DK)
            st = st_s[d]
            sall_s[c, :, ln] = _bf(st)
            st_s[d] = dec_s[c, :, ln] * st + kv_s[c, :, ln]
        return carry

    lax.fori_loop(0, nch, phase_b, 0)
    gn = gn_ref[...]

    def phase_c(part, cl, cg):
        rl, rg = _rows(cl), _rows(cg)
        o = o_s[rg, :] + _dot_nt(qs_s[rg, :], sall_s[cg])
        outr[part][rl, :] = (_rms(o, gn) * _silu(gr[part][rl, :])).astype(BF16)

    _for_chunks(n_ctx, n_lat, phase_c, unroll_c, with_ctx=oc_ref is not None)


def _hgrn(p, nb, lat_len, ctx_len, log_lb, log_ub, gn, consts, emit_ctx):
    mats, masks, signs = consts
    t_len = lat_len + ctx_len
    nch = t_len // CHUNK
    spec = functools.partial(_seq_specs, width=128, nb=nb, lat_len=lat_len, ctx_len=ctx_len)
    vec = pl.BlockSpec((1, 128), lambda b, h: (0, h))
    in_specs = [*spec(_C_HG_Q), *spec(_C_HG_FF), *spec(_C_HG_FB), *spec(_C_HG_I), *spec(_C_HG_G),
                vec, vec, vec,
                pl.BlockSpec(mats.shape, lambda b, h: (0, 0, 0)),
                pl.BlockSpec(masks.shape, lambda b, h: (0, 0, 0, 0)),
                pl.BlockSpec(signs.shape, lambda b, h: (0, 0, 0, 0))]
    scratch = [pltpu.VMEM((t_len, 2 * HGRN_DK), BF16),
               pltpu.VMEM((nch, HGRN_DV, 2 * HGRN_DK), F32),
               pltpu.VMEM((nch, HGRN_DV, 2 * HGRN_DK), BF16),
               pltpu.VMEM((nch, 1, 2 * HGRN_DK), F32),
               pltpu.VMEM((t_len, HGRN_DV), F32),
               pltpu.VMEM((2, HGRN_DV, HGRN_DK), F32)]
    args = (p, p, p, p, p, p, p, p, p, p, log_lb.reshape(1, HGRN_QK), log_ub.reshape(1, HGRN_QK),
            gn.reshape(1, HGRN_WIDTH), mats, masks, signs)
    return _mixer_call(_hgrn_kernel, "hgrn2", in_specs, args, nb, HGRN_HEADS, lat_len, ctx_len, HGRN_DV,
                       scratch, emit_ctx, 48)


def _mlstm_kernel(ql, qc, kl, kc, vl, vc, ogl, ogc, gcl, gcc, grl, grc, bc_ref, br_ref,
                  wq_ref, wk_ref, gn_ref, ol_ref, oc_ref,
                  q_s, gr_s, intra_s, kv_s, call_s, rho_s, bcol_s, bl_s, mu_s, mprev_s, st_s, m_s,
                  *, n_ctx, n_lat):
    nch = n_ctx + n_lat
    unroll_a, unroll_c = MLSTM_UNROLL
    h = pl.program_id(1)
    nh = MLSTM_HEADS
    dk, dv = MLSTM_DK, MLSTM_DV
    ext = 2 * dv
    qr, kr, vr, ogr, gcr, outr = (qc, ql), (kc, kl), (vc, vl), (ogc, ogl), (gcc, gcl), (oc_ref, ol_ref)
    n_loc = (n_ctx, n_lat)

    for cc in range(nch):
        src, c0 = (grc, cc) if cc < n_ctx else (grl, cc - n_ctx)
        gr_s[cc] = src[0, :, c0 * CHUNK:(c0 + 1) * CHUNK] + br_ref[...]

    row = lax.broadcasted_iota(jnp.int32, (CHUNK, dk), 0)
    lane = lax.broadcasted_iota(jnp.int32, (CHUNK, dv), 1)
    del lane
    ones_col = jnp.ones((CHUNK, dv), BF16)
    scale = dk ** -0.5
    t_i = lax.broadcasted_iota(jnp.int32, (CHUNK, CHUNK), 0)
    s_i = lax.broadcasted_iota(jnp.int32, (CHUNK, CHUNK), 1)
    lane16 = lax.broadcasted_iota(jnp.int32, (CHUNK, MLSTM_GATES), 1)
    sub16 = lax.broadcasted_iota(jnp.int32, (MLSTM_GATES, CHUNK), 0)

    def conv(src, w_ref, part, cl):
        r0 = pl.multiple_of(cl * CHUNK, CHUNK)
        n_rows = n_loc[part] * CHUNK
        x = src[pl.ds(r0, CHUNK), :]
        pr = src[pl.ds(jnp.maximum(r0 - 1, 0), 1), :]
        nx = src[pl.ds(jnp.minimum(r0 + CHUNK, n_rows - 1), 1), :]
        pr = jnp.where(cl != 0, pr, 0.0)
        nx = jnp.where(cl != n_loc[part] - 1, nx, 0.0)
        xp = jnp.where(row == 0, pr, pltpu.roll(x, 1, 0))
        xn = jnp.where(row == CHUNK - 1, nx, pltpu.roll(x, CHUNK - 1, 0))
        w = w_ref[...]
        return _silu(w[0:1] * xp + w[1:2] * x + w[2:3] * xn)

    def pick_col(g, j):
        return jnp.sum(jnp.where(lane16 == j, g, 0.0), axis=1, keepdims=True)

    def pick_row(g, j):
        return jnp.sum(jnp.where(sub16 == j, g, 0.0), axis=0, keepdims=True)

    def phase_a(part, cl, cg):
        rl, rg = _rows(cl), _rows(cg)
        q = conv(qr[part], wq_ref, part, cl)
        k = conv(kr[part], wk_ref, part, cl) * scale
        qb = _bf(q)
        q_s[rg, :] = qb
        vb = jnp.concatenate([_bf(vr[part][rl, :]), ones_col], axis=1)
        s = _dot_nt(qb, _bf(k))
        g_c = gcr[part][rl, :] + bc_ref[...]
        g_r = gr_s[cg]
        kws = []
        for d in range(2):
            ig_c = pick_col(g_c, d * nh + h)
            lf_c = _log_sigmoid(pick_col(g_c, 2 * nh + d * nh + h))
            ig_r = pick_row(g_r, d * nh + h)
            lf_r = _log_sigmoid(pick_row(g_r, 2 * nh + d * nh + h))
            tri = (s_i <= t_i) if d == 0 else (s_i >= t_i)
            tri_t = (t_i <= s_i) if d == 0 else (t_i >= s_i)
            b_c = jnp.sum(jnp.where(tri, lf_r, 0.0), axis=1, keepdims=True)
            b_r = jnp.sum(jnp.where(tri_t, lf_c, 0.0), axis=0, keepdims=True)
            dmat = jnp.where(tri, b_c - b_r + ig_r, -jnp.inf)
            rho = jnp.max(dmat, axis=1, keepdims=True)
            intra_s[d, rg, :] = _dot(_bf(s * jnp.exp(dmat - rho)), vb)
            b_last = b_c[CHUNK - 1:CHUNK] if d == 0 else b_c[0:1]
            mu = jnp.max(b_last - b_r + ig_r, axis=1, keepdims=True)
            kws.append(_bf(k * jnp.exp(b_last - b_c + ig_c - mu)))
            rho_s[d, rg, :] = jnp.broadcast_to(rho, (CHUNK, 128))
            bcol_s[d, rg, :] = jnp.broadcast_to(b_c, (CHUNK, 128))
            bl_s[cg, :, _lanes(d, 128)] = jnp.broadcast_to(b_last, (1, 128))
            mu_s[cg, :, _lanes(d, 128)] = jnp.broadcast_to(mu, (1, 128))
        kv_s[cg] = _dot_tn(jnp.concatenate(kws, axis=1), vb)

    _for_chunks(n_ctx, n_lat, phase_a, unroll_a)
    st_s[...] = jnp.zeros(st_s.shape, F32)
    m_s[...] = jnp.zeros(m_s.shape, F32)

    def wide(v):
        return jnp.concatenate([v, v], axis=1)

    def phase_b(i, carry):
        for d, c in ((0, i), (1, _bwd_chunk(i, n_ctx, n_lat))):
            ln = _lanes(d, 128)
            m_prev = m_s[d]
            st = st_s[d]
            mprev_s[c, :, ln] = m_prev
            call_s[c, :, _lanes(d, ext)] = _bf(st)
            bl = bl_s[c, :, ln]
            mu = mu_s[c, :, ln]
            m_new = jnp.maximum(bl + m_prev, mu)
            st_s[d] = (wide(jnp.exp(bl + m_prev - m_new)) * st
                       + wide(jnp.exp(mu - m_new)) * kv_s[c, _lanes(d, dk), :])
            m_s[d] = m_new
        return carry

    lax.fori_loop(0, nch, phase_b, 0)
    gn = gn_ref[...]

    def phase_c(part, cl, cg):
        rl, rg = _rows(cl), _rows(cg)
        qc_all = _dot(q_s[rg, :], call_s[cg])
        o = None
        for d in range(2):
            rho = rho_s[d, rg, :]
            b_c = bcol_s[d, rg, :]
            m_prev = mprev_s[cg, :, _lanes(d, 128)]
            m_t = jnp.maximum(rho, b_c + m_prev)
            nd = (wide(jnp.exp(rho - m_t)) * intra_s[d, rg, :]
                  + wide(jnp.exp(b_c + m_prev - m_t)) * qc_all[:, _lanes(d, ext)])
            hh = nd[:, 0:dv] / jnp.maximum(jnp.abs(nd[:, dv:]), jnp.exp(-m_t))
            o = hh if o is None else o + hh
        y = o - jnp.mean(o, axis=-1, keepdims=True)
        y = y * lax.rsqrt(jnp.mean(y * y, axis=-1, keepdims=True) + NORM_EPS)
        outr[part][rl, :] = (y * gn * _sigmoid(ogr[part][rl, :])).astype(BF16)

    _for_chunks(n_ctx, n_lat, phase_c, unroll_c, with_ctx=oc_ref is not None)


def _mlstm(p, g_col, g_row_lat, g_row_ctx, nb, lat_len, ctx_len, gate_b, conv_w, gn, emit_ctx):
    t_len = lat_len + ctx_len
    nch = t_len // CHUNK
    n_lat_blk = nb * lat_len // ctx_len
    ng = MLSTM_GATES
    dk, dv = MLSTM_DK, MLSTM_DV
    spec = functools.partial(_seq_specs, width=128, nb=nb, lat_len=lat_len, ctx_len=ctx_len)
    in_specs = [*spec(_C_ML_Q), *spec(_C_ML_K), *spec(_C_ML_V), *spec(_C_ML_O),
                pl.BlockSpec((lat_len, ng), lambda b, h: (b, 0)),
                pl.BlockSpec((ctx_len, ng), lambda b, h: (n_lat_blk + b, 0)),
                pl.BlockSpec((1, ng, lat_len), lambda b, h: (b, 0, 0)),
                pl.BlockSpec((1, ng, ctx_len), lambda b, h: (b, 0, 0)),
                pl.BlockSpec((1, ng), lambda b, h: (0, 0)),
                pl.BlockSpec((ng, 1), lambda b, h: (0, 0)),
                pl.BlockSpec((3, 128), lambda b, h: (0, h)),
                pl.BlockSpec((3, 128), lambda b, h: (0, MLSTM_HEADS + h)),
                pl.BlockSpec((1, 128), lambda b, h: (0, h))]
    scratch = [pltpu.VMEM((t_len, dk), BF16),
               pltpu.VMEM((nch, ng, CHUNK), F32),
               pltpu.VMEM((2, t_len, 2 * dv), F32),
               pltpu.VMEM((nch, 2 * dk, 2 * dv), F32),
               pltpu.VMEM((nch, dk, 4 * dv), BF16),
               pltpu.VMEM((2, t_len, 128), F32), pltpu.VMEM((2, t_len, 128), F32),
               pltpu.VMEM((nch, 1, 256), F32), pltpu.VMEM((nch, 1, 256), F32), pltpu.VMEM((nch, 1, 256), F32),
               pltpu.VMEM((2, dk, 2 * dv), F32), pltpu.VMEM((2, 1, 128), F32)]
    args = (p, p, p, p, p, p, p, p, g_col, g_col, g_row_lat, g_row_ctx,
            gate_b.reshape(1, ng), gate_b.reshape(ng, 1), conv_w, conv_w, gn.reshape(1, MLSTM_WIDTH))
    return _mixer_call(_mlstm_kernel, "mlstm", in_specs, args, nb, MLSTM_HEADS, lat_len, ctx_len, dv,
                       scratch, emit_ctx, 48)


def _wout_kernel(*refs, n_lat_tiles, n_mix_parts, n_x_parts, emit_next):
    it = iter(refs)
    mix = [[next(it) for _ in range(n_mix_parts)] for _ in range(3)]
    x_parts = [next(it) for _ in range(n_x_parts)]
    wr_ref, wh_ref, wm_ref, g_ref, mod_ref = (next(it) for _ in range(5))
    g2_ref = next(it) if emit_next else None
    o_ref = next(it)
    h2_ref = next(it) if emit_next else None

    def run(part):
        pm = min(part, n_mix_parts - 1)
        y = (_dot(mix[0][pm][...], wr_ref[...]) + _dot(mix[1][pm][...], wh_ref[...])
             + _dot(mix[2][pm][...], wm_ref[...]))
        o_ref[...] = y
        m = mod_ref[0]
        x_ref = x_parts[min(part, n_x_parts - 1)]
        g = g_ref[...]
        g2 = g2_ref[...] if emit_next else None

        def chunk(rows):
            xn = x_ref[rows, :] + m[2:3] * _rms(o_ref[rows, :], g)
            o_ref[rows, :] = xn
            if emit_next:
                h2_ref[rows, :] = (_rms(xn, g2) * (1.0 + m[4:5]) + m[3:4]).astype(h2_ref.dtype)

        _row_chunks(o_ref.shape[0], chunk)

    _on_part(n_lat_tiles, max(n_mix_parts, n_x_parts), run)


def _wout(mix_parts, w_out, x_parts, n_rows, g1, modtab, lat_rows, g2_next):
    d = x_parts[0].shape[1]
    nb = modtab.shape[0] - 1
    mi = _mod_index(ROW_TILE, lat_rows, nb)
    tm = ROW_TILE
    n_lat_tiles = mix_parts[0][0].shape[0] // tm
    emit_next = g2_next is not None
    widths = (RET_WIDTH, HGRN_WIDTH, MLSTM_WIDTH)
    w_r = w_out[0:RET_WIDTH]
    w_h = w_out[RET_WIDTH:RET_WIDTH + HGRN_WIDTH]
    w_m = w_out[RET_WIDTH + HGRN_WIDTH:]
    full = lambda a: pl.BlockSpec(a.shape, lambda i: (0, 0))
    in_specs, args = [], []
    for parts, w in zip(mix_parts, widths):
        in_specs += _part_specs(parts, tm, w)
        args += list(parts)
    in_specs += _part_specs(x_parts, tm, d)
    args += list(x_parts)
    in_specs += [full(w_r), full(w_h), full(w_m),
                 pl.BlockSpec((1, d), lambda i: (0, 0)),
                 pl.BlockSpec((1, 6, d), lambda i: (mi(i), 0, 0))]
    args += [w_r, w_h, w_m, g1.reshape(1, d), modtab]
    out_specs = [pl.BlockSpec((tm, d), lambda i: (i, 0))]
    out_shape = [jax.ShapeDtypeStruct((n_rows, d), F32)]
    if emit_next:
        in_specs.append(pl.BlockSpec((1, d), lambda i: (0, 0)))
        args.append(g2_next.reshape(1, d))
        out_specs.append(pl.BlockSpec((tm, d), lambda i: (i, 0)))
        out_shape.append(jax.ShapeDtypeStruct((n_rows, d), BF16))
    return pl.pallas_call(
        functools.partial(_wout_kernel, n_lat_tiles=n_lat_tiles, n_mix_parts=len(mix_parts[0]),
                          n_x_parts=len(x_parts), emit_next=emit_next),
        grid=(n_rows // tm,),
        in_specs=in_specs,
        out_specs=out_specs,
        out_shape=out_shape,
        compiler_params=_cparams(("arbitrary",), 52),
        name="wout",
    )(*args)


def _ffn_kernel(*refs, emit_next):
    if emit_next:
        h_ref, w1_ref, w3_ref, w2_ref, x_ref, g_ref, mod_ref, gn_ref, modn_ref, o_ref, hn_ref, acc_ref = refs
    else:
        h_ref, w1_ref, w3_ref, w2_ref, x_ref, g_ref, mod_ref, o_ref, acc_ref = refs
    f = pl.program_id(1)

    @pl.when(f == 0)
    def _():
        acc_ref[...] = jnp.zeros(acc_ref.shape, F32)

    h = h_ref[...]
    u = _silu(_dot(h, w1_ref[...])) * _dot(h, w3_ref[...])
    acc_ref[...] += _dot(_bf(u), w2_ref[...])

    @pl.when(f == pl.num_programs(1) - 1)
    def _():
        m = mod_ref[0]
        g = g_ref[...]
        gn = gn_ref[...] if emit_next else None
        mn = modn_ref[0] if emit_next else None

        def chunk(rows):
            xn = x_ref[rows, :] + m[5:6] * _rms(acc_ref[rows, :], g)
            o_ref[rows, :] = xn
            if emit_next:
                hn_ref[rows, :] = (_rms(xn, gn) * (1.0 + mn[1:2]) + mn[0:1]).astype(hn_ref.dtype)

        _row_chunks(o_ref.shape[0], chunk)


def _ffn(hb, w1, w3, w2, xs, g3, modtab, lat_rows, g_next, modtab_next):
    n_rows, d = hb.shape
    dff = w1.shape[1]
    nb = modtab.shape[0] - 1
    tm, tf = ROW_TILE, FFN_TF
    mi = _mod_index(tm, lat_rows, nb)
    emit_next = g_next is not None
    nf = dff // tf

    def fs(i, f):
        return jnp.where(i % 2 == 1, nf - 1 - f, f)

    in_specs = [pl.BlockSpec((tm, d), lambda i, f: (i, 0)),
                pl.BlockSpec((d, tf), lambda i, f: (0, fs(i, f))),
                pl.BlockSpec((d, tf), lambda i, f: (0, fs(i, f))),
                pl.BlockSpec((tf, d), lambda i, f: (fs(i, f), 0)),
                pl.BlockSpec((tm, d), lambda i, f: (i, 0)),
                pl.BlockSpec((1, d), lambda i, f: (0, 0)),
                pl.BlockSpec((1, 6, d), lambda i, f: (mi(i), 0, 0))]
    args = [hb, w1, w3, w2, xs, g3.reshape(1, d), modtab]
    out_specs = [pl.BlockSpec((tm, d), lambda i, f: (i, 0))]
    out_shape = [jax.ShapeDtypeStruct((n_rows, d), F32)]
    if emit_next:
        in_specs += [pl.BlockSpec((1, d), lambda i, f: (0, 0)),
                     pl.BlockSpec((1, 6, d), lambda i, f: (mi(i), 0, 0))]
        args += [g_next.reshape(1, d), modtab_next]
        out_specs.append(pl.BlockSpec((tm, d), lambda i, f: (i, 0)))
        out_shape.append(jax.ShapeDtypeStruct((n_rows, d), BF16))
    return pl.pallas_call(
        functools.partial(_ffn_kernel, emit_next=emit_next),
        grid=(n_rows // tm, dff // tf),
        in_specs=in_specs,
        out_specs=out_specs,
        out_shape=out_shape,
        scratch_shapes=[pltpu.VMEM((tm, d), F32)],
        compiler_params=_cparams(("arbitrary", "arbitrary"), 48),
        name="ffn",
    )(*args)


_META_E0, _META_E1, _META_R0, _META_R1, _META_G0, _META_G1 = range(6)


def _router_kernel(x_ref, g_ref, mod_ref, wr_ref, tri_ref, meta_ref, cnt_ref, zero_ref, carry_ref):
    i = pl.program_id(0)
    zero_ref[...] = jnp.zeros(zero_ref.shape, F32)

    @pl.when(i == 0)
    def _():
        carry_ref[...] = jnp.zeros(carry_ref.shape, F32)

    m = mod_ref[0]
    hmod = _rms(x_ref[...], g_ref[...]) * (1.0 + m[4:5]) + m[3:4]
    h_hi = _bf(hmod)
    h_lo = _bf(hmod - h_hi.astype(F32))
    w = wr_ref[...]
    w_hi = _bf(w)
    w_lo = _bf(w - w_hi.astype(F32))
    logits = _dot(h_hi, w_hi) + _dot(h_lo, w_hi) + _dot(h_hi, w_lo)
    lane = lax.broadcasted_iota(jnp.int32, logits.shape, 1)
    lanef = lane.astype(F32)
    logits = jnp.where(lane < N_EXPERTS, logits, -jnp.inf)
    v0 = jnp.max(logits, axis=1, keepdims=True)
    e0 = jnp.min(jnp.where(logits == v0, lanef, 1e9), axis=1, keepdims=True)
    rest = jnp.where(lanef == e0, -jnp.inf, logits)
    v1 = jnp.max(rest, axis=1, keepdims=True)
    e1 = jnp.min(jnp.where(rest == v1, lanef, 1e9), axis=1, keepdims=True)
    ex = jnp.exp(v1 - v0)
    g0 = 1.0 / (1.0 + ex)
    g1 = ex / (1.0 + ex)
    oh0 = lanef == e0
    oh1 = lanef == e1
    oh = jnp.where(jnp.logical_or(oh0, oh1), 1.0, 0.0)
    before = _dot(tri_ref[...], _bf(oh)) + carry_ref[0:1, :]
    r0 = jnp.sum(jnp.where(oh0, before, 0.0), axis=1, keepdims=True)
    r1 = jnp.sum(jnp.where(oh1, before, 0.0), axis=1, keepdims=True)
    carry_ref[0:1, :] = carry_ref[0:1, :] + jnp.sum(oh, axis=0, keepdims=True)
    meta = jnp.zeros(logits.shape, F32)
    for j, val in ((_META_E0, e0), (_META_E1, e1), (_META_R0, r0), (_META_R1, r1),
                   (_META_G0, g0), (_META_G1, g1)):
        meta = jnp.where(lane == j, val, meta)
    meta_ref[...] = meta
    cnt_ref[...] = carry_ref[...]


def _router(xs, n_rows, g2, modtab, w_router, lat_rows, n_slots):
    d = xs.shape[1]
    nb = modtab.shape[0] - 1
    tm = ROW_TILE
    mi = _mod_index(tm, lat_rows, nb)
    n_steps = n_rows // tm
    zrows = n_slots // n_steps
    assert zrows * n_steps == n_slots and zrows % 8 == 0, (n_slots, n_steps)
    wr = jnp.zeros((d, 128), F32).at[:, :N_EXPERTS].set(w_router)
    tri = jnp.asarray(np.tril(np.ones((tm, tm), np.float32), -1), BF16)
    return pl.pallas_call(
        _router_kernel,
        grid=(n_steps,),
        in_specs=[pl.BlockSpec((tm, d), lambda i: (i, 0)),
                  pl.BlockSpec((1, d), lambda i: (0, 0)),
                  pl.BlockSpec((1, 6, d), lambda i: (mi(i), 0, 0)),
                  pl.BlockSpec((d, 128), lambda i: (0, 0)),
                  pl.BlockSpec((tm, tm), lambda i: (0, 0))],
        out_specs=[pl.BlockSpec((tm, 128), lambda i: (i, 0)),
                   pl.BlockSpec((8, 128), lambda i: (0, 0)),
                   pl.BlockSpec((zrows, d), lambda i: (i, 0))],
        out_shape=[jax.ShapeDtypeStruct((n_rows, 128), F32),
                   jax.ShapeDtypeStruct((8, 128), F32),
                   jax.ShapeDtypeStruct((n_slots, d), F32)],
        scratch_shapes=[pltpu.VMEM((8, 128), F32)],
        compiler_params=_cparams(("arbitrary",), 48),
        name="router",
    )(xs, g2.reshape(1, d), modtab, wr, tri)


def _dispatch_kernel(dest_ref, x_ref, g_ref, mod_ref, init_ref, o_ref, h_s, sem):
    del init_ref
    i = pl.program_id(0)
    tm = h_s.shape[0]
    m = mod_ref[0]
    h_s[...] = _rms(x_ref[...], g_ref[...]) * (1.0 + m[4:5]) + m[3:4]

    def row_copy(r, k):
        dst = dest_ref[TOP_K * (i * tm + r) + k]
        return pltpu.make_async_copy(h_s.at[pl.ds(r, 1)], o_ref.at[pl.ds(dst, 1)], sem)

    def start(r, carry):
        for k in range(TOP_K):
            row_copy(r, k).start(priority=k)
        return carry

    lax.fori_loop(0, tm, start, 0, unroll=DMA_UNROLL)

    def wait(r, carry):
        for k in range(TOP_K):
            row_copy(r, k).wait()
        return carry

    lax.fori_loop(0, tm, wait, 0, unroll=DMA_UNROLL)


def _dispatch(dest, xs, n_rows, g2, modtab, init, lat_rows):
    d = xs.shape[1]
    n_slots = init.shape[0]
    nb = modtab.shape[0] - 1
    tm = ROW_TILE
    mi = _mod_index(tm, lat_rows, nb)
    grid_spec = pltpu.PrefetchScalarGridSpec(
        num_scalar_prefetch=1,
        grid=(n_rows // tm,),
        in_specs=[pl.BlockSpec((tm, d), lambda i, dst: (i, 0)),
                  pl.BlockSpec((1, d), lambda i, dst: (0, 0)),
                  pl.BlockSpec((1, 6, d), lambda i, dst: (mi(i), 0, 0)),
                  pl.BlockSpec(memory_space=pl.ANY)],
        out_specs=pl.BlockSpec(memory_space=pl.ANY),
        scratch_shapes=[pltpu.VMEM((tm, d), F32), pltpu.SemaphoreType.DMA(())],
    )
    return pl.pallas_call(
        _dispatch_kernel,
        grid_spec=grid_spec,
        out_shape=jax.ShapeDtypeStruct((n_slots, d), F32),
        input_output_aliases={4: 0},
        compiler_params=_cparams(("arbitrary",), 32),
        name="dispatch",
    )(dest, xs, g2.reshape(1, d), modtab, init)


def _expert_kernel(ge_ref, nt_ref, ng_ref, x_ref, *refs):
    w1_refs, w3_refs = refs[:MOE_WSPLIT], refs[MOE_WSPLIT:2 * MOE_WSPLIT]
    w2_ref, o_ref, xb_s = refs[2 * MOE_WSPLIT:]
    s = pl.program_id(0)
    f = pl.program_id(1)
    tm = MOE_TM

    @pl.when(s < ng_ref[0])
    def _():
        @pl.when(f == 0)
        def _():
            xb_s[...] = _bf(x_ref[...])
            o_ref[...] = jnp.zeros(o_ref.shape, F32)

        w1b = jnp.concatenate([_bf(r[0]) for r in w1_refs], axis=0)
        w3b = jnp.concatenate([_bf(r[0]) for r in w3_refs], axis=0)
        w2b = _bf(w2_ref[0])

        def tile(t):
            rows = pl.ds(t * tm, tm)
            h = xb_s[rows, :]
            u = _silu(_dot(h, w1b)) * _dot(h, w3b)
            o_ref[rows, :] += _dot(_bf(u), w2b)

        tile(0)
        for t in range(1, MOE_GROUP):
            pl.when(nt_ref[s] > t)(functools.partial(tile, t))

    @pl.when(jnp.logical_and(s >= ng_ref[0], f == pl.num_programs(1) - 1))
    def _():
        o_ref[...] = jnp.zeros(o_ref.shape, F32)


def _experts(group_e, group_tiles, n_groups, xsort, w1, w3, w2):
    n_slots, d = xsort.shape
    dff = w1.shape[2]
    tg, tf = MOE_TM * MOE_GROUP, MOE_TF
    nf = dff // tf

    def ss(s, ng):
        return jnp.maximum(jnp.minimum(s, ng[0] - 1), 0)

    def ff(s, f, ng):
        snake = jnp.where(s % 2 == 1, nf - 1 - f, f)
        last = jnp.where((ng[0] - 1) % 2 == 1, 0, nf - 1)
        return jnp.where(s < ng[0], snake, last)

    grid_spec = pltpu.PrefetchScalarGridSpec(
        num_scalar_prefetch=3,
        grid=(n_slots // tg, nf),
        in_specs=[pl.BlockSpec((tg, d), lambda s, f, ge, nt, ng: (ss(s, ng), 0)),
                  *[pl.BlockSpec((1, d // MOE_WSPLIT, tf),
                                 lambda s, f, ge, nt, ng, part=part: (ge[ss(s, ng)], part, ff(s, f, ng)))
                    for _ in range(2) for part in range(MOE_WSPLIT)],
                  pl.BlockSpec((1, tf, d), lambda s, f, ge, nt, ng: (ge[ss(s, ng)], ff(s, f, ng), 0))],
        out_specs=pl.BlockSpec((tg, d), lambda s, f, ge, nt, ng: (s, 0)),
        scratch_shapes=[pltpu.VMEM((tg, d), BF16)],
    )
    return pl.pallas_call(
        _expert_kernel,
        grid_spec=grid_spec,
        out_shape=jax.ShapeDtypeStruct((n_slots, d), F32),
        compiler_params=_cparams(("arbitrary", "arbitrary"), 56),
        name="experts",
    )(group_e, group_tiles, n_groups, xsort, *([w1] * MOE_WSPLIT), *([w3] * MOE_WSPLIT), w2)


def _combine_kernel(dest_ref, y_ref, meta_ref, x_ref, g_ref, mod_ref, o_ref, buf_s, sem):
    i = pl.program_id(0)
    tm = x_ref.shape[0]

    def row_copy(r, k):
        src = dest_ref[TOP_K * (i * tm + r) + k]
        return pltpu.make_async_copy(y_ref.at[pl.ds(src, 1)], buf_s.at[k, pl.ds(r, 1)], sem)

    def start(r, carry):
        for k in range(TOP_K):
            row_copy(r, k).start(priority=k)
        return carry

    lax.fori_loop(0, tm, start, 0, unroll=DMA_UNROLL)

    def wait(r, carry):
        for k in range(TOP_K):
            row_copy(r, k).wait()
        return carry

    lax.fori_loop(0, tm, wait, 0, unroll=DMA_UNROLL)

    m = mod_ref[0]
    g = g_ref[...]
    def chunk(rows):
        meta = meta_ref[rows, :]
        lane = lax.broadcasted_iota(jnp.int32, meta.shape, 1)
        g0 = jnp.sum(jnp.where(lane == _META_G0, meta, 0.0), axis=1, keepdims=True)
        g1 = jnp.sum(jnp.where(lane == _META_G1, meta, 0.0), axis=1, keepdims=True)
        y = buf_s[0, rows, :] * g0 + buf_s[1, rows, :] * g1
        o_ref[rows, :] = x_ref[rows, :] + m[5:6] * _rms(y, g)

    _row_chunks(tm, chunk)


def _combine(dest, yb, meta, xs, n_rows, g3, modtab, lat_rows):
    d = xs.shape[1]
    nb = modtab.shape[0] - 1
    tm = ROW_TILE
    mi = _mod_index(tm, lat_rows, nb)
    grid_spec = pltpu.PrefetchScalarGridSpec(
        num_scalar_prefetch=1,
        grid=(n_rows // tm,),
        in_specs=[pl.BlockSpec(memory_space=pl.ANY),
                  pl.BlockSpec((tm, 128), lambda i, dst: (i, 0)),
                  pl.BlockSpec((tm, d), lambda i, dst: (i, 0)),
                  pl.BlockSpec((1, d), lambda i, dst: (0, 0)),
                  pl.BlockSpec((1, 6, d), lambda i, dst: (mi(i), 0, 0))],
        out_specs=pl.BlockSpec((tm, d), lambda i, dst: (i, 0)),
        scratch_shapes=[pltpu.VMEM((TOP_K, tm, d), F32), pltpu.SemaphoreType.DMA(())],
    )
    return pl.pallas_call(
        _combine_kernel,
        grid_spec=grid_spec,
        out_shape=jax.ShapeDtypeStruct((n_rows, d), F32),
        compiler_params=_cparams(("arbitrary",), 40),
        name="combine",
    )(dest, yb, meta, xs, g3.reshape(1, d), modtab)


def _moe(xs, n_rows, g2, g3, modtab, w_router, w1, w3, w2, lat_rows):
    tg = MOE_TM * MOE_GROUP
    n_groups_max = (n_rows * TOP_K) // tg + N_EXPERTS
    n_slots = n_groups_max * tg
    meta, cnt, slots_init = _router(xs, n_rows, g2, modtab, w_router, lat_rows, n_slots)
    counts = cnt[0, :N_EXPERTS].astype(jnp.int32)
    padded = (counts + tg - 1) // tg * tg
    pad_end = jnp.cumsum(padded)
    pad_start = pad_end - padded
    e = meta[:, _META_E0:_META_E1 + 1].astype(jnp.int32)
    r = meta[:, _META_R0:_META_R1 + 1].astype(jnp.int32)
    dest = (pad_start[e] + r).reshape(-1)
    group_start = jnp.arange(n_groups_max, dtype=jnp.int32) * tg
    group_e = jnp.minimum(jnp.sum((group_start[:, None] >= pad_end[None, :]).astype(jnp.int32), axis=1),
                          N_EXPERTS - 1)
    filled = counts[group_e] - (group_start - pad_start[group_e])
    group_tiles = jnp.clip((filled + MOE_TM - 1) // MOE_TM, 1, MOE_GROUP).astype(jnp.int32)
    n_groups = (pad_end[-1:] // tg).astype(jnp.int32)
    xsort = _dispatch(dest, xs, n_rows, g2, modtab, slots_init, lat_rows)
    yb = _experts(group_e, group_tiles, n_groups, xsort, w1, w3, w2)
    return _combine(dest, yb, meta, xs, n_rows, g3, modtab, lat_rows)


def _rotary_tables(lat_len, ctx_len):
    rows = lat_len // GRID_W
    row = jnp.repeat(jnp.arange(rows, dtype=F32), GRID_W)
    col = jnp.tile(jnp.arange(GRID_W, dtype=F32), rows)
    n_freq = RET_DK // 4
    inv = ROPE_BASE ** (-jnp.arange(n_freq, dtype=F32) / n_freq)
    ang = jnp.concatenate([row[:, None] * inv, col[:, None] * inv], axis=-1)
    cos, sin = jnp.cos(ang), jnp.sin(ang)
    cosf = jnp.concatenate([jnp.ones((ctx_len, RET_DK), F32), jnp.concatenate([cos, cos], -1)], 0)
    sinf = jnp.concatenate([jnp.zeros((ctx_len, RET_DK), F32), jnp.concatenate([-sin, sin], -1)], 0)
    return cosf, sinf


def kernel(x, c, ctx, c_ctx, w_ada, b_ada, norm_g, w_in, w_out, ret_decay, ret_gn, hgrn_lb, hgrn_gn,
           mlstm_conv, mlstm_gate_b, mlstm_gn, w_ffn1, w_ffn3, w_ffn2, w_router, w_exp1, w_exp3, w_exp2):
    nb, lat_len, d = x.shape
    ctx_len = ctx.shape[1]
    depth = w_ada.shape[0]
    n_lat_rows = nb * lat_len
    n_rows = n_lat_rows + nb * ctx_len

    x_parts = [x.reshape(n_lat_rows, d), ctx.reshape(nb * ctx_len, d)]
    w_in_t = jnp.swapaxes(w_in, 1, 2)
    s_in = jnp.zeros((8, d), F32).at[:nb].set(c).at[nb].set(c_ctx)
    mod_all = _ada(s_in, w_ada, b_ada)[:, :nb + 1].reshape(depth, nb + 1, 6, d)

    cosf, sinf = _rotary_tables(lat_len, ctx_len)
    hg_cum, hg_masks, hg_signs = _hgrn_constants()
    hg_consts = (jnp.asarray(hg_cum, BF16), jnp.asarray(hg_masks, F32), jnp.asarray(hg_signs, F32))
    sm = jax.nn.softmax(hgrn_lb.astype(F32), axis=0)
    lb_all = jnp.clip(jnp.cumsum(sm, axis=0) - sm[0], 0.0, 1.0)

    hb = None
    for layer in range(depth):
        last = layer == depth - 1
        modtab = mod_all[layer]
        g = norm_g[layer]

        if hb is None:
            hb = _prenorm(x_parts, n_rows, g[0], modtab, 1, 0, lat_len, BF16)
        p = _mm(hb, w_in_t, layer, MM_TM, MM_TN, n_cols=PROJ_MAIN)
        w_g = jnp.zeros((1, 128, d), F32).at[0, :MLSTM_GATES].set(w_in_t[layer, PROJ_MAIN:])
        gates = _mm(hb, w_g, 0, MM_TM, 128)[:, :MLSTM_GATES]
        hb = None
        g_row_lat = jnp.swapaxes(gates[:n_lat_rows].reshape(nb, lat_len, MLSTM_GATES), 1, 2)
        g_row_ctx = jnp.swapaxes(gates[n_lat_rows:].reshape(nb, ctx_len, MLSTM_GATES), 1, 2)

        log_g = jax.nn.log_sigmoid(ret_decay[layer].astype(F32))
        lg_b = jnp.broadcast_to(log_g[:, :, None, None], (2, RET_HEADS, 8, RET_DV))
        lb = lb_all[layer]
        emit_ctx = not last
        o_ret = _retention(p, nb, lat_len, ctx_len, cosf, sinf, lg_b, ret_gn[layer], emit_ctx)
        o_hg = _hgrn(p, nb, lat_len, ctx_len, jnp.log(lb), jnp.log1p(-lb), hgrn_gn[layer],
                     hg_consts, emit_ctx)
        o_ml = _mlstm(p, gates, g_row_lat, g_row_ctx, nb, lat_len, ctx_len, mlstm_gate_b[layer],
                      mlstm_conv[layer], mlstm_gn[layer], emit_ctx)
        mix_parts = [list(o) for o in (o_ret, o_hg, o_ml)]
        rows_now = n_lat_rows if last else n_rows
        dense = layer % 2 == 0
        res = _wout(mix_parts, _bf(w_out[layer]), x_parts, rows_now, g[1], modtab, lat_len,
                    g[2] if dense else None)
        xs = res[0]
        x_parts = [xs]

        j = layer // 2
        if dense:
            nxt = (None, None) if last else (norm_g[layer + 1][0], mod_all[layer + 1])
            res = _ffn(res[1], _bf(w_ffn1[j]), _bf(w_ffn3[j]), _bf(w_ffn2[j]), xs, g[3], modtab, lat_len, *nxt)
            xs = res[0]
            hb = None if last else res[1]
        else:
            xs = _moe(xs, rows_now, g[2], g[3], modtab, w_router[j], w_exp1[j], w_exp3[j], w_exp2[j], lat_len)
        x_parts = [xs]
    return xs[:n_lat_rows].reshape(nb, lat_len, d)
```

```python
import functools

import numpy as np
import jax
import jax.numpy as jnp
from jax import lax
from jax.experimental import pallas as pl
from jax.experimental.pallas import tpu as pltpu

F32 = jnp.float32
BF16 = jnp.bfloat16

CHUNK = 128
NORM_EPS = 1e-6
ROPE_BASE = 10000.0
GRID_W = 64

RET_HEADS, RET_DK, RET_DV = 4, 128, 256
HGRN_HEADS, HGRN_DK, HGRN_DV = 4, 128, 128
MLSTM_HEADS, MLSTM_DK, MLSTM_DV = 4, 128, 128
N_EXPERTS = 8
TOP_K = 2

RET_QK = RET_HEADS * RET_DK
RET_WIDTH = RET_HEADS * RET_DV
HGRN_QK = HGRN_HEADS * HGRN_DK
HGRN_WIDTH = HGRN_HEADS * HGRN_DV
MLSTM_QK = MLSTM_HEADS * MLSTM_DK
MLSTM_WIDTH = MLSTM_HEADS * MLSTM_DV
MLSTM_GATES = 4 * MLSTM_HEADS

_C_RET_Q = 0
_C_RET_K = _C_RET_Q + RET_QK // 128
_C_RET_V = _C_RET_K + RET_QK // 128
_C_RET_G = _C_RET_V + RET_WIDTH // 128
_C_HG_Q = _C_RET_G + RET_WIDTH // 128
_C_HG_FF = _C_HG_Q + HGRN_QK // 128
_C_HG_FB = _C_HG_FF + HGRN_QK // 128
_C_HG_I = _C_HG_FB + HGRN_QK // 128
_C_HG_G = _C_HG_I + HGRN_WIDTH // 128
_C_ML_Q = _C_HG_G + HGRN_WIDTH // 128
_C_ML_K = _C_ML_Q + MLSTM_QK // 128
_C_ML_V = _C_ML_K + MLSTM_QK // 128
_C_ML_O = _C_ML_V + MLSTM_WIDTH // 128
PROJ_MAIN = (_C_ML_O + MLSTM_WIDTH // 128) * 128

_VMEM_CAP_BYTES = 56 * 1024 * 1024

ROW_TILE = 512
MM_TM, MM_TN = 1024, 1280
FFN_TF = 512
MOE_TM = 768
MOE_WSPLIT = 1
MOE_GROUP = 1
MOE_TF = 512
RET_UNROLL = (8, 8)
HGRN_UNROLL = (4, 8)
MLSTM_UNROLL = (2, 8)
DMA_UNROLL = 8
EPI_ROWS = None


def _cparams(sem, vmem_mb):
    return pltpu.CompilerParams(dimension_semantics=sem,
                                vmem_limit_bytes=min(int(vmem_mb * 1024 * 1024), _VMEM_CAP_BYTES))


def _bf(x):
    return x.astype(BF16)


def _dot(a, b):
    return jnp.dot(a, b, preferred_element_type=F32)


def _dot_nt(a, b):
    return lax.dot_general(a, b, (((1,), (1,)), ((), ())), preferred_element_type=F32)


def _dot_tn(a, b):
    return lax.dot_general(a, b, (((0,), (0,)), ((), ())), preferred_element_type=F32)


def _sigmoid(x):
    return 1.0 / (1.0 + jnp.exp(-x))


def _silu(x):
    return x * _sigmoid(x)


def _log_sigmoid(x):
    return jnp.minimum(x, 0.0) - jnp.log(1.0 + jnp.exp(-jnp.abs(x)))


def _rms(x, g):
    return x * lax.rsqrt(jnp.mean(x * x, axis=-1, keepdims=True) + NORM_EPS) * g


def _mod_index(tile_rows, n_lat_rows_per_batch, n_batch):
    return lambda i: jnp.minimum((i * tile_rows) // n_lat_rows_per_batch, n_batch)


def _ada_kernel(s_ref, w_ref, b_ref, o_ref):
    s = _bf(_silu(s_ref[...]))
    o_ref[0] = _dot(s, _bf(w_ref[0])) + b_ref[0]


def _ada(s_in, w_ada, b_ada):
    depth, d, n = w_ada.shape
    tn = 1024
    return pl.pallas_call(
        _ada_kernel,
        grid=(depth, n // tn),
        in_specs=[pl.BlockSpec((8, d), lambda l, j: (0, 0)),
                  pl.BlockSpec((1, d, tn), lambda l, j: (l, 0, j)),
                  pl.BlockSpec((1, 1, tn), lambda l, j: (l, 0, j))],
        out_specs=pl.BlockSpec((1, 8, tn), lambda l, j: (l, 0, j)),
        out_shape=jax.ShapeDtypeStruct((depth, 8, n), F32),
        compiler_params=_cparams(("arbitrary", "arbitrary"), 40),
        name="ada",
    )(s_in, w_ada, b_ada.reshape(depth, 1, n))


def _row_chunks(n_rows, fn):
    if EPI_ROWS is None:
        fn(pl.ds(0, n_rows))
        return

    def body(i, carry):
        fn(pl.ds(pl.multiple_of(i * EPI_ROWS, EPI_ROWS), EPI_ROWS))
        return carry
    lax.fori_loop(0, n_rows // EPI_ROWS, body, 0, unroll=2)


def _part_specs(parts, tm, width):
    if len(parts) == 1:
        return [pl.BlockSpec((tm, width), lambda i: (i, 0))]
    n0 = parts[0].shape[0] // tm
    return [pl.BlockSpec((tm, width), lambda i: (jnp.minimum(i, n0 - 1), 0)),
            pl.BlockSpec((tm, width), lambda i: (jnp.maximum(i - n0, 0), 0))]


def _on_part(n_first, n_parts, fn):
    if n_parts == 1:
        fn(0)
    else:
        i = pl.program_id(0)
        pl.when(i < n_first)(lambda: fn(0))
        pl.when(i >= n_first)(lambda: fn(1))


def _prenorm_kernel(*refs, sc, sh, n_first):
    x_refs, (g_ref, mod_ref, o_ref) = refs[:-3], refs[-3:]

    def run(part):
        m = mod_ref[0]
        g = g_ref[...]

        def chunk(rows):
            y = _rms(x_refs[part][rows, :], g)
            o_ref[rows, :] = (y * (1.0 + m[sc:sc + 1]) + m[sh:sh + 1]).astype(o_ref.dtype)

        _row_chunks(o_ref.shape[0], chunk)

    _on_part(n_first, len(x_refs), run)


def _prenorm(x_parts, n_rows, g, modtab, sc, sh, lat_rows, out_dtype):
    d = x_parts[0].shape[1]
    nb = modtab.shape[0] - 1
    mi = _mod_index(ROW_TILE, lat_rows, nb)
    return pl.pallas_call(
        functools.partial(_prenorm_kernel, sc=sc, sh=sh, n_first=x_parts[0].shape[0] // ROW_TILE),
        grid=(n_rows // ROW_TILE,),
        in_specs=[*_part_specs(x_parts, ROW_TILE, d),
                  pl.BlockSpec((1, d), lambda i: (0, 0)),
                  pl.BlockSpec((1, 6, d), lambda i: (mi(i), 0, 0))],
        out_specs=pl.BlockSpec((ROW_TILE, d), lambda i: (i, 0)),
        out_shape=jax.ShapeDtypeStruct((n_rows, d), out_dtype),
        compiler_params=_cparams(("arbitrary",), 32),
        name="prenorm",
    )(*x_parts, g.reshape(1, d), modtab)


def _mm_kernel(x_ref, wt_ref, o_ref, wb_s):
    @pl.when(pl.program_id(1) == 0)
    def _():
        wb_s[...] = _bf(wt_ref[0])

    o_ref[...] = _dot_nt(x_ref[...], wb_s[...]).astype(o_ref.dtype)


def _mm(x, wt, layer, tm, tn, n_cols=None, out_dtype=F32):
    m, k = x.shape
    n = wt.shape[1] if n_cols is None else n_cols
    return pl.pallas_call(
        _mm_kernel,
        grid=(n // tn, m // tm),
        in_specs=[pl.BlockSpec((tm, k), lambda j, i: (i, 0)),
                  pl.BlockSpec((1, tn, k), lambda j, i: (layer, j, 0))],
        out_specs=pl.BlockSpec((tm, tn), lambda j, i: (i, j)),
        out_shape=jax.ShapeDtypeStruct((m, n), out_dtype),
        scratch_shapes=[pltpu.VMEM((tn, k), BF16)],
        compiler_params=_cparams(("arbitrary", "arbitrary"), 48),
        name="proj",
    )(x, wt)


def _bwd_chunk(i, n_ctx, n_lat):
    return jnp.where(i < n_ctx, n_ctx - 1 - i, 2 * n_ctx + n_lat - 1 - i)


def _rows(c):
    return pl.ds(pl.multiple_of(c * CHUNK, CHUNK), CHUNK)


def _for_chunks(n_ctx, n_lat, fn, unroll, with_ctx=True):
    def run(part, n, off):
        def body(c, carry):
            fn(part, c, c + off)
            return carry
        lax.fori_loop(0, n, body, 0, unroll=max(u for u in (1, 2, unroll) if n % u == 0 and u <= unroll))
    if with_ctx:
        run(0, n_ctx, 0)
    run(1, n_lat, n_ctx)


def _lanes(d, w):
    return slice(d * w, (d + 1) * w)


def _mixer_call(kernel_fn, name, in_specs, args, nb, heads, lat_len, ctx_len, dv, scratch, emit_ctx, vmem_mb):
    width = heads * dv
    out_specs = [pl.BlockSpec((lat_len, dv), lambda b, h: (b, h))]
    out_shape = [jax.ShapeDtypeStruct((nb * lat_len, width), BF16)]
    if emit_ctx:
        out_specs.append(pl.BlockSpec((ctx_len, dv), lambda b, h: (b, h)))
        out_shape.append(jax.ShapeDtypeStruct((nb * ctx_len, width), BF16))
    n_in = len(in_specs)

    def body(*refs):
        ins, rest = refs[:n_in], refs[n_in:]
        if emit_ctx:
            ol, oc, scr = rest[0], rest[1], rest[2:]
        else:
            ol, oc, scr = rest[0], None, rest[1:]
        kernel_fn(*ins, ol, oc, *scr, n_ctx=ctx_len // CHUNK, n_lat=lat_len // CHUNK)

    return pl.pallas_call(
        body,
        grid=(nb, heads),
        in_specs=in_specs,
        out_specs=out_specs,
        out_shape=out_shape,
        scratch_shapes=scratch,
        compiler_params=_cparams(("arbitrary", "arbitrary"), vmem_mb),
        name=name,
    )(*args)


def _seq_specs(col0, width, nb, lat_len, ctx_len):
    n_lat_blk = nb * lat_len // ctx_len
    c0 = col0 * 128 // width
    return (pl.BlockSpec((lat_len, width), lambda b, h: (b, c0 + h)),
            pl.BlockSpec((ctx_len, width), lambda b, h: (n_lat_blk + b, c0 + h)))


def _ret_kernel(ql, qc, kl, kc, vl, vc, gl, gc, cos_ref, sin_ref, lg_ref, gn_ref, ol_ref, oc_ref,
                qs_s, kv_s, sall_s, o_s, st_s, dm_s, dq_s, dk_s, *, n_ctx, n_lat):
    nch = n_ctx + n_lat
    unroll_a, unroll_c = RET_UNROLL
    qr, kr, vr, gr, outr = (qc, ql), (kc, kl), (vc, vl), (gc, gl), (oc_ref, ol_ref)
    t_i = lax.broadcasted_iota(jnp.int32, (CHUNK, CHUNK), 0).astype(F32)
    s_i = lax.broadcasted_iota(jnp.int32, (CHUNK, CHUNK), 1).astype(F32)
    dm = None
    for d in range(2):
        lg = lg_ref[d, 0][0:1, 0:CHUNK]
        rel = (t_i - s_i) if d == 0 else (s_i - t_i)
        dmd = jnp.where(rel >= 0, jnp.exp(jnp.maximum(rel, 0.0) * lg), 0.0)
        dm = dmd if dm is None else dm + dmd
        p = t_i if d == 0 else (CHUNK - 1.0) - t_i
        dq_s[d] = jnp.exp((p + 1.0) * lg)
        dk_s[d] = jnp.exp((CHUNK - 1.0 - p) * lg)
    dm_s[...] = dm
    scale = RET_DK ** -0.5

    def phase_a(part, cl, cg):
        rl, rg = _rows(cl), _rows(cg)
        cs = cos_ref[rg, :]
        sn = sin_ref[rg, :]
        q = qr[part][rl, :]
        k = kr[part][rl, :]
        q = q * cs + pltpu.roll(q, RET_DK // 2, 1) * sn
        k = (k * cs + pltpu.roll(k, RET_DK // 2, 1) * sn) * scale
        vb = _bf(vr[part][rl, :])
        s = _dot_nt(_bf(q), _bf(k))
        o_s[rg, :] = _dot(_bf(s * dm_s[...]), vb)
        qs_s[rg, :] = jnp.concatenate([_bf(q * dq_s[0]), _bf(q * dq_s[1])], axis=1)
        kd = jnp.concatenate([_bf(k * dk_s[0]), _bf(k * dk_s[1])], axis=1)
        kv_s[cg] = _dot_tn(kd, vb)

    _for_chunks(n_ctx, n_lat, phase_a, unroll_a)

    st_s[...] = jnp.zeros(st_s.shape, F32)
    dec = [jnp.exp(float(CHUNK) * lg_ref[d, 0][0:1, :]) for d in range(2)]

    def phase_b(i, carry):
        for d, c in ((0, i), (1, _bwd_chunk(i, n_ctx, n_lat))):
            st = st_s[d]
            sall_s[c, _lanes(d, RET_DK), :] = _bf(st)
            st_s[d] = dec[d] * st + kv_s[c, _lanes(d, RET_DK), :]
        return carry

    lax.fori_loop(0, nch, phase_b, 0)
    gn = gn_ref[...]

    def phase_c(part, cl, cg):
        rl, rg = _rows(cl), _rows(cg)
        o = o_s[rg, :] + _dot(qs_s[rg, :], sall_s[cg])
        outr[part][rl, :] = (_rms(o, gn) * _silu(gr[part][rl, :])).astype(BF16)

    _for_chunks(n_ctx, n_lat, phase_c, unroll_c, with_ctx=oc_ref is not None)


def _retention(p, nb, lat_len, ctx_len, cosf, sinf, lg_b, gn, emit_ctx):
    t_len = lat_len + ctx_len
    nch = t_len // CHUNK
    s128 = functools.partial(_seq_specs, width=128, nb=nb, lat_len=lat_len, ctx_len=ctx_len)
    s256 = functools.partial(_seq_specs, width=256, nb=nb, lat_len=lat_len, ctx_len=ctx_len)
    in_specs = [*s128(_C_RET_Q), *s128(_C_RET_K), *s256(_C_RET_V), *s256(_C_RET_G),
                pl.BlockSpec((t_len, 128), lambda b, h: (0, 0)),
                pl.BlockSpec((t_len, 128), lambda b, h: (0, 0)),
                pl.BlockSpec((2, 1, 8, RET_DV), lambda b, h: (0, h, 0, 0)),
                pl.BlockSpec((1, RET_DV), lambda b, h: (0, h))]
    scratch = [pltpu.VMEM((t_len, 2 * RET_DK), BF16),
               pltpu.VMEM((nch, 2 * RET_DK, RET_DV), F32),
               pltpu.VMEM((nch, 2 * RET_DK, RET_DV), BF16),
               pltpu.VMEM((t_len, RET_DV), F32),
               pltpu.VMEM((2, RET_DK, RET_DV), F32),
               pltpu.VMEM((CHUNK, CHUNK), F32),
               pltpu.VMEM((2, CHUNK, CHUNK), F32), pltpu.VMEM((2, CHUNK, CHUNK), F32)]
    args = (p, p, p, p, p, p, p, p, cosf, sinf, lg_b, gn.reshape(1, RET_WIDTH))
    return _mixer_call(_ret_kernel, "retention", in_specs, args, nb, RET_HEADS, lat_len, ctx_len, RET_DV,
                       scratch, emit_ctx, 48)


_HG_LEVELS = (64, 32, 16, 8, 4, 2, 1)


def _hgrn_constants():
    c = CHUNK
    t = np.arange(c)[:, None]
    u = np.arange(c)[None, :]
    cum = (u <= t).astype(np.float32)
    masks, signs = [], []
    for m in _HG_LEVELS:
        base = (t // (2 * m)) * (2 * m)
        lower = t >= base + m
        tb = t // (2 * m)
        sb = u // (2 * m)
        masks.append((tb == sb) & lower & (u < (sb * 2 * m + m)))
        signs.append(np.broadcast_to(np.where(lower, 1.0, -1.0), (c, c)))
    masks.append(t == u)
    kf = np.stack([x.astype(np.float32) for x in masks], axis=0)
    kb = np.stack([x.astype(np.float32)[::-1, ::-1] for x in masks], axis=0)
    sf = np.stack([x.astype(np.float32) for x in signs], axis=0)
    sb_ = np.stack([x.astype(np.float32)[::-1, ::-1] for x in signs], axis=0)
    return (np.stack([cum, cum[::-1, ::-1]], 0), np.stack([kf, kb], 0),
            np.stack([sf[:_HG_WIDE], sb_[:_HG_WIDE]], 0))


_HG_WIDE = 5


def _hgrn_level_exponents(b, lf, d, sgn_ref, row):
    out = []
    for l, m in enumerate(_HG_LEVELS[:_HG_WIDE]):
        pieces = []
        for j in range(CHUNK // (2 * m)):
            r = 2 * m * j + (m - 1 if d == 0 else m)
            pieces.append(jnp.broadcast_to(b[r:r + 1, :], (2 * m, CHUNK)))
        bref = pieces[0] if len(pieces) == 1 else jnp.concatenate(pieces, axis=0)
        out.append((b - bref) * sgn_ref[d, l])
    up = pltpu.roll(lf, CHUNK - 1, 0)
    dn = pltpu.roll(lf, 1, 0)
    r4 = row % 4
    if d == 0:
        e2 = jnp.where(r4 == 0, up, jnp.where(r4 == 1, 0.0, jnp.where(r4 == 2, lf, lf + dn)))
        e1 = jnp.where(row % 2 == 1, lf, 0.0)
    else:
        e2 = jnp.where(r4 == 0, lf + up, jnp.where(r4 == 1, lf, jnp.where(r4 == 2, 0.0, dn)))
        e1 = jnp.where(row % 2 == 0, lf, 0.0)
    return out + [e2, e1]


def _hgrn_kernel(ql, qc, ffl, ffc, fbl, fbc, il, ic, gl, gc, llb_ref, lub_ref, gn_ref, c_ref, k_ref, sgn_ref,
                 ol_ref, oc_ref, qs_s, kv_s, sall_s, dec_s, o_s, st_s, *, n_ctx, n_lat):
    nch = n_ctx + n_lat
    unroll_a, unroll_c = HGRN_UNROLL
    qr, fr, ir, gr, outr =(qc, ql), ((ffc, ffl), (fbc, fbl)), (ic, il), (gc, gl), (oc_ref, ol_ref)
    llb = llb_ref[...]
    lub = lub_ref[...]
    nlev = len(_HG_LEVELS)
    row = lax.broadcasted_iota(jnp.int32, (CHUNK, HGRN_DK), 0)

    def phase_a(part, cl, cg):
        rl, rg = _rows(cl), _rows(cg)
        q = _silu(qr[part][rl, :])
        qb = _bf(q)
        vb = _bf(ir[part][rl, :])
        a_sum, qs, kds = None, [], []
        for d in range(2):
            lsg = lub + _log_sigmoid(fr[d][part][rl, :])
            lf = jnp.maximum(llb, lsg) + jnp.log(1.0 + jnp.exp(-jnp.abs(llb - lsg)))
            k = 1.0 - jnp.exp(lf)
            lf_hi = _bf(lf)
            lf_lo = _bf(lf - lf_hi.astype(F32))
            b2 = _dot(c_ref[d], jnp.concatenate([lf_hi, lf_lo], axis=1))
            b = b2[:, 0:HGRN_DK] + b2[:, HGRN_DK:]
            es = _hgrn_level_exponents(b, lf, d, sgn_ref, row)
            kb = _bf(k)
            a = k_ref[d, nlev] * _dot_nt(qb, kb)
            for l in range(nlev):
                xb = _bf(jnp.exp(es[l]))
                a = a + k_ref[d, l] * _dot_nt(qb * xb, kb * xb)
            a_sum = a if a_sum is None else a_sum + a
            b_last = b[CHUNK - 1:CHUNK] if d == 0 else b[0:1]
            qs.append(_bf(q * jnp.exp(b)))
            kds.append(_bf(k * jnp.exp(b_last - b)))
            dec_s[cg, :, _lanes(d, HGRN_DK)] = jnp.exp(b_last)
        o_s[rg, :] = _dot(_bf(a_sum), vb)
        qs_s[rg, :] = jnp.concatenate(qs, axis=1)
        kv_s[cg] = _dot_tn(vb, jnp.concatenate(kds, axis=1))

    _for_chunks(n_ctx, n_lat, phase_a, unroll_a)
    st_s[...] = jnp.zeros(st_s.shape, F32)

    def phase_b(i, carry):
        for d, c in ((0, i), (1, _bwd_chunk(i, n_ctx, n_lat))):
            ln = _lanes(d, HGRN_DK)
            st = st_s[d]
            sall_s[c, :, ln] = _bf(st)
            st_s[d] = dec_s[c, :, ln] * st + kv_s[c, :, ln]
        return carry

    lax.fori_loop(0, nch, phase_b, 0)
    gn = gn_ref[...]

    def phase_c(part, cl, cg):
        rl, rg = _rows(cl), _rows(cg)
        o = o_s[rg, :] + _dot_nt(qs_s[rg, :], sall_s[cg])
        outr[part][rl, :] = (_rms(o, gn) * _silu(gr[part][rl, :])).astype(BF16)

    _for_chunks(n_ctx, n_lat, phase_c, unroll_c, with_ctx=oc_ref is not None)


def _hgrn(p, nb, lat_len, ctx_len, log_lb, log_ub, gn, consts, emit_ctx):
    mats, masks, signs = consts
    t_len = lat_len + ctx_len
    nch = t_len // CHUNK
    spec = functools.partial(_seq_specs, width=128, nb=nb, lat_len=lat_len, ctx_len=ctx_len)
    vec = pl.BlockSpec((1, 128), lambda b, h: (0, h))
    in_specs = [*spec(_C_HG_Q), *spec(_C_HG_FF), *spec(_C_HG_FB), *spec(_C_HG_I), *spec(_C_HG_G),
                vec, vec, vec,
                pl.BlockSpec(mats.shape, lambda b, h: (0, 0, 0)),
                pl.BlockSpec(masks.shape, lambda b, h: (0, 0, 0, 0)),
                pl.BlockSpec(signs.shape, lambda b, h: (0, 0, 0, 0))]
    scratch = [pltpu.VMEM((t_len, 2 * HGRN_DK), BF16),
               pltpu.VMEM((nch, HGRN_DV, 2 * HGRN_DK), F32),
               pltpu.VMEM((nch, HGRN_DV, 2 * HGRN_DK), BF16),
               pltpu.VMEM((nch, 1, 2 * HGRN_DK), F32),
               pltpu.VMEM((t_len, HGRN_DV), F32),
               pltpu.VMEM((2, HGRN_DV, HGRN_DK), F32)]
    args = (p, p, p, p, p, p, p, p, p, p, log_lb.reshape(1, HGRN_QK), log_ub.reshape(1, HGRN_QK),
            gn.reshape(1, HGRN_WIDTH), mats, masks, signs)
    return _mixer_call(_hgrn_kernel, "hgrn2", in_specs, args, nb, HGRN_HEADS, lat_len, ctx_len, HGRN_DV,
                       scratch, emit_ctx, 48)


def _mlstm_kernel(ql, qc, kl, kc, vl, vc, ogl, ogc, gcl, gcc, grl, grc, bc_ref, br_ref,
                  wq_ref, wk_ref, gn_ref, ol_ref, oc_ref,
                  q_s, gr_s, intra_s, kv_s, call_s, rho_s, bcol_s, bl_s, mu_s, mprev_s, st_s, m_s,
                  *, n_ctx, n_lat):
    nch = n_ctx + n_lat
    unroll_a, unroll_c = MLSTM_UNROLL
    h = pl.program_id(1)
    nh = MLSTM_HEADS
    dk, dv = MLSTM_DK, MLSTM_DV
    ext = 2 * dv
    qr, kr, vr, ogr, gcr, outr = (qc, ql), (kc, kl), (vc, vl), (ogc, ogl), (gcc, gcl), (oc_ref, ol_ref)
    n_loc = (n_ctx, n_lat)

    for cc in range(nch):
        src, c0 = (grc, cc) if cc < n_ctx else (grl, cc - n_ctx)
        gr_s[cc] = src[0, :, c0 * CHUNK:(c0 + 1) * CHUNK] + br_ref[...]

    row = lax.broadcasted_iota(jnp.int32, (CHUNK, dk), 0)
    lane = lax.broadcasted_iota(jnp.int32, (CHUNK, dv), 1)
    del lane
    ones_col = jnp.ones((CHUNK, dv), BF16)
    scale = dk ** -0.5
    t_i = lax.broadcasted_iota(jnp.int32, (CHUNK, CHUNK), 0)
    s_i = lax.broadcasted_iota(jnp.int32, (CHUNK, CHUNK), 1)
    lane16 = lax.broadcasted_iota(jnp.int32, (CHUNK, MLSTM_GATES), 1)
    sub16 = lax.broadcasted_iota(jnp.int32, (MLSTM_GATES, CHUNK), 0)

    def conv(src, w_ref, part, cl):
        r0 = pl.multiple_of(cl * CHUNK, CHUNK)
        n_rows = n_loc[part] * CHUNK
        x = src[pl.ds(r0, CHUNK), :]
        pr = src[pl.ds(jnp.maximum(r0 - 1, 0), 1), :]
        nx = src[pl.ds(jnp.minimum(r0 + CHUNK, n_rows - 1), 1), :]
        pr = jnp.where(cl != 0, pr, 0.0)
        nx = jnp.where(cl != n_loc[part] - 1, nx, 0.0)
        xp = jnp.where(row == 0, pr, pltpu.roll(x, 1, 0))
        xn = jnp.where(row == CHUNK - 1, nx, pltpu.roll(x, CHUNK - 1, 0))
        w = w_ref[...]
        return _silu(w[0:1] * xp + w[1:2] * x + w[2:3] * xn)

    def pick_col(g, j):
        return jnp.sum(jnp.where(lane16 == j, g, 0.0), axis=1, keepdims=True)

    def pick_row(g, j):
        return jnp.sum(jnp.where(sub16 == j, g, 0.0), axis=0, keepdims=True)

    def phase_a(part, cl, cg):
        rl, rg = _rows(cl), _rows(cg)
        q = conv(qr[part], wq_ref, part, cl)
        k = conv(kr[part], wk_ref, part, cl) * scale
        qb = _bf(q)
        q_s[rg, :] = qb
        vb = jnp.concatenate([_bf(vr[part][rl, :]), ones_col], axis=1)
        s = _dot_nt(qb, _bf(k))
        g_c = gcr[part][rl, :] + bc_ref[...]
        g_r = gr_s[cg]
        kws = []
        for d in range(2):
            ig_c = pick_col(g_c, d * nh + h)
            lf_c = _log_sigmoid(pick_col(g_c, 2 * nh + d * nh + h))
            ig_r = pick_row(g_r, d * nh + h)
            lf_r = _log_sigmoid(pick_row(g_r, 2 * nh + d * nh + h))
            tri = (s_i <= t_i) if d == 0 else (s_i >= t_i)
            tri_t = (t_i <= s_i) if d == 0 else (t_i >= s_i)
            b_c = jnp.sum(jnp.where(tri, lf_r, 0.0), axis=1, keepdims=True)
            b_r = jnp.sum(jnp.where(tri_t, lf_c, 0.0), axis=0, keepdims=True)
            dmat = jnp.where(tri, b_c - b_r + ig_r, -jnp.inf)
            rho = jnp.max(dmat, axis=1, keepdims=True)
            intra_s[d, rg, :] = _dot(_bf(s * jnp.exp(dmat - rho)), vb)
            b_last = b_c[CHUNK - 1:CHUNK] if d == 0 else b_c[0:1]
            mu = jnp.max(b_last - b_r + ig_r, axis=1, keepdims=True)
            kws.append(_bf(k * jnp.exp(b_last - b_c + ig_c - mu)))
            rho_s[d, rg, :] = jnp.broadcast_to(rho, (CHUNK, 128))
            bcol_s[d, rg, :] = jnp.broadcast_to(b_c, (CHUNK, 128))
            bl_s[cg, :, _lanes(d, 128)] = jnp.broadcast_to(b_last, (1, 128))
            mu_s[cg, :, _lanes(d, 128)] = jnp.broadcast_to(mu, (1, 128))
        kv_s[cg] = _dot_tn(jnp.concatenate(kws, axis=1), vb)

    _for_chunks(n_ctx, n_lat, phase_a, unroll_a)
    st_s[...] = jnp.zeros(st_s.shape, F32)
    m_s[...] = jnp.zeros(m_s.shape, F32)

    def wide(v):
        return jnp.concatenate([v, v], axis=1)

    def phase_b(i, carry):
        for d, c in ((0, i), (1, _bwd_chunk(i, n_ctx, n_lat))):
            ln = _lanes(d, 128)
            m_prev = m_s[d]
            st = st_s[d]
            mprev_s[c, :, ln] = m_prev
            call_s[c, :, _lanes(d, ext)] = _bf(st)
            bl = bl_s[c, :, ln]
            mu = mu_s[c, :, ln]
            m_new = jnp.maximum(bl + m_prev, mu)
            st_s[d] = (wide(jnp.exp(bl + m_prev - m_new)) * st
                       + wide(jnp.exp(mu - m_new)) * kv_s[c, _lanes(d, dk), :])
            m_s[d] = m_new
        return carry

    lax.fori_loop(0, nch, phase_b, 0)
    gn = gn_ref[...]

    def phase_c(part, cl, cg):
        rl, rg = _rows(cl), _rows(cg)
        qc_all = _dot(q_s[rg, :], call_s[cg])
        o = None
        for d in range(2):
            rho = rho_s[d, rg, :]
            b_c = bcol_s[d, rg, :]
            m_prev = mprev_s[cg, :, _lanes(d, 128)]
            m_t = jnp.maximum(rho, b_c + m_prev)
            nd = (wide(jnp.exp(rho - m_t)) * intra_s[d, rg, :]
                  + wide(jnp.exp(b_c + m_prev - m_t)) * qc_all[:, _lanes(d, ext)])
            hh = nd[:, 0:dv] / jnp.maximum(jnp.abs(nd[:, dv:]), jnp.exp(-m_t))
            o = hh if o is None else o + hh
        y = o - jnp.mean(o, axis=-1, keepdims=True)
        y = y * lax.rsqrt(jnp.mean(y * y, axis=-1, keepdims=True) + NORM_EPS)
        outr[part][rl, :] = (y * gn * _sigmoid(ogr[part][rl, :])).astype(BF16)

    _for_chunks(n_ctx, n_lat, phase_c, unroll_c, with_ctx=oc_ref is not None)


def _mlstm(p, g_col, g_row_lat, g_row_ctx, nb, lat_len, ctx_len, gate_b, conv_w, gn, emit_ctx):
    t_len = lat_len + ctx_len
    nch = t_len // CHUNK
    n_lat_blk = nb * lat_len // ctx_len
    ng = MLSTM_GATES
    dk, dv = MLSTM_DK, MLSTM_DV
    spec = functools.partial(_seq_specs, width=128, nb=nb, lat_len=lat_len, ctx_len=ctx_len)
    in_specs = [*spec(_C_ML_Q), *spec(_C_ML_K), *spec(_C_ML_V), *spec(_C_ML_O),
                pl.BlockSpec((lat_len, ng), lambda b, h: (b, 0)),
                pl.BlockSpec((ctx_len, ng), lambda b, h: (n_lat_blk + b, 0)),
                pl.BlockSpec((1, ng, lat_len), lambda b, h: (b, 0, 0)),
                pl.BlockSpec((1, ng, ctx_len), lambda b, h: (b, 0, 0)),
                pl.BlockSpec((1, ng), lambda b, h: (0, 0)),
                pl.BlockSpec((ng, 1), lambda b, h: (0, 0)),
                pl.BlockSpec((3, 128), lambda b, h: (0, h)),
                pl.BlockSpec((3, 128), lambda b, h: (0, MLSTM_HEADS + h)),
                pl.BlockSpec((1, 128), lambda b, h: (0, h))]
    scratch = [pltpu.VMEM((t_len, dk), BF16),
               pltpu.VMEM((nch, ng, CHUNK), F32),
               pltpu.VMEM((2, t_len, 2 * dv), F32),
               pltpu.VMEM((nch, 2 * dk, 2 * dv), F32),
               pltpu.VMEM((nch, dk, 4 * dv), BF16),
               pltpu.VMEM((2, t_len, 128), F32), pltpu.VMEM((2, t_len, 128), F32),
               pltpu.VMEM((nch, 1, 256), F32), pltpu.VMEM((nch, 1, 256), F32), pltpu.VMEM((nch, 1, 256), F32),
               pltpu.VMEM((2, dk, 2 * dv), F32), pltpu.VMEM((2, 1, 128), F32)]
    args = (p, p, p, p, p, p, p, p, g_col, g_col, g_row_lat, g_row_ctx,
            gate_b.reshape(1, ng), gate_b.reshape(ng, 1), conv_w, conv_w, gn.reshape(1, MLSTM_WIDTH))
    return _mixer_call(_mlstm_kernel, "mlstm", in_specs, args, nb, MLSTM_HEADS, lat_len, ctx_len, dv,
                       scratch, emit_ctx, 48)


def _wout_kernel(*refs, n_lat_tiles, n_mix_parts, n_x_parts, emit_next):
    it = iter(refs)
    mix = [[next(it) for _ in range(n_mix_parts)] for _ in range(3)]
    x_parts = [next(it) for _ in range(n_x_parts)]
    wr_ref, wh_ref, wm_ref, g_ref, mod_ref = (next(it) for _ in range(5))
    g2_ref = next(it) if emit_next else None
    o_ref = next(it)
    h2_ref = next(it) if emit_next else None

    def run(part):
        pm = min(part, n_mix_parts - 1)
        y = (_dot(mix[0][pm][...], wr_ref[...]) + _dot(mix[1][pm][...], wh_ref[...])
             + _dot(mix[2][pm][...], wm_ref[...]))
        o_ref[...] = y
        m = mod_ref[0]
        x_ref = x_parts[min(part, n_x_parts - 1)]
        g = g_ref[...]
        g2 = g2_ref[...] if emit_next else None

        def chunk(rows):
            xn = x_ref[rows, :] + m[2:3] * _rms(o_ref[rows, :], g)
            o_ref[rows, :] = xn
            if emit_next:
                h2_ref[rows, :] = (_rms(xn, g2) * (1.0 + m[4:5]) + m[3:4]).astype(h2_ref.dtype)

        _row_chunks(o_ref.shape[0], chunk)

    _on_part(n_lat_tiles, max(n_mix_parts, n_x_parts), run)


def _wout(mix_parts, w_out, x_parts, n_rows, g1, modtab, lat_rows, g2_next):
    d = x_parts[0].shape[1]
    nb = modtab.shape[0] - 1
    mi = _mod_index(ROW_TILE, lat_rows, nb)
    tm = ROW_TILE
    n_lat_tiles = mix_parts[0][0].shape[0] // tm
    emit_next = g2_next is not None
    widths = (RET_WIDTH, HGRN_WIDTH, MLSTM_WIDTH)
    w_r = w_out[0:RET_WIDTH]
    w_h = w_out[RET_WIDTH:RET_WIDTH + HGRN_WIDTH]
    w_m = w_out[RET_WIDTH + HGRN_WIDTH:]
    full = lambda a: pl.BlockSpec(a.shape, lambda i: (0, 0))
    in_specs, args = [], []
    for parts, w in zip(mix_parts, widths):
        in_specs += _part_specs(parts, tm, w)
        args += list(parts)
    in_specs += _part_specs(x_parts, tm, d)
    args += list(x_parts)
    in_specs += [full(w_r), full(w_h), full(w_m),
                 pl.BlockSpec((1, d), lambda i: (0, 0)),
                 pl.BlockSpec((1, 6, d), lambda i: (mi(i), 0, 0))]
    args += [w_r, w_h, w_m, g1.reshape(1, d), modtab]
    out_specs = [pl.BlockSpec((tm, d), lambda i: (i, 0))]
    out_shape = [jax.ShapeDtypeStruct((n_rows, d), F32)]
    if emit_next:
        in_specs.append(pl.BlockSpec((1, d), lambda i: (0, 0)))
        args.append(g2_next.reshape(1, d))
        out_specs.append(pl.BlockSpec((tm, d), lambda i: (i, 0)))
        out_shape.append(jax.ShapeDtypeStruct((n_rows, d), BF16))
    return pl.pallas_call(
        functools.partial(_wout_kernel, n_lat_tiles=n_lat_tiles, n_mix_parts=len(mix_parts[0]),
                          n_x_parts=len(x_parts), emit_next=emit_next),
        grid=(n_rows // tm,),
        in_specs=in_specs,
        out_specs=out_specs,
        out_shape=out_shape,
        compiler_params=_cparams(("arbitrary",), 52),
        name="wout",
    )(*args)


def _ffn_kernel(*refs, emit_next):
    if emit_next:
        h_ref, w1_ref, w3_ref, w2_ref, x_ref, g_ref, mod_ref, gn_ref, modn_ref, o_ref, hn_ref, acc_ref = refs
    else:
        h_ref, w1_ref, w3_ref, w2_ref, x_ref, g_ref, mod_ref, o_ref, acc_ref = refs
    f = pl.program_id(1)

    @pl.when(f == 0)
    def _():
        acc_ref[...] = jnp.zeros(acc_ref.shape, F32)

    h = h_ref[...]
    u = _silu(_dot(h, w1_ref[...])) * _dot(h, w3_ref[...])
    acc_ref[...] += _dot(_bf(u), w2_ref[...])

    @pl.when(f == pl.num_programs(1) - 1)
    def _():
        m = mod_ref[0]
        g = g_ref[...]
        gn = gn_ref[...] if emit_next else None
        mn = modn_ref[0] if emit_next else None

        def chunk(rows):
            xn = x_ref[rows, :] + m[5:6] * _rms(acc_ref[rows, :], g)
            o_ref[rows, :] = xn
            if emit_next:
                hn_ref[rows, :] = (_rms(xn, gn) * (1.0 + mn[1:2]) + mn[0:1]).astype(hn_ref.dtype)

        _row_chunks(o_ref.shape[0], chunk)


def _ffn(hb, w1, w3, w2, xs, g3, modtab, lat_rows, g_next, modtab_next):
    n_rows, d = hb.shape
    dff = w1.shape[1]
    nb = modtab.shape[0] - 1
    tm, tf = ROW_TILE, FFN_TF
    mi = _mod_index(tm, lat_rows, nb)
    emit_next = g_next is not None
    nf = dff // tf

    def fs(i, f):
        return jnp.where(i % 2 == 1, nf - 1 - f, f)

    in_specs = [pl.BlockSpec((tm, d), lambda i, f: (i, 0)),
                pl.BlockSpec((d, tf), lambda i, f: (0, fs(i, f))),
                pl.BlockSpec((d, tf), lambda i, f: (0, fs(i, f))),
                pl.BlockSpec((tf, d), lambda i, f: (fs(i, f), 0)),
                pl.BlockSpec((tm, d), lambda i, f: (i, 0)),
                pl.BlockSpec((1, d), lambda i, f: (0, 0)),
                pl.BlockSpec((1, 6, d), lambda i, f: (mi(i), 0, 0))]
    args = [hb, w1, w3, w2, xs, g3.reshape(1, d), modtab]
    out_specs = [pl.BlockSpec((tm, d), lambda i, f: (i, 0))]
    out_shape = [jax.ShapeDtypeStruct((n_rows, d), F32)]
    if emit_next:
        in_specs += [pl.BlockSpec((1, d), lambda i, f: (0, 0)),
                     pl.BlockSpec((1, 6, d), lambda i, f: (mi(i), 0, 0))]
        args += [g_next.reshape(1, d), modtab_next]
        out_specs.append(pl.BlockSpec((tm, d), lambda i, f: (i, 0)))
        out_shape.append(jax.ShapeDtypeStruct((n_rows, d), BF16))
    return pl.pallas_call(
        functools.partial(_ffn_kernel, emit_next=emit_next),
        grid=(n_rows // tm, dff // tf),
        in_specs=in_specs,
        out_specs=out_specs,
        out_shape=out_shape,
        scratch_shapes=[pltpu.VMEM((tm, d), F32)],
        compiler_params=_cparams(("arbitrary", "arbitrary"), 48),
        name="ffn",
    )(*args)


_META_E0, _META_E1, _META_R0, _META_R1, _META_G0, _META_G1 = range(6)


def _router_kernel(x_ref, g_ref, mod_ref, wr_ref, tri_ref, meta_ref, cnt_ref, zero_ref, carry_ref):
    i = pl.program_id(0)
    zero_ref[...] = jnp.zeros(zero_ref.shape, F32)

    @pl.when(i == 0)
    def _():
        carry_ref[...] = jnp.zeros(carry_ref.shape, F32)

    m = mod_ref[0]
    hmod = _rms(x_ref[...], g_ref[...]) * (1.0 + m[4:5]) + m[3:4]
    h_hi = _bf(hmod)
    h_lo = _bf(hmod - h_hi.astype(F32))
    w = wr_ref[...]
    w_hi = _bf(w)
    w_lo = _bf(w - w_hi.astype(F32))
    logits = _dot(h_hi, w_hi) + _dot(h_lo, w_hi) + _dot(h_hi, w_lo)
    lane = lax.broadcasted_iota(jnp.int32, logits.shape, 1)
    lanef = lane.astype(F32)
    logits = jnp.where(lane < N_EXPERTS, logits, -jnp.inf)
    v0 = jnp.max(logits, axis=1, keepdims=True)
    e0 = jnp.min(jnp.where(logits == v0, lanef, 1e9), axis=1, keepdims=True)
    rest = jnp.where(lanef == e0, -jnp.inf, logits)
    v1 = jnp.max(rest, axis=1, keepdims=True)
    e1 = jnp.min(jnp.where(rest == v1, lanef, 1e9), axis=1, keepdims=True)
    ex = jnp.exp(v1 - v0)
    g0 = 1.0 / (1.0 + ex)
    g1 = ex / (1.0 + ex)
    oh0 = lanef == e0
    oh1 = lanef == e1
    oh = jnp.where(jnp.logical_or(oh0, oh1), 1.0, 0.0)
    before = _dot(tri_ref[...], _bf(oh)) + carry_ref[0:1, :]
    r0 = jnp.sum(jnp.where(oh0, before, 0.0), axis=1, keepdims=True)
    r1 = jnp.sum(jnp.where(oh1, before, 0.0), axis=1, keepdims=True)
    carry_ref[0:1, :] = carry_ref[0:1, :] + jnp.sum(oh, axis=0, keepdims=True)
    meta = jnp.zeros(logits.shape, F32)
    for j, val in ((_META_E0, e0), (_META_E1, e1), (_META_R0, r0), (_META_R1, r1),
                   (_META_G0, g0), (_META_G1, g1)):
        meta = jnp.where(lane == j, val, meta)
    meta_ref[...] = meta
    cnt_ref[...] = carry_ref[...]


def _router(xs, n_rows, g2, modtab, w_router, lat_rows, n_slots):
    d = xs.shape[1]
    nb = modtab.shape[0] - 1
    tm = ROW_TILE
    mi = _mod_index(tm, lat_rows, nb)
    n_steps = n_rows // tm
    zrows = n_slots // n_steps
    assert zrows * n_steps == n_slots and zrows % 8 == 0, (n_slots, n_steps)
    wr = jnp.zeros((d, 128), F32).at[:, :N_EXPERTS].set(w_router)
    tri = jnp.asarray(np.tril(np.ones((tm, tm), np.float32), -1), BF16)
    return pl.pallas_call(
        _router_kernel,
        grid=(n_steps,),
        in_specs=[pl.BlockSpec((tm, d), lambda i: (i, 0)),
                  pl.BlockSpec((1, d), lambda i: (0, 0)),
                  pl.BlockSpec((1, 6, d), lambda i: (mi(i), 0, 0)),
                  pl.BlockSpec((d, 128), lambda i: (0, 0)),
                  pl.BlockSpec((tm, tm), lambda i: (0, 0))],
        out_specs=[pl.BlockSpec((tm, 128), lambda i: (i, 0)),
                   pl.BlockSpec((8, 128), lambda i: (0, 0)),
                   pl.BlockSpec((zrows, d), lambda i: (i, 0))],
        out_shape=[jax.ShapeDtypeStruct((n_rows, 128), F32),
                   jax.ShapeDtypeStruct((8, 128), F32),
                   jax.ShapeDtypeStruct((n_slots, d), F32)],
        scratch_shapes=[pltpu.VMEM((8, 128), F32)],
        compiler_params=_cparams(("arbitrary",), 48),
        name="router",
    )(xs, g2.reshape(1, d), modtab, wr, tri)


def _dispatch_kernel(dest_ref, x_ref, g_ref, mod_ref, init_ref, o_ref, h_s, sem):
    del init_ref
    i = pl.program_id(0)
    tm = h_s.shape[0]
    m = mod_ref[0]
    h_s[...] = _rms(x_ref[...], g_ref[...]) * (1.0 + m[4:5]) + m[3:4]

    def row_copy(r, k):
        dst = dest_ref[TOP_K * (i * tm + r) + k]
        return pltpu.make_async_copy(h_s.at[pl.ds(r, 1)], o_ref.at[pl.ds(dst, 1)], sem)

    def start(r, carry):
        for k in range(TOP_K):
            row_copy(r, k).start(priority=k)
        return carry

    lax.fori_loop(0, tm, start, 0, unroll=DMA_UNROLL)

    def wait(r, carry):
        for k in range(TOP_K):
            row_copy(r, k).wait()
        return carry

    lax.fori_loop(0, tm, wait, 0, unroll=DMA_UNROLL)


def _dispatch(dest, xs, n_rows, g2, modtab, init, lat_rows):
    d = xs.shape[1]
    n_slots = init.shape[0]
    nb = modtab.shape[0] - 1
    tm = ROW_TILE
    mi = _mod_index(tm, lat_rows, nb)
    grid_spec = pltpu.PrefetchScalarGridSpec(
        num_scalar_prefetch=1,
        grid=(n_rows // tm,),
        in_specs=[pl.BlockSpec((tm, d), lambda i, dst: (i, 0)),
                  pl.BlockSpec((1, d), lambda i, dst: (0, 0)),
                  pl.BlockSpec((1, 6, d), lambda i, dst: (mi(i), 0, 0)),
                  pl.BlockSpec(memory_space=pl.ANY)],
        out_specs=pl.BlockSpec(memory_space=pl.ANY),
        scratch_shapes=[pltpu.VMEM((tm, d), F32), pltpu.SemaphoreType.DMA(())],
    )
    return pl.pallas_call(
        _dispatch_kernel,
        grid_spec=grid_spec,
        out_shape=jax.ShapeDtypeStruct((n_slots, d), F32),
        input_output_aliases={4: 0},
        compiler_params=_cparams(("arbitrary",), 32),
        name="dispatch",
    )(dest, xs, g2.reshape(1, d), modtab, init)


def _expert_kernel(ge_ref, nt_ref, ng_ref, x_ref, *refs):
    w1_refs, w3_refs = refs[:MOE_WSPLIT], refs[MOE_WSPLIT:2 * MOE_WSPLIT]
    w2_ref, o_ref, xb_s = refs[2 * MOE_WSPLIT:]
    s = pl.program_id(0)
    f = pl.program_id(1)
    tm = MOE_TM

    @pl.when(s < ng_ref[0])
    def _():
        @pl.when(f == 0)
        def _():
            xb_s[...] = _bf(x_ref[...])
            o_ref[...] = jnp.zeros(o_ref.shape, F32)

        w1b = jnp.concatenate([_bf(r[0]) for r in w1_refs], axis=0)
        w3b = jnp.concatenate([_bf(r[0]) for r in w3_refs], axis=0)
        w2b = _bf(w2_ref[0])

        def tile(t):
            rows = pl.ds(t * tm, tm)
            h = xb_s[rows, :]
            u = _silu(_dot(h, w1b)) * _dot(h, w3b)
            o_ref[rows, :] += _dot(_bf(u), w2b)

        tile(0)
        for t in range(1, MOE_GROUP):
            pl.when(nt_ref[s] > t)(functools.partial(tile, t))

    @pl.when(jnp.logical_and(s >= ng_ref[0], f == pl.num_programs(1) - 1))
    def _():
        o_ref[...] = jnp.zeros(o_ref.shape, F32)


def _experts(group_e, group_tiles, n_groups, xsort, w1, w3, w2):
    n_slots, d = xsort.shape
    dff = w1.shape[2]
    tg, tf = MOE_TM * MOE_GROUP, MOE_TF
    nf = dff // tf

    def ss(s, ng):
        return jnp.maximum(jnp.minimum(s, ng[0] - 1), 0)

    def ff(s, f, ng):
        snake = jnp.where(s % 2 == 1, nf - 1 - f, f)
        last = jnp.where((ng[0] - 1) % 2 == 1, 0, nf - 1)
        return jnp.where(s < ng[0], snake, last)

    grid_spec = pltpu.PrefetchScalarGridSpec(
        num_scalar_prefetch=3,
        grid=(n_slots // tg, nf),
        in_specs=[pl.BlockSpec((tg, d), lambda s, f, ge, nt, ng: (ss(s, ng), 0)),
                  *[pl.BlockSpec((1, d // MOE_WSPLIT, tf),
                                 lambda s, f, ge, nt, ng, part=part: (ge[ss(s, ng)], part, ff(s, f, ng)))
                    for _ in range(2) for part in range(MOE_WSPLIT)],
                  pl.BlockSpec((1, tf, d), lambda s, f, ge, nt, ng: (ge[ss(s, ng)], ff(s, f, ng), 0))],
        out_specs=pl.BlockSpec((tg, d), lambda s, f, ge, nt, ng: (s, 0)),
        scratch_shapes=[pltpu.VMEM((tg, d), BF16)],
    )
    return pl.pallas_call(
        _expert_kernel,
        grid_spec=grid_spec,
        out_shape=jax.ShapeDtypeStruct((n_slots, d), F32),
        compiler_params=_cparams(("arbitrary", "arbitrary"), 56),
        name="experts",
    )(group_e, group_tiles, n_groups, xsort, *([w1] * MOE_WSPLIT), *([w3] * MOE_WSPLIT), w2)


def _combine_kernel(dest_ref, y_ref, meta_ref, x_ref, g_ref, mod_ref, o_ref, buf_s, sem):
    i = pl.program_id(0)
    tm = x_ref.shape[0]

    def row_copy(r, k):
        src = dest_ref[TOP_K * (i * tm + r) + k]
        return pltpu.make_async_copy(y_ref.at[pl.ds(src, 1)], buf_s.at[k, pl.ds(r, 1)], sem)

    def start(r, carry):
        for k in range(TOP_K):
            row_copy(r, k).start(priority=k)
        return carry

    lax.fori_loop(0, tm, start, 0, unroll=DMA_UNROLL)

    def wait(r, carry):
        for k in range(TOP_K):
            row_copy(r, k).wait()
        return carry

    lax.fori_loop(0, tm, wait, 0, unroll=DMA_UNROLL)

    m = mod_ref[0]
    g = g_ref[...]
    def chunk(rows):
        meta = meta_ref[rows, :]
        lane = lax.broadcasted_iota(jnp.int32, meta.shape, 1)
        g0 = jnp.sum(jnp.where(lane == _META_G0, meta, 0.0), axis=1, keepdims=True)
        g1 = jnp.sum(jnp.where(lane == _META_G1, meta, 0.0), axis=1, keepdims=True)
        y = buf_s[0, rows, :] * g0 + buf_s[1, rows, :] * g1
        o_ref[rows, :] = x_ref[rows, :] + m[5:6] * _rms(y, g)

    _row_chunks(tm, chunk)


def _combine(dest, yb, meta, xs, n_rows, g3, modtab, lat_rows):
    d = xs.shape[1]
    nb = modtab.shape[0] - 1
    tm = ROW_TILE
    mi = _mod_index(tm, lat_rows, nb)
    grid_spec = pltpu.PrefetchScalarGridSpec(
        num_scalar_prefetch=1,
        grid=(n_rows // tm,),
        in_specs=[pl.BlockSpec(memory_space=pl.ANY),
                  pl.BlockSpec((tm, 128), lambda i, dst: (i, 0)),
                  pl.BlockSpec((tm, d), lambda i, dst: (i, 0)),
                  pl.BlockSpec((1, d), lambda i, dst: (0, 0)),
                  pl.BlockSpec((1, 6, d), lambda i, dst: (mi(i), 0, 0))],
        out_specs=pl.BlockSpec((tm, d), lambda i, dst: (i, 0)),
        scratch_shapes=[pltpu.VMEM((TOP_K, tm, d), F32), pltpu.SemaphoreType.DMA(())],
    )
    return pl.pallas_call(
        _combine_kernel,
        grid_spec=grid_spec,
        out_shape=jax.ShapeDtypeStruct((n_rows, d), F32),
        compiler_params=_cparams(("arbitrary",), 40),
        name="combine",
    )(dest, yb, meta, xs, g3.reshape(1, d), modtab)


def _moe(xs, n_rows, g2, g3, modtab, w_router, w1, w3, w2, lat_rows):
    tg = MOE_TM * MOE_GROUP
    n_groups_max = (n_rows * TOP_K) // tg + N_EXPERTS
    n_slots = n_groups_max * tg
    meta, cnt, slots_init = _router(xs, n_rows, g2, modtab, w_router, lat_rows, n_slots)
    counts = cnt[0, :N_EXPERTS].astype(jnp.int32)
    padded = (counts + tg - 1) // tg * tg
    pad_end = jnp.cumsum(padded)
    pad_start = pad_end - padded
    e = meta[:, _META_E0:_META_E1 + 1].astype(jnp.int32)
    r = meta[:, _META_R0:_META_R1 + 1].astype(jnp.int32)
    dest = (pad_start[e] + r).reshape(-1)
    group_start = jnp.arange(n_groups_max, dtype=jnp.int32) * tg
    group_e = jnp.minimum(jnp.sum((group_start[:, None] >= pad_end[None, :]).astype(jnp.int32), axis=1),
                          N_EXPERTS - 1)
    filled = counts[group_e] - (group_start - pad_start[group_e])
    group_tiles = jnp.clip((filled + MOE_TM - 1) // MOE_TM, 1, MOE_GROUP).astype(jnp.int32)
    n_groups = (pad_end[-1:] // tg).astype(jnp.int32)
    xsort = _dispatch(dest, xs, n_rows, g2, modtab, slots_init, lat_rows)
    yb = _experts(group_e, group_tiles, n_groups, xsort, w1, w3, w2)
    return _combine(dest, yb, meta, xs, n_rows, g3, modtab, lat_rows)


def _rotary_tables(lat_len, ctx_len):
    rows = lat_len // GRID_W
    row = jnp.repeat(jnp.arange(rows, dtype=F32), GRID_W)
    col = jnp.tile(jnp.arange(GRID_W, dtype=F32), rows)
    n_freq = RET_DK // 4
    inv = ROPE_BASE ** (-jnp.arange(n_freq, dtype=F32) / n_freq)
    ang = jnp.concatenate([row[:, None] * inv, col[:, None] * inv], axis=-1)
    cos, sin = jnp.cos(ang), jnp.sin(ang)
    cosf = jnp.concatenate([jnp.ones((ctx_len, RET_DK), F32), jnp.concatenate([cos, cos], -1)], 0)
    sinf = jnp.concatenate([jnp.zeros((ctx_len, RET_DK), F32), jnp.concatenate([-sin, sin], -1)], 0)
    return cosf, sinf


def kernel(x, c, ctx, c_ctx, w_ada, b_ada, norm_g, w_in, w_out, ret_decay, ret_gn, hgrn_lb, hgrn_gn,
           mlstm_conv, mlstm_gate_b, mlstm_gn, w_ffn1, w_ffn3, w_ffn2, w_router, w_exp1, w_exp3, w_exp2):
    nb, lat_len, d = x.shape
    ctx_len = ctx.shape[1]
    depth = w_ada.shape[0]
    n_lat_rows = nb * lat_len
    n_rows = n_lat_rows + nb * ctx_len

    x_parts = [x.reshape(n_lat_rows, d), ctx.reshape(nb * ctx_len, d)]
    w_in_t = jnp.swapaxes(w_in, 1, 2)
    s_in = jnp.zeros((8, d), F32).at[:nb].set(c).at[nb].set(c_ctx)
    mod_all = _ada(s_in, w_ada, b_ada)[:, :nb + 1].reshape(depth, nb + 1, 6, d)

    cosf, sinf = _rotary_tables(lat_len, ctx_len)
    hg_cum, hg_masks, hg_signs = _hgrn_constants()
    hg_consts = (jnp.asarray(hg_cum, BF16), jnp.asarray(hg_masks, F32), jnp.asarray(hg_signs, F32))
    sm = jax.nn.softmax(hgrn_lb.astype(F32), axis=0)
    lb_all = jnp.clip(jnp.cumsum(sm, axis=0) - sm[0], 0.0, 1.0)

    hb = None
    for layer in range(depth):
        last = layer == depth - 1
        modtab = mod_all[layer]
        g = norm_g[layer]

        if hb is None:
            hb = _prenorm(x_parts, n_rows, g[0], modtab, 1, 0, lat_len, BF16)
        p = _mm(hb, w_in_t, layer, MM_TM, MM_TN, n_cols=PROJ_MAIN)
        w_g = jnp.zeros((1, 128, d), F32).at[0, :MLSTM_GATES].set(w_in_t[layer, PROJ_MAIN:])
        gates = _mm(hb, w_g, 0, MM_TM, 128)[:, :MLSTM_GATES]
        hb = None
        g_row_lat = jnp.swapaxes(gates[:n_lat_rows].reshape(nb, lat_len, MLSTM_GATES), 1, 2)
        g_row_ctx = jnp.swapaxes(gates[n_lat_rows:].reshape(nb, ctx_len, MLSTM_GATES), 1, 2)

        log_g = jax.nn.log_sigmoid(ret_decay[layer].astype(F32))
        lg_b = jnp.broadcast_to(log_g[:, :, None, None], (2, RET_HEADS, 8, RET_DV))
        lb = lb_all[layer]
        emit_ctx = not last
        o_ret = _retention(p, nb, lat_len, ctx_len, cosf, sinf, lg_b, ret_gn[layer], emit_ctx)
        o_hg = _hgrn(p, nb, lat_len, ctx_len, jnp.log(lb), jnp.log1p(-lb), hgrn_gn[layer],
                     hg_consts, emit_ctx)
        o_ml = _mlstm(p, gates, g_row_lat, g_row_ctx, nb, lat_len, ctx_len, mlstm_gate_b[layer],
                      mlstm_conv[layer], mlstm_gn[layer], emit_ctx)
        mix_parts = [list(o) for o in (o_ret, o_hg, o_ml)]
        rows_now = n_lat_rows if last else n_rows
        dense = layer % 2 == 0
        res = _wout(mix_parts, _bf(w_out[layer]), x_parts, rows_now, g[1], modtab, lat_len,
                    g[2] if dense else None)
        xs = res[0]
        x_parts = [xs]

        j = layer // 2
        if dense:
            nxt = (None, None) if last else (norm_g[layer + 1][0], mod_all[layer + 1])
            res = _ffn(res[1], _bf(w_ffn1[j]), _bf(w_ffn3[j]), _bf(w_ffn2[j]), xs, g[3], modtab, lat_len, *nxt)
            xs = res[0]
            hb = None if last else res[1]
        else:
            xs = _moe(xs, rows_now, g[2], g[3], modtab, w_router[j], w_exp1[j], w_exp3[j], w_exp2[j], lat_len)
        x_parts = [xs]
    return xs[:n_lat_rows].reshape(nb, lat_len, d)
```

```python
import functools

import numpy as np
import jax
import jax.numpy as jnp
from jax import lax
from jax.experimental import pallas as pl
from jax.experimental.pallas import tpu as pltpu

F32 = jnp.float32
BF16 = jnp.bfloat16

CHUNK = 128
NORM_EPS = 1e-6
ROPE_BASE = 10000.0
GRID_W = 64

RET_HEADS, RET_DK, RET_DV = 4, 128, 256
HGRN_HEADS, HGRN_DK, HGRN_DV = 4, 128, 128
MLSTM_HEADS, MLSTM_DK, MLSTM_DV = 4, 128, 128
N_EXPERTS = 8
TOP_K = 2

RET_QK = RET_HEADS * RET_DK
RET_WIDTH = RET_HEADS * RET_DV
HGRN_QK = HGRN_HEADS * HGRN_DK
HGRN_WIDTH = HGRN_HEADS * HGRN_DV
MLSTM_QK = MLSTM_HEADS * MLSTM_DK
MLSTM_WIDTH = MLSTM_HEADS * MLSTM_DV
MLSTM_GATES = 4 * MLSTM_HEADS

_C_RET_Q = 0
_C_RET_K = _C_RET_Q + RET_QK // 128
_C_RET_V = _C_RET_K + RET_QK // 128
_C_RET_G = _C_RET_V + RET_WIDTH // 128
_C_HG_Q = _C_RET_G + RET_WIDTH // 128
_C_HG_FF = _C_HG_Q + HGRN_QK // 128
_C_HG_FB = _C_HG_FF + HGRN_QK // 128
_C_HG_I = _C_HG_FB + HGRN_QK // 128
_C_HG_G = _C_HG_I + HGRN_WIDTH // 128
_C_ML_Q = _C_HG_G + HGRN_WIDTH // 128
_C_ML_K = _C_ML_Q + MLSTM_QK // 128
_C_ML_V = _C_ML_K + MLSTM_QK // 128
_C_ML_O = _C_ML_V + MLSTM_WIDTH // 128
PROJ_MAIN = (_C_ML_O + MLSTM_WIDTH // 128) * 128

_VMEM_CAP_BYTES = 56 * 1024 * 1024

ROW_TILE = 512
MM_TM, MM_TN = 1024, 1280
FFN_TF = 512
MOE_TM = 768
MOE_WSPLIT = 1
MOE_GROUP = 1
MOE_TF = 512
RET_UNROLL = (16, 16)
HGRN_UNROLL = (4, 8)
MLSTM_UNROLL = (2, 8)
DMA_UNROLL = 8
EPI_ROWS = None


def _cparams(sem, vmem_mb):
    return pltpu.CompilerParams(dimension_semantics=sem,
                                vmem_limit_bytes=min(int(vmem_mb * 1024 * 1024), _VMEM_CAP_BYTES))


def _bf(x):
    return x.astype(BF16)


def _dot(a, b):
    return jnp.dot(a, b, preferred_element_type=F32)


def _dot_nt(a, b):
    return lax.dot_general(a, b, (((1,), (1,)), ((), ())), preferred_element_type=F32)


def _dot_tn(a, b):
    return lax.dot_general(a, b, (((0,), (0,)), ((), ())), preferred_element_type=F32)


def _sigmoid(x):
    return 1.0 / (1.0 + jnp.exp(-x))


def _silu(x):
    return x * _sigmoid(x)


def _log_sigmoid(x):
    return jnp.minimum(x, 0.0) - jnp.log(1.0 + jnp.exp(-jnp.abs(x)))


def _rms(x, g):
    return x * lax.rsqrt(jnp.mean(x * x, axis=-1, keepdims=True) + NORM_EPS) * g


def _mod_index(tile_rows, n_lat_rows_per_batch, n_batch):
    return lambda i: jnp.minimum((i * tile_rows) // n_lat_rows_per_batch, n_batch)


def _ada_kernel(s_ref, w_ref, b_ref, o_ref):
    s = _bf(_silu(s_ref[...]))
    o_ref[0] = _dot(s, _bf(w_ref[0])) + b_ref[0]


def _ada(s_in, w_ada, b_ada):
    depth, d, n = w_ada.shape
    tn = 1024
    return pl.pallas_call(
        _ada_kernel,
        grid=(depth, n // tn),
        in_specs=[pl.BlockSpec((8, d), lambda l, j: (0, 0)),
                  pl.BlockSpec((1, d, tn), lambda l, j: (l, 0, j)),
                  pl.BlockSpec((1, 1, tn), lambda l, j: (l, 0, j))],
        out_specs=pl.BlockSpec((1, 8, tn), lambda l, j: (l, 0, j)),
        out_shape=jax.ShapeDtypeStruct((depth, 8, n), F32),
        compiler_params=_cparams(("arbitrary", "arbitrary"), 40),
        name="ada",
    )(s_in, w_ada, b_ada.reshape(depth, 1, n))


def _row_chunks(n_rows, fn):
    if EPI_ROWS is None:
        fn(pl.ds(0, n_rows))
        return

    def body(i, carry):
        fn(pl.ds(pl.multiple_of(i * EPI_ROWS, EPI_ROWS), EPI_ROWS))
        return carry
    lax.fori_loop(0, n_rows // EPI_ROWS, body, 0, unroll=2)


def _part_specs(parts, tm, width):
    if len(parts) == 1:
        return [pl.BlockSpec((tm, width), lambda i: (i, 0))]
    n0 = parts[0].shape[0] // tm
    return [pl.BlockSpec((tm, width), lambda i: (jnp.minimum(i, n0 - 1), 0)),
            pl.BlockSpec((tm, width), lambda i: (jnp.maximum(i - n0, 0), 0))]


def _on_part(n_first, n_parts, fn):
    if n_parts == 1:
        fn(0)
    else:
        i = pl.program_id(0)
        pl.when(i < n_first)(lambda: fn(0))
        pl.when(i >= n_first)(lambda: fn(1))


def _prenorm_kernel(*refs, sc, sh, n_first):
    x_refs, (g_ref, mod_ref, o_ref) = refs[:-3], refs[-3:]

    def run(part):
        m = mod_ref[0]
        g = g_ref[...]

        def chunk(rows):
            y = _rms(x_refs[part][rows, :], g)
            o_ref[rows, :] = (y * (1.0 + m[sc:sc + 1]) + m[sh:sh + 1]).astype(o_ref.dtype)

        _row_chunks(o_ref.shape[0], chunk)

    _on_part(n_first, len(x_refs), run)


def _prenorm(x_parts, n_rows, g, modtab, sc, sh, lat_rows, out_dtype):
    d = x_parts[0].shape[1]
    nb = modtab.shape[0] - 1
    mi = _mod_index(ROW_TILE, lat_rows, nb)
    return pl.pallas_call(
        functools.partial(_prenorm_kernel, sc=sc, sh=sh, n_first=x_parts[0].shape[0] // ROW_TILE),
        grid=(n_rows // ROW_TILE,),
        in_specs=[*_part_specs(x_parts, ROW_TILE, d),
                  pl.BlockSpec((1, d), lambda i: (0, 0)),
                  pl.BlockSpec((1, 6, d), lambda i: (mi(i), 0, 0))],
        out_specs=pl.BlockSpec((ROW_TILE, d), lambda i: (i, 0)),
        out_shape=jax.ShapeDtypeStruct((n_rows, d), out_dtype),
        compiler_params=_cparams(("arbitrary",), 32),
        name="prenorm",
    )(*x_parts, g.reshape(1, d), modtab)


def _mm_kernel(x_ref, wt_ref, o_ref, wb_s):
    @pl.when(pl.program_id(1) == 0)
    def _():
        wb_s[...] = _bf(wt_ref[0])

    o_ref[...] = _dot_nt(x_ref[...], wb_s[...]).astype(o_ref.dtype)


def _mm(x, wt, layer, tm, tn, n_cols=None, out_dtype=F32):
    m, k = x.shape
    n = wt.shape[1] if n_cols is None else n_cols
    return pl.pallas_call(
        _mm_kernel,
        grid=(n // tn, m // tm),
        in_specs=[pl.BlockSpec((tm, k), lambda j, i: (i, 0)),
                  pl.BlockSpec((1, tn, k), lambda j, i: (layer, j, 0))],
        out_specs=pl.BlockSpec((tm, tn), lambda j, i: (i, j)),
        out_shape=jax.ShapeDtypeStruct((m, n), out_dtype),
        scratch_shapes=[pltpu.VMEM((tn, k), BF16)],
        compiler_params=_cparams(("arbitrary", "arbitrary"), 48),
        name="proj",
    )(x, wt)


def _bwd_chunk(i, n_ctx, n_lat):
    return jnp.where(i < n_ctx, n_ctx - 1 - i, 2 * n_ctx + n_lat - 1 - i)


def _rows(c):
    return pl.ds(pl.multiple_of(c * CHUNK, CHUNK), CHUNK)


def _for_chunks(n_ctx, n_lat, fn, unroll, with_ctx=True):
    def run(part, n, off):
        def body(c, carry):
            fn(part, c, c + off)
            return carry
        lax.fori_loop(0, n, body, 0, unroll=max(u for u in (1, 2, unroll) if n % u == 0 and u <= unroll))
    if with_ctx:
        run(0, n_ctx, 0)
    run(1, n_lat, n_ctx)


def _lanes(d, w):
    return slice(d * w, (d + 1) * w)


def _mixer_call(kernel_fn, name, in_specs, args, nb, heads, lat_len, ctx_len, dv, scratch, emit_ctx, vmem_mb):
    width = heads * dv
    out_specs = [pl.BlockSpec((lat_len, dv), lambda b, h: (b, h))]
    out_shape = [jax.ShapeDtypeStruct((nb * lat_len, width), BF16)]
    if emit_ctx:
        out_specs.append(pl.BlockSpec((ctx_len, dv), lambda b, h: (b, h)))
        out_shape.append(jax.ShapeDtypeStruct((nb * ctx_len, width), BF16))
    n_in = len(in_specs)

    def body(*refs):
        ins, rest = refs[:n_in], refs[n_in:]
        if emit_ctx:
            ol, oc, scr = rest[0], rest[1], rest[2:]
        else:
            ol, oc, scr = rest[0], None, rest[1:]
        kernel_fn(*ins, ol, oc, *scr, n_ctx=ctx_len // CHUNK, n_lat=lat_len // CHUNK)

    return pl.pallas_call(
        body,
        grid=(nb, heads),
        in_specs=in_specs,
        out_specs=out_specs,
        out_shape=out_shape,
        scratch_shapes=scratch,
        compiler_params=_cparams(("arbitrary", "arbitrary"), vmem_mb),
        name=name,
    )(*args)


def _seq_specs(col0, width, nb, lat_len, ctx_len):
    n_lat_blk = nb * lat_len // ctx_len
    c0 = col0 * 128 // width
    return (pl.BlockSpec((lat_len, width), lambda b, h: (b, c0 + h)),
            pl.BlockSpec((ctx_len, width), lambda b, h: (n_lat_blk + b, c0 + h)))


def _ret_kernel(ql, qc, kl, kc, vl, vc, gl, gc, cos_ref, sin_ref, lg_ref, gn_ref, ol_ref, oc_ref,
                qs_s, kv_s, sall_s, o_s, st_s, dm_s, dq_s, dk_s, *, n_ctx, n_lat):
    nch = n_ctx + n_lat
    unroll_a, unroll_c = RET_UNROLL
    qr, kr, vr, gr, outr = (qc, ql), (kc, kl), (vc, vl), (gc, gl), (oc_ref, ol_ref)
    t_i = lax.broadcasted_iota(jnp.int32, (CHUNK, CHUNK), 0).astype(F32)
    s_i = lax.broadcasted_iota(jnp.int32, (CHUNK, CHUNK), 1).astype(F32)
    dm = None
    for d in range(2):
        lg = lg_ref[d, 0][0:1, 0:CHUNK]
        rel = (t_i - s_i) if d == 0 else (s_i - t_i)
        dmd = jnp.where(rel >= 0, jnp.exp(jnp.maximum(rel, 0.0) * lg), 0.0)
        dm = dmd if dm is None else dm + dmd
        p = t_i if d == 0 else (CHUNK - 1.0) - t_i
        dq_s[d] = jnp.exp((p + 1.0) * lg)
        dk_s[d] = jnp.exp((CHUNK - 1.0 - p) * lg)
    dm_s[...] = dm
    scale = RET_DK ** -0.5

    def phase_a(part, cl, cg):
        rl, rg = _rows(cl), _rows(cg)
        cs = cos_ref[rg, :]
        sn = sin_ref[rg, :]
        q = qr[part][rl, :]
        k = kr[part][rl, :]
        q = q * cs + pltpu.roll(q, RET_DK // 2, 1) * sn
        k = (k * cs + pltpu.roll(k, RET_DK // 2, 1) * sn) * scale
        vb = _bf(vr[part][rl, :])
        s = _dot_nt(_bf(q), _bf(k))
        o_s[rg, :] = _dot(_bf(s * dm_s[...]), vb)
        qs_s[rg, :] = jnp.concatenate([_bf(q * dq_s[0]), _bf(q * dq_s[1])], axis=1)
        kd = jnp.concatenate([_bf(k * dk_s[0]), _bf(k * dk_s[1])], axis=1)
        kv_s[cg] = _dot_tn(kd, vb)

    _for_chunks(n_ctx, n_lat, phase_a, unroll_a)

    st_s[...] = jnp.zeros(st_s.shape, F32)
    dec = [jnp.exp(float(CHUNK) * lg_ref[d, 0][0:1, :]) for d in range(2)]

    def phase_b(i, carry):
        for d, c in ((0, i), (1, _bwd_chunk(i, n_ctx, n_lat))):
            st = st_s[d]
            sall_s[c, _lanes(d, RET_DK), :] = _bf(st)
            st_s[d] = dec[d] * st + kv_s[c, _lanes(d, RET_DK), :]
        return carry

    lax.fori_loop(0, nch, phase_b, 0)
    gn = gn_ref[...]

    def phase_c(part, cl, cg):
        rl, rg = _rows(cl), _rows(cg)
        o = o_s[rg, :] + _dot(qs_s[rg, :], sall_s[cg])
        outr[part][rl, :] = (_rms(o, gn) * _silu(gr[part][rl, :])).astype(BF16)

    _for_chunks(n_ctx, n_lat, phase_c, unroll_c, with_ctx=oc_ref is not None)


def _retention(p, nb, lat_len, ctx_len, cosf, sinf, lg_b, gn, emit_ctx):
    t_len = lat_len + ctx_len
    nch = t_len // CHUNK
    s128 = functools.partial(_seq_specs, width=128, nb=nb, lat_len=lat_len, ctx_len=ctx_len)
    s256 = functools.partial(_seq_specs, width=256, nb=nb, lat_len=lat_len, ctx_len=ctx_len)
    in_specs = [*s128(_C_RET_Q), *s128(_C_RET_K), *s256(_C_RET_V), *s256(_C_RET_G),
                pl.BlockSpec((t_len, 128), lambda b, h: (0, 0)),
                pl.BlockSpec((t_len, 128), lambda b, h: (0, 0)),
                pl.BlockSpec((2, 1, 8, RET_DV), lambda b, h: (0, h, 0, 0)),
                pl.BlockSpec((1, RET_DV), lambda b, h: (0, h))]
    scratch = [pltpu.VMEM((t_len, 2 * RET_DK), BF16),
               pltpu.VMEM((nch, 2 * RET_DK, RET_DV), F32),
               pltpu.VMEM((nch, 2 * RET_DK, RET_DV), BF16),
               pltpu.VMEM((t_len, RET_DV), F32),
               pltpu.VMEM((2, RET_DK, RET_DV), F32),
               pltpu.VMEM((CHUNK, CHUNK), F32),
               pltpu.VMEM((2, CHUNK, CHUNK), F32), pltpu.VMEM((2, CHUNK, CHUNK), F32)]
    args = (p, p, p, p, p, p, p, p, cosf, sinf, lg_b, gn.reshape(1, RET_WIDTH))
    return _mixer_call(_ret_kernel, "retention", in_specs, args, nb, RET_HEADS, lat_len, ctx_len, RET_DV,
                       scratch, emit_ctx, 48)


_HG_LEVELS = (64, 32, 16, 8, 4, 2, 1)


def _hgrn_constants():
    c = CHUNK
    t = np.arange(c)[:, None]
    u = np.arange(c)[None, :]
    cum = (u <= t).astype(np.float32)
    masks, signs = [], []
    for m in _HG_LEVELS:
        base = (t // (2 * m)) * (2 * m)
        lower = t >= base + m
        tb = t // (2 * m)
        sb = u // (2 * m)
        masks.append((tb == sb) & lower & (u < (sb * 2 * m + m)))
        signs.append(np.broadcast_to(np.where(lower, 1.0, -1.0), (c, c)))
    masks.append(t == u)
    kf = np.stack([x.astype(np.float32) for x in masks], axis=0)
    kb = np.stack([x.astype(np.float32)[::-1, ::-1] for x in masks], axis=0)
    sf = np.stack([x.astype(np.float32) for x in signs], axis=0)
    sb_ = np.stack([x.astype(np.float32)[::-1, ::-1] for x in signs], axis=0)
    return (np.stack([cum, cum[::-1, ::-1]], 0), np.stack([kf, kb], 0),
            np.stack([sf[:_HG_WIDE], sb_[:_HG_WIDE]], 0))


_HG_WIDE = 5


def _hgrn_level_exponents(b, lf, d, sgn_ref, row):
    out = []
    for l, m in enumerate(_HG_LEVELS[:_HG_WIDE]):
        pieces = []
        for j in range(CHUNK // (2 * m)):
            r = 2 * m * j + (m - 1 if d == 0 else m)
            pieces.append(jnp.broadcast_to(b[r:r + 1, :], (2 * m, CHUNK)))
        bref = pieces[0] if len(pieces) == 1 else jnp.concatenate(pieces, axis=0)
        out.append((b - bref) * sgn_ref[d, l])
    up = pltpu.roll(lf, CHUNK - 1, 0)
    dn = pltpu.roll(lf, 1, 0)
    r4 = row % 4
    if d == 0:
        e2 = jnp.where(r4 == 0, up, jnp.where(r4 == 1, 0.0, jnp.where(r4 == 2, lf, lf + dn)))
        e1 = jnp.where(row % 2 == 1, lf, 0.0)
    else:
        e2 = jnp.where(r4 == 0, lf + up, jnp.where(r4 == 1, lf, jnp.where(r4 == 2, 0.0, dn)))
        e1 = jnp.where(row % 2 == 0, lf, 0.0)
    return out + [e2, e1]


def _hgrn_kernel(ql, qc, ffl, ffc, fbl, fbc, il, ic, gl, gc, llb_ref, lub_ref, gn_ref, c_ref, k_ref, sgn_ref,
                 ol_ref, oc_ref, qs_s, kv_s, sall_s, dec_s, o_s, st_s, *, n_ctx, n_lat):
    nch = n_ctx + n_lat
    unroll_a, unroll_c = HGRN_UNROLL
    qr, fr, ir, gr, outr =(qc, ql), ((ffc, ffl), (fbc, fbl)), (ic, il), (gc, gl), (oc_ref, ol_ref)
    llb = llb_ref[...]
    lub = lub_ref[...]
    nlev = len(_HG_LEVELS)
    row = lax.broadcasted_iota(jnp.int32, (CHUNK, HGRN_DK), 0)

    def phase_a(part, cl, cg):
        rl, rg = _rows(cl), _rows(cg)
        q = _silu(qr[part][rl, :])
        qb = _bf(q)
        vb = _bf(ir[part][rl, :])
        a_sum, qs, kds = None, [], []
        for d in range(2):
            lsg = lub + _log_sigmoid(fr[d][part][rl, :])
            lf = jnp.maximum(llb, lsg) + jnp.log(1.0 + jnp.exp(-jnp.abs(llb - lsg)))
            k = 1.0 - jnp.exp(lf)
            lf_hi = _bf(lf)
            lf_lo = _bf(lf - lf_hi.astype(F32))
            b2 = _dot(c_ref[d], jnp.concatenate([lf_hi, lf_lo], axis=1))
            b = b2[:, 0:HGRN_DK] + b2[:, HGRN_DK:]
            es = _hgrn_level_exponents(b, lf, d, sgn_ref, row)
            kb = _bf(k)
            a = k_ref[d, nlev] * _dot_nt(qb, kb)
            for l in range(nlev):
                xb = _bf(jnp.exp(es[l]))
                a = a + k_ref[d, l] * _dot_nt(qb * xb, kb * xb)
            a_sum = a if a_sum is None else a_sum + a
            b_last = b[CHUNK - 1:CHUNK] if d == 0 else b[0:1]
            qs.append(_bf(q * jnp.exp(b)))
            kds.append(_bf(k * jnp.exp(b_last - b)))
            dec_s[cg, :, _lanes(d, HGRN_DK)] = jnp.exp(b_last)
        o_s[rg, :] = _dot(_bf(a_sum), vb)
        qs_s[rg, :] = jnp.concatenate(qs, axis=1)
        kv_s[cg] = _dot_tn(vb, jnp.concatenate(kds, axis=1))

    _for_chunks(n_ctx, n_lat, phase_a, unroll_a)
    st_s[...] = jnp.zeros(st_s.shape, F32)

    def phase_b(i, carry):
        for d, c in ((0, i), (1, _bwd_chunk(i, n_ctx, n_lat))):
            ln = _lanes(d, HGRN_DK)
            st = st_s[d]
            sall_s[c, :, ln] = _bf(st)
            st_s[d] = dec_s[c, :, ln] * st + kv_s[c, :, ln]
        return carry

    lax.fori_loop(0, nch, phase_b, 0)
    gn = gn_ref[...]

    def phase_c(part, cl, cg):
        rl, rg = _rows(cl), _rows(cg)
        o = o_s[rg, :] + _dot_nt(qs_s[rg, :], sall_s[cg])
        outr[part][rl, :] = (_rms(o, gn) * _silu(gr[part][rl, :])).astype(BF16)

    _for_chunks(n_ctx, n_lat, phase_c, unroll_c, with_ctx=oc_ref is not None)


def _hgrn(p, nb, lat_len, ctx_len, log_lb, log_ub, gn, consts, emit_ctx):
    mats, masks, signs = consts
    t_len = lat_len + ctx_len
    nch = t_len // CHUNK
    spec = functools.partial(_seq_specs, width=128, nb=nb, lat_len=lat_len, ctx_len=ctx_len)
    vec = pl.BlockSpec((1, 128), lambda b, h: (0, h))
    in_specs = [*spec(_C_HG_Q), *spec(_C_HG_FF), *spec(_C_HG_FB), *spec(_C_HG_I), *spec(_C_HG_G),
                vec, vec, vec,
                pl.BlockSpec(mats.shape, lambda b, h: (0, 0, 0)),
                pl.BlockSpec(masks.shape, lambda b, h: (0, 0, 0, 0)),
                pl.BlockSpec(signs.shape, lambda b, h: (0, 0, 0, 0))]
    scratch = [pltpu.VMEM((t_len, 2 * HGRN_DK), BF16),
               pltpu.VMEM((nch, HGRN_DV, 2 * HGRN_DK), F32),
               pltpu.VMEM((nch, HGRN_DV, 2 * HGRN_DK), BF16),
               pltpu.VMEM((nch, 1, 2 * HGRN_DK), F32),
               pltpu.VMEM((t_len, HGRN_DV), F32),
               pltpu.VMEM((2, HGRN_DV, HGRN_DK), F32)]
    args = (p, p, p, p, p, p, p, p, p, p, log_lb.reshape(1, HGRN_QK), log_ub.reshape(1, HGRN_QK),
            gn.reshape(1, HGRN_WIDTH), mats, masks, signs)
    return _mixer_call(_hgrn_kernel, "hgrn2", in_specs, args, nb, HGRN_HEADS, lat_len, ctx_len, HGRN_DV,
                       scratch, emit_ctx, 48)


def _mlstm_kernel(ql, qc, kl, kc, vl, vc, ogl, ogc, gcl, gcc, grl, grc, bc_ref, br_ref,
                  wq_ref, wk_ref, gn_ref, ol_ref, oc_ref,
                  q_s, gr_s, intra_s, kv_s, call_s, rho_s, bcol_s, bl_s, mu_s, mprev_s, st_s, m_s,
                  *, n_ctx, n_lat):
    nch = n_ctx + n_lat
    unroll_a, unroll_c = MLSTM_UNROLL
    h = pl.program_id(1)
    nh = MLSTM_HEADS
    dk, dv = MLSTM_DK, MLSTM_DV
    ext = 2 * dv
    qr, kr, vr, ogr, gcr, outr = (qc, ql), (kc, kl), (vc, vl), (ogc, ogl), (gcc, gcl), (oc_ref, ol_ref)
    n_loc = (n_ctx, n_lat)

    for cc in range(nch):
        src, c0 = (grc, cc) if cc < n_ctx else (grl, cc - n_ctx)
        gr_s[cc] = src[0, :, c0 * CHUNK:(c0 + 1) * CHUNK] + br_ref[...]

    row = lax.broadcasted_iota(jnp.int32, (CHUNK, dk), 0)
    lane = lax.broadcasted_iota(jnp.int32, (CHUNK, dv), 1)
    del lane
    ones_col = jnp.ones((CHUNK, dv), BF16)
    scale = dk ** -0.5
    t_i = lax.broadcasted_iota(jnp.int32, (CHUNK, CHUNK), 0)
    s_i = lax.broadcasted_iota(jnp.int32, (CHUNK, CHUNK), 1)
    lane16 = lax.broadcasted_iota(jnp.int32, (CHUNK, MLSTM_GATES), 1)
    sub16 = lax.broadcasted_iota(jnp.int32, (MLSTM_GATES, CHUNK), 0)

    def conv(src, w_ref, part, cl):
        r0 = pl.multiple_of(cl * CHUNK, CHUNK)
        n_rows = n_loc[part] * CHUNK
        x = src[pl.ds(r0, CHUNK), :]
        pr = src[pl.ds(jnp.maximum(r0 - 1, 0), 1), :]
        nx = src[pl.ds(jnp.minimum(r0 + CHUNK, n_rows - 1), 1), :]
        pr = jnp.where(cl != 0, pr, 0.0)
        nx = jnp.where(cl != n_loc[part] - 1, nx, 0.0)
        xp = jnp.where(row == 0, pr, pltpu.roll(x, 1, 0))
        xn = jnp.where(row == CHUNK - 1, nx, pltpu.roll(x, CHUNK - 1, 0))
        w = w_ref[...]
        return _silu(w[0:1] * xp + w[1:2] * x + w[2:3] * xn)

    def pick_col(g, j):
        return jnp.sum(jnp.where(lane16 == j, g, 0.0), axis=1, keepdims=True)

    def pick_row(g, j):
        return jnp.sum(jnp.where(sub16 == j, g, 0.0), axis=0, keepdims=True)

    def phase_a(part, cl, cg):
        rl, rg = _rows(cl), _rows(cg)
        q = conv(qr[part], wq_ref, part, cl)
        k = conv(kr[part], wk_ref, part, cl) * scale
        qb = _bf(q)
        q_s[rg, :] = qb
        vb = jnp.concatenate([_bf(vr[part][rl, :]), ones_col], axis=1)
        s = _dot_nt(qb, _bf(k))
        g_c = gcr[part][rl, :] + bc_ref[...]
        g_r = gr_s[cg]
        kws = []
        for d in range(2):
            ig_c = pick_col(g_c, d * nh + h)
            lf_c = _log_sigmoid(pick_col(g_c, 2 * nh + d * nh + h))
            ig_r = pick_row(g_r, d * nh + h)
            lf_r = _log_sigmoid(pick_row(g_r, 2 * nh + d * nh + h))
            tri = (s_i <= t_i) if d == 0 else (s_i >= t_i)
            tri_t = (t_i <= s_i) if d == 0 else (t_i >= s_i)
            b_c = jnp.sum(jnp.where(tri, lf_r, 0.0), axis=1, keepdims=True)
            b_r = jnp.sum(jnp.where(tri_t, lf_c, 0.0), axis=0, keepdims=True)
            dmat = jnp.where(tri, b_c - b_r + ig_r, -jnp.inf)
            rho = jnp.max(dmat, axis=1, keepdims=True)
            intra_s[d, rg, :] = _dot(_bf(s * jnp.exp(dmat - rho)), vb)
            b_last = b_c[CHUNK - 1:CHUNK] if d == 0 else b_c[0:1]
            mu = jnp.max(b_last - b_r + ig_r, axis=1, keepdims=True)
            kws.append(_bf(k * jnp.exp(b_last - b_c + ig_c - mu)))
            rho_s[d, rg, :] = jnp.broadcast_to(rho, (CHUNK, 128))
            bcol_s[d, rg, :] = jnp.broadcast_to(b_c, (CHUNK, 128))
            bl_s[cg, :, _lanes(d, 128)] = jnp.broadcast_to(b_last, (1, 128))
            mu_s[cg, :, _lanes(d, 128)] = jnp.broadcast_to(mu, (1, 128))
        kv_s[cg] = _dot_tn(jnp.concatenate(kws, axis=1), vb)

    _for_chunks(n_ctx, n_lat, phase_a, unroll_a)
    st_s[...] = jnp.zeros(st_s.shape, F32)
    m_s[...] = jnp.zeros(m_s.shape, F32)

    def wide(v):
        return jnp.concatenate([v, v], axis=1)

    def phase_b(i, carry):
        for d, c in ((0, i), (1, _bwd_chunk(i, n_ctx, n_lat))):
            ln = _lanes(d, 128)
            m_prev = m_s[d]
            st = st_s[d]
            mprev_s[c, :, ln] = m_prev
            call_s[c, :, _lanes(d, ext)] = _bf(st)
            bl = bl_s[c, :, ln]
            mu = mu_s[c, :, ln]
            m_new = jnp.maximum(bl + m_prev, mu)
            st_s[d] = (wide(jnp.exp(bl + m_prev - m_new)) * st
                       + wide(jnp.exp(mu - m_new)) * kv_s[c, _lanes(d, dk), :])
            m_s[d] = m_new
        return carry

    lax.fori_loop(0, nch, phase_b, 0)
    gn = gn_ref[...]

    def phase_c(part, cl, cg):
        rl, rg = _rows(cl), _rows(cg)
        qc_all = _dot(q_s[rg, :], call_s[cg])
        o = None
        for d in range(2):
            rho = rho_s[d, rg, :]
            b_c = bcol_s[d, rg, :]
            m_prev = mprev_s[cg, :, _lanes(d, 128)]
            m_t = jnp.maximum(rho, b_c + m_prev)
            nd = (wide(jnp.exp(rho - m_t)) * intra_s[d, rg, :]
                  + wide(jnp.exp(b_c + m_prev - m_t)) * qc_all[:, _lanes(d, ext)])
            hh = nd[:, 0:dv] / jnp.maximum(jnp.abs(nd[:, dv:]), jnp.exp(-m_t))
            o = hh if o is None else o + hh
        y = o - jnp.mean(o, axis=-1, keepdims=True)
        y = y * lax.rsqrt(jnp.mean(y * y, axis=-1, keepdims=True) + NORM_EPS)
        outr[part][rl, :] = (y * gn * _sigmoid(ogr[part][rl, :])).astype(BF16)

    _for_chunks(n_ctx, n_lat, phase_c, unroll_c, with_ctx=oc_ref is not None)


def _mlstm(p, g_col, g_row_lat, g_row_ctx, nb, lat_len, ctx_len, gate_b, conv_w, gn, emit_ctx):
    t_len = lat_len + ctx_len
    nch = t_len // CHUNK
    n_lat_blk = nb * lat_len // ctx_len
    ng = MLSTM_GATES
    dk, dv = MLSTM_DK, MLSTM_DV
    spec = functools.partial(_seq_specs, width=128, nb=nb, lat_len=lat_len, ctx_len=ctx_len)
    in_specs = [*spec(_C_ML_Q), *spec(_C_ML_K), *spec(_C_ML_V), *spec(_C_ML_O),
                pl.BlockSpec((lat_len, ng), lambda b, h: (b, 0)),
                pl.BlockSpec((ctx_len, ng), lambda b, h: (n_lat_blk + b, 0)),
                pl.BlockSpec((1, ng, lat_len), lambda b, h: (b, 0, 0)),
                pl.BlockSpec((1, ng, ctx_len), lambda b, h: (b, 0, 0)),
                pl.BlockSpec((1, ng), lambda b, h: (0, 0)),
                pl.BlockSpec((ng, 1), lambda b, h: (0, 0)),
                pl.BlockSpec((3, 128), lambda b, h: (0, h)),
                pl.BlockSpec((3, 128), lambda b, h: (0, MLSTM_HEADS + h)),
                pl.BlockSpec((1, 128), lambda b, h: (0, h))]
    scratch = [pltpu.VMEM((t_len, dk), BF16),
               pltpu.VMEM((nch, ng, CHUNK), F32),
               pltpu.VMEM((2, t_len, 2 * dv), F32),
               pltpu.VMEM((nch, 2 * dk, 2 * dv), F32),
               pltpu.VMEM((nch, dk, 4 * dv), BF16),
               pltpu.VMEM((2, t_len, 128), F32), pltpu.VMEM((2, t_len, 128), F32),
               pltpu.VMEM((nch, 1, 256), F32), pltpu.VMEM((nch, 1, 256), F32), pltpu.VMEM((nch, 1, 256), F32),
               pltpu.VMEM((2, dk, 2 * dv), F32), pltpu.VMEM((2, 1, 128), F32)]
    args = (p, p, p, p, p, p, p, p, g_col, g_col, g_row_lat, g_row_ctx,
            gate_b.reshape(1, ng), gate_b.reshape(ng, 1), conv_w, conv_w, gn.reshape(1, MLSTM_WIDTH))
    return _mixer_call(_mlstm_kernel, "mlstm", in_specs, args, nb, MLSTM_HEADS, lat_len, ctx_len, dv,
                       scratch, emit_ctx, 48)


def _wout_kernel(*refs, n_lat_tiles, n_mix_parts, n_x_parts, emit_next):
    it = iter(refs)
    mix = [[next(it) for _ in range(n_mix_parts)] for _ in range(3)]
    x_parts = [next(it) for _ in range(n_x_parts)]
    wr_ref, wh_ref, wm_ref, g_ref, mod_ref = (next(it) for _ in range(5))
    g2_ref = next(it) if emit_next else None
    o_ref = next(it)
    h2_ref = next(it) if emit_next else None

    def run(part):
        pm = min(part, n_mix_parts - 1)
        y = (_dot(mix[0][pm][...], wr_ref[...]) + _dot(mix[1][pm][...], wh_ref[...])
             + _dot(mix[2][pm][...], wm_ref[...]))
        o_ref[...] = y
        m = mod_ref[0]
        x_ref = x_parts[min(part, n_x_parts - 1)]
        g = g_ref[...]
        g2 = g2_ref[...] if emit_next else None

        def chunk(rows):
            xn = x_ref[rows, :] + m[2:3] * _rms(o_ref[rows, :], g)
            o_ref[rows, :] = xn
            if emit_next:
                h2_ref[rows, :] = (_rms(xn, g2) * (1.0 + m[4:5]) + m[3:4]).astype(h2_ref.dtype)

        _row_chunks(o_ref.shape[0], chunk)

    _on_part(n_lat_tiles, max(n_mix_parts, n_x_parts), run)


def _wout(mix_parts, w_out, x_parts, n_rows, g1, modtab, lat_rows, g2_next):
    d = x_parts[0].shape[1]
    nb = modtab.shape[0] - 1
    mi = _mod_index(ROW_TILE, lat_rows, nb)
    tm = ROW_TILE
    n_lat_tiles = mix_parts[0][0].shape[0] // tm
    emit_next = g2_next is not None
    widths = (RET_WIDTH, HGRN_WIDTH, MLSTM_WIDTH)
    w_r = w_out[0:RET_WIDTH]
    w_h = w_out[RET_WIDTH:RET_WIDTH + HGRN_WIDTH]
    w_m = w_out[RET_WIDTH + HGRN_WIDTH:]
    full = lambda a: pl.BlockSpec(a.shape, lambda i: (0, 0))
    in_specs, args = [], []
    for parts, w in zip(mix_parts, widths):
        in_specs += _part_specs(parts, tm, w)
        args += list(parts)
    in_specs += _part_specs(x_parts, tm, d)
    args += list(x_parts)
    in_specs += [full(w_r), full(w_h), full(w_m),
                 pl.BlockSpec((1, d), lambda i: (0, 0)),
                 pl.BlockSpec((1, 6, d), lambda i: (mi(i), 0, 0))]
    args += [w_r, w_h, w_m, g1.reshape(1, d), modtab]
    out_specs = [pl.BlockSpec((tm, d), lambda i: (i, 0))]
    out_shape = [jax.ShapeDtypeStruct((n_rows, d), F32)]
    if emit_next:
        in_specs.append(pl.BlockSpec((1, d), lambda i: (0, 0)))
        args.append(g2_next.reshape(1, d))
        out_specs.append(pl.BlockSpec((tm, d), lambda i: (i, 0)))
        out_shape.append(jax.ShapeDtypeStruct((n_rows, d), BF16))
    return pl.pallas_call(
        functools.partial(_wout_kernel, n_lat_tiles=n_lat_tiles, n_mix_parts=len(mix_parts[0]),
                          n_x_parts=len(x_parts), emit_next=emit_next),
        grid=(n_rows // tm,),
        in_specs=in_specs,
        out_specs=out_specs,
        out_shape=out_shape,
        compiler_params=_cparams(("arbitrary",), 52),
        name="wout",
    )(*args)


def _ffn_kernel(*refs, emit_next):
    if emit_next:
        h_ref, w1_ref, w3_ref, w2_ref, x_ref, g_ref, mod_ref, gn_ref, modn_ref, o_ref, hn_ref, acc_ref = refs
    else:
        h_ref, w1_ref, w3_ref, w2_ref, x_ref, g_ref, mod_ref, o_ref, acc_ref = refs
    f = pl.program_id(1)

    @pl.when(f == 0)
    def _():
        acc_ref[...] = jnp.zeros(acc_ref.shape, F32)

    h = h_ref[...]
    u = _silu(_dot(h, w1_ref[...])) * _dot(h, w3_ref[...])
    acc_ref[...] += _dot(_bf(u), w2_ref[...])

    @pl.when(f == pl.num_programs(1) - 1)
    def _():
        m = mod_ref[0]
        g = g_ref[...]
        gn = gn_ref[...] if emit_next else None
        mn = modn_ref[0] if emit_next else None

        def chunk(rows):
            xn = x_ref[rows, :] + m[5:6] * _rms(acc_ref[rows, :], g)
            o_ref[rows, :] = xn
            if emit_next:
                hn_ref[rows, :] = (_rms(xn, gn) * (1.0 + mn[1:2]) + mn[0:1]).astype(hn_ref.dtype)

        _row_chunks(o_ref.shape[0], chunk)


def _ffn(hb, w1, w3, w2, xs, g3, modtab, lat_rows, g_next, modtab_next):
    n_rows, d = hb.shape
    dff = w1.shape[1]
    nb = modtab.shape[0] - 1
    tm, tf = ROW_TILE, FFN_TF
    mi = _mod_index(tm, lat_rows, nb)
    emit_next = g_next is not None
    nf = dff // tf

    def fs(i, f):
        return jnp.where(i % 2 == 1, nf - 1 - f, f)

    in_specs = [pl.BlockSpec((tm, d), lambda i, f: (i, 0)),
                pl.BlockSpec((d, tf), lambda i, f: (0, fs(i, f))),
                pl.BlockSpec((d, tf), lambda i, f: (0, fs(i, f))),
                pl.BlockSpec((tf, d), lambda i, f: (fs(i, f), 0)),
                pl.BlockSpec((tm, d), lambda i, f: (i, 0)),
                pl.BlockSpec((1, d), lambda i, f: (0, 0)),
                pl.BlockSpec((1, 6, d), lambda i, f: (mi(i), 0, 0))]
    args = [hb, w1, w3, w2, xs, g3.reshape(1, d), modtab]
    out_specs = [pl.BlockSpec((tm, d), lambda i, f: (i, 0))]
    out_shape = [jax.ShapeDtypeStruct((n_rows, d), F32)]
    if emit_next:
        in_specs += [pl.BlockSpec((1, d), lambda i, f: (0, 0)),
                     pl.BlockSpec((1, 6, d), lambda i, f: (mi(i), 0, 0))]
        args += [g_next.reshape(1, d), modtab_next]
        out_specs.append(pl.BlockSpec((tm, d), lambda i, f: (i, 0)))
        out_shape.append(jax.ShapeDtypeStruct((n_rows, d), BF16))
    return pl.pallas_call(
        functools.partial(_ffn_kernel, emit_next=emit_next),
        grid=(n_rows // tm, dff // tf),
        in_specs=in_specs,
        out_specs=out_specs,
        out_shape=out_shape,
        scratch_shapes=[pltpu.VMEM((tm, d), F32)],
        compiler_params=_cparams(("arbitrary", "arbitrary"), 48),
        name="ffn",
    )(*args)


_META_E0, _META_E1, _META_R0, _META_R1, _META_G0, _META_G1 = range(6)


def _router_kernel(x_ref, g_ref, mod_ref, wr_ref, tri_ref, meta_ref, cnt_ref, zero_ref, carry_ref):
    i = pl.program_id(0)
    zero_ref[...] = jnp.zeros(zero_ref.shape, F32)

    @pl.when(i == 0)
    def _():
        carry_ref[...] = jnp.zeros(carry_ref.shape, F32)

    m = mod_ref[0]
    hmod = _rms(x_ref[...], g_ref[...]) * (1.0 + m[4:5]) + m[3:4]
    h_hi = _bf(hmod)
    h_lo = _bf(hmod - h_hi.astype(F32))
    w = wr_ref[...]
    w_hi = _bf(w)
    w_lo = _bf(w - w_hi.astype(F32))
    logits = _dot(h_hi, w_hi) + _dot(h_lo, w_hi) + _dot(h_hi, w_lo)
    lane = lax.broadcasted_iota(jnp.int32, logits.shape, 1)
    lanef = lane.astype(F32)
    logits = jnp.where(lane < N_EXPERTS, logits, -jnp.inf)
    v0 = jnp.max(logits, axis=1, keepdims=True)
    e0 = jnp.min(jnp.where(logits == v0, lanef, 1e9), axis=1, keepdims=True)
    rest = jnp.where(lanef == e0, -jnp.inf, logits)
    v1 = jnp.max(rest, axis=1, keepdims=True)
    e1 = jnp.min(jnp.where(rest == v1, lanef, 1e9), axis=1, keepdims=True)
    ex = jnp.exp(v1 - v0)
    g0 = 1.0 / (1.0 + ex)
    g1 = ex / (1.0 + ex)
    oh0 = lanef == e0
    oh1 = lanef == e1
    oh = jnp.where(jnp.logical_or(oh0, oh1), 1.0, 0.0)
    before = _dot(tri_ref[...], _bf(oh)) + carry_ref[0:1, :]
    r0 = jnp.sum(jnp.where(oh0, before, 0.0), axis=1, keepdims=True)
    r1 = jnp.sum(jnp.where(oh1, before, 0.0), axis=1, keepdims=True)
    carry_ref[0:1, :] = carry_ref[0:1, :] + jnp.sum(oh, axis=0, keepdims=True)
    meta = jnp.zeros(logits.shape, F32)
    for j, val in ((_META_E0, e0), (_META_E1, e1), (_META_R0, r0), (_META_R1, r1),
                   (_META_G0, g0), (_META_G1, g1)):
        meta = jnp.where(lane == j, val, meta)
    meta_ref[...] = meta
    cnt_ref[...] = carry_ref[...]


def _router(xs, n_rows, g2, modtab, w_router, lat_rows, n_slots):
    d = xs.shape[1]
    nb = modtab.shape[0] - 1
    tm = ROW_TILE
    mi = _mod_index(tm, lat_rows, nb)
    n_steps = n_rows // tm
    zrows = n_slots // n_steps
    assert zrows * n_steps == n_slots and zrows % 8 == 0, (n_slots, n_steps)
    wr = jnp.zeros((d, 128), F32).at[:, :N_EXPERTS].set(w_router)
    tri = jnp.asarray(np.tril(np.ones((tm, tm), np.float32), -1), BF16)
    return pl.pallas_call(
        _router_kernel,
        grid=(n_steps,),
        in_specs=[pl.BlockSpec((tm, d), lambda i: (i, 0)),
                  pl.BlockSpec((1, d), lambda i: (0, 0)),
                  pl.BlockSpec((1, 6, d), lambda i: (mi(i), 0, 0)),
                  pl.BlockSpec((d, 128), lambda i: (0, 0)),
                  pl.BlockSpec((tm, tm), lambda i: (0, 0))],
        out_specs=[pl.BlockSpec((tm, 128), lambda i: (i, 0)),
                   pl.BlockSpec((8, 128), lambda i: (0, 0)),
                   pl.BlockSpec((zrows, d), lambda i: (i, 0))],
        out_shape=[jax.ShapeDtypeStruct((n_rows, 128), F32),
                   jax.ShapeDtypeStruct((8, 128), F32),
                   jax.ShapeDtypeStruct((n_slots, d), F32)],
        scratch_shapes=[pltpu.VMEM((8, 128), F32)],
        compiler_params=_cparams(("arbitrary",), 48),
        name="router",
    )(xs, g2.reshape(1, d), modtab, wr, tri)


def _dispatch_kernel(dest_ref, x_ref, g_ref, mod_ref, init_ref, o_ref, h_s, sem):
    del init_ref
    i = pl.program_id(0)
    tm = h_s.shape[0]
    m = mod_ref[0]
    h_s[...] = _rms(x_ref[...], g_ref[...]) * (1.0 + m[4:5]) + m[3:4]

    def row_copy(r, k):
        dst = dest_ref[TOP_K * (i * tm + r) + k]
        return pltpu.make_async_copy(h_s.at[pl.ds(r, 1)], o_ref.at[pl.ds(dst, 1)], sem)

    def start(r, carry):
        for k in range(TOP_K):
            row_copy(r, k).start(priority=k)
        return carry

    lax.fori_loop(0, tm, start, 0, unroll=DMA_UNROLL)

    def wait(r, carry):
        for k in range(TOP_K):
            row_copy(r, k).wait()
        return carry

    lax.fori_loop(0, tm, wait, 0, unroll=DMA_UNROLL)


def _dispatch(dest, xs, n_rows, g2, modtab, init, lat_rows):
    d = xs.shape[1]
    n_slots = init.shape[0]
    nb = modtab.shape[0] - 1
    tm = ROW_TILE
    mi = _mod_index(tm, lat_rows, nb)
    grid_spec = pltpu.PrefetchScalarGridSpec(
        num_scalar_prefetch=1,
        grid=(n_rows // tm,),
        in_specs=[pl.BlockSpec((tm, d), lambda i, dst: (i, 0)),
                  pl.BlockSpec((1, d), lambda i, dst: (0, 0)),
                  pl.BlockSpec((1, 6, d), lambda i, dst: (mi(i), 0, 0)),
                  pl.BlockSpec(memory_space=pl.ANY)],
        out_specs=pl.BlockSpec(memory_space=pl.ANY),
        scratch_shapes=[pltpu.VMEM((tm, d), F32), pltpu.SemaphoreType.DMA(())],
    )
    return pl.pallas_call(
        _dispatch_kernel,
        grid_spec=grid_spec,
        out_shape=jax.ShapeDtypeStruct((n_slots, d), F32),
        input_output_aliases={4: 0},
        compiler_params=_cparams(("arbitrary",), 32),
        name="dispatch",
    )(dest, xs, g2.reshape(1, d), modtab, init)


def _expert_kernel(ge_ref, nt_ref, ng_ref, x_ref, *refs):
    w1_refs, w3_refs = refs[:MOE_WSPLIT], refs[MOE_WSPLIT:2 * MOE_WSPLIT]
    w2_ref, o_ref, xb_s = refs[2 * MOE_WSPLIT:]
    s = pl.program_id(0)
    f = pl.program_id(1)
    tm = MOE_TM

    @pl.when(s < ng_ref[0])
    def _():
        @pl.when(f == 0)
        def _():
            xb_s[...] = _bf(x_ref[...])
            o_ref[...] = jnp.zeros(o_ref.shape, F32)

        w1b = jnp.concatenate([_bf(r[0]) for r in w1_refs], axis=0)
        w3b = jnp.concatenate([_bf(r[0]) for r in w3_refs], axis=0)
        w2b = _bf(w2_ref[0])

        def tile(t):
            rows = pl.ds(t * tm, tm)
            h = xb_s[rows, :]
            u = _silu(_dot(h, w1b)) * _dot(h, w3b)
            o_ref[rows, :] += _dot(_bf(u), w2b)

        tile(0)
        for t in range(1, MOE_GROUP):
            pl.when(nt_ref[s] > t)(functools.partial(tile, t))

    @pl.when(jnp.logical_and(s >= ng_ref[0], f == pl.num_programs(1) - 1))
    def _():
        o_ref[...] = jnp.zeros(o_ref.shape, F32)


def _experts(group_e, group_tiles, n_groups, xsort, w1, w3, w2):
    n_slots, d = xsort.shape
    dff = w1.shape[2]
    tg, tf = MOE_TM * MOE_GROUP, MOE_TF
    nf = dff // tf

    def ss(s, ng):
        return jnp.maximum(jnp.minimum(s, ng[0] - 1), 0)

    def ff(s, f, ng):
        snake = jnp.where(s % 2 == 1, nf - 1 - f, f)
        last = jnp.where((ng[0] - 1) % 2 == 1, 0, nf - 1)
        return jnp.where(s < ng[0], snake, last)

    grid_spec = pltpu.PrefetchScalarGridSpec(
        num_scalar_prefetch=3,
        grid=(n_slots // tg, nf),
        in_specs=[pl.BlockSpec((tg, d), lambda s, f, ge, nt, ng: (ss(s, ng), 0)),
                  *[pl.BlockSpec((1, d // MOE_WSPLIT, tf),
                                 lambda s, f, ge, nt, ng, part=part: (ge[ss(s, ng)], part, ff(s, f, ng)))
                    for _ in range(2) for part in range(MOE_WSPLIT)],
                  pl.BlockSpec((1, tf, d), lambda s, f, ge, nt, ng: (ge[ss(s, ng)], ff(s, f, ng), 0))],
        out_specs=pl.BlockSpec((tg, d), lambda s, f, ge, nt, ng: (s, 0)),
        scratch_shapes=[pltpu.VMEM((tg, d), BF16)],
    )
    return pl.pallas_call(
        _expert_kernel,
        grid_spec=grid_spec,
        out_shape=jax.ShapeDtypeStruct((n_slots, d), F32),
        compiler_params=_cparams(("arbitrary", "arbitrary"), 56),
        name="experts",
    )(group_e, group_tiles, n_groups, xsort, *([w1] * MOE_WSPLIT), *([w3] * MOE_WSPLIT), w2)


def _combine_kernel(dest_ref, y_ref, meta_ref, x_ref, g_ref, mod_ref, o_ref, buf_s, sem):
    i = pl.program_id(0)
    tm = x_ref.shape[0]

    def row_copy(r, k):
        src = dest_ref[TOP_K * (i * tm + r) + k]
        return pltpu.make_async_copy(y_ref.at[pl.ds(src, 1)], buf_s.at[k, pl.ds(r, 1)], sem)

    def start(r, carry):
        for k in range(TOP_K):
            row_copy(r, k).start(priority=k)
        return carry

    lax.fori_loop(0, tm, start, 0, unroll=DMA_UNROLL)

    def wait(r, carry):
        for k in range(TOP_K):
            row_copy(r, k).wait()
        return carry

    lax.fori_loop(0, tm, wait, 0, unroll=DMA_UNROLL)

    m = mod_ref[0]
    g = g_ref[...]
    def chunk(rows):
        meta = meta_ref[rows, :]
        lane = lax.broadcasted_iota(jnp.int32, meta.shape, 1)
        g0 = jnp.sum(jnp.where(lane == _META_G0, meta, 0.0), axis=1, keepdims=True)
        g1 = jnp.sum(jnp.where(lane == _META_G1, meta, 0.0), axis=1, keepdims=True)
        y = buf_s[0, rows, :] * g0 + buf_s[1, rows, :] * g1
        o_ref[rows, :] = x_ref[rows, :] + m[5:6] * _rms(y, g)

    _row_chunks(tm, chunk)


def _combine(dest, yb, meta, xs, n_rows, g3, modtab, lat_rows):
    d = xs.shape[1]
    nb = modtab.shape[0] - 1
    tm = ROW_TILE
    mi = _mod_index(tm, lat_rows, nb)
    grid_spec = pltpu.PrefetchScalarGridSpec(
        num_scalar_prefetch=1,
        grid=(n_rows // tm,),
        in_specs=[pl.BlockSpec(memory_space=pl.ANY),
                  pl.BlockSpec((tm, 128), lambda i, dst: (i, 0)),
                  pl.BlockSpec((tm, d), lambda i, dst: (i, 0)),
                  pl.BlockSpec((1, d), lambda i, dst: (0, 0)),
                  pl.BlockSpec((1, 6, d), lambda i, dst: (mi(i), 0, 0))],
        out_specs=pl.BlockSpec((tm, d), lambda i, dst: (i, 0)),
        scratch_shapes=[pltpu.VMEM((TOP_K, tm, d), F32), pltpu.SemaphoreType.DMA(())],
    )
    return pl.pallas_call(
        _combine_kernel,
        grid_spec=grid_spec,
        out_shape=jax.ShapeDtypeStruct((n_rows, d), F32),
        compiler_params=_cparams(("arbitrary",), 40),
        name="combine",
    )(dest, yb, meta, xs, g3.reshape(1, d), modtab)


def _moe(xs, n_rows, g2, g3, modtab, w_router, w1, w3, w2, lat_rows):
    tg = MOE_TM * MOE_GROUP
    n_groups_max = (n_rows * TOP_K) // tg + N_EXPERTS
    n_slots = n_groups_max * tg
    meta, cnt, slots_init = _router(xs, n_rows, g2, modtab, w_router, lat_rows, n_slots)
    counts = cnt[0, :N_EXPERTS].astype(jnp.int32)
    padded = (counts + tg - 1) // tg * tg
    pad_end = jnp.cumsum(padded)
    pad_start = pad_end - padded
    e = meta[:, _META_E0:_META_E1 + 1].astype(jnp.int32)
    r = meta[:, _META_R0:_META_R1 + 1].astype(jnp.int32)
    dest = (pad_start[e] + r).reshape(-1)
    group_start = jnp.arange(n_groups_max, dtype=jnp.int32) * tg
    group_e = jnp.minimum(jnp.sum((group_start[:, None] >= pad_end[None, :]).astype(jnp.int32), axis=1),
                          N_EXPERTS - 1)
    filled = counts[group_e] - (group_start - pad_start[group_e])
    group_tiles = jnp.clip((filled + MOE_TM - 1) // MOE_TM, 1, MOE_GROUP).astype(jnp.int32)
    n_groups = (pad_end[-1:] // tg).astype(jnp.int32)
    xsort = _dispatch(dest, xs, n_rows, g2, modtab, slots_init, lat_rows)
    yb = _experts(group_e, group_tiles, n_groups, xsort, w1, w3, w2)
    return _combine(dest, yb, meta, xs, n_rows, g3, modtab, lat_rows)


def _rotary_tables(lat_len, ctx_len):
    rows = lat_len // GRID_W
    row = jnp.repeat(jnp.arange(rows, dtype=F32), GRID_W)
    col = jnp.tile(jnp.arange(GRID_W, dtype=F32), rows)
    n_freq = RET_DK // 4
    inv = ROPE_BASE ** (-jnp.arange(n_freq, dtype=F32) / n_freq)
    ang = jnp.concatenate([row[:, None] * inv, col[:, None] * inv], axis=-1)
    cos, sin = jnp.cos(ang), jnp.sin(ang)
    cosf = jnp.concatenate([jnp.ones((ctx_len, RET_DK), F32), jnp.concatenate([cos, cos], -1)], 0)
    sinf = jnp.concatenate([jnp.zeros((ctx_len, RET_DK), F32), jnp.concatenate([-sin, sin], -1)], 0)
    return cosf, sinf


def kernel(x, c, ctx, c_ctx, w_ada, b_ada, norm_g, w_in, w_out, ret_decay, ret_gn, hgrn_lb, hgrn_gn,
           mlstm_conv, mlstm_gate_b, mlstm_gn, w_ffn1, w_ffn3, w_ffn2, w_router, w_exp1, w_exp3, w_exp2):
    nb, lat_len, d = x.shape
    ctx_len = ctx.shape[1]
    depth = w_ada.shape[0]
    n_lat_rows = nb * lat_len
    n_rows = n_lat_rows + nb * ctx_len

    x_parts = [x.reshape(n_lat_rows, d), ctx.reshape(nb * ctx_len, d)]
    w_in_t = jnp.swapaxes(w_in, 1, 2)
    s_in = jnp.zeros((8, d), F32).at[:nb].set(c).at[nb].set(c_ctx)
    mod_all = _ada(s_in, w_ada, b_ada)[:, :nb + 1].reshape(depth, nb + 1, 6, d)

    cosf, sinf = _rotary_tables(lat_len, ctx_len)
    hg_cum, hg_masks, hg_signs = _hgrn_constants()
    hg_consts = (jnp.asarray(hg_cum, BF16), jnp.asarray(hg_masks, F32), jnp.asarray(hg_signs, F32))
    sm = jax.nn.softmax(hgrn_lb.astype(F32), axis=0)
    lb_all = jnp.clip(jnp.cumsum(sm, axis=0) - sm[0], 0.0, 1.0)

    hb = None
    for layer in range(depth):
        last = layer == depth - 1
        modtab = mod_all[layer]
        g = norm_g[layer]

        if hb is None:
            hb = _prenorm(x_parts, n_rows, g[0], modtab, 1, 0, lat_len, BF16)
        p = _mm(hb, w_in_t, layer, MM_TM, MM_TN, n_cols=PROJ_MAIN)
        w_g = jnp.zeros((1, 128, d), F32).at[0, :MLSTM_GATES].set(w_in_t[layer, PROJ_MAIN:])
        gates = _mm(hb, w_g, 0, MM_TM, 128)[:, :MLSTM_GATES]
        hb = None
        g_row_lat = jnp.swapaxes(gates[:n_lat_rows].reshape(nb, lat_len, MLSTM_GATES), 1, 2)
        g_row_ctx = jnp.swapaxes(gates[n_lat_rows:].reshape(nb, ctx_len, MLSTM_GATES), 1, 2)

        log_g = jax.nn.log_sigmoid(ret_decay[layer].astype(F32))
        lg_b = jnp.broadcast_to(log_g[:, :, None, None], (2, RET_HEADS, 8, RET_DV))
        lb = lb_all[layer]
        emit_ctx = not last
        o_ret = _retention(p, nb, lat_len, ctx_len, cosf, sinf, lg_b, ret_gn[layer], emit_ctx)
        o_hg = _hgrn(p, nb, lat_len, ctx_len, jnp.log(lb), jnp.log1p(-lb), hgrn_gn[layer],
                     hg_consts, emit_ctx)
        o_ml = _mlstm(p, gates, g_row_lat, g_row_ctx, nb, lat_len, ctx_len, mlstm_gate_b[layer],
                      mlstm_conv[layer], mlstm_gn[layer], emit_ctx)
        mix_parts = [list(o) for o in (o_ret, o_hg, o_ml)]
        rows_now = n_lat_rows if last else n_rows
        dense = layer % 2 == 0
        res = _wout(mix_parts, _bf(w_out[layer]), x_parts, rows_now, g[1], modtab, lat_len,
                    g[2] if dense else None)
        xs = res[0]
        x_parts = [xs]

        j = layer // 2
        if dense:
            nxt = (None, None) if last else (norm_g[layer + 1][0], mod_all[layer + 1])
            res = _ffn(res[1], _bf(w_ffn1[j]), _bf(w_ffn3[j]), _bf(w_ffn2[j]), xs, g[3], modtab, lat_len, *nxt)
            xs = res[0]
            hb = None if last else res[1]
        else:
            xs = _moe(xs, rows_now, g[2], g[3], modtab, w_router[j], w_exp1[j], w_exp3[j], w_exp2[j], lat_len)
        x_parts = [xs]
    return xs[:n_lat_rows].reshape(nb, lat_len, d)
```
